```python
import math
import jax, jax.numpy as jnp
from jax import lax
import numpy as np

D_MODEL = 2048
BATCH = 8
SEQ = 2048
DEPTH = 2

N_MEM = 256
N_MIXERS = 2
EXPAND = 2
INNER = EXPAND * D_MODEL
XA_HEADS = 4
XA_WIDTH = INNER // 4
XA_DIM = XA_WIDTH // XA_HEADS
MIX_WIDTH = INNER - XA_WIDTH
DN_HEAD_DIM = 128
DN_V_HEADS = MIX_WIDTH // DN_HEAD_DIM
DN_QK_HEADS = DN_V_HEADS // 2
DN_QK_WIDTH = DN_QK_HEADS * DN_HEAD_DIM
DN_CONV = 4
DN_CHUNK = 64
DN_MIX_COLS = 2 * DN_QK_WIDTH + MIX_WIDTH + 2 * DN_V_HEADS
DN_PROJ = DN_MIX_COLS + XA_WIDTH + INNER
SB_HEAD_DIM = 128
SB_HEADS = MIX_WIDTH // SB_HEAD_DIM
SB_BLOCK = 128
SB_MIX_COLS = 3 * MIX_WIDTH
SB_PROJ = SB_MIX_COLS + XA_WIDTH + INNER
EPS = 1e-6

kernel_name = "hybrid_deltanet_stickbreaking_memxattn"


def rms_norm(x, g):
    xf = x.astype(jnp.float32)
    y = xf * lax.rsqrt(jnp.mean(xf * xf, axis=-1, keepdims=True) + EPS)
    return (y * g.astype(jnp.float32)).astype(x.dtype)


def l2_norm(x):
    xf = x.astype(jnp.float32)
    return (xf * lax.rsqrt(jnp.sum(xf * xf, axis=-1, keepdims=True) + EPS)).astype(x.dtype)


def causal_depthwise_conv(x, w):
    c = x.shape[-1]
    return lax.conv_general_dilated(
        x, w[:, None, :].astype(x.dtype), window_strides=(1,),
        padding=((w.shape[0] - 1, 0),), dimension_numbers=("NWC", "WIO", "NWC"),
        feature_group_count=c)


def gated_delta_rule(q, k, v, g, beta):
    f32 = jnp.float32
    B, H, S, dk = q.shape
    dv = v.shape[-1]
    C, N = DN_CHUNK, S // DN_CHUNK
    q, k, v = (t.astype(f32).reshape(B, H, N, C, t.shape[-1]) for t in (q, k, v))
    g = g.astype(f32).reshape(B, H, N, C)
    beta = beta.astype(f32).reshape(B, H, N, C)
    gc = jnp.cumsum(g, axis=-1)
    idx = jnp.arange(C)
    incl = idx[:, None] >= idx[None, :]
    strict = idx[:, None] > idx[None, :]
    decay = jnp.exp(jnp.where(incl, gc[..., :, None] - gc[..., None, :], -jnp.inf))
    kk = jnp.einsum("bhncd,bhnjd->bhncj", k, k)
    m = jnp.where(strict, beta[..., :, None] * kk * decay, 0.0) + jnp.eye(C, dtype=f32)
    rhs = jnp.concatenate([v * beta[..., None], k * (beta * jnp.exp(gc))[..., None]], axis=-1)
    sol = lax.linalg.triangular_solve(m, rhs, left_side=True, lower=True, unit_diagonal=True)
    u0, w = sol[..., :dv], sol[..., dv:]
    qk = jnp.einsum("bhncd,bhnjd->bhncj", q, k) * decay
    q_dec = q * jnp.exp(gc)[..., None]
    k_dec = k * jnp.exp(gc[..., -1:] - gc)[..., None]
    chunk_decay = jnp.exp(gc[..., -1])

    def step(state, inp):
        u0_c, w_c, qk_c, qd_c, kd_c, cd_c = inp
        u = u0_c - jnp.einsum("bhcd,bhde->bhce", w_c, state)
        o = jnp.einsum("bhcd,bhde->bhce", qd_c, state) + jnp.einsum("bhcj,bhje->bhce", qk_c, u)
        state = cd_c[..., None, None] * state + jnp.einsum("bhcd,bhce->bhde", kd_c, u)
        return state, o

    xs = tuple(jnp.moveaxis(t, 2, 0) for t in (u0, w, qk, q_dec, k_dec, chunk_decay))
    _, o = lax.scan(step, jnp.zeros((B, H, dk, dv), f32), xs)
    return jnp.moveaxis(o, 0, 2).reshape(B, H, S, dv)


def deltanet_branch(p, conv_w, a_log, dt_bias, out_g):
    B, S, _ = p.shape
    c_qkv = 2 * DN_QK_WIDTH + MIX_WIDTH
    qkv = jax.nn.silu(causal_depthwise_conv(p[..., :c_qkv], conv_w))
    q = qkv[..., :DN_QK_WIDTH].reshape(B, S, DN_QK_HEADS, DN_HEAD_DIM)
    k = qkv[..., DN_QK_WIDTH:2 * DN_QK_WIDTH].reshape(B, S, DN_QK_HEADS, DN_HEAD_DIM)
    v = qkv[..., 2 * DN_QK_WIDTH:].reshape(B, S, DN_V_HEADS, DN_HEAD_DIM)
    a = p[..., c_qkv:c_qkv + DN_V_HEADS].astype(jnp.float32)
    b = p[..., c_qkv + DN_V_HEADS:].astype(jnp.float32)
    rep = DN_V_HEADS // DN_QK_HEADS
    q = jnp.repeat(l2_norm(q), rep, axis=2) * DN_HEAD_DIM ** -0.5
    k = jnp.repeat(l2_norm(k), rep, axis=2)
    g = -jnp.exp(a_log.astype(jnp.float32)) * jax.nn.softplus(a + dt_bias.astype(jnp.float32))
    beta = jax.nn.sigmoid(b)
    o = gated_delta_rule(q.transpose(0, 2, 1, 3), k.transpose(0, 2, 1, 3), v.transpose(0, 2, 1, 3),
                         g.transpose(0, 2, 1), beta.transpose(0, 2, 1))
    o = rms_norm(o.astype(p.dtype), out_g)
    return o.transpose(0, 2, 1, 3).reshape(B, S, MIX_WIDTH)


def stick_breaking_attention(q, k, v):
    B, H, S, d = q.shape
    scale = d ** -0.5
    outs = []
    for blk in range(S // SB_BLOCK):
        q0 = blk * SB_BLOCK
        kv_len = q0 + SB_BLOCK
        qb = q[:, :, q0:kv_len]
        kb, vb = k[:, :, :kv_len], v[:, :, :kv_len]
        z = jnp.einsum("bhtd,bhsd->bhts", qb, kb).astype(jnp.float32) * scale
        t_pos = q0 + jnp.arange(SB_BLOCK)
        s_pos = jnp.arange(kv_len)
        mask = s_pos[None, :] < t_pos[:, None]
        log_rest = jnp.where(mask, jax.nn.log_sigmoid(-z), 0.0)
        later = lax.cumsum(log_rest, axis=3, reverse=True) - log_rest
        wts = jnp.where(mask, jnp.exp(jax.nn.log_sigmoid(z) + later), 0.0)
        outs.append(jnp.einsum("bhts,bhsd->bhtd", wts.astype(v.dtype), vb))
    return jnp.concatenate(outs, axis=2)


def stick_breaking_branch(p, qn_g, kn_g):
    B, S, _ = p.shape
    q, k, v = (t.reshape(B, S, SB_HEADS, SB_HEAD_DIM) for t in jnp.split(p, 3, axis=-1))
    q, k = rms_norm(q, qn_g), rms_norm(k, kn_g)
    o = stick_breaking_attention(q.transpose(0, 2, 1, 3), k.transpose(0, 2, 1, 3), v.transpose(0, 2, 1, 3))
    return o.transpose(0, 2, 1, 3).reshape(B, S, MIX_WIDTH)


def memory_cross_attention(xq, mem_n, w_kv, qn_g, kn_g):
    B, S, _ = xq.shape
    q = rms_norm(xq.reshape(B, S, XA_HEADS, XA_DIM), qn_g)
    kv = jnp.einsum("bmd,de->bme", mem_n, w_kv)
    k = rms_norm(kv[..., :XA_WIDTH].reshape(B, -1, XA_HEADS, XA_DIM), kn_g)
    v = kv[..., XA_WIDTH:].reshape(B, -1, XA_HEADS, XA_DIM)
    s = jnp.einsum("bthd,bmhd->bhtm", q, k).astype(jnp.float32) * XA_DIM ** -0.5
    p = jax.nn.softmax(s, axis=-1).astype(v.dtype)
    return jnp.einsum("bhtm,bmhd->bthd", p, v).reshape(B, S, XA_WIDTH)


def _fwd_setup_inputs(seed: int = 0) -> dict:
    key = jax.random.key(seed)
    ks = jax.random.split(key, 20)
    f32 = jnp.float32
    n_dn = (DEPTH + N_MIXERS - 1) // N_MIXERS
    n_sb = DEPTH // N_MIXERS

    def dense(k, shape, fan_in):
        return jax.random.normal(k, shape, f32) * fan_in ** -0.5

    def gain(k, shape):
        return 1.0 + 0.02 * jax.random.normal(k, shape, f32)

    dt = jnp.exp(jax.random.uniform(ks[10], (n_dn, DN_V_HEADS), f32,
                                    minval=math.log(1e-3), maxval=math.log(1e-1)))
    return {
        "x": jax.random.normal(ks[0], (BATCH, SEQ, D_MODEL), f32),
        "mem": jax.random.normal(ks[1], (BATCH, N_MEM, D_MODEL), f32),
        "norm_g": gain(ks[2], (DEPTH, D_MODEL)),
        "mem_norm_g": gain(ks[3], (D_MODEL,)),
        "mem_w_kv": dense(ks[4], (DEPTH, D_MODEL, 2 * XA_WIDTH), D_MODEL),
        "xa_q_norm_g": gain(ks[5], (DEPTH, XA_DIM)),
        "xa_k_norm_g": gain(ks[6], (DEPTH, XA_DIM)),
        "w_out": dense(ks[7], (DEPTH, INNER, D_MODEL), INNER),
        "dn_w_in": dense(ks[8], (n_dn, D_MODEL, DN_PROJ), D_MODEL),
        "dn_conv_w": dense(ks[9], (n_dn, DN_CONV, 2 * DN_QK_WIDTH + MIX_WIDTH), DN_CONV),
        "dn_a_log": jnp.log(jax.random.uniform(ks[11], (n_dn, DN_V_HEADS), f32, minval=1.0, maxval=16.0)),
        "dn_dt_bias": dt + jnp.log(-jnp.expm1(-dt)),
        "dn_out_norm_g": gain(ks[12], (n_dn, DN_HEAD_DIM)),
        "sb_w_in": dense(ks[13], (n_sb, D_MODEL, SB_PROJ), D_MODEL),
        "sb_q_norm_g": gain(ks[14], (n_sb, SB_HEAD_DIM)),
        "sb_k_norm_g": gain(ks[15], (n_sb, SB_HEAD_DIM)),
    }


def _fwd_reference(x, mem, norm_g, mem_norm_g, mem_w_kv, xa_q_norm_g, xa_k_norm_g, w_out,
              dn_w_in, dn_conv_w, dn_a_log, dn_dt_bias, dn_out_norm_g,
              sb_w_in, sb_q_norm_g, sb_k_norm_g):
    mem_n = rms_norm(mem, mem_norm_g)
    for i in range(DEPTH):
        h = rms_norm(x, norm_g[i])
        j = i // N_MIXERS
        if i % N_MIXERS == 0:
            proj = jnp.einsum("bsd,de->bse", h, dn_w_in[j])
            mix = deltanet_branch(proj[..., :DN_MIX_COLS], dn_conv_w[j], dn_a_log[j],
                                  dn_dt_bias[j], dn_out_norm_g[j])
        else:
            proj = jnp.einsum("bsd,de->bse", h, sb_w_in[j])
            mix = stick_breaking_branch(proj[..., :SB_MIX_COLS], sb_q_norm_g[j], sb_k_norm_g[j])
        xq = proj[..., -(XA_WIDTH + INNER):-INNER]
        z = proj[..., -INNER:]
        xa = memory_cross_attention(xq, mem_n, mem_w_kv[i], xa_q_norm_g[i], xa_k_norm_g[i])
        y = jnp.concatenate([mix, xa], axis=-1) * jax.nn.silu(z)
        x = x + jnp.einsum("bse,ed->bsd", y, w_out[i])
    return x


import jax as _jax
import jax.numpy as _jnp

TWIN_FORMAT = 'train_step'
FWD_PARAMS = ['x', 'mem', 'norm_g', 'mem_norm_g', 'mem_w_kv', 'xa_q_norm_g', 'xa_k_norm_g', 'w_out', 'dn_w_in', 'dn_conv_w', 'dn_a_log', 'dn_dt_bias', 'dn_out_norm_g', 'sb_w_in', 'sb_q_norm_g', 'sb_k_norm_g']
TWIN_WEIGHTS = ['norm_g', 'mem_norm_g', 'mem_w_kv', 'xa_q_norm_g', 'xa_k_norm_g', 'w_out', 'dn_w_in', 'dn_conv_w', 'dn_a_log', 'dn_dt_bias', 'dn_out_norm_g', 'sb_w_in', 'sb_q_norm_g', 'sb_k_norm_g']
TWIN_DIFF_INPUT = 'x'
TWIN_INPUTS = ['x', 'mem', 'norm_g', 'mem_norm_g', 'mem_w_kv', 'xa_q_norm_g', 'xa_k_norm_g', 'w_out', 'dn_w_in', 'dn_conv_w', 'dn_a_log', 'dn_dt_bias', 'dn_out_norm_g', 'sb_w_in', 'sb_q_norm_g', 'sb_k_norm_g', 'loss_target', 'm_norm_g', 'm_mem_norm_g', 'm_mem_w_kv', 'm_xa_q_norm_g', 'm_xa_k_norm_g', 'm_w_out', 'm_dn_w_in', 'm_dn_conv_w', 'm_dn_a_log', 'm_dn_dt_bias', 'm_dn_out_norm_g', 'm_sb_w_in', 'm_sb_q_norm_g', 'm_sb_k_norm_g', 'v_norm_g', 'v_mem_norm_g', 'v_mem_w_kv', 'v_xa_q_norm_g', 'v_xa_k_norm_g', 'v_w_out', 'v_dn_w_in', 'v_dn_conv_w', 'v_dn_a_log', 'v_dn_dt_bias', 'v_dn_out_norm_g', 'v_sb_w_in', 'v_sb_q_norm_g', 'v_sb_k_norm_g']
TWIN_OUTPUTS = ['loss', 'grad_x', 'grad_norm_g', 'grad_mem_norm_g', 'grad_mem_w_kv', 'grad_xa_q_norm_g', 'grad_xa_k_norm_g', 'grad_w_out', 'grad_dn_w_in', 'grad_dn_conv_w', 'grad_dn_a_log', 'grad_dn_dt_bias', 'grad_dn_out_norm_g', 'grad_sb_w_in', 'grad_sb_q_norm_g', 'grad_sb_k_norm_g', 'delta_norm_g', 'delta_mem_norm_g', 'delta_mem_w_kv', 'delta_xa_q_norm_g', 'delta_xa_k_norm_g', 'delta_w_out', 'delta_dn_w_in', 'delta_dn_conv_w', 'delta_dn_a_log', 'delta_dn_dt_bias', 'delta_dn_out_norm_g', 'delta_sb_w_in', 'delta_sb_q_norm_g', 'delta_sb_k_norm_g', 'new_m_norm_g', 'new_m_mem_norm_g', 'new_m_mem_w_kv', 'new_m_xa_q_norm_g', 'new_m_xa_k_norm_g', 'new_m_w_out', 'new_m_dn_w_in', 'new_m_dn_conv_w', 'new_m_dn_a_log', 'new_m_dn_dt_bias', 'new_m_dn_out_norm_g', 'new_m_sb_w_in', 'new_m_sb_q_norm_g', 'new_m_sb_k_norm_g', 'new_v_norm_g', 'new_v_mem_norm_g', 'new_v_mem_w_kv', 'new_v_xa_q_norm_g', 'new_v_xa_k_norm_g', 'new_v_w_out', 'new_v_dn_w_in', 'new_v_dn_conv_w', 'new_v_dn_a_log', 'new_v_dn_dt_bias', 'new_v_dn_out_norm_g', 'new_v_sb_w_in', 'new_v_sb_q_norm_g', 'new_v_sb_k_norm_g']
TWIN_LEAF_KINDS = {'loss': 'loss', 'grad_x': 'grad_x', 'grad_norm_g': 'grad_w', 'grad_mem_norm_g': 'grad_w', 'grad_mem_w_kv': 'grad_w', 'grad_xa_q_norm_g': 'grad_w', 'grad_xa_k_norm_g': 'grad_w', 'grad_w_out': 'grad_w', 'grad_dn_w_in': 'grad_w', 'grad_dn_conv_w': 'grad_w', 'grad_dn_a_log': 'grad_w', 'grad_dn_dt_bias': 'grad_w', 'grad_dn_out_norm_g': 'grad_w', 'grad_sb_w_in': 'grad_w', 'grad_sb_q_norm_g': 'grad_w', 'grad_sb_k_norm_g': 'grad_w', 'delta_norm_g': 'delta_w', 'delta_mem_norm_g': 'delta_w', 'delta_mem_w_kv': 'delta_w', 'delta_xa_q_norm_g': 'delta_w', 'delta_xa_k_norm_g': 'delta_w', 'delta_w_out': 'delta_w', 'delta_dn_w_in': 'delta_w', 'delta_dn_conv_w': 'delta_w', 'delta_dn_a_log': 'delta_w', 'delta_dn_dt_bias': 'delta_w', 'delta_dn_out_norm_g': 'delta_w', 'delta_sb_w_in': 'delta_w', 'delta_sb_q_norm_g': 'delta_w', 'delta_sb_k_norm_g': 'delta_w', 'new_m_norm_g': 'new_m', 'new_m_mem_norm_g': 'new_m', 'new_m_mem_w_kv': 'new_m', 'new_m_xa_q_norm_g': 'new_m', 'new_m_xa_k_norm_g': 'new_m', 'new_m_w_out': 'new_m', 'new_m_dn_w_in': 'new_m', 'new_m_dn_conv_w': 'new_m', 'new_m_dn_a_log': 'new_m', 'new_m_dn_dt_bias': 'new_m', 'new_m_dn_out_norm_g': 'new_m', 'new_m_sb_w_in': 'new_m', 'new_m_sb_q_norm_g': 'new_m', 'new_m_sb_k_norm_g': 'new_m', 'new_v_norm_g': 'new_v', 'new_v_mem_norm_g': 'new_v', 'new_v_mem_w_kv': 'new_v', 'new_v_xa_q_norm_g': 'new_v', 'new_v_xa_k_norm_g': 'new_v', 'new_v_w_out': 'new_v', 'new_v_dn_w_in': 'new_v', 'new_v_dn_conv_w': 'new_v', 'new_v_dn_a_log': 'new_v', 'new_v_dn_dt_bias': 'new_v', 'new_v_dn_out_norm_g': 'new_v', 'new_v_sb_w_in': 'new_v', 'new_v_sb_q_norm_g': 'new_v', 'new_v_sb_k_norm_g': 'new_v'}


def _forward(args):
    return _fwd_reference(*[args[k] for k in FWD_PARAMS])


def _output_shape():
    out = _jax.eval_shape(lambda: _forward(_fwd_setup_inputs(0)))
    return out.shape, out.dtype

N_MICROBATCH = 1
ADAM_LR = 0.001
ADAM_B1 = 0.9
ADAM_B2 = 0.999
ADAM_EPS = 1e-08
ADAM_WD = 0.01
ADAM_STEP = 10
PER_EXAMPLE_BATCH_AXIS = {'x': 0, 'mem': 0, 'loss_target': 0}
SHARED_INPUTS = []
_WEIGHT_DTYPES = {'norm_g': _jnp.float32, 'mem_norm_g': _jnp.float32, 'mem_w_kv': _jnp.float32, 'xa_q_norm_g': _jnp.float32, 'xa_k_norm_g': _jnp.float32, 'w_out': _jnp.float32, 'dn_w_in': _jnp.float32, 'dn_conv_w': _jnp.float32, 'dn_a_log': _jnp.float32, 'dn_dt_bias': _jnp.float32, 'dn_out_norm_g': _jnp.float32, 'sb_w_in': _jnp.float32, 'sb_q_norm_g': _jnp.float32, 'sb_k_norm_g': _jnp.float32}
MOMENT_SCALE = {'norm_g': 2.278626e+00, 'mem_norm_g': 1.443826e-02, 'mem_w_kv': 5.451849e-03, 'xa_q_norm_g': 4.997048e-02, 'xa_k_norm_g': 5.010260e-02, 'w_out': 9.324615e-02, 'dn_w_in': 6.036541e-02, 'dn_conv_w': 8.851325e-02, 'dn_a_log': 4.637579e+00, 'dn_dt_bias': 4.385409e+00, 'dn_out_norm_g': 3.338114e+01, 'sb_w_in': 3.044979e-02, 'sb_q_norm_g': 2.090995e+00, 'sb_k_norm_g': 2.095941e+00}


def _to_microbatches(a, axis):
    t = _jnp.moveaxis(a, axis, 0)
    t = t.reshape((N_MICROBATCH, t.shape[0] // N_MICROBATCH) + t.shape[1:])
    return _jnp.moveaxis(t, 1, axis + 1)


def setup_inputs(seed: int = 0) -> dict:
    inp = _fwd_setup_inputs(seed)
    key = _jax.random.fold_in(_jax.random.key(seed), 7919)
    shape, _ = _output_shape()
    out = dict(inp)
    out["loss_target"] = _jax.random.normal(_jax.random.fold_in(key, 0), shape, _jnp.float32)
    for i, name in enumerate(TWIN_WEIGHTS):
        w = inp[name].astype(_jnp.float32)
        if MOMENT_SCALE is None:
            s = _jnp.sqrt(_jnp.mean(_jnp.square(w)) + 1e-30)
        else:
            s = MOMENT_SCALE[name]
        km, kv = _jax.random.split(_jax.random.fold_in(key, i + 1))
        out[name] = w
        out["m_" + name] = s * _jax.random.normal(km, w.shape, _jnp.float32)
        out["v_" + name] = (s * s) * _jax.random.uniform(kv, w.shape, _jnp.float32, 0.5, 1.5)
    if N_MICROBATCH > 1:
        for name, axis in PER_EXAMPLE_BATCH_AXIS.items():
            out[name] = _to_microbatches(out[name], axis)
    return {'x': out['x'], 'mem': out['mem'], 'norm_g': out['norm_g'], 'mem_norm_g': out['mem_norm_g'], 'mem_w_kv': out['mem_w_kv'], 'xa_q_norm_g': out['xa_q_norm_g'], 'xa_k_norm_g': out['xa_k_norm_g'], 'w_out': out['w_out'], 'dn_w_in': out['dn_w_in'], 'dn_conv_w': out['dn_conv_w'], 'dn_a_log': out['dn_a_log'], 'dn_dt_bias': out['dn_dt_bias'], 'dn_out_norm_g': out['dn_out_norm_g'], 'sb_w_in': out['sb_w_in'], 'sb_q_norm_g': out['sb_q_norm_g'], 'sb_k_norm_g': out['sb_k_norm_g'], 'loss_target': out['loss_target'], 'm_norm_g': out['m_norm_g'], 'm_mem_norm_g': out['m_mem_norm_g'], 'm_mem_w_kv': out['m_mem_w_kv'], 'm_xa_q_norm_g': out['m_xa_q_norm_g'], 'm_xa_k_norm_g': out['m_xa_k_norm_g'], 'm_w_out': out['m_w_out'], 'm_dn_w_in': out['m_dn_w_in'], 'm_dn_conv_w': out['m_dn_conv_w'], 'm_dn_a_log': out['m_dn_a_log'], 'm_dn_dt_bias': out['m_dn_dt_bias'], 'm_dn_out_norm_g': out['m_dn_out_norm_g'], 'm_sb_w_in': out['m_sb_w_in'], 'm_sb_q_norm_g': out['m_sb_q_norm_g'], 'm_sb_k_norm_g': out['m_sb_k_norm_g'], 'v_norm_g': out['v_norm_g'], 'v_mem_norm_g': out['v_mem_norm_g'], 'v_mem_w_kv': out['v_mem_w_kv'], 'v_xa_q_norm_g': out['v_xa_q_norm_g'], 'v_xa_k_norm_g': out['v_xa_k_norm_g'], 'v_w_out': out['v_w_out'], 'v_dn_w_in': out['v_dn_w_in'], 'v_dn_conv_w': out['v_dn_conv_w'], 'v_dn_a_log': out['v_dn_a_log'], 'v_dn_dt_bias': out['v_dn_dt_bias'], 'v_dn_out_norm_g': out['v_dn_out_norm_g'], 'v_sb_w_in': out['v_sb_w_in'], 'v_sb_q_norm_g': out['v_sb_q_norm_g'], 'v_sb_k_norm_g': out['v_sb_k_norm_g']}


def _loss(weights, diff, rest, loss_target):
    with _jax.named_scope("forward"):
        args = {**rest, TWIN_DIFF_INPUT: diff, **{k: w.astype(_WEIGHT_DTYPES[k]) for k, w in weights.items()}}
        y = _forward(args)
    with _jax.named_scope("loss_head"):
        err = _jnp.square(y.astype(_jnp.float32) - loss_target)
        return 0.5 * _jnp.sum(_jnp.mean(err, axis=-1)) if err.ndim else 0.5 * err


def _adamw(w, g, m, v):
    m = ADAM_B1 * m + (1.0 - ADAM_B1) * g
    v = ADAM_B2 * v + (1.0 - ADAM_B2) * _jnp.square(g)
    m_hat = m / (1.0 - ADAM_B1 ** ADAM_STEP)
    v_hat = v / (1.0 - ADAM_B2 ** ADAM_STEP)
    delta = -ADAM_LR * (m_hat / (_jnp.sqrt(v_hat) + ADAM_EPS) + ADAM_WD * w)
    return delta, m, v


def reference(x, mem, norm_g, mem_norm_g, mem_w_kv, xa_q_norm_g, xa_k_norm_g, w_out, dn_w_in, dn_conv_w, dn_a_log, dn_dt_bias, dn_out_norm_g, sb_w_in, sb_q_norm_g, sb_k_norm_g, loss_target, m_norm_g, m_mem_norm_g, m_mem_w_kv, m_xa_q_norm_g, m_xa_k_norm_g, m_w_out, m_dn_w_in, m_dn_conv_w, m_dn_a_log, m_dn_dt_bias, m_dn_out_norm_g, m_sb_w_in, m_sb_q_norm_g, m_sb_k_norm_g, v_norm_g, v_mem_norm_g, v_mem_w_kv, v_xa_q_norm_g, v_xa_k_norm_g, v_w_out, v_dn_w_in, v_dn_conv_w, v_dn_a_log, v_dn_dt_bias, v_dn_out_norm_g, v_sb_w_in, v_sb_q_norm_g, v_sb_k_norm_g):
    given = dict(x=x, mem=mem, norm_g=norm_g, mem_norm_g=mem_norm_g, mem_w_kv=mem_w_kv, xa_q_norm_g=xa_q_norm_g, xa_k_norm_g=xa_k_norm_g, w_out=w_out, dn_w_in=dn_w_in, dn_conv_w=dn_conv_w, dn_a_log=dn_a_log, dn_dt_bias=dn_dt_bias, dn_out_norm_g=dn_out_norm_g, sb_w_in=sb_w_in, sb_q_norm_g=sb_q_norm_g, sb_k_norm_g=sb_k_norm_g, loss_target=loss_target, m_norm_g=m_norm_g, m_mem_norm_g=m_mem_norm_g, m_mem_w_kv=m_mem_w_kv, m_xa_q_norm_g=m_xa_q_norm_g, m_xa_k_norm_g=m_xa_k_norm_g, m_w_out=m_w_out, m_dn_w_in=m_dn_w_in, m_dn_conv_w=m_dn_conv_w, m_dn_a_log=m_dn_a_log, m_dn_dt_bias=m_dn_dt_bias, m_dn_out_norm_g=m_dn_out_norm_g, m_sb_w_in=m_sb_w_in, m_sb_q_norm_g=m_sb_q_norm_g, m_sb_k_norm_g=m_sb_k_norm_g, v_norm_g=v_norm_g, v_mem_norm_g=v_mem_norm_g, v_mem_w_kv=v_mem_w_kv, v_xa_q_norm_g=v_xa_q_norm_g, v_xa_k_norm_g=v_xa_k_norm_g, v_w_out=v_w_out, v_dn_w_in=v_dn_w_in, v_dn_conv_w=v_dn_conv_w, v_dn_a_log=v_dn_a_log, v_dn_dt_bias=v_dn_dt_bias, v_dn_out_norm_g=v_dn_out_norm_g, v_sb_w_in=v_sb_w_in, v_sb_q_norm_g=v_sb_q_norm_g, v_sb_k_norm_g=v_sb_k_norm_g)
    weights = {n: given[n] for n in TWIN_WEIGHTS}
    shared = {n: given[n] for n in SHARED_INPUTS}
    per_example = {n: given[n] for n in ['x', 'mem']}
    grad_fn = _jax.value_and_grad(_loss, argnums=(0, 1))

    def one_microbatch(ex, loss_target):
        ex = dict(ex)
        diff = ex.pop(TWIN_DIFF_INPUT)
        return grad_fn(weights, diff, {**shared, **ex}, loss_target)

    if N_MICROBATCH == 1:
        loss, (grad_w, grad_x) = one_microbatch(per_example, given["loss_target"])
    else:
        def body(carry, xs):
            loss_sum, grad_sum = carry
            l_k, (gw_k, gx_k) = one_microbatch(xs[0], xs[1])
            with _jax.named_scope("update"):
                return (loss_sum + l_k, _jax.tree.map(_jnp.add, grad_sum, gw_k)), gx_k

        init = (_jnp.zeros((), _jnp.float32), _jax.tree.map(_jnp.zeros_like, weights))
        (loss, grad_w), grad_x = _jax.lax.scan(body, init, (per_example, given["loss_target"]))
    with _jax.named_scope("update"):
        delta_w, new_m, new_v = {}, {}, {}
        for n in TWIN_WEIGHTS:
            delta_w[n], new_m[n], new_v[n] = _adamw(weights[n], grad_w[n], given["m_" + n], given["v_" + n])
    return (loss, grad_x, *[grad_w[n] for n in TWIN_WEIGHTS], *[delta_w[n] for n in TWIN_WEIGHTS],
            *[new_m[n] for n in TWIN_WEIGHTS], *[new_v[n] for n in TWIN_WEIGHTS])
```

```python
import functools
import math

import jax
import jax.numpy as jnp
from jax import lax
from jax.experimental import pallas as pl
from jax.experimental.pallas import tpu as pltpu

F32 = jnp.float32
BF16 = jnp.bfloat16
HI = lax.Precision.HIGHEST
MESH = pl.DeviceIdType.MESH

D_MODEL = 2048
INNER = 4096
XA_WIDTH = 1024
XA_HEADS = 4
XA_DIM = 256
MIX_WIDTH = 3072
HEAD_DIM = 128
DN_V_HEADS = 24
DN_QK_WIDTH = 1536
DN_CONV = 4
DN_PROJ = 11312
SB_PROJ = 14336
EPS = 1e-6
N_CHIPS = 4

CH = 128
LANE = 128

P0_XQ = 6144
P0_Z = 7168
P0_AB = 11264
P0 = 11776
P0_SHARD = DN_PROJ // N_CHIPS
P0_SHARD_PAD = 2944
P1_XQ = 9216
P1_Z = 10240
P1 = SB_PROJ

ADAM_LR = 0.001
ADAM_B1 = 0.9
ADAM_B2 = 0.999
ADAM_EPS = 1e-08
ADAM_WD = 0.01
ADAM_STEP = 10

VMEM_LIMIT = 48 * 1024 * 1024


def _cp(sem=None, **kw):
    return pltpu.CompilerParams(dimension_semantics=sem, vmem_limit_bytes=VMEM_LIMIT, **kw)


def _bdot(a, b, dims):
    return lax.dot_general(a.astype(BF16), b.astype(BF16), (dims, ((), ())), preferred_element_type=F32)


def _fdot(a, b, dims):
    return lax.dot_general(a, b, (dims, ((), ())), precision=HI, preferred_element_type=F32)


NN = ((1,), (0,))
NT = ((1,), (1,))
TN = ((0,), (0,))


def _sigmoid(x):
    return 1.0 / (1.0 + jnp.exp(-x))


def _softplus(x):
    return jnp.maximum(x, 0.0) + jnp.log(1.0 + jnp.exp(-jnp.abs(x)))


def _iota2(shape, axis):
    return lax.broadcasted_iota(jnp.int32, shape, axis)


def _matmul(a, b, *, ta=False, tb=False, out_dtype=F32, res=None, tm=1024, tn=512, tk=512, name):
    M = a.shape[1] if ta else a.shape[0]
    K = a.shape[0] if ta else a.shape[1]
    N = b.shape[0] if tb else b.shape[1]
    tm, tn, tk = min(tm, M), min(tn, N), min(tk, K)
    assert M % tm == 0 and N % tn == 0 and K % tk == 0, (name, M, N, K, tm, tn, tk)
    nk = K // tk
    dims = ((0,) if ta else (1,), (1,) if tb else (0,))
    has_res = res is not None

    def body(*refs):
        if has_res:
            a_ref, b_ref, r_ref, o_ref, acc = refs
        else:
            a_ref, b_ref, o_ref, acc = refs
        k = pl.program_id(2)

        @pl.when(k == 0)
        def _():
            acc[...] = jnp.zeros_like(acc)

        acc[...] += _bdot(a_ref[...], b_ref[...], dims)

        @pl.when(k == nk - 1)
        def _():
            r = acc[...]
            if has_res:
                r = r + r_ref[...]
            o_ref[...] = r.astype(out_dtype)

    a_spec = pl.BlockSpec((tk, tm), lambda i, j, k: (k, i)) if ta else pl.BlockSpec((tm, tk), lambda i, j, k: (i, k))
    b_spec = pl.BlockSpec((tn, tk), lambda i, j, k: (j, k)) if tb else pl.BlockSpec((tk, tn), lambda i, j, k: (k, j))
    o_spec = pl.BlockSpec((tm, tn), lambda i, j, k: (i, j))
    in_specs = [a_spec, b_spec] + ([o_spec] if has_res else [])
    args = (a, b) + ((res,) if has_res else ())
    return pl.pallas_call(
        body, grid=(M // tm, N // tn, nk), in_specs=in_specs, out_specs=o_spec,
        out_shape=jax.ShapeDtypeStruct((M, N), out_dtype), scratch_shapes=[pltpu.VMEM((tm, tn), F32)],
        compiler_params=_cp(("parallel", "parallel", "arbitrary")), name=name)(*args)


def _rmsnorm_fwd(x, g, *, name, tm=256):
    S, Dm = x.shape
    tm = min(tm, S)

    def body(x_ref, g_ref, o_ref):
        xv = x_ref[...]
        r = lax.rsqrt(jnp.mean(xv * xv, axis=-1, keepdims=True) + EPS)
        o_ref[...] = (xv * r * g_ref[...]).astype(BF16)

    return pl.pallas_call(
        body, grid=(S // tm,), in_specs=[pl.BlockSpec((tm, Dm), lambda i: (i, 0)), pl.BlockSpec((1, Dm), lambda i: (0, 0))],
        out_specs=pl.BlockSpec((tm, Dm), lambda i: (i, 0)), out_shape=jax.ShapeDtypeStruct((S, Dm), BF16),
        compiler_params=_cp(("parallel",)), name=name)(x, g.reshape(1, Dm))


def _rmsnorm_bwd(dh, x, g, dres, *, name, tm=256):
    S, Dm = x.shape
    tm = min(tm, S)
    want_dx = dres is not None

    def body(*refs):
        if want_dx:
            dh_ref, x_ref, g_ref, dr_ref, dx_ref, dg_ref = refs
        else:
            dh_ref, x_ref, g_ref, dg_ref = refs
        i = pl.program_id(0)
        xv = x_ref[...]
        dhv = dh_ref[...]
        r = lax.rsqrt(jnp.mean(xv * xv, axis=-1, keepdims=True) + EPS)
        y = xv * r
        part = jnp.sum(dhv * y, axis=0, keepdims=True)

        @pl.when(i == 0)
        def _():
            dg_ref[...] = jnp.zeros_like(dg_ref)

        dg_ref[...] += part
        if want_dx:
            dy = dhv * g_ref[...]
            dx_ref[...] = dr_ref[...] + r * (dy - y * jnp.mean(dy * y, axis=-1, keepdims=True))

    row = pl.BlockSpec((tm, Dm), lambda i: (i, 0))
    vec = pl.BlockSpec((1, Dm), lambda i: (0, 0))
    if want_dx:
        dx, dg = pl.pallas_call(
            body, grid=(S // tm,), in_specs=[row, row, vec, row], out_specs=(row, vec),
            out_shape=(jax.ShapeDtypeStruct((S, Dm), F32), jax.ShapeDtypeStruct((1, Dm), F32)),
            compiler_params=_cp(("arbitrary",)), name=name)(dh, x, g.reshape(1, Dm), dres)
        return dx, dg
    dg = pl.pallas_call(
        body, grid=(S // tm,), in_specs=[row, row, vec], out_specs=vec,
        out_shape=jax.ShapeDtypeStruct((1, Dm), F32), compiler_params=_cp(("arbitrary",)), name=name)(dh, x, g.reshape(1, Dm))
    return None, dg


def _gate_fwd(cat, proj, z_off, *, name, tm=256, tn=1024):
    S = cat.shape[0]
    tm = min(tm, S)
    zb = z_off // tn

    def body(c_ref, z_ref, y_ref):
        z = z_ref[...]
        y_ref[...] = (c_ref[...] * z * _sigmoid(z)).astype(BF16)

    return pl.pallas_call(
        body, grid=(S // tm, INNER // tn),
        in_specs=[pl.BlockSpec((tm, tn), lambda i, j: (i, j)), pl.BlockSpec((tm, tn), lambda i, j: (i, zb + j))],
        out_specs=pl.BlockSpec((tm, tn), lambda i, j: (i, j)), out_shape=jax.ShapeDtypeStruct((S, INNER), BF16),
        compiler_params=_cp(("parallel", "parallel")), name=name)(cat, proj)


def _gate_bwd(dy, cat, proj, z_off, *, name, tm=256, tn=1024):
    S = cat.shape[0]
    tm = min(tm, S)
    zb = z_off // tn

    def body(dy_ref, c_ref, z_ref, dc_ref, dz_ref):
        z = z_ref[...]
        sg = _sigmoid(z)
        d = dy_ref[...]
        dc_ref[...] = d * z * sg
        dz_ref[...] = d * c_ref[...] * sg * (1.0 + z * (1.0 - sg))

    blk = pl.BlockSpec((tm, tn), lambda i, j: (i, j))
    return pl.pallas_call(
        body, grid=(S // tm, INNER // tn),
        in_specs=[blk, blk, pl.BlockSpec((tm, tn), lambda i, j: (i, zb + j))], out_specs=(blk, blk),
        out_shape=(jax.ShapeDtypeStruct((S, INNER), F32), jax.ShapeDtypeStruct((S, INNER), F32)),
        compiler_params=_cp(("parallel", "parallel")), name=name)(dy, cat, proj)


def _loss_head(x, target, *, name, tm=256):
    S, Dm = x.shape
    tm = min(tm, S)

    nt = S // tm

    def body(x_ref, t_ref, dx_ref, l_ref, acc):
        i = pl.program_id(0)
        e = x_ref[...] - t_ref[...]
        dx_ref[...] = e * (1.0 / Dm)

        @pl.when(i == 0)
        def _():
            acc[...] = jnp.zeros_like(acc)

        acc[...] += jnp.sum(e * e, axis=0, keepdims=True) * (0.5 / Dm)

        @pl.when(i == nt - 1)
        def _():
            l_ref[...] = jnp.sum(acc[...], axis=1, keepdims=True) + jnp.zeros((1, LANE), F32)

    row = pl.BlockSpec((tm, Dm), lambda i: (i, 0))
    return pl.pallas_call(
        body, grid=(nt,), in_specs=[row, row], out_specs=(row, pl.BlockSpec((1, LANE), lambda i: (0, 0))),
        out_shape=(jax.ShapeDtypeStruct((S, Dm), F32), jax.ShapeDtypeStruct((1, LANE), F32)),
        scratch_shapes=[pltpu.VMEM((1, Dm), F32)],
        compiler_params=_cp(("arbitrary",)), name=name)(x, target)


def _xa_norm(v, g):
    r = lax.rsqrt(jnp.mean(v * v, axis=-1, keepdims=True) + EPS)
    return v * r, r


def _xa_fwd(proj, xq_off, kv, gq, gk, *, name, tm=512):
    S = proj.shape[0]
    tm = min(tm, S)
    qb = xq_off // XA_DIM
    n_mem = kv.shape[0]
    scale = XA_DIM ** -0.5

    def body(q_ref, k_ref, v_ref, gq_ref, gk_ref, o_ref):
        qh, _ = _xa_norm(q_ref[...], None)
        kh, _ = _xa_norm(k_ref[...], None)
        qn = qh * gq_ref[...]
        kn = kh * gk_ref[...]
        s = _bdot(qn, kn, NT) * scale
        s = s - jnp.max(s, axis=-1, keepdims=True)
        p = jnp.exp(s)
        p = p / jnp.sum(p, axis=-1, keepdims=True)
        o_ref[...] = _bdot(p, v_ref[...], NN)

    vec = pl.BlockSpec((1, XA_DIM), lambda h, i: (0, 0))
    return pl.pallas_call(
        body, grid=(XA_HEADS, S // tm),
        in_specs=[pl.BlockSpec((tm, XA_DIM), lambda h, i: (i, qb + h)),
                  pl.BlockSpec((n_mem, XA_DIM), lambda h, i: (0, h)),
                  pl.BlockSpec((n_mem, XA_DIM), lambda h, i: (0, XA_HEADS + h)), vec, vec],
        out_specs=pl.BlockSpec((tm, XA_DIM), lambda h, i: (i, h)),
        out_shape=jax.ShapeDtypeStruct((S, XA_WIDTH), F32),
        compiler_params=_cp(("parallel", "parallel")), name=name)(proj, kv, kv, gq.reshape(1, XA_DIM), gk.reshape(1, XA_DIM))


def _xa_bwd(dxa, proj, xq_off, kv, gq, gk, *, name, tm=512):
    S = proj.shape[0]
    tm = min(tm, S)
    nt = S // tm
    qb = xq_off // XA_DIM
    n_mem = kv.shape[0]
    scale = XA_DIM ** -0.5

    def body(d_ref, q_ref, k_ref, v_ref, gq_ref, gk_ref, dq_ref, dk_ref, dv_ref, dgq_ref, dgk_ref, dkn_acc):
        h = pl.program_id(0)
        i = pl.program_id(1)
        q = q_ref[...]
        k = k_ref[...]
        qh, rq = _xa_norm(q, None)
        kh, rk = _xa_norm(k, None)
        gqv = gq_ref[...]
        gkv = gk_ref[...]
        qn = qh * gqv
        kn = kh * gkv
        s = _bdot(qn, kn, NT) * scale
        s = s - jnp.max(s, axis=-1, keepdims=True)
        p = jnp.exp(s)
        p = p / jnp.sum(p, axis=-1, keepdims=True)
        d = d_ref[...]
        dp = _bdot(d, v_ref[...], NT)
        ds = p * (dp - jnp.sum(dp * p, axis=-1, keepdims=True)) * scale
        dqn = _bdot(ds, kn, NN)

        @pl.when(i == 0)
        def _():
            dkn_acc[...] = jnp.zeros_like(dkn_acc)
            dv_ref[...] = jnp.zeros_like(dv_ref)

        @pl.when(jnp.logical_and(i == 0, h == 0))
        def _():
            dgq_ref[...] = jnp.zeros_like(dgq_ref)
            dgk_ref[...] = jnp.zeros_like(dgk_ref)

        dkn_acc[...] += _bdot(ds, qn, TN)
        dv_ref[...] += _bdot(p, d, TN)
        dgq_ref[...] += jnp.sum(dqn * qh, axis=0, keepdims=True)
        dy = dqn * gqv
        dq_ref[...] = rq * (dy - qh * jnp.mean(dy * qh, axis=-1, keepdims=True))

        @pl.when(i == nt - 1)
        def _():
            dkn = dkn_acc[...]
            dgk_ref[...] += jnp.sum(dkn * kh, axis=0, keepdims=True)
            dyk = dkn * gkv
            dk_ref[...] = rk * (dyk - kh * jnp.mean(dyk * kh, axis=-1, keepdims=True))

    vec = pl.BlockSpec((1, XA_DIM), lambda h, i: (0, 0))
    kblk = pl.BlockSpec((n_mem, XA_DIM), lambda h, i: (0, h))
    vblk = pl.BlockSpec((n_mem, XA_DIM), lambda h, i: (0, XA_HEADS + h))
    dq, dk, dv, dgq, dgk = pl.pallas_call(
        body, grid=(XA_HEADS, nt),
        in_specs=[pl.BlockSpec((tm, XA_DIM), lambda h, i: (i, h)),
                  pl.BlockSpec((tm, XA_DIM), lambda h, i: (i, qb + h)), kblk, vblk, vec, vec],
        out_specs=(pl.BlockSpec((tm, XA_DIM), lambda h, i: (i, h)), kblk, kblk, vec, vec),
        out_shape=(jax.ShapeDtypeStruct((S, XA_WIDTH), F32), jax.ShapeDtypeStruct((n_mem, XA_WIDTH), F32),
                   jax.ShapeDtypeStruct((n_mem, XA_WIDTH), F32), jax.ShapeDtypeStruct((1, XA_DIM), F32),
                   jax.ShapeDtypeStruct((1, XA_DIM), F32)),
        scratch_shapes=[pltpu.VMEM((n_mem, XA_DIM), F32)],
        compiler_params=_cp(("arbitrary", "arbitrary")), name=name)(
            dxa, proj, kv, kv, gq.reshape(1, XA_DIM), gk.reshape(1, XA_DIM))
    return dq, jnp.concatenate([dk, dv], axis=1), dgq, dgk


SB_BLK = 128
SB_HEADS = 24


def _sb_scores(qi, kj, i, j):
    z = _bdot(qi, kj, NT) * (HEAD_DIM ** -0.5)
    t_pos = i * SB_BLK + _iota2((SB_BLK, SB_BLK), 0)
    s_pos = j * SB_BLK + _iota2((SB_BLK, SB_BLK), 1)
    mask = s_pos < t_pos
    sp = _softplus(z)
    lr = jnp.where(mask, -sp, 0.0)
    ls = z - sp
    return lr, ls, mask


def _sb_fwd(proj, gq, gk, *, name):
    S = proj.shape[0]
    nb = S // SB_BLK

    def body(q_ref, k_ref, v_ref, gq_ref, gk_ref, o_ref, tot_ref, qn_s, kn_s):
        q = q_ref[...]
        k = k_ref[...]
        qn_s[...] = (q * lax.rsqrt(jnp.mean(q * q, axis=-1, keepdims=True) + EPS) * gq_ref[...]).astype(BF16)
        kn_s[...] = (k * lax.rsqrt(jnp.mean(k * k, axis=-1, keepdims=True) + EPS) * gk_ref[...]).astype(BF16)
        after = (_iota2((SB_BLK, SB_BLK), 0) > _iota2((SB_BLK, SB_BLK), 1)).astype(F32)

        def qblock(i, _):
            rows = pl.ds(pl.multiple_of(i * SB_BLK, SB_BLK), SB_BLK)
            qi = qn_s[rows, :]

            def kblock(jj, carry):
                acc, run = carry
                j = i - jj
                cols = pl.ds(pl.multiple_of(j * SB_BLK, SB_BLK), SB_BLK)
                lr, ls, mask = _sb_scores(qi, kn_s[cols, :], i, j)
                later = _fdot(lr, after, NN) + run
                a = jnp.where(mask, jnp.exp(ls + later), 0.0)
                acc = acc + _bdot(a, v_ref[cols, :], NN)
                run = run + jnp.sum(lr, axis=-1, keepdims=True)
                return acc, run

            acc, run = lax.fori_loop(0, i + 1, kblock,
                                     (jnp.zeros((SB_BLK, HEAD_DIM), F32), jnp.zeros((SB_BLK, 1), F32)))
            o_ref[rows, :] = acc
            tot_ref[rows, :] = run + jnp.zeros((SB_BLK, HEAD_DIM), F32)
            return 0

        lax.fori_loop(0, nb, qblock, 0)

    vec = pl.BlockSpec((1, HEAD_DIM), lambda h: (0, 0))
    out = pl.BlockSpec((S, HEAD_DIM), lambda h: (0, h))
    return pl.pallas_call(
        body, grid=(SB_HEADS,),
        in_specs=[pl.BlockSpec((S, HEAD_DIM), lambda h: (0, h)), pl.BlockSpec((S, HEAD_DIM), lambda h: (0, SB_HEADS + h)),
                  pl.BlockSpec((S, HEAD_DIM), lambda h: (0, 2 * SB_HEADS + h)), vec, vec],
        out_specs=(out, out), out_shape=(jax.ShapeDtypeStruct((S, MIX_WIDTH), F32),) * 2,
        scratch_shapes=[pltpu.VMEM((S, HEAD_DIM), BF16), pltpu.VMEM((S, HEAD_DIM), BF16)],
        compiler_params=_cp(("parallel",)), name=name)(proj, proj, proj, gq.reshape(1, HEAD_DIM), gk.reshape(1, HEAD_DIM))


def _sb_bwd(dmix, tot, proj, gq, gk, *, name):
    S = proj.shape[0]
    nb = S // SB_BLK
    scale = HEAD_DIM ** -0.5

    def body(do_ref, o_ref, q_ref, k_ref, v_ref, gq_ref, gk_ref, dq_ref, dk_ref, dv_ref, dgq_ref, dgk_ref,
             qn_s, kn_s, dkn_s):
        h = pl.program_id(0)
        q = q_ref[...]
        k = k_ref[...]
        rq = lax.rsqrt(jnp.mean(q * q, axis=-1, keepdims=True) + EPS)
        rk = lax.rsqrt(jnp.mean(k * k, axis=-1, keepdims=True) + EPS)
        gqv = gq_ref[...]
        gkv = gk_ref[...]
        qn_s[...] = (q * rq * gqv).astype(BF16)
        kn_s[...] = (k * rk * gkv).astype(BF16)
        dkn_s[...] = jnp.zeros_like(dkn_s)
        dv_ref[...] = jnp.zeros_like(dv_ref)
        r_i = _iota2((SB_BLK, SB_BLK), 0)
        c_i = _iota2((SB_BLK, SB_BLK), 1)
        upto = (r_i <= c_i).astype(F32)
        before = (r_i < c_i).astype(F32)

        def qblock(i, _):
            rows = pl.ds(pl.multiple_of(i * SB_BLK, SB_BLK), SB_BLK)
            qi = qn_s[rows, :]
            doi = do_ref[rows, :]
            tot_i = o_ref[rows, :]

            def kblock(j, carry):
                dqn, run, run_b = carry
                cols = pl.ds(pl.multiple_of(j * SB_BLK, SB_BLK), SB_BLK)
                kj = kn_s[cols, :]
                vj = v_ref[cols, :]
                lr, ls, mask = _sb_scores(qi, kj, i, j)
                later = tot_i - (_fdot(lr, upto, NN) + run)
                a = jnp.where(mask, jnp.exp(ls + later), 0.0)
                b = _bdot(doi, vj, NT) * a
                cum = _fdot(b, before, NN) + run_b
                beta = jnp.exp(ls)
                dz = jnp.where(mask, b * (1.0 - beta) - cum * beta, 0.0) * scale
                dv_ref[cols, :] += _bdot(a, doi, TN)
                dkn_s[cols, :] += _bdot(dz, qi, TN)
                dqn = dqn + _bdot(dz, kj, NN)
                run = run + jnp.sum(lr, axis=-1, keepdims=True)
                run_b = run_b + jnp.sum(b, axis=-1, keepdims=True)
                return dqn, run, run_b

            zero1 = jnp.zeros((SB_BLK, 1), F32)
            dqn, _, _ = lax.fori_loop(0, i + 1, kblock, (jnp.zeros((SB_BLK, HEAD_DIM), F32), zero1, zero1))
            dq_ref[rows, :] = dqn
            return 0

        lax.fori_loop(0, nb, qblock, 0)

        @pl.when(h == 0)
        def _():
            dgq_ref[...] = jnp.zeros_like(dgq_ref)
            dgk_ref[...] = jnp.zeros_like(dgk_ref)

        dqn = dq_ref[...]
        qh = q * rq
        dgq_ref[...] += jnp.sum(dqn * qh, axis=0, keepdims=True)
        dy = dqn * gqv
        dq_ref[...] = rq * (dy - qh * jnp.mean(dy * qh, axis=-1, keepdims=True))
        dkn = dkn_s[...]
        kh = k * rk
        dgk_ref[...] += jnp.sum(dkn * kh, axis=0, keepdims=True)
        dyk = dkn * gkv
        dk_ref[...] = rk * (dyk - kh * jnp.mean(dyk * kh, axis=-1, keepdims=True))

    vec = pl.BlockSpec((1, HEAD_DIM), lambda h: (0, 0))
    hb = lambda off: pl.BlockSpec((S, HEAD_DIM), lambda h: (0, off + h))
    dq, dk, dv, dgq, dgk = pl.pallas_call(
        body, grid=(SB_HEADS,),
        in_specs=[hb(0), hb(0), hb(0), hb(SB_HEADS), hb(2 * SB_HEADS), vec, vec],
        out_specs=(hb(0), hb(0), hb(0), vec, vec),
        out_shape=(jax.ShapeDtypeStruct((S, MIX_WIDTH), F32),) * 3 + (jax.ShapeDtypeStruct((1, HEAD_DIM), F32),) * 2,
        scratch_shapes=[pltpu.VMEM((S, HEAD_DIM), BF16), pltpu.VMEM((S, HEAD_DIM), BF16), pltpu.VMEM((S, HEAD_DIM), F32)],
        compiler_params=_cp(("arbitrary",)), name=name)(
            dmix, tot, proj, proj, proj, gq.reshape(1, HEAD_DIM), gk.reshape(1, HEAD_DIM))
    return jnp.concatenate([dq, dk, dv], axis=1), dgq, dgk


def _shift_down(x, k):
    if k == 0:
        return x
    r = pltpu.roll(x, k, 0)
    return jnp.where(_iota2(x.shape, 0) >= k, r, 0.0)


def _shift_up(x, k):
    if k == 0:
        return x
    n = x.shape[0]
    r = pltpu.roll(x, n - k, 0)
    return jnp.where(_iota2(x.shape, 0) < n - k, r, 0.0)


def _conv(x, w):
    c = w[DN_CONV - 1] * x
    for k in range(1, DN_CONV):
        c = c + w[DN_CONV - 1 - k] * _shift_down(x, k)
    return c


def _dn_pre_fwd(proj, conv_w, col0, ncols, *, l2, scale, name):
    S = proj.shape[0]
    cb = col0 // HEAD_DIM

    def body(x_ref, w_ref, o_ref):
        c = _conv(x_ref[...], [w_ref[k:k + 1, :] for k in range(DN_CONV)])
        a = c * _sigmoid(c)
        if l2:
            a = a * (lax.rsqrt(jnp.sum(a * a, axis=-1, keepdims=True) + EPS) * scale)
        o_ref[...] = a

    return pl.pallas_call(
        body, grid=(ncols // HEAD_DIM,),
        in_specs=[pl.BlockSpec((S, HEAD_DIM), lambda j: (0, cb + j)), pl.BlockSpec((DN_CONV, HEAD_DIM), lambda j: (0, cb + j))],
        out_specs=pl.BlockSpec((S, HEAD_DIM), lambda j: (0, j)), out_shape=jax.ShapeDtypeStruct((S, ncols), F32),
        compiler_params=_cp(("parallel",)), name=name)(proj, conv_w)


def _dn_pre_bwd(dout, proj, conv_w, col0, ncols, *, l2, scale, pair, name):
    S = proj.shape[0]
    cb = col0 // HEAD_DIM
    dw_in = 2 * HEAD_DIM if pair else HEAD_DIM

    def body(d_ref, x_ref, w_ref, dx_ref, dw_ref):
        x = x_ref[...]
        w = [w_ref[k:k + 1, :] for k in range(DN_CONV)]
        c = _conv(x, w)
        sg = _sigmoid(c)
        a = c * sg
        d = d_ref[...]
        if pair:
            d = d[:, :HEAD_DIM] + d[:, HEAD_DIM:]
        if l2:
            r = lax.rsqrt(jnp.sum(a * a, axis=-1, keepdims=True) + EPS)
            y = a * r
            d = d * scale
            d = r * (d - y * jnp.sum(d * y, axis=-1, keepdims=True))
        dc = d * sg * (1.0 + c * (1.0 - sg))
        dx = w[DN_CONV - 1] * dc
        for k in range(1, DN_CONV):
            dx = dx + w[DN_CONV - 1 - k] * _shift_up(dc, k)
        dx_ref[...] = dx
        for k in range(DN_CONV):
            dw_ref[3 - k:4 - k, :] = jnp.sum(dc * _shift_down(x, k), axis=0, keepdims=True)

    return pl.pallas_call(
        body, grid=(ncols // HEAD_DIM,),
        in_specs=[pl.BlockSpec((S, dw_in), lambda j: (0, j)), pl.BlockSpec((S, HEAD_DIM), lambda j: (0, cb + j)),
                  pl.BlockSpec((DN_CONV, HEAD_DIM), lambda j: (0, cb + j))],
        out_specs=(pl.BlockSpec((S, HEAD_DIM), lambda j: (0, j)), pl.BlockSpec((DN_CONV, HEAD_DIM), lambda j: (0, j))),
        out_shape=(jax.ShapeDtypeStruct((S, ncols), F32), jax.ShapeDtypeStruct((DN_CONV, ncols), F32)),
        compiler_params=_cp(("parallel",)), name=name)(dout, proj, conv_w)


def _dn_ab_fwd(proj, a_log, dt_bias, *, name, tm=512):
    S = proj.shape[0]
    tm = min(tm, S)
    ab = P0_AB // LANE

    def body(a_ref, b_ref, al_ref, dt_ref, g_ref, be_ref):
        g_ref[...] = -jnp.exp(al_ref[...]) * _softplus(a_ref[...] + dt_ref[...])
        be_ref[...] = _sigmoid(b_ref[...])

    vec = pl.BlockSpec((1, LANE), lambda i: (0, 0))
    out = pl.BlockSpec((tm, LANE), lambda i: (i, 0))
    return pl.pallas_call(
        body, grid=(S // tm,),
        in_specs=[pl.BlockSpec((tm, LANE), lambda i: (i, ab)), pl.BlockSpec((tm, LANE), lambda i: (i, ab + 1)), vec, vec],
        out_specs=(out, out), out_shape=(jax.ShapeDtypeStruct((S, LANE), F32),) * 2,
        compiler_params=_cp(("parallel",)), name=name)(proj, proj, a_log, dt_bias)


def _dn_ab_bwd(dg, dbeta, proj, a_log, dt_bias, *, name, tm=512):
    S = proj.shape[0]
    tm = min(tm, S)
    ab = P0_AB // LANE

    def body(dg_ref, db_ref, a_ref, b_ref, al_ref, dt_ref, dab_ref, dal_ref, ddt_ref):
        i = pl.program_id(0)
        ea = jnp.exp(al_ref[...])
        u = a_ref[...] + dt_ref[...]
        dgv = dg_ref[...]
        da = dgv * (-ea) * _sigmoid(u)
        be = _sigmoid(b_ref[...])
        dab_ref[:, 0:LANE] = da
        dab_ref[:, LANE:2 * LANE] = db_ref[...] * be * (1.0 - be)
        dab_ref[:, 2 * LANE:] = jnp.zeros((tm, 2 * LANE), F32)

        @pl.when(i == 0)
        def _():
            dal_ref[...] = jnp.zeros_like(dal_ref)
            ddt_ref[...] = jnp.zeros_like(ddt_ref)

        dal_ref[...] += jnp.sum(dgv * (-ea) * _softplus(u), axis=0, keepdims=True)
        ddt_ref[...] += jnp.sum(da, axis=0, keepdims=True)

    vec = pl.BlockSpec((1, LANE), lambda i: (0, 0))
    row = pl.BlockSpec((tm, LANE), lambda i: (i, 0))
    return pl.pallas_call(
        body, grid=(S // tm,),
        in_specs=[row, row, pl.BlockSpec((tm, LANE), lambda i: (i, ab)), pl.BlockSpec((tm, LANE), lambda i: (i, ab + 1)), vec, vec],
        out_specs=(pl.BlockSpec((tm, 4 * LANE), lambda i: (i, 0)), vec, vec),
        out_shape=(jax.ShapeDtypeStruct((S, 4 * LANE), F32), jax.ShapeDtypeStruct((1, LANE), F32),
                   jax.ShapeDtypeStruct((1, LANE), F32)),
        compiler_params=_cp(("arbitrary",)), name=name)(dg, dbeta, proj, proj, a_log, dt_bias)


def _tri_inverse(a):
    eye = (_iota2((CH, CH), 0) == _iota2((CH, CH), 1)).astype(F32)
    t = eye - a
    x = _fdot(a, a, NN)
    n = 2
    while True:
        t = t + _fdot(t, x, NN)
        n *= 2
        if n >= CH:
            break
        x = _fdot(x, x, NN)
    return t


def _pick_col(m, n):
    return jnp.sum(jnp.where(_iota2(m.shape, 1) == n, m, 0.0), axis=1, keepdims=True)


def _dn_chunk_common(q, k, v, gc_c, gc_r, be_c):
    r_i = _iota2((CH, CH), 0)
    c_i = _iota2((CH, CH), 1)
    incl = r_i >= c_i
    strict = r_i > c_i
    dec = jnp.exp(jnp.where(incl, gc_c - gc_r, -1e30))
    e = jnp.exp(gc_c)
    gl = jnp.sum(jnp.where(_iota2((1, CH), 1) == CH - 1, gc_r, 0.0), axis=1, keepdims=True)
    kds = jnp.exp(gl - gc_c)
    cd = jnp.exp(gl)
    kk = _bdot(k, k, NT)
    a = jnp.where(strict, be_c * kk * dec, 0.0)
    qk = _bdot(q, k, NT)
    p = qk * dec
    return dict(incl=incl, strict=strict, dec=dec, e=e, kds=kds, cd=cd, kk=kk, a=a, qk=qk, p=p)


def _dn_core_fwd(qn, kn, vc, g_rows, b_rows, out_g, *, name):
    S = qn.shape[0]
    nc = S // CH

    def body(q_ref, k_ref, v_ref, g_ref, b_ref, og_ref, o_ref, st_ref, t_ref, gcr, gcc, bcc):
        r_i = _iota2((CH, CH), 0)
        c_i = _iota2((CH, CH), 1)
        lc = (r_i >= c_i).astype(F32)
        eye = (r_i == c_i).astype(F32)
        g_rows_v = g_ref[0]
        gcr[...] = _fdot(g_rows_v, lc, NT)
        gcc[...] = _fdot(lc, g_rows_v, NT)
        bcc[...] = _fdot(eye, b_ref[0], NT)
        ogv = og_ref[...]

        def chunk(n, state):
            rows = pl.ds(pl.multiple_of(n * CH, CH), CH)
            q = q_ref[rows, :]
            k = k_ref[rows, :]
            v = v_ref[rows, :]
            gc_c = _pick_col(gcc[...], n)
            be_c = _pick_col(bcc[...], n)
            gc_r = gcr[pl.ds(n, 1), :]
            c = _dn_chunk_common(q, k, v, gc_c, gc_r, be_c)
            t = _tri_inverse(c["a"])
            u0 = _bdot(t, be_c * v, NN)
            w = _bdot(t, (be_c * c["e"]) * k, NN)
            u = u0 - _bdot(w, state, NN)
            o = _bdot(c["e"] * q, state, NN) + _bdot(c["p"], u, NN)
            st_ref[0, n] = state
            t_ref[0, n] = t
            o_ref[rows, :] = o * lax.rsqrt(jnp.mean(o * o, axis=-1, keepdims=True) + EPS) * ogv
            return c["cd"] * state + _bdot(c["kds"] * k, u, TN)

        lax.fori_loop(0, nc, chunk, jnp.zeros((HEAD_DIM, HEAD_DIM), F32))

    hb = lambda div: pl.BlockSpec((S, HEAD_DIM), lambda h: (0, h // div))
    rows_spec = pl.BlockSpec((1, LANE, CH), lambda h: (h, 0, 0))
    return pl.pallas_call(
        body, grid=(DN_V_HEADS,),
        in_specs=[hb(2), hb(2), hb(1), rows_spec, rows_spec, pl.BlockSpec((1, HEAD_DIM), lambda h: (0, 0))],
        out_specs=(hb(1), pl.BlockSpec((1, nc, HEAD_DIM, HEAD_DIM), lambda h: (h, 0, 0, 0)),
                   pl.BlockSpec((1, nc, CH, CH), lambda h: (h, 0, 0, 0))),
        out_shape=(jax.ShapeDtypeStruct((S, MIX_WIDTH), F32), jax.ShapeDtypeStruct((DN_V_HEADS, nc, HEAD_DIM, HEAD_DIM), F32),
                   jax.ShapeDtypeStruct((DN_V_HEADS, nc, CH, CH), F32)),
        scratch_shapes=[pltpu.VMEM((LANE, CH), F32), pltpu.VMEM((CH, LANE), F32), pltpu.VMEM((CH, LANE), F32)],
        compiler_params=_cp(("parallel",)), name=name)(qn, kn, vc, g_rows, b_rows, out_g.reshape(1, HEAD_DIM))


def _dn_core_bwd(dmix, qn, kn, vc, g_rows, b_rows, out_g, states, tinv, *, name):
    S = qn.shape[0]
    nc = S // CH

    def body(do_ref, q_ref, k_ref, v_ref, g_ref, b_ref, og_ref, st_ref, t_ref,
             dq_ref, dk_ref, dv_ref, dg_ref, db_ref, dog_ref, gcr, gcc, bcc, dgc_acc):
        h = pl.program_id(0)
        r_i = _iota2((CH, CH), 0)
        c_i = _iota2((CH, CH), 1)
        lc = (r_i >= c_i).astype(F32)
        eye = (r_i == c_i).astype(F32)
        ones = jnp.ones((CH, LANE), F32)
        g_rows_v = g_ref[0]
        gcr[...] = _fdot(g_rows_v, lc, NT)
        gcc[...] = _fdot(lc, g_rows_v, NT)
        bcc[...] = _fdot(eye, b_ref[0], NT)
        ogv = og_ref[...]
        dgc_acc[...] = jnp.zeros_like(dgc_acc)
        db_ref[...] = jnp.zeros_like(db_ref)
        lane_n = _iota2((CH, LANE), 1)
        last_row = _iota2((CH, 1), 0) == CH - 1

        @pl.when(h == 0)
        def _():
            dog_ref[...] = jnp.zeros_like(dog_ref)

        def chunk(m, carry):
            ds_next, dog = carry
            n = nc - 1 - m
            rows = pl.ds(pl.multiple_of(n * CH, CH), CH)
            q = q_ref[rows, :]
            k = k_ref[rows, :]
            v = v_ref[rows, :]
            state = st_ref[0, n]
            t = t_ref[0, n]
            gc_c = _pick_col(gcc[...], n)
            be_c = _pick_col(bcc[...], n)
            gc_r = gcr[pl.ds(n, 1), :]
            c = _dn_chunk_common(q, k, v, gc_c, gc_r, be_c)
            e, kds, cd, dec, a, p = c["e"], c["kds"], c["cd"], c["dec"], c["a"], c["p"]
            vb = be_c * v
            kbe = (be_c * e) * k
            u0 = _bdot(t, vb, NN)
            w = _bdot(t, kbe, NN)
            u = u0 - _bdot(w, state, NN)
            qd = e * q
            kd = kds * k
            o = _bdot(qd, state, NN) + _bdot(p, u, NN)
            r = lax.rsqrt(jnp.mean(o * o, axis=-1, keepdims=True) + EPS)
            y = o * r
            don = do_ref[rows, :]
            dog = dog + jnp.sum(don * y, axis=0, keepdims=True)
            dy = don * ogv
            d_o = r * (dy - y * jnp.mean(dy * y, axis=-1, keepdims=True))
            du = _bdot(p, d_o, TN) + _bdot(kd, ds_next, NN)
            dqd = _bdot(d_o, state, NT)
            dstate = _bdot(qd, d_o, TN) + cd * ds_next - _bdot(w, du, TN)
            dcd = jnp.sum(jnp.sum(ds_next * state, axis=1, keepdims=True), axis=0, keepdims=True)
            dkd = _bdot(u, ds_next, NT)
            dw = -_bdot(du, state, NT)
            dvb = _bdot(t, du, TN)
            dkbe = _bdot(t, dw, TN)
            da = -jnp.where(c["strict"], _bdot(dvb, u0, NT) + _bdot(dkbe, w, NT), 0.0)
            dp = jnp.where(c["incl"], _bdot(d_o, u, NT), 0.0)
            gmat = da * a + dp * p
            dad = da * dec
            x = be_c * dad
            dpd = dp * dec
            dk = _bdot(x, k, NN) + _bdot(x, k, TN) + _bdot(dpd, q, TN)
            dq = _bdot(dpd, k, NN) + e * dqd
            dbe = jnp.sum(dad * c["kk"], axis=1, keepdims=True)
            dgc = jnp.sum(gmat, axis=1, keepdims=True) + jnp.sum(dqd * q, axis=1, keepdims=True) * e
            rk = jnp.sum(dkd * k, axis=1, keepdims=True) * kds
            dk = dk + kds * dkd
            dgc = dgc - rk
            dgl = jnp.sum(rk, axis=0, keepdims=True) + dcd * cd
            sk = jnp.sum(dkbe * k, axis=1, keepdims=True)
            dk = dk + (be_c * e) * dkbe
            dbe = dbe + sk * e + jnp.sum(dvb * v, axis=1, keepdims=True)
            dgc = dgc + sk * be_c * e
            dgc = dgc + jnp.where(last_row, dgl, 0.0)
            dgc = dgc - _fdot(gmat, ones, TN)
            dq_ref[rows, :] = dq
            dk_ref[rows, :] = dk
            dv_ref[rows, :] = be_c * dvb
            dgc_acc[...] = jnp.where(lane_n == n, dgc, dgc_acc[...])
            db_ref[0] = jnp.where(lane_n == n, dbe, db_ref[0])
            return dstate, dog

        _, dog = lax.fori_loop(0, nc, chunk, (jnp.zeros((HEAD_DIM, HEAD_DIM), F32), jnp.zeros((1, HEAD_DIM), F32)))
        dog_ref[...] += dog
        dg_ref[0] = _fdot(lc, dgc_acc[...], TN)

    hb = lambda div: pl.BlockSpec((S, HEAD_DIM), lambda h: (0, h // div))
    rows_spec = pl.BlockSpec((1, LANE, CH), lambda h: (h, 0, 0))
    cols_spec = pl.BlockSpec((1, CH, LANE), lambda h: (h, 0, 0))
    vec = pl.BlockSpec((1, HEAD_DIM), lambda h: (0, 0))
    big = lambda: jax.ShapeDtypeStruct((S, MIX_WIDTH), F32)
    return pl.pallas_call(
        body, grid=(DN_V_HEADS,),
        in_specs=[hb(1), hb(2), hb(2), hb(1), rows_spec, rows_spec, vec,
                  pl.BlockSpec((1, nc, HEAD_DIM, HEAD_DIM), lambda h: (h, 0, 0, 0)),
                  pl.BlockSpec((1, nc, CH, CH), lambda h: (h, 0, 0, 0))],
        out_specs=(hb(1), hb(1), hb(1), cols_spec, cols_spec, vec),
        out_shape=(big(), big(), big(), jax.ShapeDtypeStruct((DN_V_HEADS, CH, LANE), F32),
                   jax.ShapeDtypeStruct((DN_V_HEADS, CH, LANE), F32), jax.ShapeDtypeStruct((1, HEAD_DIM), F32)),
        scratch_shapes=[pltpu.VMEM((LANE, CH), F32), pltpu.VMEM((CH, LANE), F32), pltpu.VMEM((CH, LANE), F32),
                        pltpu.VMEM((CH, LANE), F32)],
        compiler_params=_cp(("arbitrary",)), name=name)(
            dmix, qn, kn, vc, g_rows, b_rows, out_g.reshape(1, HEAD_DIM), states, tinv)


def _rows_form(x, nc):
    t = x[:, :DN_V_HEADS].T.reshape(DN_V_HEADS, nc, CH)
    return jnp.pad(t, ((0, 0), (0, LANE - nc), (0, 0)))


def _cols_to_nat(x, nc):
    t = jnp.transpose(x[:, :, :nc], (2, 1, 0)).reshape(nc * CH, DN_V_HEADS)
    return jnp.pad(t, ((0, 0), (0, LANE - DN_V_HEADS)))


_C_QKV = 2 * DN_QK_WIDTH + MIX_WIDTH


def _w0_to_padded(w):
    rows = w.shape[0]
    z = lambda n: jnp.zeros((rows, n), w.dtype)
    a = w[:, _C_QKV:_C_QKV + DN_V_HEADS]
    b = w[:, _C_QKV + DN_V_HEADS:_C_QKV + 2 * DN_V_HEADS]
    return jnp.concatenate([w[:, :_C_QKV], w[:, _C_QKV + 2 * DN_V_HEADS:], a, z(LANE - DN_V_HEADS), b,
                            z(P0 - P0_AB - LANE - DN_V_HEADS)], axis=1)


def _w0_from_padded(g):
    return jnp.concatenate([g[:, :_C_QKV], g[:, P0_AB:P0_AB + DN_V_HEADS], g[:, P0_AB + LANE:P0_AB + LANE + DN_V_HEADS],
                            g[:, _C_QKV:P0_AB]], axis=1)


def _pad_lane(v):
    v = v.reshape(1, -1)
    return jnp.pad(v, ((0, 0), (0, LANE - v.shape[1])))


def _local_step(x, mem, target, norm_g, mem_norm_g, w_kv, xa_q_g, xa_k_g, w_out, w_in0, conv_w, a_log, dt_bias, out_g,
                w_in1, sb_q_g, sb_k_g):
    S = x.shape[0]
    nc = S // CH
    al = _pad_lane(a_log)
    dtb = _pad_lane(dt_bias)
    q_scale = HEAD_DIM ** -0.5

    mem_n = _rmsnorm_fwd(mem, mem_norm_g, name="mem_norm")
    kv = [_matmul(mem_n, w_kv[i], out_dtype=F32, name=f"kv{i}") for i in range(2)]

    h0 = _rmsnorm_fwd(x, norm_g[0], name="norm0")
    proj0 = _matmul(h0, w_in0, name="proj0")
    qn = _dn_pre_fwd(proj0, conv_w, 0, DN_QK_WIDTH, l2=True, scale=q_scale, name="dn_pre_q")
    kn = _dn_pre_fwd(proj0, conv_w, DN_QK_WIDTH, DN_QK_WIDTH, l2=True, scale=1.0, name="dn_pre_k")
    vc = _dn_pre_fwd(proj0, conv_w, 2 * DN_QK_WIDTH, MIX_WIDTH, l2=False, scale=1.0, name="dn_pre_v")
    g_nat, b_nat = _dn_ab_fwd(proj0, al, dtb, name="dn_ab")
    g_rows = _rows_form(g_nat, nc)
    b_rows = _rows_form(b_nat, nc)
    mix0, states, tinv = _dn_core_fwd(qn, kn, vc, g_rows, b_rows, out_g, name="dn_core")
    xa0 = _xa_fwd(proj0, P0_XQ, kv[0], xa_q_g[0], xa_k_g[0], name="xa0")
    cat0 = jnp.concatenate([mix0, xa0], axis=1)
    y0 = _gate_fwd(cat0, proj0, P0_Z, name="gate0")
    x1 = _matmul(y0, w_out[0], res=x, name="out0")

    h1 = _rmsnorm_fwd(x1, norm_g[1], name="norm1")
    proj1 = _matmul(h1, w_in1, name="proj1")
    mix1, tot1 = _sb_fwd(proj1, sb_q_g, sb_k_g, name="sb")
    xa1 = _xa_fwd(proj1, P1_XQ, kv[1], xa_q_g[1], xa_k_g[1], name="xa1")
    cat1 = jnp.concatenate([mix1, xa1], axis=1)
    y1 = _gate_fwd(cat1, proj1, P1_Z, name="gate1")
    x2 = _matmul(y1, w_out[1], res=x1, name="out1")

    dx2, loss_vec = _loss_head(x2, target, name="loss")

    d_wout1 = _matmul(y1, dx2, ta=True, name="d_wout1")
    dy1 = _matmul(dx2, w_out[1], tb=True, name="dy1")
    dcat1, dz1 = _gate_bwd(dy1, cat1, proj1, P1_Z, name="gate1_bwd")
    dqkv1, d_sbq, d_sbk = _sb_bwd(dcat1[:, :MIX_WIDTH], tot1, proj1, sb_q_g, sb_k_g, name="sb_bwd")
    dxq1, dkv1, d_xaq1, d_xak1 = _xa_bwd(dcat1[:, MIX_WIDTH:], proj1, P1_XQ, kv[1], xa_q_g[1], xa_k_g[1], name="xa1_bwd")
    dproj1 = jnp.concatenate([dqkv1, dxq1, dz1], axis=1)
    d_win1 = _matmul(h1, dproj1, ta=True, name="d_win1")
    dh1 = _matmul(dproj1, w_in1, tb=True, name="dh1")
    dx1, d_ng1 = _rmsnorm_bwd(dh1, x1, norm_g[1], dx2, name="norm1_bwd")

    d_wout0 = _matmul(y0, dx1, ta=True, name="d_wout0")
    dy0 = _matmul(dx1, w_out[0], tb=True, name="dy0")
    dcat0, dz0 = _gate_bwd(dy0, cat0, proj0, P0_Z, name="gate0_bwd")
    dqv, dkv_h, dvc, dg_cols, db_cols, d_outg = _dn_core_bwd(
        dcat0[:, :MIX_WIDTH], qn, kn, vc, g_rows, b_rows, out_g, states, tinv, name="dn_core_bwd")
    dpq, dwq = _dn_pre_bwd(dqv, proj0, conv_w, 0, DN_QK_WIDTH, l2=True, scale=q_scale, pair=True, name="dn_pre_q_bwd")
    dpk, dwk = _dn_pre_bwd(dkv_h, proj0, conv_w, DN_QK_WIDTH, DN_QK_WIDTH, l2=True, scale=1.0, pair=True, name="dn_pre_k_bwd")
    dpv, dwv = _dn_pre_bwd(dvc, proj0, conv_w, 2 * DN_QK_WIDTH, MIX_WIDTH, l2=False, scale=1.0, pair=False, name="dn_pre_v_bwd")
    dab, d_alog, d_dt = _dn_ab_bwd(_cols_to_nat(dg_cols, nc), _cols_to_nat(db_cols, nc), proj0, al, dtb, name="dn_ab_bwd")
    dxq0, dkv0, d_xaq0, d_xak0 = _xa_bwd(dcat0[:, MIX_WIDTH:], proj0, P0_XQ, kv[0], xa_q_g[0], xa_k_g[0], name="xa0_bwd")
    dproj0 = jnp.concatenate([dpq, dpk, dpv, dxq0, dz0, dab], axis=1)
    d_win0 = _matmul(h0, dproj0, ta=True, name="d_win0")
    dh0 = _matmul(dproj0, w_in0, tb=True, name="dh0")
    dx0, d_ng0 = _rmsnorm_bwd(dh0, x, norm_g[0], dx1, name="norm0_bwd")

    d_wkv = [_matmul(mem_n, d, ta=True, name=f"d_wkv{i}") for i, d in enumerate((dkv0, dkv1))]
    dmem_n = _matmul(dkv1, w_kv[1], tb=True, res=_matmul(dkv0, w_kv[0], tb=True, name="dmem0"), name="dmem1")
    _, d_memg = _rmsnorm_bwd(dmem_n, mem, mem_norm_g, None, name="mem_norm_bwd")

    grads = dict(
        norm_g=jnp.concatenate([d_ng0, d_ng1], axis=0), mem_norm_g=d_memg.reshape(-1), mem_w_kv=jnp.stack(d_wkv),
        xa_q_norm_g=jnp.concatenate([d_xaq0, d_xaq1], axis=0), xa_k_norm_g=jnp.concatenate([d_xak0, d_xak1], axis=0),
        w_out=jnp.stack([d_wout0, d_wout1]), dn_w_in=d_win0, dn_conv_w=jnp.concatenate([dwq, dwk, dwv], axis=1),
        dn_a_log=d_alog[:, :DN_V_HEADS], dn_dt_bias=d_dt[:, :DN_V_HEADS], dn_out_norm_g=d_outg, sb_w_in=d_win1,
        sb_q_norm_g=d_sbq, sb_k_norm_g=d_sbk)
    return loss_vec, dx0, grads


ANY = pl.BlockSpec(memory_space=pl.ANY)


def _place():
    x, y, c = lax.axis_index("x"), lax.axis_index("y"), lax.axis_index("c")
    chips = [(1 - x, y), (x, 1 - y), (1 - x, 1 - y)]
    return x, y, c, 2 * x + y, (x, y, 1 - c), chips


def _rcopy(src, dst, send, recv, i, dev):
    return pltpu.make_async_remote_copy(src_ref=src, dst_ref=dst, send_sem=send.at[i], recv_sem=recv.at[i],
                                        device_id=dev, device_id_type=MESH)


def _gather_weights(srcs, *, name):
    nt = len(srcs)

    def body(*refs):
        src, dst = refs[:nt], refs[nt:2 * nt]
        send, recv, lsem = refs[2 * nt:]
        x, y, c, j, sib, chips = _place()
        local = [pltpu.make_async_copy(src[t], dst[t].at[j], lsem.at[t]) for t in range(nt)]
        for cp in local:
            cp.start()
        sends = []
        for t in range(nt):
            for k, (cx, cy) in enumerate(chips):
                sends.append(_rcopy(src[t].at[c], dst[t].at[j, c], send, recv, 6 * t + k, (cx, cy, c)))
                sends[-1].start()
        for t in range(nt):
            for k, (cx, cy) in enumerate(chips):
                landed = dst[t].at[2 * cx + cy, c]
                _rcopy(landed, landed, send, recv, 6 * t + k, (cx, cy, c)).wait_recv()
                sends.append(_rcopy(landed, landed, send, recv, 6 * t + 3 + k, sib))
                sends[-1].start()
        for t in range(nt):
            for k, (cx, cy) in enumerate(chips):
                other = dst[t].at[2 * cx + cy, 1 - c]
                _rcopy(other, other, send, recv, 6 * t + 3 + k, sib).wait_recv()
        for cp in sends:
            cp.wait_send()
        for cp in local:
            cp.wait()

    return pl.pallas_call(
        body, in_specs=[ANY] * nt, out_specs=[ANY] * nt,
        out_shape=[jax.ShapeDtypeStruct((N_CHIPS,) + s.shape, s.dtype) for s in srcs],
        scratch_shapes=[pltpu.SemaphoreType.DMA((6 * nt,)), pltpu.SemaphoreType.DMA((6 * nt,)), pltpu.SemaphoreType.DMA((nt,))],
        name=name)(*srcs)


def _swap_halves(xs, *, name):
    nt = len(xs)

    def body(*refs):
        src, dst = refs[:nt], refs[nt:2 * nt]
        send, recv = refs[2 * nt:]
        x, y, c, j, sib, chips = _place()
        cps = []
        for t in range(nt):
            for s in range(N_CHIPS):
                cps.append(_rcopy(src[t].at[s, 1 - c], dst[t].at[s], send, recv, 4 * t + s, sib))
                cps[-1].start()
        for cp in cps:
            cp.wait_recv()
        for cp in cps:
            cp.wait_send()

    return pl.pallas_call(
        body, in_specs=[ANY] * nt, out_specs=[ANY] * nt,
        out_shape=[jax.ShapeDtypeStruct((N_CHIPS,) + a.shape[2:], a.dtype) for a in xs],
        scratch_shapes=[pltpu.SemaphoreType.DMA((4 * nt,)), pltpu.SemaphoreType.DMA((4 * nt,))], name=name)(*xs)


def _scatter_to_chips(ps, *, name):
    nt = len(ps)

    def body(*refs):
        src, dst = refs[:nt], refs[nt:2 * nt]
        send, recv = refs[2 * nt:]
        x, y, c, j, sib, chips = _place()
        cps = []
        for t in range(nt):
            for k, (cx, cy) in enumerate(chips):
                cps.append(_rcopy(src[t].at[2 * cx + cy], dst[t].at[k], send, recv, 3 * t + k, (cx, cy, c)))
                cps[-1].start()
        for cp in cps:
            cp.wait_recv()
        for cp in cps:
            cp.wait_send()

    return pl.pallas_call(
        body, in_specs=[ANY] * nt, out_specs=[ANY] * nt,
        out_shape=[jax.ShapeDtypeStruct((3,) + a.shape[1:], a.dtype) for a in ps],
        scratch_shapes=[pltpu.SemaphoreType.DMA((3 * nt,)), pltpu.SemaphoreType.DMA((3 * nt,))], name=name)(*ps)


def _join_halves(fs, *, name):
    nt = len(fs)

    def body(*refs):
        src, dst = refs[:nt], refs[nt:2 * nt]
        send, recv, lsem = refs[2 * nt:]
        x, y, c, j, sib, chips = _place()
        local = [pltpu.make_async_copy(src[t], dst[t].at[c], lsem.at[t]) for t in range(nt)]
        cps = [_rcopy(src[t], dst[t].at[c], send, recv, t, sib) for t in range(nt)]
        for cp in local + cps:
            cp.start()
        for t in range(nt):
            other = dst[t].at[1 - c]
            _rcopy(other, other, send, recv, t, sib).wait_recv()
        for cp in cps:
            cp.wait_send()
        for cp in local:
            cp.wait()

    return pl.pallas_call(
        body, in_specs=[ANY] * nt, out_specs=[ANY] * nt,
        out_shape=[jax.ShapeDtypeStruct((2,) + a.shape, a.dtype) for a in fs],
        scratch_shapes=[pltpu.SemaphoreType.DMA((nt,)), pltpu.SemaphoreType.DMA((nt,)), pltpu.SemaphoreType.DMA((nt,))],
        name=name)(*fs)


def _all_reduce_small(pack, *, name):
    rows = pack.shape[0]

    def body(p_ref, o_ref, buf, send, recv):
        x, y, c = lax.axis_index("x"), lax.axis_index("y"), lax.axis_index("c")
        me = 4 * x + 2 * y + c
        buf[me] = p_ref[...]
        cps = []
        for r in range(1, 8):
            dev = (x ^ (r >> 2), y ^ ((r >> 1) & 1), c ^ (r & 1))
            cps.append(_rcopy(p_ref, buf.at[me], send, recv, r - 1, dev))
            cps[-1].start()
        for r in range(1, 8):
            frm = buf.at[me ^ r]
            _rcopy(frm, frm, send, recv, r - 1, (x, y, c)).wait_recv()
        for cp in cps:
            cp.wait_send()
        acc = buf[0]
        for d in range(1, 8):
            acc = acc + buf[d]
        o_ref[...] = acc

    vm = pl.BlockSpec(memory_space=pltpu.VMEM)
    return pl.pallas_call(
        body, in_specs=[vm], out_specs=vm, out_shape=jax.ShapeDtypeStruct(pack.shape, F32),
        scratch_shapes=[pltpu.VMEM((8, rows, LANE), F32), pltpu.SemaphoreType.DMA((7,)), pltpu.SemaphoreType.DMA((7,))],
        name=name)(pack)


def _add_halves(x, b, c_idx, *, name, tr=256):
    _, _, R, C = x.shape
    tr = min(tr, R)

    def body(c_ref, x_ref, b_ref, o_ref):
        o_ref[...] = (x_ref[...].astype(F32) + b_ref[...].astype(F32)).astype(o_ref.dtype)

    return pl.pallas_call(
        body,
        grid_spec=pltpu.PrefetchScalarGridSpec(
            num_scalar_prefetch=1, grid=(N_CHIPS, R // tr),
            in_specs=[pl.BlockSpec((None, None, tr, C), lambda s, i, c_ref: (s, c_ref[0], i, 0)),
                      pl.BlockSpec((None, tr, C), lambda s, i, c_ref: (s, i, 0))],
            out_specs=pl.BlockSpec((None, tr, C), lambda s, i, c_ref: (s, i, 0))),
        out_shape=jax.ShapeDtypeStruct(b.shape, b.dtype), compiler_params=_cp(("parallel", "parallel")), name=name)(c_idx, x, b)


def _sum_slot(p, rcv, j_idx, *, name, tr=256):
    _, R, C = p.shape
    tr = min(tr, R)

    def body(j_ref, p_ref, r_ref, o_ref):
        acc = p_ref[...].astype(F32)
        for k in range(3):
            acc = acc + r_ref[k].astype(F32)
        o_ref[...] = acc

    return pl.pallas_call(
        body,
        grid_spec=pltpu.PrefetchScalarGridSpec(
            num_scalar_prefetch=1, grid=(R // tr,),
            in_specs=[pl.BlockSpec((None, tr, C), lambda i, j_ref: (j_ref[0], i, 0)),
                      pl.BlockSpec((3, tr, C), lambda i, j_ref: (0, i, 0))],
            out_specs=pl.BlockSpec((tr, C), lambda i, j_ref: (i, 0))),
        out_shape=jax.ShapeDtypeStruct((R, C), F32), compiler_params=_cp(("parallel",)), name=name)(j_idx, p, rcv)


def _adamw(w, g, m, v, *, name, tr=128):
    R, C = w.shape
    tr = tr if R % tr == 0 else R

    def body(w_ref, g_ref, m_ref, v_ref, d_ref, nm_ref, nv_ref):
        gv = g_ref[...]
        nm = ADAM_B1 * m_ref[...] + (1.0 - ADAM_B1) * gv
        nv = ADAM_B2 * v_ref[...] + (1.0 - ADAM_B2) * (gv * gv)
        m_hat = nm / (1.0 - ADAM_B1 ** ADAM_STEP)
        v_hat = nv / (1.0 - ADAM_B2 ** ADAM_STEP)
        d_ref[...] = -ADAM_LR * (m_hat / (jnp.sqrt(v_hat) + ADAM_EPS) + ADAM_WD * w_ref[...])
        nm_ref[...] = nm
        nv_ref[...] = nv

    blk = pl.BlockSpec((tr, C), lambda i: (i, 0))
    sh = jax.ShapeDtypeStruct((R, C), F32)
    return pl.pallas_call(body, grid=(R // tr,), in_specs=[blk] * 4, out_specs=(blk,) * 3, out_shape=(sh,) * 3,
                          compiler_params=_cp(("parallel",)), name=name)(w, g, m, v)


_SMALL = ["norm_g", "mem_norm_g", "xa_q_norm_g", "xa_k_norm_g", "dn_a_log", "dn_dt_bias", "dn_out_norm_g",
          "sb_q_norm_g", "sb_k_norm_g"]


def _pack(parts):
    rows, metas, r0 = [], [], 0
    for p in parts:
        flat = p.reshape(-1).astype(F32)
        n = flat.shape[0]
        nr = -(-n // (8 * LANE)) * 8
        rows.append(jnp.pad(flat, (0, nr * LANE - n)).reshape(nr, LANE))
        metas.append((r0, nr, n, p.shape))
        r0 += nr
    return jnp.concatenate(rows, axis=0), metas


def _unpack(pack, metas):
    return [pack[r0:r0 + nr].reshape(-1)[:n].reshape(shape) for r0, nr, n, shape in metas]


def kernel(x, mem, norm_g, mem_norm_g, mem_w_kv, xa_q_norm_g, xa_k_norm_g, w_out, dn_w_in, dn_conv_w, dn_a_log, dn_dt_bias, dn_out_norm_g, sb_w_in, sb_q_norm_g, sb_k_norm_g, loss_target, m_norm_g, m_mem_norm_g, m_mem_w_kv, m_xa_q_norm_g, m_xa_k_norm_g, m_w_out, m_dn_w_in, m_dn_conv_w, m_dn_a_log, m_dn_dt_bias, m_dn_out_norm_g, m_sb_w_in, m_sb_q_norm_g, m_sb_k_norm_g, v_norm_g, v_mem_norm_g, v_mem_w_kv, v_xa_q_norm_g, v_xa_k_norm_g, v_w_out, v_dn_w_in, v_dn_conv_w, v_dn_a_log, v_dn_dt_bias, v_dn_out_norm_g, v_sb_w_in, v_sb_q_norm_g, v_sb_k_norm_g):
    W = dict(norm_g=norm_g, mem_norm_g=mem_norm_g, mem_w_kv=mem_w_kv, xa_q_norm_g=xa_q_norm_g, xa_k_norm_g=xa_k_norm_g,
             w_out=w_out, dn_w_in=dn_w_in, dn_conv_w=dn_conv_w, dn_a_log=dn_a_log, dn_dt_bias=dn_dt_bias,
             dn_out_norm_g=dn_out_norm_g, sb_w_in=sb_w_in, sb_q_norm_g=sb_q_norm_g, sb_k_norm_g=sb_k_norm_g)
    M = dict(norm_g=m_norm_g, mem_norm_g=m_mem_norm_g, mem_w_kv=m_mem_w_kv, xa_q_norm_g=m_xa_q_norm_g,
             xa_k_norm_g=m_xa_k_norm_g, w_out=m_w_out, dn_w_in=m_dn_w_in, dn_conv_w=m_dn_conv_w, dn_a_log=m_dn_a_log,
             dn_dt_bias=m_dn_dt_bias, dn_out_norm_g=m_dn_out_norm_g, sb_w_in=m_sb_w_in, sb_q_norm_g=m_sb_q_norm_g,
             sb_k_norm_g=m_sb_k_norm_g)
    V = dict(norm_g=v_norm_g, mem_norm_g=v_mem_norm_g, mem_w_kv=v_mem_w_kv, xa_q_norm_g=v_xa_q_norm_g,
             xa_k_norm_g=v_xa_k_norm_g, w_out=v_w_out, dn_w_in=v_dn_w_in, dn_conv_w=v_dn_conv_w, dn_a_log=v_dn_a_log,
             dn_dt_bias=v_dn_dt_bias, dn_out_norm_g=v_dn_out_norm_g, sb_w_in=v_sb_w_in, sb_q_norm_g=v_sb_q_norm_g,
             sb_k_norm_g=v_sb_k_norm_g)
    names = ["norm_g", "mem_norm_g", "mem_w_kv", "xa_q_norm_g", "xa_k_norm_g", "w_out", "dn_w_in", "dn_conv_w",
             "dn_a_log", "dn_dt_bias", "dn_out_norm_g", "sb_w_in", "sb_q_norm_g", "sb_k_norm_g"]
    cx, cy, cc = lax.axis_index("x"), lax.axis_index("y"), lax.axis_index("c")
    slot = 2 * cx + cy
    half_r = D_MODEL // 2
    conv_cols = dn_conv_w.shape[2]

    w0s = jnp.pad(dn_w_in[0].astype(BF16), ((0, 0), (0, P0_SHARD_PAD - P0_SHARD))).reshape(2, half_r, P0_SHARD_PAD)
    w1s = sb_w_in[0].astype(BF16).reshape(2, half_r, SB_PROJ // N_CHIPS)
    convs = jnp.pad(dn_conv_w[0], ((0, 8 - DN_CONV), (0, 0))).reshape(8, 2, conv_cols // 2).transpose(1, 0, 2)
    g0, g1, gout, gkv, gconv = _gather_weights([w0s, w1s, w_out.astype(BF16), mem_w_kv.astype(BF16), convs], name="gather_weights")
    w0_true = g0.reshape(N_CHIPS, D_MODEL, P0_SHARD_PAD)[:, :, :P0_SHARD].transpose(1, 0, 2).reshape(D_MODEL, DN_PROJ)
    w_in0 = _w0_to_padded(w0_true)
    w_in1 = g1.reshape(N_CHIPS, D_MODEL, SB_PROJ // N_CHIPS).transpose(1, 0, 2).reshape(D_MODEL, SB_PROJ)
    w_out_f = gout.transpose(1, 0, 2, 3).reshape(2, INNER, D_MODEL)
    w_kv_f = gkv.transpose(1, 0, 2, 3).reshape(2, D_MODEL, 2 * XA_WIDTH)
    conv_f = gconv.transpose(2, 0, 1, 3).reshape(8, N_CHIPS * conv_cols)[:DN_CONV]

    loss_vec, grad_x, g = _local_step(
        x[0], mem[0], loss_target[0], norm_g, mem_norm_g, w_kv_f, xa_q_norm_g, xa_k_norm_g, w_out_f, w_in0, conv_f,
        dn_a_log[0], dn_dt_bias[0], dn_out_norm_g[0], w_in1, sb_q_norm_g[0], sb_k_norm_g[0])

    d0 = _w0_from_padded(g["dn_w_in"]).reshape(2, half_r, N_CHIPS, P0_SHARD)
    d0 = jnp.pad(d0, ((0, 0), (0, 0), (0, 0), (0, P0_SHARD_PAD - P0_SHARD))).transpose(2, 0, 1, 3).astype(BF16)
    d1 = g["sb_w_in"].reshape(2, half_r, N_CHIPS, SB_PROJ // N_CHIPS).transpose(2, 0, 1, 3).astype(BF16)
    dout = g["w_out"].reshape(2, N_CHIPS, INNER // N_CHIPS, D_MODEL).transpose(1, 0, 2, 3).astype(BF16)
    dkv = g["mem_w_kv"].reshape(2, N_CHIPS, D_MODEL // N_CHIPS, 2 * XA_WIDTH).transpose(1, 0, 2, 3).astype(BF16)
    xs = [d0, d1, dout, dkv]
    c_idx = jnp.reshape(cc, (1,)).astype(jnp.int32)
    j_idx = jnp.reshape(slot, (1,)).astype(jnp.int32)
    from_sib = _swap_halves(xs, name="rs_swap")
    ps = [_add_halves(a, b, c_idx, name=f"rs_add{t}") for t, (a, b) in enumerate(zip(xs, from_sib))]
    rcvs = _scatter_to_chips(ps, name="rs_scatter")
    fs = [_sum_slot(p, r, j_idx, name=f"rs_sum{t}") for t, (p, r) in enumerate(zip(ps, rcvs))]
    j0, j1, jout, jkv = _join_halves(fs, name="rs_join")
    big_grads = dict(dn_w_in=j0.reshape(D_MODEL, P0_SHARD_PAD)[:, :P0_SHARD], sb_w_in=j1.reshape(D_MODEL, SB_PROJ // N_CHIPS),
                     w_out=jout.reshape(2 * INNER // N_CHIPS, D_MODEL), mem_w_kv=jkv.reshape(2 * D_MODEL // N_CHIPS, 2 * XA_WIDTH))

    pack, metas = _pack([g[n] for n in _SMALL] + [g["dn_conv_w"], loss_vec])
    red = _unpack(_all_reduce_small(pack, name="all_reduce_small"), metas)
    small_grads = {n: r.reshape(W[n].shape) for n, r in zip(_SMALL, red)}
    small_grads["dn_conv_w"] = lax.dynamic_slice_in_dim(red[len(_SMALL)], slot * conv_cols, conv_cols, axis=1).reshape(W["dn_conv_w"].shape)
    loss = red[-1][0, 0]

    out_g, out_d, out_m, out_v = {}, {}, {}, {}
    for n, gr in big_grads.items():
        shp = W[n].shape
        d, nm, nv = _adamw(W[n].reshape(gr.shape), gr, M[n].reshape(gr.shape), V[n].reshape(gr.shape), name=f"adamw_{n}")
        out_g[n], out_d[n], out_m[n], out_v[n] = gr.reshape(shp), d.reshape(shp), nm.reshape(shp), nv.reshape(shp)
    small_names = _SMALL + ["dn_conv_w"]
    wp, sm = _pack([W[n] for n in small_names])
    gp, _ = _pack([small_grads[n] for n in small_names])
    mp, _ = _pack([M[n] for n in small_names])
    vp, _ = _pack([V[n] for n in small_names])
    dp, nmp, nvp = _adamw(wp, gp, mp, vp, name="adamw_small")
    for n, d, nm, nv in zip(small_names, _unpack(dp, sm), _unpack(nmp, sm), _unpack(nvp, sm)):
        out_g[n], out_d[n], out_m[n], out_v[n] = small_grads[n], d, nm, nv

    return (loss, grad_x[None], *[out_g[n] for n in names], *[out_d[n] for n in names], *[out_m[n] for n in names],
            *[out_v[n] for n in names])
```

```python
import functools
import math

import jax
import jax.numpy as jnp
from jax import lax
from jax.experimental import pallas as pl
from jax.experimental.pallas import tpu as pltpu

F32 = jnp.float32
BF16 = jnp.bfloat16
HI = lax.Precision.HIGHEST
MESH = pl.DeviceIdType.MESH

D_MODEL = 2048
INNER = 4096
XA_WIDTH = 1024
XA_HEADS = 4
XA_DIM = 256
MIX_WIDTH = 3072
HEAD_DIM = 128
DN_V_HEADS = 24
DN_QK_WIDTH = 1536
DN_CONV = 4
DN_PROJ = 11312
SB_PROJ = 14336
EPS = 1e-6
N_CHIPS = 4

CH = 128
LANE = 128

P0_XQ = 6144
P0_Z = 7168
P0_AB = 11264
P0 = 11776
P0_SHARD = DN_PROJ // N_CHIPS
P0_SHARD_PAD = 2944
P1_XQ = 9216
P1_Z = 10240
P1 = SB_PROJ

ADAM_LR = 0.001
ADAM_B1 = 0.9
ADAM_B2 = 0.999
ADAM_EPS = 1e-08
ADAM_WD = 0.01
ADAM_STEP = 10

VMEM_LIMIT = 48 * 1024 * 1024


def _cp(sem=None, **kw):
    return pltpu.CompilerParams(dimension_semantics=sem, vmem_limit_bytes=VMEM_LIMIT, **kw)


def _bdot(a, b, dims):
    return lax.dot_general(a.astype(BF16), b.astype(BF16), (dims, ((), ())), preferred_element_type=F32)


def _fdot(a, b, dims):
    return lax.dot_general(a, b, (dims, ((), ())), precision=HI, preferred_element_type=F32)


NN = ((1,), (0,))
NT = ((1,), (1,))
TN = ((0,), (0,))


def _sigmoid(x):
    return 1.0 / (1.0 + jnp.exp(-x))


def _softplus(x):
    return jnp.maximum(x, 0.0) + jnp.log(1.0 + jnp.exp(-jnp.abs(x)))


def _iota2(shape, axis):
    return lax.broadcasted_iota(jnp.int32, shape, axis)


def _matmul(a, b, *, ta=False, tb=False, out_dtype=F32, res=None, tm=1024, tn=512, tk=512, name):
    M = a.shape[1] if ta else a.shape[0]
    K = a.shape[0] if ta else a.shape[1]
    N = b.shape[0] if tb else b.shape[1]
    tm, tn, tk = min(tm, M), min(tn, N), min(tk, K)
    assert M % tm == 0 and N % tn == 0 and K % tk == 0, (name, M, N, K, tm, tn, tk)
    nk = K // tk
    dims = ((0,) if ta else (1,), (1,) if tb else (0,))
    has_res = res is not None

    def body(*refs):
        if has_res:
            a_ref, b_ref, r_ref, o_ref, acc = refs
        else:
            a_ref, b_ref, o_ref, acc = refs
        k = pl.program_id(2)

        @pl.when(k == 0)
        def _():
            acc[...] = jnp.zeros_like(acc)

        acc[...] += _bdot(a_ref[...], b_ref[...], dims)

        @pl.when(k == nk - 1)
        def _():
            r = acc[...]
            if has_res:
                r = r + r_ref[...]
            o_ref[...] = r.astype(out_dtype)

    a_spec = pl.BlockSpec((tk, tm), lambda i, j, k: (k, i)) if ta else pl.BlockSpec((tm, tk), lambda i, j, k: (i, k))
    b_spec = pl.BlockSpec((tn, tk), lambda i, j, k: (j, k)) if tb else pl.BlockSpec((tk, tn), lambda i, j, k: (k, j))
    o_spec = pl.BlockSpec((tm, tn), lambda i, j, k: (i, j))
    in_specs = [a_spec, b_spec] + ([o_spec] if has_res else [])
    args = (a, b) + ((res,) if has_res else ())
    return pl.pallas_call(
        body, grid=(M // tm, N // tn, nk), in_specs=in_specs, out_specs=o_spec,
        out_shape=jax.ShapeDtypeStruct((M, N), out_dtype), scratch_shapes=[pltpu.VMEM((tm, tn), F32)],
        compiler_params=_cp(("parallel", "parallel", "arbitrary")), name=name)(*args)


def _rmsnorm_fwd(x, g, *, name, tm=256):
    S, Dm = x.shape
    tm = min(tm, S)

    def body(x_ref, g_ref, o_ref):
        xv = x_ref[...]
        r = lax.rsqrt(jnp.mean(xv * xv, axis=-1, keepdims=True) + EPS)
        o_ref[...] = (xv * r * g_ref[...]).astype(BF16)

    return pl.pallas_call(
        body, grid=(S // tm,), in_specs=[pl.BlockSpec((tm, Dm), lambda i: (i, 0)), pl.BlockSpec((1, Dm), lambda i: (0, 0))],
        out_specs=pl.BlockSpec((tm, Dm), lambda i: (i, 0)), out_shape=jax.ShapeDtypeStruct((S, Dm), BF16),
        compiler_params=_cp(("parallel",)), name=name)(x, g.reshape(1, Dm))


def _rmsnorm_bwd(dh, x, g, dres, *, name, tm=256):
    S, Dm = x.shape
    tm = min(tm, S)
    want_dx = dres is not None

    def body(*refs):
        if want_dx:
            dh_ref, x_ref, g_ref, dr_ref, dx_ref, dg_ref = refs
        else:
            dh_ref, x_ref, g_ref, dg_ref = refs
        i = pl.program_id(0)
        xv = x_ref[...]
        dhv = dh_ref[...]
        r = lax.rsqrt(jnp.mean(xv * xv, axis=-1, keepdims=True) + EPS)
        y = xv * r
        part = jnp.sum(dhv * y, axis=0, keepdims=True)

        @pl.when(i == 0)
        def _():
            dg_ref[...] = jnp.zeros_like(dg_ref)

        dg_ref[...] += part
        if want_dx:
            dy = dhv * g_ref[...]
            dx_ref[...] = dr_ref[...] + r * (dy - y * jnp.mean(dy * y, axis=-1, keepdims=True))

    row = pl.BlockSpec((tm, Dm), lambda i: (i, 0))
    vec = pl.BlockSpec((1, Dm), lambda i: (0, 0))
    if want_dx:
        dx, dg = pl.pallas_call(
            body, grid=(S // tm,), in_specs=[row, row, vec, row], out_specs=(row, vec),
            out_shape=(jax.ShapeDtypeStruct((S, Dm), F32), jax.ShapeDtypeStruct((1, Dm), F32)),
            compiler_params=_cp(("arbitrary",)), name=name)(dh, x, g.reshape(1, Dm), dres)
        return dx, dg
    dg = pl.pallas_call(
        body, grid=(S // tm,), in_specs=[row, row, vec], out_specs=vec,
        out_shape=jax.ShapeDtypeStruct((1, Dm), F32), compiler_params=_cp(("arbitrary",)), name=name)(dh, x, g.reshape(1, Dm))
    return None, dg


def _gate_fwd(cat, proj, z_off, *, name, tm=256, tn=1024):
    S = cat.shape[0]
    tm = min(tm, S)
    zb = z_off // tn

    def body(c_ref, z_ref, y_ref):
        z = z_ref[...]
        y_ref[...] = (c_ref[...] * z * _sigmoid(z)).astype(BF16)

    return pl.pallas_call(
        body, grid=(S // tm, INNER // tn),
        in_specs=[pl.BlockSpec((tm, tn), lambda i, j: (i, j)), pl.BlockSpec((tm, tn), lambda i, j: (i, zb + j))],
        out_specs=pl.BlockSpec((tm, tn), lambda i, j: (i, j)), out_shape=jax.ShapeDtypeStruct((S, INNER), BF16),
        compiler_params=_cp(("parallel", "parallel")), name=name)(cat, proj)


def _gate_bwd(dy, cat, proj, z_off, *, name, tm=256, tn=1024):
    S = cat.shape[0]
    tm = min(tm, S)
    zb = z_off // tn

    def body(dy_ref, c_ref, z_ref, dc_ref, dz_ref):
        z = z_ref[...]
        sg = _sigmoid(z)
        d = dy_ref[...]
        dc_ref[...] = d * z * sg
        dz_ref[...] = d * c_ref[...] * sg * (1.0 + z * (1.0 - sg))

    blk = pl.BlockSpec((tm, tn), lambda i, j: (i, j))
    return pl.pallas_call(
        body, grid=(S // tm, INNER // tn),
        in_specs=[blk, blk, pl.BlockSpec((tm, tn), lambda i, j: (i, zb + j))], out_specs=(blk, blk),
        out_shape=(jax.ShapeDtypeStruct((S, INNER), F32), jax.ShapeDtypeStruct((S, INNER), F32)),
        compiler_params=_cp(("parallel", "parallel")), name=name)(dy, cat, proj)


def _loss_head(x, target, *, name, tm=256):
    S, Dm = x.shape
    tm = min(tm, S)

    nt = S // tm

    def body(x_ref, t_ref, dx_ref, l_ref, acc):
        i = pl.program_id(0)
        e = x_ref[...] - t_ref[...]
        dx_ref[...] = e * (1.0 / Dm)

        @pl.when(i == 0)
        def _():
            acc[...] = jnp.zeros_like(acc)

        acc[...] += jnp.sum(e * e, axis=0, keepdims=True) * (0.5 / Dm)

        @pl.when(i == nt - 1)
        def _():
            l_ref[...] = jnp.sum(acc[...], axis=1, keepdims=True) + jnp.zeros((1, LANE), F32)

    row = pl.BlockSpec((tm, Dm), lambda i: (i, 0))
    return pl.pallas_call(
        body, grid=(nt,), in_specs=[row, row], out_specs=(row, pl.BlockSpec((1, LANE), lambda i: (0, 0))),
        out_shape=(jax.ShapeDtypeStruct((S, Dm), F32), jax.ShapeDtypeStruct((1, LANE), F32)),
        scratch_shapes=[pltpu.VMEM((1, Dm), F32)],
        compiler_params=_cp(("arbitrary",)), name=name)(x, target)


def _xa_norm(v, g):
    r = lax.rsqrt(jnp.mean(v * v, axis=-1, keepdims=True) + EPS)
    return v * r, r


def _xa_fwd(proj, xq_off, kv, gq, gk, *, name, tm=512):
    S = proj.shape[0]
    tm = min(tm, S)
    qb = xq_off // XA_DIM
    n_mem = kv.shape[0]
    scale = XA_DIM ** -0.5

    def body(q_ref, k_ref, v_ref, gq_ref, gk_ref, o_ref):
        qh, _ = _xa_norm(q_ref[...], None)
        kh, _ = _xa_norm(k_ref[...], None)
        qn = qh * gq_ref[...]
        kn = kh * gk_ref[...]
        s = _bdot(qn, kn, NT) * scale
        s = s - jnp.max(s, axis=-1, keepdims=True)
        p = jnp.exp(s)
        p = p / jnp.sum(p, axis=-1, keepdims=True)
        o_ref[...] = _bdot(p, v_ref[...], NN)

    vec = pl.BlockSpec((1, XA_DIM), lambda h, i: (0, 0))
    return pl.pallas_call(
        body, grid=(XA_HEADS, S // tm),
        in_specs=[pl.BlockSpec((tm, XA_DIM), lambda h, i: (i, qb + h)),
                  pl.BlockSpec((n_mem, XA_DIM), lambda h, i: (0, h)),
                  pl.BlockSpec((n_mem, XA_DIM), lambda h, i: (0, XA_HEADS + h)), vec, vec],
        out_specs=pl.BlockSpec((tm, XA_DIM), lambda h, i: (i, h)),
        out_shape=jax.ShapeDtypeStruct((S, XA_WIDTH), F32),
        compiler_params=_cp(("parallel", "parallel")), name=name)(proj, kv, kv, gq.reshape(1, XA_DIM), gk.reshape(1, XA_DIM))


def _xa_bwd(dxa, proj, xq_off, kv, gq, gk, *, name, tm=512):
    S = proj.shape[0]
    tm = min(tm, S)
    nt = S // tm
    qb = xq_off // XA_DIM
    n_mem = kv.shape[0]
    scale = XA_DIM ** -0.5

    def body(d_ref, q_ref, k_ref, v_ref, gq_ref, gk_ref, dq_ref, dk_ref, dv_ref, dgq_ref, dgk_ref, dkn_acc):
        h = pl.program_id(0)
        i = pl.program_id(1)
        q = q_ref[...]
        k = k_ref[...]
        qh, rq = _xa_norm(q, None)
        kh, rk = _xa_norm(k, None)
        gqv = gq_ref[...]
        gkv = gk_ref[...]
        qn = qh * gqv
        kn = kh * gkv
        s = _bdot(qn, kn, NT) * scale
        s = s - jnp.max(s, axis=-1, keepdims=True)
        p = jnp.exp(s)
        p = p / jnp.sum(p, axis=-1, keepdims=True)
        d = d_ref[...]
        dp = _bdot(d, v_ref[...], NT)
        ds = p * (dp - jnp.sum(dp * p, axis=-1, keepdims=True)) * scale
        dqn = _bdot(ds, kn, NN)

        @pl.when(i == 0)
        def _():
            dkn_acc[...] = jnp.zeros_like(dkn_acc)
            dv_ref[...] = jnp.zeros_like(dv_ref)

        @pl.when(jnp.logical_and(i == 0, h == 0))
        def _():
            dgq_ref[...] = jnp.zeros_like(dgq_ref)
            dgk_ref[...] = jnp.zeros_like(dgk_ref)

        dkn_acc[...] += _bdot(ds, qn, TN)
        dv_ref[...] += _bdot(p, d, TN)
        dgq_ref[...] += jnp.sum(dqn * qh, axis=0, keepdims=True)
        dy = dqn * gqv
        dq_ref[...] = rq * (dy - qh * jnp.mean(dy * qh, axis=-1, keepdims=True))

        @pl.when(i == nt - 1)
        def _():
            dkn = dkn_acc[...]
            dgk_ref[...] += jnp.sum(dkn * kh, axis=0, keepdims=True)
            dyk = dkn * gkv
            dk_ref[...] = rk * (dyk - kh * jnp.mean(dyk * kh, axis=-1, keepdims=True))

    vec = pl.BlockSpec((1, XA_DIM), lambda h, i: (0, 0))
    kblk = pl.BlockSpec((n_mem, XA_DIM), lambda h, i: (0, h))
    vblk = pl.BlockSpec((n_mem, XA_DIM), lambda h, i: (0, XA_HEADS + h))
    dq, dk, dv, dgq, dgk = pl.pallas_call(
        body, grid=(XA_HEADS, nt),
        in_specs=[pl.BlockSpec((tm, XA_DIM), lambda h, i: (i, h)),
                  pl.BlockSpec((tm, XA_DIM), lambda h, i: (i, qb + h)), kblk, vblk, vec, vec],
        out_specs=(pl.BlockSpec((tm, XA_DIM), lambda h, i: (i, h)), kblk, kblk, vec, vec),
        out_shape=(jax.ShapeDtypeStruct((S, XA_WIDTH), F32), jax.ShapeDtypeStruct((n_mem, XA_WIDTH), F32),
                   jax.ShapeDtypeStruct((n_mem, XA_WIDTH), F32), jax.ShapeDtypeStruct((1, XA_DIM), F32),
                   jax.ShapeDtypeStruct((1, XA_DIM), F32)),
        scratch_shapes=[pltpu.VMEM((n_mem, XA_DIM), F32)],
        compiler_params=_cp(("arbitrary", "arbitrary")), name=name)(
            dxa, proj, kv, kv, gq.reshape(1, XA_DIM), gk.reshape(1, XA_DIM))
    return dq, jnp.concatenate([dk, dv], axis=1), dgq, dgk


SB_TQ = 256
SB_TK = 256
SB_HEADS = 24


def _sb_tile(qi, kj, t0, s0, masked):
    z = _bdot(qi, kj, NT)
    sp = _softplus(z)
    ls = z - sp
    if not masked:
        return -sp, ls, None
    mask = (s0 + _iota2(z.shape, 1)) < (t0 + _iota2(z.shape, 0))
    return jnp.where(mask, -sp, 0.0), ls, mask


def _dot2(x, tri):
    hi = x.astype(BF16)
    lo = (x - hi.astype(F32)).astype(BF16)
    dims = (NN, ((), ()))
    return (lax.dot_general(hi, tri, dims, preferred_element_type=F32)
            + lax.dot_general(lo, tri, dims, preferred_element_type=F32))


def _sb_fwd(proj, gq, gk, *, name):
    S = proj.shape[0]
    tq, tk = min(SB_TQ, S), min(SB_TK, S)
    nq = S // tq
    scale = HEAD_DIM ** -0.5

    def body(q_ref, k_ref, v_ref, gq_ref, gk_ref, o_ref, tot_ref, qn_s, kn_s, v_s):
        q = q_ref[...]
        k = k_ref[...]
        qn_s[...] = (q * lax.rsqrt(jnp.mean(q * q, axis=-1, keepdims=True) + EPS) * (gq_ref[...] * scale)).astype(BF16)
        kn_s[...] = (k * lax.rsqrt(jnp.mean(k * k, axis=-1, keepdims=True) + EPS) * gk_ref[...]).astype(BF16)
        v_s[...] = v_ref[...].astype(BF16)
        after = (_iota2((tk, tk), 0) > _iota2((tk, tk), 1)).astype(BF16)

        def qblock(i, _):
            rows = pl.ds(pl.multiple_of(i * tq, tq), tq)
            qi = qn_s[rows, :]
            jd = (i * tq) // tk

            def tile(j, acc, run, masked):
                cols = pl.ds(pl.multiple_of(j * tk, tk), tk)
                lr, ls, mask = _sb_tile(qi, kn_s[cols, :], i * tq, j * tk, masked)
                later = _dot2(lr, after) + run
                a = jnp.exp(ls + later)
                if masked:
                    a = jnp.where(mask, a, 0.0)
                acc = acc + _bdot(a, v_s[cols, :], NN)
                return acc, run + jnp.sum(lr, axis=-1, keepdims=True)

            acc, run = tile(jd, jnp.zeros((tq, HEAD_DIM), F32), jnp.zeros((tq, 1), F32), True)
            acc, run = lax.fori_loop(0, jd, lambda jj, c: tile(jd - 1 - jj, c[0], c[1], False), (acc, run))
            o_ref[rows, :] = acc
            tot_ref[rows, :] = run + jnp.zeros((tq, HEAD_DIM), F32)
            return 0

        lax.fori_loop(0, nq, qblock, 0)

    vec = pl.BlockSpec((1, HEAD_DIM), lambda h: (0, 0))
    out = pl.BlockSpec((S, HEAD_DIM), lambda h: (0, h))
    return pl.pallas_call(
        body, grid=(SB_HEADS,),
        in_specs=[pl.BlockSpec((S, HEAD_DIM), lambda h: (0, h)), pl.BlockSpec((S, HEAD_DIM), lambda h: (0, SB_HEADS + h)),
                  pl.BlockSpec((S, HEAD_DIM), lambda h: (0, 2 * SB_HEADS + h)), vec, vec],
        out_specs=(out, out), out_shape=(jax.ShapeDtypeStruct((S, MIX_WIDTH), F32),) * 2,
        scratch_shapes=[pltpu.VMEM((S, HEAD_DIM), BF16)] * 3,
        compiler_params=_cp(("parallel",)), name=name)(proj, proj, proj, gq.reshape(1, HEAD_DIM), gk.reshape(1, HEAD_DIM))


def _sb_bwd(dmix, tot, proj, gq, gk, *, name):
    S = proj.shape[0]
    tq, tk = min(SB_TQ, S), min(SB_TK, S)
    nq = S // tq
    scale = HEAD_DIM ** -0.5

    def body(do_ref, o_ref, q_ref, k_ref, v_ref, gq_ref, gk_ref, dq_ref, dk_ref, dv_ref, dgq_ref, dgk_ref,
             qn_s, kn_s, v_s, dkn_s):
        h = pl.program_id(0)
        q = q_ref[...]
        k = k_ref[...]
        rq = lax.rsqrt(jnp.mean(q * q, axis=-1, keepdims=True) + EPS)
        rk = lax.rsqrt(jnp.mean(k * k, axis=-1, keepdims=True) + EPS)
        gqv = gq_ref[...]
        gkv = gk_ref[...]
        qn_s[...] = (q * rq * (gqv * scale)).astype(BF16)
        kn_s[...] = (k * rk * gkv).astype(BF16)
        v_s[...] = v_ref[...].astype(BF16)
        dkn_s[...] = jnp.zeros_like(dkn_s)
        dv_ref[...] = jnp.zeros_like(dv_ref)
        r_i = _iota2((tk, tk), 0)
        c_i = _iota2((tk, tk), 1)
        upto = (r_i <= c_i).astype(BF16)
        before = (r_i < c_i).astype(BF16)

        def qblock(i, _):
            rows = pl.ds(pl.multiple_of(i * tq, tq), tq)
            qi = qn_s[rows, :]
            doi = do_ref[rows, :].astype(BF16)
            tot_i = jnp.max(o_ref[rows, :], axis=-1, keepdims=True)
            jd = (i * tq) // tk

            def tile(j, dqn, run, run_b, masked):
                cols = pl.ds(pl.multiple_of(j * tk, tk), tk)
                kj = kn_s[cols, :]
                lr, ls, mask = _sb_tile(qi, kj, i * tq, j * tk, masked)
                later = tot_i - (_dot2(lr, upto) + run)
                a = jnp.exp(ls + later)
                if masked:
                    a = jnp.where(mask, a, 0.0)
                b = _bdot(doi, v_s[cols, :], NT) * a
                cum = _dot2(b, before) + run_b
                beta = jnp.exp(ls)
                dz = b * (1.0 - beta) - cum * beta
                if masked:
                    dz = jnp.where(mask, dz, 0.0)
                dzb = dz.astype(BF16)
                dv_ref[cols, :] += _bdot(a, doi, TN)
                dkn_s[cols, :] += _bdot(dzb, qi, TN)
                dqn = dqn + _bdot(dzb, kj, NN)
                return dqn, run + jnp.sum(lr, axis=-1, keepdims=True), run_b + jnp.sum(b, axis=-1, keepdims=True)

            zero1 = jnp.zeros((tq, 1), F32)
            carry = lax.fori_loop(0, jd, lambda j, c: tile(j, c[0], c[1], c[2], False),
                                  (jnp.zeros((tq, HEAD_DIM), F32), zero1, zero1))
            dqn, _, _ = tile(jd, carry[0], carry[1], carry[2], True)
            dq_ref[rows, :] = dqn * scale
            return 0

        lax.fori_loop(0, nq, qblock, 0)

        @pl.when(h == 0)
        def _():
            dgq_ref[...] = jnp.zeros_like(dgq_ref)
            dgk_ref[...] = jnp.zeros_like(dgk_ref)

        dqn = dq_ref[...]
        qh = q * rq
        dgq_ref[...] += jnp.sum(dqn * qh, axis=0, keepdims=True)
        dy = dqn * gqv
        dq_ref[...] = rq * (dy - qh * jnp.mean(dy * qh, axis=-1, keepdims=True))
        dkn = dkn_s[...]
        kh = k * rk
        dgk_ref[...] += jnp.sum(dkn * kh, axis=0, keepdims=True)
        dyk = dkn * gkv
        dk_ref[...] = rk * (dyk - kh * jnp.mean(dyk * kh, axis=-1, keepdims=True))

    vec = pl.BlockSpec((1, HEAD_DIM), lambda h: (0, 0))
    hb = lambda off: pl.BlockSpec((S, HEAD_DIM), lambda h: (0, off + h))
    dq, dk, dv, dgq, dgk = pl.pallas_call(
        body, grid=(SB_HEADS,),
        in_specs=[hb(0), hb(0), hb(0), hb(SB_HEADS), hb(2 * SB_HEADS), vec, vec],
        out_specs=(hb(0), hb(0), hb(0), vec, vec),
        out_shape=(jax.ShapeDtypeStruct((S, MIX_WIDTH), F32),) * 3 + (jax.ShapeDtypeStruct((1, HEAD_DIM), F32),) * 2,
        scratch_shapes=[pltpu.VMEM((S, HEAD_DIM), BF16)] * 3 + [pltpu.VMEM((S, HEAD_DIM), F32)],
        compiler_params=_cp(("arbitrary",)), name=name)(
            dmix, tot, proj, proj, proj, gq.reshape(1, HEAD_DIM), gk.reshape(1, HEAD_DIM))
    return jnp.concatenate([dq, dk, dv], axis=1), dgq, dgk


def _shift_down(x, k):
    if k == 0:
        return x
    r = pltpu.roll(x, k, 0)
    return jnp.where(_iota2(x.shape, 0) >= k, r, 0.0)


def _shift_up(x, k):
    if k == 0:
        return x
    n = x.shape[0]
    r = pltpu.roll(x, n - k, 0)
    return jnp.where(_iota2(x.shape, 0) < n - k, r, 0.0)


def _conv(x, w):
    c = w[DN_CONV - 1] * x
    for k in range(1, DN_CONV):
        c = c + w[DN_CONV - 1 - k] * _shift_down(x, k)
    return c


def _dn_pre_fwd(proj, conv_w, col0, ncols, *, l2, scale, name):
    S = proj.shape[0]
    cb = col0 // HEAD_DIM

    def body(x_ref, w_ref, o_ref):
        c = _conv(x_ref[...], [w_ref[k:k + 1, :] for k in range(DN_CONV)])
        a = c * _sigmoid(c)
        if l2:
            a = a * (lax.rsqrt(jnp.sum(a * a, axis=-1, keepdims=True) + EPS) * scale)
        o_ref[...] = a

    return pl.pallas_call(
        body, grid=(ncols // HEAD_DIM,),
        in_specs=[pl.BlockSpec((S, HEAD_DIM), lambda j: (0, cb + j)), pl.BlockSpec((DN_CONV, HEAD_DIM), lambda j: (0, cb + j))],
        out_specs=pl.BlockSpec((S, HEAD_DIM), lambda j: (0, j)), out_shape=jax.ShapeDtypeStruct((S, ncols), F32),
        compiler_params=_cp(("parallel",)), name=name)(proj, conv_w)


def _dn_pre_bwd(dout, proj, conv_w, col0, ncols, *, l2, scale, pair, name):
    S = proj.shape[0]
    cb = col0 // HEAD_DIM
    dw_in = 2 * HEAD_DIM if pair else HEAD_DIM

    def body(d_ref, x_ref, w_ref, dx_ref, dw_ref):
        x = x_ref[...]
        w = [w_ref[k:k + 1, :] for k in range(DN_CONV)]
        c = _conv(x, w)
        sg = _sigmoid(c)
        a = c * sg
        d = d_ref[...]
        if pair:
            d = d[:, :HEAD_DIM] + d[:, HEAD_DIM:]
        if l2:
            r = lax.rsqrt(jnp.sum(a * a, axis=-1, keepdims=True) + EPS)
            y = a * r
            d = d * scale
            d = r * (d - y * jnp.sum(d * y, axis=-1, keepdims=True))
        dc = d * sg * (1.0 + c * (1.0 - sg))
        dx = w[DN_CONV - 1] * dc
        for k in range(1, DN_CONV):
            dx = dx + w[DN_CONV - 1 - k] * _shift_up(dc, k)
        dx_ref[...] = dx
        for k in range(DN_CONV):
            dw_ref[3 - k:4 - k, :] = jnp.sum(dc * _shift_down(x, k), axis=0, keepdims=True)

    return pl.pallas_call(
        body, grid=(ncols // HEAD_DIM,),
        in_specs=[pl.BlockSpec((S, dw_in), lambda j: (0, j)), pl.BlockSpec((S, HEAD_DIM), lambda j: (0, cb + j)),
                  pl.BlockSpec((DN_CONV, HEAD_DIM), lambda j: (0, cb + j))],
        out_specs=(pl.BlockSpec((S, HEAD_DIM), lambda j: (0, j)), pl.BlockSpec((DN_CONV, HEAD_DIM), lambda j: (0, j))),
        out_shape=(jax.ShapeDtypeStruct((S, ncols), F32), jax.ShapeDtypeStruct((DN_CONV, ncols), F32)),
        compiler_params=_cp(("parallel",)), name=name)(dout, proj, conv_w)


def _dn_ab_fwd(proj, a_log, dt_bias, *, name, tm=512):
    S = proj.shape[0]
    tm = min(tm, S)
    ab = P0_AB // LANE

    def body(a_ref, b_ref, al_ref, dt_ref, g_ref, be_ref):
        g_ref[...] = -jnp.exp(al_ref[...]) * _softplus(a_ref[...] + dt_ref[...])
        be_ref[...] = _sigmoid(b_ref[...])

    vec = pl.BlockSpec((1, LANE), lambda i: (0, 0))
    out = pl.BlockSpec((tm, LANE), lambda i: (i, 0))
    return pl.pallas_call(
        body, grid=(S // tm,),
        in_specs=[pl.BlockSpec((tm, LANE), lambda i: (i, ab)), pl.BlockSpec((tm, LANE), lambda i: (i, ab + 1)), vec, vec],
        out_specs=(out, out), out_shape=(jax.ShapeDtypeStruct((S, LANE), F32),) * 2,
        compiler_params=_cp(("parallel",)), name=name)(proj, proj, a_log, dt_bias)


def _dn_ab_bwd(dg, dbeta, proj, a_log, dt_bias, *, name, tm=512):
    S = proj.shape[0]
    tm = min(tm, S)
    ab = P0_AB // LANE

    def body(dg_ref, db_ref, a_ref, b_ref, al_ref, dt_ref, dab_ref, dal_ref, ddt_ref):
        i = pl.program_id(0)
        ea = jnp.exp(al_ref[...])
        u = a_ref[...] + dt_ref[...]
        dgv = dg_ref[...]
        da = dgv * (-ea) * _sigmoid(u)
        be = _sigmoid(b_ref[...])
        dab_ref[:, 0:LANE] = da
        dab_ref[:, LANE:2 * LANE] = db_ref[...] * be * (1.0 - be)
        dab_ref[:, 2 * LANE:] = jnp.zeros((tm, 2 * LANE), F32)

        @pl.when(i == 0)
        def _():
            dal_ref[...] = jnp.zeros_like(dal_ref)
            ddt_ref[...] = jnp.zeros_like(ddt_ref)

        dal_ref[...] += jnp.sum(dgv * (-ea) * _softplus(u), axis=0, keepdims=True)
        ddt_ref[...] += jnp.sum(da, axis=0, keepdims=True)

    vec = pl.BlockSpec((1, LANE), lambda i: (0, 0))
    row = pl.BlockSpec((tm, LANE), lambda i: (i, 0))
    return pl.pallas_call(
        body, grid=(S // tm,),
        in_specs=[row, row, pl.BlockSpec((tm, LANE), lambda i: (i, ab)), pl.BlockSpec((tm, LANE), lambda i: (i, ab + 1)), vec, vec],
        out_specs=(pl.BlockSpec((tm, 4 * LANE), lambda i: (i, 0)), vec, vec),
        out_shape=(jax.ShapeDtypeStruct((S, 4 * LANE), F32), jax.ShapeDtypeStruct((1, LANE), F32),
                   jax.ShapeDtypeStruct((1, LANE), F32)),
        compiler_params=_cp(("arbitrary",)), name=name)(dg, dbeta, proj, proj, a_log, dt_bias)


def _tri_inverse(a):
    eye = (_iota2((CH, CH), 0) == _iota2((CH, CH), 1)).astype(F32)
    t = eye - a
    x = _fdot(a, a, NN)
    n = 2
    while True:
        t = t + _fdot(t, x, NN)
        n *= 2
        if n >= CH:
            break
        x = _fdot(x, x, NN)
    return t


def _pick_col(m, n):
    return jnp.sum(jnp.where(_iota2(m.shape, 1) == n, m, 0.0), axis=1, keepdims=True)


def _dn_chunk_common(q, k, v, gc_c, gc_r, be_c):
    r_i = _iota2((CH, CH), 0)
    c_i = _iota2((CH, CH), 1)
    incl = r_i >= c_i
    strict = r_i > c_i
    dec = jnp.exp(jnp.where(incl, gc_c - gc_r, -1e30))
    e = jnp.exp(gc_c)
    gl = jnp.sum(jnp.where(_iota2((1, CH), 1) == CH - 1, gc_r, 0.0), axis=1, keepdims=True)
    kds = jnp.exp(gl - gc_c)
    cd = jnp.exp(gl)
    kk = _bdot(k, k, NT)
    a = jnp.where(strict, be_c * kk * dec, 0.0)
    qk = _bdot(q, k, NT)
    p = qk * dec
    return dict(incl=incl, strict=strict, dec=dec, e=e, kds=kds, cd=cd, kk=kk, a=a, qk=qk, p=p)


def _dn_core_fwd(qn, kn, vc, g_rows, b_rows, out_g, *, name):
    S = qn.shape[0]
    nc = S // CH

    def body(q_ref, k_ref, v_ref, g_ref, b_ref, og_ref, o_ref, st_ref, t_ref, gcr, gcc, bcc):
        r_i = _iota2((CH, CH), 0)
        c_i = _iota2((CH, CH), 1)
        lc = (r_i >= c_i).astype(F32)
        eye = (r_i == c_i).astype(F32)
        g_rows_v = g_ref[0]
        gcr[...] = _fdot(g_rows_v, lc, NT)
        gcc[...] = _fdot(lc, g_rows_v, NT)
        bcc[...] = _fdot(eye, b_ref[0], NT)
        ogv = og_ref[...]

        def chunk(n, state):
            rows = pl.ds(pl.multiple_of(n * CH, CH), CH)
            q = q_ref[rows, :]
            k = k_ref[rows, :]
            v = v_ref[rows, :]
            gc_c = _pick_col(gcc[...], n)
            be_c = _pick_col(bcc[...], n)
            gc_r = gcr[pl.ds(n, 1), :]
            c = _dn_chunk_common(q, k, v, gc_c, gc_r, be_c)
            t = _tri_inverse(c["a"])
            u0 = _bdot(t, be_c * v, NN)
            w = _bdot(t, (be_c * c["e"]) * k, NN)
            u = u0 - _bdot(w, state, NN)
            o = _bdot(c["e"] * q, state, NN) + _bdot(c["p"], u, NN)
            st_ref[0, n] = state
            t_ref[0, n] = t
            o_ref[rows, :] = o * lax.rsqrt(jnp.mean(o * o, axis=-1, keepdims=True) + EPS) * ogv
            return c["cd"] * state + _bdot(c["kds"] * k, u, TN)

        lax.fori_loop(0, nc, chunk, jnp.zeros((HEAD_DIM, HEAD_DIM), F32))

    hb = lambda div: pl.BlockSpec((S, HEAD_DIM), lambda h: (0, h // div))
    rows_spec = pl.BlockSpec((1, LANE, CH), lambda h: (h, 0, 0))
    return pl.pallas_call(
        body, grid=(DN_V_HEADS,),
        in_specs=[hb(2), hb(2), hb(1), rows_spec, rows_spec, pl.BlockSpec((1, HEAD_DIM), lambda h: (0, 0))],
        out_specs=(hb(1), pl.BlockSpec((1, nc, HEAD_DIM, HEAD_DIM), lambda h: (h, 0, 0, 0)),
                   pl.BlockSpec((1, nc, CH, CH), lambda h: (h, 0, 0, 0))),
        out_shape=(jax.ShapeDtypeStruct((S, MIX_WIDTH), F32), jax.ShapeDtypeStruct((DN_V_HEADS, nc, HEAD_DIM, HEAD_DIM), F32),
                   jax.ShapeDtypeStruct((DN_V_HEADS, nc, CH, CH), F32)),
        scratch_shapes=[pltpu.VMEM((LANE, CH), F32), pltpu.VMEM((CH, LANE), F32), pltpu.VMEM((CH, LANE), F32)],
        compiler_params=_cp(("parallel",)), name=name)(qn, kn, vc, g_rows, b_rows, out_g.reshape(1, HEAD_DIM))


def _dn_core_bwd(dmix, qn, kn, vc, g_rows, b_rows, out_g, states, tinv, *, name):
    S = qn.shape[0]
    nc = S // CH

    def body(do_ref, q_ref, k_ref, v_ref, g_ref, b_ref, og_ref, st_ref, t_ref,
             dq_ref, dk_ref, dv_ref, dg_ref, db_ref, dog_ref, gcr, gcc, bcc, dgc_acc):
        h = pl.program_id(0)
        r_i = _iota2((CH, CH), 0)
        c_i = _iota2((CH, CH), 1)
        lc = (r_i >= c_i).astype(F32)
        eye = (r_i == c_i).astype(F32)
        ones = jnp.ones((CH, LANE), F32)
        g_rows_v = g_ref[0]
        gcr[...] = _fdot(g_rows_v, lc, NT)
        gcc[...] = _fdot(lc, g_rows_v, NT)
        bcc[...] = _fdot(eye, b_ref[0], NT)
        ogv = og_ref[...]
        dgc_acc[...] = jnp.zeros_like(dgc_acc)
        db_ref[...] = jnp.zeros_like(db_ref)
        lane_n = _iota2((CH, LANE), 1)
        last_row = _iota2((CH, 1), 0) == CH - 1

        @pl.when(h == 0)
        def _():
            dog_ref[...] = jnp.zeros_like(dog_ref)

        def chunk(m, carry):
            ds_next, dog = carry
            n = nc - 1 - m
            rows = pl.ds(pl.multiple_of(n * CH, CH), CH)
            q = q_ref[rows, :]
            k = k_ref[rows, :]
            v = v_ref[rows, :]
            state = st_ref[0, n]
            t = t_ref[0, n]
            gc_c = _pick_col(gcc[...], n)
            be_c = _pick_col(bcc[...], n)
            gc_r = gcr[pl.ds(n, 1), :]
            c = _dn_chunk_common(q, k, v, gc_c, gc_r, be_c)
            e, kds, cd, dec, a, p = c["e"], c["kds"], c["cd"], c["dec"], c["a"], c["p"]
            vb = be_c * v
            kbe = (be_c * e) * k
            u0 = _bdot(t, vb, NN)
            w = _bdot(t, kbe, NN)
            u = u0 - _bdot(w, state, NN)
            qd = e * q
            kd = kds * k
            o = _bdot(qd, state, NN) + _bdot(p, u, NN)
            r = lax.rsqrt(jnp.mean(o * o, axis=-1, keepdims=True) + EPS)
            y = o * r
            don = do_ref[rows, :]
            dog = dog + jnp.sum(don * y, axis=0, keepdims=True)
            dy = don * ogv
            d_o = r * (dy - y * jnp.mean(dy * y, axis=-1, keepdims=True))
            du = _bdot(p, d_o, TN) + _bdot(kd, ds_next, NN)
            dqd = _bdot(d_o, state, NT)
            dstate = _bdot(qd, d_o, TN) + cd * ds_next - _bdot(w, du, TN)
            dcd = jnp.sum(jnp.sum(ds_next * state, axis=1, keepdims=True), axis=0, keepdims=True)
            dkd = _bdot(u, ds_next, NT)
            dw = -_bdot(du, state, NT)
            dvb = _bdot(t, du, TN)
            dkbe = _bdot(t, dw, TN)
            da = -jnp.where(c["strict"], _bdot(dvb, u0, NT) + _bdot(dkbe, w, NT), 0.0)
            dp = jnp.where(c["incl"], _bdot(d_o, u, NT), 0.0)
            gmat = da * a + dp * p
            dad = da * dec
            x = be_c * dad
            dpd = dp * dec
            dk = _bdot(x, k, NN) + _bdot(x, k, TN) + _bdot(dpd, q, TN)
            dq = _bdot(dpd, k, NN) + e * dqd
            dbe = jnp.sum(dad * c["kk"], axis=1, keepdims=True)
            dgc = jnp.sum(gmat, axis=1, keepdims=True) + jnp.sum(dqd * q, axis=1, keepdims=True) * e
            rk = jnp.sum(dkd * k, axis=1, keepdims=True) * kds
            dk = dk + kds * dkd
            dgc = dgc - rk
            dgl = jnp.sum(rk, axis=0, keepdims=True) + dcd * cd
            sk = jnp.sum(dkbe * k, axis=1, keepdims=True)
            dk = dk + (be_c * e) * dkbe
            dbe = dbe + sk * e + jnp.sum(dvb * v, axis=1, keepdims=True)
            dgc = dgc + sk * be_c * e
            dgc = dgc + jnp.where(last_row, dgl, 0.0)
            dgc = dgc - _fdot(gmat, ones, TN)
            dq_ref[rows, :] = dq
            dk_ref[rows, :] = dk
            dv_ref[rows, :] = be_c * dvb
            dgc_acc[...] = jnp.where(lane_n == n, dgc, dgc_acc[...])
            db_ref[0] = jnp.where(lane_n == n, dbe, db_ref[0])
            return dstate, dog

        _, dog = lax.fori_loop(0, nc, chunk, (jnp.zeros((HEAD_DIM, HEAD_DIM), F32), jnp.zeros((1, HEAD_DIM), F32)))
        dog_ref[...] += dog
        dg_ref[0] = _fdot(lc, dgc_acc[...], TN)

    hb = lambda div: pl.BlockSpec((S, HEAD_DIM), lambda h: (0, h // div))
    rows_spec = pl.BlockSpec((1, LANE, CH), lambda h: (h, 0, 0))
    cols_spec = pl.BlockSpec((1, CH, LANE), lambda h: (h, 0, 0))
    vec = pl.BlockSpec((1, HEAD_DIM), lambda h: (0, 0))
    big = lambda: jax.ShapeDtypeStruct((S, MIX_WIDTH), F32)
    return pl.pallas_call(
        body, grid=(DN_V_HEADS,),
        in_specs=[hb(1), hb(2), hb(2), hb(1), rows_spec, rows_spec, vec,
                  pl.BlockSpec((1, nc, HEAD_DIM, HEAD_DIM), lambda h: (h, 0, 0, 0)),
                  pl.BlockSpec((1, nc, CH, CH), lambda h: (h, 0, 0, 0))],
        out_specs=(hb(1), hb(1), hb(1), cols_spec, cols_spec, vec),
        out_shape=(big(), big(), big(), jax.ShapeDtypeStruct((DN_V_HEADS, CH, LANE), F32),
                   jax.ShapeDtypeStruct((DN_V_HEADS, CH, LANE), F32), jax.ShapeDtypeStruct((1, HEAD_DIM), F32)),
        scratch_shapes=[pltpu.VMEM((LANE, CH), F32), pltpu.VMEM((CH, LANE), F32), pltpu.VMEM((CH, LANE), F32),
                        pltpu.VMEM((CH, LANE), F32)],
        compiler_params=_cp(("arbitrary",)), name=name)(
            dmix, qn, kn, vc, g_rows, b_rows, out_g.reshape(1, HEAD_DIM), states, tinv)


def _rows_form(x, nc):
    t = x[:, :DN_V_HEADS].T.reshape(DN_V_HEADS, nc, CH)
    return jnp.pad(t, ((0, 0), (0, LANE - nc), (0, 0)))


def _cols_to_nat(x, nc):
    t = jnp.transpose(x[:, :, :nc], (2, 1, 0)).reshape(nc * CH, DN_V_HEADS)
    return jnp.pad(t, ((0, 0), (0, LANE - DN_V_HEADS)))


_C_QKV = 2 * DN_QK_WIDTH + MIX_WIDTH


def _w0_to_padded(w):
    rows = w.shape[0]
    z = lambda n: jnp.zeros((rows, n), w.dtype)
    a = w[:, _C_QKV:_C_QKV + DN_V_HEADS]
    b = w[:, _C_QKV + DN_V_HEADS:_C_QKV + 2 * DN_V_HEADS]
    return jnp.concatenate([w[:, :_C_QKV], w[:, _C_QKV + 2 * DN_V_HEADS:], a, z(LANE - DN_V_HEADS), b,
                            z(P0 - P0_AB - LANE - DN_V_HEADS)], axis=1)


def _w0_from_padded(g):
    return jnp.concatenate([g[:, :_C_QKV], g[:, P0_AB:P0_AB + DN_V_HEADS], g[:, P0_AB + LANE:P0_AB + LANE + DN_V_HEADS],
                            g[:, _C_QKV:P0_AB]], axis=1)


def _pad_lane(v):
    v = v.reshape(1, -1)
    return jnp.pad(v, ((0, 0), (0, LANE - v.shape[1])))


def _local_step(x, mem, target, norm_g, mem_norm_g, w_kv, xa_q_g, xa_k_g, w_out, w_in0, conv_w, a_log, dt_bias, out_g,
                w_in1, sb_q_g, sb_k_g):
    S = x.shape[0]
    nc = S // CH
    al = _pad_lane(a_log)
    dtb = _pad_lane(dt_bias)
    q_scale = HEAD_DIM ** -0.5

    mem_n = _rmsnorm_fwd(mem, mem_norm_g, name="mem_norm")
    kv = [_matmul(mem_n, w_kv[i], out_dtype=F32, name=f"kv{i}") for i in range(2)]

    h0 = _rmsnorm_fwd(x, norm_g[0], name="norm0")
    proj0 = _matmul(h0, w_in0, name="proj0")
    qn = _dn_pre_fwd(proj0, conv_w, 0, DN_QK_WIDTH, l2=True, scale=q_scale, name="dn_pre_q")
    kn = _dn_pre_fwd(proj0, conv_w, DN_QK_WIDTH, DN_QK_WIDTH, l2=True, scale=1.0, name="dn_pre_k")
    vc = _dn_pre_fwd(proj0, conv_w, 2 * DN_QK_WIDTH, MIX_WIDTH, l2=False, scale=1.0, name="dn_pre_v")
    g_nat, b_nat = _dn_ab_fwd(proj0, al, dtb, name="dn_ab")
    g_rows = _rows_form(g_nat, nc)
    b_rows = _rows_form(b_nat, nc)
    mix0, states, tinv = _dn_core_fwd(qn, kn, vc, g_rows, b_rows, out_g, name="dn_core")
    xa0 = _xa_fwd(proj0, P0_XQ, kv[0], xa_q_g[0], xa_k_g[0], name="xa0")
    cat0 = jnp.concatenate([mix0, xa0], axis=1)
    y0 = _gate_fwd(cat0, proj0, P0_Z, name="gate0")
    x1 = _matmul(y0, w_out[0], res=x, name="out0")

    h1 = _rmsnorm_fwd(x1, norm_g[1], name="norm1")
    proj1 = _matmul(h1, w_in1, name="proj1")
    mix1, tot1 = _sb_fwd(proj1, sb_q_g, sb_k_g, name="sb")
    xa1 = _xa_fwd(proj1, P1_XQ, kv[1], xa_q_g[1], xa_k_g[1], name="xa1")
    cat1 = jnp.concatenate([mix1, xa1], axis=1)
    y1 = _gate_fwd(cat1, proj1, P1_Z, name="gate1")
    x2 = _matmul(y1, w_out[1], res=x1, name="out1")

    dx2, loss_vec = _loss_head(x2, target, name="loss")

    d_wout1 = _matmul(y1, dx2, ta=True, name="d_wout1")
    dy1 = _matmul(dx2, w_out[1], tb=True, name="dy1")
    dcat1, dz1 = _gate_bwd(dy1, cat1, proj1, P1_Z, name="gate1_bwd")
    dqkv1, d_sbq, d_sbk = _sb_bwd(dcat1[:, :MIX_WIDTH], tot1, proj1, sb_q_g, sb_k_g, name="sb_bwd")
    dxq1, dkv1, d_xaq1, d_xak1 = _xa_bwd(dcat1[:, MIX_WIDTH:], proj1, P1_XQ, kv[1], xa_q_g[1], xa_k_g[1], name="xa1_bwd")
    dproj1 = jnp.concatenate([dqkv1, dxq1, dz1], axis=1)
    d_win1 = _matmul(h1, dproj1, ta=True, name="d_win1")
    dh1 = _matmul(dproj1, w_in1, tb=True, name="dh1")
    dx1, d_ng1 = _rmsnorm_bwd(dh1, x1, norm_g[1], dx2, name="norm1_bwd")

    d_wout0 = _matmul(y0, dx1, ta=True, name="d_wout0")
    dy0 = _matmul(dx1, w_out[0], tb=True, name="dy0")
    dcat0, dz0 = _gate_bwd(dy0, cat0, proj0, P0_Z, name="gate0_bwd")
    dqv, dkv_h, dvc, dg_cols, db_cols, d_outg = _dn_core_bwd(
        dcat0[:, :MIX_WIDTH], qn, kn, vc, g_rows, b_rows, out_g, states, tinv, name="dn_core_bwd")
    dpq, dwq = _dn_pre_bwd(dqv, proj0, conv_w, 0, DN_QK_WIDTH, l2=True, scale=q_scale, pair=True, name="dn_pre_q_bwd")
    dpk, dwk = _dn_pre_bwd(dkv_h, proj0, conv_w, DN_QK_WIDTH, DN_QK_WIDTH, l2=True, scale=1.0, pair=True, name="dn_pre_k_bwd")
    dpv, dwv = _dn_pre_bwd(dvc, proj0, conv_w, 2 * DN_QK_WIDTH, MIX_WIDTH, l2=False, scale=1.0, pair=False, name="dn_pre_v_bwd")
    dab, d_alog, d_dt = _dn_ab_bwd(_cols_to_nat(dg_cols, nc), _cols_to_nat(db_cols, nc), proj0, al, dtb, name="dn_ab_bwd")
    dxq0, dkv0, d_xaq0, d_xak0 = _xa_bwd(dcat0[:, MIX_WIDTH:], proj0, P0_XQ, kv[0], xa_q_g[0], xa_k_g[0], name="xa0_bwd")
    dproj0 = jnp.concatenate([dpq, dpk, dpv, dxq0, dz0, dab], axis=1)
    d_win0 = _matmul(h0, dproj0, ta=True, name="d_win0")
    dh0 = _matmul(dproj0, w_in0, tb=True, name="dh0")
    dx0, d_ng0 = _rmsnorm_bwd(dh0, x, norm_g[0], dx1, name="norm0_bwd")

    d_wkv = [_matmul(mem_n, d, ta=True, name=f"d_wkv{i}") for i, d in enumerate((dkv0, dkv1))]
    dmem_n = _matmul(dkv1, w_kv[1], tb=True, res=_matmul(dkv0, w_kv[0], tb=True, name="dmem0"), name="dmem1")
    _, d_memg = _rmsnorm_bwd(dmem_n, mem, mem_norm_g, None, name="mem_norm_bwd")

    grads = dict(
        norm_g=jnp.concatenate([d_ng0, d_ng1], axis=0), mem_norm_g=d_memg.reshape(-1), mem_w_kv=jnp.stack(d_wkv),
        xa_q_norm_g=jnp.concatenate([d_xaq0, d_xaq1], axis=0), xa_k_norm_g=jnp.concatenate([d_xak0, d_xak1], axis=0),
        w_out=jnp.stack([d_wout0, d_wout1]), dn_w_in=d_win0, dn_conv_w=jnp.concatenate([dwq, dwk, dwv], axis=1),
        dn_a_log=d_alog[:, :DN_V_HEADS], dn_dt_bias=d_dt[:, :DN_V_HEADS], dn_out_norm_g=d_outg, sb_w_in=d_win1,
        sb_q_norm_g=d_sbq, sb_k_norm_g=d_sbk)
    return loss_vec, dx0, grads


ANY = pl.BlockSpec(memory_space=pl.ANY)


def _place():
    x, y, c = lax.axis_index("x"), lax.axis_index("y"), lax.axis_index("c")
    chips = [(1 - x, y), (x, 1 - y), (1 - x, 1 - y)]
    return x, y, c, 2 * x + y, (x, y, 1 - c), chips


def _rcopy(src, dst, send, recv, i, dev):
    return pltpu.make_async_remote_copy(src_ref=src, dst_ref=dst, send_sem=send.at[i], recv_sem=recv.at[i],
                                        device_id=dev, device_id_type=MESH)


def _gather_weights(srcs, *, name):
    nt = len(srcs)

    def body(*refs):
        src, dst = refs[:nt], refs[nt:2 * nt]
        send, recv = refs[2 * nt:]
        x, y, c, j, sib, chips = _place()
        sends = []
        for t in range(nt):
            for k, (cx, cy) in enumerate(chips):
                sends.append(_rcopy(src[t].at[c], dst[t].at[j, c], send, recv, 6 * t + k, (cx, cy, c)))
                sends[-1].start()
        for t in range(nt):
            for k, (cx, cy) in enumerate(chips):
                landed = dst[t].at[2 * cx + cy, c]
                _rcopy(landed, landed, send, recv, 6 * t + k, (cx, cy, c)).wait_recv()
                sends.append(_rcopy(landed, landed, send, recv, 6 * t + 3 + k, sib))
                sends[-1].start()
        for t in range(nt):
            for k, (cx, cy) in enumerate(chips):
                other = dst[t].at[2 * cx + cy, 1 - c]
                _rcopy(other, other, send, recv, 6 * t + 3 + k, sib).wait_recv()
        for cp in sends:
            cp.wait_send()

    return pl.pallas_call(
        body, in_specs=[ANY] * nt, out_specs=[ANY] * nt,
        out_shape=[jax.ShapeDtypeStruct((N_CHIPS,) + s.shape, s.dtype) for s in srcs],
        scratch_shapes=[pltpu.SemaphoreType.DMA((6 * nt,)), pltpu.SemaphoreType.DMA((6 * nt,))],
        name=name)(*srcs)


def _swap_halves(xs, *, name):
    nt = len(xs)

    def body(*refs):
        src, dst = refs[:nt], refs[nt:2 * nt]
        send, recv = refs[2 * nt:]
        x, y, c, j, sib, chips = _place()
        cps = []
        for t in range(nt):
            for s in range(N_CHIPS):
                cps.append(_rcopy(src[t].at[s, 1 - c], dst[t].at[s], send, recv, 4 * t + s, sib))
                cps[-1].start()
        for cp in cps:
            cp.wait_recv()
        for cp in cps:
            cp.wait_send()

    return pl.pallas_call(
        body, in_specs=[ANY] * nt, out_specs=[ANY] * nt,
        out_shape=[jax.ShapeDtypeStruct((N_CHIPS,) + a.shape[2:], a.dtype) for a in xs],
        scratch_shapes=[pltpu.SemaphoreType.DMA((4 * nt,)), pltpu.SemaphoreType.DMA((4 * nt,))], name=name)(*xs)


def _scatter_to_chips(ps, *, name):
    nt = len(ps)

    def body(*refs):
        src, dst = refs[:nt], refs[nt:2 * nt]
        send, recv = refs[2 * nt:]
        x, y, c, j, sib, chips = _place()
        cps = []
        for t in range(nt):
            for k, (cx, cy) in enumerate(chips):
                cps.append(_rcopy(src[t].at[2 * cx + cy], dst[t].at[k], send, recv, 3 * t + k, (cx, cy, c)))
                cps[-1].start()
        for cp in cps:
            cp.wait_recv()
        for cp in cps:
            cp.wait_send()

    return pl.pallas_call(
        body, in_specs=[ANY] * nt, out_specs=[ANY] * nt,
        out_shape=[jax.ShapeDtypeStruct((3,) + a.shape[1:], a.dtype) for a in ps],
        scratch_shapes=[pltpu.SemaphoreType.DMA((3 * nt,)), pltpu.SemaphoreType.DMA((3 * nt,))], name=name)(*ps)


def _swap_with_sibling(fs, *, name):
    nt = len(fs)

    def body(*refs):
        src, dst = refs[:nt], refs[nt:2 * nt]
        send, recv = refs[2 * nt:]
        x, y, c, j, sib, chips = _place()
        cps = [_rcopy(src[t], dst[t], send, recv, t, sib) for t in range(nt)]
        for cp in cps:
            cp.start()
        for cp in cps:
            cp.wait_recv()
        for cp in cps:
            cp.wait_send()

    return pl.pallas_call(
        body, in_specs=[ANY] * nt, out_specs=[ANY] * nt,
        out_shape=[jax.ShapeDtypeStruct(a.shape, a.dtype) for a in fs],
        scratch_shapes=[pltpu.SemaphoreType.DMA((nt,)), pltpu.SemaphoreType.DMA((nt,))], name=name)(*fs)


def _all_reduce_small(pack, *, name):
    rows = pack.shape[0]

    def body(p_ref, o_ref, buf, send, recv):
        x, y, c = lax.axis_index("x"), lax.axis_index("y"), lax.axis_index("c")
        me = 4 * x + 2 * y + c
        buf[me] = p_ref[...]
        cps = []
        for r in range(1, 8):
            dev = (x ^ (r >> 2), y ^ ((r >> 1) & 1), c ^ (r & 1))
            cps.append(_rcopy(p_ref, buf.at[me], send, recv, r - 1, dev))
            cps[-1].start()
        for r in range(1, 8):
            frm = buf.at[me ^ r]
            _rcopy(frm, frm, send, recv, r - 1, (x, y, c)).wait_recv()
        for cp in cps:
            cp.wait_send()
        acc = buf[0]
        for d in range(1, 8):
            acc = acc + buf[d]
        o_ref[...] = acc

    vm = pl.BlockSpec(memory_space=pltpu.VMEM)
    return pl.pallas_call(
        body, in_specs=[vm], out_specs=vm, out_shape=jax.ShapeDtypeStruct(pack.shape, F32),
        scratch_shapes=[pltpu.VMEM((8, rows, LANE), F32), pltpu.SemaphoreType.DMA((7,)), pltpu.SemaphoreType.DMA((7,))],
        name=name)(pack)


def _add_halves(x, b, c_idx, *, name, tr=256):
    _, _, R, C = x.shape
    tr = min(tr, R)

    def body(c_ref, x_ref, b_ref, o_ref):
        o_ref[...] = (x_ref[...].astype(F32) + b_ref[...].astype(F32)).astype(o_ref.dtype)

    return pl.pallas_call(
        body,
        grid_spec=pltpu.PrefetchScalarGridSpec(
            num_scalar_prefetch=1, grid=(N_CHIPS, R // tr),
            in_specs=[pl.BlockSpec((None, None, tr, C), lambda s, i, c_ref: (s, c_ref[0], i, 0)),
                      pl.BlockSpec((None, tr, C), lambda s, i, c_ref: (s, i, 0))],
            out_specs=pl.BlockSpec((None, tr, C), lambda s, i, c_ref: (s, i, 0))),
        out_shape=jax.ShapeDtypeStruct(b.shape, b.dtype), compiler_params=_cp(("parallel", "parallel")), name=name)(c_idx, x, b)


def _sum_slot(p, rcv, j_idx, *, name, tr=256):
    _, R, C = p.shape
    tr = min(tr, R)

    def body(j_ref, p_ref, r_ref, o_ref):
        acc = p_ref[...].astype(F32)
        for k in range(3):
            acc = acc + r_ref[k].astype(F32)
        o_ref[...] = acc

    return pl.pallas_call(
        body,
        grid_spec=pltpu.PrefetchScalarGridSpec(
            num_scalar_prefetch=1, grid=(R // tr,),
            in_specs=[pl.BlockSpec((None, tr, C), lambda i, j_ref: (j_ref[0], i, 0)),
                      pl.BlockSpec((3, tr, C), lambda i, j_ref: (0, i, 0))],
            out_specs=pl.BlockSpec((tr, C), lambda i, j_ref: (i, 0))),
        out_shape=jax.ShapeDtypeStruct((R, C), F32), compiler_params=_cp(("parallel",)), name=name)(j_idx, p, rcv)


def _adamw_math(w, g, m, v):
    nm = ADAM_B1 * m + (1.0 - ADAM_B1) * g
    nv = ADAM_B2 * v + (1.0 - ADAM_B2) * (g * g)
    m_hat = nm / (1.0 - ADAM_B1 ** ADAM_STEP)
    v_hat = nv / (1.0 - ADAM_B2 ** ADAM_STEP)
    return -ADAM_LR * (m_hat / (jnp.sqrt(v_hat) + ADAM_EPS) + ADAM_WD * w), nm, nv


def _adamw_halves(w, g_mine, g_theirs, m, v, c_idx, *, name, tr=128):
    _, R, C = w.shape
    tr = tr if R % tr == 0 else R

    def body(c_ref, w_ref, gm_ref, gt_ref, m_ref, v_ref, g_ref, d_ref, nm_ref, nv_ref):
        gv = jnp.where(pl.program_id(0) == c_ref[0], gm_ref[...], gt_ref[...])
        d, nm, nv = _adamw_math(w_ref[...], gv, m_ref[...], v_ref[...])
        g_ref[...] = gv
        d_ref[...] = d
        nm_ref[...] = nm
        nv_ref[...] = nv

    full = pl.BlockSpec((None, tr, C), lambda hh, i, c_ref: (hh, i, 0))
    half = pl.BlockSpec((tr, C), lambda hh, i, c_ref: (i, 0))
    sh = jax.ShapeDtypeStruct(w.shape, F32)
    return pl.pallas_call(
        body,
        grid_spec=pltpu.PrefetchScalarGridSpec(num_scalar_prefetch=1, grid=(2, R // tr),
                                               in_specs=[full, half, half, full, full], out_specs=(full,) * 4),
        out_shape=(sh,) * 4, compiler_params=_cp(("parallel", "parallel")), name=name)(c_idx, w, g_mine, g_theirs, m, v)


def _adamw(w, g, m, v, *, name, tr=128):
    R, C = w.shape
    tr = tr if R % tr == 0 else R

    def body(w_ref, g_ref, m_ref, v_ref, d_ref, nm_ref, nv_ref):
        d_ref[...], nm_ref[...], nv_ref[...] = _adamw_math(w_ref[...], g_ref[...], m_ref[...], v_ref[...])

    blk = pl.BlockSpec((tr, C), lambda i: (i, 0))
    sh = jax.ShapeDtypeStruct((R, C), F32)
    return pl.pallas_call(body, grid=(R // tr,), in_specs=[blk] * 4, out_specs=(blk,) * 3, out_shape=(sh,) * 3,
                          compiler_params=_cp(("parallel",)), name=name)(w, g, m, v)


_SMALL = ["norm_g", "mem_norm_g", "xa_q_norm_g", "xa_k_norm_g", "dn_a_log", "dn_dt_bias", "dn_out_norm_g",
          "sb_q_norm_g", "sb_k_norm_g"]


def _pack(parts):
    rows, metas, r0 = [], [], 0
    for p in parts:
        flat = p.reshape(-1).astype(F32)
        n = flat.shape[0]
        nr = -(-n // (8 * LANE)) * 8
        rows.append(jnp.pad(flat, (0, nr * LANE - n)).reshape(nr, LANE))
        metas.append((r0, nr, n, p.shape))
        r0 += nr
    return jnp.concatenate(rows, axis=0), metas


def _unpack(pack, metas):
    return [pack[r0:r0 + nr].reshape(-1)[:n].reshape(shape) for r0, nr, n, shape in metas]


def kernel(x, mem, norm_g, mem_norm_g, mem_w_kv, xa_q_norm_g, xa_k_norm_g, w_out, dn_w_in, dn_conv_w, dn_a_log, dn_dt_bias, dn_out_norm_g, sb_w_in, sb_q_norm_g, sb_k_norm_g, loss_target, m_norm_g, m_mem_norm_g, m_mem_w_kv, m_xa_q_norm_g, m_xa_k_norm_g, m_w_out, m_dn_w_in, m_dn_conv_w, m_dn_a_log, m_dn_dt_bias, m_dn_out_norm_g, m_sb_w_in, m_sb_q_norm_g, m_sb_k_norm_g, v_norm_g, v_mem_norm_g, v_mem_w_kv, v_xa_q_norm_g, v_xa_k_norm_g, v_w_out, v_dn_w_in, v_dn_conv_w, v_dn_a_log, v_dn_dt_bias, v_dn_out_norm_g, v_sb_w_in, v_sb_q_norm_g, v_sb_k_norm_g):
    W = dict(norm_g=norm_g, mem_norm_g=mem_norm_g, mem_w_kv=mem_w_kv, xa_q_norm_g=xa_q_norm_g, xa_k_norm_g=xa_k_norm_g,
             w_out=w_out, dn_w_in=dn_w_in, dn_conv_w=dn_conv_w, dn_a_log=dn_a_log, dn_dt_bias=dn_dt_bias,
             dn_out_norm_g=dn_out_norm_g, sb_w_in=sb_w_in, sb_q_norm_g=sb_q_norm_g, sb_k_norm_g=sb_k_norm_g)
    M = dict(norm_g=m_norm_g, mem_norm_g=m_mem_norm_g, mem_w_kv=m_mem_w_kv, xa_q_norm_g=m_xa_q_norm_g,
             xa_k_norm_g=m_xa_k_norm_g, w_out=m_w_out, dn_w_in=m_dn_w_in, dn_conv_w=m_dn_conv_w, dn_a_log=m_dn_a_log,
             dn_dt_bias=m_dn_dt_bias, dn_out_norm_g=m_dn_out_norm_g, sb_w_in=m_sb_w_in, sb_q_norm_g=m_sb_q_norm_g,
             sb_k_norm_g=m_sb_k_norm_g)
    V = dict(norm_g=v_norm_g, mem_norm_g=v_mem_norm_g, mem_w_kv=v_mem_w_kv, xa_q_norm_g=v_xa_q_norm_g,
             xa_k_norm_g=v_xa_k_norm_g, w_out=v_w_out, dn_w_in=v_dn_w_in, dn_conv_w=v_dn_conv_w, dn_a_log=v_dn_a_log,
             dn_dt_bias=v_dn_dt_bias, dn_out_norm_g=v_dn_out_norm_g, sb_w_in=v_sb_w_in, sb_q_norm_g=v_sb_q_norm_g,
             sb_k_norm_g=v_sb_k_norm_g)
    names = ["norm_g", "mem_norm_g", "mem_w_kv", "xa_q_norm_g", "xa_k_norm_g", "w_out", "dn_w_in", "dn_conv_w",
             "dn_a_log", "dn_dt_bias", "dn_out_norm_g", "sb_w_in", "sb_q_norm_g", "sb_k_norm_g"]
    cx, cy, cc = lax.axis_index("x"), lax.axis_index("y"), lax.axis_index("c")
    slot = 2 * cx + cy
    half_r = D_MODEL // 2
    conv_cols = dn_conv_w.shape[2]

    w0s = jnp.pad(dn_w_in[0].astype(BF16), ((0, 0), (0, P0_SHARD_PAD - P0_SHARD))).reshape(2, half_r, P0_SHARD_PAD)
    w1s = sb_w_in[0].astype(BF16).reshape(2, half_r, SB_PROJ // N_CHIPS)
    convs = jnp.pad(dn_conv_w[0], ((0, 8 - DN_CONV), (0, 0))).reshape(8, 2, conv_cols // 2).transpose(1, 0, 2)
    own = [w0s, w1s, w_out.astype(BF16), mem_w_kv.astype(BF16), convs]
    gathered = _gather_weights(own, name="gather_weights")
    mine = (jnp.arange(N_CHIPS) == slot).reshape(N_CHIPS, 1, 1, 1)
    g0, g1, gout, gkv, gconv = [jnp.where(mine, o[None], g) for o, g in zip(own, gathered)]
    w0_true = g0.reshape(N_CHIPS, D_MODEL, P0_SHARD_PAD)[:, :, :P0_SHARD].transpose(1, 0, 2).reshape(D_MODEL, DN_PROJ)
    w_in0 = _w0_to_padded(w0_true)
    w_in1 = g1.reshape(N_CHIPS, D_MODEL, SB_PROJ // N_CHIPS).transpose(1, 0, 2).reshape(D_MODEL, SB_PROJ)
    w_out_f = gout.transpose(1, 0, 2, 3).reshape(2, INNER, D_MODEL)
    w_kv_f = gkv.transpose(1, 0, 2, 3).reshape(2, D_MODEL, 2 * XA_WIDTH)
    conv_f = gconv.transpose(2, 0, 1, 3).reshape(8, N_CHIPS * conv_cols)[:DN_CONV]

    loss_vec, grad_x, g = _local_step(
        x[0], mem[0], loss_target[0], norm_g, mem_norm_g, w_kv_f, xa_q_norm_g, xa_k_norm_g, w_out_f, w_in0, conv_f,
        dn_a_log[0], dn_dt_bias[0], dn_out_norm_g[0], w_in1, sb_q_norm_g[0], sb_k_norm_g[0])

    d0 = _w0_from_padded(g["dn_w_in"]).reshape(2, half_r, N_CHIPS, P0_SHARD)
    d0 = jnp.pad(d0, ((0, 0), (0, 0), (0, 0), (0, P0_SHARD_PAD - P0_SHARD))).transpose(2, 0, 1, 3).astype(BF16)
    d1 = g["sb_w_in"].reshape(2, half_r, N_CHIPS, SB_PROJ // N_CHIPS).transpose(2, 0, 1, 3).astype(BF16)
    dout = g["w_out"].reshape(2, N_CHIPS, INNER // N_CHIPS, D_MODEL).transpose(1, 0, 2, 3).astype(BF16)
    dkv = g["mem_w_kv"].reshape(2, N_CHIPS, D_MODEL // N_CHIPS, 2 * XA_WIDTH).transpose(1, 0, 2, 3).astype(BF16)
    xs = [d0, d1, dout, dkv]
    c_idx = jnp.reshape(cc, (1,)).astype(jnp.int32)
    j_idx = jnp.reshape(slot, (1,)).astype(jnp.int32)
    from_sib = _swap_halves(xs, name="rs_swap")
    ps = [_add_halves(a, b, c_idx, name=f"rs_add{t}") for t, (a, b) in enumerate(zip(xs, from_sib))]
    rcvs = _scatter_to_chips(ps, name="rs_scatter")
    fs = [_sum_slot(p, r, j_idx, name=f"rs_sum{t}") for t, (p, r) in enumerate(zip(ps, rcvs))]
    fs[0] = fs[0][:, :P0_SHARD]
    theirs = _swap_with_sibling(fs, name="rs_join")
    big_names = ["dn_w_in", "sb_w_in", "w_out", "mem_w_kv"]

    pack, metas = _pack([g[n] for n in _SMALL] + [g["dn_conv_w"], loss_vec])
    red = _unpack(_all_reduce_small(pack, name="all_reduce_small"), metas)
    small_grads = {n: r.reshape(W[n].shape) for n, r in zip(_SMALL, red)}
    small_grads["dn_conv_w"] = lax.dynamic_slice_in_dim(red[len(_SMALL)], slot * conv_cols, conv_cols, axis=1).reshape(W["dn_conv_w"].shape)
    loss = red[-1][0, 0]

    out_g, out_d, out_m, out_v = {}, {}, {}, {}
    for n, mine_g, their_g in zip(big_names, fs, theirs):
        shp = W[n].shape
        h3 = (2,) + mine_g.shape
        outs = _adamw_halves(W[n].reshape(h3), mine_g, their_g, M[n].reshape(h3), V[n].reshape(h3), c_idx, name=f"adamw_{n}")
        out_g[n], out_d[n], out_m[n], out_v[n] = [o.reshape(shp) for o in outs]
    small_names = _SMALL + ["dn_conv_w"]
    wp, sm = _pack([W[n] for n in small_names])
    gp, _ = _pack([small_grads[n] for n in small_names])
    mp, _ = _pack([M[n] for n in small_names])
    vp, _ = _pack([V[n] for n in small_names])
    dp, nmp, nvp = _adamw(wp, gp, mp, vp, name="adamw_small")
    for n, d, nm, nv in zip(small_names, _unpack(dp, sm), _unpack(nmp, sm), _unpack(nvp, sm)):
        out_g[n], out_d[n], out_m[n], out_v[n] = small_grads[n], d, nm, nv

    return (loss, grad_x[None], *[out_g[n] for n in names], *[out_d[n] for n in names], *[out_m[n] for n in names],
            *[out_v[n] for n in names])
```

```python
import functools
import math

import jax
import jax.numpy as jnp
from jax import lax
from jax.experimental import pallas as pl
from jax.experimental.pallas import tpu as pltpu

F32 = jnp.float32
BF16 = jnp.bfloat16
HI = lax.Precision.HIGHEST
MESH = pl.DeviceIdType.MESH

D_MODEL = 2048
INNER = 4096
XA_WIDTH = 1024
XA_HEADS = 4
XA_DIM = 256
MIX_WIDTH = 3072
HEAD_DIM = 128
DN_V_HEADS = 24
DN_QK_WIDTH = 1536
DN_CONV = 4
DN_PROJ = 11312
SB_PROJ = 14336
EPS = 1e-6
N_CHIPS = 4

CH = 128
LANE = 128

P0_XQ = 6144
P0_Z = 7168
P0_AB = 11264
P0 = 11776
P0_SHARD = DN_PROJ // N_CHIPS
P0_SHARD_PAD = 2944
P1_XQ = 9216
P1_Z = 10240
P1 = SB_PROJ

ADAM_LR = 0.001
ADAM_B1 = 0.9
ADAM_B2 = 0.999
ADAM_EPS = 1e-08
ADAM_WD = 0.01
ADAM_STEP = 10

VMEM_LIMIT = 48 * 1024 * 1024


def _cp(sem=None, **kw):
    return pltpu.CompilerParams(dimension_semantics=sem, vmem_limit_bytes=VMEM_LIMIT, **kw)


def _bdot(a, b, dims):
    return lax.dot_general(a.astype(BF16), b.astype(BF16), (dims, ((), ())), preferred_element_type=F32)


def _fdot(a, b, dims):
    return lax.dot_general(a, b, (dims, ((), ())), precision=HI, preferred_element_type=F32)


NN = ((1,), (0,))
NT = ((1,), (1,))
TN = ((0,), (0,))


def _sigmoid(x):
    return 1.0 / (1.0 + jnp.exp(-x))


def _softplus(x):
    return jnp.maximum(x, 0.0) + jnp.log(1.0 + jnp.exp(-jnp.abs(x)))


def _iota2(shape, axis):
    return lax.broadcasted_iota(jnp.int32, shape, axis)


MM_FULL_K = 4096
MM_BLOCK_BYTES = 4 * 1024 * 1024


def _matmul(a, b, *, ta=False, tb=False, out_dtype=F32, res=None, name):
    M = a.shape[1] if ta else a.shape[0]
    K = a.shape[0] if ta else a.shape[1]
    N = b.shape[0] if tb else b.shape[1]
    dims = ((0,) if ta else (1,), (1,) if tb else (0,))
    has_res = res is not None
    if K <= MM_FULL_K:
        tm = min(M, 1024, max(256, MM_BLOCK_BYTES // (K * a.dtype.itemsize)))
        tn = min(N, 512)
        assert M % tm == 0 and N % tn == 0, (name, M, N, K, tm, tn)

        def body_full(*refs):
            if has_res:
                a_ref, b_ref, r_ref, o_ref = refs
            else:
                a_ref, b_ref, o_ref = refs
            r = _bdot(a_ref[...], b_ref[...], dims)
            if has_res:
                r = r + r_ref[...]
            o_ref[...] = r.astype(out_dtype)

        a_spec = pl.BlockSpec((K, tm), lambda i, j: (0, i)) if ta else pl.BlockSpec((tm, K), lambda i, j: (i, 0))
        b_spec = pl.BlockSpec((tn, K), lambda i, j: (j, 0)) if tb else pl.BlockSpec((K, tn), lambda i, j: (0, j))
        o_spec = pl.BlockSpec((tm, tn), lambda i, j: (i, j))
        return pl.pallas_call(
            body_full, grid=(M // tm, N // tn), in_specs=[a_spec, b_spec] + ([o_spec] if has_res else []), out_specs=o_spec,
            out_shape=jax.ShapeDtypeStruct((M, N), out_dtype),
            compiler_params=_cp(("parallel", "parallel")), name=name)(*((a, b) + ((res,) if has_res else ())))

    tm, tn = min(M, 1024), min(N, 1024)
    tk = 1024 if K % 1024 == 0 else 512
    assert M % tm == 0 and N % tn == 0 and K % tk == 0, (name, M, N, K, tm, tn, tk)
    nk = K // tk

    def body(*refs):
        if has_res:
            a_ref, b_ref, r_ref, o_ref, acc = refs
        else:
            a_ref, b_ref, o_ref, acc = refs
        k = pl.program_id(2)

        @pl.when(k == 0)
        def _():
            acc[...] = jnp.zeros_like(acc)

        acc[...] += _bdot(a_ref[...], b_ref[...], dims)

        @pl.when(k == nk - 1)
        def _():
            r = acc[...]
            if has_res:
                r = r + r_ref[...]
            o_ref[...] = r.astype(out_dtype)

    a_spec = pl.BlockSpec((tk, tm), lambda i, j, k: (k, i)) if ta else pl.BlockSpec((tm, tk), lambda i, j, k: (i, k))
    b_spec = pl.BlockSpec((tn, tk), lambda i, j, k: (j, k)) if tb else pl.BlockSpec((tk, tn), lambda i, j, k: (k, j))
    o_spec = pl.BlockSpec((tm, tn), lambda i, j, k: (i, j))
    in_specs = [a_spec, b_spec] + ([o_spec] if has_res else [])
    args = (a, b) + ((res,) if has_res else ())
    return pl.pallas_call(
        body, grid=(M // tm, N // tn, nk), in_specs=in_specs, out_specs=o_spec,
        out_shape=jax.ShapeDtypeStruct((M, N), out_dtype), scratch_shapes=[pltpu.VMEM((tm, tn), F32)],
        compiler_params=_cp(("parallel", "parallel", "arbitrary")), name=name)(*args)


def _rmsnorm_fwd(x, g, *, name, tm=256):
    S, Dm = x.shape
    tm = min(tm, S)

    def body(x_ref, g_ref, o_ref):
        xv = x_ref[...]
        r = lax.rsqrt(jnp.mean(xv * xv, axis=-1, keepdims=True) + EPS)
        o_ref[...] = (xv * r * g_ref[...]).astype(BF16)

    return pl.pallas_call(
        body, grid=(S // tm,), in_specs=[pl.BlockSpec((tm, Dm), lambda i: (i, 0)), pl.BlockSpec((1, Dm), lambda i: (0, 0))],
        out_specs=pl.BlockSpec((tm, Dm), lambda i: (i, 0)), out_shape=jax.ShapeDtypeStruct((S, Dm), BF16),
        compiler_params=_cp(("parallel",)), name=name)(x, g.reshape(1, Dm))


def _rmsnorm_bwd(dh, x, g, dres, *, name, tm=256):
    S, Dm = x.shape
    tm = min(tm, S)
    want_dx = dres is not None

    def body(*refs):
        if want_dx:
            dh_ref, x_ref, g_ref, dr_ref, dx_ref, dg_ref = refs
        else:
            dh_ref, x_ref, g_ref, dg_ref = refs
        i = pl.program_id(0)
        xv = x_ref[...]
        dhv = dh_ref[...]
        r = lax.rsqrt(jnp.mean(xv * xv, axis=-1, keepdims=True) + EPS)
        y = xv * r
        part = jnp.sum(dhv * y, axis=0, keepdims=True)

        @pl.when(i == 0)
        def _():
            dg_ref[...] = jnp.zeros_like(dg_ref)

        dg_ref[...] += part
        if want_dx:
            dy = dhv * g_ref[...]
            dx_ref[...] = dr_ref[...] + r * (dy - y * jnp.mean(dy * y, axis=-1, keepdims=True))

    row = pl.BlockSpec((tm, Dm), lambda i: (i, 0))
    vec = pl.BlockSpec((1, Dm), lambda i: (0, 0))
    if want_dx:
        dx, dg = pl.pallas_call(
            body, grid=(S // tm,), in_specs=[row, row, vec, row], out_specs=(row, vec),
            out_shape=(jax.ShapeDtypeStruct((S, Dm), F32), jax.ShapeDtypeStruct((1, Dm), F32)),
            compiler_params=_cp(("arbitrary",)), name=name)(dh, x, g.reshape(1, Dm), dres)
        return dx, dg
    dg = pl.pallas_call(
        body, grid=(S // tm,), in_specs=[row, row, vec], out_specs=vec,
        out_shape=jax.ShapeDtypeStruct((1, Dm), F32), compiler_params=_cp(("arbitrary",)), name=name)(dh, x, g.reshape(1, Dm))
    return None, dg


def _gate_fwd(cat, proj, z_off, *, name, tm=256, tn=1024):
    S = cat.shape[0]
    tm = min(tm, S)
    zb = z_off // tn

    def body(c_ref, z_ref, y_ref):
        z = z_ref[...]
        y_ref[...] = (c_ref[...] * z * _sigmoid(z)).astype(BF16)

    return pl.pallas_call(
        body, grid=(S // tm, INNER // tn),
        in_specs=[pl.BlockSpec((tm, tn), lambda i, j: (i, j)), pl.BlockSpec((tm, tn), lambda i, j: (i, zb + j))],
        out_specs=pl.BlockSpec((tm, tn), lambda i, j: (i, j)), out_shape=jax.ShapeDtypeStruct((S, INNER), BF16),
        compiler_params=_cp(("parallel", "parallel")), name=name)(cat, proj)


def _gate_bwd(dy, cat, proj, z_off, *, name, tm=256, tn=1024):
    S = cat.shape[0]
    tm = min(tm, S)
    zb = z_off // tn

    def body(dy_ref, c_ref, z_ref, dc_ref, dz_ref):
        z = z_ref[...]
        sg = _sigmoid(z)
        d = dy_ref[...]
        dc_ref[...] = d * z * sg
        dz_ref[...] = d * c_ref[...] * sg * (1.0 + z * (1.0 - sg))

    blk = pl.BlockSpec((tm, tn), lambda i, j: (i, j))
    return pl.pallas_call(
        body, grid=(S // tm, INNER // tn),
        in_specs=[blk, blk, pl.BlockSpec((tm, tn), lambda i, j: (i, zb + j))], out_specs=(blk, blk),
        out_shape=(jax.ShapeDtypeStruct((S, INNER), F32), jax.ShapeDtypeStruct((S, INNER), F32)),
        compiler_params=_cp(("parallel", "parallel")), name=name)(dy, cat, proj)


def _loss_head(x, target, *, name, tm=256):
    S, Dm = x.shape
    tm = min(tm, S)

    nt = S // tm

    def body(x_ref, t_ref, dx_ref, l_ref, acc):
        i = pl.program_id(0)
        e = x_ref[...] - t_ref[...]
        dx_ref[...] = e * (1.0 / Dm)

        @pl.when(i == 0)
        def _():
            acc[...] = jnp.zeros_like(acc)

        acc[...] += jnp.sum(e * e, axis=0, keepdims=True) * (0.5 / Dm)

        @pl.when(i == nt - 1)
        def _():
            l_ref[...] = jnp.sum(acc[...], axis=1, keepdims=True) + jnp.zeros((1, LANE), F32)

    row = pl.BlockSpec((tm, Dm), lambda i: (i, 0))
    return pl.pallas_call(
        body, grid=(nt,), in_specs=[row, row], out_specs=(row, pl.BlockSpec((1, LANE), lambda i: (0, 0))),
        out_shape=(jax.ShapeDtypeStruct((S, Dm), F32), jax.ShapeDtypeStruct((1, LANE), F32)),
        scratch_shapes=[pltpu.VMEM((1, Dm), F32)],
        compiler_params=_cp(("arbitrary",)), name=name)(x, target)


def _xa_norm(v, g):
    r = lax.rsqrt(jnp.mean(v * v, axis=-1, keepdims=True) + EPS)
    return v * r, r


def _xa_fwd(proj, xq_off, kv, gq, gk, *, name, tm=512):
    S = proj.shape[0]
    tm = min(tm, S)
    qb = xq_off // XA_DIM
    n_mem = kv.shape[0]
    scale = XA_DIM ** -0.5

    def body(q_ref, k_ref, v_ref, gq_ref, gk_ref, o_ref):
        qh, _ = _xa_norm(q_ref[...], None)
        kh, _ = _xa_norm(k_ref[...], None)
        qn = qh * gq_ref[...]
        kn = kh * gk_ref[...]
        s = _bdot(qn, kn, NT) * scale
        s = s - jnp.max(s, axis=-1, keepdims=True)
        p = jnp.exp(s)
        p = p / jnp.sum(p, axis=-1, keepdims=True)
        o_ref[...] = _bdot(p, v_ref[...], NN)

    vec = pl.BlockSpec((1, XA_DIM), lambda h, i: (0, 0))
    return pl.pallas_call(
        body, grid=(XA_HEADS, S // tm),
        in_specs=[pl.BlockSpec((tm, XA_DIM), lambda h, i: (i, qb + h)),
                  pl.BlockSpec((n_mem, XA_DIM), lambda h, i: (0, h)),
                  pl.BlockSpec((n_mem, XA_DIM), lambda h, i: (0, XA_HEADS + h)), vec, vec],
        out_specs=pl.BlockSpec((tm, XA_DIM), lambda h, i: (i, h)),
        out_shape=jax.ShapeDtypeStruct((S, XA_WIDTH), F32),
        compiler_params=_cp(("parallel", "parallel")), name=name)(proj, kv, kv, gq.reshape(1, XA_DIM), gk.reshape(1, XA_DIM))


def _xa_bwd(dxa, proj, xq_off, kv, gq, gk, *, name, tm=512):
    S = proj.shape[0]
    tm = min(tm, S)
    nt = S // tm
    qb = xq_off // XA_DIM
    n_mem = kv.shape[0]
    scale = XA_DIM ** -0.5

    def body(d_ref, q_ref, k_ref, v_ref, gq_ref, gk_ref, dq_ref, dk_ref, dv_ref, dgq_ref, dgk_ref, dkn_acc):
        h = pl.program_id(0)
        i = pl.program_id(1)
        q = q_ref[...]
        k = k_ref[...]
        qh, rq = _xa_norm(q, None)
        kh, rk = _xa_norm(k, None)
        gqv = gq_ref[...]
        gkv = gk_ref[...]
        qn = qh * gqv
        kn = kh * gkv
        s = _bdot(qn, kn, NT) * scale
        s = s - jnp.max(s, axis=-1, keepdims=True)
        p = jnp.exp(s)
        p = p / jnp.sum(p, axis=-1, keepdims=True)
        d = d_ref[...]
        dp = _bdot(d, v_ref[...], NT)
        ds = p * (dp - jnp.sum(dp * p, axis=-1, keepdims=True)) * scale
        dqn = _bdot(ds, kn, NN)

        @pl.when(i == 0)
        def _():
            dkn_acc[...] = jnp.zeros_like(dkn_acc)
            dv_ref[...] = jnp.zeros_like(dv_ref)

        @pl.when(jnp.logical_and(i == 0, h == 0))
        def _():
            dgq_ref[...] = jnp.zeros_like(dgq_ref)
            dgk_ref[...] = jnp.zeros_like(dgk_ref)

        dkn_acc[...] += _bdot(ds, qn, TN)
        dv_ref[...] += _bdot(p, d, TN)
        dgq_ref[...] += jnp.sum(dqn * qh, axis=0, keepdims=True)
        dy = dqn * gqv
        dq_ref[...] = rq * (dy - qh * jnp.mean(dy * qh, axis=-1, keepdims=True))

        @pl.when(i == nt - 1)
        def _():
            dkn = dkn_acc[...]
            dgk_ref[...] += jnp.sum(dkn * kh, axis=0, keepdims=True)
            dyk = dkn * gkv
            dk_ref[...] = rk * (dyk - kh * jnp.mean(dyk * kh, axis=-1, keepdims=True))

    vec = pl.BlockSpec((1, XA_DIM), lambda h, i: (0, 0))
    kblk = pl.BlockSpec((n_mem, XA_DIM), lambda h, i: (0, h))
    vblk = pl.BlockSpec((n_mem, XA_DIM), lambda h, i: (0, XA_HEADS + h))
    dq, dk, dv, dgq, dgk = pl.pallas_call(
        body, grid=(XA_HEADS, nt),
        in_specs=[pl.BlockSpec((tm, XA_DIM), lambda h, i: (i, h)),
                  pl.BlockSpec((tm, XA_DIM), lambda h, i: (i, qb + h)), kblk, vblk, vec, vec],
        out_specs=(pl.BlockSpec((tm, XA_DIM), lambda h, i: (i, h)), kblk, kblk, vec, vec),
        out_shape=(jax.ShapeDtypeStruct((S, XA_WIDTH), F32), jax.ShapeDtypeStruct((n_mem, XA_WIDTH), F32),
                   jax.ShapeDtypeStruct((n_mem, XA_WIDTH), F32), jax.ShapeDtypeStruct((1, XA_DIM), F32),
                   jax.ShapeDtypeStruct((1, XA_DIM), F32)),
        scratch_shapes=[pltpu.VMEM((n_mem, XA_DIM), F32)],
        compiler_params=_cp(("arbitrary", "arbitrary")), name=name)(
            dxa, proj, kv, kv, gq.reshape(1, XA_DIM), gk.reshape(1, XA_DIM))
    return dq, jnp.concatenate([dk, dv], axis=1), dgq, dgk


SB_TQ = 256
SB_TK = 256
SB_HEADS = 24


def _sb_tile(qi, kj, t0, s0, masked):
    z = _bdot(qi, kj, NT)
    sp = _softplus(z)
    ls = z - sp
    if not masked:
        return -sp, ls, None
    mask = (s0 + _iota2(z.shape, 1)) < (t0 + _iota2(z.shape, 0))
    return jnp.where(mask, -sp, 0.0), ls, mask


def _dot2(x, tri):
    hi = x.astype(BF16)
    lo = (x - hi.astype(F32)).astype(BF16)
    dims = (NN, ((), ()))
    return (lax.dot_general(hi, tri, dims, preferred_element_type=F32)
            + lax.dot_general(lo, tri, dims, preferred_element_type=F32))


def _sb_fwd(proj, gq, gk, *, name):
    S = proj.shape[0]
    tq, tk = min(SB_TQ, S), min(SB_TK, S)
    nq = S // tq
    scale = HEAD_DIM ** -0.5

    def body(q_ref, k_ref, v_ref, gq_ref, gk_ref, o_ref, tot_ref, qn_s, kn_s, v_s):
        q = q_ref[...]
        k = k_ref[...]
        qn_s[...] = (q * lax.rsqrt(jnp.mean(q * q, axis=-1, keepdims=True) + EPS) * (gq_ref[...] * scale)).astype(BF16)
        kn_s[...] = (k * lax.rsqrt(jnp.mean(k * k, axis=-1, keepdims=True) + EPS) * gk_ref[...]).astype(BF16)
        v_s[...] = v_ref[...].astype(BF16)
        after = (_iota2((tk, tk), 0) > _iota2((tk, tk), 1)).astype(BF16)

        def qblock(i, _):
            rows = pl.ds(pl.multiple_of(i * tq, tq), tq)
            qi = qn_s[rows, :]
            jd = (i * tq) // tk

            def tile(j, acc, run, masked):
                cols = pl.ds(pl.multiple_of(j * tk, tk), tk)
                lr, ls, mask = _sb_tile(qi, kn_s[cols, :], i * tq, j * tk, masked)
                later = _dot2(lr, after) + run
                a = jnp.exp(ls + later)
                if masked:
                    a = jnp.where(mask, a, 0.0)
                acc = acc + _bdot(a, v_s[cols, :], NN)
                return acc, run + jnp.sum(lr, axis=-1, keepdims=True)

            acc, run = tile(jd, jnp.zeros((tq, HEAD_DIM), F32), jnp.zeros((tq, 1), F32), True)
            acc, run = lax.fori_loop(0, jd, lambda jj, c: tile(jd - 1 - jj, c[0], c[1], False), (acc, run))
            o_ref[rows, :] = acc
            tot_ref[rows, :] = run + jnp.zeros((tq, HEAD_DIM), F32)
            return 0

        lax.fori_loop(0, nq, qblock, 0)

    vec = pl.BlockSpec((1, HEAD_DIM), lambda h: (0, 0))
    out = pl.BlockSpec((S, HEAD_DIM), lambda h: (0, h))
    return pl.pallas_call(
        body, grid=(SB_HEADS,),
        in_specs=[pl.BlockSpec((S, HEAD_DIM), lambda h: (0, h)), pl.BlockSpec((S, HEAD_DIM), lambda h: (0, SB_HEADS + h)),
                  pl.BlockSpec((S, HEAD_DIM), lambda h: (0, 2 * SB_HEADS + h)), vec, vec],
        out_specs=(out, out), out_shape=(jax.ShapeDtypeStruct((S, MIX_WIDTH), F32),) * 2,
        scratch_shapes=[pltpu.VMEM((S, HEAD_DIM), BF16)] * 3,
        compiler_params=_cp(("parallel",)), name=name)(proj, proj, proj, gq.reshape(1, HEAD_DIM), gk.reshape(1, HEAD_DIM))


def _sb_bwd(dmix, tot, proj, gq, gk, *, name):
    S = proj.shape[0]
    tq, tk = min(SB_TQ, S), min(SB_TK, S)
    nq = S // tq
    scale = HEAD_DIM ** -0.5

    def body(do_ref, o_ref, q_ref, k_ref, v_ref, gq_ref, gk_ref, dq_ref, dk_ref, dv_ref, dgq_ref, dgk_ref,
             qn_s, kn_s, v_s, dkn_s):
        h = pl.program_id(0)
        q = q_ref[...]
        k = k_ref[...]
        rq = lax.rsqrt(jnp.mean(q * q, axis=-1, keepdims=True) + EPS)
        rk = lax.rsqrt(jnp.mean(k * k, axis=-1, keepdims=True) + EPS)
        gqv = gq_ref[...]
        gkv = gk_ref[...]
        qn_s[...] = (q * rq * (gqv * scale)).astype(BF16)
        kn_s[...] = (k * rk * gkv).astype(BF16)
        v_s[...] = v_ref[...].astype(BF16)
        dkn_s[...] = jnp.zeros_like(dkn_s)
        dv_ref[...] = jnp.zeros_like(dv_ref)
        r_i = _iota2((tk, tk), 0)
        c_i = _iota2((tk, tk), 1)
        upto = (r_i <= c_i).astype(BF16)
        before = (r_i < c_i).astype(BF16)

        def qblock(i, _):
            rows = pl.ds(pl.multiple_of(i * tq, tq), tq)
            qi = qn_s[rows, :]
            doi = do_ref[rows, :].astype(BF16)
            tot_i = jnp.max(o_ref[rows, :], axis=-1, keepdims=True)
            jd = (i * tq) // tk

            def tile(j, dqn, run, run_b, masked):
                cols = pl.ds(pl.multiple_of(j * tk, tk), tk)
                kj = kn_s[cols, :]
                lr, ls, mask = _sb_tile(qi, kj, i * tq, j * tk, masked)
                later = tot_i - (_dot2(lr, upto) + run)
                a = jnp.exp(ls + later)
                if masked:
                    a = jnp.where(mask, a, 0.0)
                b = _bdot(doi, v_s[cols, :], NT) * a
                cum = _dot2(b, before) + run_b
                beta = jnp.exp(ls)
                dz = b * (1.0 - beta) - cum * beta
                if masked:
                    dz = jnp.where(mask, dz, 0.0)
                dzb = dz.astype(BF16)
                dv_ref[cols, :] += _bdot(a, doi, TN)
                dkn_s[cols, :] += _bdot(dzb, qi, TN)
                dqn = dqn + _bdot(dzb, kj, NN)
                return dqn, run + jnp.sum(lr, axis=-1, keepdims=True), run_b + jnp.sum(b, axis=-1, keepdims=True)

            zero1 = jnp.zeros((tq, 1), F32)
            carry = lax.fori_loop(0, jd, lambda j, c: tile(j, c[0], c[1], c[2], False),
                                  (jnp.zeros((tq, HEAD_DIM), F32), zero1, zero1))
            dqn, _, _ = tile(jd, carry[0], carry[1], carry[2], True)
            dq_ref[rows, :] = dqn * scale
            return 0

        lax.fori_loop(0, nq, qblock, 0)

        @pl.when(h == 0)
        def _():
            dgq_ref[...] = jnp.zeros_like(dgq_ref)
            dgk_ref[...] = jnp.zeros_like(dgk_ref)

        dqn = dq_ref[...]
        qh = q * rq
        dgq_ref[...] += jnp.sum(dqn * qh, axis=0, keepdims=True)
        dy = dqn * gqv
        dq_ref[...] = rq * (dy - qh * jnp.mean(dy * qh, axis=-1, keepdims=True))
        dkn = dkn_s[...]
        kh = k * rk
        dgk_ref[...] += jnp.sum(dkn * kh, axis=0, keepdims=True)
        dyk = dkn * gkv
        dk_ref[...] = rk * (dyk - kh * jnp.mean(dyk * kh, axis=-1, keepdims=True))

    vec = pl.BlockSpec((1, HEAD_DIM), lambda h: (0, 0))
    hb = lambda off: pl.BlockSpec((S, HEAD_DIM), lambda h: (0, off + h))
    dq, dk, dv, dgq, dgk = pl.pallas_call(
        body, grid=(SB_HEADS,),
        in_specs=[hb(0), hb(0), hb(0), hb(SB_HEADS), hb(2 * SB_HEADS), vec, vec],
        out_specs=(hb(0), hb(0), hb(0), vec, vec),
        out_shape=(jax.ShapeDtypeStruct((S, MIX_WIDTH), F32),) * 3 + (jax.ShapeDtypeStruct((1, HEAD_DIM), F32),) * 2,
        scratch_shapes=[pltpu.VMEM((S, HEAD_DIM), BF16)] * 3 + [pltpu.VMEM((S, HEAD_DIM), F32)],
        compiler_params=_cp(("arbitrary",)), name=name)(
            dmix, tot, proj, proj, proj, gq.reshape(1, HEAD_DIM), gk.reshape(1, HEAD_DIM))
    return jnp.concatenate([dq, dk, dv], axis=1), dgq, dgk


def _shift_down(x, k):
    if k == 0:
        return x
    r = pltpu.roll(x, k, 0)
    return jnp.where(_iota2(x.shape, 0) >= k, r, 0.0)


def _shift_up(x, k):
    if k == 0:
        return x
    n = x.shape[0]
    r = pltpu.roll(x, n - k, 0)
    return jnp.where(_iota2(x.shape, 0) < n - k, r, 0.0)


def _conv(x, w):
    c = w[DN_CONV - 1] * x
    for k in range(1, DN_CONV):
        c = c + w[DN_CONV - 1 - k] * _shift_down(x, k)
    return c


def _dn_pre_fwd(proj, conv_w, col0, ncols, *, l2, scale, name):
    S = proj.shape[0]
    cb = col0 // HEAD_DIM

    def body(x_ref, w_ref, o_ref):
        c = _conv(x_ref[...], [w_ref[k:k + 1, :] for k in range(DN_CONV)])
        a = c * _sigmoid(c)
        if l2:
            a = a * (lax.rsqrt(jnp.sum(a * a, axis=-1, keepdims=True) + EPS) * scale)
        o_ref[...] = a

    return pl.pallas_call(
        body, grid=(ncols // HEAD_DIM,),
        in_specs=[pl.BlockSpec((S, HEAD_DIM), lambda j: (0, cb + j)), pl.BlockSpec((DN_CONV, HEAD_DIM), lambda j: (0, cb + j))],
        out_specs=pl.BlockSpec((S, HEAD_DIM), lambda j: (0, j)), out_shape=jax.ShapeDtypeStruct((S, ncols), F32),
        compiler_params=_cp(("parallel",)), name=name)(proj, conv_w)


def _dn_pre_bwd(dout, proj, conv_w, col0, ncols, *, l2, scale, name):
    S = proj.shape[0]
    cb = col0 // HEAD_DIM
    dw_in = HEAD_DIM

    def body(d_ref, x_ref, w_ref, dx_ref, dw_ref):
        x = x_ref[...]
        w = [w_ref[k:k + 1, :] for k in range(DN_CONV)]
        c = _conv(x, w)
        sg = _sigmoid(c)
        a = c * sg
        d = d_ref[...]
        if l2:
            r = lax.rsqrt(jnp.sum(a * a, axis=-1, keepdims=True) + EPS)
            y = a * r
            d = d * scale
            d = r * (d - y * jnp.sum(d * y, axis=-1, keepdims=True))
        dc = d * sg * (1.0 + c * (1.0 - sg))
        dx = w[DN_CONV - 1] * dc
        for k in range(1, DN_CONV):
            dx = dx + w[DN_CONV - 1 - k] * _shift_up(dc, k)
        dx_ref[...] = dx
        for k in range(DN_CONV):
            dw_ref[3 - k:4 - k, :] = jnp.sum(dc * _shift_down(x, k), axis=0, keepdims=True)

    return pl.pallas_call(
        body, grid=(ncols // HEAD_DIM,),
        in_specs=[pl.BlockSpec((S, dw_in), lambda j: (0, j)), pl.BlockSpec((S, HEAD_DIM), lambda j: (0, cb + j)),
                  pl.BlockSpec((DN_CONV, HEAD_DIM), lambda j: (0, cb + j))],
        out_specs=(pl.BlockSpec((S, HEAD_DIM), lambda j: (0, j)), pl.BlockSpec((DN_CONV, HEAD_DIM), lambda j: (0, j))),
        out_shape=(jax.ShapeDtypeStruct((S, ncols), F32), jax.ShapeDtypeStruct((DN_CONV, ncols), F32)),
        compiler_params=_cp(("parallel",)), name=name)(dout, proj, conv_w)


def _dn_ab_fwd(proj, a_log, dt_bias, *, name, tm=512):
    S = proj.shape[0]
    tm = min(tm, S)
    ab = P0_AB // LANE

    def body(a_ref, b_ref, al_ref, dt_ref, g_ref, be_ref):
        g_ref[...] = -jnp.exp(al_ref[...]) * _softplus(a_ref[...] + dt_ref[...])
        be_ref[...] = _sigmoid(b_ref[...])

    vec = pl.BlockSpec((1, LANE), lambda i: (0, 0))
    out = pl.BlockSpec((tm, LANE), lambda i: (i, 0))
    return pl.pallas_call(
        body, grid=(S // tm,),
        in_specs=[pl.BlockSpec((tm, LANE), lambda i: (i, ab)), pl.BlockSpec((tm, LANE), lambda i: (i, ab + 1)), vec, vec],
        out_specs=(out, out), out_shape=(jax.ShapeDtypeStruct((S, LANE), F32),) * 2,
        compiler_params=_cp(("parallel",)), name=name)(proj, proj, a_log, dt_bias)


def _dn_ab_bwd(dg, dbeta, proj, a_log, dt_bias, *, name, tm=512):
    S = proj.shape[0]
    tm = min(tm, S)
    ab = P0_AB // LANE

    def body(dg_ref, db_ref, a_ref, b_ref, al_ref, dt_ref, dab_ref, dal_ref, ddt_ref):
        i = pl.program_id(0)
        ea = jnp.exp(al_ref[...])
        u = a_ref[...] + dt_ref[...]
        dgv = dg_ref[...]
        da = dgv * (-ea) * _sigmoid(u)
        be = _sigmoid(b_ref[...])
        dab_ref[:, 0:LANE] = da
        dab_ref[:, LANE:2 * LANE] = db_ref[...] * be * (1.0 - be)
        dab_ref[:, 2 * LANE:] = jnp.zeros((tm, 2 * LANE), F32)

        @pl.when(i == 0)
        def _():
            dal_ref[...] = jnp.zeros_like(dal_ref)
            ddt_ref[...] = jnp.zeros_like(ddt_ref)

        dal_ref[...] += jnp.sum(dgv * (-ea) * _softplus(u), axis=0, keepdims=True)
        ddt_ref[...] += jnp.sum(da, axis=0, keepdims=True)

    vec = pl.BlockSpec((1, LANE), lambda i: (0, 0))
    row = pl.BlockSpec((tm, LANE), lambda i: (i, 0))
    return pl.pallas_call(
        body, grid=(S // tm,),
        in_specs=[row, row, pl.BlockSpec((tm, LANE), lambda i: (i, ab)), pl.BlockSpec((tm, LANE), lambda i: (i, ab + 1)), vec, vec],
        out_specs=(pl.BlockSpec((tm, 4 * LANE), lambda i: (i, 0)), vec, vec),
        out_shape=(jax.ShapeDtypeStruct((S, 4 * LANE), F32), jax.ShapeDtypeStruct((1, LANE), F32),
                   jax.ShapeDtypeStruct((1, LANE), F32)),
        compiler_params=_cp(("arbitrary",)), name=name)(dg, dbeta, proj, proj, a_log, dt_bias)


def _dot3(a, b):
    ah = a.astype(BF16)
    al = (a - ah.astype(F32)).astype(BF16)
    bh = b.astype(BF16)
    bl = (b - bh.astype(F32)).astype(BF16)
    d = lambda u, v: lax.dot_general(u, v, (NN, ((), ())), preferred_element_type=F32)
    return d(ah, bh) + (d(ah, bl) + d(al, bh))


DN_PAIR = 2


def _pdot(a, b, dims, dot=None):
    dot = dot or _bdot
    return jnp.stack([dot(a[i] if a.ndim == 3 else a, b[i] if b.ndim == 3 else b, dims) for i in range(DN_PAIR)])


def _tri_inverse(a):
    eye = (_iota2((CH, CH), 0) == _iota2((CH, CH), 1)).astype(F32)
    d3 = lambda u, v: jnp.stack([_dot3(u[i], v[i]) for i in range(DN_PAIR)])
    t = eye - a
    x = d3(a, a)
    n = 2
    while True:
        t = t + d3(t, x)
        n *= 2
        if n >= CH:
            break
        x = d3(x, x)
    return t


def _pick_col(m, n):
    return jnp.sum(jnp.where(_iota2(m.shape, 2) == n, m, 0.0), axis=2, keepdims=True)


def _dn_chunk_common(kk, qk, gc_c, gc_r, be_c):
    r_i = _iota2((CH, CH), 0)
    c_i = _iota2((CH, CH), 1)
    incl = r_i >= c_i
    strict = r_i > c_i
    dec = jnp.exp(jnp.where(incl, gc_c - gc_r, -1e30))
    e = jnp.exp(gc_c)
    gl = jnp.sum(jnp.where(_iota2((1, CH), 1) == CH - 1, gc_r, 0.0), axis=-1, keepdims=True)
    kds = jnp.exp(gl - gc_c)
    cd = jnp.exp(gl)
    a = jnp.where(strict, be_c * kk * dec, 0.0)
    p = qk * dec
    return dict(incl=incl, strict=strict, dec=dec, e=e, kds=kds, cd=cd, kk=kk, a=a, qk=qk, p=p)


def _dn_decay_tables(g_ref, b_ref, gcr, gcc, bcc):
    r_i = _iota2((CH, CH), 0)
    c_i = _iota2((CH, CH), 1)
    lc = (r_i >= c_i).astype(F32)
    eye = (r_i == c_i).astype(F32)
    for hh in range(DN_PAIR):
        g_rows_v = g_ref[hh]
        gcr[hh] = _fdot(g_rows_v, lc, NT)
        gcc[hh] = _fdot(lc, g_rows_v, NT)
        bcc[hh] = _fdot(eye, b_ref[hh], NT)
    return lc


def _dn_core_fwd(qn, kn, vc, g_rows, b_rows, out_g, *, name):
    S = qn.shape[0]
    nc = S // CH

    def body(q_ref, k_ref, v_ref, g_ref, b_ref, og_ref, o_ref, st_ref, t_ref, gcr, gcc, bcc):
        _dn_decay_tables(g_ref, b_ref, gcr, gcc, bcc)
        ogv = og_ref[...]

        def chunk(n, states):
            rows = pl.ds(pl.multiple_of(n * CH, CH), CH)
            q = q_ref[rows, :]
            k = k_ref[rows, :]
            kk = _bdot(k, k, NT)
            qk = _bdot(q, k, NT)
            v = jnp.stack([v_ref[rows, hh * HEAD_DIM:(hh + 1) * HEAD_DIM] for hh in range(DN_PAIR)])
            gc_c = _pick_col(gcc[...], n)
            be_c = _pick_col(bcc[...], n)
            gc_r = gcr[:, pl.ds(n, 1), :]
            c = _dn_chunk_common(kk, qk, gc_c, gc_r, be_c)
            t = _tri_inverse(c["a"])
            u0 = _pdot(t, be_c * v, NN)
            w = _pdot(t, (be_c * c["e"]) * k, NN)
            u = u0 - _pdot(w, states, NN)
            o = _pdot(c["e"] * q, states, NN) + _pdot(c["p"], u, NN)
            on = o * lax.rsqrt(jnp.mean(o * o, axis=-1, keepdims=True) + EPS) * ogv
            for hh in range(DN_PAIR):
                st_ref[hh, n] = states[hh]
                t_ref[hh, n] = t[hh]
                o_ref[rows, hh * HEAD_DIM:(hh + 1) * HEAD_DIM] = on[hh]
            return c["cd"] * states + _pdot(c["kds"] * k, u, TN)

        lax.fori_loop(0, nc, chunk, jnp.zeros((DN_PAIR, HEAD_DIM, HEAD_DIM), F32))

    qk_spec = pl.BlockSpec((S, HEAD_DIM), lambda h: (0, h))
    v_spec = pl.BlockSpec((S, DN_PAIR * HEAD_DIM), lambda h: (0, h))
    rows_spec = pl.BlockSpec((DN_PAIR, LANE, CH), lambda h: (h, 0, 0))
    return pl.pallas_call(
        body, grid=(DN_V_HEADS // DN_PAIR,),
        in_specs=[qk_spec, qk_spec, v_spec, rows_spec, rows_spec, pl.BlockSpec((1, HEAD_DIM), lambda h: (0, 0))],
        out_specs=(v_spec, pl.BlockSpec((DN_PAIR, nc, HEAD_DIM, HEAD_DIM), lambda h: (h, 0, 0, 0)),
                   pl.BlockSpec((DN_PAIR, nc, CH, CH), lambda h: (h, 0, 0, 0))),
        out_shape=(jax.ShapeDtypeStruct((S, MIX_WIDTH), F32), jax.ShapeDtypeStruct((DN_V_HEADS, nc, HEAD_DIM, HEAD_DIM), F32),
                   jax.ShapeDtypeStruct((DN_V_HEADS, nc, CH, CH), F32)),
        scratch_shapes=[pltpu.VMEM((DN_PAIR, LANE, CH), F32), pltpu.VMEM((DN_PAIR, CH, LANE), F32),
                        pltpu.VMEM((DN_PAIR, CH, LANE), F32)],
        compiler_params=_cp(("parallel",)), name=name)(qn, kn, vc, g_rows, b_rows, out_g.reshape(1, HEAD_DIM))


def _dn_chunk_bwd(q, k, v, kk, qk, state, t, gc_c, gc_r, be_c, don, ogv, ds_next):
    ones = jnp.ones((CH, LANE), F32)
    last_row = _iota2((CH, 1), 0) == CH - 1
    rowsum = lambda z: jnp.sum(z, axis=-1, keepdims=True)
    colsum = lambda z: jnp.sum(z, axis=-2, keepdims=True)
    c = _dn_chunk_common(kk, qk, gc_c, gc_r, be_c)
    e, kds, cd, dec, a, p = c["e"], c["kds"], c["cd"], c["dec"], c["a"], c["p"]
    vb = be_c * v
    kbe = (be_c * e) * k
    u0 = _pdot(t, vb, NN)
    w = _pdot(t, kbe, NN)
    u = u0 - _pdot(w, state, NN)
    qd = e * q
    kd = kds * k
    o = _pdot(qd, state, NN) + _pdot(p, u, NN)
    r = lax.rsqrt(jnp.mean(o * o, axis=-1, keepdims=True) + EPS)
    y = o * r
    dog = colsum(don * y)
    dy = don * ogv
    d_o = r * (dy - y * jnp.mean(dy * y, axis=-1, keepdims=True))
    du = _pdot(p, d_o, TN) + _pdot(kd, ds_next, NN)
    dqd = _pdot(d_o, state, NT)
    dstate = _pdot(qd, d_o, TN) + cd * ds_next - _pdot(w, du, TN)
    dcd = colsum(rowsum(ds_next * state))
    dkd = _pdot(u, ds_next, NT)
    dw = -_pdot(du, state, NT)
    dvb = _pdot(t, du, TN)
    dkbe = _pdot(t, dw, TN)
    da = -jnp.where(c["strict"], _pdot(dvb, u0, NT) + _pdot(dkbe, w, NT), 0.0)
    dp = jnp.where(c["incl"], _pdot(d_o, u, NT), 0.0)
    gmat = da * a + dp * p
    dad = da * dec
    x = be_c * dad
    dpd = dp * dec
    dk = _pdot(x, k, NN) + _pdot(x, k, TN) + _pdot(dpd, q, TN)
    dq = _pdot(dpd, k, NN) + e * dqd
    dbe = rowsum(dad * c["kk"])
    dgc = rowsum(gmat) + rowsum(dqd * q) * e
    rk = rowsum(dkd * k) * kds
    dk = dk + kds * dkd
    dgc = dgc - rk
    dgl = colsum(rk) + dcd * cd
    sk = rowsum(dkbe * k)
    dk = dk + (be_c * e) * dkbe
    dbe = dbe + sk * e + rowsum(dvb * v)
    dgc = dgc + sk * be_c * e
    dgc = dgc + jnp.where(last_row, dgl, 0.0)
    dgc = dgc - _pdot(gmat, ones, TN, dot=_fdot)
    return dq, dk, be_c * dvb, dgc, dbe, dog, dstate


def _dn_core_bwd(dmix, qn, kn, vc, g_rows, b_rows, out_g, states, tinv, *, name):
    S = qn.shape[0]
    nc = S // CH

    def body(do_ref, q_ref, k_ref, v_ref, g_ref, b_ref, og_ref, st_ref, t_ref,
             dq_ref, dk_ref, dv_ref, dg_ref, db_ref, dog_ref, gcr, gcc, bcc, dgc_acc):
        h = pl.program_id(0)
        lc = _dn_decay_tables(g_ref, b_ref, gcr, gcc, bcc)
        ogv = og_ref[...]
        dgc_acc[...] = jnp.zeros_like(dgc_acc)
        db_ref[...] = jnp.zeros_like(db_ref)
        lane_n = _iota2((CH, LANE), 1)

        @pl.when(h == 0)
        def _():
            dog_ref[...] = jnp.zeros_like(dog_ref)

        def chunk(m, carry):
            ds_nexts, dog = carry
            n = nc - 1 - m
            rows = pl.ds(pl.multiple_of(n * CH, CH), CH)
            q = q_ref[rows, :]
            k = k_ref[rows, :]
            kk = _bdot(k, k, NT)
            qk = _bdot(q, k, NT)
            heads = lambda ref: jnp.stack([ref[rows, hh * HEAD_DIM:(hh + 1) * HEAD_DIM] for hh in range(DN_PAIR)])
            state = jnp.stack([st_ref[hh, n] for hh in range(DN_PAIR)])
            t = jnp.stack([t_ref[hh, n] for hh in range(DN_PAIR)])
            dq, dk, dv, dgc, dbe, dog_h, dstate = _dn_chunk_bwd(
                q, k, heads(v_ref), kk, qk, state, t, _pick_col(gcc[...], n), gcr[:, pl.ds(n, 1), :],
                _pick_col(bcc[...], n), heads(do_ref), ogv, ds_nexts)
            for hh in range(DN_PAIR):
                dv_ref[rows, hh * HEAD_DIM:(hh + 1) * HEAD_DIM] = dv[hh]
            dgc_acc[...] = jnp.where(lane_n == n, dgc, dgc_acc[...])
            db_ref[...] = jnp.where(lane_n == n, dbe, db_ref[...])
            dq_ref[rows, :] = jnp.sum(dq, axis=0)
            dk_ref[rows, :] = jnp.sum(dk, axis=0)
            return dstate, dog + jnp.sum(dog_h, axis=0)

        _, dog = lax.fori_loop(0, nc, chunk, (jnp.zeros((DN_PAIR, HEAD_DIM, HEAD_DIM), F32), jnp.zeros((1, HEAD_DIM), F32)))
        dog_ref[...] += dog
        for hh in range(DN_PAIR):
            dg_ref[hh] = _fdot(lc, dgc_acc[hh], TN)

    qk_spec = pl.BlockSpec((S, HEAD_DIM), lambda h: (0, h))
    v_spec = pl.BlockSpec((S, DN_PAIR * HEAD_DIM), lambda h: (0, h))
    rows_spec = pl.BlockSpec((DN_PAIR, LANE, CH), lambda h: (h, 0, 0))
    cols_spec = pl.BlockSpec((DN_PAIR, CH, LANE), lambda h: (h, 0, 0))
    vec = pl.BlockSpec((1, HEAD_DIM), lambda h: (0, 0))
    qk_out = jax.ShapeDtypeStruct((S, DN_QK_WIDTH), F32)
    return pl.pallas_call(
        body, grid=(DN_V_HEADS // DN_PAIR,),
        in_specs=[v_spec, qk_spec, qk_spec, v_spec, rows_spec, rows_spec, vec,
                  pl.BlockSpec((DN_PAIR, nc, HEAD_DIM, HEAD_DIM), lambda h: (h, 0, 0, 0)),
                  pl.BlockSpec((DN_PAIR, nc, CH, CH), lambda h: (h, 0, 0, 0))],
        out_specs=(qk_spec, qk_spec, v_spec, cols_spec, cols_spec, vec),
        out_shape=(qk_out, qk_out, jax.ShapeDtypeStruct((S, MIX_WIDTH), F32), jax.ShapeDtypeStruct((DN_V_HEADS, CH, LANE), F32),
                   jax.ShapeDtypeStruct((DN_V_HEADS, CH, LANE), F32), jax.ShapeDtypeStruct((1, HEAD_DIM), F32)),
        scratch_shapes=[pltpu.VMEM((DN_PAIR, LANE, CH), F32), pltpu.VMEM((DN_PAIR, CH, LANE), F32),
                        pltpu.VMEM((DN_PAIR, CH, LANE), F32), pltpu.VMEM((DN_PAIR, CH, LANE), F32)],
        compiler_params=_cp(("arbitrary",)), name=name)(
            dmix, qn, kn, vc, g_rows, b_rows, out_g.reshape(1, HEAD_DIM), states, tinv)


def _rows_form(x, nc):
    t = x[:, :DN_V_HEADS].T.reshape(DN_V_HEADS, nc, CH)
    return jnp.pad(t, ((0, 0), (0, LANE - nc), (0, 0)))


def _cols_to_nat(x, nc):
    t = jnp.transpose(x[:, :, :nc], (2, 1, 0)).reshape(nc * CH, DN_V_HEADS)
    return jnp.pad(t, ((0, 0), (0, LANE - DN_V_HEADS)))


_C_QKV = 2 * DN_QK_WIDTH + MIX_WIDTH


def _w0_to_padded(w):
    rows = w.shape[0]
    z = lambda n: jnp.zeros((rows, n), w.dtype)
    a = w[:, _C_QKV:_C_QKV + DN_V_HEADS]
    b = w[:, _C_QKV + DN_V_HEADS:_C_QKV + 2 * DN_V_HEADS]
    return jnp.concatenate([w[:, :_C_QKV], w[:, _C_QKV + 2 * DN_V_HEADS:], a, z(LANE - DN_V_HEADS), b,
                            z(P0 - P0_AB - LANE - DN_V_HEADS)], axis=1)


def _w0_from_padded(g):
    return jnp.concatenate([g[:, :_C_QKV], g[:, P0_AB:P0_AB + DN_V_HEADS], g[:, P0_AB + LANE:P0_AB + LANE + DN_V_HEADS],
                            g[:, _C_QKV:P0_AB]], axis=1)


def _pad_lane(v):
    v = v.reshape(1, -1)
    return jnp.pad(v, ((0, 0), (0, LANE - v.shape[1])))


def _local_step(x, mem, target, norm_g, mem_norm_g, w_kv, xa_q_g, xa_k_g, w_out, w_in0, conv_w, a_log, dt_bias, out_g,
                w_in1, sb_q_g, sb_k_g):
    S = x.shape[0]
    nc = S // CH
    al = _pad_lane(a_log)
    dtb = _pad_lane(dt_bias)
    q_scale = HEAD_DIM ** -0.5

    mem_n = _rmsnorm_fwd(mem, mem_norm_g, name="mem_norm")
    kv = [_matmul(mem_n, w_kv[i], out_dtype=F32, name=f"kv{i}") for i in range(2)]

    h0 = _rmsnorm_fwd(x, norm_g[0], name="norm0")
    proj0 = _matmul(h0, w_in0, name="proj0")
    qn = _dn_pre_fwd(proj0, conv_w, 0, DN_QK_WIDTH, l2=True, scale=q_scale, name="dn_pre_q")
    kn = _dn_pre_fwd(proj0, conv_w, DN_QK_WIDTH, DN_QK_WIDTH, l2=True, scale=1.0, name="dn_pre_k")
    vc = _dn_pre_fwd(proj0, conv_w, 2 * DN_QK_WIDTH, MIX_WIDTH, l2=False, scale=1.0, name="dn_pre_v")
    g_nat, b_nat = _dn_ab_fwd(proj0, al, dtb, name="dn_ab")
    g_rows = _rows_form(g_nat, nc)
    b_rows = _rows_form(b_nat, nc)
    mix0, states, tinv = _dn_core_fwd(qn, kn, vc, g_rows, b_rows, out_g, name="dn_core")
    xa0 = _xa_fwd(proj0, P0_XQ, kv[0], xa_q_g[0], xa_k_g[0], name="xa0")
    cat0 = jnp.concatenate([mix0, xa0], axis=1)
    y0 = _gate_fwd(cat0, proj0, P0_Z, name="gate0")
    x1 = _matmul(y0, w_out[0], res=x, name="out0")

    h1 = _rmsnorm_fwd(x1, norm_g[1], name="norm1")
    proj1 = _matmul(h1, w_in1, name="proj1")
    mix1, tot1 = _sb_fwd(proj1, sb_q_g, sb_k_g, name="sb")
    xa1 = _xa_fwd(proj1, P1_XQ, kv[1], xa_q_g[1], xa_k_g[1], name="xa1")
    cat1 = jnp.concatenate([mix1, xa1], axis=1)
    y1 = _gate_fwd(cat1, proj1, P1_Z, name="gate1")
    x2 = _matmul(y1, w_out[1], res=x1, name="out1")

    dx2, loss_vec = _loss_head(x2, target, name="loss")

    d_wout1 = _matmul(y1, dx2, ta=True, name="d_wout1")
    dy1 = _matmul(dx2, w_out[1], tb=True, name="dy1")
    dcat1, dz1 = _gate_bwd(dy1, cat1, proj1, P1_Z, name="gate1_bwd")
    dqkv1, d_sbq, d_sbk = _sb_bwd(dcat1[:, :MIX_WIDTH], tot1, proj1, sb_q_g, sb_k_g, name="sb_bwd")
    dxq1, dkv1, d_xaq1, d_xak1 = _xa_bwd(dcat1[:, MIX_WIDTH:], proj1, P1_XQ, kv[1], xa_q_g[1], xa_k_g[1], name="xa1_bwd")
    dproj1 = jnp.concatenate([dqkv1, dxq1, dz1], axis=1)
    d_win1 = _matmul(h1, dproj1, ta=True, name="d_win1")
    dh1 = _matmul(dproj1, w_in1, tb=True, name="dh1")
    dx1, d_ng1 = _rmsnorm_bwd(dh1, x1, norm_g[1], dx2, name="norm1_bwd")

    d_wout0 = _matmul(y0, dx1, ta=True, name="d_wout0")
    dy0 = _matmul(dx1, w_out[0], tb=True, name="dy0")
    dcat0, dz0 = _gate_bwd(dy0, cat0, proj0, P0_Z, name="gate0_bwd")
    dqv, dkv_h, dvc, dg_cols, db_cols, d_outg = _dn_core_bwd(
        dcat0[:, :MIX_WIDTH], qn, kn, vc, g_rows, b_rows, out_g, states, tinv, name="dn_core_bwd")
    dpq, dwq = _dn_pre_bwd(dqv, proj0, conv_w, 0, DN_QK_WIDTH, l2=True, scale=q_scale, name="dn_pre_q_bwd")
    dpk, dwk = _dn_pre_bwd(dkv_h, proj0, conv_w, DN_QK_WIDTH, DN_QK_WIDTH, l2=True, scale=1.0, name="dn_pre_k_bwd")
    dpv, dwv = _dn_pre_bwd(dvc, proj0, conv_w, 2 * DN_QK_WIDTH, MIX_WIDTH, l2=False, scale=1.0, name="dn_pre_v_bwd")
    dab, d_alog, d_dt = _dn_ab_bwd(_cols_to_nat(dg_cols, nc), _cols_to_nat(db_cols, nc), proj0, al, dtb, name="dn_ab_bwd")
    dxq0, dkv0, d_xaq0, d_xak0 = _xa_bwd(dcat0[:, MIX_WIDTH:], proj0, P0_XQ, kv[0], xa_q_g[0], xa_k_g[0], name="xa0_bwd")
    dproj0 = jnp.concatenate([dpq, dpk, dpv, dxq0, dz0, dab], axis=1)
    d_win0 = _matmul(h0, dproj0, ta=True, name="d_win0")
    dh0 = _matmul(dproj0, w_in0, tb=True, name="dh0")
    dx0, d_ng0 = _rmsnorm_bwd(dh0, x, norm_g[0], dx1, name="norm0_bwd")

    d_wkv = [_matmul(mem_n, d, ta=True, name=f"d_wkv{i}") for i, d in enumerate((dkv0, dkv1))]
    dmem_n = _matmul(dkv1, w_kv[1], tb=True, res=_matmul(dkv0, w_kv[0], tb=True, name="dmem0"), name="dmem1")
    _, d_memg = _rmsnorm_bwd(dmem_n, mem, mem_norm_g, None, name="mem_norm_bwd")

    grads = dict(
        norm_g=jnp.concatenate([d_ng0, d_ng1], axis=0), mem_norm_g=d_memg.reshape(-1), mem_w_kv=jnp.stack(d_wkv),
        xa_q_norm_g=jnp.concatenate([d_xaq0, d_xaq1], axis=0), xa_k_norm_g=jnp.concatenate([d_xak0, d_xak1], axis=0),
        w_out=jnp.stack([d_wout0, d_wout1]), dn_w_in=d_win0, dn_conv_w=jnp.concatenate([dwq, dwk, dwv], axis=1),
        dn_a_log=d_alog[:, :DN_V_HEADS], dn_dt_bias=d_dt[:, :DN_V_HEADS], dn_out_norm_g=d_outg, sb_w_in=d_win1,
        sb_q_norm_g=d_sbq, sb_k_norm_g=d_sbk)
    return loss_vec, dx0, grads


ANY = pl.BlockSpec(memory_space=pl.ANY)


def _place():
    x, y, c = lax.axis_index("x"), lax.axis_index("y"), lax.axis_index("c")
    chips = [(1 - x, y), (x, 1 - y), (1 - x, 1 - y)]
    return x, y, c, 2 * x + y, (x, y, 1 - c), chips


def _rcopy(src, dst, send, recv, i, dev):
    return pltpu.make_async_remote_copy(src_ref=src, dst_ref=dst, send_sem=send.at[i], recv_sem=recv.at[i],
                                        device_id=dev, device_id_type=MESH)


def _gather_weights(srcs, *, name):
    nt = len(srcs)

    def body(*refs):
        src, dst = refs[:nt], refs[nt:2 * nt]
        send, recv = refs[2 * nt:]
        x, y, c, j, sib, chips = _place()
        sends = []
        for t in range(nt):
            for k, (cx, cy) in enumerate(chips):
                sends.append(_rcopy(src[t].at[c], dst[t].at[j, c], send, recv, 6 * t + k, (cx, cy, c)))
                sends[-1].start()
        for t in range(nt):
            for k, (cx, cy) in enumerate(chips):
                landed = dst[t].at[2 * cx + cy, c]
                _rcopy(landed, landed, send, recv, 6 * t + k, (cx, cy, c)).wait_recv()
                sends.append(_rcopy(landed, landed, send, recv, 6 * t + 3 + k, sib))
                sends[-1].start()
        for t in range(nt):
            for k, (cx, cy) in enumerate(chips):
                other = dst[t].at[2 * cx + cy, 1 - c]
                _rcopy(other, other, send, recv, 6 * t + 3 + k, sib).wait_recv()
        for cp in sends:
            cp.wait_send()

    return pl.pallas_call(
        body, in_specs=[ANY] * nt, out_specs=[ANY] * nt,
        out_shape=[jax.ShapeDtypeStruct((N_CHIPS,) + s.shape, s.dtype) for s in srcs],
        scratch_shapes=[pltpu.SemaphoreType.DMA((6 * nt,)), pltpu.SemaphoreType.DMA((6 * nt,))],
        name=name)(*srcs)


def _swap_halves(xs, *, name):
    nt = len(xs)

    def body(*refs):
        src, dst = refs[:nt], refs[nt:2 * nt]
        send, recv = refs[2 * nt:]
        x, y, c, j, sib, chips = _place()
        cps = []
        for t in range(nt):
            for s in range(N_CHIPS):
                cps.append(_rcopy(src[t].at[s, 1 - c], dst[t].at[s], send, recv, 4 * t + s, sib))
                cps[-1].start()
        for cp in cps:
            cp.wait_recv()
        for cp in cps:
            cp.wait_send()

    return pl.pallas_call(
        body, in_specs=[ANY] * nt, out_specs=[ANY] * nt,
        out_shape=[jax.ShapeDtypeStruct((N_CHIPS,) + a.shape[2:], a.dtype) for a in xs],
        scratch_shapes=[pltpu.SemaphoreType.DMA((4 * nt,)), pltpu.SemaphoreType.DMA((4 * nt,))], name=name)(*xs)


def _scatter_to_chips(ps, *, name):
    nt = len(ps)

    def body(*refs):
        src, dst = refs[:nt], refs[nt:2 * nt]
        send, recv = refs[2 * nt:]
        x, y, c, j, sib, chips = _place()
        cps = []
        for t in range(nt):
            for k, (cx, cy) in enumerate(chips):
                cps.append(_rcopy(src[t].at[2 * cx + cy], dst[t].at[k], send, recv, 3 * t + k, (cx, cy, c)))
                cps[-1].start()
        for cp in cps:
            cp.wait_recv()
        for cp in cps:
            cp.wait_send()

    return pl.pallas_call(
        body, in_specs=[ANY] * nt, out_specs=[ANY] * nt,
        out_shape=[jax.ShapeDtypeStruct((3,) + a.shape[1:], a.dtype) for a in ps],
        scratch_shapes=[pltpu.SemaphoreType.DMA((3 * nt,)), pltpu.SemaphoreType.DMA((3 * nt,))], name=name)(*ps)


def _swap_with_sibling(fs, *, name):
    nt = len(fs)

    def body(*refs):
        src, dst = refs[:nt], refs[nt:2 * nt]
        send, recv = refs[2 * nt:]
        x, y, c, j, sib, chips = _place()
        cps = [_rcopy(src[t], dst[t], send, recv, t, sib) for t in range(nt)]
        for cp in cps:
            cp.start()
        for cp in cps:
            cp.wait_recv()
        for cp in cps:
            cp.wait_send()

    return pl.pallas_call(
        body, in_specs=[ANY] * nt, out_specs=[ANY] * nt,
        out_shape=[jax.ShapeDtypeStruct(a.shape, a.dtype) for a in fs],
        scratch_shapes=[pltpu.SemaphoreType.DMA((nt,)), pltpu.SemaphoreType.DMA((nt,))], name=name)(*fs)


def _all_reduce_small(pack, *, name):
    rows = pack.shape[0]

    def body(p_ref, o_ref, buf, send, recv):
        x, y, c = lax.axis_index("x"), lax.axis_index("y"), lax.axis_index("c")
        me = 4 * x + 2 * y + c
        buf[me] = p_ref[...]
        cps = []
        for r in range(1, 8):
            dev = (x ^ (r >> 2), y ^ ((r >> 1) & 1), c ^ (r & 1))
            cps.append(_rcopy(p_ref, buf.at[me], send, recv, r - 1, dev))
            cps[-1].start()
        for r in range(1, 8):
            frm = buf.at[me ^ r]
            _rcopy(frm, frm, send, recv, r - 1, (x, y, c)).wait_recv()
        for cp in cps:
            cp.wait_send()
        acc = buf[0]
        for d in range(1, 8):
            acc = acc + buf[d]
        o_ref[...] = acc

    vm = pl.BlockSpec(memory_space=pltpu.VMEM)
    return pl.pallas_call(
        body, in_specs=[vm], out_specs=vm, out_shape=jax.ShapeDtypeStruct(pack.shape, F32),
        scratch_shapes=[pltpu.VMEM((8, rows, LANE), F32), pltpu.SemaphoreType.DMA((7,)), pltpu.SemaphoreType.DMA((7,))],
        name=name)(pack)


def _add_halves(x, b, c_idx, *, name, tr=256):
    _, _, R, C = x.shape
    tr = min(tr, R)

    def body(c_ref, x_ref, b_ref, o_ref):
        o_ref[...] = (x_ref[...].astype(F32) + b_ref[...].astype(F32)).astype(o_ref.dtype)

    return pl.pallas_call(
        body,
        grid_spec=pltpu.PrefetchScalarGridSpec(
            num_scalar_prefetch=1, grid=(N_CHIPS, R // tr),
            in_specs=[pl.BlockSpec((None, None, tr, C), lambda s, i, c_ref: (s, c_ref[0], i, 0)),
                      pl.BlockSpec((None, tr, C), lambda s, i, c_ref: (s, i, 0))],
            out_specs=pl.BlockSpec((None, tr, C), lambda s, i, c_ref: (s, i, 0))),
        out_shape=jax.ShapeDtypeStruct(b.shape, b.dtype), compiler_params=_cp(("parallel", "parallel")), name=name)(c_idx, x, b)


def _sum_slot(p, rcv, j_idx, *, name, tr=256):
    _, R, C = p.shape
    tr = min(tr, R)

    def body(j_ref, p_ref, r_ref, o_ref):
        acc = p_ref[...].astype(F32)
        for k in range(3):
            acc = acc + r_ref[k].astype(F32)
        o_ref[...] = acc

    return pl.pallas_call(
        body,
        grid_spec=pltpu.PrefetchScalarGridSpec(
            num_scalar_prefetch=1, grid=(R // tr,),
            in_specs=[pl.BlockSpec((None, tr, C), lambda i, j_ref: (j_ref[0], i, 0)),
                      pl.BlockSpec((3, tr, C), lambda i, j_ref: (0, i, 0))],
            out_specs=pl.BlockSpec((tr, C), lambda i, j_ref: (i, 0))),
        out_shape=jax.ShapeDtypeStruct((R, C), F32), compiler_params=_cp(("parallel",)), name=name)(j_idx, p, rcv)


def _adamw_math(w, g, m, v):
    nm = ADAM_B1 * m + (1.0 - ADAM_B1) * g
    nv = ADAM_B2 * v + (1.0 - ADAM_B2) * (g * g)
    m_hat = nm / (1.0 - ADAM_B1 ** ADAM_STEP)
    v_hat = nv / (1.0 - ADAM_B2 ** ADAM_STEP)
    return -ADAM_LR * (m_hat / (jnp.sqrt(v_hat) + ADAM_EPS) + ADAM_WD * w), nm, nv


def _adamw_halves(w, g_mine, g_theirs, m, v, c_idx, *, name, tr=128):
    _, R, C = w.shape
    tr = tr if R % tr == 0 else R

    def body(c_ref, w_ref, gm_ref, gt_ref, m_ref, v_ref, g_ref, d_ref, nm_ref, nv_ref):
        gv = jnp.where(pl.program_id(0) == c_ref[0], gm_ref[...], gt_ref[...])
        d, nm, nv = _adamw_math(w_ref[...], gv, m_ref[...], v_ref[...])
        g_ref[...] = gv
        d_ref[...] = d
        nm_ref[...] = nm
        nv_ref[...] = nv

    full = pl.BlockSpec((None, tr, C), lambda hh, i, c_ref: (hh, i, 0))
    half = pl.BlockSpec((tr, C), lambda hh, i, c_ref: (i, 0))
    sh = jax.ShapeDtypeStruct(w.shape, F32)
    return pl.pallas_call(
        body,
        grid_spec=pltpu.PrefetchScalarGridSpec(num_scalar_prefetch=1, grid=(2, R // tr),
                                               in_specs=[full, half, half, full, full], out_specs=(full,) * 4),
        out_shape=(sh,) * 4, compiler_params=_cp(("parallel", "parallel")), name=name)(c_idx, w, g_mine, g_theirs, m, v)


def _adamw(w, g, m, v, *, name, tr=128):
    R, C = w.shape
    tr = tr if R % tr == 0 else R

    def body(w_ref, g_ref, m_ref, v_ref, d_ref, nm_ref, nv_ref):
        d_ref[...], nm_ref[...], nv_ref[...] = _adamw_math(w_ref[...], g_ref[...], m_ref[...], v_ref[...])

    blk = pl.BlockSpec((tr, C), lambda i: (i, 0))
    sh = jax.ShapeDtypeStruct((R, C), F32)
    return pl.pallas_call(body, grid=(R // tr,), in_specs=[blk] * 4, out_specs=(blk,) * 3, out_shape=(sh,) * 3,
                          compiler_params=_cp(("parallel",)), name=name)(w, g, m, v)


_SMALL = ["norm_g", "mem_norm_g", "xa_q_norm_g", "xa_k_norm_g", "dn_a_log", "dn_dt_bias", "dn_out_norm_g",
          "sb_q_norm_g", "sb_k_norm_g"]


def _pack(parts):
    rows, metas, r0 = [], [], 0
    for p in parts:
        flat = p.reshape(-1).astype(F32)
        n = flat.shape[0]
        nr = -(-n // (8 * LANE)) * 8
        rows.append(jnp.pad(flat, (0, nr * LANE - n)).reshape(nr, LANE))
        metas.append((r0, nr, n, p.shape))
        r0 += nr
    return jnp.concatenate(rows, axis=0), metas


def _unpack(pack, metas):
    return [pack[r0:r0 + nr].reshape(-1)[:n].reshape(shape) for r0, nr, n, shape in metas]


def kernel(x, mem, norm_g, mem_norm_g, mem_w_kv, xa_q_norm_g, xa_k_norm_g, w_out, dn_w_in, dn_conv_w, dn_a_log, dn_dt_bias, dn_out_norm_g, sb_w_in, sb_q_norm_g, sb_k_norm_g, loss_target, m_norm_g, m_mem_norm_g, m_mem_w_kv, m_xa_q_norm_g, m_xa_k_norm_g, m_w_out, m_dn_w_in, m_dn_conv_w, m_dn_a_log, m_dn_dt_bias, m_dn_out_norm_g, m_sb_w_in, m_sb_q_norm_g, m_sb_k_norm_g, v_norm_g, v_mem_norm_g, v_mem_w_kv, v_xa_q_norm_g, v_xa_k_norm_g, v_w_out, v_dn_w_in, v_dn_conv_w, v_dn_a_log, v_dn_dt_bias, v_dn_out_norm_g, v_sb_w_in, v_sb_q_norm_g, v_sb_k_norm_g):
    W = dict(norm_g=norm_g, mem_norm_g=mem_norm_g, mem_w_kv=mem_w_kv, xa_q_norm_g=xa_q_norm_g, xa_k_norm_g=xa_k_norm_g,
             w_out=w_out, dn_w_in=dn_w_in, dn_conv_w=dn_conv_w, dn_a_log=dn_a_log, dn_dt_bias=dn_dt_bias,
             dn_out_norm_g=dn_out_norm_g, sb_w_in=sb_w_in, sb_q_norm_g=sb_q_norm_g, sb_k_norm_g=sb_k_norm_g)
    M = dict(norm_g=m_norm_g, mem_norm_g=m_mem_norm_g, mem_w_kv=m_mem_w_kv, xa_q_norm_g=m_xa_q_norm_g,
             xa_k_norm_g=m_xa_k_norm_g, w_out=m_w_out, dn_w_in=m_dn_w_in, dn_conv_w=m_dn_conv_w, dn_a_log=m_dn_a_log,
             dn_dt_bias=m_dn_dt_bias, dn_out_norm_g=m_dn_out_norm_g, sb_w_in=m_sb_w_in, sb_q_norm_g=m_sb_q_norm_g,
             sb_k_norm_g=m_sb_k_norm_g)
    V = dict(norm_g=v_norm_g, mem_norm_g=v_mem_norm_g, mem_w_kv=v_mem_w_kv, xa_q_norm_g=v_xa_q_norm_g,
             xa_k_norm_g=v_xa_k_norm_g, w_out=v_w_out, dn_w_in=v_dn_w_in, dn_conv_w=v_dn_conv_w, dn_a_log=v_dn_a_log,
             dn_dt_bias=v_dn_dt_bias, dn_out_norm_g=v_dn_out_norm_g, sb_w_in=v_sb_w_in, sb_q_norm_g=v_sb_q_norm_g,
             sb_k_norm_g=v_sb_k_norm_g)
    names = ["norm_g", "mem_norm_g", "mem_w_kv", "xa_q_norm_g", "xa_k_norm_g", "w_out", "dn_w_in", "dn_conv_w",
             "dn_a_log", "dn_dt_bias", "dn_out_norm_g", "sb_w_in", "sb_q_norm_g", "sb_k_norm_g"]
    cx, cy, cc = lax.axis_index("x"), lax.axis_index("y"), lax.axis_index("c")
    slot = 2 * cx + cy
    half_r = D_MODEL // 2
    conv_cols = dn_conv_w.shape[2]

    w0s = jnp.pad(dn_w_in[0].astype(BF16), ((0, 0), (0, P0_SHARD_PAD - P0_SHARD))).reshape(2, half_r, P0_SHARD_PAD)
    w1s = sb_w_in[0].astype(BF16).reshape(2, half_r, SB_PROJ // N_CHIPS)
    convs = jnp.pad(dn_conv_w[0], ((0, 8 - DN_CONV), (0, 0))).reshape(8, 2, conv_cols // 2).transpose(1, 0, 2)
    own = [w0s, w1s, w_out.astype(BF16), mem_w_kv.astype(BF16), convs]
    gathered = _gather_weights(own, name="gather_weights")
    mine = (jnp.arange(N_CHIPS) == slot).reshape(N_CHIPS, 1, 1, 1)
    g0, g1, gout, gkv, gconv = [jnp.where(mine, o[None], g) for o, g in zip(own, gathered)]
    w0_true = g0.reshape(N_CHIPS, D_MODEL, P0_SHARD_PAD)[:, :, :P0_SHARD].transpose(1, 0, 2).reshape(D_MODEL, DN_PROJ)
    w_in0 = _w0_to_padded(w0_true)
    w_in1 = g1.reshape(N_CHIPS, D_MODEL, SB_PROJ // N_CHIPS).transpose(1, 0, 2).reshape(D_MODEL, SB_PROJ)
    w_out_f = gout.transpose(1, 0, 2, 3).reshape(2, INNER, D_MODEL)
    w_kv_f = gkv.transpose(1, 0, 2, 3).reshape(2, D_MODEL, 2 * XA_WIDTH)
    conv_f = gconv.transpose(2, 0, 1, 3).reshape(8, N_CHIPS * conv_cols)[:DN_CONV]

    loss_vec, grad_x, g = _local_step(
        x[0], mem[0], loss_target[0], norm_g, mem_norm_g, w_kv_f, xa_q_norm_g, xa_k_norm_g, w_out_f, w_in0, conv_f,
        dn_a_log[0], dn_dt_bias[0], dn_out_norm_g[0], w_in1, sb_q_norm_g[0], sb_k_norm_g[0])

    d0 = _w0_from_padded(g["dn_w_in"]).reshape(2, half_r, N_CHIPS, P0_SHARD)
    d0 = jnp.pad(d0, ((0, 0), (0, 0), (0, 0), (0, P0_SHARD_PAD - P0_SHARD))).transpose(2, 0, 1, 3).astype(BF16)
    d1 = g["sb_w_in"].reshape(2, half_r, N_CHIPS, SB_PROJ // N_CHIPS).transpose(2, 0, 1, 3).astype(BF16)
    dout = g["w_out"].reshape(2, N_CHIPS, INNER // N_CHIPS, D_MODEL).transpose(1, 0, 2, 3).astype(BF16)
    dkv = g["mem_w_kv"].reshape(2, N_CHIPS, D_MODEL // N_CHIPS, 2 * XA_WIDTH).transpose(1, 0, 2, 3).astype(BF16)
    xs = [d0, d1, dout, dkv]
    c_idx = jnp.reshape(cc, (1,)).astype(jnp.int32)
    j_idx = jnp.reshape(slot, (1,)).astype(jnp.int32)
    from_sib = _swap_halves(xs, name="rs_swap")
    ps = [_add_halves(a, b, c_idx, name=f"rs_add{t}") for t, (a, b) in enumerate(zip(xs, from_sib))]
    rcvs = _scatter_to_chips(ps, name="rs_scatter")
    fs = [_sum_slot(p, r, j_idx, name=f"rs_sum{t}") for t, (p, r) in enumerate(zip(ps, rcvs))]
    fs[0] = fs[0][:, :P0_SHARD]
    theirs = _swap_with_sibling(fs, name="rs_join")
    big_names = ["dn_w_in", "sb_w_in", "w_out", "mem_w_kv"]

    pack, metas = _pack([g[n] for n in _SMALL] + [g["dn_conv_w"], loss_vec])
    red = _unpack(_all_reduce_small(pack, name="all_reduce_small"), metas)
    small_grads = {n: r.reshape(W[n].shape) for n, r in zip(_SMALL, red)}
    small_grads["dn_conv_w"] = lax.dynamic_slice_in_dim(red[len(_SMALL)], slot * conv_cols, conv_cols, axis=1).reshape(W["dn_conv_w"].shape)
    loss = red[-1][0, 0]

    out_g, out_d, out_m, out_v = {}, {}, {}, {}
    for n, mine_g, their_g in zip(big_names, fs, theirs):
        shp = W[n].shape
        h3 = (2,) + mine_g.shape
        outs = _adamw_halves(W[n].reshape(h3), mine_g, their_g, M[n].reshape(h3), V[n].reshape(h3), c_idx, name=f"adamw_{n}")
        out_g[n], out_d[n], out_m[n], out_v[n] = [o.reshape(shp) for o in outs]
    small_names = _SMALL + ["dn_conv_w"]
    wp, sm = _pack([W[n] for n in small_names])
    gp, _ = _pack([small_grads[n] for n in small_names])
    mp, _ = _pack([M[n] for n in small_names])
    vp, _ = _pack([V[n] for n in small_names])
    dp, nmp, nvp = _adamw(wp, gp, mp, vp, name="adamw_small")
    for n, d, nm, nv in zip(small_names, _unpack(dp, sm), _unpack(nmp, sm), _unpack(nvp, sm)):
        out_g[n], out_d[n], out_m[n], out_v[n] = small_grads[n], d, nm, nv

    return (loss, grad_x[None], *[out_g[n] for n in names], *[out_d[n] for n in names], *[out_m[n] for n in names],
            *[out_v[n] for n in names])
```

```python
import functools
import math

import jax
import jax.numpy as jnp
from jax import lax
from jax.experimental import pallas as pl
from jax.experimental.pallas import tpu as pltpu

F32 = jnp.float32
BF16 = jnp.bfloat16
HI = lax.Precision.HIGHEST
MESH = pl.DeviceIdType.MESH

D_MODEL = 2048
INNER = 4096
XA_WIDTH = 1024
XA_HEADS = 4
XA_DIM = 256
MIX_WIDTH = 3072
HEAD_DIM = 128
DN_V_HEADS = 24
DN_QK_WIDTH = 1536
DN_CONV = 4
DN_PROJ = 11312
SB_PROJ = 14336
EPS = 1e-6
N_CHIPS = 4

CH = 128
LANE = 128

P0_XQ = 6144
P0_Z = 7168
P0_AB = 11264
P0 = 11776
P0_SHARD = DN_PROJ // N_CHIPS
P0_SHARD_PAD = 2944
P1_XQ = 9216
P1_Z = 10240
P1 = SB_PROJ

ADAM_LR = 0.001
ADAM_B1 = 0.9
ADAM_B2 = 0.999
ADAM_EPS = 1e-08
ADAM_WD = 0.01
ADAM_STEP = 10

VMEM_LIMIT = 48 * 1024 * 1024


def _cp(sem=None, **kw):
    return pltpu.CompilerParams(dimension_semantics=sem, vmem_limit_bytes=VMEM_LIMIT, **kw)


def _bdot(a, b, dims):
    return lax.dot_general(a.astype(BF16), b.astype(BF16), (dims, ((), ())), preferred_element_type=F32)


def _fdot(a, b, dims):
    return lax.dot_general(a, b, (dims, ((), ())), precision=HI, preferred_element_type=F32)


NN = ((1,), (0,))
NT = ((1,), (1,))
TN = ((0,), (0,))


def _sigmoid(x):
    return 1.0 / (1.0 + jnp.exp(-x))


def _softplus(x):
    return jnp.maximum(x, 0.0) + jnp.log(1.0 + jnp.exp(-jnp.abs(x)))


def _iota2(shape, axis):
    return lax.broadcasted_iota(jnp.int32, shape, axis)


MM_FULL_K = 4096
MM_BLOCK_BYTES = 4 * 1024 * 1024


def _matmul(a, b, *, ta=False, tb=False, out_dtype=F32, res=None, name, n=None, tm=None, tn=None, tk=None,
            b_spec=None, o_spec=None, o_shape=None):
    a_segs = list(a) if isinstance(a, (list, tuple)) else [a]
    b_segs = list(b) if isinstance(b, (list, tuple)) else [b]
    a0, b0 = a_segs[0], b_segs[0]
    M = a0.shape[1] if ta else a0.shape[0]
    K = a0.shape[0] if ta else sum(s.shape[1] for s in a_segs)
    if n is None:
        n = b0.shape[0] if tb else sum(s.shape[1] for s in b_segs)
    N = n
    dims = ((0,) if ta else (1,), (1,) if tb else (0,))
    has_res = res is not None
    flat = lambda v: v.reshape(-1, v.shape[-1])
    o_shape = o_shape or jax.ShapeDtypeStruct((M, N), out_dtype)

    def seg_specs(segs, tile, block, pos):
        specs, ranges, off = [], [], 0
        for s in segs:
            cnt = s.shape[1] // tile
            assert s.shape[1] % tile == 0, (name, s.shape, tile)

            def imap(*g, off=off, cnt=cnt):
                t = jnp.clip(g[pos] - off, 0, cnt - 1)
                return (g[0], t) if pos == 2 else (0, t)

            specs.append(pl.BlockSpec(block, imap))
            ranges.append((off, off + cnt))
            off += cnt
        return specs, ranges

    if K <= MM_FULL_K:
        assert len(a_segs) == 1
        tm = tm or min(M, 1024, max(256, MM_BLOCK_BYTES // (K * a0.dtype.itemsize)))
        tn = tn or min(N, 512)
        assert M % tm == 0 and N % tn == 0, (name, M, N, K, tm, tn)
        nb = len(b_segs)
        if b_spec is not None:
            b_specs, b_ranges = [b_spec], [(0, N // tn)]
        elif nb > 1:
            assert not tb
            b_specs, b_ranges = seg_specs(b_segs, tn, (K, tn), 1)
        else:
            b_specs = [pl.BlockSpec((tn, K), lambda i, j: (j, 0)) if tb else pl.BlockSpec((K, tn), lambda i, j: (0, j))]
            b_ranges = [(0, N // tn)]

        def body_full(*refs):
            a_ref, b_refs = refs[0], refs[1:1 + nb]
            r_ref = refs[1 + nb] if has_res else None
            o_ref = refs[-1]
            j = pl.program_id(1)
            for b_ref, (lo, hi) in zip(b_refs, b_ranges):
                def emit(b_ref=b_ref):
                    r = _bdot(a_ref[...], flat(b_ref[...]), dims)
                    if has_res:
                        r = r + r_ref[...]
                    o_ref[...] = r.astype(o_ref.dtype).reshape(o_ref.shape)
                if nb == 1:
                    emit()
                else:
                    pl.when(jnp.logical_and(j >= lo, j < hi))(emit)

        a_spec = pl.BlockSpec((K, tm), lambda i, j: (0, i)) if ta else pl.BlockSpec((tm, K), lambda i, j: (i, 0))
        o_spec = o_spec or pl.BlockSpec((tm, tn), lambda i, j: (i, j))
        r_spec = [pl.BlockSpec((tm, tn), lambda i, j: (i, j))] if has_res else []
        return pl.pallas_call(
            body_full, grid=(M // tm, N // tn), in_specs=[a_spec] + b_specs + r_spec, out_specs=o_spec, out_shape=o_shape,
            compiler_params=_cp(("parallel", "arbitrary")), name=name)(*([a0] + b_segs + ([res] if has_res else [])))

    assert tb and not ta and len(b_segs) == 1
    tm, tn = tm or min(M, 1024), tn or min(N, 1024)
    tk = tk or (1024 if all(s.shape[1] % 1024 == 0 for s in a_segs) else 512)
    assert M % tm == 0 and N % tn == 0 and K % tk == 0, (name, M, N, K, tm, tn, tk)
    nk = K // tk
    na = len(a_segs)
    if na > 1:
        a_specs, a_ranges = seg_specs(a_segs, tk, (tm, tk), 2)
    else:
        a_specs, a_ranges = [pl.BlockSpec((tm, tk), lambda i, j, k: (i, k))], [(0, nk)]
    b_spec = b_spec or pl.BlockSpec((tn, tk), lambda i, j, k: (j, k))

    def body(*refs):
        a_refs, b_ref = refs[:na], refs[na]
        r_ref = refs[na + 1] if has_res else None
        o_ref, acc = refs[-2], refs[-1]
        k = pl.program_id(2)

        @pl.when(k == 0)
        def _():
            acc[...] = jnp.zeros_like(acc)

        for a_ref, (lo, hi) in zip(a_refs, a_ranges):
            def emit(a_ref=a_ref):
                acc[...] += _bdot(a_ref[...], flat(b_ref[...]), dims)
            if na == 1:
                emit()
            else:
                pl.when(jnp.logical_and(k >= lo, k < hi))(emit)

        @pl.when(k == nk - 1)
        def _():
            r = acc[...]
            if has_res:
                r = r + r_ref[...]
            o_ref[...] = r.astype(o_ref.dtype).reshape(o_ref.shape)

    o_spec = o_spec or pl.BlockSpec((tm, tn), lambda i, j, k: (i, j))
    r_spec = [pl.BlockSpec((tm, tn), lambda i, j, k: (i, j))] if has_res else []
    return pl.pallas_call(
        body, grid=(M // tm, N // tn, nk), in_specs=a_specs + [b_spec] + r_spec, out_specs=o_spec, out_shape=o_shape,
        scratch_shapes=[pltpu.VMEM((tm, tn), F32)],
        compiler_params=_cp(("parallel", "parallel", "arbitrary")), name=name)(*(a_segs + [b0] + ([res] if has_res else [])))


def _rmsnorm_fwd(x, g, *, name, tm=256):
    S, Dm = x.shape
    tm = min(tm, S)

    def body(x_ref, g_ref, o_ref):
        xv = x_ref[...]
        r = lax.rsqrt(jnp.mean(xv * xv, axis=-1, keepdims=True) + EPS)
        o_ref[...] = (xv * r * g_ref[...]).astype(BF16)

    return pl.pallas_call(
        body, grid=(S // tm,), in_specs=[pl.BlockSpec((tm, Dm), lambda i: (i, 0)), pl.BlockSpec((1, Dm), lambda i: (0, 0))],
        out_specs=pl.BlockSpec((tm, Dm), lambda i: (i, 0)), out_shape=jax.ShapeDtypeStruct((S, Dm), BF16),
        compiler_params=_cp(("parallel",)), name=name)(x, g.reshape(1, Dm))


def _rmsnorm_bwd(dh, x, g, dres, *, name, tm=256):
    S, Dm = x.shape
    tm = min(tm, S)
    want_dx = dres is not None

    def body(*refs):
        if want_dx:
            dh_ref, x_ref, g_ref, dr_ref, dx_ref, dg_ref = refs
        else:
            dh_ref, x_ref, g_ref, dg_ref = refs
        i = pl.program_id(0)
        xv = x_ref[...]
        dhv = dh_ref[...]
        r = lax.rsqrt(jnp.mean(xv * xv, axis=-1, keepdims=True) + EPS)
        y = xv * r
        part = jnp.sum(dhv * y, axis=0, keepdims=True)

        @pl.when(i == 0)
        def _():
            dg_ref[...] = jnp.zeros_like(dg_ref)

        dg_ref[...] += part
        if want_dx:
            dy = dhv * g_ref[...]
            dx_ref[...] = dr_ref[...] + r * (dy - y * jnp.mean(dy * y, axis=-1, keepdims=True))

    row = pl.BlockSpec((tm, Dm), lambda i: (i, 0))
    vec = pl.BlockSpec((1, Dm), lambda i: (0, 0))
    if want_dx:
        dx, dg = pl.pallas_call(
            body, grid=(S // tm,), in_specs=[row, row, vec, row], out_specs=(row, vec),
            out_shape=(jax.ShapeDtypeStruct((S, Dm), F32), jax.ShapeDtypeStruct((1, Dm), F32)),
            compiler_params=_cp(("arbitrary",)), name=name)(dh, x, g.reshape(1, Dm), dres)
        return dx, dg
    dg = pl.pallas_call(
        body, grid=(S // tm,), in_specs=[row, row, vec], out_specs=vec,
        out_shape=jax.ShapeDtypeStruct((1, Dm), F32), compiler_params=_cp(("arbitrary",)), name=name)(dh, x, g.reshape(1, Dm))
    return None, dg


GATE_TN = XA_WIDTH
GATE_MIX_TILES = MIX_WIDTH // GATE_TN


def _gate_cat_specs(tm):
    return [pl.BlockSpec((tm, GATE_TN), lambda i, j: (i, jnp.minimum(j, GATE_MIX_TILES - 1))),
            pl.BlockSpec((tm, GATE_TN), lambda i, j: (i, 0))]


def _gate_fwd(mix, xa, proj, z_off, *, name, tm=256):
    S = mix.shape[0]
    tm = min(tm, S)
    zb = z_off // GATE_TN

    def body(m_ref, x_ref, z_ref, y_ref):
        z = z_ref[...]
        c = jnp.where(pl.program_id(1) < GATE_MIX_TILES, m_ref[...], x_ref[...])
        y_ref[...] = (c * z * _sigmoid(z)).astype(BF16)

    blk = pl.BlockSpec((tm, GATE_TN), lambda i, j: (i, j))
    return pl.pallas_call(
        body, grid=(S // tm, INNER // GATE_TN),
        in_specs=_gate_cat_specs(tm) + [pl.BlockSpec((tm, GATE_TN), lambda i, j: (i, zb + j))],
        out_specs=blk, out_shape=jax.ShapeDtypeStruct((S, INNER), BF16),
        compiler_params=_cp(("parallel", "arbitrary")), name=name)(mix, xa, proj)


def _gate_bwd(dy, mix, xa, proj, z_off, *, name, tm=256):
    S = mix.shape[0]
    tm = min(tm, S)
    zb = z_off // GATE_TN

    def body(dy_ref, m_ref, x_ref, z_ref, dc_ref, dz_ref):
        z = z_ref[...]
        sg = _sigmoid(z)
        d = dy_ref[...]
        c = jnp.where(pl.program_id(1) < GATE_MIX_TILES, m_ref[...], x_ref[...])
        dc_ref[...] = d * z * sg
        dz_ref[...] = (d * c * sg * (1.0 + z * (1.0 - sg))).astype(BF16)

    blk = pl.BlockSpec((tm, GATE_TN), lambda i, j: (i, j))
    return pl.pallas_call(
        body, grid=(S // tm, INNER // GATE_TN),
        in_specs=[blk] + _gate_cat_specs(tm) + [pl.BlockSpec((tm, GATE_TN), lambda i, j: (i, zb + j))], out_specs=(blk, blk),
        out_shape=(jax.ShapeDtypeStruct((S, INNER), F32), jax.ShapeDtypeStruct((S, INNER), BF16)),
        compiler_params=_cp(("parallel", "arbitrary")), name=name)(dy, mix, xa, proj)


def _loss_head(x, target, *, name, tm=256):
    S, Dm = x.shape
    tm = min(tm, S)

    nt = S // tm

    def body(x_ref, t_ref, dx_ref, l_ref, acc):
        i = pl.program_id(0)
        e = x_ref[...] - t_ref[...]
        dx_ref[...] = e * (1.0 / Dm)

        @pl.when(i == 0)
        def _():
            acc[...] = jnp.zeros_like(acc)

        acc[...] += jnp.sum(e * e, axis=0, keepdims=True) * (0.5 / Dm)

        @pl.when(i == nt - 1)
        def _():
            l_ref[...] = jnp.sum(acc[...], axis=1, keepdims=True) + jnp.zeros((1, LANE), F32)

    row = pl.BlockSpec((tm, Dm), lambda i: (i, 0))
    return pl.pallas_call(
        body, grid=(nt,), in_specs=[row, row], out_specs=(row, pl.BlockSpec((1, LANE), lambda i: (0, 0))),
        out_shape=(jax.ShapeDtypeStruct((S, Dm), F32), jax.ShapeDtypeStruct((1, LANE), F32)),
        scratch_shapes=[pltpu.VMEM((1, Dm), F32)],
        compiler_params=_cp(("arbitrary",)), name=name)(x, target)


def _xa_norm(v, g):
    r = lax.rsqrt(jnp.mean(v * v, axis=-1, keepdims=True) + EPS)
    return v * r, r


def _xa_fwd(proj, xq_off, kv, gq, gk, *, name, tm=512):
    S = proj.shape[0]
    tm = min(tm, S)
    qb = xq_off // XA_DIM
    n_mem = kv.shape[0]
    scale = XA_DIM ** -0.5

    def body(q_ref, k_ref, v_ref, gq_ref, gk_ref, o_ref):
        qh, _ = _xa_norm(q_ref[...], None)
        kh, _ = _xa_norm(k_ref[...], None)
        qn = qh * gq_ref[...]
        kn = kh * gk_ref[...]
        s = _bdot(qn, kn, NT) * scale
        s = s - jnp.max(s, axis=-1, keepdims=True)
        p = jnp.exp(s)
        p = p / jnp.sum(p, axis=-1, keepdims=True)
        o_ref[...] = _bdot(p, v_ref[...], NN)

    vec = pl.BlockSpec((1, XA_DIM), lambda h, i: (0, 0))
    return pl.pallas_call(
        body, grid=(XA_HEADS, S // tm),
        in_specs=[pl.BlockSpec((tm, XA_DIM), lambda h, i: (i, qb + h)),
                  pl.BlockSpec((n_mem, XA_DIM), lambda h, i: (0, h)),
                  pl.BlockSpec((n_mem, XA_DIM), lambda h, i: (0, XA_HEADS + h)), vec, vec],
        out_specs=pl.BlockSpec((tm, XA_DIM), lambda h, i: (i, h)),
        out_shape=jax.ShapeDtypeStruct((S, XA_WIDTH), F32),
        compiler_params=_cp(("parallel", "parallel")), name=name)(proj, kv, kv, gq.reshape(1, XA_DIM), gk.reshape(1, XA_DIM))


def _xa_bwd(dcat, proj, xq_off, kv, gq, gk, *, name, tm=512):
    S = proj.shape[0]
    tm = min(tm, S)
    nt = S // tm
    qb = xq_off // XA_DIM
    db = MIX_WIDTH // XA_DIM
    n_mem = kv.shape[0]
    scale = XA_DIM ** -0.5

    def body(d_ref, q_ref, k_ref, v_ref, gq_ref, gk_ref, dq_ref, dk_ref, dv_ref, dgq_ref, dgk_ref, dkn_acc):
        h = pl.program_id(0)
        i = pl.program_id(1)
        q = q_ref[...]
        k = k_ref[...]
        qh, rq = _xa_norm(q, None)
        kh, rk = _xa_norm(k, None)
        gqv = gq_ref[...]
        gkv = gk_ref[...]
        qn = qh * gqv
        kn = kh * gkv
        s = _bdot(qn, kn, NT) * scale
        s = s - jnp.max(s, axis=-1, keepdims=True)
        p = jnp.exp(s)
        p = p / jnp.sum(p, axis=-1, keepdims=True)
        d = d_ref[...]
        dp = _bdot(d, v_ref[...], NT)
        ds = p * (dp - jnp.sum(dp * p, axis=-1, keepdims=True)) * scale
        dqn = _bdot(ds, kn, NN)

        @pl.when(i == 0)
        def _():
            dkn_acc[...] = jnp.zeros_like(dkn_acc)
            dv_ref[...] = jnp.zeros_like(dv_ref)

        @pl.when(jnp.logical_and(i == 0, h == 0))
        def _():
            dgq_ref[...] = jnp.zeros_like(dgq_ref)
            dgk_ref[...] = jnp.zeros_like(dgk_ref)

        dkn_acc[...] += _bdot(ds, qn, TN)
        dv_ref[...] += _bdot(p, d, TN)
        dgq_ref[...] += jnp.sum(dqn * qh, axis=0, keepdims=True)
        dy = dqn * gqv
        dq_ref[...] = (rq * (dy - qh * jnp.mean(dy * qh, axis=-1, keepdims=True))).astype(BF16)

        @pl.when(i == nt - 1)
        def _():
            dkn = dkn_acc[...]
            dgk_ref[...] += jnp.sum(dkn * kh, axis=0, keepdims=True)
            dyk = dkn * gkv
            dk_ref[...] = rk * (dyk - kh * jnp.mean(dyk * kh, axis=-1, keepdims=True))

    vec = pl.BlockSpec((1, XA_DIM), lambda h, i: (0, 0))
    kblk = pl.BlockSpec((n_mem, XA_DIM), lambda h, i: (0, h))
    vblk = pl.BlockSpec((n_mem, XA_DIM), lambda h, i: (0, XA_HEADS + h))
    dq, dk, dv, dgq, dgk = pl.pallas_call(
        body, grid=(XA_HEADS, nt),
        in_specs=[pl.BlockSpec((tm, XA_DIM), lambda h, i: (i, db + h)),
                  pl.BlockSpec((tm, XA_DIM), lambda h, i: (i, qb + h)), kblk, vblk, vec, vec],
        out_specs=(pl.BlockSpec((tm, XA_DIM), lambda h, i: (i, h)), kblk, kblk, vec, vec),
        out_shape=(jax.ShapeDtypeStruct((S, XA_WIDTH), BF16), jax.ShapeDtypeStruct((n_mem, XA_WIDTH), F32),
                   jax.ShapeDtypeStruct((n_mem, XA_WIDTH), F32), jax.ShapeDtypeStruct((1, XA_DIM), F32),
                   jax.ShapeDtypeStruct((1, XA_DIM), F32)),
        scratch_shapes=[pltpu.VMEM((n_mem, XA_DIM), F32)],
        compiler_params=_cp(("arbitrary", "arbitrary")), name=name)(
            dcat, proj, kv, kv, gq.reshape(1, XA_DIM), gk.reshape(1, XA_DIM))
    return dq, jnp.concatenate([dk, dv], axis=1), dgq, dgk


SB_TQ = 256
SB_TK = 256
SB_HEADS = 24


def _sb_tile(qi, kj, t0, s0, masked):
    z = _bdot(qi, kj, NT)
    sp = _softplus(z)
    ls = z - sp
    if not masked:
        return -sp, ls, None
    mask = (s0 + _iota2(z.shape, 1)) < (t0 + _iota2(z.shape, 0))
    return jnp.where(mask, -sp, 0.0), ls, mask


def _dot2(x, tri):
    hi = x.astype(BF16)
    lo = (x - hi.astype(F32)).astype(BF16)
    dims = (NN, ((), ()))
    return (lax.dot_general(hi, tri, dims, preferred_element_type=F32)
            + lax.dot_general(lo, tri, dims, preferred_element_type=F32))


def _sb_fwd(proj, gq, gk, *, name):
    S = proj.shape[0]
    tq, tk = min(SB_TQ, S), min(SB_TK, S)
    nq = S // tq
    scale = HEAD_DIM ** -0.5

    def body(q_ref, k_ref, v_ref, gq_ref, gk_ref, o_ref, tot_ref, qn_s, kn_s, v_s):
        q = q_ref[...]
        k = k_ref[...]
        qn_s[...] = (q * lax.rsqrt(jnp.mean(q * q, axis=-1, keepdims=True) + EPS) * (gq_ref[...] * scale)).astype(BF16)
        kn_s[...] = (k * lax.rsqrt(jnp.mean(k * k, axis=-1, keepdims=True) + EPS) * gk_ref[...]).astype(BF16)
        v_s[...] = v_ref[...].astype(BF16)
        after = (_iota2((tk, tk), 0) > _iota2((tk, tk), 1)).astype(BF16)

        def qblock(i, _):
            rows = pl.ds(pl.multiple_of(i * tq, tq), tq)
            qi = qn_s[rows, :]
            jd = (i * tq) // tk

            def tile(j, acc, run, masked):
                cols = pl.ds(pl.multiple_of(j * tk, tk), tk)
                lr, ls, mask = _sb_tile(qi, kn_s[cols, :], i * tq, j * tk, masked)
                later = _dot2(lr, after) + run
                a = jnp.exp(ls + later)
                if masked:
                    a = jnp.where(mask, a, 0.0)
                acc = acc + _bdot(a, v_s[cols, :], NN)
                return acc, run + jnp.sum(lr, axis=-1, keepdims=True)

            acc, run = tile(jd, jnp.zeros((tq, HEAD_DIM), F32), jnp.zeros((tq, 1), F32), True)
            acc, run = lax.fori_loop(0, jd, lambda jj, c: tile(jd - 1 - jj, c[0], c[1], False), (acc, run))
            o_ref[rows, :] = acc
            tot_ref[rows, :] = run + jnp.zeros((tq, HEAD_DIM), F32)
            return 0

        lax.fori_loop(0, nq, qblock, 0)

    vec = pl.BlockSpec((1, HEAD_DIM), lambda h: (0, 0))
    out = pl.BlockSpec((S, HEAD_DIM), lambda h: (0, h))
    return pl.pallas_call(
        body, grid=(SB_HEADS,),
        in_specs=[pl.BlockSpec((S, HEAD_DIM), lambda h: (0, h)), pl.BlockSpec((S, HEAD_DIM), lambda h: (0, SB_HEADS + h)),
                  pl.BlockSpec((S, HEAD_DIM), lambda h: (0, 2 * SB_HEADS + h)), vec, vec],
        out_specs=(out, out), out_shape=(jax.ShapeDtypeStruct((S, MIX_WIDTH), F32),) * 2,
        scratch_shapes=[pltpu.VMEM((S, HEAD_DIM), BF16)] * 3,
        compiler_params=_cp(("parallel",)), name=name)(proj, proj, proj, gq.reshape(1, HEAD_DIM), gk.reshape(1, HEAD_DIM))


def _sb_bwd(dmix, tot, proj, gq, gk, *, name):
    S = proj.shape[0]
    tq, tk = min(SB_TQ, S), min(SB_TK, S)
    nq = S // tq
    scale = HEAD_DIM ** -0.5

    def body(do_ref, o_ref, q_ref, k_ref, v_ref, gq_ref, gk_ref, dq_ref, dk_ref, dv_ref, dgq_ref, dgk_ref,
             qn_s, kn_s, v_s, dkn_s, dqn_s, dv_s):
        h = pl.program_id(0)
        q = q_ref[...]
        k = k_ref[...]
        rq = lax.rsqrt(jnp.mean(q * q, axis=-1, keepdims=True) + EPS)
        rk = lax.rsqrt(jnp.mean(k * k, axis=-1, keepdims=True) + EPS)
        gqv = gq_ref[...]
        gkv = gk_ref[...]
        qn_s[...] = (q * rq * (gqv * scale)).astype(BF16)
        kn_s[...] = (k * rk * gkv).astype(BF16)
        v_s[...] = v_ref[...].astype(BF16)
        dkn_s[...] = jnp.zeros_like(dkn_s)
        dv_s[...] = jnp.zeros_like(dv_s)
        r_i = _iota2((tk, tk), 0)
        c_i = _iota2((tk, tk), 1)
        upto = (r_i <= c_i).astype(BF16)
        before = (r_i < c_i).astype(BF16)

        def qblock(i, _):
            rows = pl.ds(pl.multiple_of(i * tq, tq), tq)
            qi = qn_s[rows, :]
            doi = do_ref[rows, :].astype(BF16)
            tot_i = jnp.max(o_ref[rows, :], axis=-1, keepdims=True)
            jd = (i * tq) // tk

            def tile(j, dqn, run, run_b, masked):
                cols = pl.ds(pl.multiple_of(j * tk, tk), tk)
                kj = kn_s[cols, :]
                lr, ls, mask = _sb_tile(qi, kj, i * tq, j * tk, masked)
                later = tot_i - (_dot2(lr, upto) + run)
                a = jnp.exp(ls + later)
                if masked:
                    a = jnp.where(mask, a, 0.0)
                b = _bdot(doi, v_s[cols, :], NT) * a
                cum = _dot2(b, before) + run_b
                beta = jnp.exp(ls)
                dz = b * (1.0 - beta) - cum * beta
                if masked:
                    dz = jnp.where(mask, dz, 0.0)
                dzb = dz.astype(BF16)
                dv_s[cols, :] += _bdot(a, doi, TN)
                dkn_s[cols, :] += _bdot(dzb, qi, TN)
                dqn = dqn + _bdot(dzb, kj, NN)
                return dqn, run + jnp.sum(lr, axis=-1, keepdims=True), run_b + jnp.sum(b, axis=-1, keepdims=True)

            zero1 = jnp.zeros((tq, 1), F32)
            carry = lax.fori_loop(0, jd, lambda j, c: tile(j, c[0], c[1], c[2], False),
                                  (jnp.zeros((tq, HEAD_DIM), F32), zero1, zero1))
            dqn, _, _ = tile(jd, carry[0], carry[1], carry[2], True)
            dqn_s[rows, :] = dqn * scale
            return 0

        lax.fori_loop(0, nq, qblock, 0)

        @pl.when(h == 0)
        def _():
            dgq_ref[...] = jnp.zeros_like(dgq_ref)
            dgk_ref[...] = jnp.zeros_like(dgk_ref)

        dv_ref[...] = dv_s[...].astype(BF16)
        dqn = dqn_s[...]
        qh = q * rq
        dgq_ref[...] += jnp.sum(dqn * qh, axis=0, keepdims=True)
        dy = dqn * gqv
        dq_ref[...] = (rq * (dy - qh * jnp.mean(dy * qh, axis=-1, keepdims=True))).astype(BF16)
        dkn = dkn_s[...]
        kh = k * rk
        dgk_ref[...] += jnp.sum(dkn * kh, axis=0, keepdims=True)
        dyk = dkn * gkv
        dk_ref[...] = (rk * (dyk - kh * jnp.mean(dyk * kh, axis=-1, keepdims=True))).astype(BF16)

    vec = pl.BlockSpec((1, HEAD_DIM), lambda h: (0, 0))
    hb = lambda off: pl.BlockSpec((S, HEAD_DIM), lambda h: (0, off + h))
    dq, dk, dv, dgq, dgk = pl.pallas_call(
        body, grid=(SB_HEADS,),
        in_specs=[hb(0), hb(0), hb(0), hb(SB_HEADS), hb(2 * SB_HEADS), vec, vec],
        out_specs=(hb(0), hb(0), hb(0), vec, vec),
        out_shape=(jax.ShapeDtypeStruct((S, MIX_WIDTH), BF16),) * 3 + (jax.ShapeDtypeStruct((1, HEAD_DIM), F32),) * 2,
        scratch_shapes=[pltpu.VMEM((S, HEAD_DIM), BF16)] * 3 + [pltpu.VMEM((S, HEAD_DIM), F32)] * 3,
        compiler_params=_cp(("arbitrary",)), name=name)(
            dmix, tot, proj, proj, proj, gq.reshape(1, HEAD_DIM), gk.reshape(1, HEAD_DIM))
    return [dq, dk, dv], dgq, dgk


def _shift_down(x, k):
    if k == 0:
        return x
    r = pltpu.roll(x, k, 0)
    return jnp.where(_iota2(x.shape, 0) >= k, r, 0.0)


def _shift_up(x, k):
    if k == 0:
        return x
    n = x.shape[0]
    r = pltpu.roll(x, n - k, 0)
    return jnp.where(_iota2(x.shape, 0) < n - k, r, 0.0)


def _conv(x, w):
    c = w[DN_CONV - 1] * x
    for k in range(1, DN_CONV):
        c = c + w[DN_CONV - 1 - k] * _shift_down(x, k)
    return c


def _dn_pre_fwd(proj, conv_w, col0, ncols, *, l2, scale, name):
    S = proj.shape[0]
    cb = col0 // HEAD_DIM

    def body(x_ref, w_ref, o_ref):
        c = _conv(x_ref[...], [w_ref[k:k + 1, :] for k in range(DN_CONV)])
        a = c * _sigmoid(c)
        if l2:
            a = a * (lax.rsqrt(jnp.sum(a * a, axis=-1, keepdims=True) + EPS) * scale)
        o_ref[...] = a

    return pl.pallas_call(
        body, grid=(ncols // HEAD_DIM,),
        in_specs=[pl.BlockSpec((S, HEAD_DIM), lambda j: (0, cb + j)), pl.BlockSpec((DN_CONV, HEAD_DIM), lambda j: (0, cb + j))],
        out_specs=pl.BlockSpec((S, HEAD_DIM), lambda j: (0, j)), out_shape=jax.ShapeDtypeStruct((S, ncols), F32),
        compiler_params=_cp(("parallel",)), name=name)(proj, conv_w)


def _dn_pre_bwd(dout, proj, conv_w, col0, ncols, *, l2, scale, name):
    S = proj.shape[0]
    cb = col0 // HEAD_DIM
    dw_in = HEAD_DIM

    def body(d_ref, x_ref, w_ref, dx_ref, dw_ref):
        x = x_ref[...]
        w = [w_ref[k:k + 1, :] for k in range(DN_CONV)]
        c = _conv(x, w)
        sg = _sigmoid(c)
        a = c * sg
        d = d_ref[...]
        if l2:
            r = lax.rsqrt(jnp.sum(a * a, axis=-1, keepdims=True) + EPS)
            y = a * r
            d = d * scale
            d = r * (d - y * jnp.sum(d * y, axis=-1, keepdims=True))
        dc = d * sg * (1.0 + c * (1.0 - sg))
        dx = w[DN_CONV - 1] * dc
        for k in range(1, DN_CONV):
            dx = dx + w[DN_CONV - 1 - k] * _shift_up(dc, k)
        dx_ref[...] = dx.astype(BF16)
        for k in range(DN_CONV):
            dw_ref[3 - k:4 - k, :] = jnp.sum(dc * _shift_down(x, k), axis=0, keepdims=True)

    return pl.pallas_call(
        body, grid=(ncols // HEAD_DIM,),
        in_specs=[pl.BlockSpec((S, dw_in), lambda j: (0, j)), pl.BlockSpec((S, HEAD_DIM), lambda j: (0, cb + j)),
                  pl.BlockSpec((DN_CONV, HEAD_DIM), lambda j: (0, cb + j))],
        out_specs=(pl.BlockSpec((S, HEAD_DIM), lambda j: (0, j)), pl.BlockSpec((DN_CONV, HEAD_DIM), lambda j: (0, j))),
        out_shape=(jax.ShapeDtypeStruct((S, ncols), BF16), jax.ShapeDtypeStruct((DN_CONV, ncols), F32)),
        compiler_params=_cp(("parallel",)), name=name)(dout, proj, conv_w)


def _dn_ab_fwd(proj, a_log, dt_bias, *, name, tm=512):
    S = proj.shape[0]
    tm = min(tm, S)
    ab = P0_AB // LANE

    def body(a_ref, b_ref, al_ref, dt_ref, g_ref, be_ref):
        g_ref[...] = -jnp.exp(al_ref[...]) * _softplus(a_ref[...] + dt_ref[...])
        be_ref[...] = _sigmoid(b_ref[...])

    vec = pl.BlockSpec((1, LANE), lambda i: (0, 0))
    out = pl.BlockSpec((tm, LANE), lambda i: (i, 0))
    return pl.pallas_call(
        body, grid=(S // tm,),
        in_specs=[pl.BlockSpec((tm, LANE), lambda i: (i, ab)), pl.BlockSpec((tm, LANE), lambda i: (i, ab + 1)), vec, vec],
        out_specs=(out, out), out_shape=(jax.ShapeDtypeStruct((S, LANE), F32),) * 2,
        compiler_params=_cp(("parallel",)), name=name)(proj, proj, a_log, dt_bias)


def _dn_ab_bwd(dg, dbeta, proj, a_log, dt_bias, *, name, tm=512):
    S = proj.shape[0]
    tm = min(tm, S)
    ab = P0_AB // LANE

    def body(dg_ref, db_ref, a_ref, b_ref, al_ref, dt_ref, dab_ref, dal_ref, ddt_ref):
        i = pl.program_id(0)
        ea = jnp.exp(al_ref[...])
        u = a_ref[...] + dt_ref[...]
        dgv = dg_ref[...]
        da = dgv * (-ea) * _sigmoid(u)
        be = _sigmoid(b_ref[...])
        dab_ref[:, 0:LANE] = da.astype(BF16)
        dab_ref[:, LANE:2 * LANE] = (db_ref[...] * be * (1.0 - be)).astype(BF16)
        dab_ref[:, 2 * LANE:] = jnp.zeros((tm, 2 * LANE), BF16)

        @pl.when(i == 0)
        def _():
            dal_ref[...] = jnp.zeros_like(dal_ref)
            ddt_ref[...] = jnp.zeros_like(ddt_ref)

        dal_ref[...] += jnp.sum(dgv * (-ea) * _softplus(u), axis=0, keepdims=True)
        ddt_ref[...] += jnp.sum(da, axis=0, keepdims=True)

    vec = pl.BlockSpec((1, LANE), lambda i: (0, 0))
    row = pl.BlockSpec((tm, LANE), lambda i: (i, 0))
    return pl.pallas_call(
        body, grid=(S // tm,),
        in_specs=[row, row, pl.BlockSpec((tm, LANE), lambda i: (i, ab)), pl.BlockSpec((tm, LANE), lambda i: (i, ab + 1)), vec, vec],
        out_specs=(pl.BlockSpec((tm, 4 * LANE), lambda i: (i, 0)), vec, vec),
        out_shape=(jax.ShapeDtypeStruct((S, 4 * LANE), BF16), jax.ShapeDtypeStruct((1, LANE), F32),
                   jax.ShapeDtypeStruct((1, LANE), F32)),
        compiler_params=_cp(("arbitrary",)), name=name)(dg, dbeta, proj, proj, a_log, dt_bias)


def _dot3(a, b):
    ah = a.astype(BF16)
    al = (a - ah.astype(F32)).astype(BF16)
    bh = b.astype(BF16)
    bl = (b - bh.astype(F32)).astype(BF16)
    d = lambda u, v: lax.dot_general(u, v, (NN, ((), ())), preferred_element_type=F32)
    return d(ah, bh) + (d(ah, bl) + d(al, bh))


DN_PAIR = 2


def _pdot(a, b, dims, dot=None):
    dot = dot or _bdot
    return jnp.stack([dot(a[i] if a.ndim == 3 else a, b[i] if b.ndim == 3 else b, dims) for i in range(DN_PAIR)])


def _tri_inverse(a):
    eye = (_iota2((CH, CH), 0) == _iota2((CH, CH), 1)).astype(F32)
    d3 = lambda u, v: jnp.stack([_dot3(u[i], v[i]) for i in range(DN_PAIR)])
    t = eye - a
    x = d3(a, a)
    n = 2
    while True:
        t = t + d3(t, x)
        n *= 2
        if n >= CH:
            break
        x = d3(x, x)
    return t


def _pick_col(m, n):
    return jnp.sum(jnp.where(_iota2(m.shape, 2) == n, m, 0.0), axis=2, keepdims=True)


def _dn_chunk_common(kk, qk, gc_c, gc_r, be_c):
    r_i = _iota2((CH, CH), 0)
    c_i = _iota2((CH, CH), 1)
    incl = r_i >= c_i
    strict = r_i > c_i
    dec = jnp.exp(jnp.where(incl, gc_c - gc_r, -1e30))
    e = jnp.exp(gc_c)
    gl = jnp.sum(jnp.where(_iota2((1, CH), 1) == CH - 1, gc_r, 0.0), axis=-1, keepdims=True)
    kds = jnp.exp(gl - gc_c)
    cd = jnp.exp(gl)
    a = jnp.where(strict, be_c * kk * dec, 0.0)
    p = qk * dec
    return dict(incl=incl, strict=strict, dec=dec, e=e, kds=kds, cd=cd, kk=kk, a=a, qk=qk, p=p)


def _dn_decay_tables(g_ref, b_ref, gcr, gcc, bcc):
    r_i = _iota2((CH, CH), 0)
    c_i = _iota2((CH, CH), 1)
    lc = (r_i >= c_i).astype(F32)
    eye = (r_i == c_i).astype(F32)
    for hh in range(DN_PAIR):
        g_rows_v = g_ref[hh]
        gcr[hh] = _fdot(g_rows_v, lc, NT)
        gcc[hh] = _fdot(lc, g_rows_v, NT)
        bcc[hh] = _fdot(eye, b_ref[hh], NT)
    return lc


def _dn_core_fwd(qn, kn, vc, g_rows, b_rows, out_g, *, name):
    S = qn.shape[0]
    nc = S // CH

    def body(q_ref, k_ref, v_ref, g_ref, b_ref, og_ref, o_ref, st_ref, t_ref, gcr, gcc, bcc):
        _dn_decay_tables(g_ref, b_ref, gcr, gcc, bcc)
        ogv = og_ref[...]

        def chunk(n, states):
            rows = pl.ds(pl.multiple_of(n * CH, CH), CH)
            q = q_ref[rows, :]
            k = k_ref[rows, :]
            kk = _bdot(k, k, NT)
            qk = _bdot(q, k, NT)
            v = jnp.stack([v_ref[rows, hh * HEAD_DIM:(hh + 1) * HEAD_DIM] for hh in range(DN_PAIR)])
            gc_c = _pick_col(gcc[...], n)
            be_c = _pick_col(bcc[...], n)
            gc_r = gcr[:, pl.ds(n, 1), :]
            c = _dn_chunk_common(kk, qk, gc_c, gc_r, be_c)
            t = _tri_inverse(c["a"])
            u0 = _pdot(t, be_c * v, NN)
            w = _pdot(t, (be_c * c["e"]) * k, NN)
            u = u0 - _pdot(w, states, NN)
            o = _pdot(c["e"] * q, states, NN) + _pdot(c["p"], u, NN)
            on = o * lax.rsqrt(jnp.mean(o * o, axis=-1, keepdims=True) + EPS) * ogv
            for hh in range(DN_PAIR):
                st_ref[hh, n] = states[hh]
                t_ref[hh, n] = t[hh]
                o_ref[rows, hh * HEAD_DIM:(hh + 1) * HEAD_DIM] = on[hh]
            return c["cd"] * states + _pdot(c["kds"] * k, u, TN)

        lax.fori_loop(0, nc, chunk, jnp.zeros((DN_PAIR, HEAD_DIM, HEAD_DIM), F32))

    qk_spec = pl.BlockSpec((S, HEAD_DIM), lambda h: (0, h))
    v_spec = pl.BlockSpec((S, DN_PAIR * HEAD_DIM), lambda h: (0, h))
    rows_spec = pl.BlockSpec((DN_PAIR, LANE, CH), lambda h: (h, 0, 0))
    return pl.pallas_call(
        body, grid=(DN_V_HEADS // DN_PAIR,),
        in_specs=[qk_spec, qk_spec, v_spec, rows_spec, rows_spec, pl.BlockSpec((1, HEAD_DIM), lambda h: (0, 0))],
        out_specs=(v_spec, pl.BlockSpec((DN_PAIR, nc, HEAD_DIM, HEAD_DIM), lambda h: (h, 0, 0, 0)),
                   pl.BlockSpec((DN_PAIR, nc, CH, CH), lambda h: (h, 0, 0, 0))),
        out_shape=(jax.ShapeDtypeStruct((S, MIX_WIDTH), F32), jax.ShapeDtypeStruct((DN_V_HEADS, nc, HEAD_DIM, HEAD_DIM), F32),
                   jax.ShapeDtypeStruct((DN_V_HEADS, nc, CH, CH), F32)),
        scratch_shapes=[pltpu.VMEM((DN_PAIR, LANE, CH), F32), pltpu.VMEM((DN_PAIR, CH, LANE), F32),
                        pltpu.VMEM((DN_PAIR, CH, LANE), F32)],
        compiler_params=_cp(("parallel",)), name=name)(qn, kn, vc, g_rows, b_rows, out_g.reshape(1, HEAD_DIM))


def _dn_chunk_bwd(q, k, v, kk, qk, state, t, gc_c, gc_r, be_c, don, ogv, ds_next):
    ones = jnp.ones((CH, LANE), F32)
    last_row = _iota2((CH, 1), 0) == CH - 1
    rowsum = lambda z: jnp.sum(z, axis=-1, keepdims=True)
    colsum = lambda z: jnp.sum(z, axis=-2, keepdims=True)
    c = _dn_chunk_common(kk, qk, gc_c, gc_r, be_c)
    e, kds, cd, dec, a, p = c["e"], c["kds"], c["cd"], c["dec"], c["a"], c["p"]
    vb = be_c * v
    kbe = (be_c * e) * k
    u0 = _pdot(t, vb, NN)
    w = _pdot(t, kbe, NN)
    u = u0 - _pdot(w, state, NN)
    qd = e * q
    kd = kds * k
    o = _pdot(qd, state, NN) + _pdot(p, u, NN)
    r = lax.rsqrt(jnp.mean(o * o, axis=-1, keepdims=True) + EPS)
    y = o * r
    dog = colsum(don * y)
    dy = don * ogv
    d_o = r * (dy - y * jnp.mean(dy * y, axis=-1, keepdims=True))
    du = _pdot(p, d_o, TN) + _pdot(kd, ds_next, NN)
    dqd = _pdot(d_o, state, NT)
    dstate = _pdot(qd, d_o, TN) + cd * ds_next - _pdot(w, du, TN)
    dcd = colsum(rowsum(ds_next * state))
    dkd = _pdot(u, ds_next, NT)
    dw = -_pdot(du, state, NT)
    dvb = _pdot(t, du, TN)
    dkbe = _pdot(t, dw, TN)
    da = -jnp.where(c["strict"], _pdot(dvb, u0, NT) + _pdot(dkbe, w, NT), 0.0)
    dp = jnp.where(c["incl"], _pdot(d_o, u, NT), 0.0)
    gmat = da * a + dp * p
    dad = da * dec
    x = be_c * dad
    dpd = dp * dec
    dk = _pdot(x, k, NN) + _pdot(x, k, TN) + _pdot(dpd, q, TN)
    dq = _pdot(dpd, k, NN) + e * dqd
    dbe = rowsum(dad * c["kk"])
    dgc = rowsum(gmat) + rowsum(dqd * q) * e
    rk = rowsum(dkd * k) * kds
    dk = dk + kds * dkd
    dgc = dgc - rk
    dgl = colsum(rk) + dcd * cd
    sk = rowsum(dkbe * k)
    dk = dk + (be_c * e) * dkbe
    dbe = dbe + sk * e + rowsum(dvb * v)
    dgc = dgc + sk * be_c * e
    dgc = dgc + jnp.where(last_row, dgl, 0.0)
    dgc = dgc - _pdot(gmat, ones, TN, dot=_fdot)
    return dq, dk, be_c * dvb, dgc, dbe, dog, dstate


def _dn_core_bwd(dmix, qn, kn, vc, g_rows, b_rows, out_g, states, tinv, *, name):
    S = qn.shape[0]
    nc = S // CH

    def body(do_ref, q_ref, k_ref, v_ref, g_ref, b_ref, og_ref, st_ref, t_ref,
             dq_ref, dk_ref, dv_ref, dg_ref, db_ref, dog_ref, gcr, gcc, bcc, dgc_acc):
        h = pl.program_id(0)
        lc = _dn_decay_tables(g_ref, b_ref, gcr, gcc, bcc)
        ogv = og_ref[...]
        dgc_acc[...] = jnp.zeros_like(dgc_acc)
        db_ref[...] = jnp.zeros_like(db_ref)
        lane_n = _iota2((CH, LANE), 1)

        @pl.when(h == 0)
        def _():
            dog_ref[...] = jnp.zeros_like(dog_ref)

        def chunk(m, carry):
            ds_nexts, dog = carry
            n = nc - 1 - m
            rows = pl.ds(pl.multiple_of(n * CH, CH), CH)
            q = q_ref[rows, :]
            k = k_ref[rows, :]
            kk = _bdot(k, k, NT)
            qk = _bdot(q, k, NT)
            heads = lambda ref: jnp.stack([ref[rows, hh * HEAD_DIM:(hh + 1) * HEAD_DIM] for hh in range(DN_PAIR)])
            state = jnp.stack([st_ref[hh, n] for hh in range(DN_PAIR)])
            t = jnp.stack([t_ref[hh, n] for hh in range(DN_PAIR)])
            dq, dk, dv, dgc, dbe, dog_h, dstate = _dn_chunk_bwd(
                q, k, heads(v_ref), kk, qk, state, t, _pick_col(gcc[...], n), gcr[:, pl.ds(n, 1), :],
                _pick_col(bcc[...], n), heads(do_ref), ogv, ds_nexts)
            for hh in range(DN_PAIR):
                dv_ref[rows, hh * HEAD_DIM:(hh + 1) * HEAD_DIM] = dv[hh]
            dgc_acc[...] = jnp.where(lane_n == n, dgc, dgc_acc[...])
            db_ref[...] = jnp.where(lane_n == n, dbe, db_ref[...])
            dq_ref[rows, :] = jnp.sum(dq, axis=0)
            dk_ref[rows, :] = jnp.sum(dk, axis=0)
            return dstate, dog + jnp.sum(dog_h, axis=0)

        _, dog = lax.fori_loop(0, nc, chunk, (jnp.zeros((DN_PAIR, HEAD_DIM, HEAD_DIM), F32), jnp.zeros((1, HEAD_DIM), F32)))
        dog_ref[...] += dog
        for hh in range(DN_PAIR):
            dg_ref[hh] = _fdot(lc, dgc_acc[hh], TN)

    qk_spec = pl.BlockSpec((S, HEAD_DIM), lambda h: (0, h))
    v_spec = pl.BlockSpec((S, DN_PAIR * HEAD_DIM), lambda h: (0, h))
    rows_spec = pl.BlockSpec((DN_PAIR, LANE, CH), lambda h: (h, 0, 0))
    cols_spec = pl.BlockSpec((DN_PAIR, CH, LANE), lambda h: (h, 0, 0))
    vec = pl.BlockSpec((1, HEAD_DIM), lambda h: (0, 0))
    qk_out = jax.ShapeDtypeStruct((S, DN_QK_WIDTH), F32)
    return pl.pallas_call(
        body, grid=(DN_V_HEADS // DN_PAIR,),
        in_specs=[v_spec, qk_spec, qk_spec, v_spec, rows_spec, rows_spec, vec,
                  pl.BlockSpec((DN_PAIR, nc, HEAD_DIM, HEAD_DIM), lambda h: (h, 0, 0, 0)),
                  pl.BlockSpec((DN_PAIR, nc, CH, CH), lambda h: (h, 0, 0, 0))],
        out_specs=(qk_spec, qk_spec, v_spec, cols_spec, cols_spec, vec),
        out_shape=(qk_out, qk_out, jax.ShapeDtypeStruct((S, MIX_WIDTH), F32), jax.ShapeDtypeStruct((DN_V_HEADS, CH, LANE), F32),
                   jax.ShapeDtypeStruct((DN_V_HEADS, CH, LANE), F32), jax.ShapeDtypeStruct((1, HEAD_DIM), F32)),
        scratch_shapes=[pltpu.VMEM((DN_PAIR, LANE, CH), F32), pltpu.VMEM((DN_PAIR, CH, LANE), F32),
                        pltpu.VMEM((DN_PAIR, CH, LANE), F32), pltpu.VMEM((DN_PAIR, CH, LANE), F32)],
        compiler_params=_cp(("arbitrary",)), name=name)(
            dmix, qn, kn, vc, g_rows, b_rows, out_g.reshape(1, HEAD_DIM), states, tinv)


def _rows_form(x, nc):
    t = x[:, :DN_V_HEADS].T.reshape(DN_V_HEADS, nc, CH)
    return jnp.pad(t, ((0, 0), (0, LANE - nc), (0, 0)))


def _cols_to_nat(x, nc):
    t = jnp.transpose(x[:, :, :nc], (2, 1, 0)).reshape(nc * CH, DN_V_HEADS)
    return jnp.pad(t, ((0, 0), (0, LANE - DN_V_HEADS)))


_C_QKV = 2 * DN_QK_WIDTH + MIX_WIDTH


def _w0_to_padded(w):
    rows = w.shape[0]
    z = lambda n: jnp.zeros((rows, n), w.dtype)
    a = w[:, _C_QKV:_C_QKV + DN_V_HEADS]
    b = w[:, _C_QKV + DN_V_HEADS:_C_QKV + 2 * DN_V_HEADS]
    return jnp.concatenate([w[:, :_C_QKV], w[:, _C_QKV + 2 * DN_V_HEADS:], a, z(LANE - DN_V_HEADS), b,
                            z(P0 - P0_AB - LANE - DN_V_HEADS)], axis=1)


def _w0_from_padded(g):
    return jnp.concatenate([g[:, :_C_QKV], g[:, P0_AB:P0_AB + DN_V_HEADS], g[:, P0_AB + LANE:P0_AB + LANE + DN_V_HEADS],
                            g[:, _C_QKV:P0_AB]], axis=1)


def _true_pieces(lo, hi):
    out = []
    while lo < hi:
        s = lo // P0_SHARD
        end = min(hi, (s + 1) * P0_SHARD)
        out.append((s, lo - s * P0_SHARD, end - s * P0_SHARD))
        lo = end
    return out


def _padded_pieces(lo, hi):
    a0, b0, x0 = _C_QKV, _C_QKV + DN_V_HEADS, _C_QKV + 2 * DN_V_HEADS
    out = []
    for t0, t1, shift in ((0, a0, 0), (a0, b0, P0_AB - a0), (b0, x0, P0_AB + LANE - b0), (x0, DN_PROJ, a0 - x0)):
        s, e = max(lo, t0), min(hi, t1)
        if s < e:
            out.append((s + shift, e + shift))
    return out


def _pad_lane(v):
    v = v.reshape(1, -1)
    return jnp.pad(v, ((0, 0), (0, LANE - v.shape[1])))


SLOT1 = SB_PROJ // N_CHIPS
MM_TN = 512


def _local_step(x, mem, target, norm_g, mem_norm_g, w_kv, xa_q_g, xa_k_g, w_out, w_in0, conv_w, a_log, dt_bias, out_g,
                w_in1, sb_q_g, sb_k_g):
    S = x.shape[0]
    nc = S // CH
    al = _pad_lane(a_log)
    dtb = _pad_lane(dt_bias)
    q_scale = HEAD_DIM ** -0.5
    tiles1 = SLOT1 // MM_TN

    kv_rhs = lambda l: pl.BlockSpec((N_CHIPS, None, D_MODEL // N_CHIPS, MM_TN), lambda i, j: (0, l, 0, j))
    kv_rhs_t = lambda l: pl.BlockSpec((None, None, D_MODEL // N_CHIPS, 2 * XA_WIDTH), lambda i, j: (j, l, 0, 0))
    out_rhs = lambda l: pl.BlockSpec((N_CHIPS, None, INNER // N_CHIPS, MM_TN), lambda i, j: (0, l, 0, j))
    out_rhs_t = lambda l: pl.BlockSpec((None, None, MM_TN, D_MODEL), lambda i, j: (j // 2, l, j % 2, 0))
    in1_rhs = pl.BlockSpec((None, 2, D_MODEL // 2, MM_TN), lambda i, j: (j // tiles1, 0, 0, j % tiles1))
    in1_rhs_t = pl.BlockSpec((None, None, D_MODEL // 2, MM_TN), lambda i, j, k: (k // tiles1, j, 0, k % tiles1))
    slot_rows = lambda rows: dict(
        tm=rows, o_spec=pl.BlockSpec((None, rows, MM_TN), lambda i, j: (i, 0, j)),
        o_shape=jax.ShapeDtypeStruct((N_CHIPS, rows, 2 * XA_WIDTH), BF16))
    in1_out = dict(tm=D_MODEL // 2, o_spec=pl.BlockSpec((None, None, D_MODEL // 2, MM_TN),
                                                        lambda i, j: (j // tiles1, i, 0, j % tiles1)),
                   o_shape=jax.ShapeDtypeStruct((N_CHIPS, 2, D_MODEL // 2, SLOT1), BF16))

    mem_n = _rmsnorm_fwd(mem, mem_norm_g, name="mem_norm")
    kv = [_matmul(mem_n, w_kv, n=2 * XA_WIDTH, tn=MM_TN, b_spec=kv_rhs(l), name=f"kv{l}") for l in range(2)]

    h0 = _rmsnorm_fwd(x, norm_g[0], name="norm0")
    proj0 = _matmul(h0, w_in0, name="proj0")
    qn = _dn_pre_fwd(proj0, conv_w, 0, DN_QK_WIDTH, l2=True, scale=q_scale, name="dn_pre_q")
    kn = _dn_pre_fwd(proj0, conv_w, DN_QK_WIDTH, DN_QK_WIDTH, l2=True, scale=1.0, name="dn_pre_k")
    vc = _dn_pre_fwd(proj0, conv_w, 2 * DN_QK_WIDTH, MIX_WIDTH, l2=False, scale=1.0, name="dn_pre_v")
    g_nat, b_nat = _dn_ab_fwd(proj0, al, dtb, name="dn_ab")
    g_rows = _rows_form(g_nat, nc)
    b_rows = _rows_form(b_nat, nc)
    mix0, states, tinv = _dn_core_fwd(qn, kn, vc, g_rows, b_rows, out_g, name="dn_core")
    xa0 = _xa_fwd(proj0, P0_XQ, kv[0], xa_q_g[0], xa_k_g[0], name="xa0")
    y0 = _gate_fwd(mix0, xa0, proj0, P0_Z, name="gate0")
    x1 = _matmul(y0, w_out, n=D_MODEL, tn=MM_TN, b_spec=out_rhs(0), res=x, name="out0")

    h1 = _rmsnorm_fwd(x1, norm_g[1], name="norm1")
    proj1 = _matmul(h1, w_in1, n=SB_PROJ, tn=MM_TN, b_spec=in1_rhs, name="proj1")
    mix1, tot1 = _sb_fwd(proj1, sb_q_g, sb_k_g, name="sb")
    xa1 = _xa_fwd(proj1, P1_XQ, kv[1], xa_q_g[1], xa_k_g[1], name="xa1")
    y1 = _gate_fwd(mix1, xa1, proj1, P1_Z, name="gate1")
    x2 = _matmul(y1, w_out, n=D_MODEL, tn=MM_TN, b_spec=out_rhs(1), res=x1, name="out1")

    dx2, loss_vec = _loss_head(x2, target, name="loss")

    d_wout1 = _matmul(y1, dx2, ta=True, name="d_wout1", **slot_rows(INNER // N_CHIPS))
    dy1 = _matmul(dx2, w_out, tb=True, n=INNER, tn=MM_TN, b_spec=out_rhs_t(1), name="dy1")
    dcat1, dz1 = _gate_bwd(dy1, mix1, xa1, proj1, P1_Z, name="gate1_bwd")
    dqkv1, d_sbq, d_sbk = _sb_bwd(dcat1, tot1, proj1, sb_q_g, sb_k_g, name="sb_bwd")
    dxq1, dkv1, d_xaq1, d_xak1 = _xa_bwd(dcat1, proj1, P1_XQ, kv[1], xa_q_g[1], xa_k_g[1], name="xa1_bwd")
    dproj1 = dqkv1 + [dxq1, dz1]
    d_win1 = _matmul(h1, dproj1, ta=True, name="d_win1", **in1_out)
    dh1 = _matmul(dproj1, w_in1, tb=True, n=D_MODEL, tm=min(S, 512), tn=D_MODEL // 2, tk=MM_TN, b_spec=in1_rhs_t, name="dh1")
    dx1, d_ng1 = _rmsnorm_bwd(dh1, x1, norm_g[1], dx2, name="norm1_bwd")

    d_wout0 = _matmul(y0, dx1, ta=True, name="d_wout0", **slot_rows(INNER // N_CHIPS))
    dy0 = _matmul(dx1, w_out, tb=True, n=INNER, tn=MM_TN, b_spec=out_rhs_t(0), name="dy0")
    dcat0, dz0 = _gate_bwd(dy0, mix0, xa0, proj0, P0_Z, name="gate0_bwd")
    dqv, dkv_h, dvc, dg_cols, db_cols, d_outg = _dn_core_bwd(
        dcat0, qn, kn, vc, g_rows, b_rows, out_g, states, tinv, name="dn_core_bwd")
    dpq, dwq = _dn_pre_bwd(dqv, proj0, conv_w, 0, DN_QK_WIDTH, l2=True, scale=q_scale, name="dn_pre_q_bwd")
    dpk, dwk = _dn_pre_bwd(dkv_h, proj0, conv_w, DN_QK_WIDTH, DN_QK_WIDTH, l2=True, scale=1.0, name="dn_pre_k_bwd")
    dpv, dwv = _dn_pre_bwd(dvc, proj0, conv_w, 2 * DN_QK_WIDTH, MIX_WIDTH, l2=False, scale=1.0, name="dn_pre_v_bwd")
    dab, d_alog, d_dt = _dn_ab_bwd(_cols_to_nat(dg_cols, nc), _cols_to_nat(db_cols, nc), proj0, al, dtb, name="dn_ab_bwd")
    dxq0, dkv0, d_xaq0, d_xak0 = _xa_bwd(dcat0, proj0, P0_XQ, kv[0], xa_q_g[0], xa_k_g[0], name="xa0_bwd")
    dproj0 = [dpq, dpk, dpv, dxq0, dz0, dab]
    d_win0 = _matmul(h0, dproj0, ta=True, out_dtype=BF16, name="d_win0")
    dh0 = _matmul(dproj0, w_in0, tb=True, tm=min(S, 512), tk=MM_TN, name="dh0")
    dx0, d_ng0 = _rmsnorm_bwd(dh0, x, norm_g[0], dx1, name="norm0_bwd")

    d_wkv = [_matmul(mem_n, d, ta=True, name=f"d_wkv{l}", **slot_rows(D_MODEL // N_CHIPS)) for l, d in enumerate((dkv0, dkv1))]
    dmem0 = _matmul(dkv0, w_kv, tb=True, n=D_MODEL, tn=D_MODEL // N_CHIPS, b_spec=kv_rhs_t(0), name="dmem0")
    dmem_n = _matmul(dkv1, w_kv, tb=True, n=D_MODEL, tn=D_MODEL // N_CHIPS, b_spec=kv_rhs_t(1), res=dmem0, name="dmem1")
    _, d_memg = _rmsnorm_bwd(dmem_n, mem, mem_norm_g, None, name="mem_norm_bwd")

    grads = dict(
        norm_g=jnp.concatenate([d_ng0, d_ng1], axis=0), mem_norm_g=d_memg.reshape(-1), mem_w_kv=jnp.stack(d_wkv, axis=1),
        xa_q_norm_g=jnp.concatenate([d_xaq0, d_xaq1], axis=0), xa_k_norm_g=jnp.concatenate([d_xak0, d_xak1], axis=0),
        w_out=jnp.stack([d_wout0, d_wout1], axis=1), dn_w_in=d_win0, dn_conv_w=jnp.concatenate([dwq, dwk, dwv], axis=1),
        dn_a_log=d_alog[:, :DN_V_HEADS], dn_dt_bias=d_dt[:, :DN_V_HEADS], dn_out_norm_g=d_outg, sb_w_in=d_win1,
        sb_q_norm_g=d_sbq, sb_k_norm_g=d_sbk)
    return loss_vec, dx0, grads


ANY = pl.BlockSpec(memory_space=pl.ANY)


def _place():
    x, y, c = lax.axis_index("x"), lax.axis_index("y"), lax.axis_index("c")
    chips = [(1 - x, y), (x, 1 - y), (1 - x, 1 - y)]
    return x, y, c, 2 * x + y, (x, y, 1 - c), chips


def _rcopy(src, dst, send, recv, i, dev):
    return pltpu.make_async_remote_copy(src_ref=src, dst_ref=dst, send_sem=send.at[i], recv_sem=recv.at[i],
                                        device_id=dev, device_id_type=MESH)


def _gather_weights(srcs, *, name):
    nt = len(srcs)

    def body(*refs):
        src, dst = refs[:nt], refs[nt:2 * nt]
        send, recv = refs[2 * nt:]
        x, y, c, j, sib, chips = _place()
        sends = []
        for t in range(nt):
            for k, (cx, cy) in enumerate(chips):
                sends.append(_rcopy(src[t].at[c], dst[t].at[j, c], send, recv, 6 * t + k, (cx, cy, c)))
                sends[-1].start()
        for t in range(nt):
            for k, (cx, cy) in enumerate(chips):
                landed = dst[t].at[2 * cx + cy, c]
                _rcopy(landed, landed, send, recv, 6 * t + k, (cx, cy, c)).wait_recv()
                sends.append(_rcopy(landed, landed, send, recv, 6 * t + 3 + k, sib))
                sends[-1].start()
        for t in range(nt):
            for k, (cx, cy) in enumerate(chips):
                other = dst[t].at[2 * cx + cy, 1 - c]
                _rcopy(other, other, send, recv, 6 * t + 3 + k, sib).wait_recv()
        for cp in sends:
            cp.wait_send()

    return pl.pallas_call(
        body, in_specs=[ANY] * nt, out_specs=[ANY] * nt,
        out_shape=[jax.ShapeDtypeStruct((N_CHIPS,) + s.shape, s.dtype) for s in srcs],
        scratch_shapes=[pltpu.SemaphoreType.DMA((6 * nt,)), pltpu.SemaphoreType.DMA((6 * nt,))],
        name=name)(*srcs)


def _swap_halves(xs, *, name):
    nt = len(xs)

    def body(*refs):
        src, dst = refs[:nt], refs[nt:2 * nt]
        send, recv = refs[2 * nt:]
        x, y, c, j, sib, chips = _place()
        cps = []
        for t in range(nt):
            for s in range(N_CHIPS):
                cps.append(_rcopy(src[t].at[s, 1 - c], dst[t].at[s], send, recv, 4 * t + s, sib))
                cps[-1].start()
        for cp in cps:
            cp.wait_recv()
        for cp in cps:
            cp.wait_send()

    return pl.pallas_call(
        body, in_specs=[ANY] * nt, out_specs=[ANY] * nt,
        out_shape=[jax.ShapeDtypeStruct((N_CHIPS,) + a.shape[2:], a.dtype) for a in xs],
        scratch_shapes=[pltpu.SemaphoreType.DMA((4 * nt,)), pltpu.SemaphoreType.DMA((4 * nt,))], name=name)(*xs)


def _scatter_to_chips(ps, *, name):
    nt = len(ps)

    def body(*refs):
        src, dst = refs[:nt], refs[nt:2 * nt]
        send, recv = refs[2 * nt:]
        x, y, c, j, sib, chips = _place()
        cps = []
        for t in range(nt):
            for k, (cx, cy) in enumerate(chips):
                cps.append(_rcopy(src[t].at[2 * cx + cy], dst[t].at[k], send, recv, 3 * t + k, (cx, cy, c)))
                cps[-1].start()
        for cp in cps:
            cp.wait_recv()
        for cp in cps:
            cp.wait_send()

    return pl.pallas_call(
        body, in_specs=[ANY] * nt, out_specs=[ANY] * nt,
        out_shape=[jax.ShapeDtypeStruct((3,) + a.shape[1:], a.dtype) for a in ps],
        scratch_shapes=[pltpu.SemaphoreType.DMA((3 * nt,)), pltpu.SemaphoreType.DMA((3 * nt,))], name=name)(*ps)


def _swap_with_sibling(fs, *, name):
    nt = len(fs)

    def body(*refs):
        src, dst = refs[:nt], refs[nt:2 * nt]
        send, recv = refs[2 * nt:]
        x, y, c, j, sib, chips = _place()
        cps = [_rcopy(src[t], dst[t], send, recv, t, sib) for t in range(nt)]
        for cp in cps:
            cp.start()
        for cp in cps:
            cp.wait_recv()
        for cp in cps:
            cp.wait_send()

    return pl.pallas_call(
        body, in_specs=[ANY] * nt, out_specs=[ANY] * nt,
        out_shape=[jax.ShapeDtypeStruct(a.shape, a.dtype) for a in fs],
        scratch_shapes=[pltpu.SemaphoreType.DMA((nt,)), pltpu.SemaphoreType.DMA((nt,))], name=name)(*fs)


def _all_reduce_small(parts, *, name):
    n = len(parts)
    offs, rows = [], 0
    for p in parts:
        offs.append(rows)
        rows += -(-p.shape[0] // 8) * 8

    def body(*refs):
        p_refs, o_refs = refs[:n], refs[n:2 * n]
        buf, send, recv = refs[2 * n:]
        x, y, c = lax.axis_index("x"), lax.axis_index("y"), lax.axis_index("c")
        me = 4 * x + 2 * y + c
        buf[me] = jnp.zeros((rows, LANE), F32)
        for p_ref, off in zip(p_refs, offs):
            buf[me, off:off + p_ref.shape[0], :] = p_ref[...]
        cps = []
        for r in range(1, 8):
            dev = (x ^ (r >> 2), y ^ ((r >> 1) & 1), c ^ (r & 1))
            cps.append(_rcopy(buf.at[me], buf.at[me], send, recv, r - 1, dev))
            cps[-1].start()
        for r in range(1, 8):
            frm = buf.at[me ^ r]
            _rcopy(frm, frm, send, recv, r - 1, (x, y, c)).wait_recv()
        for cp in cps:
            cp.wait_send()
        acc = buf[0]
        for d in range(1, 8):
            acc = acc + buf[d]
        for o_ref, off in zip(o_refs, offs):
            o_ref[...] = acc[off:off + o_ref.shape[0], :]

    vm = pl.BlockSpec(memory_space=pltpu.VMEM)
    return pl.pallas_call(
        body, in_specs=[vm] * n, out_specs=[vm] * n, out_shape=[jax.ShapeDtypeStruct(p.shape, F32) for p in parts],
        scratch_shapes=[pltpu.VMEM((8, rows, LANE), F32), pltpu.SemaphoreType.DMA((7,)), pltpu.SemaphoreType.DMA((7,))],
        name=name)(*parts)


def _add_halves(x, b, c_idx, *, name, tr=256):
    _, _, R, C = x.shape
    tr = min(tr, R)

    def body(c_ref, x_ref, b_ref, o_ref):
        o_ref[...] = (x_ref[...].astype(F32) + b_ref[...].astype(F32)).astype(o_ref.dtype)

    return pl.pallas_call(
        body,
        grid_spec=pltpu.PrefetchScalarGridSpec(
            num_scalar_prefetch=1, grid=(N_CHIPS, R // tr),
            in_specs=[pl.BlockSpec((None, None, tr, C), lambda s, i, c_ref: (s, c_ref[0], i, 0)),
                      pl.BlockSpec((None, tr, C), lambda s, i, c_ref: (s, i, 0))],
            out_specs=pl.BlockSpec((None, tr, C), lambda s, i, c_ref: (s, i, 0))),
        out_shape=jax.ShapeDtypeStruct(b.shape, b.dtype), compiler_params=_cp(("parallel", "parallel")), name=name)(c_idx, x, b)


def _sum_slot(p, rcv, j_idx, *, name, tr=256):
    _, R, C = p.shape
    tr = min(tr, R)

    def body(j_ref, p_ref, r_ref, o_ref):
        acc = p_ref[...].astype(F32)
        for k in range(3):
            acc = acc + r_ref[k].astype(F32)
        o_ref[...] = acc

    return pl.pallas_call(
        body,
        grid_spec=pltpu.PrefetchScalarGridSpec(
            num_scalar_prefetch=1, grid=(R // tr,),
            in_specs=[pl.BlockSpec((None, tr, C), lambda i, j_ref: (j_ref[0], i, 0)),
                      pl.BlockSpec((3, tr, C), lambda i, j_ref: (0, i, 0))],
            out_specs=pl.BlockSpec((tr, C), lambda i, j_ref: (i, 0))),
        out_shape=jax.ShapeDtypeStruct((R, C), F32), compiler_params=_cp(("parallel",)), name=name)(j_idx, p, rcv)


def _adamw_math(w, g, m, v):
    nm = ADAM_B1 * m + (1.0 - ADAM_B1) * g
    nv = ADAM_B2 * v + (1.0 - ADAM_B2) * (g * g)
    m_hat = nm / (1.0 - ADAM_B1 ** ADAM_STEP)
    v_hat = nv / (1.0 - ADAM_B2 ** ADAM_STEP)
    return -ADAM_LR * (m_hat / (jnp.sqrt(v_hat) + ADAM_EPS) + ADAM_WD * w), nm, nv


def _adamw_halves(w, g_mine, g_theirs, m, v, c_idx, *, name, tr=128):
    _, R, C = w.shape
    tr = tr if R % tr == 0 else R

    def body(c_ref, w_ref, gm_ref, gt_ref, m_ref, v_ref, g_ref, d_ref, nm_ref, nv_ref):
        gv = jnp.where(pl.program_id(0) == c_ref[0], gm_ref[...], gt_ref[...])
        d, nm, nv = _adamw_math(w_ref[...], gv, m_ref[...], v_ref[...])
        g_ref[...] = gv
        d_ref[...] = d
        nm_ref[...] = nm
        nv_ref[...] = nv

    full = pl.BlockSpec((None, tr, C), lambda hh, i, c_ref: (hh, i, 0))
    half = pl.BlockSpec((tr, C), lambda hh, i, c_ref: (i, 0))
    sh = jax.ShapeDtypeStruct(w.shape, F32)
    return pl.pallas_call(
        body,
        grid_spec=pltpu.PrefetchScalarGridSpec(num_scalar_prefetch=1, grid=(2, R // tr),
                                               in_specs=[full, half, half, full, full], out_specs=(full,) * 4),
        out_shape=(sh,) * 4, compiler_params=_cp(("parallel", "parallel")), name=name)(c_idx, w, g_mine, g_theirs, m, v)


def _adamw_parts(ws, gs, ms, vs, *, name):
    n = len(ws)

    def body(*refs):
        ins, outs = refs[:4 * n], refs[4 * n:]
        for t in range(n):
            d, nm, nv = _adamw_math(ins[t][...], ins[n + t][...], ins[2 * n + t][...], ins[3 * n + t][...])
            outs[t][...] = d
            outs[n + t][...] = nm
            outs[2 * n + t][...] = nv

    vm = pl.BlockSpec(memory_space=pltpu.VMEM)
    shapes = [jax.ShapeDtypeStruct(w.shape, F32) for w in ws] * 3
    outs = pl.pallas_call(body, in_specs=[vm] * (4 * n), out_specs=[vm] * (3 * n), out_shape=shapes, name=name)(
        *ws, *gs, *ms, *vs)
    return outs[:n], outs[n:2 * n], outs[2 * n:]


_SMALL = ["norm_g", "mem_norm_g", "xa_q_norm_g", "xa_k_norm_g", "dn_a_log", "dn_dt_bias", "dn_out_norm_g",
          "sb_q_norm_g", "sb_k_norm_g"]


def _rows128(a):
    flat = a.reshape(-1)
    pad = -flat.shape[0] % LANE
    if pad:
        flat = jnp.pad(flat, (0, pad))
    return flat.reshape(-1, LANE)


def _unrows(r, shape):
    return r.reshape(-1)[:math.prod(shape)].reshape(shape)


def kernel(x, mem, norm_g, mem_norm_g, mem_w_kv, xa_q_norm_g, xa_k_norm_g, w_out, dn_w_in, dn_conv_w, dn_a_log, dn_dt_bias, dn_out_norm_g, sb_w_in, sb_q_norm_g, sb_k_norm_g, loss_target, m_norm_g, m_mem_norm_g, m_mem_w_kv, m_xa_q_norm_g, m_xa_k_norm_g, m_w_out, m_dn_w_in, m_dn_conv_w, m_dn_a_log, m_dn_dt_bias, m_dn_out_norm_g, m_sb_w_in, m_sb_q_norm_g, m_sb_k_norm_g, v_norm_g, v_mem_norm_g, v_mem_w_kv, v_xa_q_norm_g, v_xa_k_norm_g, v_w_out, v_dn_w_in, v_dn_conv_w, v_dn_a_log, v_dn_dt_bias, v_dn_out_norm_g, v_sb_w_in, v_sb_q_norm_g, v_sb_k_norm_g):
    W = dict(norm_g=norm_g, mem_norm_g=mem_norm_g, mem_w_kv=mem_w_kv, xa_q_norm_g=xa_q_norm_g, xa_k_norm_g=xa_k_norm_g,
             w_out=w_out, dn_w_in=dn_w_in, dn_conv_w=dn_conv_w, dn_a_log=dn_a_log, dn_dt_bias=dn_dt_bias,
             dn_out_norm_g=dn_out_norm_g, sb_w_in=sb_w_in, sb_q_norm_g=sb_q_norm_g, sb_k_norm_g=sb_k_norm_g)
    M = dict(norm_g=m_norm_g, mem_norm_g=m_mem_norm_g, mem_w_kv=m_mem_w_kv, xa_q_norm_g=m_xa_q_norm_g,
             xa_k_norm_g=m_xa_k_norm_g, w_out=m_w_out, dn_w_in=m_dn_w_in, dn_conv_w=m_dn_conv_w, dn_a_log=m_dn_a_log,
             dn_dt_bias=m_dn_dt_bias, dn_out_norm_g=m_dn_out_norm_g, sb_w_in=m_sb_w_in, sb_q_norm_g=m_sb_q_norm_g,
             sb_k_norm_g=m_sb_k_norm_g)
    V = dict(norm_g=v_norm_g, mem_norm_g=v_mem_norm_g, mem_w_kv=v_mem_w_kv, xa_q_norm_g=v_xa_q_norm_g,
             xa_k_norm_g=v_xa_k_norm_g, w_out=v_w_out, dn_w_in=v_dn_w_in, dn_conv_w=v_dn_conv_w, dn_a_log=v_dn_a_log,
             dn_dt_bias=v_dn_dt_bias, dn_out_norm_g=v_dn_out_norm_g, sb_w_in=v_sb_w_in, sb_q_norm_g=v_sb_q_norm_g,
             sb_k_norm_g=v_sb_k_norm_g)
    names = ["norm_g", "mem_norm_g", "mem_w_kv", "xa_q_norm_g", "xa_k_norm_g", "w_out", "dn_w_in", "dn_conv_w",
             "dn_a_log", "dn_dt_bias", "dn_out_norm_g", "sb_w_in", "sb_q_norm_g", "sb_k_norm_g"]
    cx, cy, cc = lax.axis_index("x"), lax.axis_index("y"), lax.axis_index("c")
    slot = 2 * cx + cy
    half_r = D_MODEL // 2
    conv_cols = dn_conv_w.shape[2]

    w0s = jnp.pad(dn_w_in[0].astype(BF16), ((0, 0), (0, P0_SHARD_PAD - P0_SHARD))).reshape(2, half_r, P0_SHARD_PAD)
    w1s = sb_w_in[0].astype(BF16).reshape(2, half_r, SB_PROJ // N_CHIPS)
    convs = jnp.pad(dn_conv_w[0], ((0, 8 - DN_CONV), (0, 0))).reshape(8, 2, conv_cols // 2).transpose(1, 0, 2)
    own = [w0s, w1s, w_out.astype(BF16), mem_w_kv.astype(BF16), convs]
    gathered = _gather_weights(own, name="gather_weights")
    g0, g1, gout, gkv, gconv = [lax.dynamic_update_slice(g, o[None], (slot, 0, 0, 0)) for o, g in zip(own, gathered)]
    shards0 = g0.reshape(N_CHIPS, D_MODEL, P0_SHARD_PAD)
    z = lambda n: jnp.zeros((D_MODEL, n), BF16)
    w_in0 = jnp.concatenate(
        [shards0[s][:, lo:hi] for s, lo, hi in _true_pieces(0, _C_QKV) + _true_pieces(_C_QKV + 2 * DN_V_HEADS, DN_PROJ)]
        + [shards0[s][:, lo:hi] for s, lo, hi in _true_pieces(_C_QKV, _C_QKV + DN_V_HEADS)] + [z(LANE - DN_V_HEADS)]
        + [shards0[s][:, lo:hi] for s, lo, hi in _true_pieces(_C_QKV + DN_V_HEADS, _C_QKV + 2 * DN_V_HEADS)]
        + [z(P0 - P0_AB - LANE - DN_V_HEADS)], axis=1)
    conv_f = gconv.transpose(2, 0, 1, 3).reshape(8, N_CHIPS * conv_cols)[:DN_CONV]

    loss_vec, grad_x, g = _local_step(
        x[0], mem[0], loss_target[0], norm_g, mem_norm_g, gkv, xa_q_norm_g, xa_k_norm_g, gout, w_in0, conv_f,
        dn_a_log[0], dn_dt_bias[0], dn_out_norm_g[0], g1, sb_q_norm_g[0], sb_k_norm_g[0])

    gp = g["dn_w_in"]
    d0 = jnp.stack([jnp.pad(jnp.concatenate([gp[:, lo:hi] for lo, hi in _padded_pieces(s * P0_SHARD, (s + 1) * P0_SHARD)],
                                            axis=1), ((0, 0), (0, P0_SHARD_PAD - P0_SHARD))) for s in range(N_CHIPS)])
    xs = [d0.reshape(N_CHIPS, 2, half_r, P0_SHARD_PAD), g["sb_w_in"], g["w_out"], g["mem_w_kv"]]
    c_idx = jnp.reshape(cc, (1,)).astype(jnp.int32)
    j_idx = jnp.reshape(slot, (1,)).astype(jnp.int32)
    from_sib = _swap_halves(xs, name="rs_swap")
    ps = [_add_halves(a, b, c_idx, name=f"rs_add{t}") for t, (a, b) in enumerate(zip(xs, from_sib))]
    rcvs = _scatter_to_chips(ps, name="rs_scatter")
    fs = [_sum_slot(p, r, j_idx, name=f"rs_sum{t}") for t, (p, r) in enumerate(zip(ps, rcvs))]
    fs[0] = fs[0][:, :P0_SHARD]
    theirs = _swap_with_sibling(fs, name="rs_join")
    big_names = ["dn_w_in", "sb_w_in", "w_out", "mem_w_kv"]

    red = _all_reduce_small([_rows128(g[n]) for n in _SMALL] + [_rows128(g["dn_conv_w"]), loss_vec], name="all_reduce_small")
    small_rows = dict(zip(_SMALL, red))
    conv_full = red[len(_SMALL)].reshape(DN_CONV, N_CHIPS * conv_cols)
    small_rows["dn_conv_w"] = _rows128(lax.dynamic_slice_in_dim(conv_full, slot * conv_cols, conv_cols, axis=1))
    loss = red[-1][0, 0]

    out_g, out_d, out_m, out_v = {}, {}, {}, {}
    for n, mine_g, their_g in zip(big_names, fs, theirs):
        shp = W[n].shape
        h3 = (2,) + mine_g.shape
        outs = _adamw_halves(W[n].reshape(h3), mine_g, their_g, M[n].reshape(h3), V[n].reshape(h3), c_idx, name=f"adamw_{n}")
        out_g[n], out_d[n], out_m[n], out_v[n] = [o.reshape(shp) for o in outs]
    small_names = _SMALL + ["dn_conv_w"]
    ds, nms, nvs = _adamw_parts([_rows128(W[n]) for n in small_names], [small_rows[n] for n in small_names],
                                [_rows128(M[n]) for n in small_names], [_rows128(V[n]) for n in small_names], name="adamw_small")
    for n, d, nm, nv in zip(small_names, ds, nms, nvs):
        shp = W[n].shape
        out_g[n], out_d[n], out_m[n], out_v[n] = [_unrows(r, shp) for r in (small_rows[n], d, nm, nv)]

    return (loss, grad_x[None], *[out_g[n] for n in names], *[out_d[n] for n in names], *[out_m[n] for n in names],
            *[out_v[n] for n in names])
```

```python
import functools
import math

import jax
import jax.numpy as jnp
from jax import lax
from jax.experimental import pallas as pl
from jax.experimental.pallas import tpu as pltpu

F32 = jnp.float32
BF16 = jnp.bfloat16
HI = lax.Precision.HIGHEST
MESH = pl.DeviceIdType.MESH

D_MODEL = 2048
INNER = 4096
XA_WIDTH = 1024
XA_HEADS = 4
XA_DIM = 256
MIX_WIDTH = 3072
HEAD_DIM = 128
DN_V_HEADS = 24
DN_QK_WIDTH = 1536
DN_CONV = 4
DN_PROJ = 11312
SB_PROJ = 14336
EPS = 1e-6
N_CHIPS = 4

CH = 128
LANE = 128

P0_XQ = 6144
P0_Z = 7168
P0_AB = 11264
P0 = 11776
P0_SHARD = DN_PROJ // N_CHIPS
P0_SHARD_PAD = 2944
P1_XQ = 9216
P1_Z = 10240
P1 = SB_PROJ

ADAM_LR = 0.001
ADAM_B1 = 0.9
ADAM_B2 = 0.999
ADAM_EPS = 1e-08
ADAM_WD = 0.01
ADAM_STEP = 10

VMEM_LIMIT = 48 * 1024 * 1024


def _cp(sem=None, **kw):
    return pltpu.CompilerParams(dimension_semantics=sem, vmem_limit_bytes=VMEM_LIMIT, **kw)


def _bdot(a, b, dims):
    return lax.dot_general(a.astype(BF16), b.astype(BF16), (dims, ((), ())), preferred_element_type=F32)


def _fdot(a, b, dims):
    return lax.dot_general(a, b, (dims, ((), ())), precision=HI, preferred_element_type=F32)


NN = ((1,), (0,))
NT = ((1,), (1,))
TN = ((0,), (0,))


def _sigmoid(x):
    return 1.0 / (1.0 + jnp.exp(-x))


def _softplus(x):
    return jnp.maximum(x, 0.0) + jnp.log(1.0 + jnp.exp(-jnp.abs(x)))


def _iota2(shape, axis):
    return lax.broadcasted_iota(jnp.int32, shape, axis)


MM_FULL_K = 4096
MM_BLOCK_BYTES = 4 * 1024 * 1024


def _matmul(a, b, *, ta=False, tb=False, out_dtype=F32, res=None, name, n=None, tm=None, tn=None, tk=None,
            b_spec=None, o_spec=None, o_shape=None):
    a_segs = list(a) if isinstance(a, (list, tuple)) else [a]
    b_segs = list(b) if isinstance(b, (list, tuple)) else [b]
    a0, b0 = a_segs[0], b_segs[0]
    M = a0.shape[1] if ta else a0.shape[0]
    K = a0.shape[0] if ta else sum(s.shape[1] for s in a_segs)
    if n is None:
        n = b0.shape[0] if tb else sum(s.shape[1] for s in b_segs)
    N = n
    dims = ((0,) if ta else (1,), (1,) if tb else (0,))
    has_res = res is not None
    flat = lambda v: v.reshape(-1, v.shape[-1])
    o_shape = o_shape or jax.ShapeDtypeStruct((M, N), out_dtype)

    def seg_specs(segs, tile, block, pos):
        specs, ranges, off = [], [], 0
        for s in segs:
            cnt = s.shape[1] // tile
            assert s.shape[1] % tile == 0, (name, s.shape, tile)

            def imap(*g, off=off, cnt=cnt):
                t = jnp.clip(g[pos] - off, 0, cnt - 1)
                return (g[0], t) if pos == 2 else (0, t)

            specs.append(pl.BlockSpec(block, imap))
            ranges.append((off, off + cnt))
            off += cnt
        return specs, ranges

    if K <= MM_FULL_K:
        assert len(a_segs) == 1
        tm = tm or min(M, 1024, max(256, MM_BLOCK_BYTES // (K * a0.dtype.itemsize)))
        tn = tn or min(N, 512)
        assert M % tm == 0 and N % tn == 0, (name, M, N, K, tm, tn)
        nb = len(b_segs)
        if b_spec is not None:
            b_specs, b_ranges = [b_spec], [(0, N // tn)]
        elif nb > 1:
            assert not tb
            b_specs, b_ranges = seg_specs(b_segs, tn, (K, tn), 1)
        else:
            b_specs = [pl.BlockSpec((tn, K), lambda i, j: (j, 0)) if tb else pl.BlockSpec((K, tn), lambda i, j: (0, j))]
            b_ranges = [(0, N // tn)]

        def body_full(*refs):
            a_ref, b_refs = refs[0], refs[1:1 + nb]
            r_ref = refs[1 + nb] if has_res else None
            o_ref = refs[-1]
            j = pl.program_id(1)
            for b_ref, (lo, hi) in zip(b_refs, b_ranges):
                def emit(b_ref=b_ref):
                    r = _bdot(a_ref[...], flat(b_ref[...]), dims)
                    if has_res:
                        r = r + r_ref[...]
                    o_ref[...] = r.astype(o_ref.dtype).reshape(o_ref.shape)
                if nb == 1:
                    emit()
                else:
                    pl.when(jnp.logical_and(j >= lo, j < hi))(emit)

        a_spec = pl.BlockSpec((K, tm), lambda i, j: (0, i)) if ta else pl.BlockSpec((tm, K), lambda i, j: (i, 0))
        o_spec = o_spec or pl.BlockSpec((tm, tn), lambda i, j: (i, j))
        r_spec = [pl.BlockSpec((tm, tn), lambda i, j: (i, j))] if has_res else []
        return pl.pallas_call(
            body_full, grid=(M // tm, N // tn), in_specs=[a_spec] + b_specs + r_spec, out_specs=o_spec, out_shape=o_shape,
            compiler_params=_cp(("parallel", "arbitrary")), name=name)(*([a0] + b_segs + ([res] if has_res else [])))

    assert tb and not ta and len(b_segs) == 1
    tm, tn = tm or min(M, 1024), tn or min(N, 1024)
    tk = tk or (1024 if all(s.shape[1] % 1024 == 0 for s in a_segs) else 512)
    assert M % tm == 0 and N % tn == 0 and K % tk == 0, (name, M, N, K, tm, tn, tk)
    nk = K // tk
    na = len(a_segs)
    if na > 1:
        a_specs, a_ranges = seg_specs(a_segs, tk, (tm, tk), 2)
    else:
        a_specs, a_ranges = [pl.BlockSpec((tm, tk), lambda i, j, k: (i, k))], [(0, nk)]
    b_spec = b_spec or pl.BlockSpec((tn, tk), lambda i, j, k: (j, k))

    def body(*refs):
        a_refs, b_ref = refs[:na], refs[na]
        r_ref = refs[na + 1] if has_res else None
        o_ref, acc = refs[-2], refs[-1]
        k = pl.program_id(2)

        @pl.when(k == 0)
        def _():
            acc[...] = jnp.zeros_like(acc)

        for a_ref, (lo, hi) in zip(a_refs, a_ranges):
            def emit(a_ref=a_ref):
                acc[...] += _bdot(a_ref[...], flat(b_ref[...]), dims)
            if na == 1:
                emit()
            else:
                pl.when(jnp.logical_and(k >= lo, k < hi))(emit)

        @pl.when(k == nk - 1)
        def _():
            r = acc[...]
            if has_res:
                r = r + r_ref[...]
            o_ref[...] = r.astype(o_ref.dtype).reshape(o_ref.shape)

    o_spec = o_spec or pl.BlockSpec((tm, tn), lambda i, j, k: (i, j))
    r_spec = [pl.BlockSpec((tm, tn), lambda i, j, k: (i, j))] if has_res else []
    return pl.pallas_call(
        body, grid=(M // tm, N // tn, nk), in_specs=a_specs + [b_spec] + r_spec, out_specs=o_spec, out_shape=o_shape,
        scratch_shapes=[pltpu.VMEM((tm, tn), F32)],
        compiler_params=_cp(("parallel", "parallel", "arbitrary")), name=name)(*(a_segs + [b0] + ([res] if has_res else [])))


def _rmsnorm_fwd(x, g, *, name, tm=256):
    S, Dm = x.shape
    tm = min(tm, S)

    def body(x_ref, g_ref, o_ref):
        xv = x_ref[...]
        r = lax.rsqrt(jnp.mean(xv * xv, axis=-1, keepdims=True) + EPS)
        o_ref[...] = (xv * r * g_ref[...]).astype(BF16)

    return pl.pallas_call(
        body, grid=(S // tm,), in_specs=[pl.BlockSpec((tm, Dm), lambda i: (i, 0)), pl.BlockSpec((1, Dm), lambda i: (0, 0))],
        out_specs=pl.BlockSpec((tm, Dm), lambda i: (i, 0)), out_shape=jax.ShapeDtypeStruct((S, Dm), BF16),
        compiler_params=_cp(("parallel",)), name=name)(x, g.reshape(1, Dm))


def _rmsnorm_bwd(dh, x, g, dres, *, name, tm=256):
    S, Dm = x.shape
    tm = min(tm, S)
    want_dx = dres is not None

    def body(*refs):
        if want_dx:
            dh_ref, x_ref, g_ref, dr_ref, dx_ref, dg_ref = refs
        else:
            dh_ref, x_ref, g_ref, dg_ref = refs
        i = pl.program_id(0)
        xv = x_ref[...]
        dhv = dh_ref[...]
        r = lax.rsqrt(jnp.mean(xv * xv, axis=-1, keepdims=True) + EPS)
        y = xv * r
        part = jnp.sum(dhv * y, axis=0, keepdims=True)

        @pl.when(i == 0)
        def _():
            dg_ref[...] = jnp.zeros_like(dg_ref)

        dg_ref[...] += part
        if want_dx:
            dy = dhv * g_ref[...]
            dx_ref[...] = dr_ref[...] + r * (dy - y * jnp.mean(dy * y, axis=-1, keepdims=True))

    row = pl.BlockSpec((tm, Dm), lambda i: (i, 0))
    vec = pl.BlockSpec((1, Dm), lambda i: (0, 0))
    if want_dx:
        dx, dg = pl.pallas_call(
            body, grid=(S // tm,), in_specs=[row, row, vec, row], out_specs=(row, vec),
            out_shape=(jax.ShapeDtypeStruct((S, Dm), F32), jax.ShapeDtypeStruct((1, Dm), F32)),
            compiler_params=_cp(("arbitrary",)), name=name)(dh, x, g.reshape(1, Dm), dres)
        return dx, dg
    dg = pl.pallas_call(
        body, grid=(S // tm,), in_specs=[row, row, vec], out_specs=vec,
        out_shape=jax.ShapeDtypeStruct((1, Dm), F32), compiler_params=_cp(("arbitrary",)), name=name)(dh, x, g.reshape(1, Dm))
    return None, dg


GATE_TN = XA_WIDTH
GATE_MIX_TILES = MIX_WIDTH // GATE_TN


def _gate_cat_specs(tm):
    return [pl.BlockSpec((tm, GATE_TN), lambda i, j: (i, jnp.minimum(j, GATE_MIX_TILES - 1))),
            pl.BlockSpec((tm, GATE_TN), lambda i, j: (i, 0))]


def _gate_fwd(mix, xa, proj, z_off, *, name, tm=256):
    S = mix.shape[0]
    tm = min(tm, S)
    zb = z_off // GATE_TN

    def body(m_ref, x_ref, z_ref, y_ref):
        z = z_ref[...]
        c = jnp.where(pl.program_id(1) < GATE_MIX_TILES, m_ref[...], x_ref[...])
        y_ref[...] = (c * z * _sigmoid(z)).astype(BF16)

    blk = pl.BlockSpec((tm, GATE_TN), lambda i, j: (i, j))
    return pl.pallas_call(
        body, grid=(S // tm, INNER // GATE_TN),
        in_specs=_gate_cat_specs(tm) + [pl.BlockSpec((tm, GATE_TN), lambda i, j: (i, zb + j))],
        out_specs=blk, out_shape=jax.ShapeDtypeStruct((S, INNER), BF16),
        compiler_params=_cp(("parallel", "arbitrary")), name=name)(mix, xa, proj)


def _gate_bwd(dy, mix, xa, proj, z_off, *, name, tm=256):
    S = mix.shape[0]
    tm = min(tm, S)
    zb = z_off // GATE_TN

    def body(dy_ref, m_ref, x_ref, z_ref, dc_ref, dz_ref):
        z = z_ref[...]
        sg = _sigmoid(z)
        d = dy_ref[...]
        c = jnp.where(pl.program_id(1) < GATE_MIX_TILES, m_ref[...], x_ref[...])
        dc_ref[...] = d * z * sg
        dz_ref[...] = (d * c * sg * (1.0 + z * (1.0 - sg))).astype(BF16)

    blk = pl.BlockSpec((tm, GATE_TN), lambda i, j: (i, j))
    return pl.pallas_call(
        body, grid=(S // tm, INNER // GATE_TN),
        in_specs=[blk] + _gate_cat_specs(tm) + [pl.BlockSpec((tm, GATE_TN), lambda i, j: (i, zb + j))], out_specs=(blk, blk),
        out_shape=(jax.ShapeDtypeStruct((S, INNER), F32), jax.ShapeDtypeStruct((S, INNER), BF16)),
        compiler_params=_cp(("parallel", "arbitrary")), name=name)(dy, mix, xa, proj)


def _loss_head(x, target, *, name, tm=256):
    S, Dm = x.shape
    tm = min(tm, S)

    nt = S // tm

    def body(x_ref, t_ref, dx_ref, l_ref, acc):
        i = pl.program_id(0)
        e = x_ref[...] - t_ref[...]
        dx_ref[...] = e * (1.0 / Dm)

        @pl.when(i == 0)
        def _():
            acc[...] = jnp.zeros_like(acc)

        acc[...] += jnp.sum(e * e, axis=0, keepdims=True) * (0.5 / Dm)

        @pl.when(i == nt - 1)
        def _():
            l_ref[...] = jnp.sum(acc[...], axis=1, keepdims=True) + jnp.zeros((1, LANE), F32)

    row = pl.BlockSpec((tm, Dm), lambda i: (i, 0))
    return pl.pallas_call(
        body, grid=(nt,), in_specs=[row, row], out_specs=(row, pl.BlockSpec((1, LANE), lambda i: (0, 0))),
        out_shape=(jax.ShapeDtypeStruct((S, Dm), F32), jax.ShapeDtypeStruct((1, LANE), F32)),
        scratch_shapes=[pltpu.VMEM((1, Dm), F32)],
        compiler_params=_cp(("arbitrary",)), name=name)(x, target)


def _xa_norm(v, g):
    r = lax.rsqrt(jnp.mean(v * v, axis=-1, keepdims=True) + EPS)
    return v * r, r


def _xa_fwd(proj, xq_off, kv, gq, gk, *, name, tm=512):
    S = proj.shape[0]
    tm = min(tm, S)
    qb = xq_off // XA_DIM
    n_mem = kv.shape[0]
    scale = XA_DIM ** -0.5

    def body(q_ref, k_ref, v_ref, gq_ref, gk_ref, o_ref):
        qh, _ = _xa_norm(q_ref[...], None)
        kh, _ = _xa_norm(k_ref[...], None)
        qn = qh * gq_ref[...]
        kn = kh * gk_ref[...]
        s = _bdot(qn, kn, NT) * scale
        s = s - jnp.max(s, axis=-1, keepdims=True)
        p = jnp.exp(s)
        p = p / jnp.sum(p, axis=-1, keepdims=True)
        o_ref[...] = _bdot(p, v_ref[...], NN)

    vec = pl.BlockSpec((1, XA_DIM), lambda h, i: (0, 0))
    return pl.pallas_call(
        body, grid=(XA_HEADS, S // tm),
        in_specs=[pl.BlockSpec((tm, XA_DIM), lambda h, i: (i, qb + h)),
                  pl.BlockSpec((n_mem, XA_DIM), lambda h, i: (0, h)),
                  pl.BlockSpec((n_mem, XA_DIM), lambda h, i: (0, XA_HEADS + h)), vec, vec],
        out_specs=pl.BlockSpec((tm, XA_DIM), lambda h, i: (i, h)),
        out_shape=jax.ShapeDtypeStruct((S, XA_WIDTH), F32),
        compiler_params=_cp(("parallel", "parallel")), name=name)(proj, kv, kv, gq.reshape(1, XA_DIM), gk.reshape(1, XA_DIM))


def _xa_bwd(dcat, proj, xq_off, kv, gq, gk, *, name, tm=512):
    S = proj.shape[0]
    tm = min(tm, S)
    nt = S // tm
    qb = xq_off // XA_DIM
    db = MIX_WIDTH // XA_DIM
    n_mem = kv.shape[0]
    scale = XA_DIM ** -0.5

    def body(d_ref, q_ref, k_ref, v_ref, gq_ref, gk_ref, dq_ref, dk_ref, dv_ref, dgq_ref, dgk_ref, dkn_acc):
        h = pl.program_id(0)
        i = pl.program_id(1)
        q = q_ref[...]
        k = k_ref[...]
        qh, rq = _xa_norm(q, None)
        kh, rk = _xa_norm(k, None)
        gqv = gq_ref[...]
        gkv = gk_ref[...]
        qn = qh * gqv
        kn = kh * gkv
        s = _bdot(qn, kn, NT) * scale
        s = s - jnp.max(s, axis=-1, keepdims=True)
        p = jnp.exp(s)
        p = p / jnp.sum(p, axis=-1, keepdims=True)
        d = d_ref[...]
        dp = _bdot(d, v_ref[...], NT)
        ds = p * (dp - jnp.sum(dp * p, axis=-1, keepdims=True)) * scale
        dqn = _bdot(ds, kn, NN)

        @pl.when(i == 0)
        def _():
            dkn_acc[...] = jnp.zeros_like(dkn_acc)
            dv_ref[...] = jnp.zeros_like(dv_ref)

        @pl.when(jnp.logical_and(i == 0, h == 0))
        def _():
            dgq_ref[...] = jnp.zeros_like(dgq_ref)
            dgk_ref[...] = jnp.zeros_like(dgk_ref)

        dkn_acc[...] += _bdot(ds, qn, TN)
        dv_ref[...] += _bdot(p, d, TN)
        dgq_ref[...] += jnp.sum(dqn * qh, axis=0, keepdims=True)
        dy = dqn * gqv
        dq_ref[...] = (rq * (dy - qh * jnp.mean(dy * qh, axis=-1, keepdims=True))).astype(BF16)

        @pl.when(i == nt - 1)
        def _():
            dkn = dkn_acc[...]
            dgk_ref[...] += jnp.sum(dkn * kh, axis=0, keepdims=True)
            dyk = dkn * gkv
            dk_ref[...] = rk * (dyk - kh * jnp.mean(dyk * kh, axis=-1, keepdims=True))

    vec = pl.BlockSpec((1, XA_DIM), lambda h, i: (0, 0))
    kblk = pl.BlockSpec((n_mem, XA_DIM), lambda h, i: (0, h))
    vblk = pl.BlockSpec((n_mem, XA_DIM), lambda h, i: (0, XA_HEADS + h))
    dq, dk, dv, dgq, dgk = pl.pallas_call(
        body, grid=(XA_HEADS, nt),
        in_specs=[pl.BlockSpec((tm, XA_DIM), lambda h, i: (i, db + h)),
                  pl.BlockSpec((tm, XA_DIM), lambda h, i: (i, qb + h)), kblk, vblk, vec, vec],
        out_specs=(pl.BlockSpec((tm, XA_DIM), lambda h, i: (i, h)), kblk, kblk, vec, vec),
        out_shape=(jax.ShapeDtypeStruct((S, XA_WIDTH), BF16), jax.ShapeDtypeStruct((n_mem, XA_WIDTH), F32),
                   jax.ShapeDtypeStruct((n_mem, XA_WIDTH), F32), jax.ShapeDtypeStruct((1, XA_DIM), F32),
                   jax.ShapeDtypeStruct((1, XA_DIM), F32)),
        scratch_shapes=[pltpu.VMEM((n_mem, XA_DIM), F32)],
        compiler_params=_cp(("arbitrary", "arbitrary")), name=name)(
            dcat, proj, kv, kv, gq.reshape(1, XA_DIM), gk.reshape(1, XA_DIM))
    return dq, jnp.concatenate([dk, dv], axis=1), dgq, dgk


SB_TQ = 256
SB_TK = 256
SB_HEADS = 24


def _sb_tile(qi, kj, t0, s0, masked):
    z = _bdot(qi, kj, NT)
    sp = _softplus(z)
    ls = z - sp
    if not masked:
        return -sp, ls, None
    mask = (s0 + _iota2(z.shape, 1)) < (t0 + _iota2(z.shape, 0))
    return jnp.where(mask, -sp, 0.0), ls, mask


def _dot2(x, tri):
    hi = x.astype(BF16)
    lo = (x - hi.astype(F32)).astype(BF16)
    dims = (NN, ((), ()))
    return (lax.dot_general(hi, tri, dims, preferred_element_type=F32)
            + lax.dot_general(lo, tri, dims, preferred_element_type=F32))


def _sb_fwd(proj, gq, gk, *, name):
    S = proj.shape[0]
    tq, tk = min(SB_TQ, S), min(SB_TK, S)
    nq = S // tq
    scale = HEAD_DIM ** -0.5

    def body(q_ref, k_ref, v_ref, gq_ref, gk_ref, o_ref, tot_ref, qn_s, kn_s, v_s):
        q = q_ref[...]
        k = k_ref[...]
        qn_s[...] = (q * lax.rsqrt(jnp.mean(q * q, axis=-1, keepdims=True) + EPS) * (gq_ref[...] * scale)).astype(BF16)
        kn_s[...] = (k * lax.rsqrt(jnp.mean(k * k, axis=-1, keepdims=True) + EPS) * gk_ref[...]).astype(BF16)
        v_s[...] = v_ref[...].astype(BF16)
        after = (_iota2((tk, tk), 0) > _iota2((tk, tk), 1)).astype(BF16)

        def qblock(i, _):
            rows = pl.ds(pl.multiple_of(i * tq, tq), tq)
            qi = qn_s[rows, :]
            jd = (i * tq) // tk

            def tile(j, acc, run, masked):
                cols = pl.ds(pl.multiple_of(j * tk, tk), tk)
                lr, ls, mask = _sb_tile(qi, kn_s[cols, :], i * tq, j * tk, masked)
                later = _dot2(lr, after) + run
                a = jnp.exp(ls + later)
                if masked:
                    a = jnp.where(mask, a, 0.0)
                acc = acc + _bdot(a, v_s[cols, :], NN)
                return acc, run + jnp.sum(lr, axis=-1, keepdims=True)

            acc, run = tile(jd, jnp.zeros((tq, HEAD_DIM), F32), jnp.zeros((tq, 1), F32), True)
            acc, run = lax.fori_loop(0, jd, lambda jj, c: tile(jd - 1 - jj, c[0], c[1], False), (acc, run))
            o_ref[rows, :] = acc
            tot_ref[rows, :] = run + jnp.zeros((tq, HEAD_DIM), F32)
            return 0

        lax.fori_loop(0, nq, qblock, 0)

    vec = pl.BlockSpec((1, HEAD_DIM), lambda h: (0, 0))
    out = pl.BlockSpec((S, HEAD_DIM), lambda h: (0, h))
    return pl.pallas_call(
        body, grid=(SB_HEADS,),
        in_specs=[pl.BlockSpec((S, HEAD_DIM), lambda h: (0, h)), pl.BlockSpec((S, HEAD_DIM), lambda h: (0, SB_HEADS + h)),
                  pl.BlockSpec((S, HEAD_DIM), lambda h: (0, 2 * SB_HEADS + h)), vec, vec],
        out_specs=(out, out), out_shape=(jax.ShapeDtypeStruct((S, MIX_WIDTH), F32),) * 2,
        scratch_shapes=[pltpu.VMEM((S, HEAD_DIM), BF16)] * 3,
        compiler_params=_cp(("parallel",)), name=name)(proj, proj, proj, gq.reshape(1, HEAD_DIM), gk.reshape(1, HEAD_DIM))


def _sb_bwd(dmix, tot, proj, gq, gk, *, name):
    S = proj.shape[0]
    tq, tk = min(SB_TQ, S), min(SB_TK, S)
    nq = S // tq
    scale = HEAD_DIM ** -0.5

    def body(do_ref, o_ref, q_ref, k_ref, v_ref, gq_ref, gk_ref, dq_ref, dk_ref, dv_ref, dgq_ref, dgk_ref,
             qn_s, kn_s, v_s, dkn_s, dqn_s, dv_s):
        h = pl.program_id(0)
        q = q_ref[...]
        k = k_ref[...]
        rq = lax.rsqrt(jnp.mean(q * q, axis=-1, keepdims=True) + EPS)
        rk = lax.rsqrt(jnp.mean(k * k, axis=-1, keepdims=True) + EPS)
        gqv = gq_ref[...]
        gkv = gk_ref[...]
        qn_s[...] = (q * rq * (gqv * scale)).astype(BF16)
        kn_s[...] = (k * rk * gkv).astype(BF16)
        v_s[...] = v_ref[...].astype(BF16)
        dkn_s[...] = jnp.zeros_like(dkn_s)
        dv_s[...] = jnp.zeros_like(dv_s)
        r_i = _iota2((tk, tk), 0)
        c_i = _iota2((tk, tk), 1)
        upto = (r_i <= c_i).astype(BF16)
        before = (r_i < c_i).astype(BF16)

        def qblock(i, _):
            rows = pl.ds(pl.multiple_of(i * tq, tq), tq)
            qi = qn_s[rows, :]
            doi = do_ref[rows, :].astype(BF16)
            tot_i = jnp.max(o_ref[rows, :], axis=-1, keepdims=True)
            jd = (i * tq) // tk

            def tile(j, dqn, run, run_b, masked):
                cols = pl.ds(pl.multiple_of(j * tk, tk), tk)
                kj = kn_s[cols, :]
                lr, ls, mask = _sb_tile(qi, kj, i * tq, j * tk, masked)
                later = tot_i - (_dot2(lr, upto) + run)
                a = jnp.exp(ls + later)
                if masked:
                    a = jnp.where(mask, a, 0.0)
                b = _bdot(doi, v_s[cols, :], NT) * a
                cum = _dot2(b, before) + run_b
                beta = jnp.exp(ls)
                dz = b * (1.0 - beta) - cum * beta
                if masked:
                    dz = jnp.where(mask, dz, 0.0)
                dzb = dz.astype(BF16)
                dv_s[cols, :] += _bdot(a, doi, TN)
                dkn_s[cols, :] += _bdot(dzb, qi, TN)
                dqn = dqn + _bdot(dzb, kj, NN)
                return dqn, run + jnp.sum(lr, axis=-1, keepdims=True), run_b + jnp.sum(b, axis=-1, keepdims=True)

            zero1 = jnp.zeros((tq, 1), F32)
            carry = lax.fori_loop(0, jd, lambda j, c: tile(j, c[0], c[1], c[2], False),
                                  (jnp.zeros((tq, HEAD_DIM), F32), zero1, zero1))
            dqn, _, _ = tile(jd, carry[0], carry[1], carry[2], True)
            dqn_s[rows, :] = dqn * scale
            return 0

        lax.fori_loop(0, nq, qblock, 0)

        @pl.when(h == 0)
        def _():
            dgq_ref[...] = jnp.zeros_like(dgq_ref)
            dgk_ref[...] = jnp.zeros_like(dgk_ref)

        dv_ref[...] = dv_s[...].astype(BF16)
        dqn = dqn_s[...]
        qh = q * rq
        dgq_ref[...] += jnp.sum(dqn * qh, axis=0, keepdims=True)
        dy = dqn * gqv
        dq_ref[...] = (rq * (dy - qh * jnp.mean(dy * qh, axis=-1, keepdims=True))).astype(BF16)
        dkn = dkn_s[...]
        kh = k * rk
        dgk_ref[...] += jnp.sum(dkn * kh, axis=0, keepdims=True)
        dyk = dkn * gkv
        dk_ref[...] = (rk * (dyk - kh * jnp.mean(dyk * kh, axis=-1, keepdims=True))).astype(BF16)

    vec = pl.BlockSpec((1, HEAD_DIM), lambda h: (0, 0))
    hb = lambda off: pl.BlockSpec((S, HEAD_DIM), lambda h: (0, off + h))
    dq, dk, dv, dgq, dgk = pl.pallas_call(
        body, grid=(SB_HEADS,),
        in_specs=[hb(0), hb(0), hb(0), hb(SB_HEADS), hb(2 * SB_HEADS), vec, vec],
        out_specs=(hb(0), hb(0), hb(0), vec, vec),
        out_shape=(jax.ShapeDtypeStruct((S, MIX_WIDTH), BF16),) * 3 + (jax.ShapeDtypeStruct((1, HEAD_DIM), F32),) * 2,
        scratch_shapes=[pltpu.VMEM((S, HEAD_DIM), BF16)] * 3 + [pltpu.VMEM((S, HEAD_DIM), F32)] * 3,
        compiler_params=_cp(("arbitrary",)), name=name)(
            dmix, tot, proj, proj, proj, gq.reshape(1, HEAD_DIM), gk.reshape(1, HEAD_DIM))
    return [dq, dk, dv], dgq, dgk


def _shift_down(x, k):
    if k == 0:
        return x
    r = pltpu.roll(x, k, 0)
    return jnp.where(_iota2(x.shape, 0) >= k, r, 0.0)


def _shift_up(x, k):
    if k == 0:
        return x
    n = x.shape[0]
    r = pltpu.roll(x, n - k, 0)
    return jnp.where(_iota2(x.shape, 0) < n - k, r, 0.0)


def _conv(x, w):
    c = w[DN_CONV - 1] * x
    for k in range(1, DN_CONV):
        c = c + w[DN_CONV - 1 - k] * _shift_down(x, k)
    return c


def _dn_pre_fwd(proj, conv_w, col0, ncols, *, l2, scale, name):
    S = proj.shape[0]
    cb = col0 // HEAD_DIM

    def body(x_ref, w_ref, o_ref):
        c = _conv(x_ref[...], [w_ref[k:k + 1, :] for k in range(DN_CONV)])
        a = c * _sigmoid(c)
        if l2:
            a = a * (lax.rsqrt(jnp.sum(a * a, axis=-1, keepdims=True) + EPS) * scale)
        o_ref[...] = a

    return pl.pallas_call(
        body, grid=(ncols // HEAD_DIM,),
        in_specs=[pl.BlockSpec((S, HEAD_DIM), lambda j: (0, cb + j)), pl.BlockSpec((DN_CONV, HEAD_DIM), lambda j: (0, cb + j))],
        out_specs=pl.BlockSpec((S, HEAD_DIM), lambda j: (0, j)), out_shape=jax.ShapeDtypeStruct((S, ncols), F32),
        compiler_params=_cp(("parallel",)), name=name)(proj, conv_w)


def _dn_pre_bwd(dout, proj, conv_w, col0, ncols, *, l2, scale, name):
    S = proj.shape[0]
    cb = col0 // HEAD_DIM
    dw_in = HEAD_DIM

    def body(d_ref, x_ref, w_ref, dx_ref, dw_ref):
        x = x_ref[...]
        w = [w_ref[k:k + 1, :] for k in range(DN_CONV)]
        c = _conv(x, w)
        sg = _sigmoid(c)
        a = c * sg
        d = d_ref[...]
        if l2:
            r = lax.rsqrt(jnp.sum(a * a, axis=-1, keepdims=True) + EPS)
            y = a * r
            d = d * scale
            d = r * (d - y * jnp.sum(d * y, axis=-1, keepdims=True))
        dc = d * sg * (1.0 + c * (1.0 - sg))
        dx = w[DN_CONV - 1] * dc
        for k in range(1, DN_CONV):
            dx = dx + w[DN_CONV - 1 - k] * _shift_up(dc, k)
        dx_ref[...] = dx.astype(BF16)
        for k in range(DN_CONV):
            dw_ref[3 - k:4 - k, :] = jnp.sum(dc * _shift_down(x, k), axis=0, keepdims=True)

    return pl.pallas_call(
        body, grid=(ncols // HEAD_DIM,),
        in_specs=[pl.BlockSpec((S, dw_in), lambda j: (0, j)), pl.BlockSpec((S, HEAD_DIM), lambda j: (0, cb + j)),
                  pl.BlockSpec((DN_CONV, HEAD_DIM), lambda j: (0, cb + j))],
        out_specs=(pl.BlockSpec((S, HEAD_DIM), lambda j: (0, j)), pl.BlockSpec((DN_CONV, HEAD_DIM), lambda j: (0, j))),
        out_shape=(jax.ShapeDtypeStruct((S, ncols), BF16), jax.ShapeDtypeStruct((DN_CONV, ncols), F32)),
        compiler_params=_cp(("parallel",)), name=name)(dout, proj, conv_w)


def _dn_ab_fwd(proj, a_log, dt_bias, *, name, tm=512):
    S = proj.shape[0]
    tm = min(tm, S)
    ab = P0_AB // LANE

    def body(a_ref, b_ref, al_ref, dt_ref, g_ref, be_ref):
        g_ref[...] = -jnp.exp(al_ref[...]) * _softplus(a_ref[...] + dt_ref[...])
        be_ref[...] = _sigmoid(b_ref[...])

    vec = pl.BlockSpec((1, LANE), lambda i: (0, 0))
    out = pl.BlockSpec((tm, LANE), lambda i: (i, 0))
    return pl.pallas_call(
        body, grid=(S // tm,),
        in_specs=[pl.BlockSpec((tm, LANE), lambda i: (i, ab)), pl.BlockSpec((tm, LANE), lambda i: (i, ab + 1)), vec, vec],
        out_specs=(out, out), out_shape=(jax.ShapeDtypeStruct((S, LANE), F32),) * 2,
        compiler_params=_cp(("parallel",)), name=name)(proj, proj, a_log, dt_bias)


def _dn_ab_bwd(dg, dbeta, proj, a_log, dt_bias, *, name, tm=512):
    S = proj.shape[0]
    tm = min(tm, S)
    ab = P0_AB // LANE

    def body(dg_ref, db_ref, a_ref, b_ref, al_ref, dt_ref, dab_ref, dal_ref, ddt_ref):
        i = pl.program_id(0)
        ea = jnp.exp(al_ref[...])
        u = a_ref[...] + dt_ref[...]
        dgv = dg_ref[...]
        da = dgv * (-ea) * _sigmoid(u)
        be = _sigmoid(b_ref[...])
        dab_ref[:, 0:LANE] = da.astype(BF16)
        dab_ref[:, LANE:2 * LANE] = (db_ref[...] * be * (1.0 - be)).astype(BF16)
        dab_ref[:, 2 * LANE:] = jnp.zeros((tm, 2 * LANE), BF16)

        @pl.when(i == 0)
        def _():
            dal_ref[...] = jnp.zeros_like(dal_ref)
            ddt_ref[...] = jnp.zeros_like(ddt_ref)

        dal_ref[...] += jnp.sum(dgv * (-ea) * _softplus(u), axis=0, keepdims=True)
        ddt_ref[...] += jnp.sum(da, axis=0, keepdims=True)

    vec = pl.BlockSpec((1, LANE), lambda i: (0, 0))
    row = pl.BlockSpec((tm, LANE), lambda i: (i, 0))
    return pl.pallas_call(
        body, grid=(S // tm,),
        in_specs=[row, row, pl.BlockSpec((tm, LANE), lambda i: (i, ab)), pl.BlockSpec((tm, LANE), lambda i: (i, ab + 1)), vec, vec],
        out_specs=(pl.BlockSpec((tm, 4 * LANE), lambda i: (i, 0)), vec, vec),
        out_shape=(jax.ShapeDtypeStruct((S, 4 * LANE), BF16), jax.ShapeDtypeStruct((1, LANE), F32),
                   jax.ShapeDtypeStruct((1, LANE), F32)),
        compiler_params=_cp(("arbitrary",)), name=name)(dg, dbeta, proj, proj, a_log, dt_bias)


def _dot3(a, b):
    ah = a.astype(BF16)
    al = (a - ah.astype(F32)).astype(BF16)
    bh = b.astype(BF16)
    bl = (b - bh.astype(F32)).astype(BF16)
    d = lambda u, v: lax.dot_general(u, v, (NN, ((), ())), preferred_element_type=F32)
    return d(ah, bh) + (d(ah, bl) + d(al, bh))


DN_PAIR = 2


def _pdot(a, b, dims, dot=None):
    dot = dot or _bdot
    return jnp.stack([dot(a[i] if a.ndim == 3 else a, b[i] if b.ndim == 3 else b, dims) for i in range(DN_PAIR)])


def _tri_inverse(a):
    eye = (_iota2((CH, CH), 0) == _iota2((CH, CH), 1)).astype(F32)
    d3 = lambda u, v: jnp.stack([_dot3(u[i], v[i]) for i in range(DN_PAIR)])
    t = eye - a
    x = d3(a, a)
    n = 2
    while True:
        t = t + d3(t, x)
        n *= 2
        if n >= CH:
            break
        x = d3(x, x)
    return t


def _pick_col(m, n):
    return jnp.sum(jnp.where(_iota2(m.shape, 2) == n, m, 0.0), axis=2, keepdims=True)


def _dn_chunk_common(kk, qk, gc_c, gc_r, be_c):
    r_i = _iota2((CH, CH), 0)
    c_i = _iota2((CH, CH), 1)
    incl = r_i >= c_i
    strict = r_i > c_i
    dec = jnp.exp(jnp.where(incl, gc_c - gc_r, -1e30))
    e = jnp.exp(gc_c)
    gl = jnp.sum(jnp.where(_iota2((1, CH), 1) == CH - 1, gc_r, 0.0), axis=-1, keepdims=True)
    kds = jnp.exp(gl - gc_c)
    cd = jnp.exp(gl)
    a = jnp.where(strict, be_c * kk * dec, 0.0)
    p = qk * dec
    return dict(incl=incl, strict=strict, dec=dec, e=e, kds=kds, cd=cd, kk=kk, a=a, qk=qk, p=p)


def _dn_decay_tables(g_ref, b_ref, gcr, gcc, bcc):
    r_i = _iota2((CH, CH), 0)
    c_i = _iota2((CH, CH), 1)
    lc = (r_i >= c_i).astype(F32)
    eye = (r_i == c_i).astype(F32)
    for hh in range(DN_PAIR):
        g_rows_v = g_ref[hh]
        gcr[hh] = _fdot(g_rows_v, lc, NT)
        gcc[hh] = _fdot(lc, g_rows_v, NT)
        bcc[hh] = _fdot(eye, b_ref[hh], NT)
    return lc


def _dn_core_fwd(qn, kn, vc, g_rows, b_rows, out_g, *, name):
    S = qn.shape[0]
    nc = S // CH

    def body(q_ref, k_ref, v_ref, g_ref, b_ref, og_ref, o_ref, st_ref, t_ref, gcr, gcc, bcc):
        _dn_decay_tables(g_ref, b_ref, gcr, gcc, bcc)
        ogv = og_ref[...]

        def chunk(n, states):
            rows = pl.ds(pl.multiple_of(n * CH, CH), CH)
            q = q_ref[rows, :]
            k = k_ref[rows, :]
            kk = _bdot(k, k, NT)
            qk = _bdot(q, k, NT)
            v = jnp.stack([v_ref[rows, hh * HEAD_DIM:(hh + 1) * HEAD_DIM] for hh in range(DN_PAIR)])
            gc_c = _pick_col(gcc[...], n)
            be_c = _pick_col(bcc[...], n)
            gc_r = gcr[:, pl.ds(n, 1), :]
            c = _dn_chunk_common(kk, qk, gc_c, gc_r, be_c)
            t = _tri_inverse(c["a"])
            u0 = _pdot(t, be_c * v, NN)
            w = _pdot(t, (be_c * c["e"]) * k, NN)
            u = u0 - _pdot(w, states, NN)
            o = _pdot(c["e"] * q, states, NN) + _pdot(c["p"], u, NN)
            on = o * lax.rsqrt(jnp.mean(o * o, axis=-1, keepdims=True) + EPS) * ogv
            for hh in range(DN_PAIR):
                st_ref[hh, n] = states[hh]
                t_ref[hh, n] = t[hh]
                o_ref[rows, hh * HEAD_DIM:(hh + 1) * HEAD_DIM] = on[hh]
            return c["cd"] * states + _pdot(c["kds"] * k, u, TN)

        lax.fori_loop(0, nc, chunk, jnp.zeros((DN_PAIR, HEAD_DIM, HEAD_DIM), F32))

    qk_spec = pl.BlockSpec((S, HEAD_DIM), lambda h: (0, h))
    v_spec = pl.BlockSpec((S, DN_PAIR * HEAD_DIM), lambda h: (0, h))
    rows_spec = pl.BlockSpec((DN_PAIR, LANE, CH), lambda h: (h, 0, 0))
    return pl.pallas_call(
        body, grid=(DN_V_HEADS // DN_PAIR,),
        in_specs=[qk_spec, qk_spec, v_spec, rows_spec, rows_spec, pl.BlockSpec((1, HEAD_DIM), lambda h: (0, 0))],
        out_specs=(v_spec, pl.BlockSpec((DN_PAIR, nc, HEAD_DIM, HEAD_DIM), lambda h: (h, 0, 0, 0)),
                   pl.BlockSpec((DN_PAIR, nc, CH, CH), lambda h: (h, 0, 0, 0))),
        out_shape=(jax.ShapeDtypeStruct((S, MIX_WIDTH), F32), jax.ShapeDtypeStruct((DN_V_HEADS, nc, HEAD_DIM, HEAD_DIM), F32),
                   jax.ShapeDtypeStruct((DN_V_HEADS, nc, CH, CH), F32)),
        scratch_shapes=[pltpu.VMEM((DN_PAIR, LANE, CH), F32), pltpu.VMEM((DN_PAIR, CH, LANE), F32),
                        pltpu.VMEM((DN_PAIR, CH, LANE), F32)],
        compiler_params=_cp(("parallel",)), name=name)(qn, kn, vc, g_rows, b_rows, out_g.reshape(1, HEAD_DIM))


def _dn_chunk_bwd(q, k, v, kk, qk, state, t, gc_c, gc_r, be_c, don, ogv, ds_next):
    ones = jnp.ones((CH, LANE), F32)
    last_row = _iota2((CH, 1), 0) == CH - 1
    rowsum = lambda z: jnp.sum(z, axis=-1, keepdims=True)
    colsum = lambda z: jnp.sum(z, axis=-2, keepdims=True)
    c = _dn_chunk_common(kk, qk, gc_c, gc_r, be_c)
    e, kds, cd, dec, a, p = c["e"], c["kds"], c["cd"], c["dec"], c["a"], c["p"]
    vb = be_c * v
    kbe = (be_c * e) * k
    u0 = _pdot(t, vb, NN)
    w = _pdot(t, kbe, NN)
    u = u0 - _pdot(w, state, NN)
    qd = e * q
    kd = kds * k
    o = _pdot(qd, state, NN) + _pdot(p, u, NN)
    r = lax.rsqrt(jnp.mean(o * o, axis=-1, keepdims=True) + EPS)
    y = o * r
    dog = colsum(don * y)
    dy = don * ogv
    d_o = r * (dy - y * jnp.mean(dy * y, axis=-1, keepdims=True))
    du = _pdot(p, d_o, TN) + _pdot(kd, ds_next, NN)
    dqd = _pdot(d_o, state, NT)
    dstate = _pdot(qd, d_o, TN) + cd * ds_next - _pdot(w, du, TN)
    dcd = colsum(rowsum(ds_next * state))
    dkd = _pdot(u, ds_next, NT)
    dw = -_pdot(du, state, NT)
    dvb = _pdot(t, du, TN)
    dkbe = _pdot(t, dw, TN)
    da = -jnp.where(c["strict"], _pdot(dvb, u0, NT) + _pdot(dkbe, w, NT), 0.0)
    dp = jnp.where(c["incl"], _pdot(d_o, u, NT), 0.0)
    gmat = da * a + dp * p
    dad = da * dec
    x = be_c * dad
    dpd = dp * dec
    dk = _pdot(x, k, NN) + _pdot(x, k, TN) + _pdot(dpd, q, TN)
    dq = _pdot(dpd, k, NN) + e * dqd
    dbe = rowsum(dad * c["kk"])
    dgc = rowsum(gmat) + rowsum(dqd * q) * e
    rk = rowsum(dkd * k) * kds
    dk = dk + kds * dkd
    dgc = dgc - rk
    dgl = colsum(rk) + dcd * cd
    sk = rowsum(dkbe * k)
    dk = dk + (be_c * e) * dkbe
    dbe = dbe + sk * e + rowsum(dvb * v)
    dgc = dgc + sk * be_c * e
    dgc = dgc + jnp.where(last_row, dgl, 0.0)
    dgc = dgc - _pdot(gmat, ones, TN, dot=_fdot)
    return dq, dk, be_c * dvb, dgc, dbe, dog, dstate


def _dn_core_bwd(dmix, qn, kn, vc, g_rows, b_rows, out_g, states, tinv, *, name):
    S = qn.shape[0]
    nc = S // CH

    def body(do_ref, q_ref, k_ref, v_ref, g_ref, b_ref, og_ref, st_ref, t_ref,
             dq_ref, dk_ref, dv_ref, dg_ref, db_ref, dog_ref, gcr, gcc, bcc, dgc_acc):
        h = pl.program_id(0)
        lc = _dn_decay_tables(g_ref, b_ref, gcr, gcc, bcc)
        ogv = og_ref[...]
        dgc_acc[...] = jnp.zeros_like(dgc_acc)
        db_ref[...] = jnp.zeros_like(db_ref)
        lane_n = _iota2((CH, LANE), 1)

        @pl.when(h == 0)
        def _():
            dog_ref[...] = jnp.zeros_like(dog_ref)

        def chunk(m, carry):
            ds_nexts, dog = carry
            n = nc - 1 - m
            rows = pl.ds(pl.multiple_of(n * CH, CH), CH)
            q = q_ref[rows, :]
            k = k_ref[rows, :]
            kk = _bdot(k, k, NT)
            qk = _bdot(q, k, NT)
            heads = lambda ref: jnp.stack([ref[rows, hh * HEAD_DIM:(hh + 1) * HEAD_DIM] for hh in range(DN_PAIR)])
            state = jnp.stack([st_ref[hh, n] for hh in range(DN_PAIR)])
            t = jnp.stack([t_ref[hh, n] for hh in range(DN_PAIR)])
            dq, dk, dv, dgc, dbe, dog_h, dstate = _dn_chunk_bwd(
                q, k, heads(v_ref), kk, qk, state, t, _pick_col(gcc[...], n), gcr[:, pl.ds(n, 1), :],
                _pick_col(bcc[...], n), heads(do_ref), ogv, ds_nexts)
            for hh in range(DN_PAIR):
                dv_ref[rows, hh * HEAD_DIM:(hh + 1) * HEAD_DIM] = dv[hh]
            dgc_acc[...] = jnp.where(lane_n == n, dgc, dgc_acc[...])
            db_ref[...] = jnp.where(lane_n == n, dbe, db_ref[...])
            dq_ref[rows, :] = jnp.sum(dq, axis=0)
            dk_ref[rows, :] = jnp.sum(dk, axis=0)
            return dstate, dog + jnp.sum(dog_h, axis=0)

        _, dog = lax.fori_loop(0, nc, chunk, (jnp.zeros((DN_PAIR, HEAD_DIM, HEAD_DIM), F32), jnp.zeros((1, HEAD_DIM), F32)))
        dog_ref[...] += dog
        for hh in range(DN_PAIR):
            dg_ref[hh] = _fdot(lc, dgc_acc[hh], TN)

    qk_spec = pl.BlockSpec((S, HEAD_DIM), lambda h: (0, h))
    v_spec = pl.BlockSpec((S, DN_PAIR * HEAD_DIM), lambda h: (0, h))
    rows_spec = pl.BlockSpec((DN_PAIR, LANE, CH), lambda h: (h, 0, 0))
    cols_spec = pl.BlockSpec((DN_PAIR, CH, LANE), lambda h: (h, 0, 0))
    vec = pl.BlockSpec((1, HEAD_DIM), lambda h: (0, 0))
    qk_out = jax.ShapeDtypeStruct((S, DN_QK_WIDTH), F32)
    return pl.pallas_call(
        body, grid=(DN_V_HEADS // DN_PAIR,),
        in_specs=[v_spec, qk_spec, qk_spec, v_spec, rows_spec, rows_spec, vec,
                  pl.BlockSpec((DN_PAIR, nc, HEAD_DIM, HEAD_DIM), lambda h: (h, 0, 0, 0)),
                  pl.BlockSpec((DN_PAIR, nc, CH, CH), lambda h: (h, 0, 0, 0))],
        out_specs=(qk_spec, qk_spec, v_spec, cols_spec, cols_spec, vec),
        out_shape=(qk_out, qk_out, jax.ShapeDtypeStruct((S, MIX_WIDTH), F32), jax.ShapeDtypeStruct((DN_V_HEADS, CH, LANE), F32),
                   jax.ShapeDtypeStruct((DN_V_HEADS, CH, LANE), F32), jax.ShapeDtypeStruct((1, HEAD_DIM), F32)),
        scratch_shapes=[pltpu.VMEM((DN_PAIR, LANE, CH), F32), pltpu.VMEM((DN_PAIR, CH, LANE), F32),
                        pltpu.VMEM((DN_PAIR, CH, LANE), F32), pltpu.VMEM((DN_PAIR, CH, LANE), F32)],
        compiler_params=_cp(("arbitrary",)), name=name)(
            dmix, qn, kn, vc, g_rows, b_rows, out_g.reshape(1, HEAD_DIM), states, tinv)


def _rows_form(x, nc):
    t = x[:, :DN_V_HEADS].T.reshape(DN_V_HEADS, nc, CH)
    return jnp.pad(t, ((0, 0), (0, LANE - nc), (0, 0)))


def _cols_to_nat(x, nc):
    t = jnp.transpose(x[:, :, :nc], (2, 1, 0)).reshape(nc * CH, DN_V_HEADS)
    return jnp.pad(t, ((0, 0), (0, LANE - DN_V_HEADS)))


_C_QKV = 2 * DN_QK_WIDTH + MIX_WIDTH


def _w0_to_padded(w):
    rows = w.shape[0]
    z = lambda n: jnp.zeros((rows, n), w.dtype)
    a = w[:, _C_QKV:_C_QKV + DN_V_HEADS]
    b = w[:, _C_QKV + DN_V_HEADS:_C_QKV + 2 * DN_V_HEADS]
    return jnp.concatenate([w[:, :_C_QKV], w[:, _C_QKV + 2 * DN_V_HEADS:], a, z(LANE - DN_V_HEADS), b,
                            z(P0 - P0_AB - LANE - DN_V_HEADS)], axis=1)


def _w0_from_padded(g):
    return jnp.concatenate([g[:, :_C_QKV], g[:, P0_AB:P0_AB + DN_V_HEADS], g[:, P0_AB + LANE:P0_AB + LANE + DN_V_HEADS],
                            g[:, _C_QKV:P0_AB]], axis=1)


def _true_pieces(lo, hi):
    out = []
    while lo < hi:
        s = lo // P0_SHARD
        end = min(hi, (s + 1) * P0_SHARD)
        out.append((s, lo - s * P0_SHARD, end - s * P0_SHARD))
        lo = end
    return out


def _padded_pieces(lo, hi):
    a0, b0, x0 = _C_QKV, _C_QKV + DN_V_HEADS, _C_QKV + 2 * DN_V_HEADS
    out = []
    for t0, t1, shift in ((0, a0, 0), (a0, b0, P0_AB - a0), (b0, x0, P0_AB + LANE - b0), (x0, DN_PROJ, a0 - x0)):
        s, e = max(lo, t0), min(hi, t1)
        if s < e:
            out.append((s + shift, e + shift))
    return out


def _pad_lane(v):
    v = v.reshape(1, -1)
    return jnp.pad(v, ((0, 0), (0, LANE - v.shape[1])))


SLOT1 = SB_PROJ // N_CHIPS
MM_TN = 512


def _local_step(x, mem, target, norm_g, mem_norm_g, xa_q_g, xa_k_g, w_in0, conv_w, a_log, dt_bias, out_g, sb_q_g, sb_k_g,
                late_weights, early_grads):
    S = x.shape[0]
    nc = S // CH
    al = _pad_lane(a_log)
    dtb = _pad_lane(dt_bias)
    q_scale = HEAD_DIM ** -0.5
    tiles1 = SLOT1 // MM_TN

    kv_rhs = lambda l: pl.BlockSpec((N_CHIPS, None, D_MODEL // N_CHIPS, MM_TN), lambda i, j: (0, l, 0, j))
    kv_rhs_t = lambda l: pl.BlockSpec((None, None, D_MODEL // N_CHIPS, 2 * XA_WIDTH), lambda i, j: (j, l, 0, 0))
    out_rhs = lambda l: pl.BlockSpec((N_CHIPS, None, INNER // N_CHIPS, MM_TN), lambda i, j: (0, l, 0, j))
    out_rhs_t = lambda l: pl.BlockSpec((None, None, MM_TN, D_MODEL), lambda i, j: (j // 2, l, j % 2, 0))
    in1_rhs = pl.BlockSpec((None, 2, D_MODEL // 2, MM_TN), lambda i, j: (j // tiles1, 0, 0, j % tiles1))
    in1_rhs_t = pl.BlockSpec((None, None, D_MODEL // 2, MM_TN), lambda i, j, k: (k // tiles1, j, 0, k % tiles1))
    slot_rows = lambda rows: dict(
        tm=rows, o_spec=pl.BlockSpec((None, rows, MM_TN), lambda i, j: (i, 0, j)),
        o_shape=jax.ShapeDtypeStruct((N_CHIPS, rows, 2 * XA_WIDTH), BF16))
    in1_out = dict(tm=D_MODEL // 2, o_spec=pl.BlockSpec((None, None, D_MODEL // 2, MM_TN),
                                                        lambda i, j: (j // tiles1, i, 0, j % tiles1)),
                   o_shape=jax.ShapeDtypeStruct((N_CHIPS, 2, D_MODEL // 2, SLOT1), BF16))

    h0 = _rmsnorm_fwd(x, norm_g[0], name="norm0")
    proj0 = _matmul(h0, w_in0, name="proj0")
    qn = _dn_pre_fwd(proj0, conv_w, 0, DN_QK_WIDTH, l2=True, scale=q_scale, name="dn_pre_q")
    kn = _dn_pre_fwd(proj0, conv_w, DN_QK_WIDTH, DN_QK_WIDTH, l2=True, scale=1.0, name="dn_pre_k")
    vc = _dn_pre_fwd(proj0, conv_w, 2 * DN_QK_WIDTH, MIX_WIDTH, l2=False, scale=1.0, name="dn_pre_v")
    g_nat, b_nat = _dn_ab_fwd(proj0, al, dtb, name="dn_ab")
    g_rows = _rows_form(g_nat, nc)
    b_rows = _rows_form(b_nat, nc)
    mix0, states, tinv = _dn_core_fwd(qn, kn, vc, g_rows, b_rows, out_g, name="dn_core")
    w_kv, w_out, w_in1 = late_weights(mix0)
    mem_n = _rmsnorm_fwd(mem, mem_norm_g, name="mem_norm")
    kv = [_matmul(mem_n, w_kv, n=2 * XA_WIDTH, tn=MM_TN, b_spec=kv_rhs(l), name=f"kv{l}") for l in range(2)]
    xa0 = _xa_fwd(proj0, P0_XQ, kv[0], xa_q_g[0], xa_k_g[0], name="xa0")
    y0 = _gate_fwd(mix0, xa0, proj0, P0_Z, name="gate0")
    x1 = _matmul(y0, w_out, n=D_MODEL, tn=MM_TN, b_spec=out_rhs(0), res=x, name="out0")

    h1 = _rmsnorm_fwd(x1, norm_g[1], name="norm1")
    proj1 = _matmul(h1, w_in1, n=SB_PROJ, tn=MM_TN, b_spec=in1_rhs, name="proj1")
    mix1, tot1 = _sb_fwd(proj1, sb_q_g, sb_k_g, name="sb")
    xa1 = _xa_fwd(proj1, P1_XQ, kv[1], xa_q_g[1], xa_k_g[1], name="xa1")
    y1 = _gate_fwd(mix1, xa1, proj1, P1_Z, name="gate1")
    x2 = _matmul(y1, w_out, n=D_MODEL, tn=MM_TN, b_spec=out_rhs(1), res=x1, name="out1")

    dx2, loss_vec = _loss_head(x2, target, name="loss")

    d_wout1 = _matmul(y1, dx2, ta=True, name="d_wout1", **slot_rows(INNER // N_CHIPS))
    dy1 = _matmul(dx2, w_out, tb=True, n=INNER, tn=MM_TN, b_spec=out_rhs_t(1), name="dy1")
    dcat1, dz1 = _gate_bwd(dy1, mix1, xa1, proj1, P1_Z, name="gate1_bwd")
    dqkv1, d_sbq, d_sbk = _sb_bwd(dcat1, tot1, proj1, sb_q_g, sb_k_g, name="sb_bwd")
    dxq1, dkv1, d_xaq1, d_xak1 = _xa_bwd(dcat1, proj1, P1_XQ, kv[1], xa_q_g[1], xa_k_g[1], name="xa1_bwd")
    dproj1 = dqkv1 + [dxq1, dz1]
    d_win1 = _matmul(h1, dproj1, ta=True, name="d_win1", **in1_out)
    d_wkv1 = _matmul(mem_n, dkv1, ta=True, name="d_wkv1", **slot_rows(D_MODEL // N_CHIPS))
    token = early_grads(d_win1, d_wout1, d_wkv1)
    dh1 = _matmul(dproj1, w_in1, tb=True, n=D_MODEL, tn=D_MODEL // 2, tk=MM_TN, b_spec=in1_rhs_t, name="dh1")
    dx1, d_ng1 = _rmsnorm_bwd(dh1, x1, norm_g[1] + token[0, 0], dx2, name="norm1_bwd")

    d_wout0 = _matmul(y0, dx1, ta=True, name="d_wout0", **slot_rows(INNER // N_CHIPS))
    dy0 = _matmul(dx1, w_out, tb=True, n=INNER, tn=MM_TN, b_spec=out_rhs_t(0), name="dy0")
    dcat0, dz0 = _gate_bwd(dy0, mix0, xa0, proj0, P0_Z, name="gate0_bwd")
    dqv, dkv_h, dvc, dg_cols, db_cols, d_outg = _dn_core_bwd(
        dcat0, qn, kn, vc, g_rows, b_rows, out_g, states, tinv, name="dn_core_bwd")
    dpq, dwq = _dn_pre_bwd(dqv, proj0, conv_w, 0, DN_QK_WIDTH, l2=True, scale=q_scale, name="dn_pre_q_bwd")
    dpk, dwk = _dn_pre_bwd(dkv_h, proj0, conv_w, DN_QK_WIDTH, DN_QK_WIDTH, l2=True, scale=1.0, name="dn_pre_k_bwd")
    dpv, dwv = _dn_pre_bwd(dvc, proj0, conv_w, 2 * DN_QK_WIDTH, MIX_WIDTH, l2=False, scale=1.0, name="dn_pre_v_bwd")
    dab, d_alog, d_dt = _dn_ab_bwd(_cols_to_nat(dg_cols, nc), _cols_to_nat(db_cols, nc), proj0, al, dtb, name="dn_ab_bwd")
    dxq0, dkv0, d_xaq0, d_xak0 = _xa_bwd(dcat0, proj0, P0_XQ, kv[0], xa_q_g[0], xa_k_g[0], name="xa0_bwd")
    dproj0 = [dpq, dpk, dpv, dxq0, dz0, dab]
    d_win0 = _matmul(h0, dproj0, ta=True, out_dtype=BF16, name="d_win0")
    dh0 = _matmul(dproj0, w_in0, tb=True, tk=MM_TN, name="dh0")
    dx0, d_ng0 = _rmsnorm_bwd(dh0, x, norm_g[0], dx1, name="norm0_bwd")

    d_wkv0 = _matmul(mem_n, dkv0, ta=True, name="d_wkv0", **slot_rows(D_MODEL // N_CHIPS))
    dmem0 = _matmul(dkv0, w_kv, tb=True, n=D_MODEL, tn=D_MODEL // N_CHIPS, b_spec=kv_rhs_t(0), name="dmem0")
    dmem_n = _matmul(dkv1, w_kv, tb=True, n=D_MODEL, tn=D_MODEL // N_CHIPS, b_spec=kv_rhs_t(1), res=dmem0, name="dmem1")
    _, d_memg = _rmsnorm_bwd(dmem_n, mem, mem_norm_g, None, name="mem_norm_bwd")

    grads = dict(
        norm_g=jnp.concatenate([d_ng0, d_ng1], axis=0), mem_norm_g=d_memg.reshape(-1), mem_w_kv0=d_wkv0,
        xa_q_norm_g=jnp.concatenate([d_xaq0, d_xaq1], axis=0), xa_k_norm_g=jnp.concatenate([d_xak0, d_xak1], axis=0),
        w_out0=d_wout0, dn_w_in=d_win0, dn_conv_w=jnp.concatenate([dwq, dwk, dwv], axis=1),
        dn_a_log=d_alog[:, :DN_V_HEADS], dn_dt_bias=d_dt[:, :DN_V_HEADS], dn_out_norm_g=d_outg,
        sb_q_norm_g=d_sbq, sb_k_norm_g=d_sbk)
    return loss_vec, dx0, grads


ANY = pl.BlockSpec(memory_space=pl.ANY)


def _place():
    x, y, c = lax.axis_index("x"), lax.axis_index("y"), lax.axis_index("c")
    chips = [(1 - x, y), (x, 1 - y), (1 - x, 1 - y)]
    return x, y, c, 2 * x + y, (x, y, 1 - c), chips


def _rcopy(src, dst, send, recv, i, dev):
    return pltpu.make_async_remote_copy(src_ref=src, dst_ref=dst, send_sem=send.at[i], recv_sem=recv.at[i],
                                        device_id=dev, device_id_type=MESH)


def _gather_weights(srcs, *, name):
    nt = len(srcs)

    def body(*refs):
        src, dst = refs[:nt], refs[nt:2 * nt]
        send, recv = refs[2 * nt:]
        x, y, c, j, sib, chips = _place()
        sends = []
        for t in range(nt):
            for k, (cx, cy) in enumerate(chips):
                sends.append(_rcopy(src[t].at[c], dst[t].at[j, c], send, recv, 6 * t + k, (cx, cy, c)))
                sends[-1].start()
        for t in range(nt):
            for k, (cx, cy) in enumerate(chips):
                landed = dst[t].at[2 * cx + cy, c]
                _rcopy(landed, landed, send, recv, 6 * t + k, (cx, cy, c)).wait_recv()
                sends.append(_rcopy(landed, landed, send, recv, 6 * t + 3 + k, sib))
                sends[-1].start()
        for t in range(nt):
            for k, (cx, cy) in enumerate(chips):
                other = dst[t].at[2 * cx + cy, 1 - c]
                _rcopy(other, other, send, recv, 6 * t + 3 + k, sib).wait_recv()
        for cp in sends:
            cp.wait_send()

    return pl.pallas_call(
        body, in_specs=[ANY] * nt, out_specs=[ANY] * nt,
        out_shape=[jax.ShapeDtypeStruct((N_CHIPS,) + s.shape, s.dtype) for s in srcs],
        scratch_shapes=[pltpu.SemaphoreType.DMA((6 * nt,)), pltpu.SemaphoreType.DMA((6 * nt,))],
        name=name)(*srcs)


def _swap_halves(xs, *, name):
    nt = len(xs)

    def body(*refs):
        src, dst = refs[:nt], refs[nt:2 * nt]
        send, recv = refs[2 * nt:]
        x, y, c, j, sib, chips = _place()
        cps = []
        for t in range(nt):
            for s in range(N_CHIPS):
                cps.append(_rcopy(src[t].at[s, 1 - c], dst[t].at[s], send, recv, 4 * t + s, sib))
                cps[-1].start()
        for cp in cps:
            cp.wait_recv()
        for cp in cps:
            cp.wait_send()

    return pl.pallas_call(
        body, in_specs=[ANY] * nt, out_specs=[ANY] * nt,
        out_shape=[jax.ShapeDtypeStruct((N_CHIPS,) + a.shape[2:], a.dtype) for a in xs],
        scratch_shapes=[pltpu.SemaphoreType.DMA((4 * nt,)), pltpu.SemaphoreType.DMA((4 * nt,))], name=name)(*xs)


def _scatter_to_chips(ps, *, name):
    nt = len(ps)

    def body(*refs):
        src, dst = refs[:nt], refs[nt:2 * nt]
        send, recv = refs[2 * nt:]
        x, y, c, j, sib, chips = _place()
        cps = []
        for t in range(nt):
            for k, (cx, cy) in enumerate(chips):
                cps.append(_rcopy(src[t].at[2 * cx + cy], dst[t].at[k], send, recv, 3 * t + k, (cx, cy, c)))
                cps[-1].start()
        for cp in cps:
            cp.wait_recv()
        for cp in cps:
            cp.wait_send()

    return pl.pallas_call(
        body, in_specs=[ANY] * nt, out_specs=[ANY] * nt,
        out_shape=[jax.ShapeDtypeStruct((3,) + a.shape[1:], a.dtype) for a in ps],
        scratch_shapes=[pltpu.SemaphoreType.DMA((3 * nt,)), pltpu.SemaphoreType.DMA((3 * nt,))], name=name)(*ps)


def _swap_with_sibling(fs, *, name):
    nt = len(fs)

    def body(*refs):
        src, dst = refs[:nt], refs[nt:2 * nt]
        send, recv = refs[2 * nt:]
        x, y, c, j, sib, chips = _place()
        cps = [_rcopy(src[t], dst[t], send, recv, t, sib) for t in range(nt)]
        for cp in cps:
            cp.start()
        for cp in cps:
            cp.wait_recv()
        for cp in cps:
            cp.wait_send()

    return pl.pallas_call(
        body, in_specs=[ANY] * nt, out_specs=[ANY] * nt,
        out_shape=[jax.ShapeDtypeStruct(a.shape, a.dtype) for a in fs],
        scratch_shapes=[pltpu.SemaphoreType.DMA((nt,)), pltpu.SemaphoreType.DMA((nt,))], name=name)(*fs)


HBM_SPEC = pl.BlockSpec(memory_space=pltpu.HBM)
SEM_SPEC = pl.BlockSpec(memory_space=pltpu.SEMAPHORE)
SIDE_EFFECT = pltpu.SideEffectType.DATAFLOW_SIDE_EFFECTING


def _gather_plan(src, land):
    x, y, c, j, sib, chips = _place()
    return [(src[t].at[c], land[t].at[j, c], (cx, cy, c), land[t].at[2 * cx + cy, c])
            for t in range(len(src)) for cx, cy in chips]


def _scatter_plan(src, land):
    x, y, c, j, sib, chips = _place()
    return [(src[t].at[2 * cx + cy], land[t].at[k], (cx, cy, c), land[t].at[k])
            for t in range(len(src)) for k, (cx, cy) in enumerate(chips)]


def _exchange_start(srcs, lands, plan, *, name):
    ns, nb = len(srcs), len(srcs) + len(lands)
    n = 3 * ns

    def body(*refs):
        send, recv, token = refs[nb], refs[nb + 1], refs[-1]
        for i, (s, d, dev, _) in enumerate(plan(refs[:ns], refs[ns:nb])):
            _rcopy(s, d, send, recv, i, dev).start()
        token[...] = jnp.zeros_like(token)

    bufs = list(srcs) + list(lands)
    outs = pl.pallas_call(
        body, name=name,
        out_shape=(pltpu.SemaphoreType.DMA((n,)), pltpu.SemaphoreType.DMA((n,)), *[pltpu.HBM(a.shape, a.dtype) for a in bufs],
                   jax.ShapeDtypeStruct((8, LANE), F32)),
        in_specs=[HBM_SPEC] * nb, out_specs=(SEM_SPEC, SEM_SPEC, *[HBM_SPEC] * nb, pl.BlockSpec(memory_space=pltpu.VMEM)),
        input_output_aliases={i: 2 + i for i in range(nb)},
        compiler_params=pltpu.CompilerParams(has_side_effects=SIDE_EFFECT))(
            *[pltpu.with_memory_space_constraint(a, pltpu.HBM) for a in bufs])
    return outs[0], outs[1], list(outs[2:2 + ns]), list(outs[2 + ns:2 + nb]), outs[-1]


def _exchange_wait(srcs, lands, send, recv, after, plan, *, name):
    ns, nb = len(srcs), len(srcs) + len(lands)

    def body(*refs):
        send_s, recv_s = refs[nb], refs[nb + 1]
        for i, (s, d, dev, inc) in enumerate(plan(refs[:ns], refs[ns:nb])):
            _rcopy(s, d, send_s, recv_s, i, dev).wait_send()
            _rcopy(inc, inc, send_s, recv_s, i, dev).wait_recv()

    bufs = list(srcs) + list(lands)
    outs = pl.pallas_call(
        body, name=name, out_shape=tuple(pltpu.HBM(a.shape, a.dtype) for a in bufs),
        in_specs=[HBM_SPEC] * nb + [SEM_SPEC, SEM_SPEC, ANY], out_specs=tuple([HBM_SPEC] * nb),
        input_output_aliases={i: i for i in range(nb)},
        compiler_params=pltpu.CompilerParams(has_side_effects=SIDE_EFFECT))(*bufs, send, recv, after)
    return list(outs[:ns]), list(outs[ns:])


def _forward_halves(lands, *, name):
    nt = len(lands)

    def body(*refs):
        src, dst = refs[:nt], refs[nt:2 * nt]
        send, recv = refs[2 * nt:]
        x, y, c, j, sib, chips = _place()
        cps = []
        for t in range(nt):
            for k, (cx, cy) in enumerate(chips):
                cps.append(_rcopy(src[t].at[2 * cx + cy, c], dst[t].at[2 * cx + cy, c], send, recv, 3 * t + k, sib))
                cps[-1].start()
        for t in range(nt):
            for k, (cx, cy) in enumerate(chips):
                other = dst[t].at[2 * cx + cy, 1 - c]
                _rcopy(other, other, send, recv, 3 * t + k, sib).wait_recv()
        for cp in cps:
            cp.wait_send()

    return pl.pallas_call(
        body, in_specs=[ANY] * nt, out_specs=[ANY] * nt, out_shape=[jax.ShapeDtypeStruct(a.shape, a.dtype) for a in lands],
        input_output_aliases={t: t for t in range(nt)},
        scratch_shapes=[pltpu.SemaphoreType.DMA((3 * nt,)), pltpu.SemaphoreType.DMA((3 * nt,))], name=name)(*lands)


def _all_reduce_small(parts, *, name):
    n = len(parts)
    offs, rows = [], 0
    for p in parts:
        offs.append(rows)
        rows += -(-p.shape[0] // 8) * 8

    def body(*refs):
        p_refs, o_refs = refs[:n], refs[n:2 * n]
        buf, send, recv = refs[2 * n:]
        x, y, c = lax.axis_index("x"), lax.axis_index("y"), lax.axis_index("c")
        me = 4 * x + 2 * y + c
        buf[me] = jnp.zeros((rows, LANE), F32)
        for p_ref, off in zip(p_refs, offs):
            buf[me, off:off + p_ref.shape[0], :] = p_ref[...]
        cps = []
        for r in range(1, 8):
            dev = (x ^ (r >> 2), y ^ ((r >> 1) & 1), c ^ (r & 1))
            cps.append(_rcopy(buf.at[me], buf.at[me], send, recv, r - 1, dev))
            cps[-1].start()
        for r in range(1, 8):
            frm = buf.at[me ^ r]
            _rcopy(frm, frm, send, recv, r - 1, (x, y, c)).wait_recv()
        for cp in cps:
            cp.wait_send()
        acc = buf[0]
        for d in range(1, 8):
            acc = acc + buf[d]
        for o_ref, off in zip(o_refs, offs):
            o_ref[...] = acc[off:off + o_ref.shape[0], :]

    vm = pl.BlockSpec(memory_space=pltpu.VMEM)
    return pl.pallas_call(
        body, in_specs=[vm] * n, out_specs=[vm] * n, out_shape=[jax.ShapeDtypeStruct(p.shape, F32) for p in parts],
        scratch_shapes=[pltpu.VMEM((8, rows, LANE), F32), pltpu.SemaphoreType.DMA((7,)), pltpu.SemaphoreType.DMA((7,))],
        name=name)(*parts)


def _add_halves(x, b, c_idx, *, name, tr=256):
    _, _, R, C = x.shape
    tr = min(tr, R)

    def body(c_ref, x_ref, b_ref, o_ref):
        o_ref[...] = (x_ref[...].astype(F32) + b_ref[...].astype(F32)).astype(o_ref.dtype)

    return pl.pallas_call(
        body,
        grid_spec=pltpu.PrefetchScalarGridSpec(
            num_scalar_prefetch=1, grid=(N_CHIPS, R // tr),
            in_specs=[pl.BlockSpec((None, None, tr, C), lambda s, i, c_ref: (s, c_ref[0], i, 0)),
                      pl.BlockSpec((None, tr, C), lambda s, i, c_ref: (s, i, 0))],
            out_specs=pl.BlockSpec((None, tr, C), lambda s, i, c_ref: (s, i, 0))),
        out_shape=jax.ShapeDtypeStruct(b.shape, b.dtype), compiler_params=_cp(("parallel", "parallel")), name=name)(c_idx, x, b)


def _sum_slot(p, rcv, j_idx, *, name, tr=256):
    _, R, C = p.shape
    tr = min(tr, R)

    def body(j_ref, p_ref, r_ref, o_ref):
        acc = p_ref[...].astype(F32)
        for k in range(3):
            acc = acc + r_ref[k].astype(F32)
        o_ref[...] = acc

    return pl.pallas_call(
        body,
        grid_spec=pltpu.PrefetchScalarGridSpec(
            num_scalar_prefetch=1, grid=(R // tr,),
            in_specs=[pl.BlockSpec((None, tr, C), lambda i, j_ref: (j_ref[0], i, 0)),
                      pl.BlockSpec((3, tr, C), lambda i, j_ref: (0, i, 0))],
            out_specs=pl.BlockSpec((tr, C), lambda i, j_ref: (i, 0))),
        out_shape=jax.ShapeDtypeStruct((R, C), F32), compiler_params=_cp(("parallel",)), name=name)(j_idx, p, rcv)


def _adamw_math(w, g, m, v):
    nm = ADAM_B1 * m + (1.0 - ADAM_B1) * g
    nv = ADAM_B2 * v + (1.0 - ADAM_B2) * (g * g)
    m_hat = nm / (1.0 - ADAM_B1 ** ADAM_STEP)
    v_hat = nv / (1.0 - ADAM_B2 ** ADAM_STEP)
    return -ADAM_LR * (m_hat / (jnp.sqrt(v_hat) + ADAM_EPS) + ADAM_WD * w), nm, nv


def _adamw_halves(w, g_mine, g_theirs, m, v, c_idx, *, name, tr=128):
    _, R, C = w.shape
    tr = tr if R % tr == 0 else R

    def body(c_ref, w_ref, gm_ref, gt_ref, m_ref, v_ref, g_ref, d_ref, nm_ref, nv_ref):
        gv = jnp.where(pl.program_id(0) == c_ref[0], gm_ref[...], gt_ref[...])
        d, nm, nv = _adamw_math(w_ref[...], gv, m_ref[...], v_ref[...])
        g_ref[...] = gv
        d_ref[...] = d
        nm_ref[...] = nm
        nv_ref[...] = nv

    full = pl.BlockSpec((None, tr, C), lambda hh, i, c_ref: (hh, i, 0))
    half = pl.BlockSpec((tr, C), lambda hh, i, c_ref: (i, 0))
    sh = jax.ShapeDtypeStruct(w.shape, F32)
    return pl.pallas_call(
        body,
        grid_spec=pltpu.PrefetchScalarGridSpec(num_scalar_prefetch=1, grid=(2, R // tr),
                                               in_specs=[full, half, half, full, full], out_specs=(full,) * 4),
        out_shape=(sh,) * 4, compiler_params=_cp(("parallel", "parallel")), name=name)(c_idx, w, g_mine, g_theirs, m, v)


def _adamw_parts(ws, gs, ms, vs, *, name):
    n = len(ws)

    def body(*refs):
        ins, outs = refs[:4 * n], refs[4 * n:]
        for t in range(n):
            d, nm, nv = _adamw_math(ins[t][...], ins[n + t][...], ins[2 * n + t][...], ins[3 * n + t][...])
            outs[t][...] = d
            outs[n + t][...] = nm
            outs[2 * n + t][...] = nv

    vm = pl.BlockSpec(memory_space=pltpu.VMEM)
    shapes = [jax.ShapeDtypeStruct(w.shape, F32) for w in ws] * 3
    outs = pl.pallas_call(body, in_specs=[vm] * (4 * n), out_specs=[vm] * (3 * n), out_shape=shapes, name=name)(
        *ws, *gs, *ms, *vs)
    return outs[:n], outs[n:2 * n], outs[2 * n:]


_SMALL = ["norm_g", "mem_norm_g", "xa_q_norm_g", "xa_k_norm_g", "dn_a_log", "dn_dt_bias", "dn_out_norm_g",
          "sb_q_norm_g", "sb_k_norm_g"]


def _rows128(a):
    flat = a.reshape(-1)
    pad = -flat.shape[0] % LANE
    if pad:
        flat = jnp.pad(flat, (0, pad))
    return flat.reshape(-1, LANE)


def _unrows(r, shape):
    return r.reshape(-1)[:math.prod(shape)].reshape(shape)


def kernel(x, mem, norm_g, mem_norm_g, mem_w_kv, xa_q_norm_g, xa_k_norm_g, w_out, dn_w_in, dn_conv_w, dn_a_log, dn_dt_bias, dn_out_norm_g, sb_w_in, sb_q_norm_g, sb_k_norm_g, loss_target, m_norm_g, m_mem_norm_g, m_mem_w_kv, m_xa_q_norm_g, m_xa_k_norm_g, m_w_out, m_dn_w_in, m_dn_conv_w, m_dn_a_log, m_dn_dt_bias, m_dn_out_norm_g, m_sb_w_in, m_sb_q_norm_g, m_sb_k_norm_g, v_norm_g, v_mem_norm_g, v_mem_w_kv, v_xa_q_norm_g, v_xa_k_norm_g, v_w_out, v_dn_w_in, v_dn_conv_w, v_dn_a_log, v_dn_dt_bias, v_dn_out_norm_g, v_sb_w_in, v_sb_q_norm_g, v_sb_k_norm_g):
    W = dict(norm_g=norm_g, mem_norm_g=mem_norm_g, mem_w_kv=mem_w_kv, xa_q_norm_g=xa_q_norm_g, xa_k_norm_g=xa_k_norm_g,
             w_out=w_out, dn_w_in=dn_w_in, dn_conv_w=dn_conv_w, dn_a_log=dn_a_log, dn_dt_bias=dn_dt_bias,
             dn_out_norm_g=dn_out_norm_g, sb_w_in=sb_w_in, sb_q_norm_g=sb_q_norm_g, sb_k_norm_g=sb_k_norm_g)
    M = dict(norm_g=m_norm_g, mem_norm_g=m_mem_norm_g, mem_w_kv=m_mem_w_kv, xa_q_norm_g=m_xa_q_norm_g,
             xa_k_norm_g=m_xa_k_norm_g, w_out=m_w_out, dn_w_in=m_dn_w_in, dn_conv_w=m_dn_conv_w, dn_a_log=m_dn_a_log,
             dn_dt_bias=m_dn_dt_bias, dn_out_norm_g=m_dn_out_norm_g, sb_w_in=m_sb_w_in, sb_q_norm_g=m_sb_q_norm_g,
             sb_k_norm_g=m_sb_k_norm_g)
    V = dict(norm_g=v_norm_g, mem_norm_g=v_mem_norm_g, mem_w_kv=v_mem_w_kv, xa_q_norm_g=v_xa_q_norm_g,
             xa_k_norm_g=v_xa_k_norm_g, w_out=v_w_out, dn_w_in=v_dn_w_in, dn_conv_w=v_dn_conv_w, dn_a_log=v_dn_a_log,
             dn_dt_bias=v_dn_dt_bias, dn_out_norm_g=v_dn_out_norm_g, sb_w_in=v_sb_w_in, sb_q_norm_g=v_sb_q_norm_g,
             sb_k_norm_g=v_sb_k_norm_g)
    names = ["norm_g", "mem_norm_g", "mem_w_kv", "xa_q_norm_g", "xa_k_norm_g", "w_out", "dn_w_in", "dn_conv_w",
             "dn_a_log", "dn_dt_bias", "dn_out_norm_g", "sb_w_in", "sb_q_norm_g", "sb_k_norm_g"]
    cx, cy, cc = lax.axis_index("x"), lax.axis_index("y"), lax.axis_index("c")
    slot = 2 * cx + cy
    half_r = D_MODEL // 2
    conv_cols = dn_conv_w.shape[2]

    w0s = jnp.pad(dn_w_in[0].astype(BF16), ((0, 0), (0, P0_SHARD_PAD - P0_SHARD))).reshape(2, half_r, P0_SHARD_PAD)
    w1s = sb_w_in[0].astype(BF16).reshape(2, half_r, SB_PROJ // N_CHIPS)
    convs = jnp.pad(dn_conv_w[0], ((0, 8 - DN_CONV), (0, 0))).reshape(8, 2, conv_cols // 2).transpose(1, 0, 2)
    c_idx = jnp.reshape(cc, (1,)).astype(jnp.int32)
    j_idx = jnp.reshape(slot, (1,)).astype(jnp.int32)
    own_a = [w0s, convs]
    g0, gconv = [lax.dynamic_update_slice(g, o[None], (slot, 0, 0, 0))
                 for o, g in zip(own_a, _gather_weights(own_a, name="gather_weights"))]
    own_b = [w1s, w_out.astype(BF16), mem_w_kv.astype(BF16)]
    lands_b = [lax.dynamic_update_slice(lax.empty((N_CHIPS,) + o.shape, o.dtype), o[None], (slot, 0, 0, 0)) for o in own_b]
    send_b, recv_b, own_b, lands_b, token_b = _exchange_start(own_b, lands_b, _gather_plan, name="gather_late_start")

    def late_weights(after):
        _, lands = _exchange_wait(own_b, lands_b, send_b, recv_b, after, _gather_plan, name="gather_late_wait")
        g1, gout, gkv = _forward_halves(lands, name="gather_late_forward")
        return gkv, gout, g1

    rs1 = {}

    def early_grads(d_win1, d_wout1, d_wkv1):
        xs1 = [d_win1, d_wout1.reshape(N_CHIPS, 2, -1, D_MODEL), d_wkv1.reshape(N_CHIPS, 2, -1, 2 * XA_WIDTH)]
        from_sib = _swap_halves(xs1, name="rs1_swap")
        ps = [_add_halves(a, b, c_idx, name=f"rs1_add{t}") for t, (a, b) in enumerate(zip(xs1, from_sib))]
        rcv = [lax.empty((3,) + p.shape[1:], p.dtype) for p in ps]
        rs1["send"], rs1["recv"], rs1["ps"], rs1["rcv"], token = _exchange_start(ps, rcv, _scatter_plan, name="rs1_scatter_start")
        return token

    shards0 = g0.reshape(N_CHIPS, D_MODEL, P0_SHARD_PAD)
    z = lambda n: jnp.zeros((D_MODEL, n), BF16)
    w_in0 = jnp.concatenate(
        [shards0[s][:, lo:hi] for s, lo, hi in _true_pieces(0, _C_QKV) + _true_pieces(_C_QKV + 2 * DN_V_HEADS, DN_PROJ)]
        + [shards0[s][:, lo:hi] for s, lo, hi in _true_pieces(_C_QKV, _C_QKV + DN_V_HEADS)] + [z(LANE - DN_V_HEADS)]
        + [shards0[s][:, lo:hi] for s, lo, hi in _true_pieces(_C_QKV + DN_V_HEADS, _C_QKV + 2 * DN_V_HEADS)]
        + [z(P0 - P0_AB - LANE - DN_V_HEADS)], axis=1)
    conv_f = gconv.transpose(2, 0, 1, 3).reshape(8, N_CHIPS * conv_cols)[:DN_CONV]

    loss_vec, grad_x, g = _local_step(
        x[0], mem[0], loss_target[0], norm_g + token_b[0, 0], mem_norm_g, xa_q_norm_g, xa_k_norm_g, w_in0, conv_f,
        dn_a_log[0], dn_dt_bias[0], dn_out_norm_g[0], sb_q_norm_g[0], sb_k_norm_g[0], late_weights, early_grads)

    gp = g["dn_w_in"]
    d0 = jnp.stack([jnp.pad(jnp.concatenate([gp[:, lo:hi] for lo, hi in _padded_pieces(s * P0_SHARD, (s + 1) * P0_SHARD)],
                                            axis=1), ((0, 0), (0, P0_SHARD_PAD - P0_SHARD))) for s in range(N_CHIPS)])
    xs0 = [d0.reshape(N_CHIPS, 2, half_r, P0_SHARD_PAD), g["w_out0"].reshape(N_CHIPS, 2, -1, D_MODEL),
           g["mem_w_kv0"].reshape(N_CHIPS, 2, -1, 2 * XA_WIDTH)]
    from_sib = _swap_halves(xs0, name="rs0_swap")
    ps0 = [_add_halves(a, b, c_idx, name=f"rs0_add{t}") for t, (a, b) in enumerate(zip(xs0, from_sib))]
    rcvs0 = _scatter_to_chips(ps0, name="rs0_scatter")
    fs0 = [_sum_slot(p, r, j_idx, name=f"rs0_sum{t}") for t, (p, r) in enumerate(zip(ps0, rcvs0))]
    fs0[0] = fs0[0][:, :P0_SHARD]
    ps1, rcvs1 = _exchange_wait(rs1["ps"], rs1["rcv"], rs1["send"], rs1["recv"], grad_x, _scatter_plan, name="rs1_scatter_wait")
    fs1 = [_sum_slot(p, r, j_idx, name=f"rs1_sum{t}") for t, (p, r) in enumerate(zip(ps1, rcvs1))]
    mine = fs0 + fs1
    theirs = _swap_with_sibling(mine, name="rs_join")
    big = [("dn_w_in", None), ("w_out", 0), ("mem_w_kv", 0), ("sb_w_in", None), ("w_out", 1), ("mem_w_kv", 1)]

    red = _all_reduce_small([_rows128(g[n]) for n in _SMALL] + [_rows128(g["dn_conv_w"]), loss_vec], name="all_reduce_small")
    small_rows = dict(zip(_SMALL, red))
    conv_full = red[len(_SMALL)].reshape(DN_CONV, N_CHIPS * conv_cols)
    small_rows["dn_conv_w"] = _rows128(lax.dynamic_slice_in_dim(conv_full, slot * conv_cols, conv_cols, axis=1))
    loss = red[-1][0, 0]

    out_g, out_d, out_m, out_v = {}, {}, {}, {}
    per_layer = {}
    for (n, layer), mine_g, their_g in zip(big, mine, theirs):
        pick = (lambda a: a) if layer is None else (lambda a: a[layer])
        h3 = (2,) + mine_g.shape
        outs = _adamw_halves(pick(W[n]).reshape(h3), mine_g, their_g, pick(M[n]).reshape(h3), pick(V[n]).reshape(h3), c_idx,
                             name=f"adamw_{n}" + ("" if layer is None else str(layer)))
        if layer is None:
            out_g[n], out_d[n], out_m[n], out_v[n] = [o.reshape(W[n].shape) for o in outs]
        else:
            per_layer[(n, layer)] = [o.reshape(W[n].shape[1:]) for o in outs]
    for n in ("w_out", "mem_w_kv"):
        out_g[n], out_d[n], out_m[n], out_v[n] = [jnp.stack([per_layer[(n, 0)][i], per_layer[(n, 1)][i]]) for i in range(4)]
    small_names = _SMALL + ["dn_conv_w"]
    ds, nms, nvs = _adamw_parts([_rows128(W[n]) for n in small_names], [small_rows[n] for n in small_names],
                                [_rows128(M[n]) for n in small_names], [_rows128(V[n]) for n in small_names], name="adamw_small")
    for n, d, nm, nv in zip(small_names, ds, nms, nvs):
        shp = W[n].shape
        out_g[n], out_d[n], out_m[n], out_v[n] = [_unrows(r, shp) for r in (small_rows[n], d, nm, nv)]

    return (loss, grad_x[None], *[out_g[n] for n in names], *[out_d[n] for n in names], *[out_m[n] for n in names],
            *[out_v[n] for n in names])
```

```python
import functools
import math

import jax
import jax.numpy as jnp
from jax import lax
from jax.experimental import pallas as pl
from jax.experimental.pallas import tpu as pltpu

F32 = jnp.float32
BF16 = jnp.bfloat16
HI = lax.Precision.HIGHEST
MESH = pl.DeviceIdType.MESH

D_MODEL = 2048
INNER = 4096
XA_WIDTH = 1024
XA_HEADS = 4
XA_DIM = 256
MIX_WIDTH = 3072
HEAD_DIM = 128
DN_V_HEADS = 24
DN_QK_WIDTH = 1536
DN_CONV = 4
DN_PROJ = 11312
SB_PROJ = 14336
EPS = 1e-6
N_CHIPS = 4

CH = 128
LANE = 128

P0_XQ = 6144
P0_Z = 7168
P0_AB = 11264
P0 = 11776
P0_SHARD = DN_PROJ // N_CHIPS
P0_SHARD_PAD = 2944
P1_XQ = 9216
P1_Z = 10240
P1 = SB_PROJ

ADAM_LR = 0.001
ADAM_B1 = 0.9
ADAM_B2 = 0.999
ADAM_EPS = 1e-08
ADAM_WD = 0.01
ADAM_STEP = 10

VMEM_LIMIT = 48 * 1024 * 1024


def _cp(sem=None, **kw):
    return pltpu.CompilerParams(dimension_semantics=sem, vmem_limit_bytes=VMEM_LIMIT, **kw)


def _bdot(a, b, dims):
    return lax.dot_general(a.astype(BF16), b.astype(BF16), (dims, ((), ())), preferred_element_type=F32)


def _fdot(a, b, dims):
    return lax.dot_general(a, b, (dims, ((), ())), precision=HI, preferred_element_type=F32)


NN = ((1,), (0,))
NT = ((1,), (1,))
TN = ((0,), (0,))


def _sigmoid(x):
    return 1.0 / (1.0 + jnp.exp(-x))


def _softplus(x):
    return jnp.maximum(x, 0.0) + jnp.log(1.0 + jnp.exp(-jnp.abs(x)))


def _iota2(shape, axis):
    return lax.broadcasted_iota(jnp.int32, shape, axis)


MM_FULL_K = 4096
MM_BLOCK_BYTES = 4 * 1024 * 1024


def _matmul(a, b, *, ta=False, tb=False, out_dtype=F32, res=None, name, n=None, tm=None, tn=None, tk=None,
            b_spec=None, o_spec=None, o_shape=None):
    a_segs = list(a) if isinstance(a, (list, tuple)) else [a]
    b_segs = list(b) if isinstance(b, (list, tuple)) else [b]
    a0, b0 = a_segs[0], b_segs[0]
    M = a0.shape[1] if ta else a0.shape[0]
    K = a0.shape[0] if ta else sum(s.shape[1] for s in a_segs)
    if n is None:
        n = b0.shape[0] if tb else sum(s.shape[1] for s in b_segs)
    N = n
    dims = ((0,) if ta else (1,), (1,) if tb else (0,))
    has_res = res is not None
    flat = lambda v: v.reshape(-1, v.shape[-1])
    o_shape = o_shape or jax.ShapeDtypeStruct((M, N), out_dtype)

    def seg_specs(segs, tile, block, pos):
        specs, ranges, off = [], [], 0
        for s in segs:
            cnt = s.shape[1] // tile
            assert s.shape[1] % tile == 0, (name, s.shape, tile)

            def imap(*g, off=off, cnt=cnt):
                t = jnp.clip(g[pos] - off, 0, cnt - 1)
                return (g[0], t) if pos == 2 else (0, t)

            specs.append(pl.BlockSpec(block, imap))
            ranges.append((off, off + cnt))
            off += cnt
        return specs, ranges

    if K <= MM_FULL_K:
        assert len(a_segs) == 1
        tm = tm or min(M, 1024, max(256, MM_BLOCK_BYTES // (K * a0.dtype.itemsize)))
        tn = tn or min(N, 512)
        assert M % tm == 0 and N % tn == 0, (name, M, N, K, tm, tn)
        nb = len(b_segs)
        if b_spec is not None:
            b_specs, b_ranges = [b_spec], [(0, N // tn)]
        elif nb > 1:
            assert not tb
            b_specs, b_ranges = seg_specs(b_segs, tn, (K, tn), 1)
        else:
            b_specs = [pl.BlockSpec((tn, K), lambda i, j: (j, 0)) if tb else pl.BlockSpec((K, tn), lambda i, j: (0, j))]
            b_ranges = [(0, N // tn)]

        def body_full(*refs):
            a_ref, b_refs = refs[0], refs[1:1 + nb]
            r_ref = refs[1 + nb] if has_res else None
            o_ref = refs[-1]
            j = pl.program_id(1)
            for b_ref, (lo, hi) in zip(b_refs, b_ranges):
                def emit(b_ref=b_ref):
                    r = _bdot(a_ref[...], flat(b_ref[...]), dims)
                    if has_res:
                        r = r + r_ref[...]
                    o_ref[...] = r.astype(o_ref.dtype).reshape(o_ref.shape)
                if nb == 1:
                    emit()
                else:
                    pl.when(jnp.logical_and(j >= lo, j < hi))(emit)

        a_spec = pl.BlockSpec((K, tm), lambda i, j: (0, i)) if ta else pl.BlockSpec((tm, K), lambda i, j: (i, 0))
        o_spec = o_spec or pl.BlockSpec((tm, tn), lambda i, j: (i, j))
        r_spec = [pl.BlockSpec((tm, tn), lambda i, j: (i, j))] if has_res else []
        return pl.pallas_call(
            body_full, grid=(M // tm, N // tn), in_specs=[a_spec] + b_specs + r_spec, out_specs=o_spec, out_shape=o_shape,
            compiler_params=_cp(("parallel", "arbitrary")), name=name)(*([a0] + b_segs + ([res] if has_res else [])))

    assert tb and not ta and len(b_segs) == 1
    tm, tn = tm or min(M, 1024), tn or min(N, 1024)
    tk = tk or (1024 if all(s.shape[1] % 1024 == 0 for s in a_segs) else 512)
    assert M % tm == 0 and N % tn == 0 and K % tk == 0, (name, M, N, K, tm, tn, tk)
    nk = K // tk
    na = len(a_segs)
    if na > 1:
        a_specs, a_ranges = seg_specs(a_segs, tk, (tm, tk), 2)
    else:
        a_specs, a_ranges = [pl.BlockSpec((tm, tk), lambda i, j, k: (i, k))], [(0, nk)]
    b_spec = b_spec or pl.BlockSpec((tn, tk), lambda i, j, k: (j, k))

    def body(*refs):
        a_refs, b_ref = refs[:na], refs[na]
        r_ref = refs[na + 1] if has_res else None
        o_ref, acc = refs[-2], refs[-1]
        k = pl.program_id(2)

        @pl.when(k == 0)
        def _():
            acc[...] = jnp.zeros_like(acc)

        for a_ref, (lo, hi) in zip(a_refs, a_ranges):
            def emit(a_ref=a_ref):
                acc[...] += _bdot(a_ref[...], flat(b_ref[...]), dims)
            if na == 1:
                emit()
            else:
                pl.when(jnp.logical_and(k >= lo, k < hi))(emit)

        @pl.when(k == nk - 1)
        def _():
            r = acc[...]
            if has_res:
                r = r + r_ref[...]
            o_ref[...] = r.astype(o_ref.dtype).reshape(o_ref.shape)

    o_spec = o_spec or pl.BlockSpec((tm, tn), lambda i, j, k: (i, j))
    r_spec = [pl.BlockSpec((tm, tn), lambda i, j, k: (i, j))] if has_res else []
    return pl.pallas_call(
        body, grid=(M // tm, N // tn, nk), in_specs=a_specs + [b_spec] + r_spec, out_specs=o_spec, out_shape=o_shape,
        scratch_shapes=[pltpu.VMEM((tm, tn), F32)],
        compiler_params=_cp(("parallel", "parallel", "arbitrary")), name=name)(*(a_segs + [b0] + ([res] if has_res else [])))


def _rmsnorm_fwd(x, g, *, name, tm=256):
    S, Dm = x.shape
    tm = min(tm, S)

    def body(x_ref, g_ref, o_ref):
        xv = x_ref[...]
        r = lax.rsqrt(jnp.mean(xv * xv, axis=-1, keepdims=True) + EPS)
        o_ref[...] = (xv * r * g_ref[...]).astype(BF16)

    return pl.pallas_call(
        body, grid=(S // tm,), in_specs=[pl.BlockSpec((tm, Dm), lambda i: (i, 0)), pl.BlockSpec((1, Dm), lambda i: (0, 0))],
        out_specs=pl.BlockSpec((tm, Dm), lambda i: (i, 0)), out_shape=jax.ShapeDtypeStruct((S, Dm), BF16),
        compiler_params=_cp(("parallel",)), name=name)(x, g.reshape(1, Dm))


def _rmsnorm_bwd(dh, x, g, dres, *, name, tm=256):
    S, Dm = x.shape
    tm = min(tm, S)
    want_dx = dres is not None

    def body(*refs):
        if want_dx:
            dh_ref, x_ref, g_ref, dr_ref, dx_ref, dg_ref = refs
        else:
            dh_ref, x_ref, g_ref, dg_ref = refs
        i = pl.program_id(0)
        xv = x_ref[...]
        dhv = dh_ref[...]
        r = lax.rsqrt(jnp.mean(xv * xv, axis=-1, keepdims=True) + EPS)
        y = xv * r
        part = jnp.sum(dhv * y, axis=0, keepdims=True)

        @pl.when(i == 0)
        def _():
            dg_ref[...] = jnp.zeros_like(dg_ref)

        dg_ref[...] += part
        if want_dx:
            dy = dhv * g_ref[...]
            dx_ref[...] = dr_ref[...] + r * (dy - y * jnp.mean(dy * y, axis=-1, keepdims=True))

    row = pl.BlockSpec((tm, Dm), lambda i: (i, 0))
    vec = pl.BlockSpec((1, Dm), lambda i: (0, 0))
    if want_dx:
        dx, dg = pl.pallas_call(
            body, grid=(S // tm,), in_specs=[row, row, vec, row], out_specs=(row, vec),
            out_shape=(jax.ShapeDtypeStruct((S, Dm), F32), jax.ShapeDtypeStruct((1, Dm), F32)),
            compiler_params=_cp(("arbitrary",)), name=name)(dh, x, g.reshape(1, Dm), dres)
        return dx, dg
    dg = pl.pallas_call(
        body, grid=(S // tm,), in_specs=[row, row, vec], out_specs=vec,
        out_shape=jax.ShapeDtypeStruct((1, Dm), F32), compiler_params=_cp(("arbitrary",)), name=name)(dh, x, g.reshape(1, Dm))
    return None, dg


GATE_TN = XA_WIDTH
GATE_MIX_TILES = MIX_WIDTH // GATE_TN


def _gate_cat_specs(tm):
    return [pl.BlockSpec((tm, GATE_TN), lambda i, j: (i, jnp.minimum(j, GATE_MIX_TILES - 1))),
            pl.BlockSpec((tm, GATE_TN), lambda i, j: (i, 0))]


def _gate_fwd(mix, xa, proj, z_off, *, name, tm=256):
    S = mix.shape[0]
    tm = min(tm, S)
    zb = z_off // GATE_TN

    def body(m_ref, x_ref, z_ref, y_ref):
        z = z_ref[...]
        c = jnp.where(pl.program_id(1) < GATE_MIX_TILES, m_ref[...], x_ref[...])
        y_ref[...] = (c * z * _sigmoid(z)).astype(BF16)

    blk = pl.BlockSpec((tm, GATE_TN), lambda i, j: (i, j))
    return pl.pallas_call(
        body, grid=(S // tm, INNER // GATE_TN),
        in_specs=_gate_cat_specs(tm) + [pl.BlockSpec((tm, GATE_TN), lambda i, j: (i, zb + j))],
        out_specs=blk, out_shape=jax.ShapeDtypeStruct((S, INNER), BF16),
        compiler_params=_cp(("parallel", "arbitrary")), name=name)(mix, xa, proj)


def _gate_bwd(dy, mix, xa, proj, z_off, *, name, tm=256):
    S = mix.shape[0]
    tm = min(tm, S)
    zb = z_off // GATE_TN

    def body(dy_ref, m_ref, x_ref, z_ref, dc_ref, dz_ref):
        z = z_ref[...]
        sg = _sigmoid(z)
        d = dy_ref[...]
        c = jnp.where(pl.program_id(1) < GATE_MIX_TILES, m_ref[...], x_ref[...])
        dc_ref[...] = d * z * sg
        dz_ref[...] = (d * c * sg * (1.0 + z * (1.0 - sg))).astype(BF16)

    blk = pl.BlockSpec((tm, GATE_TN), lambda i, j: (i, j))
    return pl.pallas_call(
        body, grid=(S // tm, INNER // GATE_TN),
        in_specs=[blk] + _gate_cat_specs(tm) + [pl.BlockSpec((tm, GATE_TN), lambda i, j: (i, zb + j))], out_specs=(blk, blk),
        out_shape=(jax.ShapeDtypeStruct((S, INNER), F32), jax.ShapeDtypeStruct((S, INNER), BF16)),
        compiler_params=_cp(("parallel", "arbitrary")), name=name)(dy, mix, xa, proj)


def _loss_head(x, target, *, name, tm=256):
    S, Dm = x.shape
    tm = min(tm, S)

    nt = S // tm

    def body(x_ref, t_ref, dx_ref, l_ref, acc):
        i = pl.program_id(0)
        e = x_ref[...] - t_ref[...]
        dx_ref[...] = e * (1.0 / Dm)

        @pl.when(i == 0)
        def _():
            acc[...] = jnp.zeros_like(acc)

        acc[...] += jnp.sum(e * e, axis=0, keepdims=True) * (0.5 / Dm)

        @pl.when(i == nt - 1)
        def _():
            l_ref[...] = jnp.sum(acc[...], axis=1, keepdims=True) + jnp.zeros((1, LANE), F32)

    row = pl.BlockSpec((tm, Dm), lambda i: (i, 0))
    return pl.pallas_call(
        body, grid=(nt,), in_specs=[row, row], out_specs=(row, pl.BlockSpec((1, LANE), lambda i: (0, 0))),
        out_shape=(jax.ShapeDtypeStruct((S, Dm), F32), jax.ShapeDtypeStruct((1, LANE), F32)),
        scratch_shapes=[pltpu.VMEM((1, Dm), F32)],
        compiler_params=_cp(("arbitrary",)), name=name)(x, target)


def _xa_norm(v, g):
    r = lax.rsqrt(jnp.mean(v * v, axis=-1, keepdims=True) + EPS)
    return v * r, r


def _xa_fwd(proj, xq_off, kv, gq, gk, *, name, tm=512):
    S = proj.shape[0]
    tm = min(tm, S)
    qb = xq_off // XA_DIM
    n_mem = kv.shape[0]
    scale = XA_DIM ** -0.5

    def body(q_ref, k_ref, v_ref, gq_ref, gk_ref, o_ref):
        qh, _ = _xa_norm(q_ref[...], None)
        kh, _ = _xa_norm(k_ref[...], None)
        qn = qh * gq_ref[...]
        kn = kh * gk_ref[...]
        s = _bdot(qn, kn, NT) * scale
        s = s - jnp.max(s, axis=-1, keepdims=True)
        p = jnp.exp(s)
        p = p / jnp.sum(p, axis=-1, keepdims=True)
        o_ref[...] = _bdot(p, v_ref[...], NN)

    vec = pl.BlockSpec((1, XA_DIM), lambda h, i: (0, 0))
    return pl.pallas_call(
        body, grid=(XA_HEADS, S // tm),
        in_specs=[pl.BlockSpec((tm, XA_DIM), lambda h, i: (i, qb + h)),
                  pl.BlockSpec((n_mem, XA_DIM), lambda h, i: (0, h)),
                  pl.BlockSpec((n_mem, XA_DIM), lambda h, i: (0, XA_HEADS + h)), vec, vec],
        out_specs=pl.BlockSpec((tm, XA_DIM), lambda h, i: (i, h)),
        out_shape=jax.ShapeDtypeStruct((S, XA_WIDTH), F32),
        compiler_params=_cp(("parallel", "parallel")), name=name)(proj, kv, kv, gq.reshape(1, XA_DIM), gk.reshape(1, XA_DIM))


def _xa_bwd(dcat, proj, xq_off, kv, gq, gk, *, name, tm=512):
    S = proj.shape[0]
    tm = min(tm, S)
    nt = S // tm
    qb = xq_off // XA_DIM
    db = MIX_WIDTH // XA_DIM
    n_mem = kv.shape[0]
    scale = XA_DIM ** -0.5

    def body(d_ref, q_ref, k_ref, v_ref, gq_ref, gk_ref, dq_ref, dk_ref, dv_ref, dgq_ref, dgk_ref, dkn_acc):
        h = pl.program_id(0)
        i = pl.program_id(1)
        q = q_ref[...]
        k = k_ref[...]
        qh, rq = _xa_norm(q, None)
        kh, rk = _xa_norm(k, None)
        gqv = gq_ref[...]
        gkv = gk_ref[...]
        qn = qh * gqv
        kn = kh * gkv
        s = _bdot(qn, kn, NT) * scale
        s = s - jnp.max(s, axis=-1, keepdims=True)
        p = jnp.exp(s)
        p = p / jnp.sum(p, axis=-1, keepdims=True)
        d = d_ref[...]
        dp = _bdot(d, v_ref[...], NT)
        ds = p * (dp - jnp.sum(dp * p, axis=-1, keepdims=True)) * scale
        dqn = _bdot(ds, kn, NN)

        @pl.when(i == 0)
        def _():
            dkn_acc[...] = jnp.zeros_like(dkn_acc)
            dv_ref[...] = jnp.zeros_like(dv_ref)

        @pl.when(jnp.logical_and(i == 0, h == 0))
        def _():
            dgq_ref[...] = jnp.zeros_like(dgq_ref)
            dgk_ref[...] = jnp.zeros_like(dgk_ref)

        dkn_acc[...] += _bdot(ds, qn, TN)
        dv_ref[...] += _bdot(p, d, TN)
        dgq_ref[...] += jnp.sum(dqn * qh, axis=0, keepdims=True)
        dy = dqn * gqv
        dq_ref[...] = (rq * (dy - qh * jnp.mean(dy * qh, axis=-1, keepdims=True))).astype(BF16)

        @pl.when(i == nt - 1)
        def _():
            dkn = dkn_acc[...]
            dgk_ref[...] += jnp.sum(dkn * kh, axis=0, keepdims=True)
            dyk = dkn * gkv
            dk_ref[...] = rk * (dyk - kh * jnp.mean(dyk * kh, axis=-1, keepdims=True))

    vec = pl.BlockSpec((1, XA_DIM), lambda h, i: (0, 0))
    kblk = pl.BlockSpec((n_mem, XA_DIM), lambda h, i: (0, h))
    vblk = pl.BlockSpec((n_mem, XA_DIM), lambda h, i: (0, XA_HEADS + h))
    dq, dk, dv, dgq, dgk = pl.pallas_call(
        body, grid=(XA_HEADS, nt),
        in_specs=[pl.BlockSpec((tm, XA_DIM), lambda h, i: (i, db + h)),
                  pl.BlockSpec((tm, XA_DIM), lambda h, i: (i, qb + h)), kblk, vblk, vec, vec],
        out_specs=(pl.BlockSpec((tm, XA_DIM), lambda h, i: (i, h)), kblk, kblk, vec, vec),
        out_shape=(jax.ShapeDtypeStruct((S, XA_WIDTH), BF16), jax.ShapeDtypeStruct((n_mem, XA_WIDTH), F32),
                   jax.ShapeDtypeStruct((n_mem, XA_WIDTH), F32), jax.ShapeDtypeStruct((1, XA_DIM), F32),
                   jax.ShapeDtypeStruct((1, XA_DIM), F32)),
        scratch_shapes=[pltpu.VMEM((n_mem, XA_DIM), F32)],
        compiler_params=_cp(("arbitrary", "arbitrary")), name=name)(
            dcat, proj, kv, kv, gq.reshape(1, XA_DIM), gk.reshape(1, XA_DIM))
    return dq, jnp.concatenate([dk, dv], axis=1), dgq, dgk


SB_TQ = 256
SB_TK = 256
SB_HEADS = 24


def _sb_tile(qi, kj, t0, s0, masked):
    z = _bdot(qi, kj, NT)
    sp = _softplus(z)
    ls = z - sp
    if not masked:
        return -sp, ls, None
    mask = (s0 + _iota2(z.shape, 1)) < (t0 + _iota2(z.shape, 0))
    return jnp.where(mask, -sp, 0.0), ls, mask


def _dot2(x, tri):
    hi = x.astype(BF16)
    lo = (x - hi.astype(F32)).astype(BF16)
    dims = (NN, ((), ()))
    return (lax.dot_general(hi, tri, dims, preferred_element_type=F32)
            + lax.dot_general(lo, tri, dims, preferred_element_type=F32))


def _sb_fwd(proj, gq, gk, *, name):
    S = proj.shape[0]
    tq, tk = min(SB_TQ, S), min(SB_TK, S)
    nq = S // tq
    scale = HEAD_DIM ** -0.5

    def body(q_ref, k_ref, v_ref, gq_ref, gk_ref, o_ref, tot_ref, qn_s, kn_s, v_s):
        q = q_ref[...]
        k = k_ref[...]
        qn_s[...] = (q * lax.rsqrt(jnp.mean(q * q, axis=-1, keepdims=True) + EPS) * (gq_ref[...] * scale)).astype(BF16)
        kn_s[...] = (k * lax.rsqrt(jnp.mean(k * k, axis=-1, keepdims=True) + EPS) * gk_ref[...]).astype(BF16)
        v_s[...] = v_ref[...].astype(BF16)
        after = (_iota2((tk, tk), 0) > _iota2((tk, tk), 1)).astype(BF16)

        def qblock(i, _):
            rows = pl.ds(pl.multiple_of(i * tq, tq), tq)
            qi = qn_s[rows, :]
            jd = (i * tq) // tk

            def tile(j, acc, run, masked):
                cols = pl.ds(pl.multiple_of(j * tk, tk), tk)
                lr, ls, mask = _sb_tile(qi, kn_s[cols, :], i * tq, j * tk, masked)
                later = _dot2(lr, after) + run
                a = jnp.exp(ls + later)
                if masked:
                    a = jnp.where(mask, a, 0.0)
                acc = acc + _bdot(a, v_s[cols, :], NN)
                return acc, run + jnp.sum(lr, axis=-1, keepdims=True)

            acc, run = tile(jd, jnp.zeros((tq, HEAD_DIM), F32), jnp.zeros((tq, 1), F32), True)
            acc, run = lax.fori_loop(0, jd, lambda jj, c: tile(jd - 1 - jj, c[0], c[1], False), (acc, run))
            o_ref[rows, :] = acc
            tot_ref[rows, :] = run + jnp.zeros((tq, HEAD_DIM), F32)
            return 0

        lax.fori_loop(0, nq, qblock, 0)

    vec = pl.BlockSpec((1, HEAD_DIM), lambda h: (0, 0))
    out = pl.BlockSpec((S, HEAD_DIM), lambda h: (0, h))
    return pl.pallas_call(
        body, grid=(SB_HEADS,),
        in_specs=[pl.BlockSpec((S, HEAD_DIM), lambda h: (0, h)), pl.BlockSpec((S, HEAD_DIM), lambda h: (0, SB_HEADS + h)),
                  pl.BlockSpec((S, HEAD_DIM), lambda h: (0, 2 * SB_HEADS + h)), vec, vec],
        out_specs=(out, out), out_shape=(jax.ShapeDtypeStruct((S, MIX_WIDTH), F32),) * 2,
        scratch_shapes=[pltpu.VMEM((S, HEAD_DIM), BF16)] * 3,
        compiler_params=_cp(("parallel",)), name=name)(proj, proj, proj, gq.reshape(1, HEAD_DIM), gk.reshape(1, HEAD_DIM))


def _sb_bwd(dmix, tot, proj, gq, gk, *, name):
    S = proj.shape[0]
    tq, tk = min(SB_TQ, S), min(SB_TK, S)
    nq = S // tq
    scale = HEAD_DIM ** -0.5

    def body(do_ref, o_ref, q_ref, k_ref, v_ref, gq_ref, gk_ref, dq_ref, dk_ref, dv_ref, dgq_ref, dgk_ref,
             qn_s, kn_s, v_s, dkn_s, dqn_s, dv_s):
        h = pl.program_id(0)
        q = q_ref[...]
        k = k_ref[...]
        rq = lax.rsqrt(jnp.mean(q * q, axis=-1, keepdims=True) + EPS)
        rk = lax.rsqrt(jnp.mean(k * k, axis=-1, keepdims=True) + EPS)
        gqv = gq_ref[...]
        gkv = gk_ref[...]
        qn_s[...] = (q * rq * (gqv * scale)).astype(BF16)
        kn_s[...] = (k * rk * gkv).astype(BF16)
        v_s[...] = v_ref[...].astype(BF16)
        dkn_s[...] = jnp.zeros_like(dkn_s)
        dv_s[...] = jnp.zeros_like(dv_s)
        r_i = _iota2((tk, tk), 0)
        c_i = _iota2((tk, tk), 1)
        upto = (r_i <= c_i).astype(BF16)
        before = (r_i < c_i).astype(BF16)

        def qblock(i, _):
            rows = pl.ds(pl.multiple_of(i * tq, tq), tq)
            qi = qn_s[rows, :]
            doi = do_ref[rows, :].astype(BF16)
            tot_i = jnp.max(o_ref[rows, :], axis=-1, keepdims=True)
            jd = (i * tq) // tk

            def tile(j, dqn, run, run_b, masked):
                cols = pl.ds(pl.multiple_of(j * tk, tk), tk)
                kj = kn_s[cols, :]
                lr, ls, mask = _sb_tile(qi, kj, i * tq, j * tk, masked)
                later = tot_i - (_dot2(lr, upto) + run)
                a = jnp.exp(ls + later)
                if masked:
                    a = jnp.where(mask, a, 0.0)
                b = _bdot(doi, v_s[cols, :], NT) * a
                cum = _dot2(b, before) + run_b
                beta = jnp.exp(ls)
                dz = b * (1.0 - beta) - cum * beta
                if masked:
                    dz = jnp.where(mask, dz, 0.0)
                dzb = dz.astype(BF16)
                dv_s[cols, :] += _bdot(a, doi, TN)
                dkn_s[cols, :] += _bdot(dzb, qi, TN)
                dqn = dqn + _bdot(dzb, kj, NN)
                return dqn, run + jnp.sum(lr, axis=-1, keepdims=True), run_b + jnp.sum(b, axis=-1, keepdims=True)

            zero1 = jnp.zeros((tq, 1), F32)
            carry = lax.fori_loop(0, jd, lambda j, c: tile(j, c[0], c[1], c[2], False),
                                  (jnp.zeros((tq, HEAD_DIM), F32), zero1, zero1))
            dqn, _, _ = tile(jd, carry[0], carry[1], carry[2], True)
            dqn_s[rows, :] = dqn * scale
            return 0

        lax.fori_loop(0, nq, qblock, 0)

        @pl.when(h == 0)
        def _():
            dgq_ref[...] = jnp.zeros_like(dgq_ref)
            dgk_ref[...] = jnp.zeros_like(dgk_ref)

        dv_ref[...] = dv_s[...].astype(BF16)
        dqn = dqn_s[...]
        qh = q * rq
        dgq_ref[...] += jnp.sum(dqn * qh, axis=0, keepdims=True)
        dy = dqn * gqv
        dq_ref[...] = (rq * (dy - qh * jnp.mean(dy * qh, axis=-1, keepdims=True))).astype(BF16)
        dkn = dkn_s[...]
        kh = k * rk
        dgk_ref[...] += jnp.sum(dkn * kh, axis=0, keepdims=True)
        dyk = dkn * gkv
        dk_ref[...] = (rk * (dyk - kh * jnp.mean(dyk * kh, axis=-1, keepdims=True))).astype(BF16)

    vec = pl.BlockSpec((1, HEAD_DIM), lambda h: (0, 0))
    hb = lambda off: pl.BlockSpec((S, HEAD_DIM), lambda h: (0, off + h))
    dq, dk, dv, dgq, dgk = pl.pallas_call(
        body, grid=(SB_HEADS,),
        in_specs=[hb(0), hb(0), hb(0), hb(SB_HEADS), hb(2 * SB_HEADS), vec, vec],
        out_specs=(hb(0), hb(0), hb(0), vec, vec),
        out_shape=(jax.ShapeDtypeStruct((S, MIX_WIDTH), BF16),) * 3 + (jax.ShapeDtypeStruct((1, HEAD_DIM), F32),) * 2,
        scratch_shapes=[pltpu.VMEM((S, HEAD_DIM), BF16)] * 3 + [pltpu.VMEM((S, HEAD_DIM), F32)] * 3,
        compiler_params=_cp(("arbitrary",)), name=name)(
            dmix, tot, proj, proj, proj, gq.reshape(1, HEAD_DIM), gk.reshape(1, HEAD_DIM))
    return [dq, dk, dv], dgq, dgk


def _shift_down(x, k):
    if k == 0:
        return x
    r = pltpu.roll(x, k, 0)
    return jnp.where(_iota2(x.shape, 0) >= k, r, 0.0)


def _shift_up(x, k):
    if k == 0:
        return x
    n = x.shape[0]
    r = pltpu.roll(x, n - k, 0)
    return jnp.where(_iota2(x.shape, 0) < n - k, r, 0.0)


def _conv(x, w):
    c = w[DN_CONV - 1] * x
    for k in range(1, DN_CONV):
        c = c + w[DN_CONV - 1 - k] * _shift_down(x, k)
    return c


def _dn_pre_fwd(proj, conv_w, col0, ncols, *, l2, scale, name):
    S = proj.shape[0]
    cb = col0 // HEAD_DIM

    def body(x_ref, w_ref, o_ref):
        c = _conv(x_ref[...], [w_ref[k:k + 1, :] for k in range(DN_CONV)])
        a = c * _sigmoid(c)
        if l2:
            a = a * (lax.rsqrt(jnp.sum(a * a, axis=-1, keepdims=True) + EPS) * scale)
        o_ref[...] = a

    return pl.pallas_call(
        body, grid=(ncols // HEAD_DIM,),
        in_specs=[pl.BlockSpec((S, HEAD_DIM), lambda j: (0, cb + j)), pl.BlockSpec((DN_CONV, HEAD_DIM), lambda j: (0, cb + j))],
        out_specs=pl.BlockSpec((S, HEAD_DIM), lambda j: (0, j)), out_shape=jax.ShapeDtypeStruct((S, ncols), F32),
        compiler_params=_cp(("parallel",)), name=name)(proj, conv_w)


def _dn_pre_bwd(dout, proj, conv_w, col0, ncols, *, l2, scale, name):
    S = proj.shape[0]
    cb = col0 // HEAD_DIM
    dw_in = HEAD_DIM

    def body(d_ref, x_ref, w_ref, dx_ref, dw_ref):
        x = x_ref[...]
        w = [w_ref[k:k + 1, :] for k in range(DN_CONV)]
        c = _conv(x, w)
        sg = _sigmoid(c)
        a = c * sg
        d = d_ref[...]
        if l2:
            r = lax.rsqrt(jnp.sum(a * a, axis=-1, keepdims=True) + EPS)
            y = a * r
            d = d * scale
            d = r * (d - y * jnp.sum(d * y, axis=-1, keepdims=True))
        dc = d * sg * (1.0 + c * (1.0 - sg))
        dx = w[DN_CONV - 1] * dc
        for k in range(1, DN_CONV):
            dx = dx + w[DN_CONV - 1 - k] * _shift_up(dc, k)
        dx_ref[...] = dx.astype(BF16)
        for k in range(DN_CONV):
            dw_ref[3 - k:4 - k, :] = jnp.sum(dc * _shift_down(x, k), axis=0, keepdims=True)

    return pl.pallas_call(
        body, grid=(ncols // HEAD_DIM,),
        in_specs=[pl.BlockSpec((S, dw_in), lambda j: (0, j)), pl.BlockSpec((S, HEAD_DIM), lambda j: (0, cb + j)),
                  pl.BlockSpec((DN_CONV, HEAD_DIM), lambda j: (0, cb + j))],
        out_specs=(pl.BlockSpec((S, HEAD_DIM), lambda j: (0, j)), pl.BlockSpec((DN_CONV, HEAD_DIM), lambda j: (0, j))),
        out_shape=(jax.ShapeDtypeStruct((S, ncols), BF16), jax.ShapeDtypeStruct((DN_CONV, ncols), F32)),
        compiler_params=_cp(("parallel",)), name=name)(dout, proj, conv_w)


def _dn_ab_fwd(proj, a_log, dt_bias, *, name, tm=512):
    S = proj.shape[0]
    tm = min(tm, S)
    ab = P0_AB // LANE

    def body(a_ref, b_ref, al_ref, dt_ref, g_ref, be_ref):
        g_ref[...] = -jnp.exp(al_ref[...]) * _softplus(a_ref[...] + dt_ref[...])
        be_ref[...] = _sigmoid(b_ref[...])

    vec = pl.BlockSpec((1, LANE), lambda i: (0, 0))
    out = pl.BlockSpec((tm, LANE), lambda i: (i, 0))
    return pl.pallas_call(
        body, grid=(S // tm,),
        in_specs=[pl.BlockSpec((tm, LANE), lambda i: (i, ab)), pl.BlockSpec((tm, LANE), lambda i: (i, ab + 1)), vec, vec],
        out_specs=(out, out), out_shape=(jax.ShapeDtypeStruct((S, LANE), F32),) * 2,
        compiler_params=_cp(("parallel",)), name=name)(proj, proj, a_log, dt_bias)


def _dn_ab_bwd(dg, dbeta, proj, a_log, dt_bias, *, name, tm=512):
    S = proj.shape[0]
    tm = min(tm, S)
    ab = P0_AB // LANE

    def body(dg_ref, db_ref, a_ref, b_ref, al_ref, dt_ref, dab_ref, dal_ref, ddt_ref):
        i = pl.program_id(0)
        ea = jnp.exp(al_ref[...])
        u = a_ref[...] + dt_ref[...]
        dgv = dg_ref[...]
        da = dgv * (-ea) * _sigmoid(u)
        be = _sigmoid(b_ref[...])
        dab_ref[:, 0:LANE] = da.astype(BF16)
        dab_ref[:, LANE:2 * LANE] = (db_ref[...] * be * (1.0 - be)).astype(BF16)
        dab_ref[:, 2 * LANE:] = jnp.zeros((tm, 2 * LANE), BF16)

        @pl.when(i == 0)
        def _():
            dal_ref[...] = jnp.zeros_like(dal_ref)
            ddt_ref[...] = jnp.zeros_like(ddt_ref)

        dal_ref[...] += jnp.sum(dgv * (-ea) * _softplus(u), axis=0, keepdims=True)
        ddt_ref[...] += jnp.sum(da, axis=0, keepdims=True)

    vec = pl.BlockSpec((1, LANE), lambda i: (0, 0))
    row = pl.BlockSpec((tm, LANE), lambda i: (i, 0))
    return pl.pallas_call(
        body, grid=(S // tm,),
        in_specs=[row, row, pl.BlockSpec((tm, LANE), lambda i: (i, ab)), pl.BlockSpec((tm, LANE), lambda i: (i, ab + 1)), vec, vec],
        out_specs=(pl.BlockSpec((tm, 4 * LANE), lambda i: (i, 0)), vec, vec),
        out_shape=(jax.ShapeDtypeStruct((S, 4 * LANE), BF16), jax.ShapeDtypeStruct((1, LANE), F32),
                   jax.ShapeDtypeStruct((1, LANE), F32)),
        compiler_params=_cp(("arbitrary",)), name=name)(dg, dbeta, proj, proj, a_log, dt_bias)


def _dot3(a, b):
    ah = a.astype(BF16)
    al = (a - ah.astype(F32)).astype(BF16)
    bh = b.astype(BF16)
    bl = (b - bh.astype(F32)).astype(BF16)
    d = lambda u, v: lax.dot_general(u, v, (NN, ((), ())), preferred_element_type=F32)
    return d(ah, bh) + (d(ah, bl) + d(al, bh))


DN_PAIR = 2


def _pdot(a, b, dims, dot=None):
    dot = dot or _bdot
    return jnp.stack([dot(a[i] if a.ndim == 3 else a, b[i] if b.ndim == 3 else b, dims) for i in range(DN_PAIR)])


def _tri_inverse(a):
    eye = (_iota2((CH, CH), 0) == _iota2((CH, CH), 1)).astype(F32)
    d3 = lambda u, v: jnp.stack([_dot3(u[i], v[i]) for i in range(DN_PAIR)])
    t = eye - a
    x = d3(a, a)
    n = 2
    while True:
        t = t + d3(t, x)
        n *= 2
        if n >= CH:
            break
        x = d3(x, x)
    return t


def _pick_col(m, n):
    return jnp.sum(jnp.where(_iota2(m.shape, 2) == n, m, 0.0), axis=2, keepdims=True)


def _dn_chunk_common(kk, qk, gc_c, gc_r, be_c):
    r_i = _iota2((CH, CH), 0)
    c_i = _iota2((CH, CH), 1)
    incl = r_i >= c_i
    strict = r_i > c_i
    dec = jnp.exp(jnp.where(incl, gc_c - gc_r, -1e30))
    e = jnp.exp(gc_c)
    gl = jnp.sum(jnp.where(_iota2((1, CH), 1) == CH - 1, gc_r, 0.0), axis=-1, keepdims=True)
    kds = jnp.exp(gl - gc_c)
    cd = jnp.exp(gl)
    a = jnp.where(strict, be_c * kk * dec, 0.0)
    p = qk * dec
    return dict(incl=incl, strict=strict, dec=dec, e=e, kds=kds, cd=cd, kk=kk, a=a, qk=qk, p=p)


def _dn_decay_tables(g_ref, b_ref, gcr, gcc, bcc):
    r_i = _iota2((CH, CH), 0)
    c_i = _iota2((CH, CH), 1)
    lc = (r_i >= c_i).astype(F32)
    eye = (r_i == c_i).astype(F32)
    for hh in range(DN_PAIR):
        g_rows_v = g_ref[hh]
        gcr[hh] = _fdot(g_rows_v, lc, NT)
        gcc[hh] = _fdot(lc, g_rows_v, NT)
        bcc[hh] = _fdot(eye, b_ref[hh], NT)
    return lc


def _dn_core_fwd(qn, kn, vc, g_rows, b_rows, out_g, *, name):
    S = qn.shape[0]
    nc = S // CH

    def body(q_ref, k_ref, v_ref, g_ref, b_ref, og_ref, o_ref, st_ref, t_ref, gcr, gcc, bcc):
        _dn_decay_tables(g_ref, b_ref, gcr, gcc, bcc)
        ogv = og_ref[...]

        def chunk(n, states):
            rows = pl.ds(pl.multiple_of(n * CH, CH), CH)
            q = q_ref[rows, :]
            k = k_ref[rows, :]
            kk = _bdot(k, k, NT)
            qk = _bdot(q, k, NT)
            v = jnp.stack([v_ref[rows, hh * HEAD_DIM:(hh + 1) * HEAD_DIM] for hh in range(DN_PAIR)])
            gc_c = _pick_col(gcc[...], n)
            be_c = _pick_col(bcc[...], n)
            gc_r = gcr[:, pl.ds(n, 1), :]
            c = _dn_chunk_common(kk, qk, gc_c, gc_r, be_c)
            t = _tri_inverse(c["a"])
            u0 = _pdot(t, be_c * v, NN)
            w = _pdot(t, (be_c * c["e"]) * k, NN)
            u = u0 - _pdot(w, states, NN)
            o = _pdot(c["e"] * q, states, NN) + _pdot(c["p"], u, NN)
            on = o * lax.rsqrt(jnp.mean(o * o, axis=-1, keepdims=True) + EPS) * ogv
            for hh in range(DN_PAIR):
                st_ref[hh, n] = states[hh]
                t_ref[hh, n] = t[hh]
                o_ref[rows, hh * HEAD_DIM:(hh + 1) * HEAD_DIM] = on[hh]
            return c["cd"] * states + _pdot(c["kds"] * k, u, TN)

        lax.fori_loop(0, nc, chunk, jnp.zeros((DN_PAIR, HEAD_DIM, HEAD_DIM), F32))

    qk_spec = pl.BlockSpec((S, HEAD_DIM), lambda h: (0, h))
    v_spec = pl.BlockSpec((S, DN_PAIR * HEAD_DIM), lambda h: (0, h))
    rows_spec = pl.BlockSpec((DN_PAIR, LANE, CH), lambda h: (h, 0, 0))
    return pl.pallas_call(
        body, grid=(DN_V_HEADS // DN_PAIR,),
        in_specs=[qk_spec, qk_spec, v_spec, rows_spec, rows_spec, pl.BlockSpec((1, HEAD_DIM), lambda h: (0, 0))],
        out_specs=(v_spec, pl.BlockSpec((DN_PAIR, nc, HEAD_DIM, HEAD_DIM), lambda h: (h, 0, 0, 0)),
                   pl.BlockSpec((DN_PAIR, nc, CH, CH), lambda h: (h, 0, 0, 0))),
        out_shape=(jax.ShapeDtypeStruct((S, MIX_WIDTH), F32), jax.ShapeDtypeStruct((DN_V_HEADS, nc, HEAD_DIM, HEAD_DIM), F32),
                   jax.ShapeDtypeStruct((DN_V_HEADS, nc, CH, CH), F32)),
        scratch_shapes=[pltpu.VMEM((DN_PAIR, LANE, CH), F32), pltpu.VMEM((DN_PAIR, CH, LANE), F32),
                        pltpu.VMEM((DN_PAIR, CH, LANE), F32)],
        compiler_params=_cp(("parallel",)), name=name)(qn, kn, vc, g_rows, b_rows, out_g.reshape(1, HEAD_DIM))


def _dn_chunk_bwd(q, k, v, kk, qk, state, t, gc_c, gc_r, be_c, don, ogv, ds_next):
    ones = jnp.ones((CH, LANE), F32)
    last_row = _iota2((CH, 1), 0) == CH - 1
    rowsum = lambda z: jnp.sum(z, axis=-1, keepdims=True)
    colsum = lambda z: jnp.sum(z, axis=-2, keepdims=True)
    c = _dn_chunk_common(kk, qk, gc_c, gc_r, be_c)
    e, kds, cd, dec, a, p = c["e"], c["kds"], c["cd"], c["dec"], c["a"], c["p"]
    vb = be_c * v
    kbe = (be_c * e) * k
    u0 = _pdot(t, vb, NN)
    w = _pdot(t, kbe, NN)
    u = u0 - _pdot(w, state, NN)
    qd = e * q
    kd = kds * k
    o = _pdot(qd, state, NN) + _pdot(p, u, NN)
    r = lax.rsqrt(jnp.mean(o * o, axis=-1, keepdims=True) + EPS)
    y = o * r
    dog = colsum(don * y)
    dy = don * ogv
    d_o = r * (dy - y * jnp.mean(dy * y, axis=-1, keepdims=True))
    du = _pdot(p, d_o, TN) + _pdot(kd, ds_next, NN)
    dqd = _pdot(d_o, state, NT)
    dstate = _pdot(qd, d_o, TN) + cd * ds_next - _pdot(w, du, TN)
    dcd = colsum(rowsum(ds_next * state))
    dkd = _pdot(u, ds_next, NT)
    dw = -_pdot(du, state, NT)
    dvb = _pdot(t, du, TN)
    dkbe = _pdot(t, dw, TN)
    da = -jnp.where(c["strict"], _pdot(dvb, u0, NT) + _pdot(dkbe, w, NT), 0.0)
    dp = jnp.where(c["incl"], _pdot(d_o, u, NT), 0.0)
    gmat = da * a + dp * p
    dad = da * dec
    x = be_c * dad
    dpd = dp * dec
    dk = _pdot(x, k, NN) + _pdot(x, k, TN) + _pdot(dpd, q, TN)
    dq = _pdot(dpd, k, NN) + e * dqd
    dbe = rowsum(dad * c["kk"])
    dgc = rowsum(gmat) + rowsum(dqd * q) * e
    rk = rowsum(dkd * k) * kds
    dk = dk + kds * dkd
    dgc = dgc - rk
    dgl = colsum(rk) + dcd * cd
    sk = rowsum(dkbe * k)
    dk = dk + (be_c * e) * dkbe
    dbe = dbe + sk * e + rowsum(dvb * v)
    dgc = dgc + sk * be_c * e
    dgc = dgc + jnp.where(last_row, dgl, 0.0)
    dgc = dgc - _pdot(gmat, ones, TN, dot=_fdot)
    return dq, dk, be_c * dvb, dgc, dbe, dog, dstate


def _dn_core_bwd(dmix, qn, kn, vc, g_rows, b_rows, out_g, states, tinv, *, name):
    S = qn.shape[0]
    nc = S // CH

    def body(do_ref, q_ref, k_ref, v_ref, g_ref, b_ref, og_ref, st_ref, t_ref,
             dq_ref, dk_ref, dv_ref, dg_ref, db_ref, dog_ref, gcr, gcc, bcc, dgc_acc):
        h = pl.program_id(0)
        lc = _dn_decay_tables(g_ref, b_ref, gcr, gcc, bcc)
        ogv = og_ref[...]
        dgc_acc[...] = jnp.zeros_like(dgc_acc)
        db_ref[...] = jnp.zeros_like(db_ref)
        lane_n = _iota2((CH, LANE), 1)

        @pl.when(h == 0)
        def _():
            dog_ref[...] = jnp.zeros_like(dog_ref)

        def chunk(m, carry):
            ds_nexts, dog = carry
            n = nc - 1 - m
            rows = pl.ds(pl.multiple_of(n * CH, CH), CH)
            q = q_ref[rows, :]
            k = k_ref[rows, :]
            kk = _bdot(k, k, NT)
            qk = _bdot(q, k, NT)
            heads = lambda ref: jnp.stack([ref[rows, hh * HEAD_DIM:(hh + 1) * HEAD_DIM] for hh in range(DN_PAIR)])
            state = jnp.stack([st_ref[hh, n] for hh in range(DN_PAIR)])
            t = jnp.stack([t_ref[hh, n] for hh in range(DN_PAIR)])
            dq, dk, dv, dgc, dbe, dog_h, dstate = _dn_chunk_bwd(
                q, k, heads(v_ref), kk, qk, state, t, _pick_col(gcc[...], n), gcr[:, pl.ds(n, 1), :],
                _pick_col(bcc[...], n), heads(do_ref), ogv, ds_nexts)
            for hh in range(DN_PAIR):
                dv_ref[rows, hh * HEAD_DIM:(hh + 1) * HEAD_DIM] = dv[hh]
            dgc_acc[...] = jnp.where(lane_n == n, dgc, dgc_acc[...])
            db_ref[...] = jnp.where(lane_n == n, dbe, db_ref[...])
            dq_ref[rows, :] = jnp.sum(dq, axis=0)
            dk_ref[rows, :] = jnp.sum(dk, axis=0)
            return dstate, dog + jnp.sum(dog_h, axis=0)

        _, dog = lax.fori_loop(0, nc, chunk, (jnp.zeros((DN_PAIR, HEAD_DIM, HEAD_DIM), F32), jnp.zeros((1, HEAD_DIM), F32)))
        dog_ref[...] += dog
        for hh in range(DN_PAIR):
            dg_ref[hh] = _fdot(lc, dgc_acc[hh], TN)

    qk_spec = pl.BlockSpec((S, HEAD_DIM), lambda h: (0, h))
    v_spec = pl.BlockSpec((S, DN_PAIR * HEAD_DIM), lambda h: (0, h))
    rows_spec = pl.BlockSpec((DN_PAIR, LANE, CH), lambda h: (h, 0, 0))
    cols_spec = pl.BlockSpec((DN_PAIR, CH, LANE), lambda h: (h, 0, 0))
    vec = pl.BlockSpec((1, HEAD_DIM), lambda h: (0, 0))
    qk_out = jax.ShapeDtypeStruct((S, DN_QK_WIDTH), F32)
    return pl.pallas_call(
        body, grid=(DN_V_HEADS // DN_PAIR,),
        in_specs=[v_spec, qk_spec, qk_spec, v_spec, rows_spec, rows_spec, vec,
                  pl.BlockSpec((DN_PAIR, nc, HEAD_DIM, HEAD_DIM), lambda h: (h, 0, 0, 0)),
                  pl.BlockSpec((DN_PAIR, nc, CH, CH), lambda h: (h, 0, 0, 0))],
        out_specs=(qk_spec, qk_spec, v_spec, cols_spec, cols_spec, vec),
        out_shape=(qk_out, qk_out, jax.ShapeDtypeStruct((S, MIX_WIDTH), F32), jax.ShapeDtypeStruct((DN_V_HEADS, CH, LANE), F32),
                   jax.ShapeDtypeStruct((DN_V_HEADS, CH, LANE), F32), jax.ShapeDtypeStruct((1, HEAD_DIM), F32)),
        scratch_shapes=[pltpu.VMEM((DN_PAIR, LANE, CH), F32), pltpu.VMEM((DN_PAIR, CH, LANE), F32),
                        pltpu.VMEM((DN_PAIR, CH, LANE), F32), pltpu.VMEM((DN_PAIR, CH, LANE), F32)],
        compiler_params=_cp(("arbitrary",)), name=name)(
            dmix, qn, kn, vc, g_rows, b_rows, out_g.reshape(1, HEAD_DIM), states, tinv)


def _rows_form(x, nc):
    t = x[:, :DN_V_HEADS].T.reshape(DN_V_HEADS, nc, CH)
    return jnp.pad(t, ((0, 0), (0, LANE - nc), (0, 0)))


def _cols_to_nat(x, nc):
    t = jnp.transpose(x[:, :, :nc], (2, 1, 0)).reshape(nc * CH, DN_V_HEADS)
    return jnp.pad(t, ((0, 0), (0, LANE - DN_V_HEADS)))


_C_QKV = 2 * DN_QK_WIDTH + MIX_WIDTH


def _w0_to_padded(w):
    rows = w.shape[0]
    z = lambda n: jnp.zeros((rows, n), w.dtype)
    a = w[:, _C_QKV:_C_QKV + DN_V_HEADS]
    b = w[:, _C_QKV + DN_V_HEADS:_C_QKV + 2 * DN_V_HEADS]
    return jnp.concatenate([w[:, :_C_QKV], w[:, _C_QKV + 2 * DN_V_HEADS:], a, z(LANE - DN_V_HEADS), b,
                            z(P0 - P0_AB - LANE - DN_V_HEADS)], axis=1)


def _w0_from_padded(g):
    return jnp.concatenate([g[:, :_C_QKV], g[:, P0_AB:P0_AB + DN_V_HEADS], g[:, P0_AB + LANE:P0_AB + LANE + DN_V_HEADS],
                            g[:, _C_QKV:P0_AB]], axis=1)


def _true_pieces(lo, hi):
    out = []
    while lo < hi:
        s = lo // P0_SHARD
        end = min(hi, (s + 1) * P0_SHARD)
        out.append((s, lo - s * P0_SHARD, end - s * P0_SHARD))
        lo = end
    return out


def _padded_pieces(lo, hi):
    a0, b0, x0 = _C_QKV, _C_QKV + DN_V_HEADS, _C_QKV + 2 * DN_V_HEADS
    out = []
    for t0, t1, shift in ((0, a0, 0), (a0, b0, P0_AB - a0), (b0, x0, P0_AB + LANE - b0), (x0, DN_PROJ, a0 - x0)):
        s, e = max(lo, t0), min(hi, t1)
        if s < e:
            out.append((s + shift, e + shift))
    return out


def _pad_lane(v):
    v = v.reshape(1, -1)
    return jnp.pad(v, ((0, 0), (0, LANE - v.shape[1])))


SLOT1 = SB_PROJ // N_CHIPS
MM_TN = 512


def _local_step(x, mem, target, norm_g, mem_norm_g, xa_q_g, xa_k_g, w_in0, conv_w, a_log, dt_bias, out_g, sb_q_g, sb_k_g,
                late_weights, early_grads):
    S = x.shape[0]
    nc = S // CH
    al = _pad_lane(a_log)
    dtb = _pad_lane(dt_bias)
    q_scale = HEAD_DIM ** -0.5
    tiles1 = SLOT1 // MM_TN

    kv_rhs = lambda l: pl.BlockSpec((N_CHIPS, None, D_MODEL // N_CHIPS, MM_TN), lambda i, j: (0, l, 0, j))
    kv_rhs_t = lambda l: pl.BlockSpec((None, None, D_MODEL // N_CHIPS, 2 * XA_WIDTH), lambda i, j: (j, l, 0, 0))
    out_rhs = lambda l: pl.BlockSpec((N_CHIPS, None, INNER // N_CHIPS, MM_TN), lambda i, j: (0, l, 0, j))
    out_rhs_t = lambda l: pl.BlockSpec((None, None, MM_TN, D_MODEL), lambda i, j: (j // 2, l, j % 2, 0))
    in1_rhs = pl.BlockSpec((None, 2, D_MODEL // 2, MM_TN), lambda i, j: (j // tiles1, 0, 0, j % tiles1))
    in1_rhs_t = pl.BlockSpec((None, None, D_MODEL // 2, MM_TN), lambda i, j, k: (k // tiles1, j, 0, k % tiles1))
    slot_rows = lambda rows: dict(
        tm=rows, o_spec=pl.BlockSpec((None, rows, MM_TN), lambda i, j: (i, 0, j)),
        o_shape=jax.ShapeDtypeStruct((N_CHIPS, rows, 2 * XA_WIDTH), BF16))
    in1_out = dict(tm=D_MODEL // 2, o_spec=pl.BlockSpec((None, None, D_MODEL // 2, MM_TN),
                                                        lambda i, j: (j // tiles1, i, 0, j % tiles1)),
                   o_shape=jax.ShapeDtypeStruct((N_CHIPS, 2, D_MODEL // 2, SLOT1), BF16))

    h0 = _rmsnorm_fwd(x, norm_g[0], name="norm0")
    proj0 = _matmul(h0, w_in0, name="proj0")
    qn = _dn_pre_fwd(proj0, conv_w, 0, DN_QK_WIDTH, l2=True, scale=q_scale, name="dn_pre_q")
    kn = _dn_pre_fwd(proj0, conv_w, DN_QK_WIDTH, DN_QK_WIDTH, l2=True, scale=1.0, name="dn_pre_k")
    vc = _dn_pre_fwd(proj0, conv_w, 2 * DN_QK_WIDTH, MIX_WIDTH, l2=False, scale=1.0, name="dn_pre_v")
    g_nat, b_nat = _dn_ab_fwd(proj0, al, dtb, name="dn_ab")
    g_rows = _rows_form(g_nat, nc)
    b_rows = _rows_form(b_nat, nc)
    mix0, states, tinv = _dn_core_fwd(qn, kn, vc, g_rows, b_rows, out_g, name="dn_core")
    w_kv, w_out, w_in1 = late_weights(mix0)
    mem_n = _rmsnorm_fwd(mem, mem_norm_g, name="mem_norm")
    kv = [_matmul(mem_n, w_kv, n=2 * XA_WIDTH, tn=MM_TN, b_spec=kv_rhs(l), name=f"kv{l}") for l in range(2)]
    xa0 = _xa_fwd(proj0, P0_XQ, kv[0], xa_q_g[0], xa_k_g[0], name="xa0")
    y0 = _gate_fwd(mix0, xa0, proj0, P0_Z, name="gate0")
    x1 = _matmul(y0, w_out, n=D_MODEL, tn=MM_TN, b_spec=out_rhs(0), res=x, name="out0")

    h1 = _rmsnorm_fwd(x1, norm_g[1], name="norm1")
    proj1 = _matmul(h1, w_in1, n=SB_PROJ, tn=MM_TN, b_spec=in1_rhs, name="proj1")
    mix1, tot1 = _sb_fwd(proj1, sb_q_g, sb_k_g, name="sb")
    xa1 = _xa_fwd(proj1, P1_XQ, kv[1], xa_q_g[1], xa_k_g[1], name="xa1")
    y1 = _gate_fwd(mix1, xa1, proj1, P1_Z, name="gate1")
    x2 = _matmul(y1, w_out, n=D_MODEL, tn=MM_TN, b_spec=out_rhs(1), res=x1, name="out1")

    dx2, loss_vec = _loss_head(x2, target, name="loss")

    d_wout1 = _matmul(y1, dx2, ta=True, name="d_wout1", **slot_rows(INNER // N_CHIPS))
    dy1 = _matmul(dx2, w_out, tb=True, n=INNER, tn=MM_TN, b_spec=out_rhs_t(1), name="dy1")
    dcat1, dz1 = _gate_bwd(dy1, mix1, xa1, proj1, P1_Z, name="gate1_bwd")
    dqkv1, d_sbq, d_sbk = _sb_bwd(dcat1, tot1, proj1, sb_q_g, sb_k_g, name="sb_bwd")
    dxq1, dkv1, d_xaq1, d_xak1 = _xa_bwd(dcat1, proj1, P1_XQ, kv[1], xa_q_g[1], xa_k_g[1], name="xa1_bwd")
    dproj1 = dqkv1 + [dxq1, dz1]
    d_win1 = _matmul(h1, dproj1, ta=True, name="d_win1", **in1_out)
    d_wkv1 = _matmul(mem_n, dkv1, ta=True, name="d_wkv1", **slot_rows(D_MODEL // N_CHIPS))
    token = early_grads(1, d_win1, d_wout1, d_wkv1)
    dh1 = _matmul(dproj1, w_in1, tb=True, n=D_MODEL, tn=D_MODEL // 2, tk=MM_TN, b_spec=in1_rhs_t, name="dh1")
    dx1, d_ng1 = _rmsnorm_bwd(dh1, x1, norm_g[1] + token[0, 0], dx2, name="norm1_bwd")

    d_wout0 = _matmul(y0, dx1, ta=True, name="d_wout0", **slot_rows(INNER // N_CHIPS))
    dy0 = _matmul(dx1, w_out, tb=True, n=INNER, tn=MM_TN, b_spec=out_rhs_t(0), name="dy0")
    dcat0, dz0 = _gate_bwd(dy0, mix0, xa0, proj0, P0_Z, name="gate0_bwd")
    dqv, dkv_h, dvc, dg_cols, db_cols, d_outg = _dn_core_bwd(
        dcat0, qn, kn, vc, g_rows, b_rows, out_g, states, tinv, name="dn_core_bwd")
    dpq, dwq = _dn_pre_bwd(dqv, proj0, conv_w, 0, DN_QK_WIDTH, l2=True, scale=q_scale, name="dn_pre_q_bwd")
    dpk, dwk = _dn_pre_bwd(dkv_h, proj0, conv_w, DN_QK_WIDTH, DN_QK_WIDTH, l2=True, scale=1.0, name="dn_pre_k_bwd")
    dpv, dwv = _dn_pre_bwd(dvc, proj0, conv_w, 2 * DN_QK_WIDTH, MIX_WIDTH, l2=False, scale=1.0, name="dn_pre_v_bwd")
    dab, d_alog, d_dt = _dn_ab_bwd(_cols_to_nat(dg_cols, nc), _cols_to_nat(db_cols, nc), proj0, al, dtb, name="dn_ab_bwd")
    dxq0, dkv0, d_xaq0, d_xak0 = _xa_bwd(dcat0, proj0, P0_XQ, kv[0], xa_q_g[0], xa_k_g[0], name="xa0_bwd")
    d_win0 = _matmul(h0, [dpq, dpk, dpv, dxq0, dz0, dab], ta=True, out_dtype=BF16, name="d_win0")
    d_wkv0 = _matmul(mem_n, dkv0, ta=True, name="d_wkv0", **slot_rows(D_MODEL // N_CHIPS))
    token = early_grads(0, d_win0, d_wout0, d_wkv0)
    zero = token[0, 0]
    dh0 = _matmul([dpq, dpk, dpv, dxq0, dz0, dab + zero.astype(BF16)], w_in0, tb=True, tk=MM_TN, name="dh0")
    dx0, d_ng0 = _rmsnorm_bwd(dh0, x, norm_g[0] + zero, dx1, name="norm0_bwd")

    dmem0 = _matmul(dkv0, w_kv, tb=True, n=D_MODEL, tn=D_MODEL // N_CHIPS, b_spec=kv_rhs_t(0), name="dmem0")
    dmem_n = _matmul(dkv1, w_kv, tb=True, n=D_MODEL, tn=D_MODEL // N_CHIPS, b_spec=kv_rhs_t(1), res=dmem0, name="dmem1")
    _, d_memg = _rmsnorm_bwd(dmem_n, mem, mem_norm_g, None, name="mem_norm_bwd")

    grads = dict(
        norm_g=jnp.concatenate([d_ng0, d_ng1], axis=0), mem_norm_g=d_memg.reshape(-1),
        xa_q_norm_g=jnp.concatenate([d_xaq0, d_xaq1], axis=0), xa_k_norm_g=jnp.concatenate([d_xak0, d_xak1], axis=0),
        dn_conv_w=jnp.concatenate([dwq, dwk, dwv], axis=1),
        dn_a_log=d_alog[:, :DN_V_HEADS], dn_dt_bias=d_dt[:, :DN_V_HEADS], dn_out_norm_g=d_outg,
        sb_q_norm_g=d_sbq, sb_k_norm_g=d_sbk)
    return loss_vec, dx0, grads


ANY = pl.BlockSpec(memory_space=pl.ANY)


def _place():
    x, y, c = lax.axis_index("x"), lax.axis_index("y"), lax.axis_index("c")
    chips = [(1 - x, y), (x, 1 - y), (1 - x, 1 - y)]
    return x, y, c, 2 * x + y, (x, y, 1 - c), chips


def _rcopy(src, dst, send, recv, i, dev):
    return pltpu.make_async_remote_copy(src_ref=src, dst_ref=dst, send_sem=send.at[i], recv_sem=recv.at[i],
                                        device_id=dev, device_id_type=MESH)


def _gather_weights(srcs, *, name):
    nt = len(srcs)

    def body(*refs):
        src, dst = refs[:nt], refs[nt:2 * nt]
        send, recv = refs[2 * nt:]
        x, y, c, j, sib, chips = _place()
        sends = []
        for t in range(nt):
            for k, (cx, cy) in enumerate(chips):
                sends.append(_rcopy(src[t].at[c], dst[t].at[j, c], send, recv, 6 * t + k, (cx, cy, c)))
                sends[-1].start()
        for t in range(nt):
            for k, (cx, cy) in enumerate(chips):
                landed = dst[t].at[2 * cx + cy, c]
                _rcopy(landed, landed, send, recv, 6 * t + k, (cx, cy, c)).wait_recv()
                sends.append(_rcopy(landed, landed, send, recv, 6 * t + 3 + k, sib))
                sends[-1].start()
        for t in range(nt):
            for k, (cx, cy) in enumerate(chips):
                other = dst[t].at[2 * cx + cy, 1 - c]
                _rcopy(other, other, send, recv, 6 * t + 3 + k, sib).wait_recv()
        for cp in sends:
            cp.wait_send()

    return pl.pallas_call(
        body, in_specs=[ANY] * nt, out_specs=[ANY] * nt,
        out_shape=[jax.ShapeDtypeStruct((N_CHIPS,) + s.shape, s.dtype) for s in srcs],
        scratch_shapes=[pltpu.SemaphoreType.DMA((6 * nt,)), pltpu.SemaphoreType.DMA((6 * nt,))],
        name=name)(*srcs)


def _swap_halves(xs, *, name):
    nt = len(xs)

    def body(*refs):
        src, dst = refs[:nt], refs[nt:2 * nt]
        send, recv = refs[2 * nt:]
        x, y, c, j, sib, chips = _place()
        cps = []
        for t in range(nt):
            for s in range(N_CHIPS):
                cps.append(_rcopy(src[t].at[s, 1 - c], dst[t].at[s], send, recv, 4 * t + s, sib))
                cps[-1].start()
        for cp in cps:
            cp.wait_recv()
        for cp in cps:
            cp.wait_send()

    return pl.pallas_call(
        body, in_specs=[ANY] * nt, out_specs=[ANY] * nt,
        out_shape=[jax.ShapeDtypeStruct((N_CHIPS,) + a.shape[2:], a.dtype) for a in xs],
        scratch_shapes=[pltpu.SemaphoreType.DMA((4 * nt,)), pltpu.SemaphoreType.DMA((4 * nt,))], name=name)(*xs)


def _scatter_to_chips(ps, *, name):
    nt = len(ps)

    def body(*refs):
        src, dst = refs[:nt], refs[nt:2 * nt]
        send, recv = refs[2 * nt:]
        x, y, c, j, sib, chips = _place()
        cps = []
        for t in range(nt):
            for k, (cx, cy) in enumerate(chips):
                cps.append(_rcopy(src[t].at[2 * cx + cy], dst[t].at[k], send, recv, 3 * t + k, (cx, cy, c)))
                cps[-1].start()
        for cp in cps:
            cp.wait_recv()
        for cp in cps:
            cp.wait_send()

    return pl.pallas_call(
        body, in_specs=[ANY] * nt, out_specs=[ANY] * nt,
        out_shape=[jax.ShapeDtypeStruct((3,) + a.shape[1:], a.dtype) for a in ps],
        scratch_shapes=[pltpu.SemaphoreType.DMA((3 * nt,)), pltpu.SemaphoreType.DMA((3 * nt,))], name=name)(*ps)


def _swap_with_sibling(fs, *, name):
    nt = len(fs)

    def body(*refs):
        src, dst = refs[:nt], refs[nt:2 * nt]
        send, recv = refs[2 * nt:]
        x, y, c, j, sib, chips = _place()
        cps = [_rcopy(src[t], dst[t], send, recv, t, sib) for t in range(nt)]
        for cp in cps:
            cp.start()
        for cp in cps:
            cp.wait_recv()
        for cp in cps:
            cp.wait_send()

    return pl.pallas_call(
        body, in_specs=[ANY] * nt, out_specs=[ANY] * nt,
        out_shape=[jax.ShapeDtypeStruct(a.shape, a.dtype) for a in fs],
        scratch_shapes=[pltpu.SemaphoreType.DMA((nt,)), pltpu.SemaphoreType.DMA((nt,))], name=name)(*fs)


HBM_SPEC = pl.BlockSpec(memory_space=pltpu.HBM)
SEM_SPEC = pl.BlockSpec(memory_space=pltpu.SEMAPHORE)
SIDE_EFFECT = pltpu.SideEffectType.DATAFLOW_SIDE_EFFECTING


def _gather_plan(src, land):
    x, y, c, j, sib, chips = _place()
    return [(src[t].at[c], land[t].at[j, c], (cx, cy, c), land[t].at[2 * cx + cy, c])
            for t in range(len(src)) for cx, cy in chips]


def _scatter_plan(src, land):
    x, y, c, j, sib, chips = _place()
    return [(src[t].at[2 * cx + cy], land[t].at[k], (cx, cy, c), land[t].at[k])
            for t in range(len(src)) for k, (cx, cy) in enumerate(chips)]


def _exchange_start(srcs, lands, plan, *, name):
    ns, nb = len(srcs), len(srcs) + len(lands)
    n = 3 * ns

    def body(*refs):
        send, recv, token = refs[nb], refs[nb + 1], refs[-1]
        for i, (s, d, dev, _) in enumerate(plan(refs[:ns], refs[ns:nb])):
            _rcopy(s, d, send, recv, i, dev).start()
        token[...] = jnp.zeros_like(token)

    bufs = list(srcs) + list(lands)
    outs = pl.pallas_call(
        body, name=name,
        out_shape=(pltpu.SemaphoreType.DMA((n,)), pltpu.SemaphoreType.DMA((n,)), *[pltpu.HBM(a.shape, a.dtype) for a in bufs],
                   jax.ShapeDtypeStruct((8, LANE), F32)),
        in_specs=[HBM_SPEC] * nb, out_specs=(SEM_SPEC, SEM_SPEC, *[HBM_SPEC] * nb, pl.BlockSpec(memory_space=pltpu.VMEM)),
        input_output_aliases={i: 2 + i for i in range(nb)},
        compiler_params=pltpu.CompilerParams(has_side_effects=SIDE_EFFECT))(
            *[pltpu.with_memory_space_constraint(a, pltpu.HBM) for a in bufs])
    return outs[0], outs[1], list(outs[2:2 + ns]), list(outs[2 + ns:2 + nb]), outs[-1]


def _exchange_wait(srcs, lands, send, recv, after, plan, *, name):
    ns, nb = len(srcs), len(srcs) + len(lands)

    def body(*refs):
        send_s, recv_s = refs[nb], refs[nb + 1]
        for i, (s, d, dev, inc) in enumerate(plan(refs[:ns], refs[ns:nb])):
            _rcopy(s, d, send_s, recv_s, i, dev).wait_send()
            _rcopy(inc, inc, send_s, recv_s, i, dev).wait_recv()

    bufs = list(srcs) + list(lands)
    outs = pl.pallas_call(
        body, name=name, out_shape=tuple(pltpu.HBM(a.shape, a.dtype) for a in bufs),
        in_specs=[HBM_SPEC] * nb + [SEM_SPEC, SEM_SPEC, ANY], out_specs=tuple([HBM_SPEC] * nb),
        input_output_aliases={i: i for i in range(nb)},
        compiler_params=pltpu.CompilerParams(has_side_effects=SIDE_EFFECT))(*bufs, send, recv, after)
    return list(outs[:ns]), list(outs[ns:])


def _forward_halves(lands, *, name):
    nt = len(lands)

    def body(*refs):
        src, dst = refs[:nt], refs[nt:2 * nt]
        send, recv = refs[2 * nt:]
        x, y, c, j, sib, chips = _place()
        cps = []
        for t in range(nt):
            for k, (cx, cy) in enumerate(chips):
                cps.append(_rcopy(src[t].at[2 * cx + cy, c], dst[t].at[2 * cx + cy, c], send, recv, 3 * t + k, sib))
                cps[-1].start()
        for t in range(nt):
            for k, (cx, cy) in enumerate(chips):
                other = dst[t].at[2 * cx + cy, 1 - c]
                _rcopy(other, other, send, recv, 3 * t + k, sib).wait_recv()
        for cp in cps:
            cp.wait_send()

    return pl.pallas_call(
        body, in_specs=[ANY] * nt, out_specs=[ANY] * nt, out_shape=[jax.ShapeDtypeStruct(a.shape, a.dtype) for a in lands],
        input_output_aliases={t: t for t in range(nt)},
        scratch_shapes=[pltpu.SemaphoreType.DMA((3 * nt,)), pltpu.SemaphoreType.DMA((3 * nt,))], name=name)(*lands)


def _all_reduce_small(parts, *, name):
    n = len(parts)
    offs, rows = [], 0
    for p in parts:
        offs.append(rows)
        rows += -(-p.shape[0] // 8) * 8

    def body(*refs):
        p_refs, o_refs = refs[:n], refs[n:2 * n]
        buf, send, recv = refs[2 * n:]
        x, y, c = lax.axis_index("x"), lax.axis_index("y"), lax.axis_index("c")
        me = 4 * x + 2 * y + c
        buf[me] = jnp.zeros((rows, LANE), F32)
        for p_ref, off in zip(p_refs, offs):
            buf[me, off:off + p_ref.shape[0], :] = p_ref[...]
        cps = []
        for r in range(1, 8):
            dev = (x ^ (r >> 2), y ^ ((r >> 1) & 1), c ^ (r & 1))
            cps.append(_rcopy(buf.at[me], buf.at[me], send, recv, r - 1, dev))
            cps[-1].start()
        for r in range(1, 8):
            frm = buf.at[me ^ r]
            _rcopy(frm, frm, send, recv, r - 1, (x, y, c)).wait_recv()
        for cp in cps:
            cp.wait_send()
        acc = buf[0]
        for d in range(1, 8):
            acc = acc + buf[d]
        for o_ref, off in zip(o_refs, offs):
            o_ref[...] = acc[off:off + o_ref.shape[0], :]

    vm = pl.BlockSpec(memory_space=pltpu.VMEM)
    return pl.pallas_call(
        body, in_specs=[vm] * n, out_specs=[vm] * n, out_shape=[jax.ShapeDtypeStruct(p.shape, F32) for p in parts],
        scratch_shapes=[pltpu.VMEM((8, rows, LANE), F32), pltpu.SemaphoreType.DMA((7,)), pltpu.SemaphoreType.DMA((7,))],
        name=name)(*parts)


def _add_halves(x, b, c_idx, *, name, tr=256):
    _, _, R, C = x.shape
    tr = min(tr, R)

    def body(c_ref, x_ref, b_ref, o_ref):
        o_ref[...] = (x_ref[...].astype(F32) + b_ref[...].astype(F32)).astype(o_ref.dtype)

    return pl.pallas_call(
        body,
        grid_spec=pltpu.PrefetchScalarGridSpec(
            num_scalar_prefetch=1, grid=(N_CHIPS, R // tr),
            in_specs=[pl.BlockSpec((None, None, tr, C), lambda s, i, c_ref: (s, c_ref[0], i, 0)),
                      pl.BlockSpec((None, tr, C), lambda s, i, c_ref: (s, i, 0))],
            out_specs=pl.BlockSpec((None, tr, C), lambda s, i, c_ref: (s, i, 0))),
        out_shape=jax.ShapeDtypeStruct(b.shape, b.dtype), compiler_params=_cp(("parallel", "parallel")), name=name)(c_idx, x, b)


def _sum_slot(p, rcv, j_idx, *, name, tr=256):
    _, R, C = p.shape
    tr = min(tr, R)

    def body(j_ref, p_ref, r_ref, o_ref):
        acc = p_ref[...].astype(F32)
        for k in range(3):
            acc = acc + r_ref[k].astype(F32)
        o_ref[...] = acc

    return pl.pallas_call(
        body,
        grid_spec=pltpu.PrefetchScalarGridSpec(
            num_scalar_prefetch=1, grid=(R // tr,),
            in_specs=[pl.BlockSpec((None, tr, C), lambda i, j_ref: (j_ref[0], i, 0)),
                      pl.BlockSpec((3, tr, C), lambda i, j_ref: (0, i, 0))],
            out_specs=pl.BlockSpec((tr, C), lambda i, j_ref: (i, 0))),
        out_shape=jax.ShapeDtypeStruct((R, C), F32), compiler_params=_cp(("parallel",)), name=name)(j_idx, p, rcv)


def _adamw_math(w, g, m, v):
    nm = ADAM_B1 * m + (1.0 - ADAM_B1) * g
    nv = ADAM_B2 * v + (1.0 - ADAM_B2) * (g * g)
    m_hat = nm / (1.0 - ADAM_B1 ** ADAM_STEP)
    v_hat = nv / (1.0 - ADAM_B2 ** ADAM_STEP)
    return -ADAM_LR * (m_hat / (jnp.sqrt(v_hat) + ADAM_EPS) + ADAM_WD * w), nm, nv


def _adamw_halves(w, g_mine, g_theirs, m, v, c_idx, *, name, tr=128):
    _, R, C = w.shape
    tr = tr if R % tr == 0 else R

    def body(c_ref, w_ref, gm_ref, gt_ref, m_ref, v_ref, g_ref, d_ref, nm_ref, nv_ref):
        gv = jnp.where(pl.program_id(0) == c_ref[0], gm_ref[...], gt_ref[...])
        d, nm, nv = _adamw_math(w_ref[...], gv, m_ref[...], v_ref[...])
        g_ref[...] = gv
        d_ref[...] = d
        nm_ref[...] = nm
        nv_ref[...] = nv

    full = pl.BlockSpec((None, tr, C), lambda hh, i, c_ref: (hh, i, 0))
    half = pl.BlockSpec((tr, C), lambda hh, i, c_ref: (i, 0))
    sh = jax.ShapeDtypeStruct(w.shape, F32)
    return pl.pallas_call(
        body,
        grid_spec=pltpu.PrefetchScalarGridSpec(num_scalar_prefetch=1, grid=(2, R // tr),
                                               in_specs=[full, half, half, full, full], out_specs=(full,) * 4),
        out_shape=(sh,) * 4, compiler_params=_cp(("parallel", "parallel")), name=name)(c_idx, w, g_mine, g_theirs, m, v)


def _adamw_parts(ws, gs, ms, vs, *, name):
    n = len(ws)

    def body(*refs):
        ins, outs = refs[:4 * n], refs[4 * n:]
        for t in range(n):
            d, nm, nv = _adamw_math(ins[t][...], ins[n + t][...], ins[2 * n + t][...], ins[3 * n + t][...])
            outs[t][...] = d
            outs[n + t][...] = nm
            outs[2 * n + t][...] = nv

    vm = pl.BlockSpec(memory_space=pltpu.VMEM)
    shapes = [jax.ShapeDtypeStruct(w.shape, F32) for w in ws] * 3
    outs = pl.pallas_call(body, in_specs=[vm] * (4 * n), out_specs=[vm] * (3 * n), out_shape=shapes, name=name)(
        *ws, *gs, *ms, *vs)
    return outs[:n], outs[n:2 * n], outs[2 * n:]


_SMALL = ["norm_g", "mem_norm_g", "xa_q_norm_g", "xa_k_norm_g", "dn_a_log", "dn_dt_bias", "dn_out_norm_g",
          "sb_q_norm_g", "sb_k_norm_g"]


def _rows128(a):
    flat = a.reshape(-1)
    pad = -flat.shape[0] % LANE
    if pad:
        flat = jnp.pad(flat, (0, pad))
    return flat.reshape(-1, LANE)


def _unrows(r, shape):
    return r.reshape(-1)[:math.prod(shape)].reshape(shape)


def kernel(x, mem, norm_g, mem_norm_g, mem_w_kv, xa_q_norm_g, xa_k_norm_g, w_out, dn_w_in, dn_conv_w, dn_a_log, dn_dt_bias, dn_out_norm_g, sb_w_in, sb_q_norm_g, sb_k_norm_g, loss_target, m_norm_g, m_mem_norm_g, m_mem_w_kv, m_xa_q_norm_g, m_xa_k_norm_g, m_w_out, m_dn_w_in, m_dn_conv_w, m_dn_a_log, m_dn_dt_bias, m_dn_out_norm_g, m_sb_w_in, m_sb_q_norm_g, m_sb_k_norm_g, v_norm_g, v_mem_norm_g, v_mem_w_kv, v_xa_q_norm_g, v_xa_k_norm_g, v_w_out, v_dn_w_in, v_dn_conv_w, v_dn_a_log, v_dn_dt_bias, v_dn_out_norm_g, v_sb_w_in, v_sb_q_norm_g, v_sb_k_norm_g):
    W = dict(norm_g=norm_g, mem_norm_g=mem_norm_g, mem_w_kv=mem_w_kv, xa_q_norm_g=xa_q_norm_g, xa_k_norm_g=xa_k_norm_g,
             w_out=w_out, dn_w_in=dn_w_in, dn_conv_w=dn_conv_w, dn_a_log=dn_a_log, dn_dt_bias=dn_dt_bias,
             dn_out_norm_g=dn_out_norm_g, sb_w_in=sb_w_in, sb_q_norm_g=sb_q_norm_g, sb_k_norm_g=sb_k_norm_g)
    M = dict(norm_g=m_norm_g, mem_norm_g=m_mem_norm_g, mem_w_kv=m_mem_w_kv, xa_q_norm_g=m_xa_q_norm_g,
             xa_k_norm_g=m_xa_k_norm_g, w_out=m_w_out, dn_w_in=m_dn_w_in, dn_conv_w=m_dn_conv_w, dn_a_log=m_dn_a_log,
             dn_dt_bias=m_dn_dt_bias, dn_out_norm_g=m_dn_out_norm_g, sb_w_in=m_sb_w_in, sb_q_norm_g=m_sb_q_norm_g,
             sb_k_norm_g=m_sb_k_norm_g)
    V = dict(norm_g=v_norm_g, mem_norm_g=v_mem_norm_g, mem_w_kv=v_mem_w_kv, xa_q_norm_g=v_xa_q_norm_g,
             xa_k_norm_g=v_xa_k_norm_g, w_out=v_w_out, dn_w_in=v_dn_w_in, dn_conv_w=v_dn_conv_w, dn_a_log=v_dn_a_log,
             dn_dt_bias=v_dn_dt_bias, dn_out_norm_g=v_dn_out_norm_g, sb_w_in=v_sb_w_in, sb_q_norm_g=v_sb_q_norm_g,
             sb_k_norm_g=v_sb_k_norm_g)
    names = ["norm_g", "mem_norm_g", "mem_w_kv", "xa_q_norm_g", "xa_k_norm_g", "w_out", "dn_w_in", "dn_conv_w",
             "dn_a_log", "dn_dt_bias", "dn_out_norm_g", "sb_w_in", "sb_q_norm_g", "sb_k_norm_g"]
    cx, cy, cc = lax.axis_index("x"), lax.axis_index("y"), lax.axis_index("c")
    slot = 2 * cx + cy
    half_r = D_MODEL // 2
    conv_cols = dn_conv_w.shape[2]

    w0s = jnp.pad(dn_w_in[0].astype(BF16), ((0, 0), (0, P0_SHARD_PAD - P0_SHARD))).reshape(2, half_r, P0_SHARD_PAD)
    w1s = sb_w_in[0].astype(BF16).reshape(2, half_r, SB_PROJ // N_CHIPS)
    convs = jnp.pad(dn_conv_w[0], ((0, 8 - DN_CONV), (0, 0))).reshape(8, 2, conv_cols // 2).transpose(1, 0, 2)
    c_idx = jnp.reshape(cc, (1,)).astype(jnp.int32)
    j_idx = jnp.reshape(slot, (1,)).astype(jnp.int32)
    own_a = [w0s, convs]
    own_b = [w1s, w_out.astype(BF16), mem_w_kv.astype(BF16)]
    gathered_a, own_b = lax.optimization_barrier((_gather_weights(own_a, name="gather_weights"), own_b))
    g0, gconv = [lax.dynamic_update_slice(g, o[None], (slot, 0, 0, 0)) for o, g in zip(own_a, gathered_a)]
    lands_b = [lax.dynamic_update_slice(lax.empty((N_CHIPS,) + o.shape, o.dtype), o[None], (slot, 0, 0, 0)) for o in own_b]
    send_b, recv_b, own_b, lands_b, token_b = _exchange_start(own_b, lands_b, _gather_plan, name="gather_late_start")

    def late_weights(after):
        _, lands = _exchange_wait(own_b, lands_b, send_b, recv_b, after, _gather_plan, name="gather_late_wait")
        g1, gout, gkv = _forward_halves(lands, name="gather_late_forward")
        return gkv, gout, g1

    rs = {}

    def scatter_start(tag, xs):
        from_sib = _swap_halves(xs, name=f"rs{tag}_swap")
        ps = [_add_halves(a, b, c_idx, name=f"rs{tag}_add{t}") for t, (a, b) in enumerate(zip(xs, from_sib))]
        rcv = [lax.empty((3,) + p.shape[1:], p.dtype) for p in ps]
        send, recv, ps, rcv, token = _exchange_start(ps, rcv, _scatter_plan, name=f"rs{tag}_scatter_start")
        rs[tag] = (ps, rcv, send, recv)
        return token

    def scatter_finish(tag, after):
        ps, rcv, send, recv = rs[tag]
        ps, rcv = _exchange_wait(ps, rcv, send, recv, after, _scatter_plan, name=f"rs{tag}_scatter_wait")
        return [_sum_slot(p, r, j_idx, name=f"rs{tag}_sum{t}") for t, (p, r) in enumerate(zip(ps, rcv))]

    def early_grads(layer, d_win, d_wout, d_wkv):
        if layer == 0:
            d_win = jnp.stack([jnp.pad(jnp.concatenate(
                [d_win[:, lo:hi] for lo, hi in _padded_pieces(s * P0_SHARD, (s + 1) * P0_SHARD)], axis=1),
                ((0, 0), (0, P0_SHARD_PAD - P0_SHARD))) for s in range(N_CHIPS)]).reshape(N_CHIPS, 2, half_r, P0_SHARD_PAD)
        return scatter_start(layer, [d_win, d_wout.reshape(N_CHIPS, 2, -1, D_MODEL), d_wkv.reshape(N_CHIPS, 2, -1, 2 * XA_WIDTH)])

    shards0 = g0.reshape(N_CHIPS, D_MODEL, P0_SHARD_PAD)
    z = lambda n: jnp.zeros((D_MODEL, n), BF16)
    w_in0 = jnp.concatenate(
        [shards0[s][:, lo:hi] for s, lo, hi in _true_pieces(0, _C_QKV) + _true_pieces(_C_QKV + 2 * DN_V_HEADS, DN_PROJ)]
        + [shards0[s][:, lo:hi] for s, lo, hi in _true_pieces(_C_QKV, _C_QKV + DN_V_HEADS)] + [z(LANE - DN_V_HEADS)]
        + [shards0[s][:, lo:hi] for s, lo, hi in _true_pieces(_C_QKV + DN_V_HEADS, _C_QKV + 2 * DN_V_HEADS)]
        + [z(P0 - P0_AB - LANE - DN_V_HEADS)], axis=1)
    conv_f = gconv.transpose(2, 0, 1, 3).reshape(8, N_CHIPS * conv_cols)[:DN_CONV]

    loss_vec, grad_x, g = _local_step(
        x[0], mem[0], loss_target[0], norm_g + token_b[0, 0], mem_norm_g, xa_q_norm_g, xa_k_norm_g, w_in0, conv_f,
        dn_a_log[0], dn_dt_bias[0], dn_out_norm_g[0], sb_q_norm_g[0], sb_k_norm_g[0], late_weights, early_grads)

    mine1 = scatter_finish(1, grad_x)
    theirs1 = _swap_with_sibling(mine1, name="rs1_join")
    big1 = [("sb_w_in", None), ("w_out", 1), ("mem_w_kv", 1)]
    big0 = [("dn_w_in", None), ("w_out", 0), ("mem_w_kv", 0)]

    red = _all_reduce_small([_rows128(g[n]) for n in _SMALL] + [_rows128(g["dn_conv_w"]), loss_vec], name="all_reduce_small")
    small_rows = dict(zip(_SMALL, red))
    conv_full = red[len(_SMALL)].reshape(DN_CONV, N_CHIPS * conv_cols)
    small_rows["dn_conv_w"] = _rows128(lax.dynamic_slice_in_dim(conv_full, slot * conv_cols, conv_cols, axis=1))
    loss = red[-1][0, 0]

    out_g, out_d, out_m, out_v = {}, {}, {}, {}
    per_layer = {}

    def adamw_big(big, mine, theirs):
        last = None
        for (n, layer), mine_g, their_g in zip(big, mine, theirs):
            pick = (lambda a: a) if layer is None else (lambda a: a[layer])
            h3 = (2,) + mine_g.shape
            outs = _adamw_halves(pick(W[n]).reshape(h3), mine_g, their_g, pick(M[n]).reshape(h3), pick(V[n]).reshape(h3),
                                 c_idx, name=f"adamw_{n}" + ("" if layer is None else str(layer)))
            if layer is None:
                out_g[n], out_d[n], out_m[n], out_v[n] = [o.reshape(W[n].shape) for o in outs]
            else:
                per_layer[(n, layer)] = [o.reshape(W[n].shape[1:]) for o in outs]
            last = outs[0]
        return last

    done1 = adamw_big(big1, mine1, theirs1)
    mine0 = scatter_finish(0, done1)
    mine0[0] = mine0[0][:, :P0_SHARD]
    adamw_big(big0, mine0, _swap_with_sibling(mine0, name="rs0_join"))
    for n in ("w_out", "mem_w_kv"):
        out_g[n], out_d[n], out_m[n], out_v[n] = [jnp.stack([per_layer[(n, 0)][i], per_layer[(n, 1)][i]]) for i in range(4)]
    small_names = _SMALL + ["dn_conv_w"]
    ds, nms, nvs = _adamw_parts([_rows128(W[n]) for n in small_names], [small_rows[n] for n in small_names],
                                [_rows128(M[n]) for n in small_names], [_rows128(V[n]) for n in small_names], name="adamw_small")
    for n, d, nm, nv in zip(small_names, ds, nms, nvs):
        shp = W[n].shape
        out_g[n], out_d[n], out_m[n], out_v[n] = [_unrows(r, shp) for r in (small_rows[n], d, nm, nv)]

    return (loss, grad_x[None], *[out_g[n] for n in names], *[out_d[n] for n in names], *[out_m[n] for n in names],
            *[out_v[n] for n in names])
```

```python
import functools
import math

import jax
import jax.numpy as jnp
from jax import lax
from jax.experimental import pallas as pl
from jax.experimental.pallas import tpu as pltpu

F32 = jnp.float32
BF16 = jnp.bfloat16
HI = lax.Precision.HIGHEST
MESH = pl.DeviceIdType.MESH

D_MODEL = 2048
INNER = 4096
XA_WIDTH = 1024
XA_HEADS = 4
XA_DIM = 256
MIX_WIDTH = 3072
HEAD_DIM = 128
DN_V_HEADS = 24
DN_QK_WIDTH = 1536
DN_CONV = 4
DN_PROJ = 11312
SB_PROJ = 14336
EPS = 1e-6
N_CHIPS = 4

CH = 128
LANE = 128

P0_XQ = 6144
P0_Z = 7168
P0_AB = 11264
P0 = 11776
P0_SHARD = DN_PROJ // N_CHIPS
P0_SHARD_PAD = 2944
P1_XQ = 9216
P1_Z = 10240
P1 = SB_PROJ

ADAM_LR = 0.001
ADAM_B1 = 0.9
ADAM_B2 = 0.999
ADAM_EPS = 1e-08
ADAM_WD = 0.01
ADAM_STEP = 10

VMEM_LIMIT = 48 * 1024 * 1024


def _cp(sem=None, **kw):
    return pltpu.CompilerParams(dimension_semantics=sem, vmem_limit_bytes=VMEM_LIMIT, **kw)


def _bdot(a, b, dims):
    return lax.dot_general(a.astype(BF16), b.astype(BF16), (dims, ((), ())), preferred_element_type=F32)


def _fdot(a, b, dims):
    return lax.dot_general(a, b, (dims, ((), ())), precision=HI, preferred_element_type=F32)


NN = ((1,), (0,))
NT = ((1,), (1,))
TN = ((0,), (0,))


def _sigmoid(x):
    return 1.0 / (1.0 + jnp.exp(-x))


def _softplus(x):
    return jnp.maximum(x, 0.0) + jnp.log(1.0 + jnp.exp(-jnp.abs(x)))


def _iota2(shape, axis):
    return lax.broadcasted_iota(jnp.int32, shape, axis)


MM_FULL_K = 4096
MM_BLOCK_BYTES = 4 * 1024 * 1024


def _matmul(a, b, *, ta=False, tb=False, out_dtype=F32, res=None, name, n=None, tm=None, tn=None, tk=None,
            b_spec=None, o_spec=None, o_shape=None):
    a_segs = list(a) if isinstance(a, (list, tuple)) else [a]
    b_segs = list(b) if isinstance(b, (list, tuple)) else [b]
    a0, b0 = a_segs[0], b_segs[0]
    M = a0.shape[1] if ta else a0.shape[0]
    K = a0.shape[0] if ta else sum(s.shape[1] for s in a_segs)
    if n is None:
        n = b0.shape[0] if tb else sum(s.shape[1] for s in b_segs)
    N = n
    dims = ((0,) if ta else (1,), (1,) if tb else (0,))
    has_res = res is not None
    flat = lambda v: v.reshape(-1, v.shape[-1])
    o_shape = o_shape or jax.ShapeDtypeStruct((M, N), out_dtype)

    def seg_specs(segs, tile, block, pos):
        specs, ranges, off = [], [], 0
        for s in segs:
            cnt = s.shape[1] // tile
            assert s.shape[1] % tile == 0, (name, s.shape, tile)

            def imap(*g, off=off, cnt=cnt):
                t = jnp.clip(g[pos] - off, 0, cnt - 1)
                return (g[0], t) if pos == 2 else (0, t)

            specs.append(pl.BlockSpec(block, imap))
            ranges.append((off, off + cnt))
            off += cnt
        return specs, ranges

    if K <= MM_FULL_K:
        assert len(a_segs) == 1
        tm = tm or min(M, 1024, max(256, MM_BLOCK_BYTES // (K * a0.dtype.itemsize)))
        tn = tn or min(N, 512)
        assert M % tm == 0 and N % tn == 0, (name, M, N, K, tm, tn)
        nb = len(b_segs)
        if b_spec is not None:
            b_specs, b_ranges = [b_spec], [(0, N // tn)]
        elif nb > 1:
            assert not tb
            b_specs, b_ranges = seg_specs(b_segs, tn, (K, tn), 1)
        else:
            b_specs = [pl.BlockSpec((tn, K), lambda i, j: (j, 0)) if tb else pl.BlockSpec((K, tn), lambda i, j: (0, j))]
            b_ranges = [(0, N // tn)]

        def body_full(*refs):
            a_ref, b_refs = refs[0], refs[1:1 + nb]
            r_ref = refs[1 + nb] if has_res else None
            o_ref = refs[-1]
            j = pl.program_id(1)
            for b_ref, (lo, hi) in zip(b_refs, b_ranges):
                def emit(b_ref=b_ref):
                    r = _bdot(a_ref[...], flat(b_ref[...]), dims)
                    if has_res:
                        r = r + r_ref[...]
                    o_ref[...] = r.astype(o_ref.dtype).reshape(o_ref.shape)
                if nb == 1:
                    emit()
                else:
                    pl.when(jnp.logical_and(j >= lo, j < hi))(emit)

        a_spec = pl.BlockSpec((K, tm), lambda i, j: (0, i)) if ta else pl.BlockSpec((tm, K), lambda i, j: (i, 0))
        o_spec = o_spec or pl.BlockSpec((tm, tn), lambda i, j: (i, j))
        r_spec = [pl.BlockSpec((tm, tn), lambda i, j: (i, j))] if has_res else []
        return pl.pallas_call(
            body_full, grid=(M // tm, N // tn), in_specs=[a_spec] + b_specs + r_spec, out_specs=o_spec, out_shape=o_shape,
            compiler_params=_cp(("parallel", "arbitrary")), name=name)(*([a0] + b_segs + ([res] if has_res else [])))

    assert tb and not ta and len(b_segs) == 1
    tm, tn = tm or min(M, 1024), tn or min(N, 1024)
    tk = tk or (1024 if all(s.shape[1] % 1024 == 0 for s in a_segs) else 512)
    assert M % tm == 0 and N % tn == 0 and K % tk == 0, (name, M, N, K, tm, tn, tk)
    nk = K // tk
    na = len(a_segs)
    if na > 1:
        a_specs, a_ranges = seg_specs(a_segs, tk, (tm, tk), 2)
    else:
        a_specs, a_ranges = [pl.BlockSpec((tm, tk), lambda i, j, k: (i, k))], [(0, nk)]
    b_spec = b_spec or pl.BlockSpec((tn, tk), lambda i, j, k: (j, k))

    def body(*refs):
        a_refs, b_ref = refs[:na], refs[na]
        r_ref = refs[na + 1] if has_res else None
        o_ref, acc = refs[-2], refs[-1]
        k = pl.program_id(2)

        @pl.when(k == 0)
        def _():
            acc[...] = jnp.zeros_like(acc)

        for a_ref, (lo, hi) in zip(a_refs, a_ranges):
            def emit(a_ref=a_ref):
                acc[...] += _bdot(a_ref[...], flat(b_ref[...]), dims)
            if na == 1:
                emit()
            else:
                pl.when(jnp.logical_and(k >= lo, k < hi))(emit)

        @pl.when(k == nk - 1)
        def _():
            r = acc[...]
            if has_res:
                r = r + r_ref[...]
            o_ref[...] = r.astype(o_ref.dtype).reshape(o_ref.shape)

    o_spec = o_spec or pl.BlockSpec((tm, tn), lambda i, j, k: (i, j))
    r_spec = [pl.BlockSpec((tm, tn), lambda i, j, k: (i, j))] if has_res else []
    return pl.pallas_call(
        body, grid=(M // tm, N // tn, nk), in_specs=a_specs + [b_spec] + r_spec, out_specs=o_spec, out_shape=o_shape,
        scratch_shapes=[pltpu.VMEM((tm, tn), F32)],
        compiler_params=_cp(("parallel", "parallel", "arbitrary")), name=name)(*(a_segs + [b0] + ([res] if has_res else [])))


def _rmsnorm_fwd(x, g, *, name, tm=256):
    S, Dm = x.shape
    tm = min(tm, S)

    def body(x_ref, g_ref, o_ref):
        xv = x_ref[...]
        r = lax.rsqrt(jnp.mean(xv * xv, axis=-1, keepdims=True) + EPS)
        o_ref[...] = (xv * r * g_ref[...]).astype(BF16)

    return pl.pallas_call(
        body, grid=(S // tm,), in_specs=[pl.BlockSpec((tm, Dm), lambda i: (i, 0)), pl.BlockSpec((1, Dm), lambda i: (0, 0))],
        out_specs=pl.BlockSpec((tm, Dm), lambda i: (i, 0)), out_shape=jax.ShapeDtypeStruct((S, Dm), BF16),
        compiler_params=_cp(("parallel",)), name=name)(x, g.reshape(1, Dm))


def _rmsnorm_bwd(dh, x, g, dres, *, name, tm=256):
    S, Dm = x.shape
    tm = min(tm, S)
    want_dx = dres is not None

    def body(*refs):
        if want_dx:
            dh_ref, x_ref, g_ref, dr_ref, dx_ref, dg_ref = refs
        else:
            dh_ref, x_ref, g_ref, dg_ref = refs
        i = pl.program_id(0)
        xv = x_ref[...]
        dhv = dh_ref[...]
        r = lax.rsqrt(jnp.mean(xv * xv, axis=-1, keepdims=True) + EPS)
        y = xv * r
        part = jnp.sum(dhv * y, axis=0, keepdims=True)

        @pl.when(i == 0)
        def _():
            dg_ref[...] = jnp.zeros_like(dg_ref)

        dg_ref[...] += part
        if want_dx:
            dy = dhv * g_ref[...]
            dx_ref[...] = dr_ref[...] + r * (dy - y * jnp.mean(dy * y, axis=-1, keepdims=True))

    row = pl.BlockSpec((tm, Dm), lambda i: (i, 0))
    vec = pl.BlockSpec((1, Dm), lambda i: (0, 0))
    if want_dx:
        dx, dg = pl.pallas_call(
            body, grid=(S // tm,), in_specs=[row, row, vec, row], out_specs=(row, vec),
            out_shape=(jax.ShapeDtypeStruct((S, Dm), F32), jax.ShapeDtypeStruct((1, Dm), F32)),
            compiler_params=_cp(("arbitrary",)), name=name)(dh, x, g.reshape(1, Dm), dres)
        return dx, dg
    dg = pl.pallas_call(
        body, grid=(S // tm,), in_specs=[row, row, vec], out_specs=vec,
        out_shape=jax.ShapeDtypeStruct((1, Dm), F32), compiler_params=_cp(("arbitrary",)), name=name)(dh, x, g.reshape(1, Dm))
    return None, dg


GATE_TN = XA_WIDTH
GATE_MIX_TILES = MIX_WIDTH // GATE_TN


def _gate_cat_specs(tm):
    return [pl.BlockSpec((tm, GATE_TN), lambda i, j: (i, jnp.minimum(j, GATE_MIX_TILES - 1))),
            pl.BlockSpec((tm, GATE_TN), lambda i, j: (i, 0))]


def _gate_fwd(mix, xa, proj, z_off, *, name, tm=256):
    S = mix.shape[0]
    tm = min(tm, S)
    zb = z_off // GATE_TN

    def body(m_ref, x_ref, z_ref, y_ref):
        z = z_ref[...]
        c = jnp.where(pl.program_id(1) < GATE_MIX_TILES, m_ref[...], x_ref[...])
        y_ref[...] = (c * z * _sigmoid(z)).astype(BF16)

    blk = pl.BlockSpec((tm, GATE_TN), lambda i, j: (i, j))
    return pl.pallas_call(
        body, grid=(S // tm, INNER // GATE_TN),
        in_specs=_gate_cat_specs(tm) + [pl.BlockSpec((tm, GATE_TN), lambda i, j: (i, zb + j))],
        out_specs=blk, out_shape=jax.ShapeDtypeStruct((S, INNER), BF16),
        compiler_params=_cp(("parallel", "arbitrary")), name=name)(mix, xa, proj)


def _gate_bwd(dy, mix, xa, proj, z_off, *, name, tm=256):
    S = mix.shape[0]
    tm = min(tm, S)
    zb = z_off // GATE_TN

    def body(dy_ref, m_ref, x_ref, z_ref, dc_ref, dz_ref):
        z = z_ref[...]
        sg = _sigmoid(z)
        d = dy_ref[...]
        c = jnp.where(pl.program_id(1) < GATE_MIX_TILES, m_ref[...], x_ref[...])
        dc_ref[...] = d * z * sg
        dz_ref[...] = (d * c * sg * (1.0 + z * (1.0 - sg))).astype(BF16)

    blk = pl.BlockSpec((tm, GATE_TN), lambda i, j: (i, j))
    return pl.pallas_call(
        body, grid=(S // tm, INNER // GATE_TN),
        in_specs=[blk] + _gate_cat_specs(tm) + [pl.BlockSpec((tm, GATE_TN), lambda i, j: (i, zb + j))], out_specs=(blk, blk),
        out_shape=(jax.ShapeDtypeStruct((S, INNER), F32), jax.ShapeDtypeStruct((S, INNER), BF16)),
        compiler_params=_cp(("parallel", "arbitrary")), name=name)(dy, mix, xa, proj)


def _loss_head(x, target, *, name, tm=256):
    S, Dm = x.shape
    tm = min(tm, S)

    nt = S // tm

    def body(x_ref, t_ref, dx_ref, l_ref, acc):
        i = pl.program_id(0)
        e = x_ref[...] - t_ref[...]
        dx_ref[...] = e * (1.0 / Dm)

        @pl.when(i == 0)
        def _():
            acc[...] = jnp.zeros_like(acc)

        acc[...] += jnp.sum(e * e, axis=0, keepdims=True) * (0.5 / Dm)

        @pl.when(i == nt - 1)
        def _():
            l_ref[...] = jnp.sum(acc[...], axis=1, keepdims=True) + jnp.zeros((1, LANE), F32)

    row = pl.BlockSpec((tm, Dm), lambda i: (i, 0))
    return pl.pallas_call(
        body, grid=(nt,), in_specs=[row, row], out_specs=(row, pl.BlockSpec((1, LANE), lambda i: (0, 0))),
        out_shape=(jax.ShapeDtypeStruct((S, Dm), F32), jax.ShapeDtypeStruct((1, LANE), F32)),
        scratch_shapes=[pltpu.VMEM((1, Dm), F32)],
        compiler_params=_cp(("arbitrary",)), name=name)(x, target)


def _xa_norm(v, g):
    r = lax.rsqrt(jnp.mean(v * v, axis=-1, keepdims=True) + EPS)
    return v * r, r


def _xa_fwd(proj, xq_off, kv, gq, gk, *, name, tm=512):
    S = proj.shape[0]
    tm = min(tm, S)
    qb = xq_off // XA_DIM
    n_mem = kv.shape[0]
    scale = XA_DIM ** -0.5

    def body(q_ref, k_ref, v_ref, gq_ref, gk_ref, o_ref):
        qh, _ = _xa_norm(q_ref[...], None)
        kh, _ = _xa_norm(k_ref[...], None)
        qn = qh * gq_ref[...]
        kn = kh * gk_ref[...]
        s = _bdot(qn, kn, NT) * scale
        s = s - jnp.max(s, axis=-1, keepdims=True)
        p = jnp.exp(s)
        p = p / jnp.sum(p, axis=-1, keepdims=True)
        o_ref[...] = _bdot(p, v_ref[...], NN)

    vec = pl.BlockSpec((1, XA_DIM), lambda h, i: (0, 0))
    return pl.pallas_call(
        body, grid=(XA_HEADS, S // tm),
        in_specs=[pl.BlockSpec((tm, XA_DIM), lambda h, i: (i, qb + h)),
                  pl.BlockSpec((n_mem, XA_DIM), lambda h, i: (0, h)),
                  pl.BlockSpec((n_mem, XA_DIM), lambda h, i: (0, XA_HEADS + h)), vec, vec],
        out_specs=pl.BlockSpec((tm, XA_DIM), lambda h, i: (i, h)),
        out_shape=jax.ShapeDtypeStruct((S, XA_WIDTH), F32),
        compiler_params=_cp(("parallel", "parallel")), name=name)(proj, kv, kv, gq.reshape(1, XA_DIM), gk.reshape(1, XA_DIM))


def _xa_bwd(dcat, proj, xq_off, kv, gq, gk, *, name, tm=512):
    S = proj.shape[0]
    tm = min(tm, S)
    nt = S // tm
    qb = xq_off // XA_DIM
    db = MIX_WIDTH // XA_DIM
    n_mem = kv.shape[0]
    scale = XA_DIM ** -0.5

    def body(d_ref, q_ref, k_ref, v_ref, gq_ref, gk_ref, dq_ref, dk_ref, dv_ref, dgq_ref, dgk_ref, dkn_acc):
        h = pl.program_id(0)
        i = pl.program_id(1)
        q = q_ref[...]
        k = k_ref[...]
        qh, rq = _xa_norm(q, None)
        kh, rk = _xa_norm(k, None)
        gqv = gq_ref[...]
        gkv = gk_ref[...]
        qn = qh * gqv
        kn = kh * gkv
        s = _bdot(qn, kn, NT) * scale
        s = s - jnp.max(s, axis=-1, keepdims=True)
        p = jnp.exp(s)
        p = p / jnp.sum(p, axis=-1, keepdims=True)
        d = d_ref[...]
        dp = _bdot(d, v_ref[...], NT)
        ds = p * (dp - jnp.sum(dp * p, axis=-1, keepdims=True)) * scale
        dqn = _bdot(ds, kn, NN)

        @pl.when(i == 0)
        def _():
            dkn_acc[...] = jnp.zeros_like(dkn_acc)
            dv_ref[...] = jnp.zeros_like(dv_ref)

        @pl.when(jnp.logical_and(i == 0, h == 0))
        def _():
            dgq_ref[...] = jnp.zeros_like(dgq_ref)
            dgk_ref[...] = jnp.zeros_like(dgk_ref)

        dkn_acc[...] += _bdot(ds, qn, TN)
        dv_ref[...] += _bdot(p, d, TN)
        dgq_ref[...] += jnp.sum(dqn * qh, axis=0, keepdims=True)
        dy = dqn * gqv
        dq_ref[...] = (rq * (dy - qh * jnp.mean(dy * qh, axis=-1, keepdims=True))).astype(BF16)

        @pl.when(i == nt - 1)
        def _():
            dkn = dkn_acc[...]
            dgk_ref[...] += jnp.sum(dkn * kh, axis=0, keepdims=True)
            dyk = dkn * gkv
            dk_ref[...] = rk * (dyk - kh * jnp.mean(dyk * kh, axis=-1, keepdims=True))

    vec = pl.BlockSpec((1, XA_DIM), lambda h, i: (0, 0))
    kblk = pl.BlockSpec((n_mem, XA_DIM), lambda h, i: (0, h))
    vblk = pl.BlockSpec((n_mem, XA_DIM), lambda h, i: (0, XA_HEADS + h))
    dq, dk, dv, dgq, dgk = pl.pallas_call(
        body, grid=(XA_HEADS, nt),
        in_specs=[pl.BlockSpec((tm, XA_DIM), lambda h, i: (i, db + h)),
                  pl.BlockSpec((tm, XA_DIM), lambda h, i: (i, qb + h)), kblk, vblk, vec, vec],
        out_specs=(pl.BlockSpec((tm, XA_DIM), lambda h, i: (i, h)), kblk, kblk, vec, vec),
        out_shape=(jax.ShapeDtypeStruct((S, XA_WIDTH), BF16), jax.ShapeDtypeStruct((n_mem, XA_WIDTH), F32),
                   jax.ShapeDtypeStruct((n_mem, XA_WIDTH), F32), jax.ShapeDtypeStruct((1, XA_DIM), F32),
                   jax.ShapeDtypeStruct((1, XA_DIM), F32)),
        scratch_shapes=[pltpu.VMEM((n_mem, XA_DIM), F32)],
        compiler_params=_cp(("arbitrary", "arbitrary")), name=name)(
            dcat, proj, kv, kv, gq.reshape(1, XA_DIM), gk.reshape(1, XA_DIM))
    return dq, jnp.concatenate([dk, dv], axis=1), dgq, dgk


SB_TQ = 256
SB_TK = 256
SB_HEADS = 24


SB_PAIR = 2
SB_PW = SB_PAIR * HEAD_DIM


def _hdot(a, b, dims, dot=None):
    dot = dot or _bdot
    n = a.shape[0] if a.ndim == 3 else b.shape[0]
    return jnp.stack([dot(a[i] if a.ndim == 3 else a, b[i] if b.ndim == 3 else b, dims) for i in range(n)])


def _sb_tile(qi, kj, t0, s0, masked):
    z = _hdot(qi, kj, NT)
    sp = _softplus(z)
    ls = z - sp
    if not masked:
        return -sp, ls, None
    mask = (s0 + _iota2(z.shape[1:], 1)) < (t0 + _iota2(z.shape[1:], 0))
    return jnp.where(mask, -sp, 0.0), ls, mask


def _dot2(x, tri):
    hi = x.astype(BF16)
    lo = (x - hi.astype(F32)).astype(BF16)
    plain = lambda u, v, dims: lax.dot_general(u, v, (dims, ((), ())), preferred_element_type=F32)
    return _hdot(hi, tri, NN, plain) + _hdot(lo, tri, NN, plain)


def _sb_heads(ref, rows=slice(None)):
    return jnp.stack([ref[rows, hh * HEAD_DIM:(hh + 1) * HEAD_DIM] for hh in range(SB_PAIR)])


def _sb_fwd(proj, gq, gk, *, name):
    S = proj.shape[0]
    tq, tk = min(SB_TQ, S), min(SB_TK, S)
    nq = S // tq
    scale = HEAD_DIM ** -0.5

    def body(q_ref, k_ref, v_ref, gq_ref, gk_ref, o_ref, tot_ref, qn_s, kn_s, v_s):
        q = _sb_heads(q_ref)
        k = _sb_heads(k_ref)
        qn_s[...] = (q * lax.rsqrt(jnp.mean(q * q, axis=-1, keepdims=True) + EPS) * (gq_ref[...] * scale)).astype(BF16)
        kn_s[...] = (k * lax.rsqrt(jnp.mean(k * k, axis=-1, keepdims=True) + EPS) * gk_ref[...]).astype(BF16)
        v_s[...] = _sb_heads(v_ref).astype(BF16)
        after = (_iota2((tk, tk), 0) > _iota2((tk, tk), 1)).astype(BF16)

        def qblock(i, _):
            rows = pl.ds(pl.multiple_of(i * tq, tq), tq)
            qi = qn_s[:, rows, :]
            jd = (i * tq) // tk

            def tile(j, acc, run, masked):
                cols = pl.ds(pl.multiple_of(j * tk, tk), tk)
                lr, ls, mask = _sb_tile(qi, kn_s[:, cols, :], i * tq, j * tk, masked)
                later = _dot2(lr, after) + run
                a = jnp.exp(ls + later)
                if masked:
                    a = jnp.where(mask, a, 0.0)
                acc = acc + _hdot(a, v_s[:, cols, :], NN)
                return acc, run + jnp.sum(lr, axis=-1, keepdims=True)

            acc, run = tile(jd, jnp.zeros((SB_PAIR, tq, HEAD_DIM), F32), jnp.zeros((SB_PAIR, tq, 1), F32), True)
            acc, run = lax.fori_loop(0, jd, lambda jj, c: tile(jd - 1 - jj, c[0], c[1], False), (acc, run))
            tot = run + jnp.zeros((SB_PAIR, tq, HEAD_DIM), F32)
            for hh in range(SB_PAIR):
                o_ref[rows, hh * HEAD_DIM:(hh + 1) * HEAD_DIM] = acc[hh]
                tot_ref[rows, hh * HEAD_DIM:(hh + 1) * HEAD_DIM] = tot[hh]
            return 0

        lax.fori_loop(0, nq, qblock, 0)

    npair = SB_HEADS // SB_PAIR
    vec = pl.BlockSpec((1, HEAD_DIM), lambda h: (0, 0))
    hb = lambda off: pl.BlockSpec((S, SB_PW), lambda h: (0, off + h), pipeline_mode=pl.Buffered(1))
    return pl.pallas_call(
        body, grid=(npair,), in_specs=[hb(0), hb(npair), hb(2 * npair), vec, vec],
        out_specs=(hb(0), hb(0)), out_shape=(jax.ShapeDtypeStruct((S, MIX_WIDTH), F32),) * 2,
        scratch_shapes=[pltpu.VMEM((SB_PAIR, S, HEAD_DIM), BF16)] * 3,
        compiler_params=_cp(("parallel",)), name=name)(proj, proj, proj, gq.reshape(1, HEAD_DIM), gk.reshape(1, HEAD_DIM))


def _sb_bwd(dmix, tot, proj, gq, gk, *, name):
    S = proj.shape[0]
    tq, tk = min(SB_TQ, S), min(SB_TK, S)
    nq = S // tq
    scale = HEAD_DIM ** -0.5

    def body(do_ref, o_ref, q_ref, k_ref, v_ref, gq_ref, gk_ref, dq_ref, dk_ref, dv_ref, dgq_ref, dgk_ref,
             qn_s, kn_s, v_s, dkn_s, dqn_s, dv_s):
        h = pl.program_id(0)
        q = _sb_heads(q_ref)
        k = _sb_heads(k_ref)
        rq = lax.rsqrt(jnp.mean(q * q, axis=-1, keepdims=True) + EPS)
        rk = lax.rsqrt(jnp.mean(k * k, axis=-1, keepdims=True) + EPS)
        gqv = gq_ref[...]
        gkv = gk_ref[...]
        qn_s[...] = (q * rq * (gqv * scale)).astype(BF16)
        kn_s[...] = (k * rk * gkv).astype(BF16)
        v_s[...] = _sb_heads(v_ref).astype(BF16)
        dkn_s[...] = jnp.zeros_like(dkn_s)
        dv_s[...] = jnp.zeros_like(dv_s)
        r_i = _iota2((tk, tk), 0)
        c_i = _iota2((tk, tk), 1)
        upto = (r_i <= c_i).astype(BF16)
        before = (r_i < c_i).astype(BF16)

        def qblock(i, _):
            rows = pl.ds(pl.multiple_of(i * tq, tq), tq)
            qi = qn_s[:, rows, :]
            doi = _sb_heads(do_ref, rows).astype(BF16)
            tot_i = jnp.max(_sb_heads(o_ref, rows), axis=-1, keepdims=True)
            jd = (i * tq) // tk

            def tile(j, dqn, run, run_b, masked):
                cols = pl.ds(pl.multiple_of(j * tk, tk), tk)
                kj = kn_s[:, cols, :]
                lr, ls, mask = _sb_tile(qi, kj, i * tq, j * tk, masked)
                later = tot_i - (_dot2(lr, upto) + run)
                a = jnp.exp(ls + later)
                if masked:
                    a = jnp.where(mask, a, 0.0)
                b = _hdot(doi, v_s[:, cols, :], NT) * a
                cum = _dot2(b, before) + run_b
                beta = jnp.exp(ls)
                dz = b * (1.0 - beta) - cum * beta
                if masked:
                    dz = jnp.where(mask, dz, 0.0)
                dzb = dz.astype(BF16)
                dv_s[:, cols, :] += _hdot(a, doi, TN)
                dkn_s[:, cols, :] += _hdot(dzb, qi, TN)
                dqn = dqn + _hdot(dzb, kj, NN)
                return dqn, run + jnp.sum(lr, axis=-1, keepdims=True), run_b + jnp.sum(b, axis=-1, keepdims=True)

            zero1 = jnp.zeros((SB_PAIR, tq, 1), F32)
            carry = lax.fori_loop(0, jd, lambda j, c: tile(j, c[0], c[1], c[2], False),
                                  (jnp.zeros((SB_PAIR, tq, HEAD_DIM), F32), zero1, zero1))
            dqn, _, _ = tile(jd, carry[0], carry[1], carry[2], True)
            dqn_s[:, rows, :] = dqn * scale
            return 0

        lax.fori_loop(0, nq, qblock, 0)

        @pl.when(h == 0)
        def _():
            dgq_ref[...] = jnp.zeros_like(dgq_ref)
            dgk_ref[...] = jnp.zeros_like(dgk_ref)

        heads_sum = lambda z: jnp.sum(jnp.sum(z, axis=1, keepdims=True), axis=0)
        dqn = dqn_s[...]
        qh = q * rq
        dgq_ref[...] += heads_sum(dqn * qh)
        dy = dqn * gqv
        dq = (rq * (dy - qh * jnp.mean(dy * qh, axis=-1, keepdims=True))).astype(BF16)
        dkn = dkn_s[...]
        kh = k * rk
        dgk_ref[...] += heads_sum(dkn * kh)
        dyk = dkn * gkv
        dk = (rk * (dyk - kh * jnp.mean(dyk * kh, axis=-1, keepdims=True))).astype(BF16)
        dv = dv_s[...].astype(BF16)
        for hh in range(SB_PAIR):
            lanes = slice(hh * HEAD_DIM, (hh + 1) * HEAD_DIM)
            dq_ref[:, lanes] = dq[hh]
            dk_ref[:, lanes] = dk[hh]
            dv_ref[:, lanes] = dv[hh]

    npair = SB_HEADS // SB_PAIR
    vec = pl.BlockSpec((1, HEAD_DIM), lambda h: (0, 0))
    hb = lambda off: pl.BlockSpec((S, SB_PW), lambda h: (0, off + h), pipeline_mode=pl.Buffered(1))
    dq, dk, dv, dgq, dgk = pl.pallas_call(
        body, grid=(npair,),
        in_specs=[hb(0), hb(0), hb(0), hb(npair), hb(2 * npair), vec, vec],
        out_specs=(hb(0), hb(0), hb(0), vec, vec),
        out_shape=(jax.ShapeDtypeStruct((S, MIX_WIDTH), BF16),) * 3 + (jax.ShapeDtypeStruct((1, HEAD_DIM), F32),) * 2,
        scratch_shapes=[pltpu.VMEM((SB_PAIR, S, HEAD_DIM), BF16)] * 3 + [pltpu.VMEM((SB_PAIR, S, HEAD_DIM), F32)] * 3,
        compiler_params=_cp(("arbitrary",)), name=name)(
            dmix, tot, proj, proj, proj, gq.reshape(1, HEAD_DIM), gk.reshape(1, HEAD_DIM))
    return [dq, dk, dv], dgq, dgk


def _shift_down(x, k):
    if k == 0:
        return x
    r = pltpu.roll(x, k, 0)
    return jnp.where(_iota2(x.shape, 0) >= k, r, 0.0)


def _shift_up(x, k):
    if k == 0:
        return x
    n = x.shape[0]
    r = pltpu.roll(x, n - k, 0)
    return jnp.where(_iota2(x.shape, 0) < n - k, r, 0.0)


def _conv(x, w):
    c = w[DN_CONV - 1] * x
    for k in range(1, DN_CONV):
        c = c + w[DN_CONV - 1 - k] * _shift_down(x, k)
    return c


def _dn_pre_fwd(proj, conv_w, col0, ncols, *, l2, scale, name):
    S = proj.shape[0]
    cb = col0 // HEAD_DIM

    def body(x_ref, w_ref, o_ref):
        c = _conv(x_ref[...], [w_ref[k:k + 1, :] for k in range(DN_CONV)])
        a = c * _sigmoid(c)
        if l2:
            a = a * (lax.rsqrt(jnp.sum(a * a, axis=-1, keepdims=True) + EPS) * scale)
        o_ref[...] = a

    return pl.pallas_call(
        body, grid=(ncols // HEAD_DIM,),
        in_specs=[pl.BlockSpec((S, HEAD_DIM), lambda j: (0, cb + j)), pl.BlockSpec((DN_CONV, HEAD_DIM), lambda j: (0, cb + j))],
        out_specs=pl.BlockSpec((S, HEAD_DIM), lambda j: (0, j)), out_shape=jax.ShapeDtypeStruct((S, ncols), F32),
        compiler_params=_cp(("parallel",)), name=name)(proj, conv_w)


def _dn_pre_bwd(dout, proj, conv_w, col0, ncols, *, l2, scale, name):
    S = proj.shape[0]
    cb = col0 // HEAD_DIM
    dw_in = HEAD_DIM

    def body(d_ref, x_ref, w_ref, dx_ref, dw_ref):
        x = x_ref[...]
        w = [w_ref[k:k + 1, :] for k in range(DN_CONV)]
        c = _conv(x, w)
        sg = _sigmoid(c)
        a = c * sg
        d = d_ref[...]
        if l2:
            r = lax.rsqrt(jnp.sum(a * a, axis=-1, keepdims=True) + EPS)
            y = a * r
            d = d * scale
            d = r * (d - y * jnp.sum(d * y, axis=-1, keepdims=True))
        dc = d * sg * (1.0 + c * (1.0 - sg))
        dx = w[DN_CONV - 1] * dc
        for k in range(1, DN_CONV):
            dx = dx + w[DN_CONV - 1 - k] * _shift_up(dc, k)
        dx_ref[...] = dx.astype(BF16)
        for k in range(DN_CONV):
            dw_ref[3 - k:4 - k, :] = jnp.sum(dc * _shift_down(x, k), axis=0, keepdims=True)

    return pl.pallas_call(
        body, grid=(ncols // HEAD_DIM,),
        in_specs=[pl.BlockSpec((S, dw_in), lambda j: (0, j)), pl.BlockSpec((S, HEAD_DIM), lambda j: (0, cb + j)),
                  pl.BlockSpec((DN_CONV, HEAD_DIM), lambda j: (0, cb + j))],
        out_specs=(pl.BlockSpec((S, HEAD_DIM), lambda j: (0, j)), pl.BlockSpec((DN_CONV, HEAD_DIM), lambda j: (0, j))),
        out_shape=(jax.ShapeDtypeStruct((S, ncols), BF16), jax.ShapeDtypeStruct((DN_CONV, ncols), F32)),
        compiler_params=_cp(("parallel",)), name=name)(dout, proj, conv_w)


def _dn_ab_fwd(proj, a_log, dt_bias, *, name, tm=512):
    S = proj.shape[0]
    tm = min(tm, S)
    ab = P0_AB // LANE

    def body(a_ref, b_ref, al_ref, dt_ref, g_ref, be_ref):
        g_ref[...] = -jnp.exp(al_ref[...]) * _softplus(a_ref[...] + dt_ref[...])
        be_ref[...] = _sigmoid(b_ref[...])

    vec = pl.BlockSpec((1, LANE), lambda i: (0, 0))
    out = pl.BlockSpec((tm, LANE), lambda i: (i, 0))
    return pl.pallas_call(
        body, grid=(S // tm,),
        in_specs=[pl.BlockSpec((tm, LANE), lambda i: (i, ab)), pl.BlockSpec((tm, LANE), lambda i: (i, ab + 1)), vec, vec],
        out_specs=(out, out), out_shape=(jax.ShapeDtypeStruct((S, LANE), F32),) * 2,
        compiler_params=_cp(("parallel",)), name=name)(proj, proj, a_log, dt_bias)


def _dn_ab_bwd(dg, dbeta, proj, a_log, dt_bias, *, name, tm=512):
    S = proj.shape[0]
    tm = min(tm, S)
    ab = P0_AB // LANE

    def body(dg_ref, db_ref, a_ref, b_ref, al_ref, dt_ref, dab_ref, dal_ref, ddt_ref):
        i = pl.program_id(0)
        ea = jnp.exp(al_ref[...])
        u = a_ref[...] + dt_ref[...]
        dgv = dg_ref[...]
        da = dgv * (-ea) * _sigmoid(u)
        be = _sigmoid(b_ref[...])
        dab_ref[:, 0:LANE] = da.astype(BF16)
        dab_ref[:, LANE:2 * LANE] = (db_ref[...] * be * (1.0 - be)).astype(BF16)
        dab_ref[:, 2 * LANE:] = jnp.zeros((tm, 2 * LANE), BF16)

        @pl.when(i == 0)
        def _():
            dal_ref[...] = jnp.zeros_like(dal_ref)
            ddt_ref[...] = jnp.zeros_like(ddt_ref)

        dal_ref[...] += jnp.sum(dgv * (-ea) * _softplus(u), axis=0, keepdims=True)
        ddt_ref[...] += jnp.sum(da, axis=0, keepdims=True)

    vec = pl.BlockSpec((1, LANE), lambda i: (0, 0))
    row = pl.BlockSpec((tm, LANE), lambda i: (i, 0))
    return pl.pallas_call(
        body, grid=(S // tm,),
        in_specs=[row, row, pl.BlockSpec((tm, LANE), lambda i: (i, ab)), pl.BlockSpec((tm, LANE), lambda i: (i, ab + 1)), vec, vec],
        out_specs=(pl.BlockSpec((tm, 4 * LANE), lambda i: (i, 0)), vec, vec),
        out_shape=(jax.ShapeDtypeStruct((S, 4 * LANE), BF16), jax.ShapeDtypeStruct((1, LANE), F32),
                   jax.ShapeDtypeStruct((1, LANE), F32)),
        compiler_params=_cp(("arbitrary",)), name=name)(dg, dbeta, proj, proj, a_log, dt_bias)


def _dot3(a, b):
    ah = a.astype(BF16)
    al = (a - ah.astype(F32)).astype(BF16)
    bh = b.astype(BF16)
    bl = (b - bh.astype(F32)).astype(BF16)
    d = lambda u, v: lax.dot_general(u, v, (NN, ((), ())), preferred_element_type=F32)
    return d(ah, bh) + (d(ah, bl) + d(al, bh))


DN_PAIR = 2


def _pdot(a, b, dims, dot=None):
    dot = dot or _bdot
    return jnp.stack([dot(a[i] if a.ndim == 3 else a, b[i] if b.ndim == 3 else b, dims) for i in range(DN_PAIR)])


def _tri_inverse(a):
    eye = (_iota2((CH, CH), 0) == _iota2((CH, CH), 1)).astype(F32)
    d3 = lambda u, v: jnp.stack([_dot3(u[i], v[i]) for i in range(DN_PAIR)])
    t = eye - a
    x = d3(a, a)
    n = 2
    while True:
        t = t + d3(t, x)
        n *= 2
        if n >= CH:
            break
        x = d3(x, x)
    return t


def _pick_col(m, n):
    return jnp.sum(jnp.where(_iota2(m.shape, 2) == n, m, 0.0), axis=2, keepdims=True)


def _dn_chunk_common(kk, qk, gc_c, gc_r, be_c):
    r_i = _iota2((CH, CH), 0)
    c_i = _iota2((CH, CH), 1)
    incl = r_i >= c_i
    strict = r_i > c_i
    dec = jnp.exp(jnp.where(incl, gc_c - gc_r, -1e30))
    e = jnp.exp(gc_c)
    gl = jnp.sum(jnp.where(_iota2((1, CH), 1) == CH - 1, gc_r, 0.0), axis=-1, keepdims=True)
    kds = jnp.exp(gl - gc_c)
    cd = jnp.exp(gl)
    a = jnp.where(strict, be_c * kk * dec, 0.0)
    p = qk * dec
    return dict(incl=incl, strict=strict, dec=dec, e=e, kds=kds, cd=cd, kk=kk, a=a, qk=qk, p=p)


def _dn_decay_tables(g_ref, b_ref, gcr, gcc, bcc):
    r_i = _iota2((CH, CH), 0)
    c_i = _iota2((CH, CH), 1)
    lc = (r_i >= c_i).astype(F32)
    eye = (r_i == c_i).astype(F32)
    for hh in range(DN_PAIR):
        g_rows_v = g_ref[hh]
        gcr[hh] = _fdot(g_rows_v, lc, NT)
        gcc[hh] = _fdot(lc, g_rows_v, NT)
        bcc[hh] = _fdot(eye, b_ref[hh], NT)
    return lc


def _dn_core_fwd(qn, kn, vc, g_rows, b_rows, out_g, *, name):
    S = qn.shape[0]
    nc = S // CH

    def body(q_ref, k_ref, v_ref, g_ref, b_ref, og_ref, o_ref, st_ref, t_ref, gcr, gcc, bcc):
        _dn_decay_tables(g_ref, b_ref, gcr, gcc, bcc)
        ogv = og_ref[...]

        def chunk(n, states):
            rows = pl.ds(pl.multiple_of(n * CH, CH), CH)
            q = q_ref[rows, :]
            k = k_ref[rows, :]
            kk = _bdot(k, k, NT)
            qk = _bdot(q, k, NT)
            v = jnp.stack([v_ref[rows, hh * HEAD_DIM:(hh + 1) * HEAD_DIM] for hh in range(DN_PAIR)])
            gc_c = _pick_col(gcc[...], n)
            be_c = _pick_col(bcc[...], n)
            gc_r = gcr[:, pl.ds(n, 1), :]
            c = _dn_chunk_common(kk, qk, gc_c, gc_r, be_c)
            t = _tri_inverse(c["a"])
            u0 = _pdot(t, be_c * v, NN)
            w = _pdot(t, (be_c * c["e"]) * k, NN)
            u = u0 - _pdot(w, states, NN)
            o = _pdot(c["e"] * q, states, NN) + _pdot(c["p"], u, NN)
            on = o * lax.rsqrt(jnp.mean(o * o, axis=-1, keepdims=True) + EPS) * ogv
            for hh in range(DN_PAIR):
                st_ref[hh, n] = states[hh]
                t_ref[hh, n] = t[hh]
                o_ref[rows, hh * HEAD_DIM:(hh + 1) * HEAD_DIM] = on[hh]
            return c["cd"] * states + _pdot(c["kds"] * k, u, TN)

        lax.fori_loop(0, nc, chunk, jnp.zeros((DN_PAIR, HEAD_DIM, HEAD_DIM), F32))

    qk_spec = pl.BlockSpec((S, HEAD_DIM), lambda h: (0, h))
    v_spec = pl.BlockSpec((S, DN_PAIR * HEAD_DIM), lambda h: (0, h))
    rows_spec = pl.BlockSpec((DN_PAIR, LANE, CH), lambda h: (h, 0, 0))
    return pl.pallas_call(
        body, grid=(DN_V_HEADS // DN_PAIR,),
        in_specs=[qk_spec, qk_spec, v_spec, rows_spec, rows_spec, pl.BlockSpec((1, HEAD_DIM), lambda h: (0, 0))],
        out_specs=(v_spec, pl.BlockSpec((DN_PAIR, nc, HEAD_DIM, HEAD_DIM), lambda h: (h, 0, 0, 0)),
                   pl.BlockSpec((DN_PAIR, nc, CH, CH), lambda h: (h, 0, 0, 0))),
        out_shape=(jax.ShapeDtypeStruct((S, MIX_WIDTH), F32), jax.ShapeDtypeStruct((DN_V_HEADS, nc, HEAD_DIM, HEAD_DIM), F32),
                   jax.ShapeDtypeStruct((DN_V_HEADS, nc, CH, CH), F32)),
        scratch_shapes=[pltpu.VMEM((DN_PAIR, LANE, CH), F32), pltpu.VMEM((DN_PAIR, CH, LANE), F32),
                        pltpu.VMEM((DN_PAIR, CH, LANE), F32)],
        compiler_params=_cp(("parallel",)), name=name)(qn, kn, vc, g_rows, b_rows, out_g.reshape(1, HEAD_DIM))


def _dn_chunk_bwd(q, k, v, kk, qk, state, t, gc_c, gc_r, be_c, don, ogv, ds_next):
    ones = jnp.ones((CH, LANE), F32)
    last_row = _iota2((CH, 1), 0) == CH - 1
    rowsum = lambda z: jnp.sum(z, axis=-1, keepdims=True)
    colsum = lambda z: jnp.sum(z, axis=-2, keepdims=True)
    c = _dn_chunk_common(kk, qk, gc_c, gc_r, be_c)
    e, kds, cd, dec, a, p = c["e"], c["kds"], c["cd"], c["dec"], c["a"], c["p"]
    vb = be_c * v
    kbe = (be_c * e) * k
    u0 = _pdot(t, vb, NN)
    w = _pdot(t, kbe, NN)
    u = u0 - _pdot(w, state, NN)
    qd = e * q
    kd = kds * k
    o = _pdot(qd, state, NN) + _pdot(p, u, NN)
    r = lax.rsqrt(jnp.mean(o * o, axis=-1, keepdims=True) + EPS)
    y = o * r
    dog = colsum(don * y)
    dy = don * ogv
    d_o = r * (dy - y * jnp.mean(dy * y, axis=-1, keepdims=True))
    du = _pdot(p, d_o, TN) + _pdot(kd, ds_next, NN)
    dqd = _pdot(d_o, state, NT)
    dstate = _pdot(qd, d_o, TN) + cd * ds_next - _pdot(w, du, TN)
    dcd = colsum(rowsum(ds_next * state))
    dkd = _pdot(u, ds_next, NT)
    dw = -_pdot(du, state, NT)
    dvb = _pdot(t, du, TN)
    dkbe = _pdot(t, dw, TN)
    da = -jnp.where(c["strict"], _pdot(dvb, u0, NT) + _pdot(dkbe, w, NT), 0.0)
    dp = jnp.where(c["incl"], _pdot(d_o, u, NT), 0.0)
    gmat = da * a + dp * p
    dad = da * dec
    x = be_c * dad
    dpd = dp * dec
    dk = _pdot(x, k, NN) + _pdot(x, k, TN) + _pdot(dpd, q, TN)
    dq = _pdot(dpd, k, NN) + e * dqd
    dbe = rowsum(dad * c["kk"])
    dgc = rowsum(gmat) + rowsum(dqd * q) * e
    rk = rowsum(dkd * k) * kds
    dk = dk + kds * dkd
    dgc = dgc - rk
    dgl = colsum(rk) + dcd * cd
    sk = rowsum(dkbe * k)
    dk = dk + (be_c * e) * dkbe
    dbe = dbe + sk * e + rowsum(dvb * v)
    dgc = dgc + sk * be_c * e
    dgc = dgc + jnp.where(last_row, dgl, 0.0)
    dgc = dgc - _pdot(gmat, ones, TN, dot=_fdot)
    return dq, dk, be_c * dvb, dgc, dbe, dog, dstate


def _dn_core_bwd(dmix, qn, kn, vc, g_rows, b_rows, out_g, states, tinv, *, name):
    S = qn.shape[0]
    nc = S // CH

    def body(do_ref, q_ref, k_ref, v_ref, g_ref, b_ref, og_ref, st_ref, t_ref,
             dq_ref, dk_ref, dv_ref, dg_ref, db_ref, dog_ref, gcr, gcc, bcc, dgc_acc):
        h = pl.program_id(0)
        lc = _dn_decay_tables(g_ref, b_ref, gcr, gcc, bcc)
        ogv = og_ref[...]
        dgc_acc[...] = jnp.zeros_like(dgc_acc)
        db_ref[...] = jnp.zeros_like(db_ref)
        lane_n = _iota2((CH, LANE), 1)

        @pl.when(h == 0)
        def _():
            dog_ref[...] = jnp.zeros_like(dog_ref)

        def chunk(m, carry):
            ds_nexts, dog = carry
            n = nc - 1 - m
            rows = pl.ds(pl.multiple_of(n * CH, CH), CH)
            q = q_ref[rows, :]
            k = k_ref[rows, :]
            kk = _bdot(k, k, NT)
            qk = _bdot(q, k, NT)
            heads = lambda ref: jnp.stack([ref[rows, hh * HEAD_DIM:(hh + 1) * HEAD_DIM] for hh in range(DN_PAIR)])
            state = jnp.stack([st_ref[hh, n] for hh in range(DN_PAIR)])
            t = jnp.stack([t_ref[hh, n] for hh in range(DN_PAIR)])
            dq, dk, dv, dgc, dbe, dog_h, dstate = _dn_chunk_bwd(
                q, k, heads(v_ref), kk, qk, state, t, _pick_col(gcc[...], n), gcr[:, pl.ds(n, 1), :],
                _pick_col(bcc[...], n), heads(do_ref), ogv, ds_nexts)
            for hh in range(DN_PAIR):
                dv_ref[rows, hh * HEAD_DIM:(hh + 1) * HEAD_DIM] = dv[hh]
            dgc_acc[...] = jnp.where(lane_n == n, dgc, dgc_acc[...])
            db_ref[...] = jnp.where(lane_n == n, dbe, db_ref[...])
            dq_ref[rows, :] = jnp.sum(dq, axis=0)
            dk_ref[rows, :] = jnp.sum(dk, axis=0)
            return dstate, dog + jnp.sum(dog_h, axis=0)

        _, dog = lax.fori_loop(0, nc, chunk, (jnp.zeros((DN_PAIR, HEAD_DIM, HEAD_DIM), F32), jnp.zeros((1, HEAD_DIM), F32)))
        dog_ref[...] += dog
        for hh in range(DN_PAIR):
            dg_ref[hh] = _fdot(lc, dgc_acc[hh], TN)

    qk_spec = pl.BlockSpec((S, HEAD_DIM), lambda h: (0, h))
    v_spec = pl.BlockSpec((S, DN_PAIR * HEAD_DIM), lambda h: (0, h))
    rows_spec = pl.BlockSpec((DN_PAIR, LANE, CH), lambda h: (h, 0, 0))
    cols_spec = pl.BlockSpec((DN_PAIR, CH, LANE), lambda h: (h, 0, 0))
    vec = pl.BlockSpec((1, HEAD_DIM), lambda h: (0, 0))
    qk_out = jax.ShapeDtypeStruct((S, DN_QK_WIDTH), F32)
    return pl.pallas_call(
        body, grid=(DN_V_HEADS // DN_PAIR,),
        in_specs=[v_spec, qk_spec, qk_spec, v_spec, rows_spec, rows_spec, vec,
                  pl.BlockSpec((DN_PAIR, nc, HEAD_DIM, HEAD_DIM), lambda h: (h, 0, 0, 0)),
                  pl.BlockSpec((DN_PAIR, nc, CH, CH), lambda h: (h, 0, 0, 0))],
        out_specs=(qk_spec, qk_spec, v_spec, cols_spec, cols_spec, vec),
        out_shape=(qk_out, qk_out, jax.ShapeDtypeStruct((S, MIX_WIDTH), F32), jax.ShapeDtypeStruct((DN_V_HEADS, CH, LANE), F32),
                   jax.ShapeDtypeStruct((DN_V_HEADS, CH, LANE), F32), jax.ShapeDtypeStruct((1, HEAD_DIM), F32)),
        scratch_shapes=[pltpu.VMEM((DN_PAIR, LANE, CH), F32), pltpu.VMEM((DN_PAIR, CH, LANE), F32),
                        pltpu.VMEM((DN_PAIR, CH, LANE), F32), pltpu.VMEM((DN_PAIR, CH, LANE), F32)],
        compiler_params=_cp(("arbitrary",)), name=name)(
            dmix, qn, kn, vc, g_rows, b_rows, out_g.reshape(1, HEAD_DIM), states, tinv)


def _rows_form(x, nc):
    t = x[:, :DN_V_HEADS].T.reshape(DN_V_HEADS, nc, CH)
    return jnp.pad(t, ((0, 0), (0, LANE - nc), (0, 0)))


def _cols_to_nat(x, nc):
    t = jnp.transpose(x[:, :, :nc], (2, 1, 0)).reshape(nc * CH, DN_V_HEADS)
    return jnp.pad(t, ((0, 0), (0, LANE - DN_V_HEADS)))


_C_QKV = 2 * DN_QK_WIDTH + MIX_WIDTH


def _w0_to_padded(w):
    rows = w.shape[0]
    z = lambda n: jnp.zeros((rows, n), w.dtype)
    a = w[:, _C_QKV:_C_QKV + DN_V_HEADS]
    b = w[:, _C_QKV + DN_V_HEADS:_C_QKV + 2 * DN_V_HEADS]
    return jnp.concatenate([w[:, :_C_QKV], w[:, _C_QKV + 2 * DN_V_HEADS:], a, z(LANE - DN_V_HEADS), b,
                            z(P0 - P0_AB - LANE - DN_V_HEADS)], axis=1)


def _w0_from_padded(g):
    return jnp.concatenate([g[:, :_C_QKV], g[:, P0_AB:P0_AB + DN_V_HEADS], g[:, P0_AB + LANE:P0_AB + LANE + DN_V_HEADS],
                            g[:, _C_QKV:P0_AB]], axis=1)


def _true_pieces(lo, hi):
    out = []
    while lo < hi:
        s = lo // P0_SHARD
        end = min(hi, (s + 1) * P0_SHARD)
        out.append((s, lo - s * P0_SHARD, end - s * P0_SHARD))
        lo = end
    return out


def _padded_pieces(lo, hi):
    a0, b0, x0 = _C_QKV, _C_QKV + DN_V_HEADS, _C_QKV + 2 * DN_V_HEADS
    out = []
    for t0, t1, shift in ((0, a0, 0), (a0, b0, P0_AB - a0), (b0, x0, P0_AB + LANE - b0), (x0, DN_PROJ, a0 - x0)):
        s, e = max(lo, t0), min(hi, t1)
        if s < e:
            out.append((s + shift, e + shift))
    return out


def _pad_lane(v):
    v = v.reshape(1, -1)
    return jnp.pad(v, ((0, 0), (0, LANE - v.shape[1])))


SLOT1 = SB_PROJ // N_CHIPS
MM_TN = 512


def _local_step(x, mem, target, norm_g, mem_norm_g, xa_q_g, xa_k_g, w_in0, conv_w, a_log, dt_bias, out_g, sb_q_g, sb_k_g,
                late_weights, early_grads):
    S = x.shape[0]
    nc = S // CH
    al = _pad_lane(a_log)
    dtb = _pad_lane(dt_bias)
    q_scale = HEAD_DIM ** -0.5
    tiles1 = SLOT1 // MM_TN

    kv_rhs = lambda l: pl.BlockSpec((N_CHIPS, None, D_MODEL // N_CHIPS, MM_TN), lambda i, j: (0, l, 0, j))
    kv_rhs_t = lambda l: pl.BlockSpec((None, None, D_MODEL // N_CHIPS, 2 * XA_WIDTH), lambda i, j: (j, l, 0, 0))
    out_rhs = lambda l: pl.BlockSpec((N_CHIPS, None, INNER // N_CHIPS, MM_TN), lambda i, j: (0, l, 0, j))
    out_rhs_t = lambda l: pl.BlockSpec((None, None, MM_TN, D_MODEL), lambda i, j: (j // 2, l, j % 2, 0))
    in1_rhs = pl.BlockSpec((None, 2, D_MODEL // 2, MM_TN), lambda i, j: (j // tiles1, 0, 0, j % tiles1))
    in1_rhs_t = pl.BlockSpec((None, None, D_MODEL // 2, MM_TN), lambda i, j, k: (k // tiles1, j, 0, k % tiles1))
    slot_rows = lambda rows: dict(
        tm=rows, o_spec=pl.BlockSpec((None, rows, MM_TN), lambda i, j: (i, 0, j)),
        o_shape=jax.ShapeDtypeStruct((N_CHIPS, rows, 2 * XA_WIDTH), BF16))
    in1_out = dict(tm=D_MODEL // 2, o_spec=pl.BlockSpec((None, None, D_MODEL // 2, MM_TN),
                                                        lambda i, j: (j // tiles1, i, 0, j % tiles1)),
                   o_shape=jax.ShapeDtypeStruct((N_CHIPS, 2, D_MODEL // 2, SLOT1), BF16))

    h0 = _rmsnorm_fwd(x, norm_g[0], name="norm0")
    proj0 = _matmul(h0, w_in0, name="proj0")
    qn = _dn_pre_fwd(proj0, conv_w, 0, DN_QK_WIDTH, l2=True, scale=q_scale, name="dn_pre_q")
    kn = _dn_pre_fwd(proj0, conv_w, DN_QK_WIDTH, DN_QK_WIDTH, l2=True, scale=1.0, name="dn_pre_k")
    vc = _dn_pre_fwd(proj0, conv_w, 2 * DN_QK_WIDTH, MIX_WIDTH, l2=False, scale=1.0, name="dn_pre_v")
    g_nat, b_nat = _dn_ab_fwd(proj0, al, dtb, name="dn_ab")
    g_rows = _rows_form(g_nat, nc)
    b_rows = _rows_form(b_nat, nc)
    mix0, states, tinv = _dn_core_fwd(qn, kn, vc, g_rows, b_rows, out_g, name="dn_core")
    w_kv, w_out, w_in1 = late_weights(mix0)
    mem_n = _rmsnorm_fwd(mem, mem_norm_g, name="mem_norm")
    kv = [_matmul(mem_n, w_kv, n=2 * XA_WIDTH, tn=MM_TN, b_spec=kv_rhs(l), name=f"kv{l}") for l in range(2)]
    xa0 = _xa_fwd(proj0, P0_XQ, kv[0], xa_q_g[0], xa_k_g[0], name="xa0")
    y0 = _gate_fwd(mix0, xa0, proj0, P0_Z, name="gate0")
    x1 = _matmul(y0, w_out, n=D_MODEL, tn=MM_TN, b_spec=out_rhs(0), res=x, name="out0")

    h1 = _rmsnorm_fwd(x1, norm_g[1], name="norm1")
    proj1 = _matmul(h1, w_in1, n=SB_PROJ, tn=MM_TN, b_spec=in1_rhs, name="proj1")
    mix1, tot1 = _sb_fwd(proj1, sb_q_g, sb_k_g, name="sb")
    xa1 = _xa_fwd(proj1, P1_XQ, kv[1], xa_q_g[1], xa_k_g[1], name="xa1")
    y1 = _gate_fwd(mix1, xa1, proj1, P1_Z, name="gate1")
    x2 = _matmul(y1, w_out, n=D_MODEL, tn=MM_TN, b_spec=out_rhs(1), res=x1, name="out1")

    dx2, loss_vec = _loss_head(x2, target, name="loss")

    d_wout1 = _matmul(y1, dx2, ta=True, name="d_wout1", **slot_rows(INNER // N_CHIPS))
    dy1 = _matmul(dx2, w_out, tb=True, n=INNER, tn=MM_TN, b_spec=out_rhs_t(1), name="dy1")
    dcat1, dz1 = _gate_bwd(dy1, mix1, xa1, proj1, P1_Z, name="gate1_bwd")
    dqkv1, d_sbq, d_sbk = _sb_bwd(dcat1, tot1, proj1, sb_q_g, sb_k_g, name="sb_bwd")
    dxq1, dkv1, d_xaq1, d_xak1 = _xa_bwd(dcat1, proj1, P1_XQ, kv[1], xa_q_g[1], xa_k_g[1], name="xa1_bwd")
    dproj1 = dqkv1 + [dxq1, dz1]
    d_win1 = _matmul(h1, dproj1, ta=True, name="d_win1", **in1_out)
    d_wkv1 = _matmul(mem_n, dkv1, ta=True, name="d_wkv1", **slot_rows(D_MODEL // N_CHIPS))
    token = early_grads(1, d_win1, d_wout1, d_wkv1)
    dh1 = _matmul(dproj1, w_in1, tb=True, n=D_MODEL, tn=D_MODEL // 2, tk=MM_TN, b_spec=in1_rhs_t, name="dh1")
    dx1, d_ng1 = _rmsnorm_bwd(dh1, x1, norm_g[1] + token[0, 0], dx2, name="norm1_bwd")

    d_wout0 = _matmul(y0, dx1, ta=True, name="d_wout0", **slot_rows(INNER // N_CHIPS))
    dy0 = _matmul(dx1, w_out, tb=True, n=INNER, tn=MM_TN, b_spec=out_rhs_t(0), name="dy0")
    dcat0, dz0 = _gate_bwd(dy0, mix0, xa0, proj0, P0_Z, name="gate0_bwd")
    dqv, dkv_h, dvc, dg_cols, db_cols, d_outg = _dn_core_bwd(
        dcat0, qn, kn, vc, g_rows, b_rows, out_g, states, tinv, name="dn_core_bwd")
    dpq, dwq = _dn_pre_bwd(dqv, proj0, conv_w, 0, DN_QK_WIDTH, l2=True, scale=q_scale, name="dn_pre_q_bwd")
    dpk, dwk = _dn_pre_bwd(dkv_h, proj0, conv_w, DN_QK_WIDTH, DN_QK_WIDTH, l2=True, scale=1.0, name="dn_pre_k_bwd")
    dpv, dwv = _dn_pre_bwd(dvc, proj0, conv_w, 2 * DN_QK_WIDTH, MIX_WIDTH, l2=False, scale=1.0, name="dn_pre_v_bwd")
    dab, d_alog, d_dt = _dn_ab_bwd(_cols_to_nat(dg_cols, nc), _cols_to_nat(db_cols, nc), proj0, al, dtb, name="dn_ab_bwd")
    dxq0, dkv0, d_xaq0, d_xak0 = _xa_bwd(dcat0, proj0, P0_XQ, kv[0], xa_q_g[0], xa_k_g[0], name="xa0_bwd")
    d_win0 = _matmul(h0, [dpq, dpk, dpv, dxq0, dz0, dab], ta=True, out_dtype=BF16, name="d_win0")
    d_wkv0 = _matmul(mem_n, dkv0, ta=True, name="d_wkv0", **slot_rows(D_MODEL // N_CHIPS))
    token = early_grads(0, d_win0, d_wout0, d_wkv0)
    zero = token[0, 0]
    dh0 = _matmul([dpq, dpk, dpv, dxq0, dz0, dab + zero.astype(BF16)], w_in0, tb=True, tk=MM_TN, name="dh0")
    dx0, d_ng0 = _rmsnorm_bwd(dh0, x, norm_g[0] + zero, dx1, name="norm0_bwd")

    dmem0 = _matmul(dkv0, w_kv, tb=True, n=D_MODEL, tn=D_MODEL // N_CHIPS, b_spec=kv_rhs_t(0), name="dmem0")
    dmem_n = _matmul(dkv1, w_kv, tb=True, n=D_MODEL, tn=D_MODEL // N_CHIPS, b_spec=kv_rhs_t(1), res=dmem0, name="dmem1")
    _, d_memg = _rmsnorm_bwd(dmem_n, mem, mem_norm_g, None, name="mem_norm_bwd")

    grads = dict(
        norm_g=jnp.concatenate([d_ng0, d_ng1], axis=0), mem_norm_g=d_memg.reshape(-1),
        xa_q_norm_g=jnp.concatenate([d_xaq0, d_xaq1], axis=0), xa_k_norm_g=jnp.concatenate([d_xak0, d_xak1], axis=0),
        dn_conv_w=jnp.concatenate([dwq, dwk, dwv], axis=1),
        dn_a_log=d_alog[:, :DN_V_HEADS], dn_dt_bias=d_dt[:, :DN_V_HEADS], dn_out_norm_g=d_outg,
        sb_q_norm_g=d_sbq, sb_k_norm_g=d_sbk)
    return loss_vec, dx0, grads


ANY = pl.BlockSpec(memory_space=pl.ANY)


def _place():
    x, y, c = lax.axis_index("x"), lax.axis_index("y"), lax.axis_index("c")
    chips = [(1 - x, y), (x, 1 - y), (1 - x, 1 - y)]
    return x, y, c, 2 * x + y, (x, y, 1 - c), chips


def _rcopy(src, dst, send, recv, i, dev):
    return pltpu.make_async_remote_copy(src_ref=src, dst_ref=dst, send_sem=send.at[i], recv_sem=recv.at[i],
                                        device_id=dev, device_id_type=MESH)


def _gather_weights(srcs, *, name):
    nt = len(srcs)

    def body(*refs):
        src, dst = refs[:nt], refs[nt:2 * nt]
        send, recv = refs[2 * nt:]
        x, y, c, j, sib, chips = _place()
        sends = []
        for t in range(nt):
            for k, (cx, cy) in enumerate(chips):
                sends.append(_rcopy(src[t].at[c], dst[t].at[j, c], send, recv, 6 * t + k, (cx, cy, c)))
                sends[-1].start()
        for t in range(nt):
            for k, (cx, cy) in enumerate(chips):
                landed = dst[t].at[2 * cx + cy, c]
                _rcopy(landed, landed, send, recv, 6 * t + k, (cx, cy, c)).wait_recv()
                sends.append(_rcopy(landed, landed, send, recv, 6 * t + 3 + k, sib))
                sends[-1].start()
        for t in range(nt):
            for k, (cx, cy) in enumerate(chips):
                other = dst[t].at[2 * cx + cy, 1 - c]
                _rcopy(other, other, send, recv, 6 * t + 3 + k, sib).wait_recv()
        for cp in sends:
            cp.wait_send()

    return pl.pallas_call(
        body, in_specs=[ANY] * nt, out_specs=[ANY] * nt,
        out_shape=[jax.ShapeDtypeStruct((N_CHIPS,) + s.shape, s.dtype) for s in srcs],
        scratch_shapes=[pltpu.SemaphoreType.DMA((6 * nt,)), pltpu.SemaphoreType.DMA((6 * nt,))],
        name=name)(*srcs)


def _swap_halves(xs, *, name):
    nt = len(xs)

    def body(*refs):
        src, dst = refs[:nt], refs[nt:2 * nt]
        send, recv = refs[2 * nt:]
        x, y, c, j, sib, chips = _place()
        cps = []
        for t in range(nt):
            for s in range(N_CHIPS):
                cps.append(_rcopy(src[t].at[s, 1 - c], dst[t].at[s], send, recv, 4 * t + s, sib))
                cps[-1].start()
        for cp in cps:
            cp.wait_recv()
        for cp in cps:
            cp.wait_send()

    return pl.pallas_call(
        body, in_specs=[ANY] * nt, out_specs=[ANY] * nt,
        out_shape=[jax.ShapeDtypeStruct((N_CHIPS,) + a.shape[2:], a.dtype) for a in xs],
        scratch_shapes=[pltpu.SemaphoreType.DMA((4 * nt,)), pltpu.SemaphoreType.DMA((4 * nt,))], name=name)(*xs)


def _scatter_to_chips(ps, *, name):
    nt = len(ps)

    def body(*refs):
        src, dst = refs[:nt], refs[nt:2 * nt]
        send, recv = refs[2 * nt:]
        x, y, c, j, sib, chips = _place()
        cps = []
        for t in range(nt):
            for k, (cx, cy) in enumerate(chips):
                cps.append(_rcopy(src[t].at[2 * cx + cy], dst[t].at[k], send, recv, 3 * t + k, (cx, cy, c)))
                cps[-1].start()
        for cp in cps:
            cp.wait_recv()
        for cp in cps:
            cp.wait_send()

    return pl.pallas_call(
        body, in_specs=[ANY] * nt, out_specs=[ANY] * nt,
        out_shape=[jax.ShapeDtypeStruct((3,) + a.shape[1:], a.dtype) for a in ps],
        scratch_shapes=[pltpu.SemaphoreType.DMA((3 * nt,)), pltpu.SemaphoreType.DMA((3 * nt,))], name=name)(*ps)


def _swap_with_sibling(fs, *, name):
    nt = len(fs)

    def body(*refs):
        src, dst = refs[:nt], refs[nt:2 * nt]
        send, recv = refs[2 * nt:]
        x, y, c, j, sib, chips = _place()
        cps = [_rcopy(src[t], dst[t], send, recv, t, sib) for t in range(nt)]
        for cp in cps:
            cp.start()
        for cp in cps:
            cp.wait_recv()
        for cp in cps:
            cp.wait_send()

    return pl.pallas_call(
        body, in_specs=[ANY] * nt, out_specs=[ANY] * nt,
        out_shape=[jax.ShapeDtypeStruct(a.shape, a.dtype) for a in fs],
        scratch_shapes=[pltpu.SemaphoreType.DMA((nt,)), pltpu.SemaphoreType.DMA((nt,))], name=name)(*fs)


HBM_SPEC = pl.BlockSpec(memory_space=pltpu.HBM)
SEM_SPEC = pl.BlockSpec(memory_space=pltpu.SEMAPHORE)
SIDE_EFFECT = pltpu.SideEffectType.DATAFLOW_SIDE_EFFECTING


def _gather_plan(src, land):
    x, y, c, j, sib, chips = _place()
    return [(src[t].at[c], land[t].at[j, c], (cx, cy, c), land[t].at[2 * cx + cy, c])
            for t in range(len(src)) for cx, cy in chips]


def _scatter_plan(src, land):
    x, y, c, j, sib, chips = _place()
    return [(src[t].at[2 * cx + cy], land[t].at[k], (cx, cy, c), land[t].at[k])
            for t in range(len(src)) for k, (cx, cy) in enumerate(chips)]


def _exchange_start(srcs, lands, plan, *, name):
    ns, nb = len(srcs), len(srcs) + len(lands)
    n = 3 * ns

    def body(*refs):
        send, recv, token = refs[nb], refs[nb + 1], refs[-1]
        for i, (s, d, dev, _) in enumerate(plan(refs[:ns], refs[ns:nb])):
            _rcopy(s, d, send, recv, i, dev).start()
        token[...] = jnp.zeros_like(token)

    bufs = list(srcs) + list(lands)
    outs = pl.pallas_call(
        body, name=name,
        out_shape=(pltpu.SemaphoreType.DMA((n,)), pltpu.SemaphoreType.DMA((n,)), *[pltpu.HBM(a.shape, a.dtype) for a in bufs],
                   jax.ShapeDtypeStruct((8, LANE), F32)),
        in_specs=[HBM_SPEC] * nb, out_specs=(SEM_SPEC, SEM_SPEC, *[HBM_SPEC] * nb, pl.BlockSpec(memory_space=pltpu.VMEM)),
        input_output_aliases={i: 2 + i for i in range(nb)},
        compiler_params=pltpu.CompilerParams(has_side_effects=SIDE_EFFECT))(
            *[pltpu.with_memory_space_constraint(a, pltpu.HBM) for a in bufs])
    return outs[0], outs[1], list(outs[2:2 + ns]), list(outs[2 + ns:2 + nb]), outs[-1]


def _exchange_wait(srcs, lands, send, recv, after, plan, *, name):
    ns, nb = len(srcs), len(srcs) + len(lands)

    def body(*refs):
        send_s, recv_s = refs[nb], refs[nb + 1]
        for i, (s, d, dev, inc) in enumerate(plan(refs[:ns], refs[ns:nb])):
            _rcopy(s, d, send_s, recv_s, i, dev).wait_send()
            _rcopy(inc, inc, send_s, recv_s, i, dev).wait_recv()

    bufs = list(srcs) + list(lands)
    outs = pl.pallas_call(
        body, name=name, out_shape=tuple(pltpu.HBM(a.shape, a.dtype) for a in bufs),
        in_specs=[HBM_SPEC] * nb + [SEM_SPEC, SEM_SPEC, ANY], out_specs=tuple([HBM_SPEC] * nb),
        input_output_aliases={i: i for i in range(nb)},
        compiler_params=pltpu.CompilerParams(has_side_effects=SIDE_EFFECT))(*bufs, send, recv, after)
    return list(outs[:ns]), list(outs[ns:])


def _forward_halves(lands, *, name):
    nt = len(lands)

    def body(*refs):
        src, dst = refs[:nt], refs[nt:2 * nt]
        send, recv = refs[2 * nt:]
        x, y, c, j, sib, chips = _place()
        cps = []
        for t in range(nt):
            for k, (cx, cy) in enumerate(chips):
                cps.append(_rcopy(src[t].at[2 * cx + cy, c], dst[t].at[2 * cx + cy, c], send, recv, 3 * t + k, sib))
                cps[-1].start()
        for t in range(nt):
            for k, (cx, cy) in enumerate(chips):
                other = dst[t].at[2 * cx + cy, 1 - c]
                _rcopy(other, other, send, recv, 3 * t + k, sib).wait_recv()
        for cp in cps:
            cp.wait_send()

    return pl.pallas_call(
        body, in_specs=[ANY] * nt, out_specs=[ANY] * nt, out_shape=[jax.ShapeDtypeStruct(a.shape, a.dtype) for a in lands],
        input_output_aliases={t: t for t in range(nt)},
        scratch_shapes=[pltpu.SemaphoreType.DMA((3 * nt,)), pltpu.SemaphoreType.DMA((3 * nt,))], name=name)(*lands)


def _all_reduce_small(parts, *, name):
    n = len(parts)
    offs, rows = [], 0
    for p in parts:
        offs.append(rows)
        rows += -(-p.shape[0] // 8) * 8

    def body(*refs):
        p_refs, o_refs = refs[:n], refs[n:2 * n]
        buf, send, recv = refs[2 * n:]
        x, y, c = lax.axis_index("x"), lax.axis_index("y"), lax.axis_index("c")
        me = 4 * x + 2 * y + c
        buf[me] = jnp.zeros((rows, LANE), F32)
        for p_ref, off in zip(p_refs, offs):
            buf[me, off:off + p_ref.shape[0], :] = p_ref[...]
        cps = []
        for r in range(1, 8):
            dev = (x ^ (r >> 2), y ^ ((r >> 1) & 1), c ^ (r & 1))
            cps.append(_rcopy(buf.at[me], buf.at[me], send, recv, r - 1, dev))
            cps[-1].start()
        for r in range(1, 8):
            frm = buf.at[me ^ r]
            _rcopy(frm, frm, send, recv, r - 1, (x, y, c)).wait_recv()
        for cp in cps:
            cp.wait_send()
        acc = buf[0]
        for d in range(1, 8):
            acc = acc + buf[d]
        for o_ref, off in zip(o_refs, offs):
            o_ref[...] = acc[off:off + o_ref.shape[0], :]

    vm = pl.BlockSpec(memory_space=pltpu.VMEM)
    return pl.pallas_call(
        body, in_specs=[vm] * n, out_specs=[vm] * n, out_shape=[jax.ShapeDtypeStruct(p.shape, F32) for p in parts],
        scratch_shapes=[pltpu.VMEM((8, rows, LANE), F32), pltpu.SemaphoreType.DMA((7,)), pltpu.SemaphoreType.DMA((7,))],
        name=name)(*parts)


def _add_halves(x, b, c_idx, *, name, tr=256):
    _, _, R, C = x.shape
    tr = min(tr, R)

    def body(c_ref, x_ref, b_ref, o_ref):
        o_ref[...] = (x_ref[...].astype(F32) + b_ref[...].astype(F32)).astype(o_ref.dtype)

    return pl.pallas_call(
        body,
        grid_spec=pltpu.PrefetchScalarGridSpec(
            num_scalar_prefetch=1, grid=(N_CHIPS, R // tr),
            in_specs=[pl.BlockSpec((None, None, tr, C), lambda s, i, c_ref: (s, c_ref[0], i, 0)),
                      pl.BlockSpec((None, tr, C), lambda s, i, c_ref: (s, i, 0))],
            out_specs=pl.BlockSpec((None, tr, C), lambda s, i, c_ref: (s, i, 0))),
        out_shape=jax.ShapeDtypeStruct(b.shape, b.dtype), compiler_params=_cp(("parallel", "parallel")), name=name)(c_idx, x, b)


def _sum_slot(p, rcv, j_idx, *, name, tr=256):
    _, R, C = p.shape
    tr = min(tr, R)

    def body(j_ref, p_ref, r_ref, o_ref):
        acc = p_ref[...].astype(F32)
        for k in range(3):
            acc = acc + r_ref[k].astype(F32)
        o_ref[...] = acc

    return pl.pallas_call(
        body,
        grid_spec=pltpu.PrefetchScalarGridSpec(
            num_scalar_prefetch=1, grid=(R // tr,),
            in_specs=[pl.BlockSpec((None, tr, C), lambda i, j_ref: (j_ref[0], i, 0)),
                      pl.BlockSpec((3, tr, C), lambda i, j_ref: (0, i, 0))],
            out_specs=pl.BlockSpec((tr, C), lambda i, j_ref: (i, 0))),
        out_shape=jax.ShapeDtypeStruct((R, C), F32), compiler_params=_cp(("parallel",)), name=name)(j_idx, p, rcv)


def _adamw_math(w, g, m, v):
    nm = ADAM_B1 * m + (1.0 - ADAM_B1) * g
    nv = ADAM_B2 * v + (1.0 - ADAM_B2) * (g * g)
    m_hat = nm / (1.0 - ADAM_B1 ** ADAM_STEP)
    v_hat = nv / (1.0 - ADAM_B2 ** ADAM_STEP)
    return -ADAM_LR * (m_hat / (jnp.sqrt(v_hat) + ADAM_EPS) + ADAM_WD * w), nm, nv


def _adamw_halves(w, g_mine, g_theirs, m, v, c_idx, *, name, layer=0, into=None, tr=128):
    _, _, R, C = w.shape
    tr = tr if R % tr == 0 else R

    def body(c_ref, w_ref, gm_ref, gt_ref, m_ref, v_ref, *rest):
        g_ref, d_ref, nm_ref, nv_ref = rest[-4:]
        gv = jnp.where(pl.program_id(0) == c_ref[0], gm_ref[...], gt_ref[...])
        d, nm, nv = _adamw_math(w_ref[...], gv, m_ref[...], v_ref[...])
        g_ref[...] = gv
        d_ref[...] = d
        nm_ref[...] = nm
        nv_ref[...] = nv

    full = pl.BlockSpec((None, None, tr, C), lambda hh, i, c_ref: (layer, hh, i, 0))
    half = pl.BlockSpec((tr, C), lambda hh, i, c_ref: (i, 0))
    sh = jax.ShapeDtypeStruct(w.shape, F32)
    extra = [] if into is None else list(into)
    return pl.pallas_call(
        body,
        grid_spec=pltpu.PrefetchScalarGridSpec(num_scalar_prefetch=1, grid=(2, R // tr),
                                               in_specs=[full, half, half, full, full] + [ANY] * len(extra),
                                               out_specs=(full,) * 4),
        out_shape=(sh,) * 4, input_output_aliases={6 + t: t for t in range(len(extra))},
        compiler_params=_cp(("parallel", "parallel")), name=name)(c_idx, w, g_mine, g_theirs, m, v, *extra)


def _adamw_parts(ws, gs, ms, vs, *, name):
    n = len(ws)

    def body(*refs):
        ins, outs = refs[:4 * n], refs[4 * n:]
        for t in range(n):
            d, nm, nv = _adamw_math(ins[t][...], ins[n + t][...], ins[2 * n + t][...], ins[3 * n + t][...])
            outs[t][...] = d
            outs[n + t][...] = nm
            outs[2 * n + t][...] = nv

    vm = pl.BlockSpec(memory_space=pltpu.VMEM)
    shapes = [jax.ShapeDtypeStruct(w.shape, F32) for w in ws] * 3
    outs = pl.pallas_call(body, in_specs=[vm] * (4 * n), out_specs=[vm] * (3 * n), out_shape=shapes, name=name)(
        *ws, *gs, *ms, *vs)
    return outs[:n], outs[n:2 * n], outs[2 * n:]


_SMALL = ["norm_g", "mem_norm_g", "xa_q_norm_g", "xa_k_norm_g", "dn_a_log", "dn_dt_bias", "dn_out_norm_g",
          "sb_q_norm_g", "sb_k_norm_g"]


def _rows128(a):
    flat = a.reshape(-1)
    pad = -flat.shape[0] % LANE
    if pad:
        flat = jnp.pad(flat, (0, pad))
    return flat.reshape(-1, LANE)


def _unrows(r, shape):
    return r.reshape(-1)[:math.prod(shape)].reshape(shape)


def kernel(x, mem, norm_g, mem_norm_g, mem_w_kv, xa_q_norm_g, xa_k_norm_g, w_out, dn_w_in, dn_conv_w, dn_a_log, dn_dt_bias, dn_out_norm_g, sb_w_in, sb_q_norm_g, sb_k_norm_g, loss_target, m_norm_g, m_mem_norm_g, m_mem_w_kv, m_xa_q_norm_g, m_xa_k_norm_g, m_w_out, m_dn_w_in, m_dn_conv_w, m_dn_a_log, m_dn_dt_bias, m_dn_out_norm_g, m_sb_w_in, m_sb_q_norm_g, m_sb_k_norm_g, v_norm_g, v_mem_norm_g, v_mem_w_kv, v_xa_q_norm_g, v_xa_k_norm_g, v_w_out, v_dn_w_in, v_dn_conv_w, v_dn_a_log, v_dn_dt_bias, v_dn_out_norm_g, v_sb_w_in, v_sb_q_norm_g, v_sb_k_norm_g):
    W = dict(norm_g=norm_g, mem_norm_g=mem_norm_g, mem_w_kv=mem_w_kv, xa_q_norm_g=xa_q_norm_g, xa_k_norm_g=xa_k_norm_g,
             w_out=w_out, dn_w_in=dn_w_in, dn_conv_w=dn_conv_w, dn_a_log=dn_a_log, dn_dt_bias=dn_dt_bias,
             dn_out_norm_g=dn_out_norm_g, sb_w_in=sb_w_in, sb_q_norm_g=sb_q_norm_g, sb_k_norm_g=sb_k_norm_g)
    M = dict(norm_g=m_norm_g, mem_norm_g=m_mem_norm_g, mem_w_kv=m_mem_w_kv, xa_q_norm_g=m_xa_q_norm_g,
             xa_k_norm_g=m_xa_k_norm_g, w_out=m_w_out, dn_w_in=m_dn_w_in, dn_conv_w=m_dn_conv_w, dn_a_log=m_dn_a_log,
             dn_dt_bias=m_dn_dt_bias, dn_out_norm_g=m_dn_out_norm_g, sb_w_in=m_sb_w_in, sb_q_norm_g=m_sb_q_norm_g,
             sb_k_norm_g=m_sb_k_norm_g)
    V = dict(norm_g=v_norm_g, mem_norm_g=v_mem_norm_g, mem_w_kv=v_mem_w_kv, xa_q_norm_g=v_xa_q_norm_g,
             xa_k_norm_g=v_xa_k_norm_g, w_out=v_w_out, dn_w_in=v_dn_w_in, dn_conv_w=v_dn_conv_w, dn_a_log=v_dn_a_log,
             dn_dt_bias=v_dn_dt_bias, dn_out_norm_g=v_dn_out_norm_g, sb_w_in=v_sb_w_in, sb_q_norm_g=v_sb_q_norm_g,
             sb_k_norm_g=v_sb_k_norm_g)
    names = ["norm_g", "mem_norm_g", "mem_w_kv", "xa_q_norm_g", "xa_k_norm_g", "w_out", "dn_w_in", "dn_conv_w",
             "dn_a_log", "dn_dt_bias", "dn_out_norm_g", "sb_w_in", "sb_q_norm_g", "sb_k_norm_g"]
    cx, cy, cc = lax.axis_index("x"), lax.axis_index("y"), lax.axis_index("c")
    slot = 2 * cx + cy
    half_r = D_MODEL // 2
    conv_cols = dn_conv_w.shape[2]

    w0s = jnp.pad(dn_w_in[0].astype(BF16), ((0, 0), (0, P0_SHARD_PAD - P0_SHARD))).reshape(2, half_r, P0_SHARD_PAD)
    w1s = sb_w_in[0].astype(BF16).reshape(2, half_r, SB_PROJ // N_CHIPS)
    convs = jnp.pad(dn_conv_w[0], ((0, 8 - DN_CONV), (0, 0))).reshape(8, 2, conv_cols // 2).transpose(1, 0, 2)
    c_idx = jnp.reshape(cc, (1,)).astype(jnp.int32)
    j_idx = jnp.reshape(slot, (1,)).astype(jnp.int32)
    own_a = [w0s, convs]
    own_b = [w1s, w_out.astype(BF16), mem_w_kv.astype(BF16)]
    gathered_a, own_b = lax.optimization_barrier((_gather_weights(own_a, name="gather_weights"), own_b))
    g0, gconv = [lax.dynamic_update_slice(g, o[None], (slot, 0, 0, 0)) for o, g in zip(own_a, gathered_a)]
    lands_b = [lax.dynamic_update_slice(lax.empty((N_CHIPS,) + o.shape, o.dtype), o[None], (slot, 0, 0, 0)) for o in own_b]
    send_b, recv_b, own_b, lands_b, token_b = _exchange_start(own_b, lands_b, _gather_plan, name="gather_late_start")

    def late_weights(after):
        _, lands = _exchange_wait(own_b, lands_b, send_b, recv_b, after, _gather_plan, name="gather_late_wait")
        g1, gout, gkv = _forward_halves(lands, name="gather_late_forward")
        return gkv, gout, g1

    rs = {}

    def scatter_start(tag, xs):
        from_sib = _swap_halves(xs, name=f"rs{tag}_swap")
        ps = [_add_halves(a, b, c_idx, name=f"rs{tag}_add{t}") for t, (a, b) in enumerate(zip(xs, from_sib))]
        rcv = [lax.empty((3,) + p.shape[1:], p.dtype) for p in ps]
        send, recv, ps, rcv, token = _exchange_start(ps, rcv, _scatter_plan, name=f"rs{tag}_scatter_start")
        rs[tag] = (ps, rcv, send, recv)
        return token

    def scatter_finish(tag, after):
        ps, rcv, send, recv = rs[tag]
        ps, rcv = _exchange_wait(ps, rcv, send, recv, after, _scatter_plan, name=f"rs{tag}_scatter_wait")
        return [_sum_slot(p, r, j_idx, name=f"rs{tag}_sum{t}") for t, (p, r) in enumerate(zip(ps, rcv))]

    def early_grads(layer, d_win, d_wout, d_wkv):
        if layer == 0:
            d_win = jnp.stack([jnp.pad(jnp.concatenate(
                [d_win[:, lo:hi] for lo, hi in _padded_pieces(s * P0_SHARD, (s + 1) * P0_SHARD)], axis=1),
                ((0, 0), (0, P0_SHARD_PAD - P0_SHARD))) for s in range(N_CHIPS)]).reshape(N_CHIPS, 2, half_r, P0_SHARD_PAD)
        return scatter_start(layer, [d_win, d_wout.reshape(N_CHIPS, 2, -1, D_MODEL), d_wkv.reshape(N_CHIPS, 2, -1, 2 * XA_WIDTH)])

    shards0 = g0.reshape(N_CHIPS, D_MODEL, P0_SHARD_PAD)
    z = lambda n: jnp.zeros((D_MODEL, n), BF16)
    w_in0 = jnp.concatenate(
        [shards0[s][:, lo:hi] for s, lo, hi in _true_pieces(0, _C_QKV) + _true_pieces(_C_QKV + 2 * DN_V_HEADS, DN_PROJ)]
        + [shards0[s][:, lo:hi] for s, lo, hi in _true_pieces(_C_QKV, _C_QKV + DN_V_HEADS)] + [z(LANE - DN_V_HEADS)]
        + [shards0[s][:, lo:hi] for s, lo, hi in _true_pieces(_C_QKV + DN_V_HEADS, _C_QKV + 2 * DN_V_HEADS)]
        + [z(P0 - P0_AB - LANE - DN_V_HEADS)], axis=1)
    conv_f = gconv.transpose(2, 0, 1, 3).reshape(8, N_CHIPS * conv_cols)[:DN_CONV]

    loss_vec, grad_x, g = _local_step(
        x[0], mem[0], loss_target[0], norm_g + token_b[0, 0], mem_norm_g, xa_q_norm_g, xa_k_norm_g, w_in0, conv_f,
        dn_a_log[0], dn_dt_bias[0], dn_out_norm_g[0], sb_q_norm_g[0], sb_k_norm_g[0], late_weights, early_grads)

    mine1 = scatter_finish(1, grad_x)
    theirs1 = _swap_with_sibling(mine1, name="rs1_join")
    big1 = [("sb_w_in", None), ("w_out", 1), ("mem_w_kv", 1)]
    big0 = [("dn_w_in", None), ("w_out", 0), ("mem_w_kv", 0)]

    out_g, out_d, out_m, out_v = {}, {}, {}, {}
    partial = {}

    def adamw_big(big, mine, theirs):
        for (n, layer), mine_g, their_g in zip(big, mine, theirs):
            layers = 1 if layer is None else 2
            view = (layers, 2) + mine_g.shape
            partial[n] = _adamw_halves(W[n].reshape(view), mine_g, their_g, M[n].reshape(view), V[n].reshape(view), c_idx,
                                       layer=layer or 0, into=partial.get(n), name=f"adamw_{n}" + ("" if layer is None else str(layer)))
        return [partial[n][0] for n, _ in big]

    done1 = lax.optimization_barrier(tuple(adamw_big(big1, mine1, theirs1)))[-1]
    mine0 = scatter_finish(0, done1)
    mine0[0] = mine0[0][:, :P0_SHARD]

    parts, _ = lax.optimization_barrier(([_rows128(g[n]) for n in _SMALL] + [_rows128(g["dn_conv_w"]), loss_vec], mine0[0]))
    red = _all_reduce_small(parts, name="all_reduce_small")
    small_rows = dict(zip(_SMALL, red))
    conv_full = red[len(_SMALL)].reshape(DN_CONV, N_CHIPS * conv_cols)
    small_rows["dn_conv_w"] = _rows128(lax.dynamic_slice_in_dim(conv_full, slot * conv_cols, conv_cols, axis=1))
    loss = red[-1][0, 0]

    adamw_big(big0, mine0, _swap_with_sibling(mine0, name="rs0_join"))
    for n, outs in partial.items():
        out_g[n], out_d[n], out_m[n], out_v[n] = [o.reshape(W[n].shape) for o in outs]
    small_names = _SMALL + ["dn_conv_w"]
    ds, nms, nvs = _adamw_parts([_rows128(W[n]) for n in small_names], [small_rows[n] for n in small_names],
                                [_rows128(M[n]) for n in small_names], [_rows128(V[n]) for n in small_names], name="adamw_small")
    for n, d, nm, nv in zip(small_names, ds, nms, nvs):
        shp = W[n].shape
        out_g[n], out_d[n], out_m[n], out_v[n] = [_unrows(r, shp) for r in (small_rows[n], d, nm, nv)]

    return (loss, grad_x[None], *[out_g[n] for n in names], *[out_d[n] for n in names], *[out_m[n] for n in names],
            *[out_v[n] for n in names])
```

```python
import functools
import math

import jax
import jax.numpy as jnp
from jax import lax
from jax.experimental import pallas as pl
from jax.experimental.pallas import tpu as pltpu

F32 = jnp.float32
BF16 = jnp.bfloat16
HI = lax.Precision.HIGHEST
MESH = pl.DeviceIdType.MESH

D_MODEL = 2048
INNER = 4096
XA_WIDTH = 1024
XA_HEADS = 4
XA_DIM = 256
MIX_WIDTH = 3072
HEAD_DIM = 128
DN_V_HEADS = 24
DN_QK_WIDTH = 1536
DN_CONV = 4
DN_PROJ = 11312
SB_PROJ = 14336
EPS = 1e-6
N_CHIPS = 4

CH = 128
LANE = 128

P0_XQ = 6144
P0_Z = 7168
P0_AB = 11264
P0 = 11776
P0_SHARD = DN_PROJ // N_CHIPS
P0_SHARD_PAD = 2944
P1_XQ = 9216
P1_Z = 10240
P1 = SB_PROJ

ADAM_LR = 0.001
ADAM_B1 = 0.9
ADAM_B2 = 0.999
ADAM_EPS = 1e-08
ADAM_WD = 0.01
ADAM_STEP = 10

VMEM_LIMIT = 48 * 1024 * 1024


def _cp(sem=None, **kw):
    return pltpu.CompilerParams(dimension_semantics=sem, vmem_limit_bytes=VMEM_LIMIT, **kw)


def _bdot(a, b, dims):
    return lax.dot_general(a.astype(BF16), b.astype(BF16), (dims, ((), ())), preferred_element_type=F32)


def _fdot(a, b, dims):
    return lax.dot_general(a, b, (dims, ((), ())), precision=HI, preferred_element_type=F32)


NN = ((1,), (0,))
NT = ((1,), (1,))
TN = ((0,), (0,))


def _sigmoid(x):
    return 1.0 / (1.0 + jnp.exp(-x))


def _softplus(x):
    return jnp.maximum(x, 0.0) + jnp.log(1.0 + jnp.exp(-jnp.abs(x)))


def _iota2(shape, axis):
    return lax.broadcasted_iota(jnp.int32, shape, axis)


MM_FULL_K = 4096
MM_BLOCK_BYTES = 4 * 1024 * 1024


def _matmul(a, b, *, ta=False, tb=False, out_dtype=F32, res=None, name, n=None, tm=None, tn=None, tk=None,
            b_spec=None, o_spec=None, o_shape=None):
    a_segs = list(a) if isinstance(a, (list, tuple)) else [a]
    b_segs = list(b) if isinstance(b, (list, tuple)) else [b]
    a0, b0 = a_segs[0], b_segs[0]
    M = a0.shape[1] if ta else a0.shape[0]
    K = a0.shape[0] if ta else sum(s.shape[1] for s in a_segs)
    if n is None:
        n = b0.shape[0] if tb else sum(s.shape[1] for s in b_segs)
    N = n
    dims = ((0,) if ta else (1,), (1,) if tb else (0,))
    has_res = res is not None
    flat = lambda v: v.reshape(-1, v.shape[-1])
    o_shape = o_shape or jax.ShapeDtypeStruct((M, N), out_dtype)

    def seg_specs(segs, tile, block, pos):
        specs, ranges, off = [], [], 0
        for s in segs:
            cnt = s.shape[1] // tile
            assert s.shape[1] % tile == 0, (name, s.shape, tile)

            def imap(*g, off=off, cnt=cnt):
                t = jnp.clip(g[pos] - off, 0, cnt - 1)
                return (g[0], t) if pos == 2 else (0, t)

            specs.append(pl.BlockSpec(block, imap))
            ranges.append((off, off + cnt))
            off += cnt
        return specs, ranges

    if K <= MM_FULL_K:
        assert len(a_segs) == 1
        tm = tm or min(M, 1024, max(256, MM_BLOCK_BYTES // (K * a0.dtype.itemsize)))
        tn = tn or min(N, 512)
        assert M % tm == 0 and N % tn == 0, (name, M, N, K, tm, tn)
        nb = len(b_segs)
        if b_spec is not None:
            b_specs, b_ranges = [b_spec], [(0, N // tn)]
        elif nb > 1:
            assert not tb
            b_specs, b_ranges = seg_specs(b_segs, tn, (K, tn), 1)
        else:
            b_specs = [pl.BlockSpec((tn, K), lambda i, j: (j, 0)) if tb else pl.BlockSpec((K, tn), lambda i, j: (0, j))]
            b_ranges = [(0, N // tn)]

        def body_full(*refs):
            a_ref, b_refs = refs[0], refs[1:1 + nb]
            r_ref = refs[1 + nb] if has_res else None
            o_ref = refs[-1]
            j = pl.program_id(1)
            for b_ref, (lo, hi) in zip(b_refs, b_ranges):
                def emit(b_ref=b_ref):
                    r = _bdot(a_ref[...], flat(b_ref[...]), dims)
                    if has_res:
                        r = r + r_ref[...]
                    o_ref[...] = r.astype(o_ref.dtype).reshape(o_ref.shape)
                if nb == 1:
                    emit()
                else:
                    pl.when(jnp.logical_and(j >= lo, j < hi))(emit)

        a_spec = pl.BlockSpec((K, tm), lambda i, j: (0, i)) if ta else pl.BlockSpec((tm, K), lambda i, j: (i, 0))
        o_spec = o_spec or pl.BlockSpec((tm, tn), lambda i, j: (i, j))
        r_spec = [pl.BlockSpec((tm, tn), lambda i, j: (i, j))] if has_res else []
        return pl.pallas_call(
            body_full, grid=(M // tm, N // tn), in_specs=[a_spec] + b_specs + r_spec, out_specs=o_spec, out_shape=o_shape,
            compiler_params=_cp(("parallel", "arbitrary")), name=name)(*([a0] + b_segs + ([res] if has_res else [])))

    assert tb and not ta and len(b_segs) == 1
    tm, tn = tm or min(M, 1024), tn or min(N, 1024)
    tk = tk or (1024 if all(s.shape[1] % 1024 == 0 for s in a_segs) else 512)
    assert M % tm == 0 and N % tn == 0 and K % tk == 0, (name, M, N, K, tm, tn, tk)
    nk = K // tk
    na = len(a_segs)
    if na > 1:
        a_specs, a_ranges = seg_specs(a_segs, tk, (tm, tk), 2)
    else:
        a_specs, a_ranges = [pl.BlockSpec((tm, tk), lambda i, j, k: (i, k))], [(0, nk)]
    b_spec = b_spec or pl.BlockSpec((tn, tk), lambda i, j, k: (j, k))

    def body(*refs):
        a_refs, b_ref = refs[:na], refs[na]
        r_ref = refs[na + 1] if has_res else None
        o_ref, acc = refs[-2], refs[-1]
        k = pl.program_id(2)

        @pl.when(k == 0)
        def _():
            acc[...] = jnp.zeros_like(acc)

        for a_ref, (lo, hi) in zip(a_refs, a_ranges):
            def emit(a_ref=a_ref):
                acc[...] += _bdot(a_ref[...], flat(b_ref[...]), dims)
            if na == 1:
                emit()
            else:
                pl.when(jnp.logical_and(k >= lo, k < hi))(emit)

        @pl.when(k == nk - 1)
        def _():
            r = acc[...]
            if has_res:
                r = r + r_ref[...]
            o_ref[...] = r.astype(o_ref.dtype).reshape(o_ref.shape)

    o_spec = o_spec or pl.BlockSpec((tm, tn), lambda i, j, k: (i, j))
    r_spec = [pl.BlockSpec((tm, tn), lambda i, j, k: (i, j))] if has_res else []
    return pl.pallas_call(
        body, grid=(M // tm, N // tn, nk), in_specs=a_specs + [b_spec] + r_spec, out_specs=o_spec, out_shape=o_shape,
        scratch_shapes=[pltpu.VMEM((tm, tn), F32)],
        compiler_params=_cp(("parallel", "parallel", "arbitrary")), name=name)(*(a_segs + [b0] + ([res] if has_res else [])))


def _rmsnorm_fwd(x, g, *, name, tm=256):
    S, Dm = x.shape
    tm = min(tm, S)

    def body(x_ref, g_ref, o_ref):
        xv = x_ref[...]
        r = lax.rsqrt(jnp.mean(xv * xv, axis=-1, keepdims=True) + EPS)
        o_ref[...] = (xv * r * g_ref[...]).astype(BF16)

    return pl.pallas_call(
        body, grid=(S // tm,), in_specs=[pl.BlockSpec((tm, Dm), lambda i: (i, 0)), pl.BlockSpec((1, Dm), lambda i: (0, 0))],
        out_specs=pl.BlockSpec((tm, Dm), lambda i: (i, 0)), out_shape=jax.ShapeDtypeStruct((S, Dm), BF16),
        compiler_params=_cp(("parallel",)), name=name)(x, g.reshape(1, Dm))


def _rmsnorm_bwd(dh, x, g, dres, *, name, tm=256):
    S, Dm = x.shape
    tm = min(tm, S)
    want_dx = dres is not None

    def body(*refs):
        if want_dx:
            dh_ref, x_ref, g_ref, dr_ref, dx_ref, dg_ref = refs
        else:
            dh_ref, x_ref, g_ref, dg_ref = refs
        i = pl.program_id(0)
        xv = x_ref[...]
        dhv = dh_ref[...]
        r = lax.rsqrt(jnp.mean(xv * xv, axis=-1, keepdims=True) + EPS)
        y = xv * r
        part = jnp.sum(dhv * y, axis=0, keepdims=True)

        @pl.when(i == 0)
        def _():
            dg_ref[...] = jnp.zeros_like(dg_ref)

        dg_ref[...] += part
        if want_dx:
            dy = dhv * g_ref[...]
            dx_ref[...] = dr_ref[...] + r * (dy - y * jnp.mean(dy * y, axis=-1, keepdims=True))

    row = pl.BlockSpec((tm, Dm), lambda i: (i, 0))
    vec = pl.BlockSpec((1, Dm), lambda i: (0, 0))
    if want_dx:
        dx, dg = pl.pallas_call(
            body, grid=(S // tm,), in_specs=[row, row, vec, row], out_specs=(row, vec),
            out_shape=(jax.ShapeDtypeStruct((S, Dm), F32), jax.ShapeDtypeStruct((1, Dm), F32)),
            compiler_params=_cp(("arbitrary",)), name=name)(dh, x, g.reshape(1, Dm), dres)
        return dx, dg
    dg = pl.pallas_call(
        body, grid=(S // tm,), in_specs=[row, row, vec], out_specs=vec,
        out_shape=jax.ShapeDtypeStruct((1, Dm), F32), compiler_params=_cp(("arbitrary",)), name=name)(dh, x, g.reshape(1, Dm))
    return None, dg


GATE_TN = XA_WIDTH
GATE_MIX_TILES = MIX_WIDTH // GATE_TN


def _gate_cat_specs(tm):
    return [pl.BlockSpec((tm, GATE_TN), lambda i, j: (i, jnp.minimum(j, GATE_MIX_TILES - 1))),
            pl.BlockSpec((tm, GATE_TN), lambda i, j: (i, 0))]


def _gate_fwd(mix, xa, proj, z_off, *, name, tm=256):
    S = mix.shape[0]
    tm = min(tm, S)
    zb = z_off // GATE_TN

    def body(m_ref, x_ref, z_ref, y_ref):
        z = z_ref[...]
        c = jnp.where(pl.program_id(1) < GATE_MIX_TILES, m_ref[...], x_ref[...])
        y_ref[...] = (c * z * _sigmoid(z)).astype(BF16)

    blk = pl.BlockSpec((tm, GATE_TN), lambda i, j: (i, j))
    return pl.pallas_call(
        body, grid=(S // tm, INNER // GATE_TN),
        in_specs=_gate_cat_specs(tm) + [pl.BlockSpec((tm, GATE_TN), lambda i, j: (i, zb + j))],
        out_specs=blk, out_shape=jax.ShapeDtypeStruct((S, INNER), BF16),
        compiler_params=_cp(("parallel", "arbitrary")), name=name)(mix, xa, proj)


def _gate_bwd(dy, mix, xa, proj, z_off, *, name, tm=256):
    S = mix.shape[0]
    tm = min(tm, S)
    zb = z_off // GATE_TN

    def body(dy_ref, m_ref, x_ref, z_ref, dc_ref, dz_ref):
        z = z_ref[...]
        sg = _sigmoid(z)
        d = dy_ref[...]
        c = jnp.where(pl.program_id(1) < GATE_MIX_TILES, m_ref[...], x_ref[...])
        dc_ref[...] = d * z * sg
        dz_ref[...] = (d * c * sg * (1.0 + z * (1.0 - sg))).astype(BF16)

    blk = pl.BlockSpec((tm, GATE_TN), lambda i, j: (i, j))
    return pl.pallas_call(
        body, grid=(S // tm, INNER // GATE_TN),
        in_specs=[blk] + _gate_cat_specs(tm) + [pl.BlockSpec((tm, GATE_TN), lambda i, j: (i, zb + j))], out_specs=(blk, blk),
        out_shape=(jax.ShapeDtypeStruct((S, INNER), F32), jax.ShapeDtypeStruct((S, INNER), BF16)),
        compiler_params=_cp(("parallel", "arbitrary")), name=name)(dy, mix, xa, proj)


def _loss_head(x, target, *, name, tm=256):
    S, Dm = x.shape
    tm = min(tm, S)

    nt = S // tm

    def body(x_ref, t_ref, dx_ref, l_ref, acc):
        i = pl.program_id(0)
        e = x_ref[...] - t_ref[...]
        dx_ref[...] = e * (1.0 / Dm)

        @pl.when(i == 0)
        def _():
            acc[...] = jnp.zeros_like(acc)

        acc[...] += jnp.sum(e * e, axis=0, keepdims=True) * (0.5 / Dm)

        @pl.when(i == nt - 1)
        def _():
            l_ref[...] = jnp.sum(acc[...], axis=1, keepdims=True) + jnp.zeros((1, LANE), F32)

    row = pl.BlockSpec((tm, Dm), lambda i: (i, 0))
    return pl.pallas_call(
        body, grid=(nt,), in_specs=[row, row], out_specs=(row, pl.BlockSpec((1, LANE), lambda i: (0, 0))),
        out_shape=(jax.ShapeDtypeStruct((S, Dm), F32), jax.ShapeDtypeStruct((1, LANE), F32)),
        scratch_shapes=[pltpu.VMEM((1, Dm), F32)],
        compiler_params=_cp(("arbitrary",)), name=name)(x, target)


def _xa_norm(v, g):
    r = lax.rsqrt(jnp.mean(v * v, axis=-1, keepdims=True) + EPS)
    return v * r, r


def _xa_fwd(proj, xq_off, kv, gq, gk, *, name, tm=512):
    S = proj.shape[0]
    tm = min(tm, S)
    qb = xq_off // XA_DIM
    n_mem = kv.shape[0]
    scale = XA_DIM ** -0.5

    def body(q_ref, k_ref, v_ref, gq_ref, gk_ref, o_ref):
        qh, _ = _xa_norm(q_ref[...], None)
        kh, _ = _xa_norm(k_ref[...], None)
        qn = qh * gq_ref[...]
        kn = kh * gk_ref[...]
        s = _bdot(qn, kn, NT) * scale
        s = s - jnp.max(s, axis=-1, keepdims=True)
        p = jnp.exp(s)
        p = p / jnp.sum(p, axis=-1, keepdims=True)
        o_ref[...] = _bdot(p, v_ref[...], NN)

    vec = pl.BlockSpec((1, XA_DIM), lambda h, i: (0, 0))
    return pl.pallas_call(
        body, grid=(XA_HEADS, S // tm),
        in_specs=[pl.BlockSpec((tm, XA_DIM), lambda h, i: (i, qb + h)),
                  pl.BlockSpec((n_mem, XA_DIM), lambda h, i: (0, h)),
                  pl.BlockSpec((n_mem, XA_DIM), lambda h, i: (0, XA_HEADS + h)), vec, vec],
        out_specs=pl.BlockSpec((tm, XA_DIM), lambda h, i: (i, h)),
        out_shape=jax.ShapeDtypeStruct((S, XA_WIDTH), F32),
        compiler_params=_cp(("parallel", "parallel")), name=name)(proj, kv, kv, gq.reshape(1, XA_DIM), gk.reshape(1, XA_DIM))


def _xa_bwd(dcat, proj, xq_off, kv, gq, gk, *, name, tm=512):
    S = proj.shape[0]
    tm = min(tm, S)
    nt = S // tm
    qb = xq_off // XA_DIM
    db = MIX_WIDTH // XA_DIM
    n_mem = kv.shape[0]
    scale = XA_DIM ** -0.5

    def body(d_ref, q_ref, k_ref, v_ref, gq_ref, gk_ref, dq_ref, dk_ref, dv_ref, dgq_ref, dgk_ref, dkn_acc):
        h = pl.program_id(0)
        i = pl.program_id(1)
        q = q_ref[...]
        k = k_ref[...]
        qh, rq = _xa_norm(q, None)
        kh, rk = _xa_norm(k, None)
        gqv = gq_ref[...]
        gkv = gk_ref[...]
        qn = qh * gqv
        kn = kh * gkv
        s = _bdot(qn, kn, NT) * scale
        s = s - jnp.max(s, axis=-1, keepdims=True)
        p = jnp.exp(s)
        p = p / jnp.sum(p, axis=-1, keepdims=True)
        d = d_ref[...]
        dp = _bdot(d, v_ref[...], NT)
        ds = p * (dp - jnp.sum(dp * p, axis=-1, keepdims=True)) * scale
        dqn = _bdot(ds, kn, NN)

        @pl.when(i == 0)
        def _():
            dkn_acc[...] = jnp.zeros_like(dkn_acc)
            dv_ref[...] = jnp.zeros_like(dv_ref)

        @pl.when(jnp.logical_and(i == 0, h == 0))
        def _():
            dgq_ref[...] = jnp.zeros_like(dgq_ref)
            dgk_ref[...] = jnp.zeros_like(dgk_ref)

        dkn_acc[...] += _bdot(ds, qn, TN)
        dv_ref[...] += _bdot(p, d, TN)
        dgq_ref[...] += jnp.sum(dqn * qh, axis=0, keepdims=True)
        dy = dqn * gqv
        dq_ref[...] = (rq * (dy - qh * jnp.mean(dy * qh, axis=-1, keepdims=True))).astype(BF16)

        @pl.when(i == nt - 1)
        def _():
            dkn = dkn_acc[...]
            dgk_ref[...] += jnp.sum(dkn * kh, axis=0, keepdims=True)
            dyk = dkn * gkv
            dk_ref[...] = rk * (dyk - kh * jnp.mean(dyk * kh, axis=-1, keepdims=True))

    vec = pl.BlockSpec((1, XA_DIM), lambda h, i: (0, 0))
    kblk = pl.BlockSpec((n_mem, XA_DIM), lambda h, i: (0, h))
    vblk = pl.BlockSpec((n_mem, XA_DIM), lambda h, i: (0, XA_HEADS + h))
    dq, dk, dv, dgq, dgk = pl.pallas_call(
        body, grid=(XA_HEADS, nt),
        in_specs=[pl.BlockSpec((tm, XA_DIM), lambda h, i: (i, db + h)),
                  pl.BlockSpec((tm, XA_DIM), lambda h, i: (i, qb + h)), kblk, vblk, vec, vec],
        out_specs=(pl.BlockSpec((tm, XA_DIM), lambda h, i: (i, h)), kblk, kblk, vec, vec),
        out_shape=(jax.ShapeDtypeStruct((S, XA_WIDTH), BF16), jax.ShapeDtypeStruct((n_mem, XA_WIDTH), F32),
                   jax.ShapeDtypeStruct((n_mem, XA_WIDTH), F32), jax.ShapeDtypeStruct((1, XA_DIM), F32),
                   jax.ShapeDtypeStruct((1, XA_DIM), F32)),
        scratch_shapes=[pltpu.VMEM((n_mem, XA_DIM), F32)],
        compiler_params=_cp(("arbitrary", "arbitrary")), name=name)(
            dcat, proj, kv, kv, gq.reshape(1, XA_DIM), gk.reshape(1, XA_DIM))
    return dq, jnp.concatenate([dk, dv], axis=1), dgq, dgk


SB_TQ = 256
SB_TK = 256
SB_HEADS = 24


SB_PAIR = 2
SB_PW = SB_PAIR * HEAD_DIM


def _hdot(a, b, dims, dot=None):
    dot = dot or _bdot
    n = a.shape[0] if a.ndim == 3 else b.shape[0]
    return jnp.stack([dot(a[i] if a.ndim == 3 else a, b[i] if b.ndim == 3 else b, dims) for i in range(n)])


def _sb_tile(qi, kj, t0, s0, masked):
    z = _hdot(qi, kj, NT)
    sp = _softplus(z)
    ls = z - sp
    if not masked:
        return -sp, ls, None
    mask = (s0 + _iota2(z.shape[1:], 1)) < (t0 + _iota2(z.shape[1:], 0))
    return jnp.where(mask, -sp, 0.0), ls, mask


def _dot2(x, tri):
    hi = x.astype(BF16)
    lo = (x - hi.astype(F32)).astype(BF16)
    plain = lambda u, v, dims: lax.dot_general(u, v, (dims, ((), ())), preferred_element_type=F32)
    return _hdot(hi, tri, NN, plain) + _hdot(lo, tri, NN, plain)


def _sb_heads(ref, rows=slice(None)):
    return jnp.stack([ref[rows, hh * HEAD_DIM:(hh + 1) * HEAD_DIM] for hh in range(SB_PAIR)])


def _sb_fwd(proj, gq, gk, *, name):
    S = proj.shape[0]
    tq, tk = min(SB_TQ, S), min(SB_TK, S)
    nq = S // tq
    scale = HEAD_DIM ** -0.5

    def body(q_ref, k_ref, v_ref, gq_ref, gk_ref, o_ref, tot_ref, qn_s, kn_s, v_s):
        q = _sb_heads(q_ref)
        k = _sb_heads(k_ref)
        qn_s[...] = (q * lax.rsqrt(jnp.mean(q * q, axis=-1, keepdims=True) + EPS) * (gq_ref[...] * scale)).astype(BF16)
        kn_s[...] = (k * lax.rsqrt(jnp.mean(k * k, axis=-1, keepdims=True) + EPS) * gk_ref[...]).astype(BF16)
        v_s[...] = _sb_heads(v_ref).astype(BF16)
        after = (_iota2((tk, tk), 0) > _iota2((tk, tk), 1)).astype(BF16)

        def qblock(i, _):
            rows = pl.ds(pl.multiple_of(i * tq, tq), tq)
            qi = qn_s[:, rows, :]
            jd = (i * tq) // tk

            def tile(j, acc, run, masked):
                cols = pl.ds(pl.multiple_of(j * tk, tk), tk)
                lr, ls, mask = _sb_tile(qi, kn_s[:, cols, :], i * tq, j * tk, masked)
                later = _dot2(lr, after) + run
                a = jnp.exp(ls + later)
                if masked:
                    a = jnp.where(mask, a, 0.0)
                acc = acc + _hdot(a, v_s[:, cols, :], NN)
                return acc, run + jnp.sum(lr, axis=-1, keepdims=True)

            acc, run = tile(jd, jnp.zeros((SB_PAIR, tq, HEAD_DIM), F32), jnp.zeros((SB_PAIR, tq, 1), F32), True)
            acc, run = lax.fori_loop(0, jd, lambda jj, c: tile(jd - 1 - jj, c[0], c[1], False), (acc, run))
            tot = run + jnp.zeros((SB_PAIR, tq, HEAD_DIM), F32)
            for hh in range(SB_PAIR):
                o_ref[rows, hh * HEAD_DIM:(hh + 1) * HEAD_DIM] = acc[hh]
                tot_ref[rows, hh * HEAD_DIM:(hh + 1) * HEAD_DIM] = tot[hh]
            return 0

        lax.fori_loop(0, nq, qblock, 0)

    npair = SB_HEADS // SB_PAIR
    vec = pl.BlockSpec((1, HEAD_DIM), lambda h: (0, 0))
    hb = lambda off: pl.BlockSpec((S, SB_PW), lambda h: (0, off + h), pipeline_mode=pl.Buffered(1))
    return pl.pallas_call(
        body, grid=(npair,), in_specs=[hb(0), hb(npair), hb(2 * npair), vec, vec],
        out_specs=(hb(0), hb(0)), out_shape=(jax.ShapeDtypeStruct((S, MIX_WIDTH), F32),) * 2,
        scratch_shapes=[pltpu.VMEM((SB_PAIR, S, HEAD_DIM), BF16)] * 3,
        compiler_params=_cp(("parallel",)), name=name)(proj, proj, proj, gq.reshape(1, HEAD_DIM), gk.reshape(1, HEAD_DIM))


def _sb_bwd(dmix, tot, proj, gq, gk, *, name):
    S = proj.shape[0]
    tq, tk = min(SB_TQ, S), min(SB_TK, S)
    nq = S // tq
    scale = HEAD_DIM ** -0.5

    def body(do_ref, o_ref, q_ref, k_ref, v_ref, gq_ref, gk_ref, dq_ref, dk_ref, dv_ref, dgq_ref, dgk_ref,
             qn_s, kn_s, v_s, dkn_s, dqn_s, dv_s):
        h = pl.program_id(0)
        q = _sb_heads(q_ref)
        k = _sb_heads(k_ref)
        rq = lax.rsqrt(jnp.mean(q * q, axis=-1, keepdims=True) + EPS)
        rk = lax.rsqrt(jnp.mean(k * k, axis=-1, keepdims=True) + EPS)
        gqv = gq_ref[...]
        gkv = gk_ref[...]
        qn_s[...] = (q * rq * (gqv * scale)).astype(BF16)
        kn_s[...] = (k * rk * gkv).astype(BF16)
        v_s[...] = _sb_heads(v_ref).astype(BF16)
        dkn_s[...] = jnp.zeros_like(dkn_s)
        dv_s[...] = jnp.zeros_like(dv_s)
        r_i = _iota2((tk, tk), 0)
        c_i = _iota2((tk, tk), 1)
        upto = (r_i <= c_i).astype(BF16)
        before = (r_i < c_i).astype(BF16)

        def qblock(i, _):
            rows = pl.ds(pl.multiple_of(i * tq, tq), tq)
            qi = qn_s[:, rows, :]
            doi = _sb_heads(do_ref, rows).astype(BF16)
            tot_i = jnp.max(_sb_heads(o_ref, rows), axis=-1, keepdims=True)
            jd = (i * tq) // tk

            def tile(j, dqn, run, run_b, masked):
                cols = pl.ds(pl.multiple_of(j * tk, tk), tk)
                kj = kn_s[:, cols, :]
                lr, ls, mask = _sb_tile(qi, kj, i * tq, j * tk, masked)
                later = tot_i - (_dot2(lr, upto) + run)
                a = jnp.exp(ls + later)
                if masked:
                    a = jnp.where(mask, a, 0.0)
                b = _hdot(doi, v_s[:, cols, :], NT) * a
                cum = _dot2(b, before) + run_b
                beta = jnp.exp(ls)
                dz = b * (1.0 - beta) - cum * beta
                if masked:
                    dz = jnp.where(mask, dz, 0.0)
                dzb = dz.astype(BF16)
                dv_s[:, cols, :] += _hdot(a, doi, TN)
                dkn_s[:, cols, :] += _hdot(dzb, qi, TN)
                dqn = dqn + _hdot(dzb, kj, NN)
                return dqn, run + jnp.sum(lr, axis=-1, keepdims=True), run_b + jnp.sum(b, axis=-1, keepdims=True)

            zero1 = jnp.zeros((SB_PAIR, tq, 1), F32)
            carry = lax.fori_loop(0, jd, lambda j, c: tile(j, c[0], c[1], c[2], False),
                                  (jnp.zeros((SB_PAIR, tq, HEAD_DIM), F32), zero1, zero1))
            dqn, _, _ = tile(jd, carry[0], carry[1], carry[2], True)
            dqn_s[:, rows, :] = dqn * scale
            return 0

        lax.fori_loop(0, nq, qblock, 0)

        @pl.when(h == 0)
        def _():
            dgq_ref[...] = jnp.zeros_like(dgq_ref)
            dgk_ref[...] = jnp.zeros_like(dgk_ref)

        heads_sum = lambda z: jnp.sum(jnp.sum(z, axis=1, keepdims=True), axis=0)
        dqn = dqn_s[...]
        qh = q * rq
        dgq_ref[...] += heads_sum(dqn * qh)
        dy = dqn * gqv
        dq = (rq * (dy - qh * jnp.mean(dy * qh, axis=-1, keepdims=True))).astype(BF16)
        dkn = dkn_s[...]
        kh = k * rk
        dgk_ref[...] += heads_sum(dkn * kh)
        dyk = dkn * gkv
        dk = (rk * (dyk - kh * jnp.mean(dyk * kh, axis=-1, keepdims=True))).astype(BF16)
        dv = dv_s[...].astype(BF16)
        for hh in range(SB_PAIR):
            lanes = slice(hh * HEAD_DIM, (hh + 1) * HEAD_DIM)
            dq_ref[:, lanes] = dq[hh]
            dk_ref[:, lanes] = dk[hh]
            dv_ref[:, lanes] = dv[hh]

    npair = SB_HEADS // SB_PAIR
    vec = pl.BlockSpec((1, HEAD_DIM), lambda h: (0, 0))
    hb = lambda off: pl.BlockSpec((S, SB_PW), lambda h: (0, off + h), pipeline_mode=pl.Buffered(1))
    dq, dk, dv, dgq, dgk = pl.pallas_call(
        body, grid=(npair,),
        in_specs=[hb(0), hb(0), hb(0), hb(npair), hb(2 * npair), vec, vec],
        out_specs=(hb(0), hb(0), hb(0), vec, vec),
        out_shape=(jax.ShapeDtypeStruct((S, MIX_WIDTH), BF16),) * 3 + (jax.ShapeDtypeStruct((1, HEAD_DIM), F32),) * 2,
        scratch_shapes=[pltpu.VMEM((SB_PAIR, S, HEAD_DIM), BF16)] * 3 + [pltpu.VMEM((SB_PAIR, S, HEAD_DIM), F32)] * 3,
        compiler_params=_cp(("arbitrary",)), name=name)(
            dmix, tot, proj, proj, proj, gq.reshape(1, HEAD_DIM), gk.reshape(1, HEAD_DIM))
    return [dq, dk, dv], dgq, dgk


def _shift_down(x, k):
    if k == 0:
        return x
    r = pltpu.roll(x, k, 0)
    return jnp.where(_iota2(x.shape, 0) >= k, r, 0.0)


def _shift_up(x, k):
    if k == 0:
        return x
    n = x.shape[0]
    r = pltpu.roll(x, n - k, 0)
    return jnp.where(_iota2(x.shape, 0) < n - k, r, 0.0)


def _conv(x, w):
    c = w[DN_CONV - 1] * x
    for k in range(1, DN_CONV):
        c = c + w[DN_CONV - 1 - k] * _shift_down(x, k)
    return c


def _dn_pre_fwd(proj, conv_w, col0, ncols, *, l2, scale, name):
    S = proj.shape[0]
    cb = col0 // HEAD_DIM

    def body(x_ref, w_ref, o_ref):
        c = _conv(x_ref[...], [w_ref[k:k + 1, :] for k in range(DN_CONV)])
        a = c * _sigmoid(c)
        if l2:
            a = a * (lax.rsqrt(jnp.sum(a * a, axis=-1, keepdims=True) + EPS) * scale)
        o_ref[...] = a

    return pl.pallas_call(
        body, grid=(ncols // HEAD_DIM,),
        in_specs=[pl.BlockSpec((S, HEAD_DIM), lambda j: (0, cb + j)), pl.BlockSpec((DN_CONV, HEAD_DIM), lambda j: (0, cb + j))],
        out_specs=pl.BlockSpec((S, HEAD_DIM), lambda j: (0, j)), out_shape=jax.ShapeDtypeStruct((S, ncols), F32),
        compiler_params=_cp(("parallel",)), name=name)(proj, conv_w)


def _dn_pre_bwd(dout, proj, conv_w, col0, ncols, *, l2, scale, name):
    S = proj.shape[0]
    cb = col0 // HEAD_DIM
    dw_in = HEAD_DIM

    def body(d_ref, x_ref, w_ref, dx_ref, dw_ref):
        x = x_ref[...]
        w = [w_ref[k:k + 1, :] for k in range(DN_CONV)]
        c = _conv(x, w)
        sg = _sigmoid(c)
        a = c * sg
        d = d_ref[...]
        if l2:
            r = lax.rsqrt(jnp.sum(a * a, axis=-1, keepdims=True) + EPS)
            y = a * r
            d = d * scale
            d = r * (d - y * jnp.sum(d * y, axis=-1, keepdims=True))
        dc = d * sg * (1.0 + c * (1.0 - sg))
        dx = w[DN_CONV - 1] * dc
        for k in range(1, DN_CONV):
            dx = dx + w[DN_CONV - 1 - k] * _shift_up(dc, k)
        dx_ref[...] = dx.astype(BF16)
        for k in range(DN_CONV):
            dw_ref[3 - k:4 - k, :] = jnp.sum(dc * _shift_down(x, k), axis=0, keepdims=True)

    return pl.pallas_call(
        body, grid=(ncols // HEAD_DIM,),
        in_specs=[pl.BlockSpec((S, dw_in), lambda j: (0, j)), pl.BlockSpec((S, HEAD_DIM), lambda j: (0, cb + j)),
                  pl.BlockSpec((DN_CONV, HEAD_DIM), lambda j: (0, cb + j))],
        out_specs=(pl.BlockSpec((S, HEAD_DIM), lambda j: (0, j)), pl.BlockSpec((DN_CONV, HEAD_DIM), lambda j: (0, j))),
        out_shape=(jax.ShapeDtypeStruct((S, ncols), BF16), jax.ShapeDtypeStruct((DN_CONV, ncols), F32)),
        compiler_params=_cp(("parallel",)), name=name)(dout, proj, conv_w)


def _dn_ab_fwd(proj, a_log, dt_bias, *, name, tm=512):
    S = proj.shape[0]
    tm = min(tm, S)
    ab = P0_AB // LANE

    def body(a_ref, b_ref, al_ref, dt_ref, g_ref, be_ref):
        g_ref[...] = -jnp.exp(al_ref[...]) * _softplus(a_ref[...] + dt_ref[...])
        be_ref[...] = _sigmoid(b_ref[...])

    vec = pl.BlockSpec((1, LANE), lambda i: (0, 0))
    out = pl.BlockSpec((tm, LANE), lambda i: (i, 0))
    return pl.pallas_call(
        body, grid=(S // tm,),
        in_specs=[pl.BlockSpec((tm, LANE), lambda i: (i, ab)), pl.BlockSpec((tm, LANE), lambda i: (i, ab + 1)), vec, vec],
        out_specs=(out, out), out_shape=(jax.ShapeDtypeStruct((S, LANE), F32),) * 2,
        compiler_params=_cp(("parallel",)), name=name)(proj, proj, a_log, dt_bias)


def _dn_ab_bwd(dg, dbeta, proj, a_log, dt_bias, *, name, tm=512):
    S = proj.shape[0]
    tm = min(tm, S)
    ab = P0_AB // LANE

    def body(dg_ref, db_ref, a_ref, b_ref, al_ref, dt_ref, dab_ref, dal_ref, ddt_ref):
        i = pl.program_id(0)
        ea = jnp.exp(al_ref[...])
        u = a_ref[...] + dt_ref[...]
        dgv = dg_ref[...]
        da = dgv * (-ea) * _sigmoid(u)
        be = _sigmoid(b_ref[...])
        dab_ref[:, 0:LANE] = da.astype(BF16)
        dab_ref[:, LANE:2 * LANE] = (db_ref[...] * be * (1.0 - be)).astype(BF16)
        dab_ref[:, 2 * LANE:] = jnp.zeros((tm, 2 * LANE), BF16)

        @pl.when(i == 0)
        def _():
            dal_ref[...] = jnp.zeros_like(dal_ref)
            ddt_ref[...] = jnp.zeros_like(ddt_ref)

        dal_ref[...] += jnp.sum(dgv * (-ea) * _softplus(u), axis=0, keepdims=True)
        ddt_ref[...] += jnp.sum(da, axis=0, keepdims=True)

    vec = pl.BlockSpec((1, LANE), lambda i: (0, 0))
    row = pl.BlockSpec((tm, LANE), lambda i: (i, 0))
    return pl.pallas_call(
        body, grid=(S // tm,),
        in_specs=[row, row, pl.BlockSpec((tm, LANE), lambda i: (i, ab)), pl.BlockSpec((tm, LANE), lambda i: (i, ab + 1)), vec, vec],
        out_specs=(pl.BlockSpec((tm, 4 * LANE), lambda i: (i, 0)), vec, vec),
        out_shape=(jax.ShapeDtypeStruct((S, 4 * LANE), BF16), jax.ShapeDtypeStruct((1, LANE), F32),
                   jax.ShapeDtypeStruct((1, LANE), F32)),
        compiler_params=_cp(("arbitrary",)), name=name)(dg, dbeta, proj, proj, a_log, dt_bias)


def _dot3(a, b):
    ah = a.astype(BF16)
    al = (a - ah.astype(F32)).astype(BF16)
    bh = b.astype(BF16)
    bl = (b - bh.astype(F32)).astype(BF16)
    d = lambda u, v: lax.dot_general(u, v, (NN, ((), ())), preferred_element_type=F32)
    return d(ah, bh) + (d(ah, bl) + d(al, bh))


DN_PAIR = 4
DN_QK = DN_PAIR // 2


def _dn_qk_heads(ref, rows):
    return jnp.stack([ref[rows, (hh // 2) * HEAD_DIM:(hh // 2 + 1) * HEAD_DIM] for hh in range(DN_PAIR)])


def _dn_big(shape, imap):
    return pl.BlockSpec(shape, imap, pipeline_mode=pl.Buffered(1))


def _pdot(a, b, dims, dot=None):
    dot = dot or _bdot
    return jnp.stack([dot(a[i] if a.ndim == 3 else a, b[i] if b.ndim == 3 else b, dims) for i in range(DN_PAIR)])


def _tri_inverse(a):
    eye = (_iota2((CH, CH), 0) == _iota2((CH, CH), 1)).astype(F32)
    d3 = lambda u, v: jnp.stack([_dot3(u[i], v[i]) for i in range(DN_PAIR)])
    t = eye - a
    x = d3(a, a)
    n = 2
    while True:
        t = t + d3(t, x)
        n *= 2
        if n >= CH:
            break
        x = d3(x, x)
    return t


def _pick_col(m, n):
    return jnp.sum(jnp.where(_iota2(m.shape, 2) == n, m, 0.0), axis=2, keepdims=True)


def _dn_chunk_common(kk, qk, gc_c, gc_r, be_c):
    r_i = _iota2((CH, CH), 0)
    c_i = _iota2((CH, CH), 1)
    incl = r_i >= c_i
    strict = r_i > c_i
    dec = jnp.exp(jnp.where(incl, gc_c - gc_r, -1e30))
    e = jnp.exp(gc_c)
    gl = jnp.sum(jnp.where(_iota2((1, CH), 1) == CH - 1, gc_r, 0.0), axis=-1, keepdims=True)
    kds = jnp.exp(gl - gc_c)
    cd = jnp.exp(gl)
    a = jnp.where(strict, be_c * kk * dec, 0.0)
    p = qk * dec
    return dict(incl=incl, strict=strict, dec=dec, e=e, kds=kds, cd=cd, kk=kk, a=a, qk=qk, p=p)


def _dn_decay_tables(g_ref, b_ref, gcr, gcc, bcc):
    r_i = _iota2((CH, CH), 0)
    c_i = _iota2((CH, CH), 1)
    lc = (r_i >= c_i).astype(F32)
    eye = (r_i == c_i).astype(F32)
    for hh in range(DN_PAIR):
        g_rows_v = g_ref[hh]
        gcr[hh] = _fdot(g_rows_v, lc, NT)
        gcc[hh] = _fdot(lc, g_rows_v, NT)
        bcc[hh] = _fdot(eye, b_ref[hh], NT)
    return lc


def _dn_core_fwd(qn, kn, vc, g_rows, b_rows, out_g, *, name):
    S = qn.shape[0]
    nc = S // CH

    def body(q_ref, k_ref, v_ref, g_ref, b_ref, og_ref, o_ref, st_ref, t_ref, gcr, gcc, bcc):
        _dn_decay_tables(g_ref, b_ref, gcr, gcc, bcc)
        ogv = og_ref[...]

        def chunk(n, states):
            rows = pl.ds(pl.multiple_of(n * CH, CH), CH)
            q = _dn_qk_heads(q_ref, rows)
            k = _dn_qk_heads(k_ref, rows)
            kk = _pdot(k, k, NT)
            qk = _pdot(q, k, NT)
            v = jnp.stack([v_ref[rows, hh * HEAD_DIM:(hh + 1) * HEAD_DIM] for hh in range(DN_PAIR)])
            gc_c = _pick_col(gcc[...], n)
            be_c = _pick_col(bcc[...], n)
            gc_r = gcr[:, pl.ds(n, 1), :]
            c = _dn_chunk_common(kk, qk, gc_c, gc_r, be_c)
            t = _tri_inverse(c["a"])
            u0 = _pdot(t, be_c * v, NN)
            w = _pdot(t, (be_c * c["e"]) * k, NN)
            u = u0 - _pdot(w, states, NN)
            o = _pdot(c["e"] * q, states, NN) + _pdot(c["p"], u, NN)
            on = o * lax.rsqrt(jnp.mean(o * o, axis=-1, keepdims=True) + EPS) * ogv
            for hh in range(DN_PAIR):
                st_ref[hh, n] = states[hh]
                t_ref[hh, n] = t[hh]
                o_ref[rows, hh * HEAD_DIM:(hh + 1) * HEAD_DIM] = on[hh]
            return c["cd"] * states + _pdot(c["kds"] * k, u, TN)

        lax.fori_loop(0, nc, chunk, jnp.zeros((DN_PAIR, HEAD_DIM, HEAD_DIM), F32))

    qk_spec = _dn_big((S, DN_QK * HEAD_DIM), lambda h: (0, h))
    v_spec = _dn_big((S, DN_PAIR * HEAD_DIM), lambda h: (0, h))
    rows_spec = pl.BlockSpec((DN_PAIR, LANE, CH), lambda h: (h, 0, 0))
    return pl.pallas_call(
        body, grid=(DN_V_HEADS // DN_PAIR,),
        in_specs=[qk_spec, qk_spec, v_spec, rows_spec, rows_spec, pl.BlockSpec((1, HEAD_DIM), lambda h: (0, 0))],
        out_specs=(v_spec, _dn_big((DN_PAIR, nc, HEAD_DIM, HEAD_DIM), lambda h: (h, 0, 0, 0)),
                   _dn_big((DN_PAIR, nc, CH, CH), lambda h: (h, 0, 0, 0))),
        out_shape=(jax.ShapeDtypeStruct((S, MIX_WIDTH), F32), jax.ShapeDtypeStruct((DN_V_HEADS, nc, HEAD_DIM, HEAD_DIM), F32),
                   jax.ShapeDtypeStruct((DN_V_HEADS, nc, CH, CH), F32)),
        scratch_shapes=[pltpu.VMEM((DN_PAIR, LANE, CH), F32), pltpu.VMEM((DN_PAIR, CH, LANE), F32),
                        pltpu.VMEM((DN_PAIR, CH, LANE), F32)],
        compiler_params=_cp(("parallel",)), name=name)(qn, kn, vc, g_rows, b_rows, out_g.reshape(1, HEAD_DIM))


def _dn_chunk_bwd(q, k, v, kk, qk, state, t, gc_c, gc_r, be_c, don, ogv, ds_next):
    ones = jnp.ones((CH, LANE), F32)
    last_row = _iota2((CH, 1), 0) == CH - 1
    rowsum = lambda z: jnp.sum(z, axis=-1, keepdims=True)
    colsum = lambda z: jnp.sum(z, axis=-2, keepdims=True)
    c = _dn_chunk_common(kk, qk, gc_c, gc_r, be_c)
    e, kds, cd, dec, a, p = c["e"], c["kds"], c["cd"], c["dec"], c["a"], c["p"]
    vb = be_c * v
    kbe = (be_c * e) * k
    u0 = _pdot(t, vb, NN)
    w = _pdot(t, kbe, NN)
    u = u0 - _pdot(w, state, NN)
    qd = e * q
    kd = kds * k
    o = _pdot(qd, state, NN) + _pdot(p, u, NN)
    r = lax.rsqrt(jnp.mean(o * o, axis=-1, keepdims=True) + EPS)
    y = o * r
    dog = colsum(don * y)
    dy = don * ogv
    d_o = r * (dy - y * jnp.mean(dy * y, axis=-1, keepdims=True))
    du = _pdot(p, d_o, TN) + _pdot(kd, ds_next, NN)
    dqd = _pdot(d_o, state, NT)
    dstate = _pdot(qd, d_o, TN) + cd * ds_next - _pdot(w, du, TN)
    dcd = colsum(rowsum(ds_next * state))
    dkd = _pdot(u, ds_next, NT)
    dw = -_pdot(du, state, NT)
    dvb = _pdot(t, du, TN)
    dkbe = _pdot(t, dw, TN)
    da = -jnp.where(c["strict"], _pdot(dvb, u0, NT) + _pdot(dkbe, w, NT), 0.0)
    dp = jnp.where(c["incl"], _pdot(d_o, u, NT), 0.0)
    gmat = da * a + dp * p
    dad = da * dec
    x = be_c * dad
    dpd = dp * dec
    dk = _pdot(x, k, NN) + _pdot(x, k, TN) + _pdot(dpd, q, TN)
    dq = _pdot(dpd, k, NN) + e * dqd
    dbe = rowsum(dad * c["kk"])
    dgc = rowsum(gmat) + rowsum(dqd * q) * e
    rk = rowsum(dkd * k) * kds
    dk = dk + kds * dkd
    dgc = dgc - rk
    dgl = colsum(rk) + dcd * cd
    sk = rowsum(dkbe * k)
    dk = dk + (be_c * e) * dkbe
    dbe = dbe + sk * e + rowsum(dvb * v)
    dgc = dgc + sk * be_c * e
    dgc = dgc + jnp.where(last_row, dgl, 0.0)
    dgc = dgc - _pdot(gmat, ones, TN, dot=_fdot)
    return dq, dk, be_c * dvb, dgc, dbe, dog, dstate


def _dn_core_bwd(dmix, qn, kn, vc, g_rows, b_rows, out_g, states, tinv, *, name):
    S = qn.shape[0]
    nc = S // CH

    def body(do_ref, q_ref, k_ref, v_ref, g_ref, b_ref, og_ref, st_ref, t_ref,
             dq_ref, dk_ref, dv_ref, dg_ref, db_ref, dog_ref, gcr, gcc, bcc, dgc_acc):
        h = pl.program_id(0)
        lc = _dn_decay_tables(g_ref, b_ref, gcr, gcc, bcc)
        ogv = og_ref[...]
        dgc_acc[...] = jnp.zeros_like(dgc_acc)
        db_ref[...] = jnp.zeros_like(db_ref)
        lane_n = _iota2((CH, LANE), 1)

        @pl.when(h == 0)
        def _():
            dog_ref[...] = jnp.zeros_like(dog_ref)

        def chunk(m, carry):
            ds_nexts, dog = carry
            n = nc - 1 - m
            rows = pl.ds(pl.multiple_of(n * CH, CH), CH)
            q = _dn_qk_heads(q_ref, rows)
            k = _dn_qk_heads(k_ref, rows)
            kk = _pdot(k, k, NT)
            qk = _pdot(q, k, NT)
            heads = lambda ref: jnp.stack([ref[rows, hh * HEAD_DIM:(hh + 1) * HEAD_DIM] for hh in range(DN_PAIR)])
            state = jnp.stack([st_ref[hh, n] for hh in range(DN_PAIR)])
            t = jnp.stack([t_ref[hh, n] for hh in range(DN_PAIR)])
            dq, dk, dv, dgc, dbe, dog_h, dstate = _dn_chunk_bwd(
                q, k, heads(v_ref), kk, qk, state, t, _pick_col(gcc[...], n), gcr[:, pl.ds(n, 1), :],
                _pick_col(bcc[...], n), heads(do_ref), ogv, ds_nexts)
            for hh in range(DN_PAIR):
                dv_ref[rows, hh * HEAD_DIM:(hh + 1) * HEAD_DIM] = dv[hh]
            dgc_acc[...] = jnp.where(lane_n == n, dgc, dgc_acc[...])
            db_ref[...] = jnp.where(lane_n == n, dbe, db_ref[...])
            for i in range(DN_QK):
                dq_ref[rows, i * HEAD_DIM:(i + 1) * HEAD_DIM] = dq[2 * i] + dq[2 * i + 1]
                dk_ref[rows, i * HEAD_DIM:(i + 1) * HEAD_DIM] = dk[2 * i] + dk[2 * i + 1]
            return dstate, dog + jnp.sum(dog_h, axis=0)

        _, dog = lax.fori_loop(0, nc, chunk, (jnp.zeros((DN_PAIR, HEAD_DIM, HEAD_DIM), F32), jnp.zeros((1, HEAD_DIM), F32)))
        dog_ref[...] += dog
        for hh in range(DN_PAIR):
            dg_ref[hh] = _fdot(lc, dgc_acc[hh], TN)

    qk_spec = _dn_big((S, DN_QK * HEAD_DIM), lambda h: (0, h))
    v_spec = _dn_big((S, DN_PAIR * HEAD_DIM), lambda h: (0, h))
    rows_spec = pl.BlockSpec((DN_PAIR, LANE, CH), lambda h: (h, 0, 0))
    cols_spec = pl.BlockSpec((DN_PAIR, CH, LANE), lambda h: (h, 0, 0))
    vec = pl.BlockSpec((1, HEAD_DIM), lambda h: (0, 0))
    qk_out = jax.ShapeDtypeStruct((S, DN_QK_WIDTH), F32)
    return pl.pallas_call(
        body, grid=(DN_V_HEADS // DN_PAIR,),
        in_specs=[v_spec, qk_spec, qk_spec, v_spec, rows_spec, rows_spec, vec,
                  _dn_big((DN_PAIR, nc, HEAD_DIM, HEAD_DIM), lambda h: (h, 0, 0, 0)),
                  _dn_big((DN_PAIR, nc, CH, CH), lambda h: (h, 0, 0, 0))],
        out_specs=(qk_spec, qk_spec, v_spec, cols_spec, cols_spec, vec),
        out_shape=(qk_out, qk_out, jax.ShapeDtypeStruct((S, MIX_WIDTH), F32), jax.ShapeDtypeStruct((DN_V_HEADS, CH, LANE), F32),
                   jax.ShapeDtypeStruct((DN_V_HEADS, CH, LANE), F32), jax.ShapeDtypeStruct((1, HEAD_DIM), F32)),
        scratch_shapes=[pltpu.VMEM((DN_PAIR, LANE, CH), F32), pltpu.VMEM((DN_PAIR, CH, LANE), F32),
                        pltpu.VMEM((DN_PAIR, CH, LANE), F32), pltpu.VMEM((DN_PAIR, CH, LANE), F32)],
        compiler_params=_cp(("arbitrary",)), name=name)(
            dmix, qn, kn, vc, g_rows, b_rows, out_g.reshape(1, HEAD_DIM), states, tinv)


def _rows_form(x, nc):
    t = x[:, :DN_V_HEADS].T.reshape(DN_V_HEADS, nc, CH)
    return jnp.pad(t, ((0, 0), (0, LANE - nc), (0, 0)))


def _cols_to_nat(x, nc):
    t = jnp.transpose(x[:, :, :nc], (2, 1, 0)).reshape(nc * CH, DN_V_HEADS)
    return jnp.pad(t, ((0, 0), (0, LANE - DN_V_HEADS)))


_C_QKV = 2 * DN_QK_WIDTH + MIX_WIDTH


def _w0_to_padded(w):
    rows = w.shape[0]
    z = lambda n: jnp.zeros((rows, n), w.dtype)
    a = w[:, _C_QKV:_C_QKV + DN_V_HEADS]
    b = w[:, _C_QKV + DN_V_HEADS:_C_QKV + 2 * DN_V_HEADS]
    return jnp.concatenate([w[:, :_C_QKV], w[:, _C_QKV + 2 * DN_V_HEADS:], a, z(LANE - DN_V_HEADS), b,
                            z(P0 - P0_AB - LANE - DN_V_HEADS)], axis=1)


def _w0_from_padded(g):
    return jnp.concatenate([g[:, :_C_QKV], g[:, P0_AB:P0_AB + DN_V_HEADS], g[:, P0_AB + LANE:P0_AB + LANE + DN_V_HEADS],
                            g[:, _C_QKV:P0_AB]], axis=1)


def _true_pieces(lo, hi):
    out = []
    while lo < hi:
        s = lo // P0_SHARD
        end = min(hi, (s + 1) * P0_SHARD)
        out.append((s, lo - s * P0_SHARD, end - s * P0_SHARD))
        lo = end
    return out


def _padded_pieces(lo, hi):
    a0, b0, x0 = _C_QKV, _C_QKV + DN_V_HEADS, _C_QKV + 2 * DN_V_HEADS
    out = []
    for t0, t1, shift in ((0, a0, 0), (a0, b0, P0_AB - a0), (b0, x0, P0_AB + LANE - b0), (x0, DN_PROJ, a0 - x0)):
        s, e = max(lo, t0), min(hi, t1)
        if s < e:
            out.append((s + shift, e + shift))
    return out


def _pad_lane(v):
    v = v.reshape(1, -1)
    return jnp.pad(v, ((0, 0), (0, LANE - v.shape[1])))


SLOT1 = SB_PROJ // N_CHIPS
MM_TN = 512


def _local_step(x, mem, target, norm_g, mem_norm_g, xa_q_g, xa_k_g, w_in0, conv_w, a_log, dt_bias, out_g, sb_q_g, sb_k_g,
                late_weights, early_grads):
    S = x.shape[0]
    nc = S // CH
    al = _pad_lane(a_log)
    dtb = _pad_lane(dt_bias)
    q_scale = HEAD_DIM ** -0.5
    tiles1 = SLOT1 // MM_TN

    kv_rhs = lambda l: pl.BlockSpec((N_CHIPS, None, D_MODEL // N_CHIPS, MM_TN), lambda i, j: (0, l, 0, j))
    kv_rhs_t = lambda l: pl.BlockSpec((None, None, D_MODEL // N_CHIPS, 2 * XA_WIDTH), lambda i, j: (j, l, 0, 0))
    out_rhs = lambda l: pl.BlockSpec((N_CHIPS, None, INNER // N_CHIPS, MM_TN), lambda i, j: (0, l, 0, j))
    out_rhs_t = lambda l: pl.BlockSpec((None, None, MM_TN, D_MODEL), lambda i, j: (j // 2, l, j % 2, 0))
    in1_rhs = pl.BlockSpec((None, 2, D_MODEL // 2, MM_TN), lambda i, j: (j // tiles1, 0, 0, j % tiles1))
    in1_rhs_t = pl.BlockSpec((None, None, D_MODEL // 2, MM_TN), lambda i, j, k: (k // tiles1, j, 0, k % tiles1))
    slot_rows = lambda rows: dict(
        tm=rows, o_spec=pl.BlockSpec((None, rows, MM_TN), lambda i, j: (i, 0, j)),
        o_shape=jax.ShapeDtypeStruct((N_CHIPS, rows, 2 * XA_WIDTH), BF16))
    in1_out = dict(tm=D_MODEL // 2, o_spec=pl.BlockSpec((None, None, D_MODEL // 2, MM_TN),
                                                        lambda i, j: (j // tiles1, i, 0, j % tiles1)),
                   o_shape=jax.ShapeDtypeStruct((N_CHIPS, 2, D_MODEL // 2, SLOT1), BF16))

    h0 = _rmsnorm_fwd(x, norm_g[0], name="norm0")
    proj0 = _matmul(h0, w_in0, name="proj0")
    qn = _dn_pre_fwd(proj0, conv_w, 0, DN_QK_WIDTH, l2=True, scale=q_scale, name="dn_pre_q")
    kn = _dn_pre_fwd(proj0, conv_w, DN_QK_WIDTH, DN_QK_WIDTH, l2=True, scale=1.0, name="dn_pre_k")
    vc = _dn_pre_fwd(proj0, conv_w, 2 * DN_QK_WIDTH, MIX_WIDTH, l2=False, scale=1.0, name="dn_pre_v")
    g_nat, b_nat = _dn_ab_fwd(proj0, al, dtb, name="dn_ab")
    g_rows = _rows_form(g_nat, nc)
    b_rows = _rows_form(b_nat, nc)
    mix0, states, tinv = _dn_core_fwd(qn, kn, vc, g_rows, b_rows, out_g, name="dn_core")
    w_kv, w_out, w_in1 = late_weights(mix0)
    mem_n = _rmsnorm_fwd(mem, mem_norm_g, name="mem_norm")
    kv = [_matmul(mem_n, w_kv, n=2 * XA_WIDTH, tn=MM_TN, b_spec=kv_rhs(l), name=f"kv{l}") for l in range(2)]
    xa0 = _xa_fwd(proj0, P0_XQ, kv[0], xa_q_g[0], xa_k_g[0], name="xa0")
    y0 = _gate_fwd(mix0, xa0, proj0, P0_Z, name="gate0")
    x1 = _matmul(y0, w_out, n=D_MODEL, tn=MM_TN, b_spec=out_rhs(0), res=x, name="out0")

    h1 = _rmsnorm_fwd(x1, norm_g[1], name="norm1")
    proj1 = _matmul(h1, w_in1, n=SB_PROJ, tn=MM_TN, b_spec=in1_rhs, name="proj1")
    mix1, tot1 = _sb_fwd(proj1, sb_q_g, sb_k_g, name="sb")
    xa1 = _xa_fwd(proj1, P1_XQ, kv[1], xa_q_g[1], xa_k_g[1], name="xa1")
    y1 = _gate_fwd(mix1, xa1, proj1, P1_Z, name="gate1")
    x2 = _matmul(y1, w_out, n=D_MODEL, tn=MM_TN, b_spec=out_rhs(1), res=x1, name="out1")

    dx2, loss_vec = _loss_head(x2, target, name="loss")

    d_wout1 = _matmul(y1, dx2, ta=True, name="d_wout1", **slot_rows(INNER // N_CHIPS))
    dy1 = _matmul(dx2, w_out, tb=True, n=INNER, tn=MM_TN, b_spec=out_rhs_t(1), name="dy1")
    dcat1, dz1 = _gate_bwd(dy1, mix1, xa1, proj1, P1_Z, name="gate1_bwd")
    dqkv1, d_sbq, d_sbk = _sb_bwd(dcat1, tot1, proj1, sb_q_g, sb_k_g, name="sb_bwd")
    dxq1, dkv1, d_xaq1, d_xak1 = _xa_bwd(dcat1, proj1, P1_XQ, kv[1], xa_q_g[1], xa_k_g[1], name="xa1_bwd")
    dproj1 = dqkv1 + [dxq1, dz1]
    d_win1 = _matmul(h1, dproj1, ta=True, name="d_win1", **in1_out)
    d_wkv1 = _matmul(mem_n, dkv1, ta=True, name="d_wkv1", **slot_rows(D_MODEL // N_CHIPS))
    token = early_grads(1, d_win1, d_wout1, d_wkv1)
    dh1 = _matmul(dproj1, w_in1, tb=True, n=D_MODEL, tn=D_MODEL // 2, tk=MM_TN, b_spec=in1_rhs_t, name="dh1")
    dx1, d_ng1 = _rmsnorm_bwd(dh1, x1, norm_g[1] + token[0, 0], dx2, name="norm1_bwd")

    d_wout0 = _matmul(y0, dx1, ta=True, name="d_wout0", **slot_rows(INNER // N_CHIPS))
    dy0 = _matmul(dx1, w_out, tb=True, n=INNER, tn=MM_TN, b_spec=out_rhs_t(0), name="dy0")
    dcat0, dz0 = _gate_bwd(dy0, mix0, xa0, proj0, P0_Z, name="gate0_bwd")
    dqv, dkv_h, dvc, dg_cols, db_cols, d_outg = _dn_core_bwd(
        dcat0, qn, kn, vc, g_rows, b_rows, out_g, states, tinv, name="dn_core_bwd")
    dpq, dwq = _dn_pre_bwd(dqv, proj0, conv_w, 0, DN_QK_WIDTH, l2=True, scale=q_scale, name="dn_pre_q_bwd")
    dpk, dwk = _dn_pre_bwd(dkv_h, proj0, conv_w, DN_QK_WIDTH, DN_QK_WIDTH, l2=True, scale=1.0, name="dn_pre_k_bwd")
    dpv, dwv = _dn_pre_bwd(dvc, proj0, conv_w, 2 * DN_QK_WIDTH, MIX_WIDTH, l2=False, scale=1.0, name="dn_pre_v_bwd")
    dab, d_alog, d_dt = _dn_ab_bwd(_cols_to_nat(dg_cols, nc), _cols_to_nat(db_cols, nc), proj0, al, dtb, name="dn_ab_bwd")
    dxq0, dkv0, d_xaq0, d_xak0 = _xa_bwd(dcat0, proj0, P0_XQ, kv[0], xa_q_g[0], xa_k_g[0], name="xa0_bwd")
    d_win0 = _matmul(h0, [dpq, dpk, dpv, dxq0, dz0, dab], ta=True, out_dtype=BF16, name="d_win0")
    d_wkv0 = _matmul(mem_n, dkv0, ta=True, name="d_wkv0", **slot_rows(D_MODEL // N_CHIPS))
    token = early_grads(0, d_win0, d_wout0, d_wkv0)
    zero = token[0, 0]
    dh0 = _matmul([dpq, dpk, dpv, dxq0, dz0, dab + zero.astype(BF16)], w_in0, tb=True, tk=MM_TN, name="dh0")
    dx0, d_ng0 = _rmsnorm_bwd(dh0, x, norm_g[0] + zero, dx1, name="norm0_bwd")

    dmem0 = _matmul(dkv0, w_kv, tb=True, n=D_MODEL, tn=D_MODEL // N_CHIPS, b_spec=kv_rhs_t(0), name="dmem0")
    dmem_n = _matmul(dkv1, w_kv, tb=True, n=D_MODEL, tn=D_MODEL // N_CHIPS, b_spec=kv_rhs_t(1), res=dmem0, name="dmem1")
    _, d_memg = _rmsnorm_bwd(dmem_n, mem, mem_norm_g, None, name="mem_norm_bwd")

    grads = dict(
        norm_g=jnp.concatenate([d_ng0, d_ng1], axis=0), mem_norm_g=d_memg.reshape(-1),
        xa_q_norm_g=jnp.concatenate([d_xaq0, d_xaq1], axis=0), xa_k_norm_g=jnp.concatenate([d_xak0, d_xak1], axis=0),
        dn_conv_w=jnp.concatenate([dwq, dwk, dwv], axis=1),
        dn_a_log=d_alog[:, :DN_V_HEADS], dn_dt_bias=d_dt[:, :DN_V_HEADS], dn_out_norm_g=d_outg,
        sb_q_norm_g=d_sbq, sb_k_norm_g=d_sbk)
    return loss_vec, dx0, grads


ANY = pl.BlockSpec(memory_space=pl.ANY)


def _place():
    x, y, c = lax.axis_index("x"), lax.axis_index("y"), lax.axis_index("c")
    chips = [(1 - x, y), (x, 1 - y), (1 - x, 1 - y)]
    return x, y, c, 2 * x + y, (x, y, 1 - c), chips


def _rcopy(src, dst, send, recv, i, dev):
    return pltpu.make_async_remote_copy(src_ref=src, dst_ref=dst, send_sem=send.at[i], recv_sem=recv.at[i],
                                        device_id=dev, device_id_type=MESH)


def _gather_weights(srcs, *, name):
    nt = len(srcs)

    def body(*refs):
        src, dst = refs[:nt], refs[nt:2 * nt]
        send, recv = refs[2 * nt:]
        x, y, c, j, sib, chips = _place()
        sends = []
        for t in range(nt):
            for k, (cx, cy) in enumerate(chips):
                sends.append(_rcopy(src[t].at[c], dst[t].at[j, c], send, recv, 6 * t + k, (cx, cy, c)))
                sends[-1].start()
        for t in range(nt):
            for k, (cx, cy) in enumerate(chips):
                landed = dst[t].at[2 * cx + cy, c]
                _rcopy(landed, landed, send, recv, 6 * t + k, (cx, cy, c)).wait_recv()
                sends.append(_rcopy(landed, landed, send, recv, 6 * t + 3 + k, sib))
                sends[-1].start()
        for t in range(nt):
            for k, (cx, cy) in enumerate(chips):
                other = dst[t].at[2 * cx + cy, 1 - c]
                _rcopy(other, other, send, recv, 6 * t + 3 + k, sib).wait_recv()
        for cp in sends:
            cp.wait_send()

    return pl.pallas_call(
        body, in_specs=[ANY] * nt, out_specs=[ANY] * nt,
        out_shape=[jax.ShapeDtypeStruct((N_CHIPS,) + s.shape, s.dtype) for s in srcs],
        scratch_shapes=[pltpu.SemaphoreType.DMA((6 * nt,)), pltpu.SemaphoreType.DMA((6 * nt,))],
        name=name)(*srcs)


def _swap_halves(xs, *, name):
    nt = len(xs)

    def body(*refs):
        src, dst = refs[:nt], refs[nt:2 * nt]
        send, recv = refs[2 * nt:]
        x, y, c, j, sib, chips = _place()
        cps = []
        for t in range(nt):
            for s in range(N_CHIPS):
                cps.append(_rcopy(src[t].at[s, 1 - c], dst[t].at[s], send, recv, 4 * t + s, sib))
                cps[-1].start()
        for cp in cps:
            cp.wait_recv()
        for cp in cps:
            cp.wait_send()

    return pl.pallas_call(
        body, in_specs=[ANY] * nt, out_specs=[ANY] * nt,
        out_shape=[jax.ShapeDtypeStruct((N_CHIPS,) + a.shape[2:], a.dtype) for a in xs],
        scratch_shapes=[pltpu.SemaphoreType.DMA((4 * nt,)), pltpu.SemaphoreType.DMA((4 * nt,))], name=name)(*xs)


def _scatter_to_chips(ps, *, name):
    nt = len(ps)

    def body(*refs):
        src, dst = refs[:nt], refs[nt:2 * nt]
        send, recv = refs[2 * nt:]
        x, y, c, j, sib, chips = _place()
        cps = []
        for t in range(nt):
            for k, (cx, cy) in enumerate(chips):
                cps.append(_rcopy(src[t].at[2 * cx + cy], dst[t].at[k], send, recv, 3 * t + k, (cx, cy, c)))
                cps[-1].start()
        for cp in cps:
            cp.wait_recv()
        for cp in cps:
            cp.wait_send()

    return pl.pallas_call(
        body, in_specs=[ANY] * nt, out_specs=[ANY] * nt,
        out_shape=[jax.ShapeDtypeStruct((3,) + a.shape[1:], a.dtype) for a in ps],
        scratch_shapes=[pltpu.SemaphoreType.DMA((3 * nt,)), pltpu.SemaphoreType.DMA((3 * nt,))], name=name)(*ps)


def _swap_with_sibling(fs, *, name):
    nt = len(fs)

    def body(*refs):
        src, dst = refs[:nt], refs[nt:2 * nt]
        send, recv = refs[2 * nt:]
        x, y, c, j, sib, chips = _place()
        cps = [_rcopy(src[t], dst[t], send, recv, t, sib) for t in range(nt)]
        for cp in cps:
            cp.start()
        for cp in cps:
            cp.wait_recv()
        for cp in cps:
            cp.wait_send()

    return pl.pallas_call(
        body, in_specs=[ANY] * nt, out_specs=[ANY] * nt,
        out_shape=[jax.ShapeDtypeStruct(a.shape, a.dtype) for a in fs],
        scratch_shapes=[pltpu.SemaphoreType.DMA((nt,)), pltpu.SemaphoreType.DMA((nt,))], name=name)(*fs)


HBM_SPEC = pl.BlockSpec(memory_space=pltpu.HBM)
SEM_SPEC = pl.BlockSpec(memory_space=pltpu.SEMAPHORE)
SIDE_EFFECT = pltpu.SideEffectType.DATAFLOW_SIDE_EFFECTING


def _gather_plan(src, land):
    x, y, c, j, sib, chips = _place()
    return [(src[t].at[c], land[t].at[j, c], (cx, cy, c), land[t].at[2 * cx + cy, c])
            for t in range(len(src)) for cx, cy in chips]


def _scatter_plan(src, land):
    x, y, c, j, sib, chips = _place()
    return [(src[t].at[2 * cx + cy], land[t].at[k], (cx, cy, c), land[t].at[k])
            for t in range(len(src)) for k, (cx, cy) in enumerate(chips)]


def _exchange_start(srcs, lands, plan, *, name):
    ns, nb = len(srcs), len(srcs) + len(lands)
    n = 3 * ns

    def body(*refs):
        send, recv, token = refs[nb], refs[nb + 1], refs[-1]
        for i, (s, d, dev, _) in enumerate(plan(refs[:ns], refs[ns:nb])):
            _rcopy(s, d, send, recv, i, dev).start()
        token[...] = jnp.zeros_like(token)

    bufs = list(srcs) + list(lands)
    outs = pl.pallas_call(
        body, name=name,
        out_shape=(pltpu.SemaphoreType.DMA((n,)), pltpu.SemaphoreType.DMA((n,)), *[pltpu.HBM(a.shape, a.dtype) for a in bufs],
                   jax.ShapeDtypeStruct((8, LANE), F32)),
        in_specs=[HBM_SPEC] * nb, out_specs=(SEM_SPEC, SEM_SPEC, *[HBM_SPEC] * nb, pl.BlockSpec(memory_space=pltpu.VMEM)),
        input_output_aliases={i: 2 + i for i in range(nb)},
        compiler_params=pltpu.CompilerParams(has_side_effects=SIDE_EFFECT))(
            *[pltpu.with_memory_space_constraint(a, pltpu.HBM) for a in bufs])
    return outs[0], outs[1], list(outs[2:2 + ns]), list(outs[2 + ns:2 + nb]), outs[-1]


def _exchange_wait(srcs, lands, send, recv, after, plan, *, name):
    ns, nb = len(srcs), len(srcs) + len(lands)

    def body(*refs):
        send_s, recv_s = refs[nb], refs[nb + 1]
        for i, (s, d, dev, inc) in enumerate(plan(refs[:ns], refs[ns:nb])):
            _rcopy(s, d, send_s, recv_s, i, dev).wait_send()
            _rcopy(inc, inc, send_s, recv_s, i, dev).wait_recv()

    bufs = list(srcs) + list(lands)
    outs = pl.pallas_call(
        body, name=name, out_shape=tuple(pltpu.HBM(a.shape, a.dtype) for a in bufs),
        in_specs=[HBM_SPEC] * nb + [SEM_SPEC, SEM_SPEC, ANY], out_specs=tuple([HBM_SPEC] * nb),
        input_output_aliases={i: i for i in range(nb)},
        compiler_params=pltpu.CompilerParams(has_side_effects=SIDE_EFFECT))(*bufs, send, recv, after)
    return list(outs[:ns]), list(outs[ns:])


def _forward_halves(lands, *, name):
    nt = len(lands)

    def body(*refs):
        src, dst = refs[:nt], refs[nt:2 * nt]
        send, recv = refs[2 * nt:]
        x, y, c, j, sib, chips = _place()
        cps = []
        for t in range(nt):
            for k, (cx, cy) in enumerate(chips):
                cps.append(_rcopy(src[t].at[2 * cx + cy, c], dst[t].at[2 * cx + cy, c], send, recv, 3 * t + k, sib))
                cps[-1].start()
        for t in range(nt):
            for k, (cx, cy) in enumerate(chips):
                other = dst[t].at[2 * cx + cy, 1 - c]
                _rcopy(other, other, send, recv, 3 * t + k, sib).wait_recv()
        for cp in cps:
            cp.wait_send()

    return pl.pallas_call(
        body, in_specs=[ANY] * nt, out_specs=[ANY] * nt, out_shape=[jax.ShapeDtypeStruct(a.shape, a.dtype) for a in lands],
        input_output_aliases={t: t for t in range(nt)},
        scratch_shapes=[pltpu.SemaphoreType.DMA((3 * nt,)), pltpu.SemaphoreType.DMA((3 * nt,))], name=name)(*lands)


def _all_reduce_small(parts, *, name):
    n = len(parts)
    offs, rows = [], 0
    for p in parts:
        offs.append(rows)
        rows += -(-p.shape[0] // 8) * 8

    def body(*refs):
        p_refs, o_refs = refs[:n], refs[n:2 * n]
        buf, send, recv = refs[2 * n:]
        x, y, c = lax.axis_index("x"), lax.axis_index("y"), lax.axis_index("c")
        me = 4 * x + 2 * y + c
        buf[me] = jnp.zeros((rows, LANE), F32)
        for p_ref, off in zip(p_refs, offs):
            buf[me, off:off + p_ref.shape[0], :] = p_ref[...]
        cps = []
        for r in range(1, 8):
            dev = (x ^ (r >> 2), y ^ ((r >> 1) & 1), c ^ (r & 1))
            cps.append(_rcopy(buf.at[me], buf.at[me], send, recv, r - 1, dev))
            cps[-1].start()
        for r in range(1, 8):
            frm = buf.at[me ^ r]
            _rcopy(frm, frm, send, recv, r - 1, (x, y, c)).wait_recv()
        for cp in cps:
            cp.wait_send()
        acc = buf[0]
        for d in range(1, 8):
            acc = acc + buf[d]
        for o_ref, off in zip(o_refs, offs):
            o_ref[...] = acc[off:off + o_ref.shape[0], :]

    vm = pl.BlockSpec(memory_space=pltpu.VMEM)
    return pl.pallas_call(
        body, in_specs=[vm] * n, out_specs=[vm] * n, out_shape=[jax.ShapeDtypeStruct(p.shape, F32) for p in parts],
        scratch_shapes=[pltpu.VMEM((8, rows, LANE), F32), pltpu.SemaphoreType.DMA((7,)), pltpu.SemaphoreType.DMA((7,))],
        name=name)(*parts)


def _add_halves(x, b, c_idx, *, name, tr=256):
    _, _, R, C = x.shape
    tr = min(tr, R)

    def body(c_ref, x_ref, b_ref, o_ref):
        o_ref[...] = (x_ref[...].astype(F32) + b_ref[...].astype(F32)).astype(o_ref.dtype)

    return pl.pallas_call(
        body,
        grid_spec=pltpu.PrefetchScalarGridSpec(
            num_scalar_prefetch=1, grid=(N_CHIPS, R // tr),
            in_specs=[pl.BlockSpec((None, None, tr, C), lambda s, i, c_ref: (s, c_ref[0], i, 0)),
                      pl.BlockSpec((None, tr, C), lambda s, i, c_ref: (s, i, 0))],
            out_specs=pl.BlockSpec((None, tr, C), lambda s, i, c_ref: (s, i, 0))),
        out_shape=jax.ShapeDtypeStruct(b.shape, b.dtype), compiler_params=_cp(("parallel", "parallel")), name=name)(c_idx, x, b)


def _sum_slot(p, rcv, j_idx, *, name, tr=256):
    _, R, C = p.shape
    tr = min(tr, R)

    def body(j_ref, p_ref, r_ref, o_ref):
        acc = p_ref[...].astype(F32)
        for k in range(3):
            acc = acc + r_ref[k].astype(F32)
        o_ref[...] = acc

    return pl.pallas_call(
        body,
        grid_spec=pltpu.PrefetchScalarGridSpec(
            num_scalar_prefetch=1, grid=(R // tr,),
            in_specs=[pl.BlockSpec((None, tr, C), lambda i, j_ref: (j_ref[0], i, 0)),
                      pl.BlockSpec((3, tr, C), lambda i, j_ref: (0, i, 0))],
            out_specs=pl.BlockSpec((tr, C), lambda i, j_ref: (i, 0))),
        out_shape=jax.ShapeDtypeStruct((R, C), F32), compiler_params=_cp(("parallel",)), name=name)(j_idx, p, rcv)


def _adamw_math(w, g, m, v):
    nm = ADAM_B1 * m + (1.0 - ADAM_B1) * g
    nv = ADAM_B2 * v + (1.0 - ADAM_B2) * (g * g)
    m_hat = nm / (1.0 - ADAM_B1 ** ADAM_STEP)
    v_hat = nv / (1.0 - ADAM_B2 ** ADAM_STEP)
    return -ADAM_LR * (m_hat / (jnp.sqrt(v_hat) + ADAM_EPS) + ADAM_WD * w), nm, nv


def _adamw_halves(w, g_mine, g_theirs, m, v, c_idx, *, name, layer=0, into=None, tr=128):
    _, _, R, C = w.shape
    tr = tr if R % tr == 0 else R

    def body(c_ref, w_ref, gm_ref, gt_ref, m_ref, v_ref, *rest):
        g_ref, d_ref, nm_ref, nv_ref = rest[-4:]
        gv = jnp.where(pl.program_id(0) == c_ref[0], gm_ref[...], gt_ref[...])
        d, nm, nv = _adamw_math(w_ref[...], gv, m_ref[...], v_ref[...])
        g_ref[...] = gv
        d_ref[...] = d
        nm_ref[...] = nm
        nv_ref[...] = nv

    full = pl.BlockSpec((None, None, tr, C), lambda hh, i, c_ref: (layer, hh, i, 0))
    half = pl.BlockSpec((tr, C), lambda hh, i, c_ref: (i, 0))
    sh = jax.ShapeDtypeStruct(w.shape, F32)
    extra = [] if into is None else list(into)
    return pl.pallas_call(
        body,
        grid_spec=pltpu.PrefetchScalarGridSpec(num_scalar_prefetch=1, grid=(2, R // tr),
                                               in_specs=[full, half, half, full, full] + [ANY] * len(extra),
                                               out_specs=(full,) * 4),
        out_shape=(sh,) * 4, input_output_aliases={6 + t: t for t in range(len(extra))},
        compiler_params=_cp(("parallel", "parallel")), name=name)(c_idx, w, g_mine, g_theirs, m, v, *extra)


def _adamw_parts(ws, gs, ms, vs, *, name):
    n = len(ws)

    def body(*refs):
        ins, outs = refs[:4 * n], refs[4 * n:]
        for t in range(n):
            d, nm, nv = _adamw_math(ins[t][...], ins[n + t][...], ins[2 * n + t][...], ins[3 * n + t][...])
            outs[t][...] = d
            outs[n + t][...] = nm
            outs[2 * n + t][...] = nv

    vm = pl.BlockSpec(memory_space=pltpu.VMEM)
    shapes = [jax.ShapeDtypeStruct(w.shape, F32) for w in ws] * 3
    outs = pl.pallas_call(body, in_specs=[vm] * (4 * n), out_specs=[vm] * (3 * n), out_shape=shapes, name=name)(
        *ws, *gs, *ms, *vs)
    return outs[:n], outs[n:2 * n], outs[2 * n:]


_SMALL = ["norm_g", "mem_norm_g", "xa_q_norm_g", "xa_k_norm_g", "dn_a_log", "dn_dt_bias", "dn_out_norm_g",
          "sb_q_norm_g", "sb_k_norm_g"]


def _rows128(a):
    flat = a.reshape(-1)
    pad = -flat.shape[0] % LANE
    if pad:
        flat = jnp.pad(flat, (0, pad))
    return flat.reshape(-1, LANE)


def _unrows(r, shape):
    return r.reshape(-1)[:math.prod(shape)].reshape(shape)


def kernel(x, mem, norm_g, mem_norm_g, mem_w_kv, xa_q_norm_g, xa_k_norm_g, w_out, dn_w_in, dn_conv_w, dn_a_log, dn_dt_bias, dn_out_norm_g, sb_w_in, sb_q_norm_g, sb_k_norm_g, loss_target, m_norm_g, m_mem_norm_g, m_mem_w_kv, m_xa_q_norm_g, m_xa_k_norm_g, m_w_out, m_dn_w_in, m_dn_conv_w, m_dn_a_log, m_dn_dt_bias, m_dn_out_norm_g, m_sb_w_in, m_sb_q_norm_g, m_sb_k_norm_g, v_norm_g, v_mem_norm_g, v_mem_w_kv, v_xa_q_norm_g, v_xa_k_norm_g, v_w_out, v_dn_w_in, v_dn_conv_w, v_dn_a_log, v_dn_dt_bias, v_dn_out_norm_g, v_sb_w_in, v_sb_q_norm_g, v_sb_k_norm_g):
    W = dict(norm_g=norm_g, mem_norm_g=mem_norm_g, mem_w_kv=mem_w_kv, xa_q_norm_g=xa_q_norm_g, xa_k_norm_g=xa_k_norm_g,
             w_out=w_out, dn_w_in=dn_w_in, dn_conv_w=dn_conv_w, dn_a_log=dn_a_log, dn_dt_bias=dn_dt_bias,
             dn_out_norm_g=dn_out_norm_g, sb_w_in=sb_w_in, sb_q_norm_g=sb_q_norm_g, sb_k_norm_g=sb_k_norm_g)
    M = dict(norm_g=m_norm_g, mem_norm_g=m_mem_norm_g, mem_w_kv=m_mem_w_kv, xa_q_norm_g=m_xa_q_norm_g,
             xa_k_norm_g=m_xa_k_norm_g, w_out=m_w_out, dn_w_in=m_dn_w_in, dn_conv_w=m_dn_conv_w, dn_a_log=m_dn_a_log,
             dn_dt_bias=m_dn_dt_bias, dn_out_norm_g=m_dn_out_norm_g, sb_w_in=m_sb_w_in, sb_q_norm_g=m_sb_q_norm_g,
             sb_k_norm_g=m_sb_k_norm_g)
    V = dict(norm_g=v_norm_g, mem_norm_g=v_mem_norm_g, mem_w_kv=v_mem_w_kv, xa_q_norm_g=v_xa_q_norm_g,
             xa_k_norm_g=v_xa_k_norm_g, w_out=v_w_out, dn_w_in=v_dn_w_in, dn_conv_w=v_dn_conv_w, dn_a_log=v_dn_a_log,
             dn_dt_bias=v_dn_dt_bias, dn_out_norm_g=v_dn_out_norm_g, sb_w_in=v_sb_w_in, sb_q_norm_g=v_sb_q_norm_g,
             sb_k_norm_g=v_sb_k_norm_g)
    names = ["norm_g", "mem_norm_g", "mem_w_kv", "xa_q_norm_g", "xa_k_norm_g", "w_out", "dn_w_in", "dn_conv_w",
             "dn_a_log", "dn_dt_bias", "dn_out_norm_g", "sb_w_in", "sb_q_norm_g", "sb_k_norm_g"]
    cx, cy, cc = lax.axis_index("x"), lax.axis_index("y"), lax.axis_index("c")
    slot = 2 * cx + cy
    half_r = D_MODEL // 2
    conv_cols = dn_conv_w.shape[2]

    w0s = jnp.pad(dn_w_in[0].astype(BF16), ((0, 0), (0, P0_SHARD_PAD - P0_SHARD))).reshape(2, half_r, P0_SHARD_PAD)
    w1s = sb_w_in[0].astype(BF16).reshape(2, half_r, SB_PROJ // N_CHIPS)
    convs = jnp.pad(dn_conv_w[0], ((0, 8 - DN_CONV), (0, 0))).reshape(8, 2, conv_cols // 2).transpose(1, 0, 2)
    c_idx = jnp.reshape(cc, (1,)).astype(jnp.int32)
    j_idx = jnp.reshape(slot, (1,)).astype(jnp.int32)
    own_a = [w0s, convs]
    own_b = [w1s, w_out.astype(BF16), mem_w_kv.astype(BF16)]
    gathered_a, own_b = lax.optimization_barrier((_gather_weights(own_a, name="gather_weights"), own_b))
    g0, gconv = [lax.dynamic_update_slice(g, o[None], (slot, 0, 0, 0)) for o, g in zip(own_a, gathered_a)]
    lands_b = [lax.dynamic_update_slice(lax.empty((N_CHIPS,) + o.shape, o.dtype), o[None], (slot, 0, 0, 0)) for o in own_b]
    send_b, recv_b, own_b, lands_b, token_b = _exchange_start(own_b, lands_b, _gather_plan, name="gather_late_start")

    def late_weights(after):
        _, lands = _exchange_wait(own_b, lands_b, send_b, recv_b, after, _gather_plan, name="gather_late_wait")
        g1, gout, gkv = _forward_halves(lands, name="gather_late_forward")
        return gkv, gout, g1

    rs = {}

    def scatter_start(tag, xs):
        from_sib = _swap_halves(xs, name=f"rs{tag}_swap")
        ps = [_add_halves(a, b, c_idx, name=f"rs{tag}_add{t}") for t, (a, b) in enumerate(zip(xs, from_sib))]
        rcv = [lax.empty((3,) + p.shape[1:], p.dtype) for p in ps]
        send, recv, ps, rcv, token = _exchange_start(ps, rcv, _scatter_plan, name=f"rs{tag}_scatter_start")
        rs[tag] = (ps, rcv, send, recv)
        return token

    def scatter_finish(tag, after):
        ps, rcv, send, recv = rs[tag]
        ps, rcv = _exchange_wait(ps, rcv, send, recv, after, _scatter_plan, name=f"rs{tag}_scatter_wait")
        return [_sum_slot(p, r, j_idx, name=f"rs{tag}_sum{t}") for t, (p, r) in enumerate(zip(ps, rcv))]

    def early_grads(layer, d_win, d_wout, d_wkv):
        if layer == 0:
            d_win = jnp.stack([jnp.pad(jnp.concatenate(
                [d_win[:, lo:hi] for lo, hi in _padded_pieces(s * P0_SHARD, (s + 1) * P0_SHARD)], axis=1),
                ((0, 0), (0, P0_SHARD_PAD - P0_SHARD))) for s in range(N_CHIPS)]).reshape(N_CHIPS, 2, half_r, P0_SHARD_PAD)
        return scatter_start(layer, [d_win, d_wout.reshape(N_CHIPS, 2, -1, D_MODEL), d_wkv.reshape(N_CHIPS, 2, -1, 2 * XA_WIDTH)])

    shards0 = g0.reshape(N_CHIPS, D_MODEL, P0_SHARD_PAD)
    z = lambda n: jnp.zeros((D_MODEL, n), BF16)
    w_in0 = jnp.concatenate(
        [shards0[s][:, lo:hi] for s, lo, hi in _true_pieces(0, _C_QKV) + _true_pieces(_C_QKV + 2 * DN_V_HEADS, DN_PROJ)]
        + [shards0[s][:, lo:hi] for s, lo, hi in _true_pieces(_C_QKV, _C_QKV + DN_V_HEADS)] + [z(LANE - DN_V_HEADS)]
        + [shards0[s][:, lo:hi] for s, lo, hi in _true_pieces(_C_QKV + DN_V_HEADS, _C_QKV + 2 * DN_V_HEADS)]
        + [z(P0 - P0_AB - LANE - DN_V_HEADS)], axis=1)
    conv_f = gconv.transpose(2, 0, 1, 3).reshape(8, N_CHIPS * conv_cols)[:DN_CONV]

    loss_vec, grad_x, g = _local_step(
        x[0], mem[0], loss_target[0], norm_g + token_b[0, 0], mem_norm_g, xa_q_norm_g, xa_k_norm_g, w_in0, conv_f,
        dn_a_log[0], dn_dt_bias[0], dn_out_norm_g[0], sb_q_norm_g[0], sb_k_norm_g[0], late_weights, early_grads)

    mine1 = scatter_finish(1, grad_x)
    theirs1 = _swap_with_sibling(mine1, name="rs1_join")
    big1 = [("sb_w_in", None), ("w_out", 1), ("mem_w_kv", 1)]
    big0 = [("dn_w_in", None), ("w_out", 0), ("mem_w_kv", 0)]

    out_g, out_d, out_m, out_v = {}, {}, {}, {}
    partial = {}

    def adamw_big(big, mine, theirs):
        for (n, layer), mine_g, their_g in zip(big, mine, theirs):
            layers = 1 if layer is None else 2
            view = (layers, 2) + mine_g.shape
            partial[n] = _adamw_halves(W[n].reshape(view), mine_g, their_g, M[n].reshape(view), V[n].reshape(view), c_idx,
                                       layer=layer or 0, into=partial.get(n), name=f"adamw_{n}" + ("" if layer is None else str(layer)))
        return [partial[n][0] for n, _ in big]

    done1 = lax.optimization_barrier(tuple(adamw_big(big1, mine1, theirs1)))[-1]
    mine0 = scatter_finish(0, done1)
    mine0[0] = mine0[0][:, :P0_SHARD]

    parts, _ = lax.optimization_barrier(([_rows128(g[n]) for n in _SMALL] + [_rows128(g["dn_conv_w"]), loss_vec], mine0[0]))
    red = _all_reduce_small(parts, name="all_reduce_small")
    small_rows = dict(zip(_SMALL, red))
    conv_full = red[len(_SMALL)].reshape(DN_CONV, N_CHIPS * conv_cols)
    small_rows["dn_conv_w"] = _rows128(lax.dynamic_slice_in_dim(conv_full, slot * conv_cols, conv_cols, axis=1))
    loss = red[-1][0, 0]

    adamw_big(big0, mine0, _swap_with_sibling(mine0, name="rs0_join"))
    for n, outs in partial.items():
        out_g[n], out_d[n], out_m[n], out_v[n] = [o.reshape(W[n].shape) for o in outs]
    small_names = _SMALL + ["dn_conv_w"]
    ds, nms, nvs = _adamw_parts([_rows128(W[n]) for n in small_names], [small_rows[n] for n in small_names],
                                [_rows128(M[n]) for n in small_names], [_rows128(V[n]) for n in small_names], name="adamw_small")
    for n, d, nm, nv in zip(small_names, ds, nms, nvs):
        shp = W[n].shape
        out_g[n], out_d[n], out_m[n], out_v[n] = [_unrows(r, shp) for r in (small_rows[n], d, nm, nv)]

    return (loss, grad_x[None], *[out_g[n] for n in names], *[out_d[n] for n in names], *[out_m[n] for n in names],
            *[out_v[n] for n in names])
```

```python
import math

import jax
import jax.numpy as jnp
from jax import lax
from jax.experimental import pallas as pl
from jax.experimental.pallas import tpu as pltpu

F32 = jnp.float32
BF16 = jnp.bfloat16
HI = lax.Precision.HIGHEST
MESH = pl.DeviceIdType.MESH

D_MODEL = 2048
INNER = 4096
XA_WIDTH = 1024
XA_HEADS = 4
XA_DIM = 256
MIX_WIDTH = 3072
HEAD_DIM = 128
DN_V_HEADS = 24
DN_QK_WIDTH = 1536
DN_CONV = 4
DN_PROJ = 11312
SB_PROJ = 14336
EPS = 1e-6
N_CHIPS = 4

CH = 128
LANE = 128

P0_XQ = 6144
P0_Z = 7168
P0_AB = 11264
P0 = 11776
P0_SHARD = DN_PROJ // N_CHIPS
P0_SHARD_PAD = 2944
P1_XQ = 9216
P1_Z = 10240
P1 = SB_PROJ

ADAM_LR = 0.001
ADAM_B1 = 0.9
ADAM_B2 = 0.999
ADAM_EPS = 1e-08
ADAM_WD = 0.01
ADAM_STEP = 10

VMEM_LIMIT = 48 * 1024 * 1024


def _cp(sem=None, **kw):
    return pltpu.CompilerParams(dimension_semantics=sem, vmem_limit_bytes=VMEM_LIMIT, **kw)


def _bdot(a, b, dims):
    return lax.dot_general(a.astype(BF16), b.astype(BF16), (dims, ((), ())), preferred_element_type=F32)


def _fdot(a, b, dims):
    return lax.dot_general(a, b, (dims, ((), ())), precision=HI, preferred_element_type=F32)


NN = ((1,), (0,))
NT = ((1,), (1,))
TN = ((0,), (0,))


def _sigmoid(x):
    return 1.0 / (1.0 + jnp.exp(-x))


def _softplus(x):
    return jnp.maximum(x, 0.0) + jnp.log(1.0 + jnp.exp(-jnp.abs(x)))


def _iota2(shape, axis):
    return lax.broadcasted_iota(jnp.int32, shape, axis)


MM_FULL_K = 4096
MM_BLOCK_BYTES = 4 * 1024 * 1024


def _matmul(a, b, *, ta=False, tb=False, out_dtype=F32, res=None, name, n=None, tm=None, tn=None, tk=None,
            b_spec=None, o_spec=None, o_shape=None):
    a_segs = list(a) if isinstance(a, (list, tuple)) else [a]
    b_segs = list(b) if isinstance(b, (list, tuple)) else [b]
    a0, b0 = a_segs[0], b_segs[0]
    M = a0.shape[1] if ta else a0.shape[0]
    K = a0.shape[0] if ta else sum(s.shape[1] for s in a_segs)
    if n is None:
        n = b0.shape[0] if tb else sum(s.shape[1] for s in b_segs)
    N = n
    dims = ((0,) if ta else (1,), (1,) if tb else (0,))
    has_res = res is not None
    flat = lambda v: v.reshape(-1, v.shape[-1])
    o_shape = o_shape or jax.ShapeDtypeStruct((M, N), out_dtype)

    def seg_specs(segs, tile, block, pos):
        specs, ranges, off = [], [], 0
        for s in segs:
            cnt = s.shape[1] // tile
            assert s.shape[1] % tile == 0, (name, s.shape, tile)

            def imap(*g, off=off, cnt=cnt):
                t = jnp.clip(g[pos] - off, 0, cnt - 1)
                return (g[0], t) if pos == 2 else (0, t)

            specs.append(pl.BlockSpec(block, imap))
            ranges.append((off, off + cnt))
            off += cnt
        return specs, ranges

    if K <= MM_FULL_K:
        assert len(a_segs) == 1
        tm = tm or min(M, 1024, max(256, MM_BLOCK_BYTES // (K * a0.dtype.itemsize)))
        tn = tn or min(N, 512)
        assert M % tm == 0 and N % tn == 0, (name, M, N, K, tm, tn)
        nb = len(b_segs)
        if b_spec is not None:
            b_specs, b_ranges = [b_spec], [(0, N // tn)]
        elif nb > 1:
            assert not tb
            b_specs, b_ranges = seg_specs(b_segs, tn, (K, tn), 1)
        else:
            b_specs = [pl.BlockSpec((tn, K), lambda i, j: (j, 0)) if tb else pl.BlockSpec((K, tn), lambda i, j: (0, j))]
            b_ranges = [(0, N // tn)]

        def body_full(*refs):
            a_ref, b_refs = refs[0], refs[1:1 + nb]
            r_ref = refs[1 + nb] if has_res else None
            o_ref = refs[-1]
            j = pl.program_id(1)
            for b_ref, (lo, hi) in zip(b_refs, b_ranges):
                def emit(b_ref=b_ref):
                    r = _bdot(a_ref[...], flat(b_ref[...]), dims)
                    if has_res:
                        r = r + r_ref[...]
                    o_ref[...] = r.astype(o_ref.dtype).reshape(o_ref.shape)
                if nb == 1:
                    emit()
                else:
                    pl.when(jnp.logical_and(j >= lo, j < hi))(emit)

        a_spec = pl.BlockSpec((K, tm), lambda i, j: (0, i)) if ta else pl.BlockSpec((tm, K), lambda i, j: (i, 0))
        o_spec = o_spec or pl.BlockSpec((tm, tn), lambda i, j: (i, j))
        r_spec = [pl.BlockSpec((tm, tn), lambda i, j: (i, j))] if has_res else []
        return pl.pallas_call(
            body_full, grid=(M // tm, N // tn), in_specs=[a_spec] + b_specs + r_spec, out_specs=o_spec, out_shape=o_shape,
            compiler_params=_cp(("parallel", "arbitrary")), name=name)(*([a0] + b_segs + ([res] if has_res else [])))

    assert tb and not ta and len(b_segs) == 1
    tm, tn = tm or min(M, 1024), tn or min(N, 1024)
    tk = tk or (1024 if all(s.shape[1] % 1024 == 0 for s in a_segs) else 512)
    assert M % tm == 0 and N % tn == 0 and K % tk == 0, (name, M, N, K, tm, tn, tk)
    nk = K // tk
    na = len(a_segs)
    if na > 1:
        a_specs, a_ranges = seg_specs(a_segs, tk, (tm, tk), 2)
    else:
        a_specs, a_ranges = [pl.BlockSpec((tm, tk), lambda i, j, k: (i, k))], [(0, nk)]
    b_spec = b_spec or pl.BlockSpec((tn, tk), lambda i, j, k: (j, k))

    def body(*refs):
        a_refs, b_ref = refs[:na], refs[na]
        r_ref = refs[na + 1] if has_res else None
        o_ref, acc = refs[-2], refs[-1]
        k = pl.program_id(2)

        @pl.when(k == 0)
        def _():
            acc[...] = jnp.zeros_like(acc)

        for a_ref, (lo, hi) in zip(a_refs, a_ranges):
            def emit(a_ref=a_ref):
                acc[...] += _bdot(a_ref[...], flat(b_ref[...]), dims)
            if na == 1:
                emit()
            else:
                pl.when(jnp.logical_and(k >= lo, k < hi))(emit)

        @pl.when(k == nk - 1)
        def _():
            r = acc[...]
            if has_res:
                r = r + r_ref[...]
            o_ref[...] = r.astype(o_ref.dtype).reshape(o_ref.shape)

    o_spec = o_spec or pl.BlockSpec((tm, tn), lambda i, j, k: (i, j))
    r_spec = [pl.BlockSpec((tm, tn), lambda i, j, k: (i, j))] if has_res else []
    return pl.pallas_call(
        body, grid=(M // tm, N // tn, nk), in_specs=a_specs + [b_spec] + r_spec, out_specs=o_spec, out_shape=o_shape,
        scratch_shapes=[pltpu.VMEM((tm, tn), F32)],
        compiler_params=_cp(("parallel", "parallel", "arbitrary")), name=name)(*(a_segs + [b0] + ([res] if has_res else [])))


def _rmsnorm_fwd(x, g, *, name, tm=256):
    S, Dm = x.shape
    tm = min(tm, S)

    def body(x_ref, g_ref, o_ref):
        xv = x_ref[...]
        r = lax.rsqrt(jnp.mean(xv * xv, axis=-1, keepdims=True) + EPS)
        o_ref[...] = (xv * r * g_ref[...]).astype(BF16)

    return pl.pallas_call(
        body, grid=(S // tm,), in_specs=[pl.BlockSpec((tm, Dm), lambda i: (i, 0)), pl.BlockSpec((1, Dm), lambda i: (0, 0))],
        out_specs=pl.BlockSpec((tm, Dm), lambda i: (i, 0)), out_shape=jax.ShapeDtypeStruct((S, Dm), BF16),
        compiler_params=_cp(("parallel",)), name=name)(x, g.reshape(1, Dm))


def _rmsnorm_bwd(dh, x, g, dres, *, name, tm=256):
    S, Dm = x.shape
    tm = min(tm, S)
    want_dx = dres is not None

    def body(*refs):
        if want_dx:
            dh_ref, x_ref, g_ref, dr_ref, dx_ref, dg_ref = refs
        else:
            dh_ref, x_ref, g_ref, dg_ref = refs
        i = pl.program_id(0)
        xv = x_ref[...]
        dhv = dh_ref[...]
        r = lax.rsqrt(jnp.mean(xv * xv, axis=-1, keepdims=True) + EPS)
        y = xv * r
        part = jnp.sum(dhv * y, axis=0, keepdims=True)

        @pl.when(i == 0)
        def _():
            dg_ref[...] = jnp.zeros_like(dg_ref)

        dg_ref[...] += part
        if want_dx:
            dy = dhv * g_ref[...]
            dx_ref[...] = dr_ref[...] + r * (dy - y * jnp.mean(dy * y, axis=-1, keepdims=True))

    row = pl.BlockSpec((tm, Dm), lambda i: (i, 0))
    vec = pl.BlockSpec((1, Dm), lambda i: (0, 0))
    if want_dx:
        dx, dg = pl.pallas_call(
            body, grid=(S // tm,), in_specs=[row, row, vec, row], out_specs=(row, vec),
            out_shape=(jax.ShapeDtypeStruct((S, Dm), F32), jax.ShapeDtypeStruct((1, Dm), F32)),
            compiler_params=_cp(("arbitrary",)), name=name)(dh, x, g.reshape(1, Dm), dres)
        return dx, dg
    dg = pl.pallas_call(
        body, grid=(S // tm,), in_specs=[row, row, vec], out_specs=vec,
        out_shape=jax.ShapeDtypeStruct((1, Dm), F32), compiler_params=_cp(("arbitrary",)), name=name)(dh, x, g.reshape(1, Dm))
    return None, dg


GATE_TN = XA_WIDTH
GATE_MIX_TILES = MIX_WIDTH // GATE_TN


def _gate_cat_specs(tm):
    return [pl.BlockSpec((tm, GATE_TN), lambda i, j: (i, jnp.minimum(j, GATE_MIX_TILES - 1))),
            pl.BlockSpec((tm, GATE_TN), lambda i, j: (i, 0))]


def _gate_fwd(mix, xa, proj, z_off, *, name, tm=256):
    S = mix.shape[0]
    tm = min(tm, S)
    zb = z_off // GATE_TN

    def body(m_ref, x_ref, z_ref, y_ref):
        z = z_ref[...]
        c = jnp.where(pl.program_id(1) < GATE_MIX_TILES, m_ref[...], x_ref[...])
        y_ref[...] = (c * z * _sigmoid(z)).astype(BF16)

    blk = pl.BlockSpec((tm, GATE_TN), lambda i, j: (i, j))
    return pl.pallas_call(
        body, grid=(S // tm, INNER // GATE_TN),
        in_specs=_gate_cat_specs(tm) + [pl.BlockSpec((tm, GATE_TN), lambda i, j: (i, zb + j))],
        out_specs=blk, out_shape=jax.ShapeDtypeStruct((S, INNER), BF16),
        compiler_params=_cp(("parallel", "arbitrary")), name=name)(mix, xa, proj)


def _gate_bwd(dy, mix, xa, proj, z_off, *, name, tm=256):
    S = mix.shape[0]
    tm = min(tm, S)
    zb = z_off // GATE_TN

    def body(dy_ref, m_ref, x_ref, z_ref, dc_ref, dz_ref):
        z = z_ref[...]
        sg = _sigmoid(z)
        d = dy_ref[...]
        c = jnp.where(pl.program_id(1) < GATE_MIX_TILES, m_ref[...], x_ref[...])
        dc_ref[...] = d * z * sg
        dz_ref[...] = (d * c * sg * (1.0 + z * (1.0 - sg))).astype(BF16)

    blk = pl.BlockSpec((tm, GATE_TN), lambda i, j: (i, j))
    return pl.pallas_call(
        body, grid=(S // tm, INNER // GATE_TN),
        in_specs=[blk] + _gate_cat_specs(tm) + [pl.BlockSpec((tm, GATE_TN), lambda i, j: (i, zb + j))], out_specs=(blk, blk),
        out_shape=(jax.ShapeDtypeStruct((S, INNER), F32), jax.ShapeDtypeStruct((S, INNER), BF16)),
        compiler_params=_cp(("parallel", "arbitrary")), name=name)(dy, mix, xa, proj)


def _loss_head(x, target, *, name, tm=256):
    S, Dm = x.shape
    tm = min(tm, S)

    nt = S // tm

    def body(x_ref, t_ref, dx_ref, l_ref, acc):
        i = pl.program_id(0)
        e = x_ref[...] - t_ref[...]
        dx_ref[...] = e * (1.0 / Dm)

        @pl.when(i == 0)
        def _():
            acc[...] = jnp.zeros_like(acc)

        acc[...] += jnp.sum(e * e, axis=0, keepdims=True) * (0.5 / Dm)

        @pl.when(i == nt - 1)
        def _():
            l_ref[...] = jnp.sum(acc[...], axis=1, keepdims=True) + jnp.zeros((1, LANE), F32)

    row = pl.BlockSpec((tm, Dm), lambda i: (i, 0))
    return pl.pallas_call(
        body, grid=(nt,), in_specs=[row, row], out_specs=(row, pl.BlockSpec((1, LANE), lambda i: (0, 0))),
        out_shape=(jax.ShapeDtypeStruct((S, Dm), F32), jax.ShapeDtypeStruct((1, LANE), F32)),
        scratch_shapes=[pltpu.VMEM((1, Dm), F32)],
        compiler_params=_cp(("arbitrary",)), name=name)(x, target)


def _xa_norm(v, g):
    r = lax.rsqrt(jnp.mean(v * v, axis=-1, keepdims=True) + EPS)
    return v * r, r


def _xa_fwd(proj, xq_off, kv, gq, gk, *, name, tm=512):
    S = proj.shape[0]
    tm = min(tm, S)
    qb = xq_off // XA_DIM
    n_mem = kv.shape[0]
    scale = XA_DIM ** -0.5

    def body(q_ref, k_ref, v_ref, gq_ref, gk_ref, o_ref):
        qh, _ = _xa_norm(q_ref[...], None)
        kh, _ = _xa_norm(k_ref[...], None)
        qn = qh * gq_ref[...]
        kn = kh * gk_ref[...]
        s = _bdot(qn, kn, NT) * scale
        s = s - jnp.max(s, axis=-1, keepdims=True)
        p = jnp.exp(s)
        p = p / jnp.sum(p, axis=-1, keepdims=True)
        o_ref[...] = _bdot(p, v_ref[...], NN)

    vec = pl.BlockSpec((1, XA_DIM), lambda h, i: (0, 0))
    return pl.pallas_call(
        body, grid=(XA_HEADS, S // tm),
        in_specs=[pl.BlockSpec((tm, XA_DIM), lambda h, i: (i, qb + h)),
                  pl.BlockSpec((n_mem, XA_DIM), lambda h, i: (0, h)),
                  pl.BlockSpec((n_mem, XA_DIM), lambda h, i: (0, XA_HEADS + h)), vec, vec],
        out_specs=pl.BlockSpec((tm, XA_DIM), lambda h, i: (i, h)),
        out_shape=jax.ShapeDtypeStruct((S, XA_WIDTH), F32),
        compiler_params=_cp(("parallel", "parallel")), name=name)(proj, kv, kv, gq.reshape(1, XA_DIM), gk.reshape(1, XA_DIM))


def _xa_bwd(dcat, proj, xq_off, kv, gq, gk, *, name, tm=512):
    S = proj.shape[0]
    tm = min(tm, S)
    nt = S // tm
    qb = xq_off // XA_DIM
    db = MIX_WIDTH // XA_DIM
    n_mem = kv.shape[0]
    scale = XA_DIM ** -0.5

    def body(d_ref, q_ref, k_ref, v_ref, gq_ref, gk_ref, dq_ref, dk_ref, dv_ref, dgq_ref, dgk_ref, dkn_acc):
        h = pl.program_id(0)
        i = pl.program_id(1)
        q = q_ref[...]
        k = k_ref[...]
        qh, rq = _xa_norm(q, None)
        kh, rk = _xa_norm(k, None)
        gqv = gq_ref[...]
        gkv = gk_ref[...]
        qn = qh * gqv
        kn = kh * gkv
        s = _bdot(qn, kn, NT) * scale
        s = s - jnp.max(s, axis=-1, keepdims=True)
        p = jnp.exp(s)
        p = p / jnp.sum(p, axis=-1, keepdims=True)
        d = d_ref[...]
        dp = _bdot(d, v_ref[...], NT)
        ds = p * (dp - jnp.sum(dp * p, axis=-1, keepdims=True)) * scale
        dqn = _bdot(ds, kn, NN)

        @pl.when(i == 0)
        def _():
            dkn_acc[...] = jnp.zeros_like(dkn_acc)
            dv_ref[...] = jnp.zeros_like(dv_ref)

        @pl.when(jnp.logical_and(i == 0, h == 0))
        def _():
            dgq_ref[...] = jnp.zeros_like(dgq_ref)
            dgk_ref[...] = jnp.zeros_like(dgk_ref)

        dkn_acc[...] += _bdot(ds, qn, TN)
        dv_ref[...] += _bdot(p, d, TN)
        dgq_ref[...] += jnp.sum(dqn * qh, axis=0, keepdims=True)
        dy = dqn * gqv
        dq_ref[...] = (rq * (dy - qh * jnp.mean(dy * qh, axis=-1, keepdims=True))).astype(BF16)

        @pl.when(i == nt - 1)
        def _():
            dkn = dkn_acc[...]
            dgk_ref[...] += jnp.sum(dkn * kh, axis=0, keepdims=True)
            dyk = dkn * gkv
            dk_ref[...] = rk * (dyk - kh * jnp.mean(dyk * kh, axis=-1, keepdims=True))

    vec = pl.BlockSpec((1, XA_DIM), lambda h, i: (0, 0))
    kblk = pl.BlockSpec((n_mem, XA_DIM), lambda h, i: (0, h))
    vblk = pl.BlockSpec((n_mem, XA_DIM), lambda h, i: (0, XA_HEADS + h))
    dq, dk, dv, dgq, dgk = pl.pallas_call(
        body, grid=(XA_HEADS, nt),
        in_specs=[pl.BlockSpec((tm, XA_DIM), lambda h, i: (i, db + h)),
                  pl.BlockSpec((tm, XA_DIM), lambda h, i: (i, qb + h)), kblk, vblk, vec, vec],
        out_specs=(pl.BlockSpec((tm, XA_DIM), lambda h, i: (i, h)), kblk, kblk, vec, vec),
        out_shape=(jax.ShapeDtypeStruct((S, XA_WIDTH), BF16), jax.ShapeDtypeStruct((n_mem, XA_WIDTH), F32),
                   jax.ShapeDtypeStruct((n_mem, XA_WIDTH), F32), jax.ShapeDtypeStruct((1, XA_DIM), F32),
                   jax.ShapeDtypeStruct((1, XA_DIM), F32)),
        scratch_shapes=[pltpu.VMEM((n_mem, XA_DIM), F32)],
        compiler_params=_cp(("arbitrary", "arbitrary")), name=name)(
            dcat, proj, kv, kv, gq.reshape(1, XA_DIM), gk.reshape(1, XA_DIM))
    return dq, jnp.concatenate([dk, dv], axis=1), dgq, dgk


SB_TQ = 256
SB_TK = 256
SB_HEADS = 24


SB_PAIR = 2
SB_PW = SB_PAIR * HEAD_DIM


def _hdot(a, b, dims, dot=None):
    dot = dot or _bdot
    n = a.shape[0] if a.ndim == 3 else b.shape[0]
    return jnp.stack([dot(a[i] if a.ndim == 3 else a, b[i] if b.ndim == 3 else b, dims) for i in range(n)])


def _sb_tile(qi, kj, t0, s0, masked):
    z = _hdot(qi, kj, NT)
    sp = _softplus(z)
    ls = z - sp
    if not masked:
        return -sp, ls, None
    mask = (s0 + _iota2(z.shape[1:], 1)) < (t0 + _iota2(z.shape[1:], 0))
    return jnp.where(mask, -sp, 0.0), ls, mask


def _dot2(x, tri):
    hi = x.astype(BF16)
    lo = (x - hi.astype(F32)).astype(BF16)
    plain = lambda u, v, dims: lax.dot_general(u, v, (dims, ((), ())), preferred_element_type=F32)
    return _hdot(hi, tri, NN, plain) + _hdot(lo, tri, NN, plain)


def _sb_heads(ref, rows=slice(None)):
    return jnp.stack([ref[rows, hh * HEAD_DIM:(hh + 1) * HEAD_DIM] for hh in range(SB_PAIR)])


def _sb_fwd(proj, gq, gk, *, name):
    S = proj.shape[0]
    tq, tk = min(SB_TQ, S), min(SB_TK, S)
    nq = S // tq
    scale = HEAD_DIM ** -0.5

    def body(q_ref, k_ref, v_ref, gq_ref, gk_ref, o_ref, tot_ref, qn_s, kn_s, v_s):
        q = _sb_heads(q_ref)
        k = _sb_heads(k_ref)
        qn_s[...] = (q * lax.rsqrt(jnp.mean(q * q, axis=-1, keepdims=True) + EPS) * (gq_ref[...] * scale)).astype(BF16)
        kn_s[...] = (k * lax.rsqrt(jnp.mean(k * k, axis=-1, keepdims=True) + EPS) * gk_ref[...]).astype(BF16)
        v_s[...] = _sb_heads(v_ref).astype(BF16)
        after = (_iota2((tk, tk), 0) > _iota2((tk, tk), 1)).astype(BF16)

        def qblock(i, _):
            rows = pl.ds(pl.multiple_of(i * tq, tq), tq)
            qi = qn_s[:, rows, :]
            jd = (i * tq) // tk

            def tile(j, acc, run, masked):
                cols = pl.ds(pl.multiple_of(j * tk, tk), tk)
                lr, ls, mask = _sb_tile(qi, kn_s[:, cols, :], i * tq, j * tk, masked)
                later = _dot2(lr, after) + run
                a = jnp.exp(ls + later)
                if masked:
                    a = jnp.where(mask, a, 0.0)
                acc = acc + _hdot(a, v_s[:, cols, :], NN)
                return acc, run + jnp.sum(lr, axis=-1, keepdims=True)

            acc, run = tile(jd, jnp.zeros((SB_PAIR, tq, HEAD_DIM), F32), jnp.zeros((SB_PAIR, tq, 1), F32), True)
            acc, run = lax.fori_loop(0, jd, lambda jj, c: tile(jd - 1 - jj, c[0], c[1], False), (acc, run))
            tot = run + jnp.zeros((SB_PAIR, tq, HEAD_DIM), F32)
            for hh in range(SB_PAIR):
                o_ref[rows, hh * HEAD_DIM:(hh + 1) * HEAD_DIM] = acc[hh]
                tot_ref[rows, hh * HEAD_DIM:(hh + 1) * HEAD_DIM] = tot[hh]
            return 0

        lax.fori_loop(0, nq, qblock, 0)

    npair = SB_HEADS // SB_PAIR
    vec = pl.BlockSpec((1, HEAD_DIM), lambda h: (0, 0))
    hb = lambda off: pl.BlockSpec((S, SB_PW), lambda h: (0, off + h), pipeline_mode=pl.Buffered(1))
    return pl.pallas_call(
        body, grid=(npair,), in_specs=[hb(0), hb(npair), hb(2 * npair), vec, vec],
        out_specs=(hb(0), hb(0)), out_shape=(jax.ShapeDtypeStruct((S, MIX_WIDTH), F32),) * 2,
        scratch_shapes=[pltpu.VMEM((SB_PAIR, S, HEAD_DIM), BF16)] * 3,
        compiler_params=_cp(("parallel",)), name=name)(proj, proj, proj, gq.reshape(1, HEAD_DIM), gk.reshape(1, HEAD_DIM))


def _sb_bwd(dmix, tot, proj, gq, gk, *, name):
    S = proj.shape[0]
    tq, tk = min(SB_TQ, S), min(SB_TK, S)
    nq = S // tq
    scale = HEAD_DIM ** -0.5

    def body(do_ref, o_ref, q_ref, k_ref, v_ref, gq_ref, gk_ref, dq_ref, dk_ref, dv_ref, dgq_ref, dgk_ref,
             qn_s, kn_s, v_s, dkn_s, dqn_s, dv_s):
        h = pl.program_id(0)
        q = _sb_heads(q_ref)
        k = _sb_heads(k_ref)
        rq = lax.rsqrt(jnp.mean(q * q, axis=-1, keepdims=True) + EPS)
        rk = lax.rsqrt(jnp.mean(k * k, axis=-1, keepdims=True) + EPS)
        gqv = gq_ref[...]
        gkv = gk_ref[...]
        qn_s[...] = (q * rq * (gqv * scale)).astype(BF16)
        kn_s[...] = (k * rk * gkv).astype(BF16)
        v_s[...] = _sb_heads(v_ref).astype(BF16)
        dkn_s[...] = jnp.zeros_like(dkn_s)
        dv_s[...] = jnp.zeros_like(dv_s)
        r_i = _iota2((tk, tk), 0)
        c_i = _iota2((tk, tk), 1)
        upto = (r_i <= c_i).astype(BF16)
        before = (r_i < c_i).astype(BF16)

        def qblock(i, _):
            rows = pl.ds(pl.multiple_of(i * tq, tq), tq)
            qi = qn_s[:, rows, :]
            doi = _sb_heads(do_ref, rows).astype(BF16)
            tot_i = jnp.max(_sb_heads(o_ref, rows), axis=-1, keepdims=True)
            jd = (i * tq) // tk

            def tile(j, dqn, run, run_b, masked):
                cols = pl.ds(pl.multiple_of(j * tk, tk), tk)
                kj = kn_s[:, cols, :]
                lr, ls, mask = _sb_tile(qi, kj, i * tq, j * tk, masked)
                later = tot_i - (_dot2(lr, upto) + run)
                a = jnp.exp(ls + later)
                if masked:
                    a = jnp.where(mask, a, 0.0)
                b = _hdot(doi, v_s[:, cols, :], NT) * a
                cum = _dot2(b, before) + run_b
                beta = jnp.exp(ls)
                dz = b * (1.0 - beta) - cum * beta
                if masked:
                    dz = jnp.where(mask, dz, 0.0)
                dzb = dz.astype(BF16)
                dv_s[:, cols, :] += _hdot(a, doi, TN)
                dkn_s[:, cols, :] += _hdot(dzb, qi, TN)
                dqn = dqn + _hdot(dzb, kj, NN)
                return dqn, run + jnp.sum(lr, axis=-1, keepdims=True), run_b + jnp.sum(b, axis=-1, keepdims=True)

            zero1 = jnp.zeros((SB_PAIR, tq, 1), F32)
            carry = lax.fori_loop(0, jd, lambda j, c: tile(j, c[0], c[1], c[2], False),
                                  (jnp.zeros((SB_PAIR, tq, HEAD_DIM), F32), zero1, zero1))
            dqn, _, _ = tile(jd, carry[0], carry[1], carry[2], True)
            dqn_s[:, rows, :] = dqn * scale
            return 0

        lax.fori_loop(0, nq, qblock, 0)

        @pl.when(h == 0)
        def _():
            dgq_ref[...] = jnp.zeros_like(dgq_ref)
            dgk_ref[...] = jnp.zeros_like(dgk_ref)

        heads_sum = lambda z: jnp.sum(jnp.sum(z, axis=1, keepdims=True), axis=0)
        dqn = dqn_s[...]
        qh = q * rq
        dgq_ref[...] += heads_sum(dqn * qh)
        dy = dqn * gqv
        dq = (rq * (dy - qh * jnp.mean(dy * qh, axis=-1, keepdims=True))).astype(BF16)
        dkn = dkn_s[...]
        kh = k * rk
        dgk_ref[...] += heads_sum(dkn * kh)
        dyk = dkn * gkv
        dk = (rk * (dyk - kh * jnp.mean(dyk * kh, axis=-1, keepdims=True))).astype(BF16)
        dv = dv_s[...].astype(BF16)
        for hh in range(SB_PAIR):
            lanes = slice(hh * HEAD_DIM, (hh + 1) * HEAD_DIM)
            dq_ref[:, lanes] = dq[hh]
            dk_ref[:, lanes] = dk[hh]
            dv_ref[:, lanes] = dv[hh]

    npair = SB_HEADS // SB_PAIR
    vec = pl.BlockSpec((1, HEAD_DIM), lambda h: (0, 0))
    hb = lambda off: pl.BlockSpec((S, SB_PW), lambda h: (0, off + h), pipeline_mode=pl.Buffered(1))
    dq, dk, dv, dgq, dgk = pl.pallas_call(
        body, grid=(npair,),
        in_specs=[hb(0), hb(0), hb(0), hb(npair), hb(2 * npair), vec, vec],
        out_specs=(hb(0), hb(0), hb(0), vec, vec),
        out_shape=(jax.ShapeDtypeStruct((S, MIX_WIDTH), BF16),) * 3 + (jax.ShapeDtypeStruct((1, HEAD_DIM), F32),) * 2,
        scratch_shapes=[pltpu.VMEM((SB_PAIR, S, HEAD_DIM), BF16)] * 3 + [pltpu.VMEM((SB_PAIR, S, HEAD_DIM), F32)] * 3,
        compiler_params=_cp(("arbitrary",)), name=name)(
            dmix, tot, proj, proj, proj, gq.reshape(1, HEAD_DIM), gk.reshape(1, HEAD_DIM))
    return [dq, dk, dv], dgq, dgk


def _shift_down(x, k):
    if k == 0:
        return x
    r = pltpu.roll(x, k, 0)
    return jnp.where(_iota2(x.shape, 0) >= k, r, 0.0)


def _shift_up(x, k):
    if k == 0:
        return x
    n = x.shape[0]
    r = pltpu.roll(x, n - k, 0)
    return jnp.where(_iota2(x.shape, 0) < n - k, r, 0.0)


def _conv(x, w):
    c = w[DN_CONV - 1] * x
    for k in range(1, DN_CONV):
        c = c + w[DN_CONV - 1 - k] * _shift_down(x, k)
    return c


def _dn_pre_fwd(proj, conv_w, col0, ncols, *, l2, scale, name):
    S = proj.shape[0]
    cb = col0 // HEAD_DIM

    def body(x_ref, w_ref, o_ref):
        c = _conv(x_ref[...], [w_ref[k:k + 1, :] for k in range(DN_CONV)])
        a = c * _sigmoid(c)
        if l2:
            a = a * (lax.rsqrt(jnp.sum(a * a, axis=-1, keepdims=True) + EPS) * scale)
        o_ref[...] = a

    return pl.pallas_call(
        body, grid=(ncols // HEAD_DIM,),
        in_specs=[pl.BlockSpec((S, HEAD_DIM), lambda j: (0, cb + j)), pl.BlockSpec((DN_CONV, HEAD_DIM), lambda j: (0, cb + j))],
        out_specs=pl.BlockSpec((S, HEAD_DIM), lambda j: (0, j)), out_shape=jax.ShapeDtypeStruct((S, ncols), F32),
        compiler_params=_cp(("parallel",)), name=name)(proj, conv_w)


def _dn_pre_bwd(dout, proj, conv_w, col0, ncols, *, l2, scale, name):
    S = proj.shape[0]
    cb = col0 // HEAD_DIM
    dw_in = HEAD_DIM

    def body(d_ref, x_ref, w_ref, dx_ref, dw_ref):
        x = x_ref[...]
        w = [w_ref[k:k + 1, :] for k in range(DN_CONV)]
        c = _conv(x, w)
        sg = _sigmoid(c)
        a = c * sg
        d = d_ref[...]
        if l2:
            r = lax.rsqrt(jnp.sum(a * a, axis=-1, keepdims=True) + EPS)
            y = a * r
            d = d * scale
            d = r * (d - y * jnp.sum(d * y, axis=-1, keepdims=True))
        dc = d * sg * (1.0 + c * (1.0 - sg))
        dx = w[DN_CONV - 1] * dc
        for k in range(1, DN_CONV):
            dx = dx + w[DN_CONV - 1 - k] * _shift_up(dc, k)
        dx_ref[...] = dx.astype(BF16)
        for k in range(DN_CONV):
            dw_ref[3 - k:4 - k, :] = jnp.sum(dc * _shift_down(x, k), axis=0, keepdims=True)

    return pl.pallas_call(
        body, grid=(ncols // HEAD_DIM,),
        in_specs=[pl.BlockSpec((S, dw_in), lambda j: (0, j)), pl.BlockSpec((S, HEAD_DIM), lambda j: (0, cb + j)),
                  pl.BlockSpec((DN_CONV, HEAD_DIM), lambda j: (0, cb + j))],
        out_specs=(pl.BlockSpec((S, HEAD_DIM), lambda j: (0, j)), pl.BlockSpec((DN_CONV, HEAD_DIM), lambda j: (0, j))),
        out_shape=(jax.ShapeDtypeStruct((S, ncols), BF16), jax.ShapeDtypeStruct((DN_CONV, ncols), F32)),
        compiler_params=_cp(("parallel",)), name=name)(dout, proj, conv_w)


def _dn_ab_fwd(proj, a_log, dt_bias, *, name, tm=512):
    S = proj.shape[0]
    tm = min(tm, S)
    ab = P0_AB // LANE

    def body(a_ref, b_ref, al_ref, dt_ref, g_ref, be_ref):
        g_ref[...] = -jnp.exp(al_ref[...]) * _softplus(a_ref[...] + dt_ref[...])
        be_ref[...] = _sigmoid(b_ref[...])

    vec = pl.BlockSpec((1, LANE), lambda i: (0, 0))
    out = pl.BlockSpec((tm, LANE), lambda i: (i, 0))
    return pl.pallas_call(
        body, grid=(S // tm,),
        in_specs=[pl.BlockSpec((tm, LANE), lambda i: (i, ab)), pl.BlockSpec((tm, LANE), lambda i: (i, ab + 1)), vec, vec],
        out_specs=(out, out), out_shape=(jax.ShapeDtypeStruct((S, LANE), F32),) * 2,
        compiler_params=_cp(("parallel",)), name=name)(proj, proj, a_log, dt_bias)


def _dn_ab_bwd(dg, dbeta, proj, a_log, dt_bias, *, name, tm=512):
    S = proj.shape[0]
    tm = min(tm, S)
    ab = P0_AB // LANE

    def body(dg_ref, db_ref, a_ref, b_ref, al_ref, dt_ref, dab_ref, dal_ref, ddt_ref):
        i = pl.program_id(0)
        ea = jnp.exp(al_ref[...])
        u = a_ref[...] + dt_ref[...]
        dgv = dg_ref[...]
        da = dgv * (-ea) * _sigmoid(u)
        be = _sigmoid(b_ref[...])
        dab_ref[:, 0:LANE] = da.astype(BF16)
        dab_ref[:, LANE:2 * LANE] = (db_ref[...] * be * (1.0 - be)).astype(BF16)
        dab_ref[:, 2 * LANE:] = jnp.zeros((tm, 2 * LANE), BF16)

        @pl.when(i == 0)
        def _():
            dal_ref[...] = jnp.zeros_like(dal_ref)
            ddt_ref[...] = jnp.zeros_like(ddt_ref)

        dal_ref[...] += jnp.sum(dgv * (-ea) * _softplus(u), axis=0, keepdims=True)
        ddt_ref[...] += jnp.sum(da, axis=0, keepdims=True)

    vec = pl.BlockSpec((1, LANE), lambda i: (0, 0))
    row = pl.BlockSpec((tm, LANE), lambda i: (i, 0))
    return pl.pallas_call(
        body, grid=(S // tm,),
        in_specs=[row, row, pl.BlockSpec((tm, LANE), lambda i: (i, ab)), pl.BlockSpec((tm, LANE), lambda i: (i, ab + 1)), vec, vec],
        out_specs=(pl.BlockSpec((tm, 4 * LANE), lambda i: (i, 0)), vec, vec),
        out_shape=(jax.ShapeDtypeStruct((S, 4 * LANE), BF16), jax.ShapeDtypeStruct((1, LANE), F32),
                   jax.ShapeDtypeStruct((1, LANE), F32)),
        compiler_params=_cp(("arbitrary",)), name=name)(dg, dbeta, proj, proj, a_log, dt_bias)


def _dot3(a, b):
    ah = a.astype(BF16)
    al = (a - ah.astype(F32)).astype(BF16)
    bh = b.astype(BF16)
    bl = (b - bh.astype(F32)).astype(BF16)
    d = lambda u, v: lax.dot_general(u, v, (NN, ((), ())), preferred_element_type=F32)
    return d(ah, bh) + (d(ah, bl) + d(al, bh))


DN_PAIR = 4
DN_QK = DN_PAIR // 2


def _dn_qk_heads(ref, rows):
    return jnp.stack([ref[rows, (hh // 2) * HEAD_DIM:(hh // 2 + 1) * HEAD_DIM] for hh in range(DN_PAIR)])


def _dn_big(shape, imap):
    return pl.BlockSpec(shape, imap, pipeline_mode=pl.Buffered(1))


_pdot = _hdot


def _tri_inverse(a):
    eye = (_iota2((CH, CH), 0) == _iota2((CH, CH), 1)).astype(F32)
    d3 = lambda u, v: jnp.stack([_dot3(u[i], v[i]) for i in range(DN_PAIR)])
    t = eye - a
    x = d3(a, a)
    n = 2
    while True:
        t = t + d3(t, x)
        n *= 2
        if n >= CH:
            break
        x = d3(x, x)
    return t


def _pick_col(m, n):
    return jnp.sum(jnp.where(_iota2(m.shape, 2) == n, m, 0.0), axis=2, keepdims=True)


def _dn_chunk_common(kk, qk, gc_c, gc_r, be_c):
    r_i = _iota2((CH, CH), 0)
    c_i = _iota2((CH, CH), 1)
    incl = r_i >= c_i
    strict = r_i > c_i
    dec = jnp.exp(jnp.where(incl, gc_c - gc_r, -1e30))
    e = jnp.exp(gc_c)
    gl = jnp.sum(jnp.where(_iota2((1, CH), 1) == CH - 1, gc_r, 0.0), axis=-1, keepdims=True)
    kds = jnp.exp(gl - gc_c)
    cd = jnp.exp(gl)
    a = jnp.where(strict, be_c * kk * dec, 0.0)
    p = qk * dec
    return dict(incl=incl, strict=strict, dec=dec, e=e, kds=kds, cd=cd, kk=kk, a=a, qk=qk, p=p)


def _dn_decay_tables(g_ref, b_ref, gcr, gcc, bcc):
    r_i = _iota2((CH, CH), 0)
    c_i = _iota2((CH, CH), 1)
    lc = (r_i >= c_i).astype(F32)
    eye = (r_i == c_i).astype(F32)
    for hh in range(DN_PAIR):
        g_rows_v = g_ref[hh]
        gcr[hh] = _fdot(g_rows_v, lc, NT)
        gcc[hh] = _fdot(lc, g_rows_v, NT)
        bcc[hh] = _fdot(eye, b_ref[hh], NT)
    return lc


def _dn_core_fwd(qn, kn, vc, g_rows, b_rows, out_g, *, name):
    S = qn.shape[0]
    nc = S // CH

    def body(q_ref, k_ref, v_ref, g_ref, b_ref, og_ref, o_ref, st_ref, t_ref, gcr, gcc, bcc):
        _dn_decay_tables(g_ref, b_ref, gcr, gcc, bcc)
        ogv = og_ref[...]

        def chunk(n, states):
            rows = pl.ds(pl.multiple_of(n * CH, CH), CH)
            q = _dn_qk_heads(q_ref, rows)
            k = _dn_qk_heads(k_ref, rows)
            kk = _pdot(k, k, NT)
            qk = _pdot(q, k, NT)
            v = jnp.stack([v_ref[rows, hh * HEAD_DIM:(hh + 1) * HEAD_DIM] for hh in range(DN_PAIR)])
            gc_c = _pick_col(gcc[...], n)
            be_c = _pick_col(bcc[...], n)
            gc_r = gcr[:, pl.ds(n, 1), :]
            c = _dn_chunk_common(kk, qk, gc_c, gc_r, be_c)
            t = _tri_inverse(c["a"])
            u0 = _pdot(t, be_c * v, NN)
            w = _pdot(t, (be_c * c["e"]) * k, NN)
            u = u0 - _pdot(w, states, NN)
            o = _pdot(c["e"] * q, states, NN) + _pdot(c["p"], u, NN)
            on = o * lax.rsqrt(jnp.mean(o * o, axis=-1, keepdims=True) + EPS) * ogv
            for hh in range(DN_PAIR):
                st_ref[hh, n] = states[hh]
                t_ref[hh, n] = t[hh]
                o_ref[rows, hh * HEAD_DIM:(hh + 1) * HEAD_DIM] = on[hh]
            return c["cd"] * states + _pdot(c["kds"] * k, u, TN)

        lax.fori_loop(0, nc, chunk, jnp.zeros((DN_PAIR, HEAD_DIM, HEAD_DIM), F32))

    qk_spec = _dn_big((S, DN_QK * HEAD_DIM), lambda h: (0, h))
    v_spec = _dn_big((S, DN_PAIR * HEAD_DIM), lambda h: (0, h))
    rows_spec = pl.BlockSpec((DN_PAIR, LANE, CH), lambda h: (h, 0, 0))
    return pl.pallas_call(
        body, grid=(DN_V_HEADS // DN_PAIR,),
        in_specs=[qk_spec, qk_spec, v_spec, rows_spec, rows_spec, pl.BlockSpec((1, HEAD_DIM), lambda h: (0, 0))],
        out_specs=(v_spec, _dn_big((DN_PAIR, nc, HEAD_DIM, HEAD_DIM), lambda h: (h, 0, 0, 0)),
                   _dn_big((DN_PAIR, nc, CH, CH), lambda h: (h, 0, 0, 0))),
        out_shape=(jax.ShapeDtypeStruct((S, MIX_WIDTH), F32), jax.ShapeDtypeStruct((DN_V_HEADS, nc, HEAD_DIM, HEAD_DIM), F32),
                   jax.ShapeDtypeStruct((DN_V_HEADS, nc, CH, CH), F32)),
        scratch_shapes=[pltpu.VMEM((DN_PAIR, LANE, CH), F32), pltpu.VMEM((DN_PAIR, CH, LANE), F32),
                        pltpu.VMEM((DN_PAIR, CH, LANE), F32)],
        compiler_params=_cp(("parallel",)), name=name)(qn, kn, vc, g_rows, b_rows, out_g.reshape(1, HEAD_DIM))


def _dn_chunk_bwd(q, k, v, kk, qk, state, t, gc_c, gc_r, be_c, don, ogv, ds_next):
    ones = jnp.ones((CH, LANE), F32)
    last_row = _iota2((CH, 1), 0) == CH - 1
    rowsum = lambda z: jnp.sum(z, axis=-1, keepdims=True)
    colsum = lambda z: jnp.sum(z, axis=-2, keepdims=True)
    c = _dn_chunk_common(kk, qk, gc_c, gc_r, be_c)
    e, kds, cd, dec, a, p = c["e"], c["kds"], c["cd"], c["dec"], c["a"], c["p"]
    vb = be_c * v
    kbe = (be_c * e) * k
    u0 = _pdot(t, vb, NN)
    w = _pdot(t, kbe, NN)
    u = u0 - _pdot(w, state, NN)
    qd = e * q
    kd = kds * k
    o = _pdot(qd, state, NN) + _pdot(p, u, NN)
    r = lax.rsqrt(jnp.mean(o * o, axis=-1, keepdims=True) + EPS)
    y = o * r
    dog = colsum(don * y)
    dy = don * ogv
    d_o = r * (dy - y * jnp.mean(dy * y, axis=-1, keepdims=True))
    du = _pdot(p, d_o, TN) + _pdot(kd, ds_next, NN)
    dqd = _pdot(d_o, state, NT)
    dstate = _pdot(qd, d_o, TN) + cd * ds_next - _pdot(w, du, TN)
    dcd = colsum(rowsum(ds_next * state))
    dkd = _pdot(u, ds_next, NT)
    dw = -_pdot(du, state, NT)
    dvb = _pdot(t, du, TN)
    dkbe = _pdot(t, dw, TN)
    da = -jnp.where(c["strict"], _pdot(dvb, u0, NT) + _pdot(dkbe, w, NT), 0.0)
    dp = jnp.where(c["incl"], _pdot(d_o, u, NT), 0.0)
    gmat = da * a + dp * p
    dad = da * dec
    x = be_c * dad
    dpd = dp * dec
    dk = _pdot(x, k, NN) + _pdot(x, k, TN) + _pdot(dpd, q, TN)
    dq = _pdot(dpd, k, NN) + e * dqd
    dbe = rowsum(dad * c["kk"])
    dgc = rowsum(gmat) + rowsum(dqd * q) * e
    rk = rowsum(dkd * k) * kds
    dk = dk + kds * dkd
    dgc = dgc - rk
    dgl = colsum(rk) + dcd * cd
    sk = rowsum(dkbe * k)
    dk = dk + (be_c * e) * dkbe
    dbe = dbe + sk * e + rowsum(dvb * v)
    dgc = dgc + sk * be_c * e
    dgc = dgc + jnp.where(last_row, dgl, 0.0)
    dgc = dgc - _pdot(gmat, ones, TN, dot=_fdot)
    return dq, dk, be_c * dvb, dgc, dbe, dog, dstate


def _dn_core_bwd(dmix, qn, kn, vc, g_rows, b_rows, out_g, states, tinv, *, name):
    S = qn.shape[0]
    nc = S // CH

    def body(do_ref, q_ref, k_ref, v_ref, g_ref, b_ref, og_ref, st_ref, t_ref,
             dq_ref, dk_ref, dv_ref, dg_ref, db_ref, dog_ref, gcr, gcc, bcc, dgc_acc):
        h = pl.program_id(0)
        lc = _dn_decay_tables(g_ref, b_ref, gcr, gcc, bcc)
        ogv = og_ref[...]
        dgc_acc[...] = jnp.zeros_like(dgc_acc)
        db_ref[...] = jnp.zeros_like(db_ref)
        lane_n = _iota2((CH, LANE), 1)

        @pl.when(h == 0)
        def _():
            dog_ref[...] = jnp.zeros_like(dog_ref)

        def chunk(m, carry):
            ds_nexts, dog = carry
            n = nc - 1 - m
            rows = pl.ds(pl.multiple_of(n * CH, CH), CH)
            q = _dn_qk_heads(q_ref, rows)
            k = _dn_qk_heads(k_ref, rows)
            kk = _pdot(k, k, NT)
            qk = _pdot(q, k, NT)
            heads = lambda ref: jnp.stack([ref[rows, hh * HEAD_DIM:(hh + 1) * HEAD_DIM] for hh in range(DN_PAIR)])
            state = jnp.stack([st_ref[hh, n] for hh in range(DN_PAIR)])
            t = jnp.stack([t_ref[hh, n] for hh in range(DN_PAIR)])
            dq, dk, dv, dgc, dbe, dog_h, dstate = _dn_chunk_bwd(
                q, k, heads(v_ref), kk, qk, state, t, _pick_col(gcc[...], n), gcr[:, pl.ds(n, 1), :],
                _pick_col(bcc[...], n), heads(do_ref), ogv, ds_nexts)
            for hh in range(DN_PAIR):
                dv_ref[rows, hh * HEAD_DIM:(hh + 1) * HEAD_DIM] = dv[hh]
            dgc_acc[...] = jnp.where(lane_n == n, dgc, dgc_acc[...])
            db_ref[...] = jnp.where(lane_n == n, dbe, db_ref[...])
            for i in range(DN_QK):
                dq_ref[rows, i * HEAD_DIM:(i + 1) * HEAD_DIM] = dq[2 * i] + dq[2 * i + 1]
                dk_ref[rows, i * HEAD_DIM:(i + 1) * HEAD_DIM] = dk[2 * i] + dk[2 * i + 1]
            return dstate, dog + jnp.sum(dog_h, axis=0)

        _, dog = lax.fori_loop(0, nc, chunk, (jnp.zeros((DN_PAIR, HEAD_DIM, HEAD_DIM), F32), jnp.zeros((1, HEAD_DIM), F32)))
        dog_ref[...] += dog
        for hh in range(DN_PAIR):
            dg_ref[hh] = _fdot(lc, dgc_acc[hh], TN)

    qk_spec = _dn_big((S, DN_QK * HEAD_DIM), lambda h: (0, h))
    v_spec = _dn_big((S, DN_PAIR * HEAD_DIM), lambda h: (0, h))
    rows_spec = pl.BlockSpec((DN_PAIR, LANE, CH), lambda h: (h, 0, 0))
    cols_spec = pl.BlockSpec((DN_PAIR, CH, LANE), lambda h: (h, 0, 0))
    vec = pl.BlockSpec((1, HEAD_DIM), lambda h: (0, 0))
    qk_out = jax.ShapeDtypeStruct((S, DN_QK_WIDTH), F32)
    return pl.pallas_call(
        body, grid=(DN_V_HEADS // DN_PAIR,),
        in_specs=[v_spec, qk_spec, qk_spec, v_spec, rows_spec, rows_spec, vec,
                  _dn_big((DN_PAIR, nc, HEAD_DIM, HEAD_DIM), lambda h: (h, 0, 0, 0)),
                  _dn_big((DN_PAIR, nc, CH, CH), lambda h: (h, 0, 0, 0))],
        out_specs=(qk_spec, qk_spec, v_spec, cols_spec, cols_spec, vec),
        out_shape=(qk_out, qk_out, jax.ShapeDtypeStruct((S, MIX_WIDTH), F32), jax.ShapeDtypeStruct((DN_V_HEADS, CH, LANE), F32),
                   jax.ShapeDtypeStruct((DN_V_HEADS, CH, LANE), F32), jax.ShapeDtypeStruct((1, HEAD_DIM), F32)),
        scratch_shapes=[pltpu.VMEM((DN_PAIR, LANE, CH), F32), pltpu.VMEM((DN_PAIR, CH, LANE), F32),
                        pltpu.VMEM((DN_PAIR, CH, LANE), F32), pltpu.VMEM((DN_PAIR, CH, LANE), F32)],
        compiler_params=_cp(("arbitrary",)), name=name)(
            dmix, qn, kn, vc, g_rows, b_rows, out_g.reshape(1, HEAD_DIM), states, tinv)


def _rows_form(x, nc):
    t = x[:, :DN_V_HEADS].T.reshape(DN_V_HEADS, nc, CH)
    return jnp.pad(t, ((0, 0), (0, LANE - nc), (0, 0)))


def _cols_to_nat(x, nc):
    t = jnp.transpose(x[:, :, :nc], (2, 1, 0)).reshape(nc * CH, DN_V_HEADS)
    return jnp.pad(t, ((0, 0), (0, LANE - DN_V_HEADS)))


_C_QKV = 2 * DN_QK_WIDTH + MIX_WIDTH


def _true_pieces(lo, hi):
    out = []
    while lo < hi:
        s = lo // P0_SHARD
        end = min(hi, (s + 1) * P0_SHARD)
        out.append((s, lo - s * P0_SHARD, end - s * P0_SHARD))
        lo = end
    return out


def _padded_pieces(lo, hi):
    a0, b0, x0 = _C_QKV, _C_QKV + DN_V_HEADS, _C_QKV + 2 * DN_V_HEADS
    out = []
    for t0, t1, shift in ((0, a0, 0), (a0, b0, P0_AB - a0), (b0, x0, P0_AB + LANE - b0), (x0, DN_PROJ, a0 - x0)):
        s, e = max(lo, t0), min(hi, t1)
        if s < e:
            out.append((s + shift, e + shift))
    return out


def _pad_lane(v):
    v = v.reshape(1, -1)
    return jnp.pad(v, ((0, 0), (0, LANE - v.shape[1])))


SLOT1 = SB_PROJ // N_CHIPS
MM_TN = 512


def _local_step(x, mem, target, norm_g, mem_norm_g, xa_q_g, xa_k_g, w_in0, conv_w, a_log, dt_bias, out_g, sb_q_g, sb_k_g,
                late_weights, early_grads):
    S = x.shape[0]
    nc = S // CH
    al = _pad_lane(a_log)
    dtb = _pad_lane(dt_bias)
    q_scale = HEAD_DIM ** -0.5
    tiles1 = SLOT1 // MM_TN

    kv_rhs = lambda l: pl.BlockSpec((N_CHIPS, None, D_MODEL // N_CHIPS, MM_TN), lambda i, j: (0, l, 0, j))
    kv_rhs_t = lambda l: pl.BlockSpec((None, None, D_MODEL // N_CHIPS, 2 * XA_WIDTH), lambda i, j: (j, l, 0, 0))
    out_rhs = lambda l: pl.BlockSpec((N_CHIPS, None, INNER // N_CHIPS, MM_TN), lambda i, j: (0, l, 0, j))
    out_rhs_t = lambda l: pl.BlockSpec((None, None, MM_TN, D_MODEL), lambda i, j: (j // 2, l, j % 2, 0))
    in1_rhs = pl.BlockSpec((None, 2, D_MODEL // 2, MM_TN), lambda i, j: (j // tiles1, 0, 0, j % tiles1))
    in1_rhs_t = pl.BlockSpec((None, None, D_MODEL // 2, MM_TN), lambda i, j, k: (k // tiles1, j, 0, k % tiles1))
    slot_rows = lambda rows: dict(
        tm=rows, o_spec=pl.BlockSpec((None, rows, MM_TN), lambda i, j: (i, 0, j)),
        o_shape=jax.ShapeDtypeStruct((N_CHIPS, rows, 2 * XA_WIDTH), BF16))
    in1_out = dict(tm=D_MODEL // 2, o_spec=pl.BlockSpec((None, None, D_MODEL // 2, MM_TN),
                                                        lambda i, j: (j // tiles1, i, 0, j % tiles1)),
                   o_shape=jax.ShapeDtypeStruct((N_CHIPS, 2, D_MODEL // 2, SLOT1), BF16))

    h0 = _rmsnorm_fwd(x, norm_g[0], name="norm0")
    proj0 = _matmul(h0, w_in0, name="proj0")
    qn = _dn_pre_fwd(proj0, conv_w, 0, DN_QK_WIDTH, l2=True, scale=q_scale, name="dn_pre_q")
    kn = _dn_pre_fwd(proj0, conv_w, DN_QK_WIDTH, DN_QK_WIDTH, l2=True, scale=1.0, name="dn_pre_k")
    vc = _dn_pre_fwd(proj0, conv_w, 2 * DN_QK_WIDTH, MIX_WIDTH, l2=False, scale=1.0, name="dn_pre_v")
    g_nat, b_nat = _dn_ab_fwd(proj0, al, dtb, name="dn_ab")
    g_rows = _rows_form(g_nat, nc)
    b_rows = _rows_form(b_nat, nc)
    mix0, states, tinv = _dn_core_fwd(qn, kn, vc, g_rows, b_rows, out_g, name="dn_core")
    w_kv, w_out, w_in1 = late_weights(mix0)
    mem_n = _rmsnorm_fwd(mem, mem_norm_g, name="mem_norm")
    kv = [_matmul(mem_n, w_kv, n=2 * XA_WIDTH, tn=MM_TN, b_spec=kv_rhs(l), name=f"kv{l}") for l in range(2)]
    xa0 = _xa_fwd(proj0, P0_XQ, kv[0], xa_q_g[0], xa_k_g[0], name="xa0")
    y0 = _gate_fwd(mix0, xa0, proj0, P0_Z, name="gate0")
    x1 = _matmul(y0, w_out, n=D_MODEL, tn=MM_TN, b_spec=out_rhs(0), res=x, name="out0")

    h1 = _rmsnorm_fwd(x1, norm_g[1], name="norm1")
    proj1 = _matmul(h1, w_in1, n=SB_PROJ, tn=MM_TN, b_spec=in1_rhs, name="proj1")
    mix1, tot1 = _sb_fwd(proj1, sb_q_g, sb_k_g, name="sb")
    xa1 = _xa_fwd(proj1, P1_XQ, kv[1], xa_q_g[1], xa_k_g[1], name="xa1")
    y1 = _gate_fwd(mix1, xa1, proj1, P1_Z, name="gate1")
    x2 = _matmul(y1, w_out, n=D_MODEL, tn=MM_TN, b_spec=out_rhs(1), res=x1, name="out1")

    dx2, loss_vec = _loss_head(x2, target, name="loss")

    d_wout1 = _matmul(y1, dx2, ta=True, name="d_wout1", **slot_rows(INNER // N_CHIPS))
    dy1 = _matmul(dx2, w_out, tb=True, n=INNER, tn=MM_TN, b_spec=out_rhs_t(1), name="dy1")
    dcat1, dz1 = _gate_bwd(dy1, mix1, xa1, proj1, P1_Z, name="gate1_bwd")
    dqkv1, d_sbq, d_sbk = _sb_bwd(dcat1, tot1, proj1, sb_q_g, sb_k_g, name="sb_bwd")
    dxq1, dkv1, d_xaq1, d_xak1 = _xa_bwd(dcat1, proj1, P1_XQ, kv[1], xa_q_g[1], xa_k_g[1], name="xa1_bwd")
    dproj1 = dqkv1 + [dxq1, dz1]
    d_win1 = _matmul(h1, dproj1, ta=True, name="d_win1", **in1_out)
    d_wkv1 = _matmul(mem_n, dkv1, ta=True, name="d_wkv1", **slot_rows(D_MODEL // N_CHIPS))
    token = early_grads(1, d_win1, d_wout1, d_wkv1)
    dh1 = _matmul(dproj1, w_in1, tb=True, n=D_MODEL, tn=D_MODEL // 2, tk=MM_TN, b_spec=in1_rhs_t, name="dh1")
    dx1, d_ng1 = _rmsnorm_bwd(dh1, x1, norm_g[1] + token[0, 0], dx2, name="norm1_bwd")

    d_wout0 = _matmul(y0, dx1, ta=True, name="d_wout0", **slot_rows(INNER // N_CHIPS))
    dy0 = _matmul(dx1, w_out, tb=True, n=INNER, tn=MM_TN, b_spec=out_rhs_t(0), name="dy0")
    dcat0, dz0 = _gate_bwd(dy0, mix0, xa0, proj0, P0_Z, name="gate0_bwd")
    dqv, dkv_h, dvc, dg_cols, db_cols, d_outg = _dn_core_bwd(
        dcat0, qn, kn, vc, g_rows, b_rows, out_g, states, tinv, name="dn_core_bwd")
    dpq, dwq = _dn_pre_bwd(dqv, proj0, conv_w, 0, DN_QK_WIDTH, l2=True, scale=q_scale, name="dn_pre_q_bwd")
    dpk, dwk = _dn_pre_bwd(dkv_h, proj0, conv_w, DN_QK_WIDTH, DN_QK_WIDTH, l2=True, scale=1.0, name="dn_pre_k_bwd")
    dpv, dwv = _dn_pre_bwd(dvc, proj0, conv_w, 2 * DN_QK_WIDTH, MIX_WIDTH, l2=False, scale=1.0, name="dn_pre_v_bwd")
    dab, d_alog, d_dt = _dn_ab_bwd(_cols_to_nat(dg_cols, nc), _cols_to_nat(db_cols, nc), proj0, al, dtb, name="dn_ab_bwd")
    dxq0, dkv0, d_xaq0, d_xak0 = _xa_bwd(dcat0, proj0, P0_XQ, kv[0], xa_q_g[0], xa_k_g[0], name="xa0_bwd")
    d_win0 = _matmul(h0, [dpq, dpk, dpv, dxq0, dz0, dab], ta=True, out_dtype=BF16, name="d_win0")
    d_wkv0 = _matmul(mem_n, dkv0, ta=True, name="d_wkv0", **slot_rows(D_MODEL // N_CHIPS))
    token = early_grads(0, d_win0, d_wout0, d_wkv0)
    zero = token[0, 0]
    dh0 = _matmul([dpq, dpk, dpv, dxq0, dz0, dab + zero.astype(BF16)], w_in0, tb=True, tk=MM_TN, name="dh0")
    dx0, d_ng0 = _rmsnorm_bwd(dh0, x, norm_g[0] + zero, dx1, name="norm0_bwd")

    dmem0 = _matmul(dkv0, w_kv, tb=True, n=D_MODEL, tn=D_MODEL // N_CHIPS, b_spec=kv_rhs_t(0), name="dmem0")
    dmem_n = _matmul(dkv1, w_kv, tb=True, n=D_MODEL, tn=D_MODEL // N_CHIPS, b_spec=kv_rhs_t(1), res=dmem0, name="dmem1")
    _, d_memg = _rmsnorm_bwd(dmem_n, mem, mem_norm_g, None, name="mem_norm_bwd")

    grads = dict(
        norm_g=jnp.concatenate([d_ng0, d_ng1], axis=0), mem_norm_g=d_memg.reshape(-1),
        xa_q_norm_g=jnp.concatenate([d_xaq0, d_xaq1], axis=0), xa_k_norm_g=jnp.concatenate([d_xak0, d_xak1], axis=0),
        dn_conv_w=jnp.concatenate([dwq, dwk, dwv], axis=1),
        dn_a_log=d_alog[:, :DN_V_HEADS], dn_dt_bias=d_dt[:, :DN_V_HEADS], dn_out_norm_g=d_outg,
        sb_q_norm_g=d_sbq, sb_k_norm_g=d_sbk)
    return loss_vec, dx0, grads


ANY = pl.BlockSpec(memory_space=pl.ANY)


def _place():
    x, y, c = lax.axis_index("x"), lax.axis_index("y"), lax.axis_index("c")
    chips = [(1 - x, y), (x, 1 - y), (1 - x, 1 - y)]
    return x, y, c, 2 * x + y, (x, y, 1 - c), chips


def _rcopy(src, dst, send, recv, i, dev):
    return pltpu.make_async_remote_copy(src_ref=src, dst_ref=dst, send_sem=send.at[i], recv_sem=recv.at[i],
                                        device_id=dev, device_id_type=MESH)


def _swap_halves(xs, *, name):
    nt = len(xs)

    def body(*refs):
        src, dst = refs[:nt], refs[nt:2 * nt]
        send, recv = refs[2 * nt:]
        x, y, c, j, sib, chips = _place()
        cps = []
        for t in range(nt):
            for s in range(N_CHIPS):
                cps.append(_rcopy(src[t].at[s, 1 - c], dst[t].at[s], send, recv, 4 * t + s, sib))
                cps[-1].start()
        for cp in cps:
            cp.wait_recv()
        for cp in cps:
            cp.wait_send()

    return pl.pallas_call(
        body, in_specs=[ANY] * nt, out_specs=[ANY] * nt,
        out_shape=[jax.ShapeDtypeStruct((N_CHIPS,) + a.shape[2:], a.dtype) for a in xs],
        scratch_shapes=[pltpu.SemaphoreType.DMA((4 * nt,)), pltpu.SemaphoreType.DMA((4 * nt,))], name=name)(*xs)


def _swap_with_sibling(fs, *, name):
    nt = len(fs)

    def body(*refs):
        src, dst = refs[:nt], refs[nt:2 * nt]
        send, recv = refs[2 * nt:]
        x, y, c, j, sib, chips = _place()
        cps = [_rcopy(src[t], dst[t], send, recv, t, sib) for t in range(nt)]
        for cp in cps:
            cp.start()
        for cp in cps:
            cp.wait_recv()
        for cp in cps:
            cp.wait_send()

    return pl.pallas_call(
        body, in_specs=[ANY] * nt, out_specs=[ANY] * nt,
        out_shape=[jax.ShapeDtypeStruct(a.shape, a.dtype) for a in fs],
        scratch_shapes=[pltpu.SemaphoreType.DMA((nt,)), pltpu.SemaphoreType.DMA((nt,))], name=name)(*fs)


HBM_SPEC = pl.BlockSpec(memory_space=pltpu.HBM)
SEM_SPEC = pl.BlockSpec(memory_space=pltpu.SEMAPHORE)
SIDE_EFFECT = pltpu.SideEffectType.DATAFLOW_SIDE_EFFECTING


def _gather_plan(src, land):
    x, y, c, j, sib, chips = _place()
    return [(src[t].at[c], land[t].at[j, c], (cx, cy, c), land[t].at[2 * cx + cy, c])
            for t in range(len(src)) for cx, cy in chips]


def _scatter_plan(src, land):
    x, y, c, j, sib, chips = _place()
    return [(src[t].at[2 * cx + cy], land[t].at[k], (cx, cy, c), land[t].at[k])
            for t in range(len(src)) for k, (cx, cy) in enumerate(chips)]


def _exchange_start(srcs, lands, plan, *, name):
    ns, nb = len(srcs), len(srcs) + len(lands)
    n = 3 * ns

    def body(*refs):
        send, recv, token = refs[nb], refs[nb + 1], refs[-1]
        for i, (s, d, dev, _) in enumerate(plan(refs[:ns], refs[ns:nb])):
            _rcopy(s, d, send, recv, i, dev).start()
        token[...] = jnp.zeros_like(token)

    bufs = list(srcs) + list(lands)
    outs = pl.pallas_call(
        body, name=name,
        out_shape=(pltpu.SemaphoreType.DMA((n,)), pltpu.SemaphoreType.DMA((n,)), *[pltpu.HBM(a.shape, a.dtype) for a in bufs],
                   jax.ShapeDtypeStruct((8, LANE), F32)),
        in_specs=[HBM_SPEC] * nb, out_specs=(SEM_SPEC, SEM_SPEC, *[HBM_SPEC] * nb, pl.BlockSpec(memory_space=pltpu.VMEM)),
        input_output_aliases={i: 2 + i for i in range(nb)},
        compiler_params=pltpu.CompilerParams(has_side_effects=SIDE_EFFECT))(
            *[pltpu.with_memory_space_constraint(a, pltpu.HBM) for a in bufs])
    return outs[0], outs[1], list(outs[2:2 + ns]), list(outs[2 + ns:2 + nb]), outs[-1]


def _exchange_wait(srcs, lands, send, recv, after, plan, *, name):
    ns, nb = len(srcs), len(srcs) + len(lands)
    afters = list(after) if isinstance(after, (list, tuple)) else [after]

    def body(*refs):
        send_s, recv_s = refs[nb], refs[nb + 1]
        for i, (s, d, dev, inc) in enumerate(plan(refs[:ns], refs[ns:nb])):
            _rcopy(s, d, send_s, recv_s, i, dev).wait_send()
            _rcopy(inc, inc, send_s, recv_s, i, dev).wait_recv()

    bufs = list(srcs) + list(lands)
    outs = pl.pallas_call(
        body, name=name, out_shape=tuple(pltpu.HBM(a.shape, a.dtype) for a in bufs),
        in_specs=[HBM_SPEC] * nb + [SEM_SPEC, SEM_SPEC] + [ANY] * len(afters), out_specs=tuple([HBM_SPEC] * nb),
        input_output_aliases={i: i for i in range(nb)},
        compiler_params=pltpu.CompilerParams(has_side_effects=SIDE_EFFECT))(*bufs, send, recv, *afters)
    return list(outs[:ns]), list(outs[ns:])


def _forward_halves(lands, *, name):
    nt = len(lands)

    def body(*refs):
        src, dst = refs[:nt], refs[nt:2 * nt]
        send, recv = refs[2 * nt:]
        x, y, c, j, sib, chips = _place()
        cps = []
        for t in range(nt):
            for k, (cx, cy) in enumerate(chips):
                cps.append(_rcopy(src[t].at[2 * cx + cy, c], dst[t].at[2 * cx + cy, c], send, recv, 3 * t + k, sib))
                cps[-1].start()
        for t in range(nt):
            for k, (cx, cy) in enumerate(chips):
                other = dst[t].at[2 * cx + cy, 1 - c]
                _rcopy(other, other, send, recv, 3 * t + k, sib).wait_recv()
        for cp in cps:
            cp.wait_send()

    return pl.pallas_call(
        body, in_specs=[ANY] * nt, out_specs=[ANY] * nt, out_shape=[jax.ShapeDtypeStruct(a.shape, a.dtype) for a in lands],
        input_output_aliases={t: t for t in range(nt)},
        scratch_shapes=[pltpu.SemaphoreType.DMA((3 * nt,)), pltpu.SemaphoreType.DMA((3 * nt,))], name=name)(*lands)


def _all_reduce_small(parts, *, name):
    n = len(parts)
    offs, rows = [], 0
    for p in parts:
        offs.append(rows)
        rows += -(-p.shape[0] // 8) * 8

    def body(*refs):
        p_refs, o_refs = refs[:n], refs[n:2 * n]
        buf, send, recv = refs[2 * n:]
        x, y, c = lax.axis_index("x"), lax.axis_index("y"), lax.axis_index("c")
        me = 4 * x + 2 * y + c
        buf[me] = jnp.zeros((rows, LANE), F32)
        for p_ref, off in zip(p_refs, offs):
            buf[me, off:off + p_ref.shape[0], :] = p_ref[...]
        cps = []
        for r in range(1, 8):
            dev = (x ^ (r >> 2), y ^ ((r >> 1) & 1), c ^ (r & 1))
            cps.append(_rcopy(buf.at[me], buf.at[me], send, recv, r - 1, dev))
            cps[-1].start()
        for r in range(1, 8):
            frm = buf.at[me ^ r]
            _rcopy(frm, frm, send, recv, r - 1, (x, y, c)).wait_recv()
        for cp in cps:
            cp.wait_send()
        acc = buf[0]
        for d in range(1, 8):
            acc = acc + buf[d]
        for o_ref, off in zip(o_refs, offs):
            o_ref[...] = acc[off:off + o_ref.shape[0], :]

    vm = pl.BlockSpec(memory_space=pltpu.VMEM)
    return pl.pallas_call(
        body, in_specs=[vm] * n, out_specs=[vm] * n, out_shape=[jax.ShapeDtypeStruct(p.shape, F32) for p in parts],
        scratch_shapes=[pltpu.VMEM((8, rows, LANE), F32), pltpu.SemaphoreType.DMA((7,)), pltpu.SemaphoreType.DMA((7,))],
        name=name)(*parts)


def _add_halves(x, b, c_idx, *, name, tr=256):
    _, _, R, C = x.shape
    tr = min(tr, R)

    def body(c_ref, x_ref, b_ref, o_ref):
        o_ref[...] = (x_ref[...].astype(F32) + b_ref[...].astype(F32)).astype(o_ref.dtype)

    return pl.pallas_call(
        body,
        grid_spec=pltpu.PrefetchScalarGridSpec(
            num_scalar_prefetch=1, grid=(N_CHIPS, R // tr),
            in_specs=[pl.BlockSpec((None, None, tr, C), lambda s, i, c_ref: (s, c_ref[0], i, 0)),
                      pl.BlockSpec((None, tr, C), lambda s, i, c_ref: (s, i, 0))],
            out_specs=pl.BlockSpec((None, tr, C), lambda s, i, c_ref: (s, i, 0))),
        out_shape=jax.ShapeDtypeStruct(b.shape, b.dtype), compiler_params=_cp(("parallel", "parallel")), name=name)(c_idx, x, b)


def _sum_slot(p, rcv, j_idx, *, name, tr=256):
    _, R, C = p.shape
    tr = min(tr, R)

    def body(j_ref, p_ref, r_ref, o_ref):
        acc = p_ref[...].astype(F32)
        for k in range(3):
            acc = acc + r_ref[k].astype(F32)
        o_ref[...] = acc

    return pl.pallas_call(
        body,
        grid_spec=pltpu.PrefetchScalarGridSpec(
            num_scalar_prefetch=1, grid=(R // tr,),
            in_specs=[pl.BlockSpec((None, tr, C), lambda i, j_ref: (j_ref[0], i, 0)),
                      pl.BlockSpec((3, tr, C), lambda i, j_ref: (0, i, 0))],
            out_specs=pl.BlockSpec((tr, C), lambda i, j_ref: (i, 0))),
        out_shape=jax.ShapeDtypeStruct((R, C), F32), compiler_params=_cp(("parallel",)), name=name)(j_idx, p, rcv)


def _adamw_math(w, g, m, v):
    nm = ADAM_B1 * m + (1.0 - ADAM_B1) * g
    nv = ADAM_B2 * v + (1.0 - ADAM_B2) * (g * g)
    m_hat = nm / (1.0 - ADAM_B1 ** ADAM_STEP)
    v_hat = nv / (1.0 - ADAM_B2 ** ADAM_STEP)
    return -ADAM_LR * (m_hat / (jnp.sqrt(v_hat) + ADAM_EPS) + ADAM_WD * w), nm, nv


def _adamw_halves(w, g_mine, g_theirs, m, v, c_idx, *, name, layer=0, into=None, tr=128):
    _, _, R, C = w.shape
    tr = tr if R % tr == 0 else R

    def body(c_ref, w_ref, gm_ref, gt_ref, m_ref, v_ref, *rest):
        g_ref, d_ref, nm_ref, nv_ref = rest[-4:]
        gv = jnp.where(pl.program_id(0) == c_ref[0], gm_ref[...], gt_ref[...])
        d, nm, nv = _adamw_math(w_ref[...], gv, m_ref[...], v_ref[...])
        g_ref[...] = gv
        d_ref[...] = d
        nm_ref[...] = nm
        nv_ref[...] = nv

    full = pl.BlockSpec((None, None, tr, C), lambda hh, i, c_ref: (layer, hh, i, 0))
    half = pl.BlockSpec((tr, C), lambda hh, i, c_ref: (i, 0))
    sh = jax.ShapeDtypeStruct(w.shape, F32)
    extra = [] if into is None else list(into)
    return pl.pallas_call(
        body,
        grid_spec=pltpu.PrefetchScalarGridSpec(num_scalar_prefetch=1, grid=(2, R // tr),
                                               in_specs=[full, half, half, full, full] + [ANY] * len(extra),
                                               out_specs=(full,) * 4),
        out_shape=(sh,) * 4, input_output_aliases={6 + t: t for t in range(len(extra))},
        compiler_params=_cp(("parallel", "parallel")), name=name)(c_idx, w, g_mine, g_theirs, m, v, *extra)


def _adamw_parts(ws, gs, ms, vs, *, name):
    n = len(ws)

    def body(*refs):
        ins, outs = refs[:4 * n], refs[4 * n:]
        for t in range(n):
            d, nm, nv = _adamw_math(ins[t][...], ins[n + t][...], ins[2 * n + t][...], ins[3 * n + t][...])
            outs[t][...] = d
            outs[n + t][...] = nm
            outs[2 * n + t][...] = nv

    vm = pl.BlockSpec(memory_space=pltpu.VMEM)
    shapes = [jax.ShapeDtypeStruct(w.shape, F32) for w in ws] * 3
    outs = pl.pallas_call(body, in_specs=[vm] * (4 * n), out_specs=[vm] * (3 * n), out_shape=shapes, name=name)(
        *ws, *gs, *ms, *vs)
    return outs[:n], outs[n:2 * n], outs[2 * n:]


_SMALL = ["norm_g", "mem_norm_g", "xa_q_norm_g", "xa_k_norm_g", "dn_a_log", "dn_dt_bias", "dn_out_norm_g",
          "sb_q_norm_g", "sb_k_norm_g"]


def _rows128(a):
    flat = a.reshape(-1)
    pad = -flat.shape[0] % LANE
    if pad:
        flat = jnp.pad(flat, (0, pad))
    return flat.reshape(-1, LANE)


def _unrows(r, shape):
    return r.reshape(-1)[:math.prod(shape)].reshape(shape)


def kernel(x, mem, norm_g, mem_norm_g, mem_w_kv, xa_q_norm_g, xa_k_norm_g, w_out, dn_w_in, dn_conv_w, dn_a_log, dn_dt_bias, dn_out_norm_g, sb_w_in, sb_q_norm_g, sb_k_norm_g, loss_target, m_norm_g, m_mem_norm_g, m_mem_w_kv, m_xa_q_norm_g, m_xa_k_norm_g, m_w_out, m_dn_w_in, m_dn_conv_w, m_dn_a_log, m_dn_dt_bias, m_dn_out_norm_g, m_sb_w_in, m_sb_q_norm_g, m_sb_k_norm_g, v_norm_g, v_mem_norm_g, v_mem_w_kv, v_xa_q_norm_g, v_xa_k_norm_g, v_w_out, v_dn_w_in, v_dn_conv_w, v_dn_a_log, v_dn_dt_bias, v_dn_out_norm_g, v_sb_w_in, v_sb_q_norm_g, v_sb_k_norm_g):
    W = dict(norm_g=norm_g, mem_norm_g=mem_norm_g, mem_w_kv=mem_w_kv, xa_q_norm_g=xa_q_norm_g, xa_k_norm_g=xa_k_norm_g,
             w_out=w_out, dn_w_in=dn_w_in, dn_conv_w=dn_conv_w, dn_a_log=dn_a_log, dn_dt_bias=dn_dt_bias,
             dn_out_norm_g=dn_out_norm_g, sb_w_in=sb_w_in, sb_q_norm_g=sb_q_norm_g, sb_k_norm_g=sb_k_norm_g)
    M = dict(norm_g=m_norm_g, mem_norm_g=m_mem_norm_g, mem_w_kv=m_mem_w_kv, xa_q_norm_g=m_xa_q_norm_g,
             xa_k_norm_g=m_xa_k_norm_g, w_out=m_w_out, dn_w_in=m_dn_w_in, dn_conv_w=m_dn_conv_w, dn_a_log=m_dn_a_log,
             dn_dt_bias=m_dn_dt_bias, dn_out_norm_g=m_dn_out_norm_g, sb_w_in=m_sb_w_in, sb_q_norm_g=m_sb_q_norm_g,
             sb_k_norm_g=m_sb_k_norm_g)
    V = dict(norm_g=v_norm_g, mem_norm_g=v_mem_norm_g, mem_w_kv=v_mem_w_kv, xa_q_norm_g=v_xa_q_norm_g,
             xa_k_norm_g=v_xa_k_norm_g, w_out=v_w_out, dn_w_in=v_dn_w_in, dn_conv_w=v_dn_conv_w, dn_a_log=v_dn_a_log,
             dn_dt_bias=v_dn_dt_bias, dn_out_norm_g=v_dn_out_norm_g, sb_w_in=v_sb_w_in, sb_q_norm_g=v_sb_q_norm_g,
             sb_k_norm_g=v_sb_k_norm_g)
    names = ["norm_g", "mem_norm_g", "mem_w_kv", "xa_q_norm_g", "xa_k_norm_g", "w_out", "dn_w_in", "dn_conv_w",
             "dn_a_log", "dn_dt_bias", "dn_out_norm_g", "sb_w_in", "sb_q_norm_g", "sb_k_norm_g"]
    cx, cy, cc = lax.axis_index("x"), lax.axis_index("y"), lax.axis_index("c")
    slot = 2 * cx + cy
    half_r = D_MODEL // 2
    conv_cols = dn_conv_w.shape[2]

    w0s = jnp.pad(dn_w_in[0].astype(BF16), ((0, 0), (0, P0_SHARD_PAD - P0_SHARD))).reshape(2, half_r, P0_SHARD_PAD)
    w1s = sb_w_in[0].astype(BF16).reshape(2, half_r, SB_PROJ // N_CHIPS)
    convs = jnp.pad(dn_conv_w[0], ((0, 8 - DN_CONV), (0, 0))).reshape(8, 2, conv_cols // 2).transpose(1, 0, 2)
    c_idx = jnp.reshape(cc, (1,)).astype(jnp.int32)
    j_idx = jnp.reshape(slot, (1,)).astype(jnp.int32)
    own_a = [w0s, convs]
    own_b = [w1s, w_out.astype(BF16), mem_w_kv.astype(BF16)]
    lands_a = [lax.dynamic_update_slice(lax.empty((N_CHIPS,) + o.shape, o.dtype), o[None], (slot, 0, 0, 0)) for o in own_a]
    send_a, recv_a, own_a, lands_a, _ = _exchange_start(own_a, lands_a, _gather_plan, name="gather_start")
    view0 = (1, 2, half_r, P0_SHARD)
    _, lands_a = _exchange_wait(own_a, lands_a, send_a, recv_a, [m_dn_w_in.reshape(view0), v_dn_w_in.reshape(view0)],
                                _gather_plan, name="gather_wait")
    (g0, gconv), own_b = lax.optimization_barrier((_forward_halves(lands_a, name="gather_forward"), own_b))
    lands_b = [lax.dynamic_update_slice(lax.empty((N_CHIPS,) + o.shape, o.dtype), o[None], (slot, 0, 0, 0)) for o in own_b]
    send_b, recv_b, own_b, lands_b, token_b = _exchange_start(own_b, lands_b, _gather_plan, name="gather_late_start")

    def late_weights(after):
        _, lands = _exchange_wait(own_b, lands_b, send_b, recv_b, after, _gather_plan, name="gather_late_wait")
        g1, gout, gkv = _forward_halves(lands, name="gather_late_forward")
        return gkv, gout, g1

    rs = {}

    def scatter_start(tag, xs):
        from_sib = _swap_halves(xs, name=f"rs{tag}_swap")
        ps = [_add_halves(a, b, c_idx, name=f"rs{tag}_add{t}") for t, (a, b) in enumerate(zip(xs, from_sib))]
        rcv = [lax.empty((3,) + p.shape[1:], p.dtype) for p in ps]
        send, recv, ps, rcv, token = _exchange_start(ps, rcv, _scatter_plan, name=f"rs{tag}_scatter_start")
        rs[tag] = (ps, rcv, send, recv)
        return token

    def scatter_finish(tag, after):
        ps, rcv, send, recv = rs[tag]
        ps, rcv = _exchange_wait(ps, rcv, send, recv, after, _scatter_plan, name=f"rs{tag}_scatter_wait")
        return [_sum_slot(p, r, j_idx, name=f"rs{tag}_sum{t}") for t, (p, r) in enumerate(zip(ps, rcv))]

    def early_grads(layer, d_win, d_wout, d_wkv):
        if layer == 0:
            d_win = jnp.stack([jnp.pad(jnp.concatenate(
                [d_win[:, lo:hi] for lo, hi in _padded_pieces(s * P0_SHARD, (s + 1) * P0_SHARD)], axis=1),
                ((0, 0), (0, P0_SHARD_PAD - P0_SHARD))) for s in range(N_CHIPS)]).reshape(N_CHIPS, 2, half_r, P0_SHARD_PAD)
        return scatter_start(layer, [d_win, d_wout.reshape(N_CHIPS, 2, -1, D_MODEL), d_wkv.reshape(N_CHIPS, 2, -1, 2 * XA_WIDTH)])

    shards0 = g0.reshape(N_CHIPS, D_MODEL, P0_SHARD_PAD)
    z = lambda n: jnp.zeros((D_MODEL, n), BF16)
    w_in0 = jnp.concatenate(
        [shards0[s][:, lo:hi] for s, lo, hi in _true_pieces(0, _C_QKV) + _true_pieces(_C_QKV + 2 * DN_V_HEADS, DN_PROJ)]
        + [shards0[s][:, lo:hi] for s, lo, hi in _true_pieces(_C_QKV, _C_QKV + DN_V_HEADS)] + [z(LANE - DN_V_HEADS)]
        + [shards0[s][:, lo:hi] for s, lo, hi in _true_pieces(_C_QKV + DN_V_HEADS, _C_QKV + 2 * DN_V_HEADS)]
        + [z(P0 - P0_AB - LANE - DN_V_HEADS)], axis=1)
    conv_f = gconv.transpose(2, 0, 1, 3).reshape(8, N_CHIPS * conv_cols)[:DN_CONV]

    loss_vec, grad_x, g = _local_step(
        x[0], mem[0], loss_target[0], norm_g + token_b[0, 0], mem_norm_g, xa_q_norm_g, xa_k_norm_g, w_in0, conv_f,
        dn_a_log[0], dn_dt_bias[0], dn_out_norm_g[0], sb_q_norm_g[0], sb_k_norm_g[0], late_weights, early_grads)

    mine1 = scatter_finish(1, grad_x)
    theirs1 = _swap_with_sibling(mine1, name="rs1_join")
    big1 = [("sb_w_in", None), ("w_out", 1), ("mem_w_kv", 1)]
    big0 = [("dn_w_in", None), ("w_out", 0), ("mem_w_kv", 0)]

    out_g, out_d, out_m, out_v = {}, {}, {}, {}
    partial = {}

    def adamw_big(big, mine, theirs):
        for (n, layer), mine_g, their_g in zip(big, mine, theirs):
            layers = 1 if layer is None else 2
            view = (layers, 2) + mine_g.shape
            partial[n] = _adamw_halves(W[n].reshape(view), mine_g, their_g, M[n].reshape(view), V[n].reshape(view), c_idx,
                                       layer=layer or 0, into=partial.get(n), name=f"adamw_{n}" + ("" if layer is None else str(layer)))
        return [partial[n][0] for n, _ in big]

    done1 = lax.optimization_barrier(tuple(adamw_big(big1, mine1, theirs1)))[-1]
    mine0 = scatter_finish(0, done1)
    mine0[0] = mine0[0][:, :P0_SHARD]

    parts, _ = lax.optimization_barrier(([_rows128(g[n]) for n in _SMALL] + [_rows128(g["dn_conv_w"]), loss_vec], mine0[0]))
    red = _all_reduce_small(parts, name="all_reduce_small")
    small_rows = dict(zip(_SMALL, red))
    conv_full = red[len(_SMALL)].reshape(DN_CONV, N_CHIPS * conv_cols)
    small_rows["dn_conv_w"] = _rows128(lax.dynamic_slice_in_dim(conv_full, slot * conv_cols, conv_cols, axis=1))
    loss = red[-1][0, 0]

    adamw_big(big0, mine0, _swap_with_sibling(mine0, name="rs0_join"))
    for n, outs in partial.items():
        out_g[n], out_d[n], out_m[n], out_v[n] = [o.reshape(W[n].shape) for o in outs]
    small_names = _SMALL + ["dn_conv_w"]
    ds, nms, nvs = _adamw_parts([_rows128(W[n]) for n in small_names], [small_rows[n] for n in small_names],
                                [_rows128(M[n]) for n in small_names], [_rows128(V[n]) for n in small_names], name="adamw_small")
    for n, d, nm, nv in zip(small_names, ds, nms, nvs):
        shp = W[n].shape
        out_g[n], out_d[n], out_m[n], out_v[n] = [_unrows(r, shp) for r in (small_rows[n], d, nm, nv)]

    return (loss, grad_x[None], *[out_g[n] for n in names], *[out_d[n] for n in names], *[out_m[n] for n in names],
            *[out_v[n] for n in names])
```

```python
import math

import jax
import jax.numpy as jnp
from jax import lax
from jax.experimental import pallas as pl
from jax.experimental.pallas import tpu as pltpu

F32 = jnp.float32
BF16 = jnp.bfloat16
HI = lax.Precision.HIGHEST
MESH = pl.DeviceIdType.MESH

D_MODEL = 2048
INNER = 4096
XA_WIDTH = 1024
XA_HEADS = 4
XA_DIM = 256
MIX_WIDTH = 3072
HEAD_DIM = 128
DN_V_HEADS = 24
DN_QK_WIDTH = 1536
DN_CONV = 4
DN_PROJ = 11312
SB_PROJ = 14336
EPS = 1e-6
N_CHIPS = 4

CH = 128
LANE = 128

P0_XQ = 6144
P0_Z = 7168
P0_AB = 11264
P0 = 11776
P0_SHARD = DN_PROJ // N_CHIPS
P0_SHARD_PAD = 2944
P1_XQ = 9216
P1_Z = 10240
P1 = SB_PROJ

ADAM_LR = 0.001
ADAM_B1 = 0.9
ADAM_B2 = 0.999
ADAM_EPS = 1e-08
ADAM_WD = 0.01
ADAM_STEP = 10

VMEM_LIMIT = 48 * 1024 * 1024


def _cp(sem=None, **kw):
    return pltpu.CompilerParams(dimension_semantics=sem, vmem_limit_bytes=VMEM_LIMIT, **kw)


def _bdot(a, b, dims):
    return lax.dot_general(a.astype(BF16), b.astype(BF16), (dims, ((), ())), preferred_element_type=F32)


def _fdot(a, b, dims):
    return lax.dot_general(a, b, (dims, ((), ())), precision=HI, preferred_element_type=F32)


NN = ((1,), (0,))
NT = ((1,), (1,))
TN = ((0,), (0,))


def _sigmoid(x):
    return 1.0 / (1.0 + jnp.exp(-x))


def _softplus(x):
    return jnp.maximum(x, 0.0) + jnp.log(1.0 + jnp.exp(-jnp.abs(x)))


def _iota2(shape, axis):
    return lax.broadcasted_iota(jnp.int32, shape, axis)


MM_FULL_K = 4096
MM_BLOCK_BYTES = 4 * 1024 * 1024


def _matmul(a, b, *, ta=False, tb=False, out_dtype=F32, res=None, name, n=None, tm=None, tn=None, tk=None,
            b_spec=None, o_spec=None, o_shape=None):
    a_segs = list(a) if isinstance(a, (list, tuple)) else [a]
    b_segs = list(b) if isinstance(b, (list, tuple)) else [b]
    a0, b0 = a_segs[0], b_segs[0]
    M = a0.shape[1] if ta else a0.shape[0]
    K = a0.shape[0] if ta else sum(s.shape[1] for s in a_segs)
    if n is None:
        n = b0.shape[0] if tb else sum(s.shape[1] for s in b_segs)
    N = n
    dims = ((0,) if ta else (1,), (1,) if tb else (0,))
    has_res = res is not None
    flat = lambda v: v.reshape(-1, v.shape[-1])
    o_shape = o_shape or jax.ShapeDtypeStruct((M, N), out_dtype)

    def seg_specs(segs, tile, block, pos):
        specs, ranges, off = [], [], 0
        for s in segs:
            cnt = s.shape[1] // tile
            assert s.shape[1] % tile == 0, (name, s.shape, tile)

            def imap(*g, off=off, cnt=cnt):
                t = jnp.clip(g[pos] - off, 0, cnt - 1)
                return (g[0], t) if pos == 2 else (0, t)

            specs.append(pl.BlockSpec(block, imap))
            ranges.append((off, off + cnt))
            off += cnt
        return specs, ranges

    if K <= MM_FULL_K:
        assert len(a_segs) == 1
        tm = tm or min(M, 1024, max(256, MM_BLOCK_BYTES // (K * a0.dtype.itemsize)))
        tn = tn or min(N, 512)
        assert M % tm == 0 and N % tn == 0, (name, M, N, K, tm, tn)
        nb = len(b_segs)
        if b_spec is not None:
            b_specs, b_ranges = [b_spec], [(0, N // tn)]
        elif nb > 1:
            assert not tb
            b_specs, b_ranges = seg_specs(b_segs, tn, (K, tn), 1)
        else:
            b_specs = [pl.BlockSpec((tn, K), lambda i, j: (j, 0)) if tb else pl.BlockSpec((K, tn), lambda i, j: (0, j))]
            b_ranges = [(0, N // tn)]

        def body_full(*refs):
            a_ref, b_refs = refs[0], refs[1:1 + nb]
            r_ref = refs[1 + nb] if has_res else None
            o_ref = refs[-1]
            j = pl.program_id(1)
            for b_ref, (lo, hi) in zip(b_refs, b_ranges):
                def emit(b_ref=b_ref):
                    r = _bdot(a_ref[...], flat(b_ref[...]), dims)
                    if has_res:
                        r = r + r_ref[...]
                    o_ref[...] = r.astype(o_ref.dtype).reshape(o_ref.shape)
                if nb == 1:
                    emit()
                else:
                    pl.when(jnp.logical_and(j >= lo, j < hi))(emit)

        a_spec = pl.BlockSpec((K, tm), lambda i, j: (0, i)) if ta else pl.BlockSpec((tm, K), lambda i, j: (i, 0))
        o_spec = o_spec or pl.BlockSpec((tm, tn), lambda i, j: (i, j))
        r_spec = [pl.BlockSpec((tm, tn), lambda i, j: (i, j))] if has_res else []
        return pl.pallas_call(
            body_full, grid=(M // tm, N // tn), in_specs=[a_spec] + b_specs + r_spec, out_specs=o_spec, out_shape=o_shape,
            compiler_params=_cp(("parallel", "arbitrary")), name=name)(*([a0] + b_segs + ([res] if has_res else [])))

    assert tb and not ta and len(b_segs) == 1
    tm, tn = tm or min(M, 1024), tn or min(N, 1024)
    tk = tk or (1024 if all(s.shape[1] % 1024 == 0 for s in a_segs) else 512)
    assert M % tm == 0 and N % tn == 0 and K % tk == 0, (name, M, N, K, tm, tn, tk)
    nk = K // tk
    na = len(a_segs)
    if na > 1:
        a_specs, a_ranges = seg_specs(a_segs, tk, (tm, tk), 2)
    else:
        a_specs, a_ranges = [pl.BlockSpec((tm, tk), lambda i, j, k: (i, k))], [(0, nk)]
    b_spec = b_spec or pl.BlockSpec((tn, tk), lambda i, j, k: (j, k))

    def body(*refs):
        a_refs, b_ref = refs[:na], refs[na]
        r_ref = refs[na + 1] if has_res else None
        o_ref, acc = refs[-2], refs[-1]
        k = pl.program_id(2)

        @pl.when(k == 0)
        def _():
            acc[...] = jnp.zeros_like(acc)

        for a_ref, (lo, hi) in zip(a_refs, a_ranges):
            def emit(a_ref=a_ref):
                acc[...] += _bdot(a_ref[...], flat(b_ref[...]), dims)
            if na == 1:
                emit()
            else:
                pl.when(jnp.logical_and(k >= lo, k < hi))(emit)

        @pl.when(k == nk - 1)
        def _():
            r = acc[...]
            if has_res:
                r = r + r_ref[...]
            o_ref[...] = r.astype(o_ref.dtype).reshape(o_ref.shape)

    o_spec = o_spec or pl.BlockSpec((tm, tn), lambda i, j, k: (i, j))
    r_spec = [pl.BlockSpec((tm, tn), lambda i, j, k: (i, j))] if has_res else []
    return pl.pallas_call(
        body, grid=(M // tm, N // tn, nk), in_specs=a_specs + [b_spec] + r_spec, out_specs=o_spec, out_shape=o_shape,
        scratch_shapes=[pltpu.VMEM((tm, tn), F32)],
        compiler_params=_cp(("parallel", "parallel", "arbitrary")), name=name)(*(a_segs + [b0] + ([res] if has_res else [])))


def _rmsnorm_fwd(x, g, *, name, tm=256):
    S, Dm = x.shape
    tm = min(tm, S)

    def body(x_ref, g_ref, o_ref):
        xv = x_ref[...]
        r = lax.rsqrt(jnp.mean(xv * xv, axis=-1, keepdims=True) + EPS)
        o_ref[...] = (xv * r * g_ref[...]).astype(BF16)

    return pl.pallas_call(
        body, grid=(S // tm,), in_specs=[pl.BlockSpec((tm, Dm), lambda i: (i, 0)), pl.BlockSpec((1, Dm), lambda i: (0, 0))],
        out_specs=pl.BlockSpec((tm, Dm), lambda i: (i, 0)), out_shape=jax.ShapeDtypeStruct((S, Dm), BF16),
        compiler_params=_cp(("parallel",)), name=name)(x, g.reshape(1, Dm))


def _rmsnorm_bwd(dh, x, g, dres, *, name, tm=256):
    S, Dm = x.shape
    tm = min(tm, S)
    want_dx = dres is not None

    def body(*refs):
        if want_dx:
            dh_ref, x_ref, g_ref, dr_ref, dx_ref, dg_ref = refs
        else:
            dh_ref, x_ref, g_ref, dg_ref = refs
        i = pl.program_id(0)
        xv = x_ref[...]
        dhv = dh_ref[...]
        r = lax.rsqrt(jnp.mean(xv * xv, axis=-1, keepdims=True) + EPS)
        y = xv * r
        part = jnp.sum(dhv * y, axis=0, keepdims=True)

        @pl.when(i == 0)
        def _():
            dg_ref[...] = jnp.zeros_like(dg_ref)

        dg_ref[...] += part
        if want_dx:
            dy = dhv * g_ref[...]
            dx_ref[...] = dr_ref[...] + r * (dy - y * jnp.mean(dy * y, axis=-1, keepdims=True))

    row = pl.BlockSpec((tm, Dm), lambda i: (i, 0))
    vec = pl.BlockSpec((1, Dm), lambda i: (0, 0))
    if want_dx:
        dx, dg = pl.pallas_call(
            body, grid=(S // tm,), in_specs=[row, row, vec, row], out_specs=(row, vec),
            out_shape=(jax.ShapeDtypeStruct((S, Dm), F32), jax.ShapeDtypeStruct((1, Dm), F32)),
            compiler_params=_cp(("arbitrary",)), name=name)(dh, x, g.reshape(1, Dm), dres)
        return dx, dg
    dg = pl.pallas_call(
        body, grid=(S // tm,), in_specs=[row, row, vec], out_specs=vec,
        out_shape=jax.ShapeDtypeStruct((1, Dm), F32), compiler_params=_cp(("arbitrary",)), name=name)(dh, x, g.reshape(1, Dm))
    return None, dg


GATE_TN = XA_WIDTH
GATE_MIX_TILES = MIX_WIDTH // GATE_TN


def _gate_cat_specs(tm):
    return [pl.BlockSpec((tm, GATE_TN), lambda i, j: (i, jnp.minimum(j, GATE_MIX_TILES - 1))),
            pl.BlockSpec((tm, GATE_TN), lambda i, j: (i, 0))]


def _gate_fwd(mix, xa, proj, z_off, *, name, tm=256):
    S = mix.shape[0]
    tm = min(tm, S)
    zb = z_off // GATE_TN

    def body(m_ref, x_ref, z_ref, y_ref):
        z = z_ref[...]
        c = jnp.where(pl.program_id(1) < GATE_MIX_TILES, m_ref[...], x_ref[...])
        y_ref[...] = (c * z * _sigmoid(z)).astype(BF16)

    blk = pl.BlockSpec((tm, GATE_TN), lambda i, j: (i, j))
    return pl.pallas_call(
        body, grid=(S // tm, INNER // GATE_TN),
        in_specs=_gate_cat_specs(tm) + [pl.BlockSpec((tm, GATE_TN), lambda i, j: (i, zb + j))],
        out_specs=blk, out_shape=jax.ShapeDtypeStruct((S, INNER), BF16),
        compiler_params=_cp(("parallel", "arbitrary")), name=name)(mix, xa, proj)


def _gate_bwd(dy, mix, xa, proj, z_off, *, name, tm=256):
    S = mix.shape[0]
    tm = min(tm, S)
    zb = z_off // GATE_TN

    def body(dy_ref, m_ref, x_ref, z_ref, dc_ref, dz_ref):
        z = z_ref[...]
        sg = _sigmoid(z)
        d = dy_ref[...]
        c = jnp.where(pl.program_id(1) < GATE_MIX_TILES, m_ref[...], x_ref[...])
        dc_ref[...] = d * z * sg
        dz_ref[...] = (d * c * sg * (1.0 + z * (1.0 - sg))).astype(BF16)

    blk = pl.BlockSpec((tm, GATE_TN), lambda i, j: (i, j))
    return pl.pallas_call(
        body, grid=(S // tm, INNER // GATE_TN),
        in_specs=[blk] + _gate_cat_specs(tm) + [pl.BlockSpec((tm, GATE_TN), lambda i, j: (i, zb + j))], out_specs=(blk, blk),
        out_shape=(jax.ShapeDtypeStruct((S, INNER), F32), jax.ShapeDtypeStruct((S, INNER), BF16)),
        compiler_params=_cp(("parallel", "arbitrary")), name=name)(dy, mix, xa, proj)


def _loss_head(x, target, *, name, tm=256):
    S, Dm = x.shape
    tm = min(tm, S)

    nt = S // tm

    def body(x_ref, t_ref, dx_ref, l_ref, acc):
        i = pl.program_id(0)
        e = x_ref[...] - t_ref[...]
        dx_ref[...] = e * (1.0 / Dm)

        @pl.when(i == 0)
        def _():
            acc[...] = jnp.zeros_like(acc)

        acc[...] += jnp.sum(e * e, axis=0, keepdims=True) * (0.5 / Dm)

        @pl.when(i == nt - 1)
        def _():
            l_ref[...] = jnp.sum(acc[...], axis=1, keepdims=True) + jnp.zeros((1, LANE), F32)

    row = pl.BlockSpec((tm, Dm), lambda i: (i, 0))
    return pl.pallas_call(
        body, grid=(nt,), in_specs=[row, row], out_specs=(row, pl.BlockSpec((1, LANE), lambda i: (0, 0))),
        out_shape=(jax.ShapeDtypeStruct((S, Dm), F32), jax.ShapeDtypeStruct((1, LANE), F32)),
        scratch_shapes=[pltpu.VMEM((1, Dm), F32)],
        compiler_params=_cp(("arbitrary",)), name=name)(x, target)


def _xa_norm(v, g):
    r = lax.rsqrt(jnp.mean(v * v, axis=-1, keepdims=True) + EPS)
    return v * r, r


def _xa_fwd(proj, xq_off, kv, gq, gk, *, name, tm=512):
    S = proj.shape[0]
    tm = min(tm, S)
    qb = xq_off // XA_DIM
    n_mem = kv.shape[0]
    scale = XA_DIM ** -0.5

    def body(q_ref, k_ref, v_ref, gq_ref, gk_ref, o_ref):
        qh, _ = _xa_norm(q_ref[...], None)
        kh, _ = _xa_norm(k_ref[...], None)
        qn = qh * gq_ref[...]
        kn = kh * gk_ref[...]
        s = _bdot(qn, kn, NT) * scale
        s = s - jnp.max(s, axis=-1, keepdims=True)
        p = jnp.exp(s)
        p = p / jnp.sum(p, axis=-1, keepdims=True)
        o_ref[...] = _bdot(p, v_ref[...], NN)

    vec = pl.BlockSpec((1, XA_DIM), lambda h, i: (0, 0))
    return pl.pallas_call(
        body, grid=(XA_HEADS, S // tm),
        in_specs=[pl.BlockSpec((tm, XA_DIM), lambda h, i: (i, qb + h)),
                  pl.BlockSpec((n_mem, XA_DIM), lambda h, i: (0, h)),
                  pl.BlockSpec((n_mem, XA_DIM), lambda h, i: (0, XA_HEADS + h)), vec, vec],
        out_specs=pl.BlockSpec((tm, XA_DIM), lambda h, i: (i, h)),
        out_shape=jax.ShapeDtypeStruct((S, XA_WIDTH), F32),
        compiler_params=_cp(("parallel", "parallel")), name=name)(proj, kv, kv, gq.reshape(1, XA_DIM), gk.reshape(1, XA_DIM))


def _xa_bwd(dcat, proj, xq_off, kv, gq, gk, *, name, tm=512):
    S = proj.shape[0]
    tm = min(tm, S)
    nt = S // tm
    qb = xq_off // XA_DIM
    db = MIX_WIDTH // XA_DIM
    n_mem = kv.shape[0]
    scale = XA_DIM ** -0.5

    def body(d_ref, q_ref, k_ref, v_ref, gq_ref, gk_ref, dq_ref, dk_ref, dv_ref, dgq_ref, dgk_ref, dkn_acc):
        h = pl.program_id(0)
        i = pl.program_id(1)
        q = q_ref[...]
        k = k_ref[...]
        qh, rq = _xa_norm(q, None)
        kh, rk = _xa_norm(k, None)
        gqv = gq_ref[...]
        gkv = gk_ref[...]
        qn = qh * gqv
        kn = kh * gkv
        s = _bdot(qn, kn, NT) * scale
        s = s - jnp.max(s, axis=-1, keepdims=True)
        p = jnp.exp(s)
        p = p / jnp.sum(p, axis=-1, keepdims=True)
        d = d_ref[...]
        dp = _bdot(d, v_ref[...], NT)
        ds = p * (dp - jnp.sum(dp * p, axis=-1, keepdims=True)) * scale
        dqn = _bdot(ds, kn, NN)

        @pl.when(i == 0)
        def _():
            dkn_acc[...] = jnp.zeros_like(dkn_acc)
            dv_ref[...] = jnp.zeros_like(dv_ref)

        @pl.when(jnp.logical_and(i == 0, h == 0))
        def _():
            dgq_ref[...] = jnp.zeros_like(dgq_ref)
            dgk_ref[...] = jnp.zeros_like(dgk_ref)

        dkn_acc[...] += _bdot(ds, qn, TN)
        dv_ref[...] += _bdot(p, d, TN)
        dgq_ref[...] += jnp.sum(dqn * qh, axis=0, keepdims=True)
        dy = dqn * gqv
        dq_ref[...] = (rq * (dy - qh * jnp.mean(dy * qh, axis=-1, keepdims=True))).astype(BF16)

        @pl.when(i == nt - 1)
        def _():
            dkn = dkn_acc[...]
            dgk_ref[...] += jnp.sum(dkn * kh, axis=0, keepdims=True)
            dyk = dkn * gkv
            dk_ref[...] = rk * (dyk - kh * jnp.mean(dyk * kh, axis=-1, keepdims=True))

    vec = pl.BlockSpec((1, XA_DIM), lambda h, i: (0, 0))
    kblk = pl.BlockSpec((n_mem, XA_DIM), lambda h, i: (0, h))
    vblk = pl.BlockSpec((n_mem, XA_DIM), lambda h, i: (0, XA_HEADS + h))
    dq, dk, dv, dgq, dgk = pl.pallas_call(
        body, grid=(XA_HEADS, nt),
        in_specs=[pl.BlockSpec((tm, XA_DIM), lambda h, i: (i, db + h)),
                  pl.BlockSpec((tm, XA_DIM), lambda h, i: (i, qb + h)), kblk, vblk, vec, vec],
        out_specs=(pl.BlockSpec((tm, XA_DIM), lambda h, i: (i, h)), kblk, kblk, vec, vec),
        out_shape=(jax.ShapeDtypeStruct((S, XA_WIDTH), BF16), jax.ShapeDtypeStruct((n_mem, XA_WIDTH), F32),
                   jax.ShapeDtypeStruct((n_mem, XA_WIDTH), F32), jax.ShapeDtypeStruct((1, XA_DIM), F32),
                   jax.ShapeDtypeStruct((1, XA_DIM), F32)),
        scratch_shapes=[pltpu.VMEM((n_mem, XA_DIM), F32)],
        compiler_params=_cp(("arbitrary", "arbitrary")), name=name)(
            dcat, proj, kv, kv, gq.reshape(1, XA_DIM), gk.reshape(1, XA_DIM))
    return dq, jnp.concatenate([dk, dv], axis=1), dgq, dgk


SB_TQ = 256
SB_TK = 256
SB_HEADS = 24


SB_PAIR = 2
SB_PW = SB_PAIR * HEAD_DIM


def _hdot(a, b, dims, dot=None):
    dot = dot or _bdot
    n = a.shape[0] if a.ndim == 3 else b.shape[0]
    return jnp.stack([dot(a[i] if a.ndim == 3 else a, b[i] if b.ndim == 3 else b, dims) for i in range(n)])


def _sb_tile(qi, kj, t0, s0, masked):
    z = _hdot(qi, kj, NT)
    sp = _softplus(z)
    ls = z - sp
    if not masked:
        return -sp, ls, None
    mask = (s0 + _iota2(z.shape[1:], 1)) < (t0 + _iota2(z.shape[1:], 0))
    return jnp.where(mask, -sp, 0.0), ls, mask


def _dot2(x, tri):
    hi = x.astype(BF16)
    lo = (x - hi.astype(F32)).astype(BF16)
    plain = lambda u, v, dims: lax.dot_general(u, v, (dims, ((), ())), preferred_element_type=F32)
    return _hdot(hi, tri, NN, plain) + _hdot(lo, tri, NN, plain)


def _sb_heads(ref, rows=slice(None)):
    return jnp.stack([ref[rows, hh * HEAD_DIM:(hh + 1) * HEAD_DIM] for hh in range(SB_PAIR)])


def _sb_fwd(proj, gq, gk, *, name):
    S = proj.shape[0]
    tq, tk = min(SB_TQ, S), min(SB_TK, S)
    nq = S // tq
    scale = HEAD_DIM ** -0.5

    def body(q_ref, k_ref, v_ref, gq_ref, gk_ref, o_ref, tot_ref, qn_s, kn_s, v_s):
        q = _sb_heads(q_ref)
        k = _sb_heads(k_ref)
        qn_s[...] = (q * lax.rsqrt(jnp.mean(q * q, axis=-1, keepdims=True) + EPS) * (gq_ref[...] * scale)).astype(BF16)
        kn_s[...] = (k * lax.rsqrt(jnp.mean(k * k, axis=-1, keepdims=True) + EPS) * gk_ref[...]).astype(BF16)
        v_s[...] = _sb_heads(v_ref).astype(BF16)
        after = (_iota2((tk, tk), 0) > _iota2((tk, tk), 1)).astype(BF16)

        def qblock(i, _):
            rows = pl.ds(pl.multiple_of(i * tq, tq), tq)
            qi = qn_s[:, rows, :]
            jd = (i * tq) // tk

            def tile(j, acc, run, masked):
                cols = pl.ds(pl.multiple_of(j * tk, tk), tk)
                lr, ls, mask = _sb_tile(qi, kn_s[:, cols, :], i * tq, j * tk, masked)
                later = _dot2(lr, after) + run
                a = jnp.exp(ls + later)
                if masked:
                    a = jnp.where(mask, a, 0.0)
                acc = acc + _hdot(a, v_s[:, cols, :], NN)
                return acc, run + jnp.sum(lr, axis=-1, keepdims=True)

            acc, run = tile(jd, jnp.zeros((SB_PAIR, tq, HEAD_DIM), F32), jnp.zeros((SB_PAIR, tq, 1), F32), True)
            acc, run = lax.fori_loop(0, jd, lambda jj, c: tile(jd - 1 - jj, c[0], c[1], False), (acc, run))
            tot = run + jnp.zeros((SB_PAIR, tq, HEAD_DIM), F32)
            for hh in range(SB_PAIR):
                o_ref[rows, hh * HEAD_DIM:(hh + 1) * HEAD_DIM] = acc[hh]
                tot_ref[rows, hh * HEAD_DIM:(hh + 1) * HEAD_DIM] = tot[hh]
            return 0

        lax.fori_loop(0, nq, qblock, 0)

    npair = SB_HEADS // SB_PAIR
    vec = pl.BlockSpec((1, HEAD_DIM), lambda h: (0, 0))
    hb = lambda off: pl.BlockSpec((S, SB_PW), lambda h: (0, off + h), pipeline_mode=pl.Buffered(1))
    return pl.pallas_call(
        body, grid=(npair,), in_specs=[hb(0), hb(npair), hb(2 * npair), vec, vec],
        out_specs=(hb(0), hb(0)), out_shape=(jax.ShapeDtypeStruct((S, MIX_WIDTH), F32),) * 2,
        scratch_shapes=[pltpu.VMEM((SB_PAIR, S, HEAD_DIM), BF16)] * 3,
        compiler_params=_cp(("parallel",)), name=name)(proj, proj, proj, gq.reshape(1, HEAD_DIM), gk.reshape(1, HEAD_DIM))


def _sb_bwd(dmix, tot, proj, gq, gk, *, name):
    S = proj.shape[0]
    tq, tk = min(SB_TQ, S), min(SB_TK, S)
    nq = S // tq
    scale = HEAD_DIM ** -0.5

    def body(do_ref, o_ref, q_ref, k_ref, v_ref, gq_ref, gk_ref, dq_ref, dk_ref, dv_ref, dgq_ref, dgk_ref,
             qn_s, kn_s, v_s, dkn_s, dqn_s, dv_s):
        h = pl.program_id(0)
        q = _sb_heads(q_ref)
        k = _sb_heads(k_ref)
        rq = lax.rsqrt(jnp.mean(q * q, axis=-1, keepdims=True) + EPS)
        rk = lax.rsqrt(jnp.mean(k * k, axis=-1, keepdims=True) + EPS)
        gqv = gq_ref[...]
        gkv = gk_ref[...]
        qn_s[...] = (q * rq * (gqv * scale)).astype(BF16)
        kn_s[...] = (k * rk * gkv).astype(BF16)
        v_s[...] = _sb_heads(v_ref).astype(BF16)
        dkn_s[...] = jnp.zeros_like(dkn_s)
        dv_s[...] = jnp.zeros_like(dv_s)
        r_i = _iota2((tk, tk), 0)
        c_i = _iota2((tk, tk), 1)
        upto = (r_i <= c_i).astype(BF16)
        before = (r_i < c_i).astype(BF16)

        def qblock(i, _):
            rows = pl.ds(pl.multiple_of(i * tq, tq), tq)
            qi = qn_s[:, rows, :]
            doi = _sb_heads(do_ref, rows).astype(BF16)
            tot_i = jnp.max(_sb_heads(o_ref, rows), axis=-1, keepdims=True)
            jd = (i * tq) // tk

            def tile(j, dqn, run, run_b, masked):
                cols = pl.ds(pl.multiple_of(j * tk, tk), tk)
                kj = kn_s[:, cols, :]
                lr, ls, mask = _sb_tile(qi, kj, i * tq, j * tk, masked)
                later = tot_i - (_dot2(lr, upto) + run)
                a = jnp.exp(ls + later)
                if masked:
                    a = jnp.where(mask, a, 0.0)
                b = _hdot(doi, v_s[:, cols, :], NT) * a
                cum = _dot2(b, before) + run_b
                beta = jnp.exp(ls)
                dz = b * (1.0 - beta) - cum * beta
                if masked:
                    dz = jnp.where(mask, dz, 0.0)
                dzb = dz.astype(BF16)
                dv_s[:, cols, :] += _hdot(a, doi, TN)
                dkn_s[:, cols, :] += _hdot(dzb, qi, TN)
                dqn = dqn + _hdot(dzb, kj, NN)
                return dqn, run + jnp.sum(lr, axis=-1, keepdims=True), run_b + jnp.sum(b, axis=-1, keepdims=True)

            zero1 = jnp.zeros((SB_PAIR, tq, 1), F32)
            carry = lax.fori_loop(0, jd, lambda j, c: tile(j, c[0], c[1], c[2], False),
                                  (jnp.zeros((SB_PAIR, tq, HEAD_DIM), F32), zero1, zero1))
            dqn, _, _ = tile(jd, carry[0], carry[1], carry[2], True)
            dqn_s[:, rows, :] = dqn * scale
            return 0

        lax.fori_loop(0, nq, qblock, 0)

        @pl.when(h == 0)
        def _():
            dgq_ref[...] = jnp.zeros_like(dgq_ref)
            dgk_ref[...] = jnp.zeros_like(dgk_ref)

        heads_sum = lambda z: jnp.sum(jnp.sum(z, axis=1, keepdims=True), axis=0)
        dqn = dqn_s[...]
        qh = q * rq
        dgq_ref[...] += heads_sum(dqn * qh)
        dy = dqn * gqv
        dq = (rq * (dy - qh * jnp.mean(dy * qh, axis=-1, keepdims=True))).astype(BF16)
        dkn = dkn_s[...]
        kh = k * rk
        dgk_ref[...] += heads_sum(dkn * kh)
        dyk = dkn * gkv
        dk = (rk * (dyk - kh * jnp.mean(dyk * kh, axis=-1, keepdims=True))).astype(BF16)
        dv = dv_s[...].astype(BF16)
        for hh in range(SB_PAIR):
            lanes = slice(hh * HEAD_DIM, (hh + 1) * HEAD_DIM)
            dq_ref[:, lanes] = dq[hh]
            dk_ref[:, lanes] = dk[hh]
            dv_ref[:, lanes] = dv[hh]

    npair = SB_HEADS // SB_PAIR
    vec = pl.BlockSpec((1, HEAD_DIM), lambda h: (0, 0))
    hb = lambda off: pl.BlockSpec((S, SB_PW), lambda h: (0, off + h), pipeline_mode=pl.Buffered(1))
    dq, dk, dv, dgq, dgk = pl.pallas_call(
        body, grid=(npair,),
        in_specs=[hb(0), hb(0), hb(0), hb(npair), hb(2 * npair), vec, vec],
        out_specs=(hb(0), hb(0), hb(0), vec, vec),
        out_shape=(jax.ShapeDtypeStruct((S, MIX_WIDTH), BF16),) * 3 + (jax.ShapeDtypeStruct((1, HEAD_DIM), F32),) * 2,
        scratch_shapes=[pltpu.VMEM((SB_PAIR, S, HEAD_DIM), BF16)] * 3 + [pltpu.VMEM((SB_PAIR, S, HEAD_DIM), F32)] * 3,
        compiler_params=_cp(("arbitrary",)), name=name)(
            dmix, tot, proj, proj, proj, gq.reshape(1, HEAD_DIM), gk.reshape(1, HEAD_DIM))
    return [dq, dk, dv], dgq, dgk


def _shift_down(x, k):
    if k == 0:
        return x
    r = pltpu.roll(x, k, 0)
    return jnp.where(_iota2(x.shape, 0) >= k, r, 0.0)


def _shift_up(x, k):
    if k == 0:
        return x
    n = x.shape[0]
    r = pltpu.roll(x, n - k, 0)
    return jnp.where(_iota2(x.shape, 0) < n - k, r, 0.0)


def _conv(x, w):
    c = w[DN_CONV - 1] * x
    for k in range(1, DN_CONV):
        c = c + w[DN_CONV - 1 - k] * _shift_down(x, k)
    return c


def _dn_pre_fwd(proj, conv_w, col0, ncols, *, l2, scale, name):
    S = proj.shape[0]
    cb = col0 // HEAD_DIM

    def body(x_ref, w_ref, o_ref):
        c = _conv(x_ref[...], [w_ref[k:k + 1, :] for k in range(DN_CONV)])
        a = c * _sigmoid(c)
        if l2:
            a = a * (lax.rsqrt(jnp.sum(a * a, axis=-1, keepdims=True) + EPS) * scale)
        o_ref[...] = a

    return pl.pallas_call(
        body, grid=(ncols // HEAD_DIM,),
        in_specs=[pl.BlockSpec((S, HEAD_DIM), lambda j: (0, cb + j)), pl.BlockSpec((DN_CONV, HEAD_DIM), lambda j: (0, cb + j))],
        out_specs=pl.BlockSpec((S, HEAD_DIM), lambda j: (0, j)), out_shape=jax.ShapeDtypeStruct((S, ncols), F32),
        compiler_params=_cp(("parallel",)), name=name)(proj, conv_w)


def _dn_pre_bwd(dout, proj, conv_w, col0, ncols, *, l2, scale, name):
    S = proj.shape[0]
    cb = col0 // HEAD_DIM
    dw_in = HEAD_DIM

    def body(d_ref, x_ref, w_ref, dx_ref, dw_ref):
        x = x_ref[...]
        w = [w_ref[k:k + 1, :] for k in range(DN_CONV)]
        c = _conv(x, w)
        sg = _sigmoid(c)
        a = c * sg
        d = d_ref[...]
        if l2:
            r = lax.rsqrt(jnp.sum(a * a, axis=-1, keepdims=True) + EPS)
            y = a * r
            d = d * scale
            d = r * (d - y * jnp.sum(d * y, axis=-1, keepdims=True))
        dc = d * sg * (1.0 + c * (1.0 - sg))
        dx = w[DN_CONV - 1] * dc
        for k in range(1, DN_CONV):
            dx = dx + w[DN_CONV - 1 - k] * _shift_up(dc, k)
        dx_ref[...] = dx.astype(BF16)
        for k in range(DN_CONV):
            dw_ref[3 - k:4 - k, :] = jnp.sum(dc * _shift_down(x, k), axis=0, keepdims=True)

    return pl.pallas_call(
        body, grid=(ncols // HEAD_DIM,),
        in_specs=[pl.BlockSpec((S, dw_in), lambda j: (0, j)), pl.BlockSpec((S, HEAD_DIM), lambda j: (0, cb + j)),
                  pl.BlockSpec((DN_CONV, HEAD_DIM), lambda j: (0, cb + j))],
        out_specs=(pl.BlockSpec((S, HEAD_DIM), lambda j: (0, j)), pl.BlockSpec((DN_CONV, HEAD_DIM), lambda j: (0, j))),
        out_shape=(jax.ShapeDtypeStruct((S, ncols), BF16), jax.ShapeDtypeStruct((DN_CONV, ncols), F32)),
        compiler_params=_cp(("parallel",)), name=name)(dout, proj, conv_w)


def _dn_ab_fwd(proj, a_log, dt_bias, *, name, tm=512):
    S = proj.shape[0]
    tm = min(tm, S)
    ab = P0_AB // LANE

    def body(a_ref, b_ref, al_ref, dt_ref, g_ref, be_ref):
        g_ref[...] = -jnp.exp(al_ref[...]) * _softplus(a_ref[...] + dt_ref[...])
        be_ref[...] = _sigmoid(b_ref[...])

    vec = pl.BlockSpec((1, LANE), lambda i: (0, 0))
    out = pl.BlockSpec((tm, LANE), lambda i: (i, 0))
    return pl.pallas_call(
        body, grid=(S // tm,),
        in_specs=[pl.BlockSpec((tm, LANE), lambda i: (i, ab)), pl.BlockSpec((tm, LANE), lambda i: (i, ab + 1)), vec, vec],
        out_specs=(out, out), out_shape=(jax.ShapeDtypeStruct((S, LANE), F32),) * 2,
        compiler_params=_cp(("parallel",)), name=name)(proj, proj, a_log, dt_bias)


def _dn_ab_bwd(dg, dbeta, proj, a_log, dt_bias, *, name, tm=512):
    S = proj.shape[0]
    tm = min(tm, S)
    ab = P0_AB // LANE

    def body(dg_ref, db_ref, a_ref, b_ref, al_ref, dt_ref, dab_ref, dal_ref, ddt_ref):
        i = pl.program_id(0)
        ea = jnp.exp(al_ref[...])
        u = a_ref[...] + dt_ref[...]
        dgv = dg_ref[...]
        da = dgv * (-ea) * _sigmoid(u)
        be = _sigmoid(b_ref[...])
        dab_ref[:, 0:LANE] = da.astype(BF16)
        dab_ref[:, LANE:2 * LANE] = (db_ref[...] * be * (1.0 - be)).astype(BF16)
        dab_ref[:, 2 * LANE:] = jnp.zeros((tm, 2 * LANE), BF16)

        @pl.when(i == 0)
        def _():
            dal_ref[...] = jnp.zeros_like(dal_ref)
            ddt_ref[...] = jnp.zeros_like(ddt_ref)

        dal_ref[...] += jnp.sum(dgv * (-ea) * _softplus(u), axis=0, keepdims=True)
        ddt_ref[...] += jnp.sum(da, axis=0, keepdims=True)

    vec = pl.BlockSpec((1, LANE), lambda i: (0, 0))
    row = pl.BlockSpec((tm, LANE), lambda i: (i, 0))
    return pl.pallas_call(
        body, grid=(S // tm,),
        in_specs=[row, row, pl.BlockSpec((tm, LANE), lambda i: (i, ab)), pl.BlockSpec((tm, LANE), lambda i: (i, ab + 1)), vec, vec],
        out_specs=(pl.BlockSpec((tm, 4 * LANE), lambda i: (i, 0)), vec, vec),
        out_shape=(jax.ShapeDtypeStruct((S, 4 * LANE), BF16), jax.ShapeDtypeStruct((1, LANE), F32),
                   jax.ShapeDtypeStruct((1, LANE), F32)),
        compiler_params=_cp(("arbitrary",)), name=name)(dg, dbeta, proj, proj, a_log, dt_bias)


def _dot3(a, b):
    ah = a.astype(BF16)
    al = (a - ah.astype(F32)).astype(BF16)
    bh = b.astype(BF16)
    bl = (b - bh.astype(F32)).astype(BF16)
    d = lambda u, v: lax.dot_general(u, v, (NN, ((), ())), preferred_element_type=F32)
    return d(ah, bh) + (d(ah, bl) + d(al, bh))


DN_PAIR = 4
DN_QK = DN_PAIR // 2


def _dn_qk_heads(ref, rows):
    return jnp.stack([ref[rows, (hh // 2) * HEAD_DIM:(hh // 2 + 1) * HEAD_DIM] for hh in range(DN_PAIR)])


def _dn_big(shape, imap):
    return pl.BlockSpec(shape, imap, pipeline_mode=pl.Buffered(1))


_pdot = _hdot


def _tri_inverse(a):
    eye = (_iota2((CH, CH), 0) == _iota2((CH, CH), 1)).astype(F32)
    d3 = lambda u, v: jnp.stack([_dot3(u[i], v[i]) for i in range(DN_PAIR)])
    t = eye - a
    x = d3(a, a)
    n = 2
    while True:
        t = t + d3(t, x)
        n *= 2
        if n >= CH:
            break
        x = d3(x, x)
    return t


def _pick_col(m, n):
    return jnp.sum(jnp.where(_iota2(m.shape, 2) == n, m, 0.0), axis=2, keepdims=True)


def _dn_chunk_common(kk, qk, gc_c, gc_r, be_c):
    r_i = _iota2((CH, CH), 0)
    c_i = _iota2((CH, CH), 1)
    incl = r_i >= c_i
    strict = r_i > c_i
    dec = jnp.exp(jnp.where(incl, gc_c - gc_r, -1e30))
    e = jnp.exp(gc_c)
    gl = jnp.sum(jnp.where(_iota2((1, CH), 1) == CH - 1, gc_r, 0.0), axis=-1, keepdims=True)
    kds = jnp.exp(gl - gc_c)
    cd = jnp.exp(gl)
    a = jnp.where(strict, be_c * kk * dec, 0.0)
    p = qk * dec
    return dict(incl=incl, strict=strict, dec=dec, e=e, kds=kds, cd=cd, kk=kk, a=a, qk=qk, p=p)


def _dn_decay_tables(g_ref, b_ref, gcr, gcc, bcc):
    r_i = _iota2((CH, CH), 0)
    c_i = _iota2((CH, CH), 1)
    lc = (r_i >= c_i).astype(F32)
    eye = (r_i == c_i).astype(F32)
    for hh in range(DN_PAIR):
        g_rows_v = g_ref[hh]
        gcr[hh] = _fdot(g_rows_v, lc, NT)
        gcc[hh] = _fdot(lc, g_rows_v, NT)
        bcc[hh] = _fdot(eye, b_ref[hh], NT)
    return lc


def _dn_core_fwd(qn, kn, vc, g_rows, b_rows, out_g, *, name):
    S = qn.shape[0]
    nc = S // CH

    def body(q_ref, k_ref, v_ref, g_ref, b_ref, og_ref, o_ref, st_ref, t_ref, gcr, gcc, bcc):
        _dn_decay_tables(g_ref, b_ref, gcr, gcc, bcc)
        ogv = og_ref[...]

        def chunk(n, states):
            rows = pl.ds(pl.multiple_of(n * CH, CH), CH)
            q = _dn_qk_heads(q_ref, rows)
            k = _dn_qk_heads(k_ref, rows)
            kk = _pdot(k, k, NT)
            qk = _pdot(q, k, NT)
            v = jnp.stack([v_ref[rows, hh * HEAD_DIM:(hh + 1) * HEAD_DIM] for hh in range(DN_PAIR)])
            gc_c = _pick_col(gcc[...], n)
            be_c = _pick_col(bcc[...], n)
            gc_r = gcr[:, pl.ds(n, 1), :]
            c = _dn_chunk_common(kk, qk, gc_c, gc_r, be_c)
            t = _tri_inverse(c["a"])
            u0 = _pdot(t, be_c * v, NN)
            w = _pdot(t, (be_c * c["e"]) * k, NN)
            u = u0 - _pdot(w, states, NN)
            o = _pdot(c["e"] * q, states, NN) + _pdot(c["p"], u, NN)
            on = o * lax.rsqrt(jnp.mean(o * o, axis=-1, keepdims=True) + EPS) * ogv
            for hh in range(DN_PAIR):
                st_ref[hh, n] = states[hh]
                t_ref[hh, n] = t[hh]
                o_ref[rows, hh * HEAD_DIM:(hh + 1) * HEAD_DIM] = on[hh]
            return c["cd"] * states + _pdot(c["kds"] * k, u, TN)

        lax.fori_loop(0, nc, chunk, jnp.zeros((DN_PAIR, HEAD_DIM, HEAD_DIM), F32))

    qk_spec = _dn_big((S, DN_QK * HEAD_DIM), lambda h: (0, h))
    v_spec = _dn_big((S, DN_PAIR * HEAD_DIM), lambda h: (0, h))
    rows_spec = pl.BlockSpec((DN_PAIR, LANE, CH), lambda h: (h, 0, 0))
    return pl.pallas_call(
        body, grid=(DN_V_HEADS // DN_PAIR,),
        in_specs=[qk_spec, qk_spec, v_spec, rows_spec, rows_spec, pl.BlockSpec((1, HEAD_DIM), lambda h: (0, 0))],
        out_specs=(v_spec, _dn_big((DN_PAIR, nc, HEAD_DIM, HEAD_DIM), lambda h: (h, 0, 0, 0)),
                   _dn_big((DN_PAIR, nc, CH, CH), lambda h: (h, 0, 0, 0))),
        out_shape=(jax.ShapeDtypeStruct((S, MIX_WIDTH), F32), jax.ShapeDtypeStruct((DN_V_HEADS, nc, HEAD_DIM, HEAD_DIM), F32),
                   jax.ShapeDtypeStruct((DN_V_HEADS, nc, CH, CH), F32)),
        scratch_shapes=[pltpu.VMEM((DN_PAIR, LANE, CH), F32), pltpu.VMEM((DN_PAIR, CH, LANE), F32),
                        pltpu.VMEM((DN_PAIR, CH, LANE), F32)],
        compiler_params=_cp(("parallel",)), name=name)(qn, kn, vc, g_rows, b_rows, out_g.reshape(1, HEAD_DIM))


def _dn_chunk_bwd(q, k, v, kk, qk, state, t, gc_c, gc_r, be_c, don, ogv, ds_next):
    ones = jnp.ones((CH, LANE), F32)
    last_row = _iota2((CH, 1), 0) == CH - 1
    rowsum = lambda z: jnp.sum(z, axis=-1, keepdims=True)
    colsum = lambda z: jnp.sum(z, axis=-2, keepdims=True)
    c = _dn_chunk_common(kk, qk, gc_c, gc_r, be_c)
    e, kds, cd, dec, a, p = c["e"], c["kds"], c["cd"], c["dec"], c["a"], c["p"]
    vb = be_c * v
    kbe = (be_c * e) * k
    u0 = _pdot(t, vb, NN)
    w = _pdot(t, kbe, NN)
    u = u0 - _pdot(w, state, NN)
    qd = e * q
    kd = kds * k
    o = _pdot(qd, state, NN) + _pdot(p, u, NN)
    r = lax.rsqrt(jnp.mean(o * o, axis=-1, keepdims=True) + EPS)
    y = o * r
    dog = colsum(don * y)
    dy = don * ogv
    d_o = r * (dy - y * jnp.mean(dy * y, axis=-1, keepdims=True))
    du = _pdot(p, d_o, TN) + _pdot(kd, ds_next, NN)
    dqd = _pdot(d_o, state, NT)
    dstate = _pdot(qd, d_o, TN) + cd * ds_next - _pdot(w, du, TN)
    dcd = colsum(rowsum(ds_next * state))
    dkd = _pdot(u, ds_next, NT)
    dw = -_pdot(du, state, NT)
    dvb = _pdot(t, du, TN)
    dkbe = _pdot(t, dw, TN)
    da = -jnp.where(c["strict"], _pdot(dvb, u0, NT) + _pdot(dkbe, w, NT), 0.0)
    dp = jnp.where(c["incl"], _pdot(d_o, u, NT), 0.0)
    gmat = da * a + dp * p
    dad = da * dec
    x = be_c * dad
    dpd = dp * dec
    dk = _pdot(x, k, NN) + _pdot(x, k, TN) + _pdot(dpd, q, TN)
    dq = _pdot(dpd, k, NN) + e * dqd
    dbe = rowsum(dad * c["kk"])
    dgc = rowsum(gmat) + rowsum(dqd * q) * e
    rk = rowsum(dkd * k) * kds
    dk = dk + kds * dkd
    dgc = dgc - rk
    dgl = colsum(rk) + dcd * cd
    sk = rowsum(dkbe * k)
    dk = dk + (be_c * e) * dkbe
    dbe = dbe + sk * e + rowsum(dvb * v)
    dgc = dgc + sk * be_c * e
    dgc = dgc + jnp.where(last_row, dgl, 0.0)
    dgc = dgc - _pdot(gmat, ones, TN, dot=_fdot)
    return dq, dk, be_c * dvb, dgc, dbe, dog, dstate


def _dn_core_bwd(dmix, qn, kn, vc, g_rows, b_rows, out_g, states, tinv, *, name):
    S = qn.shape[0]
    nc = S // CH

    def body(do_ref, q_ref, k_ref, v_ref, g_ref, b_ref, og_ref, st_ref, t_ref,
             dq_ref, dk_ref, dv_ref, dg_ref, db_ref, dog_ref, gcr, gcc, bcc, dgc_acc):
        h = pl.program_id(0)
        lc = _dn_decay_tables(g_ref, b_ref, gcr, gcc, bcc)
        ogv = og_ref[...]
        dgc_acc[...] = jnp.zeros_like(dgc_acc)
        db_ref[...] = jnp.zeros_like(db_ref)
        lane_n = _iota2((CH, LANE), 1)

        @pl.when(h == 0)
        def _():
            dog_ref[...] = jnp.zeros_like(dog_ref)

        def chunk(m, carry):
            ds_nexts, dog = carry
            n = nc - 1 - m
            rows = pl.ds(pl.multiple_of(n * CH, CH), CH)
            q = _dn_qk_heads(q_ref, rows)
            k = _dn_qk_heads(k_ref, rows)
            kk = _pdot(k, k, NT)
            qk = _pdot(q, k, NT)
            heads = lambda ref: jnp.stack([ref[rows, hh * HEAD_DIM:(hh + 1) * HEAD_DIM] for hh in range(DN_PAIR)])
            state = jnp.stack([st_ref[hh, n] for hh in range(DN_PAIR)])
            t = jnp.stack([t_ref[hh, n] for hh in range(DN_PAIR)])
            dq, dk, dv, dgc, dbe, dog_h, dstate = _dn_chunk_bwd(
                q, k, heads(v_ref), kk, qk, state, t, _pick_col(gcc[...], n), gcr[:, pl.ds(n, 1), :],
                _pick_col(bcc[...], n), heads(do_ref), ogv, ds_nexts)
            for hh in range(DN_PAIR):
                dv_ref[rows, hh * HEAD_DIM:(hh + 1) * HEAD_DIM] = dv[hh]
            dgc_acc[...] = jnp.where(lane_n == n, dgc, dgc_acc[...])
            db_ref[...] = jnp.where(lane_n == n, dbe, db_ref[...])
            for i in range(DN_QK):
                dq_ref[rows, i * HEAD_DIM:(i + 1) * HEAD_DIM] = dq[2 * i] + dq[2 * i + 1]
                dk_ref[rows, i * HEAD_DIM:(i + 1) * HEAD_DIM] = dk[2 * i] + dk[2 * i + 1]
            return dstate, dog + jnp.sum(dog_h, axis=0)

        _, dog = lax.fori_loop(0, nc, chunk, (jnp.zeros((DN_PAIR, HEAD_DIM, HEAD_DIM), F32), jnp.zeros((1, HEAD_DIM), F32)))
        dog_ref[...] += dog
        for hh in range(DN_PAIR):
            dg_ref[hh] = _fdot(lc, dgc_acc[hh], TN)

    qk_spec = _dn_big((S, DN_QK * HEAD_DIM), lambda h: (0, h))
    v_spec = _dn_big((S, DN_PAIR * HEAD_DIM), lambda h: (0, h))
    rows_spec = pl.BlockSpec((DN_PAIR, LANE, CH), lambda h: (h, 0, 0))
    cols_spec = pl.BlockSpec((DN_PAIR, CH, LANE), lambda h: (h, 0, 0))
    vec = pl.BlockSpec((1, HEAD_DIM), lambda h: (0, 0))
    qk_out = jax.ShapeDtypeStruct((S, DN_QK_WIDTH), F32)
    return pl.pallas_call(
        body, grid=(DN_V_HEADS // DN_PAIR,),
        in_specs=[v_spec, qk_spec, qk_spec, v_spec, rows_spec, rows_spec, vec,
                  _dn_big((DN_PAIR, nc, HEAD_DIM, HEAD_DIM), lambda h: (h, 0, 0, 0)),
                  _dn_big((DN_PAIR, nc, CH, CH), lambda h: (h, 0, 0, 0))],
        out_specs=(qk_spec, qk_spec, v_spec, cols_spec, cols_spec, vec),
        out_shape=(qk_out, qk_out, jax.ShapeDtypeStruct((S, MIX_WIDTH), F32), jax.ShapeDtypeStruct((DN_V_HEADS, CH, LANE), F32),
                   jax.ShapeDtypeStruct((DN_V_HEADS, CH, LANE), F32), jax.ShapeDtypeStruct((1, HEAD_DIM), F32)),
        scratch_shapes=[pltpu.VMEM((DN_PAIR, LANE, CH), F32), pltpu.VMEM((DN_PAIR, CH, LANE), F32),
                        pltpu.VMEM((DN_PAIR, CH, LANE), F32), pltpu.VMEM((DN_PAIR, CH, LANE), F32)],
        compiler_params=_cp(("arbitrary",)), name=name)(
            dmix, qn, kn, vc, g_rows, b_rows, out_g.reshape(1, HEAD_DIM), states, tinv)


def _rows_form(x, nc):
    t = x[:, :DN_V_HEADS].T.reshape(DN_V_HEADS, nc, CH)
    return jnp.pad(t, ((0, 0), (0, LANE - nc), (0, 0)))


def _cols_to_nat(x, nc):
    t = jnp.transpose(x[:, :, :nc], (2, 1, 0)).reshape(nc * CH, DN_V_HEADS)
    return jnp.pad(t, ((0, 0), (0, LANE - DN_V_HEADS)))


_C_QKV = 2 * DN_QK_WIDTH + MIX_WIDTH


def _true_pieces(lo, hi):
    out = []
    while lo < hi:
        s = lo // P0_SHARD
        end = min(hi, (s + 1) * P0_SHARD)
        out.append((s, lo - s * P0_SHARD, end - s * P0_SHARD))
        lo = end
    return out


def _padded_pieces(lo, hi):
    a0, b0, x0 = _C_QKV, _C_QKV + DN_V_HEADS, _C_QKV + 2 * DN_V_HEADS
    out = []
    for t0, t1, shift in ((0, a0, 0), (a0, b0, P0_AB - a0), (b0, x0, P0_AB + LANE - b0), (x0, DN_PROJ, a0 - x0)):
        s, e = max(lo, t0), min(hi, t1)
        if s < e:
            out.append((s + shift, e + shift))
    return out


def _pad_lane(v):
    v = v.reshape(1, -1)
    return jnp.pad(v, ((0, 0), (0, LANE - v.shape[1])))


SLOT1 = SB_PROJ // N_CHIPS
MM_TN = 512


def _local_step(x, mem, target, norm_g, mem_norm_g, xa_q_g, xa_k_g, w_in0, conv_w, a_log, dt_bias, out_g, sb_q_g, sb_k_g,
                late_weights, early_grads):
    S = x.shape[0]
    nc = S // CH
    al = _pad_lane(a_log)
    dtb = _pad_lane(dt_bias)
    q_scale = HEAD_DIM ** -0.5
    tiles1 = SLOT1 // MM_TN

    kv_rhs = lambda l: pl.BlockSpec((N_CHIPS, None, D_MODEL // N_CHIPS, MM_TN), lambda i, j: (0, l, 0, j))
    kv_rhs_t = lambda l: pl.BlockSpec((None, None, D_MODEL // N_CHIPS, 2 * XA_WIDTH), lambda i, j: (j, l, 0, 0))
    out_rhs = lambda l: pl.BlockSpec((N_CHIPS, None, INNER // N_CHIPS, MM_TN), lambda i, j: (0, l, 0, j))
    out_rhs_t = lambda l: pl.BlockSpec((None, None, MM_TN, D_MODEL), lambda i, j: (j // 2, l, j % 2, 0))
    in1_rhs = pl.BlockSpec((None, 2, D_MODEL // 2, MM_TN), lambda i, j: (j // tiles1, 0, 0, j % tiles1))
    in1_rhs_t = pl.BlockSpec((None, None, D_MODEL // 2, MM_TN), lambda i, j, k: (k // tiles1, j, 0, k % tiles1))
    slot_rows = lambda rows: dict(
        tm=rows, o_spec=pl.BlockSpec((None, rows, MM_TN), lambda i, j: (i, 0, j)),
        o_shape=jax.ShapeDtypeStruct((N_CHIPS, rows, 2 * XA_WIDTH), BF16))
    in1_out = dict(tm=D_MODEL // 2, o_spec=pl.BlockSpec((None, None, D_MODEL // 2, MM_TN),
                                                        lambda i, j: (j // tiles1, i, 0, j % tiles1)),
                   o_shape=jax.ShapeDtypeStruct((N_CHIPS, 2, D_MODEL // 2, SLOT1), BF16))

    h0 = _rmsnorm_fwd(x, norm_g[0], name="norm0")
    proj0 = _matmul(h0, w_in0, name="proj0")
    qn = _dn_pre_fwd(proj0, conv_w, 0, DN_QK_WIDTH, l2=True, scale=q_scale, name="dn_pre_q")
    kn = _dn_pre_fwd(proj0, conv_w, DN_QK_WIDTH, DN_QK_WIDTH, l2=True, scale=1.0, name="dn_pre_k")
    vc = _dn_pre_fwd(proj0, conv_w, 2 * DN_QK_WIDTH, MIX_WIDTH, l2=False, scale=1.0, name="dn_pre_v")
    g_nat, b_nat = _dn_ab_fwd(proj0, al, dtb, name="dn_ab")
    g_rows = _rows_form(g_nat, nc)
    b_rows = _rows_form(b_nat, nc)
    mix0, states, tinv = _dn_core_fwd(qn, kn, vc, g_rows, b_rows, out_g, name="dn_core")
    w_kv, w_out, w_in1 = late_weights(mix0)
    mem_n = _rmsnorm_fwd(mem, mem_norm_g, name="mem_norm")
    kv = [_matmul(mem_n, w_kv, n=2 * XA_WIDTH, tn=MM_TN, b_spec=kv_rhs(l), name=f"kv{l}") for l in range(2)]
    xa0 = _xa_fwd(proj0, P0_XQ, kv[0], xa_q_g[0], xa_k_g[0], name="xa0")
    y0 = _gate_fwd(mix0, xa0, proj0, P0_Z, name="gate0")
    x1 = _matmul(y0, w_out, n=D_MODEL, tn=MM_TN, b_spec=out_rhs(0), res=x, name="out0")

    h1 = _rmsnorm_fwd(x1, norm_g[1], name="norm1")
    proj1 = _matmul(h1, w_in1, n=SB_PROJ, tn=MM_TN, b_spec=in1_rhs, name="proj1")
    mix1, tot1 = _sb_fwd(proj1, sb_q_g, sb_k_g, name="sb")
    xa1 = _xa_fwd(proj1, P1_XQ, kv[1], xa_q_g[1], xa_k_g[1], name="xa1")
    y1 = _gate_fwd(mix1, xa1, proj1, P1_Z, name="gate1")
    x2 = _matmul(y1, w_out, n=D_MODEL, tn=MM_TN, b_spec=out_rhs(1), res=x1, name="out1")

    dx2, loss_vec = _loss_head(x2, target, name="loss")

    d_wout1 = _matmul(y1, dx2, ta=True, name="d_wout1", **slot_rows(INNER // N_CHIPS))
    dy1 = _matmul(dx2, w_out, tb=True, n=INNER, tn=MM_TN, b_spec=out_rhs_t(1), name="dy1")
    dcat1, dz1 = _gate_bwd(dy1, mix1, xa1, proj1, P1_Z, name="gate1_bwd")
    dqkv1, d_sbq, d_sbk = _sb_bwd(dcat1, tot1, proj1, sb_q_g, sb_k_g, name="sb_bwd")
    dxq1, dkv1, d_xaq1, d_xak1 = _xa_bwd(dcat1, proj1, P1_XQ, kv[1], xa_q_g[1], xa_k_g[1], name="xa1_bwd")
    dproj1 = dqkv1 + [dxq1, dz1]
    d_win1 = _matmul(h1, dproj1, ta=True, name="d_win1", **in1_out)
    d_wkv1 = _matmul(mem_n, dkv1, ta=True, name="d_wkv1", **slot_rows(D_MODEL // N_CHIPS))
    token = early_grads(1, d_win1, d_wout1, d_wkv1)
    dh1 = _matmul(dproj1, w_in1, tb=True, n=D_MODEL, tn=D_MODEL // 2, tk=MM_TN, b_spec=in1_rhs_t, name="dh1")
    dx1, d_ng1 = _rmsnorm_bwd(dh1, x1, norm_g[1] + token[0, 0], dx2, name="norm1_bwd")

    d_wout0 = _matmul(y0, dx1, ta=True, name="d_wout0", **slot_rows(INNER // N_CHIPS))
    dy0 = _matmul(dx1, w_out, tb=True, n=INNER, tn=MM_TN, b_spec=out_rhs_t(0), name="dy0")
    dcat0, dz0 = _gate_bwd(dy0, mix0, xa0, proj0, P0_Z, name="gate0_bwd")
    dqv, dkv_h, dvc, dg_cols, db_cols, d_outg = _dn_core_bwd(
        dcat0, qn, kn, vc, g_rows, b_rows, out_g, states, tinv, name="dn_core_bwd")
    dpq, dwq = _dn_pre_bwd(dqv, proj0, conv_w, 0, DN_QK_WIDTH, l2=True, scale=q_scale, name="dn_pre_q_bwd")
    dpk, dwk = _dn_pre_bwd(dkv_h, proj0, conv_w, DN_QK_WIDTH, DN_QK_WIDTH, l2=True, scale=1.0, name="dn_pre_k_bwd")
    dpv, dwv = _dn_pre_bwd(dvc, proj0, conv_w, 2 * DN_QK_WIDTH, MIX_WIDTH, l2=False, scale=1.0, name="dn_pre_v_bwd")
    dab, d_alog, d_dt = _dn_ab_bwd(_cols_to_nat(dg_cols, nc), _cols_to_nat(db_cols, nc), proj0, al, dtb, name="dn_ab_bwd")
    dxq0, dkv0, d_xaq0, d_xak0 = _xa_bwd(dcat0, proj0, P0_XQ, kv[0], xa_q_g[0], xa_k_g[0], name="xa0_bwd")
    d_win0 = _matmul(h0, [dpq, dpk, dpv, dxq0, dz0, dab], ta=True, out_dtype=BF16, name="d_win0")
    d_wkv0 = _matmul(mem_n, dkv0, ta=True, name="d_wkv0", **slot_rows(D_MODEL // N_CHIPS))
    token = early_grads(0, d_win0, d_wout0, d_wkv0)
    zero = token[0, 0]
    dh0 = _matmul([dpq, dpk, dpv, dxq0, dz0, dab + zero.astype(BF16)], w_in0, tb=True, tk=MM_TN, name="dh0")
    dx0, d_ng0 = _rmsnorm_bwd(dh0, x, norm_g[0] + zero, dx1, name="norm0_bwd")

    dmem0 = _matmul(dkv0, w_kv, tb=True, n=D_MODEL, tn=D_MODEL // N_CHIPS, b_spec=kv_rhs_t(0), name="dmem0")
    dmem_n = _matmul(dkv1, w_kv, tb=True, n=D_MODEL, tn=D_MODEL // N_CHIPS, b_spec=kv_rhs_t(1), res=dmem0, name="dmem1")
    _, d_memg = _rmsnorm_bwd(dmem_n, mem, mem_norm_g, None, name="mem_norm_bwd")

    grads = dict(
        norm_g=jnp.concatenate([d_ng0, d_ng1], axis=0), mem_norm_g=d_memg.reshape(-1),
        xa_q_norm_g=jnp.concatenate([d_xaq0, d_xaq1], axis=0), xa_k_norm_g=jnp.concatenate([d_xak0, d_xak1], axis=0),
        dn_conv_w=jnp.concatenate([dwq, dwk, dwv], axis=1),
        dn_a_log=d_alog[:, :DN_V_HEADS], dn_dt_bias=d_dt[:, :DN_V_HEADS], dn_out_norm_g=d_outg,
        sb_q_norm_g=d_sbq, sb_k_norm_g=d_sbk)
    return loss_vec, dx0, grads


ANY = pl.BlockSpec(memory_space=pl.ANY)


def _place():
    x, y, c = lax.axis_index("x"), lax.axis_index("y"), lax.axis_index("c")
    chips = [(1 - x, y), (x, 1 - y), (1 - x, 1 - y)]
    return x, y, c, 2 * x + y, (x, y, 1 - c), chips


def _rcopy(src, dst, send, recv, i, dev):
    return pltpu.make_async_remote_copy(src_ref=src, dst_ref=dst, send_sem=send.at[i], recv_sem=recv.at[i],
                                        device_id=dev, device_id_type=MESH)


def _swap_halves(xs, *, name):
    nt = len(xs)

    def body(*refs):
        src, dst = refs[:nt], refs[nt:2 * nt]
        send, recv = refs[2 * nt:]
        x, y, c, j, sib, chips = _place()
        cps = []
        for t in range(nt):
            for s in range(N_CHIPS):
                cps.append(_rcopy(src[t].at[s, 1 - c], dst[t].at[s], send, recv, 4 * t + s, sib))
                cps[-1].start()
        for cp in cps:
            cp.wait_recv()
        for cp in cps:
            cp.wait_send()

    return pl.pallas_call(
        body, in_specs=[ANY] * nt, out_specs=[ANY] * nt,
        out_shape=[jax.ShapeDtypeStruct((N_CHIPS,) + a.shape[2:], a.dtype) for a in xs],
        scratch_shapes=[pltpu.SemaphoreType.DMA((4 * nt,)), pltpu.SemaphoreType.DMA((4 * nt,))], name=name)(*xs)


def _swap_with_sibling(fs, *, name):
    nt = len(fs)

    def body(*refs):
        src, dst = refs[:nt], refs[nt:2 * nt]
        send, recv = refs[2 * nt:]
        x, y, c, j, sib, chips = _place()
        cps = [_rcopy(src[t], dst[t], send, recv, t, sib) for t in range(nt)]
        for cp in cps:
            cp.start()
        for cp in cps:
            cp.wait_recv()
        for cp in cps:
            cp.wait_send()

    return pl.pallas_call(
        body, in_specs=[ANY] * nt, out_specs=[ANY] * nt,
        out_shape=[jax.ShapeDtypeStruct(a.shape, a.dtype) for a in fs],
        scratch_shapes=[pltpu.SemaphoreType.DMA((nt,)), pltpu.SemaphoreType.DMA((nt,))], name=name)(*fs)


HBM_SPEC = pl.BlockSpec(memory_space=pltpu.HBM)
SEM_SPEC = pl.BlockSpec(memory_space=pltpu.SEMAPHORE)
SIDE_EFFECT = pltpu.SideEffectType.DATAFLOW_SIDE_EFFECTING


def _gather_plan(src, land):
    x, y, c, j, sib, chips = _place()
    return [(src[t].at[c], land[t].at[j, c], (cx, cy, c), land[t].at[2 * cx + cy, c])
            for t in range(len(src)) for cx, cy in chips]


def _scatter_plan(src, land):
    x, y, c, j, sib, chips = _place()
    return [(src[t].at[2 * cx + cy], land[t].at[k], (cx, cy, c), land[t].at[k])
            for t in range(len(src)) for k, (cx, cy) in enumerate(chips)]


def _exchange_start(srcs, lands, plan, *, name):
    ns, nb = len(srcs), len(srcs) + len(lands)
    n = 3 * ns

    def body(*refs):
        send, recv, token = refs[nb], refs[nb + 1], refs[-1]
        for i, (s, d, dev, _) in enumerate(plan(refs[:ns], refs[ns:nb])):
            _rcopy(s, d, send, recv, i, dev).start()
        token[...] = jnp.zeros_like(token)

    bufs = list(srcs) + list(lands)
    outs = pl.pallas_call(
        body, name=name,
        out_shape=(pltpu.SemaphoreType.DMA((n,)), pltpu.SemaphoreType.DMA((n,)), *[pltpu.HBM(a.shape, a.dtype) for a in bufs],
                   jax.ShapeDtypeStruct((8, LANE), F32)),
        in_specs=[HBM_SPEC] * nb, out_specs=(SEM_SPEC, SEM_SPEC, *[HBM_SPEC] * nb, pl.BlockSpec(memory_space=pltpu.VMEM)),
        input_output_aliases={i: 2 + i for i in range(nb)},
        compiler_params=pltpu.CompilerParams(has_side_effects=SIDE_EFFECT))(
            *[pltpu.with_memory_space_constraint(a, pltpu.HBM) for a in bufs])
    return outs[0], outs[1], list(outs[2:2 + ns]), list(outs[2 + ns:2 + nb]), outs[-1]


def _exchange_wait(srcs, lands, send, recv, after, plan, *, name):
    ns, nb = len(srcs), len(srcs) + len(lands)
    afters = list(after) if isinstance(after, (list, tuple)) else [after]

    def body(*refs):
        send_s, recv_s = refs[nb], refs[nb + 1]
        for i, (s, d, dev, inc) in enumerate(plan(refs[:ns], refs[ns:nb])):
            _rcopy(s, d, send_s, recv_s, i, dev).wait_send()
            _rcopy(inc, inc, send_s, recv_s, i, dev).wait_recv()

    bufs = list(srcs) + list(lands)
    outs = pl.pallas_call(
        body, name=name, out_shape=tuple(pltpu.HBM(a.shape, a.dtype) for a in bufs),
        in_specs=[HBM_SPEC] * nb + [SEM_SPEC, SEM_SPEC] + [ANY] * len(afters), out_specs=tuple([HBM_SPEC] * nb),
        input_output_aliases={i: i for i in range(nb)},
        compiler_params=pltpu.CompilerParams(has_side_effects=SIDE_EFFECT))(*bufs, send, recv, *afters)
    return list(outs[:ns]), list(outs[ns:])


def _forward_halves(lands, *, name):
    nt = len(lands)

    def body(*refs):
        src, dst = refs[:nt], refs[nt:2 * nt]
        send, recv = refs[2 * nt:]
        x, y, c, j, sib, chips = _place()
        cps = []
        for t in range(nt):
            for k, (cx, cy) in enumerate(chips):
                cps.append(_rcopy(src[t].at[2 * cx + cy, c], dst[t].at[2 * cx + cy, c], send, recv, 3 * t + k, sib))
                cps[-1].start()
        for t in range(nt):
            for k, (cx, cy) in enumerate(chips):
                other = dst[t].at[2 * cx + cy, 1 - c]
                _rcopy(other, other, send, recv, 3 * t + k, sib).wait_recv()
        for cp in cps:
            cp.wait_send()

    return pl.pallas_call(
        body, in_specs=[ANY] * nt, out_specs=[ANY] * nt, out_shape=[jax.ShapeDtypeStruct(a.shape, a.dtype) for a in lands],
        input_output_aliases={t: t for t in range(nt)},
        scratch_shapes=[pltpu.SemaphoreType.DMA((3 * nt,)), pltpu.SemaphoreType.DMA((3 * nt,))], name=name)(*lands)


def _all_reduce_small(parts, *, name):
    n = len(parts)
    offs, rows = [], 0
    for p in parts:
        offs.append(rows)
        rows += -(-p.shape[0] // 8) * 8

    def body(*refs):
        p_refs, o_refs = refs[:n], refs[n:2 * n]
        buf, send, recv = refs[2 * n:]
        x, y, c = lax.axis_index("x"), lax.axis_index("y"), lax.axis_index("c")
        me = 4 * x + 2 * y + c
        buf[me] = jnp.zeros((rows, LANE), F32)
        for p_ref, off in zip(p_refs, offs):
            buf[me, off:off + p_ref.shape[0], :] = p_ref[...]
        cps = []
        for r in range(1, 8):
            dev = (x ^ (r >> 2), y ^ ((r >> 1) & 1), c ^ (r & 1))
            cps.append(_rcopy(buf.at[me], buf.at[me], send, recv, r - 1, dev))
            cps[-1].start()
        for r in range(1, 8):
            frm = buf.at[me ^ r]
            _rcopy(frm, frm, send, recv, r - 1, (x, y, c)).wait_recv()
        for cp in cps:
            cp.wait_send()
        acc = buf[0]
        for d in range(1, 8):
            acc = acc + buf[d]
        for o_ref, off in zip(o_refs, offs):
            o_ref[...] = acc[off:off + o_ref.shape[0], :]

    vm = pl.BlockSpec(memory_space=pltpu.VMEM)
    return pl.pallas_call(
        body, in_specs=[vm] * n, out_specs=[vm] * n, out_shape=[jax.ShapeDtypeStruct(p.shape, F32) for p in parts],
        scratch_shapes=[pltpu.VMEM((8, rows, LANE), F32), pltpu.SemaphoreType.DMA((7,)), pltpu.SemaphoreType.DMA((7,))],
        name=name)(*parts)


def _add_halves(x, b, c_idx, *, name, tr=256):
    _, _, R, C = x.shape
    tr = min(tr, R)

    def body(c_ref, x_ref, b_ref, o_ref):
        o_ref[...] = (x_ref[...].astype(F32) + b_ref[...].astype(F32)).astype(o_ref.dtype)

    return pl.pallas_call(
        body,
        grid_spec=pltpu.PrefetchScalarGridSpec(
            num_scalar_prefetch=1, grid=(N_CHIPS, R // tr),
            in_specs=[pl.BlockSpec((None, None, tr, C), lambda s, i, c_ref: (s, c_ref[0], i, 0)),
                      pl.BlockSpec((None, tr, C), lambda s, i, c_ref: (s, i, 0))],
            out_specs=pl.BlockSpec((None, tr, C), lambda s, i, c_ref: (s, i, 0))),
        out_shape=jax.ShapeDtypeStruct(b.shape, b.dtype), compiler_params=_cp(("parallel", "parallel")), name=name)(c_idx, x, b)


def _sum_slot(p, rcv, j_idx, *, name, tr=256):
    _, R, C = p.shape
    tr = min(tr, R)

    def body(j_ref, p_ref, r_ref, o_ref):
        acc = p_ref[...].astype(F32)
        for k in range(3):
            acc = acc + r_ref[k].astype(F32)
        o_ref[...] = acc

    return pl.pallas_call(
        body,
        grid_spec=pltpu.PrefetchScalarGridSpec(
            num_scalar_prefetch=1, grid=(R // tr,),
            in_specs=[pl.BlockSpec((None, tr, C), lambda i, j_ref: (j_ref[0], i, 0)),
                      pl.BlockSpec((3, tr, C), lambda i, j_ref: (0, i, 0))],
            out_specs=pl.BlockSpec((tr, C), lambda i, j_ref: (i, 0))),
        out_shape=jax.ShapeDtypeStruct((R, C), F32), compiler_params=_cp(("parallel",)), name=name)(j_idx, p, rcv)


def _adamw_math(w, g, m, v):
    nm = ADAM_B1 * m + (1.0 - ADAM_B1) * g
    nv = ADAM_B2 * v + (1.0 - ADAM_B2) * (g * g)
    m_hat = nm / (1.0 - ADAM_B1 ** ADAM_STEP)
    v_hat = nv / (1.0 - ADAM_B2 ** ADAM_STEP)
    return -ADAM_LR * (m_hat / (jnp.sqrt(v_hat) + ADAM_EPS) + ADAM_WD * w), nm, nv


def _adamw_halves(w, g_mine, g_theirs, m, v, c_idx, *, name, layer=0, into=None, tr=128):
    _, _, R, C = w.shape
    tr = tr if R % tr == 0 else R

    def body(c_ref, w_ref, gm_ref, gt_ref, m_ref, v_ref, *rest):
        g_ref, d_ref, nm_ref, nv_ref = rest[-4:]
        gv = jnp.where(pl.program_id(0) == c_ref[0], gm_ref[...], gt_ref[...])
        d, nm, nv = _adamw_math(w_ref[...], gv, m_ref[...], v_ref[...])
        g_ref[...] = gv
        d_ref[...] = d
        nm_ref[...] = nm
        nv_ref[...] = nv

    full = pl.BlockSpec((None, None, tr, C), lambda hh, i, c_ref: (layer, hh, i, 0))
    half = pl.BlockSpec((tr, C), lambda hh, i, c_ref: (i, 0))
    sh = jax.ShapeDtypeStruct(w.shape, F32)
    extra = [] if into is None else list(into)
    return pl.pallas_call(
        body,
        grid_spec=pltpu.PrefetchScalarGridSpec(num_scalar_prefetch=1, grid=(2, R // tr),
                                               in_specs=[full, half, half, full, full] + [ANY] * len(extra),
                                               out_specs=(full,) * 4),
        out_shape=(sh,) * 4, input_output_aliases={6 + t: t for t in range(len(extra))},
        compiler_params=_cp(("parallel", "parallel")), name=name)(c_idx, w, g_mine, g_theirs, m, v, *extra)


def _adamw_parts(ws, gs, ms, vs, *, name):
    n = len(ws)

    def body(*refs):
        ins, outs = refs[:4 * n], refs[4 * n:]
        for t in range(n):
            d, nm, nv = _adamw_math(ins[t][...], ins[n + t][...], ins[2 * n + t][...], ins[3 * n + t][...])
            outs[t][...] = d
            outs[n + t][...] = nm
            outs[2 * n + t][...] = nv

    vm = pl.BlockSpec(memory_space=pltpu.VMEM)
    shapes = [jax.ShapeDtypeStruct(w.shape, F32) for w in ws] * 3
    outs = pl.pallas_call(body, in_specs=[vm] * (4 * n), out_specs=[vm] * (3 * n), out_shape=shapes, name=name)(
        *ws, *gs, *ms, *vs)
    return outs[:n], outs[n:2 * n], outs[2 * n:]


_SMALL = ["norm_g", "mem_norm_g", "xa_q_norm_g", "xa_k_norm_g", "dn_a_log", "dn_dt_bias", "dn_out_norm_g",
          "sb_q_norm_g", "sb_k_norm_g"]


def _rows128(a):
    flat = a.reshape(-1)
    pad = -flat.shape[0] % LANE
    if pad:
        flat = jnp.pad(flat, (0, pad))
    return flat.reshape(-1, LANE)


def _unrows(r, shape):
    return r.reshape(-1)[:math.prod(shape)].reshape(shape)


def kernel(x, mem, norm_g, mem_norm_g, mem_w_kv, xa_q_norm_g, xa_k_norm_g, w_out, dn_w_in, dn_conv_w, dn_a_log, dn_dt_bias, dn_out_norm_g, sb_w_in, sb_q_norm_g, sb_k_norm_g, loss_target, m_norm_g, m_mem_norm_g, m_mem_w_kv, m_xa_q_norm_g, m_xa_k_norm_g, m_w_out, m_dn_w_in, m_dn_conv_w, m_dn_a_log, m_dn_dt_bias, m_dn_out_norm_g, m_sb_w_in, m_sb_q_norm_g, m_sb_k_norm_g, v_norm_g, v_mem_norm_g, v_mem_w_kv, v_xa_q_norm_g, v_xa_k_norm_g, v_w_out, v_dn_w_in, v_dn_conv_w, v_dn_a_log, v_dn_dt_bias, v_dn_out_norm_g, v_sb_w_in, v_sb_q_norm_g, v_sb_k_norm_g):
    W = dict(norm_g=norm_g, mem_norm_g=mem_norm_g, mem_w_kv=mem_w_kv, xa_q_norm_g=xa_q_norm_g, xa_k_norm_g=xa_k_norm_g,
             w_out=w_out, dn_w_in=dn_w_in, dn_conv_w=dn_conv_w, dn_a_log=dn_a_log, dn_dt_bias=dn_dt_bias,
             dn_out_norm_g=dn_out_norm_g, sb_w_in=sb_w_in, sb_q_norm_g=sb_q_norm_g, sb_k_norm_g=sb_k_norm_g)
    M = dict(norm_g=m_norm_g, mem_norm_g=m_mem_norm_g, mem_w_kv=m_mem_w_kv, xa_q_norm_g=m_xa_q_norm_g,
             xa_k_norm_g=m_xa_k_norm_g, w_out=m_w_out, dn_w_in=m_dn_w_in, dn_conv_w=m_dn_conv_w, dn_a_log=m_dn_a_log,
             dn_dt_bias=m_dn_dt_bias, dn_out_norm_g=m_dn_out_norm_g, sb_w_in=m_sb_w_in, sb_q_norm_g=m_sb_q_norm_g,
             sb_k_norm_g=m_sb_k_norm_g)
    V = dict(norm_g=v_norm_g, mem_norm_g=v_mem_norm_g, mem_w_kv=v_mem_w_kv, xa_q_norm_g=v_xa_q_norm_g,
             xa_k_norm_g=v_xa_k_norm_g, w_out=v_w_out, dn_w_in=v_dn_w_in, dn_conv_w=v_dn_conv_w, dn_a_log=v_dn_a_log,
             dn_dt_bias=v_dn_dt_bias, dn_out_norm_g=v_dn_out_norm_g, sb_w_in=v_sb_w_in, sb_q_norm_g=v_sb_q_norm_g,
             sb_k_norm_g=v_sb_k_norm_g)
    names = ["norm_g", "mem_norm_g", "mem_w_kv", "xa_q_norm_g", "xa_k_norm_g", "w_out", "dn_w_in", "dn_conv_w",
             "dn_a_log", "dn_dt_bias", "dn_out_norm_g", "sb_w_in", "sb_q_norm_g", "sb_k_norm_g"]
    cx, cy, cc = lax.axis_index("x"), lax.axis_index("y"), lax.axis_index("c")
    slot = 2 * cx + cy
    half_r = D_MODEL // 2
    conv_cols = dn_conv_w.shape[2]

    w0s = jnp.pad(dn_w_in[0].astype(BF16), ((0, 0), (0, P0_SHARD_PAD - P0_SHARD))).reshape(2, half_r, P0_SHARD_PAD)
    w1s = sb_w_in[0].astype(BF16).reshape(2, half_r, SB_PROJ // N_CHIPS)
    convs = jnp.pad(dn_conv_w[0], ((0, 8 - DN_CONV), (0, 0))).reshape(8, 2, conv_cols // 2).transpose(1, 0, 2)
    c_idx = jnp.reshape(cc, (1,)).astype(jnp.int32)
    j_idx = jnp.reshape(slot, (1,)).astype(jnp.int32)
    own_a = [w0s, convs]
    own_b = [w1s, w_out.astype(BF16), mem_w_kv.astype(BF16)]
    lands_a = [lax.dynamic_update_slice(lax.empty((N_CHIPS,) + o.shape, o.dtype), o[None], (slot, 0, 0, 0)) for o in own_a]
    send_a, recv_a, own_a, lands_a, token_a = _exchange_start(own_a, lands_a, _gather_plan, name="gather_start")
    M["dn_w_in"] = m_dn_w_in + token_a[0, 0]
    V["dn_w_in"] = v_dn_w_in + token_a[0, 0]
    view0 = (1, 2, half_r, P0_SHARD)
    _, lands_a = _exchange_wait(own_a, lands_a, send_a, recv_a, [M["dn_w_in"].reshape(view0), V["dn_w_in"].reshape(view0)],
                                _gather_plan, name="gather_wait")
    (g0, gconv), own_b = lax.optimization_barrier((_forward_halves(lands_a, name="gather_forward"), own_b))
    lands_b = [lax.dynamic_update_slice(lax.empty((N_CHIPS,) + o.shape, o.dtype), o[None], (slot, 0, 0, 0)) for o in own_b]
    send_b, recv_b, own_b, lands_b, token_b = _exchange_start(own_b, lands_b, _gather_plan, name="gather_late_start")

    def late_weights(after):
        _, lands = _exchange_wait(own_b, lands_b, send_b, recv_b, after, _gather_plan, name="gather_late_wait")
        g1, gout, gkv = _forward_halves(lands, name="gather_late_forward")
        return gkv, gout, g1

    rs = {}

    def scatter_start(tag, xs):
        from_sib = _swap_halves(xs, name=f"rs{tag}_swap")
        ps = [_add_halves(a, b, c_idx, name=f"rs{tag}_add{t}") for t, (a, b) in enumerate(zip(xs, from_sib))]
        rcv = [lax.empty((3,) + p.shape[1:], p.dtype) for p in ps]
        send, recv, ps, rcv, token = _exchange_start(ps, rcv, _scatter_plan, name=f"rs{tag}_scatter_start")
        rs[tag] = (ps, rcv, send, recv)
        return token

    def scatter_finish(tag, after):
        ps, rcv, send, recv = rs[tag]
        ps, rcv = _exchange_wait(ps, rcv, send, recv, after, _scatter_plan, name=f"rs{tag}_scatter_wait")
        return [_sum_slot(p, r, j_idx, name=f"rs{tag}_sum{t}") for t, (p, r) in enumerate(zip(ps, rcv))]

    def early_grads(layer, d_win, d_wout, d_wkv):
        if layer == 0:
            d_win = jnp.stack([jnp.pad(jnp.concatenate(
                [d_win[:, lo:hi] for lo, hi in _padded_pieces(s * P0_SHARD, (s + 1) * P0_SHARD)], axis=1),
                ((0, 0), (0, P0_SHARD_PAD - P0_SHARD))) for s in range(N_CHIPS)]).reshape(N_CHIPS, 2, half_r, P0_SHARD_PAD)
        return scatter_start(layer, [d_win, d_wout.reshape(N_CHIPS, 2, -1, D_MODEL), d_wkv.reshape(N_CHIPS, 2, -1, 2 * XA_WIDTH)])

    shards0 = g0.reshape(N_CHIPS, D_MODEL, P0_SHARD_PAD)
    z = lambda n: jnp.zeros((D_MODEL, n), BF16)
    w_in0 = jnp.concatenate(
        [shards0[s][:, lo:hi] for s, lo, hi in _true_pieces(0, _C_QKV) + _true_pieces(_C_QKV + 2 * DN_V_HEADS, DN_PROJ)]
        + [shards0[s][:, lo:hi] for s, lo, hi in _true_pieces(_C_QKV, _C_QKV + DN_V_HEADS)] + [z(LANE - DN_V_HEADS)]
        + [shards0[s][:, lo:hi] for s, lo, hi in _true_pieces(_C_QKV + DN_V_HEADS, _C_QKV + 2 * DN_V_HEADS)]
        + [z(P0 - P0_AB - LANE - DN_V_HEADS)], axis=1)
    conv_f = gconv.transpose(2, 0, 1, 3).reshape(8, N_CHIPS * conv_cols)[:DN_CONV]

    loss_vec, grad_x, g = _local_step(
        x[0], mem[0], loss_target[0], norm_g + token_b[0, 0], mem_norm_g, xa_q_norm_g, xa_k_norm_g, w_in0, conv_f,
        dn_a_log[0], dn_dt_bias[0], dn_out_norm_g[0], sb_q_norm_g[0], sb_k_norm_g[0], late_weights, early_grads)

    mine1 = scatter_finish(1, grad_x)
    theirs1 = _swap_with_sibling(mine1, name="rs1_join")
    big1 = [("sb_w_in", None), ("w_out", 1), ("mem_w_kv", 1)]
    big0 = [("dn_w_in", None), ("w_out", 0), ("mem_w_kv", 0)]

    out_g, out_d, out_m, out_v = {}, {}, {}, {}
    partial = {}

    def adamw_big(big, mine, theirs):
        for (n, layer), mine_g, their_g in zip(big, mine, theirs):
            layers = 1 if layer is None else 2
            view = (layers, 2) + mine_g.shape
            partial[n] = _adamw_halves(W[n].reshape(view), mine_g, their_g, M[n].reshape(view), V[n].reshape(view), c_idx,
                                       layer=layer or 0, into=partial.get(n), name=f"adamw_{n}" + ("" if layer is None else str(layer)))
        return [partial[n][0] for n, _ in big]

    done1 = lax.optimization_barrier(tuple(adamw_big(big1, mine1, theirs1)))[-1]
    mine0 = scatter_finish(0, done1)
    mine0[0] = mine0[0][:, :P0_SHARD]

    parts, _ = lax.optimization_barrier(([_rows128(g[n]) for n in _SMALL] + [_rows128(g["dn_conv_w"]), loss_vec], mine0[0]))
    red = _all_reduce_small(parts, name="all_reduce_small")
    small_rows = dict(zip(_SMALL, red))
    conv_full = red[len(_SMALL)].reshape(DN_CONV, N_CHIPS * conv_cols)
    small_rows["dn_conv_w"] = _rows128(lax.dynamic_slice_in_dim(conv_full, slot * conv_cols, conv_cols, axis=1))
    loss = red[-1][0, 0]

    adamw_big(big0, mine0, _swap_with_sibling(mine0, name="rs0_join"))
    for n, outs in partial.items():
        out_g[n], out_d[n], out_m[n], out_v[n] = [o.reshape(W[n].shape) for o in outs]
    small_names = _SMALL + ["dn_conv_w"]
    ds, nms, nvs = _adamw_parts([_rows128(W[n]) for n in small_names], [small_rows[n] for n in small_names],
                                [_rows128(M[n]) for n in small_names], [_rows128(V[n]) for n in small_names], name="adamw_small")
    for n, d, nm, nv in zip(small_names, ds, nms, nvs):
        shp = W[n].shape
        out_g[n], out_d[n], out_m[n], out_v[n] = [_unrows(r, shp) for r in (small_rows[n], d, nm, nv)]

    return (loss, grad_x[None], *[out_g[n] for n in names], *[out_d[n] for n in names], *[out_m[n] for n in names],
            *[out_v[n] for n in names])
```

```python
import math

import jax
import jax.numpy as jnp
from jax import lax
from jax.experimental import pallas as pl
from jax.experimental.pallas import tpu as pltpu

F32 = jnp.float32
BF16 = jnp.bfloat16
HI = lax.Precision.HIGHEST
MESH = pl.DeviceIdType.MESH

D_MODEL = 2048
INNER = 4096
XA_WIDTH = 1024
XA_HEADS = 4
XA_DIM = 256
MIX_WIDTH = 3072
HEAD_DIM = 128
DN_V_HEADS = 24
DN_QK_WIDTH = 1536
DN_CONV = 4
DN_PROJ = 11312
SB_PROJ = 14336
EPS = 1e-6
N_CHIPS = 4

CH = 128
LANE = 128

P0_XQ = 6144
P0_Z = 7168
P0_AB = 11264
P0 = 11776
P0_SHARD = DN_PROJ // N_CHIPS
P0_SHARD_PAD = 2944
P1_XQ = 9216
P1_Z = 10240
P1 = SB_PROJ

ADAM_LR = 0.001
ADAM_B1 = 0.9
ADAM_B2 = 0.999
ADAM_EPS = 1e-08
ADAM_WD = 0.01
ADAM_STEP = 10

VMEM_LIMIT = 48 * 1024 * 1024


def _cp(sem=None, **kw):
    return pltpu.CompilerParams(dimension_semantics=sem, vmem_limit_bytes=VMEM_LIMIT, **kw)


def _bdot(a, b, dims):
    return lax.dot_general(a.astype(BF16), b.astype(BF16), (dims, ((), ())), preferred_element_type=F32)


def _fdot(a, b, dims):
    return lax.dot_general(a, b, (dims, ((), ())), precision=HI, preferred_element_type=F32)


NN = ((1,), (0,))
NT = ((1,), (1,))
TN = ((0,), (0,))


def _sigmoid(x):
    return 1.0 / (1.0 + jnp.exp(-x))


def _softplus(x):
    return jnp.maximum(x, 0.0) + jnp.log(1.0 + jnp.exp(-jnp.abs(x)))


def _iota2(shape, axis):
    return lax.broadcasted_iota(jnp.int32, shape, axis)


MM_FULL_K = 4096
MM_BLOCK_BYTES = 4 * 1024 * 1024


def _matmul(a, b, *, ta=False, tb=False, out_dtype=F32, res=None, name, n=None, tm=None, tn=None, tk=None,
            b_spec=None, o_spec=None, o_shape=None):
    a_segs = list(a) if isinstance(a, (list, tuple)) else [a]
    b_segs = list(b) if isinstance(b, (list, tuple)) else [b]
    a0, b0 = a_segs[0], b_segs[0]
    M = a0.shape[1] if ta else a0.shape[0]
    K = a0.shape[0] if ta else sum(s.shape[1] for s in a_segs)
    if n is None:
        n = b0.shape[0] if tb else sum(s.shape[1] for s in b_segs)
    N = n
    dims = ((0,) if ta else (1,), (1,) if tb else (0,))
    has_res = res is not None
    flat = lambda v: v.reshape(-1, v.shape[-1])
    o_shape = o_shape or jax.ShapeDtypeStruct((M, N), out_dtype)

    def seg_specs(segs, tile, block, pos):
        specs, ranges, off = [], [], 0
        for s in segs:
            cnt = s.shape[1] // tile
            assert s.shape[1] % tile == 0, (name, s.shape, tile)

            def imap(*g, off=off, cnt=cnt):
                t = jnp.clip(g[pos] - off, 0, cnt - 1)
                return (g[0], t) if pos == 2 else (0, t)

            specs.append(pl.BlockSpec(block, imap))
            ranges.append((off, off + cnt))
            off += cnt
        return specs, ranges

    if K <= MM_FULL_K:
        assert len(a_segs) == 1
        tm = tm or min(M, 1024, max(256, MM_BLOCK_BYTES // (K * a0.dtype.itemsize)))
        tn = tn or min(N, 512)
        assert M % tm == 0 and N % tn == 0, (name, M, N, K, tm, tn)
        nb = len(b_segs)
        if b_spec is not None:
            b_specs, b_ranges = [b_spec], [(0, N // tn)]
        elif nb > 1:
            assert not tb
            b_specs, b_ranges = seg_specs(b_segs, tn, (K, tn), 1)
        else:
            b_specs = [pl.BlockSpec((tn, K), lambda i, j: (j, 0)) if tb else pl.BlockSpec((K, tn), lambda i, j: (0, j))]
            b_ranges = [(0, N // tn)]

        def body_full(*refs):
            a_ref, b_refs = refs[0], refs[1:1 + nb]
            r_ref = refs[1 + nb] if has_res else None
            o_ref = refs[-1]
            j = pl.program_id(1)
            for b_ref, (lo, hi) in zip(b_refs, b_ranges):
                def emit(b_ref=b_ref):
                    r = _bdot(a_ref[...], flat(b_ref[...]), dims)
                    if has_res:
                        r = r + r_ref[...]
                    o_ref[...] = r.astype(o_ref.dtype).reshape(o_ref.shape)
                if nb == 1:
                    emit()
                else:
                    pl.when(jnp.logical_and(j >= lo, j < hi))(emit)

        a_spec = pl.BlockSpec((K, tm), lambda i, j: (0, i)) if ta else pl.BlockSpec((tm, K), lambda i, j: (i, 0))
        o_spec = o_spec or pl.BlockSpec((tm, tn), lambda i, j: (i, j))
        r_spec = [pl.BlockSpec((tm, tn), lambda i, j: (i, j))] if has_res else []
        return pl.pallas_call(
            body_full, grid=(M // tm, N // tn), in_specs=[a_spec] + b_specs + r_spec, out_specs=o_spec, out_shape=o_shape,
            compiler_params=_cp(("parallel", "arbitrary")), name=name)(*([a0] + b_segs + ([res] if has_res else [])))

    assert tb and not ta and len(b_segs) == 1
    tm, tn = tm or min(M, 1024), tn or min(N, 1024)
    tk = tk or (1024 if all(s.shape[1] % 1024 == 0 for s in a_segs) else 512)
    assert M % tm == 0 and N % tn == 0 and K % tk == 0, (name, M, N, K, tm, tn, tk)
    nk = K // tk
    na = len(a_segs)
    if na > 1:
        a_specs, a_ranges = seg_specs(a_segs, tk, (tm, tk), 2)
    else:
        a_specs, a_ranges = [pl.BlockSpec((tm, tk), lambda i, j, k: (i, k))], [(0, nk)]
    b_spec = b_spec or pl.BlockSpec((tn, tk), lambda i, j, k: (j, k))

    def body(*refs):
        a_refs, b_ref = refs[:na], refs[na]
        r_ref = refs[na + 1] if has_res else None
        o_ref, acc = refs[-2], refs[-1]
        k = pl.program_id(2)

        @pl.when(k == 0)
        def _():
            acc[...] = jnp.zeros_like(acc)

        for a_ref, (lo, hi) in zip(a_refs, a_ranges):
            def emit(a_ref=a_ref):
                acc[...] += _bdot(a_ref[...], flat(b_ref[...]), dims)
            if na == 1:
                emit()
            else:
                pl.when(jnp.logical_and(k >= lo, k < hi))(emit)

        @pl.when(k == nk - 1)
        def _():
            r = acc[...]
            if has_res:
                r = r + r_ref[...]
            o_ref[...] = r.astype(o_ref.dtype).reshape(o_ref.shape)

    o_spec = o_spec or pl.BlockSpec((tm, tn), lambda i, j, k: (i, j))
    r_spec = [pl.BlockSpec((tm, tn), lambda i, j, k: (i, j))] if has_res else []
    return pl.pallas_call(
        body, grid=(M // tm, N // tn, nk), in_specs=a_specs + [b_spec] + r_spec, out_specs=o_spec, out_shape=o_shape,
        scratch_shapes=[pltpu.VMEM((tm, tn), F32)],
        compiler_params=_cp(("parallel", "parallel", "arbitrary")), name=name)(*(a_segs + [b0] + ([res] if has_res else [])))


def _rmsnorm_fwd(x, g, *, name, tm=256):
    S, Dm = x.shape
    tm = min(tm, S)

    def body(x_ref, g_ref, o_ref):
        xv = x_ref[...]
        r = lax.rsqrt(jnp.mean(xv * xv, axis=-1, keepdims=True) + EPS)
        o_ref[...] = (xv * r * g_ref[...]).astype(BF16)

    return pl.pallas_call(
        body, grid=(S // tm,), in_specs=[pl.BlockSpec((tm, Dm), lambda i: (i, 0)), pl.BlockSpec((1, Dm), lambda i: (0, 0))],
        out_specs=pl.BlockSpec((tm, Dm), lambda i: (i, 0)), out_shape=jax.ShapeDtypeStruct((S, Dm), BF16),
        compiler_params=_cp(("parallel",)), name=name)(x, g.reshape(1, Dm))


def _rmsnorm_bwd(dh, x, g, dres, *, name, tm=256):
    S, Dm = x.shape
    tm = min(tm, S)
    want_dx = dres is not None

    def body(*refs):
        if want_dx:
            dh_ref, x_ref, g_ref, dr_ref, dx_ref, dg_ref = refs
        else:
            dh_ref, x_ref, g_ref, dg_ref = refs
        i = pl.program_id(0)
        xv = x_ref[...]
        dhv = dh_ref[...]
        r = lax.rsqrt(jnp.mean(xv * xv, axis=-1, keepdims=True) + EPS)
        y = xv * r
        part = jnp.sum(dhv * y, axis=0, keepdims=True)

        @pl.when(i == 0)
        def _():
            dg_ref[...] = jnp.zeros_like(dg_ref)

        dg_ref[...] += part
        if want_dx:
            dy = dhv * g_ref[...]
            dx_ref[...] = dr_ref[...] + r * (dy - y * jnp.mean(dy * y, axis=-1, keepdims=True))

    row = pl.BlockSpec((tm, Dm), lambda i: (i, 0))
    vec = pl.BlockSpec((1, Dm), lambda i: (0, 0))
    if want_dx:
        dx, dg = pl.pallas_call(
            body, grid=(S // tm,), in_specs=[row, row, vec, row], out_specs=(row, vec),
            out_shape=(jax.ShapeDtypeStruct((S, Dm), F32), jax.ShapeDtypeStruct((1, Dm), F32)),
            compiler_params=_cp(("arbitrary",)), name=name)(dh, x, g.reshape(1, Dm), dres)
        return dx, dg
    dg = pl.pallas_call(
        body, grid=(S // tm,), in_specs=[row, row, vec], out_specs=vec,
        out_shape=jax.ShapeDtypeStruct((1, Dm), F32), compiler_params=_cp(("arbitrary",)), name=name)(dh, x, g.reshape(1, Dm))
    return None, dg


GATE_TN = XA_WIDTH
GATE_MIX_TILES = MIX_WIDTH // GATE_TN


def _gate_cat_specs(tm):
    return [pl.BlockSpec((tm, GATE_TN), lambda i, j: (i, jnp.minimum(j, GATE_MIX_TILES - 1))),
            pl.BlockSpec((tm, GATE_TN), lambda i, j: (i, 0))]


def _gate_fwd(mix, xa, proj, z_off, *, name, tm=256):
    S = mix.shape[0]
    tm = min(tm, S)
    zb = z_off // GATE_TN

    def body(m_ref, x_ref, z_ref, y_ref):
        z = z_ref[...]
        c = jnp.where(pl.program_id(1) < GATE_MIX_TILES, m_ref[...], x_ref[...])
        y_ref[...] = (c * z * _sigmoid(z)).astype(BF16)

    blk = pl.BlockSpec((tm, GATE_TN), lambda i, j: (i, j))
    return pl.pallas_call(
        body, grid=(S // tm, INNER // GATE_TN),
        in_specs=_gate_cat_specs(tm) + [pl.BlockSpec((tm, GATE_TN), lambda i, j: (i, zb + j))],
        out_specs=blk, out_shape=jax.ShapeDtypeStruct((S, INNER), BF16),
        compiler_params=_cp(("parallel", "arbitrary")), name=name)(mix, xa, proj)


def _gate_bwd(dy, mix, xa, proj, z_off, *, name, tm=256):
    S = mix.shape[0]
    tm = min(tm, S)
    zb = z_off // GATE_TN

    def body(dy_ref, m_ref, x_ref, z_ref, dc_ref, dz_ref):
        z = z_ref[...]
        sg = _sigmoid(z)
        d = dy_ref[...]
        c = jnp.where(pl.program_id(1) < GATE_MIX_TILES, m_ref[...], x_ref[...])
        dc_ref[...] = d * z * sg
        dz_ref[...] = (d * c * sg * (1.0 + z * (1.0 - sg))).astype(BF16)

    blk = pl.BlockSpec((tm, GATE_TN), lambda i, j: (i, j))
    return pl.pallas_call(
        body, grid=(S // tm, INNER // GATE_TN),
        in_specs=[blk] + _gate_cat_specs(tm) + [pl.BlockSpec((tm, GATE_TN), lambda i, j: (i, zb + j))], out_specs=(blk, blk),
        out_shape=(jax.ShapeDtypeStruct((S, INNER), F32), jax.ShapeDtypeStruct((S, INNER), BF16)),
        compiler_params=_cp(("parallel", "arbitrary")), name=name)(dy, mix, xa, proj)


def _loss_head(x, target, *, name, tm=256):
    S, Dm = x.shape
    tm = min(tm, S)

    nt = S // tm

    def body(x_ref, t_ref, dx_ref, l_ref, acc):
        i = pl.program_id(0)
        e = x_ref[...] - t_ref[...]
        dx_ref[...] = e * (1.0 / Dm)

        @pl.when(i == 0)
        def _():
            acc[...] = jnp.zeros_like(acc)

        acc[...] += jnp.sum(e * e, axis=0, keepdims=True) * (0.5 / Dm)

        @pl.when(i == nt - 1)
        def _():
            l_ref[...] = jnp.sum(acc[...], axis=1, keepdims=True) + jnp.zeros((1, LANE), F32)

    row = pl.BlockSpec((tm, Dm), lambda i: (i, 0))
    return pl.pallas_call(
        body, grid=(nt,), in_specs=[row, row], out_specs=(row, pl.BlockSpec((1, LANE), lambda i: (0, 0))),
        out_shape=(jax.ShapeDtypeStruct((S, Dm), F32), jax.ShapeDtypeStruct((1, LANE), F32)),
        scratch_shapes=[pltpu.VMEM((1, Dm), F32)],
        compiler_params=_cp(("arbitrary",)), name=name)(x, target)


def _xa_norm(v, g):
    r = lax.rsqrt(jnp.mean(v * v, axis=-1, keepdims=True) + EPS)
    return v * r, r


def _xa_fwd(proj, xq_off, kv, gq, gk, *, name, tm=512):
    S = proj.shape[0]
    tm = min(tm, S)
    qb = xq_off // XA_DIM
    n_mem = kv.shape[0]
    scale = XA_DIM ** -0.5

    def body(q_ref, k_ref, v_ref, gq_ref, gk_ref, o_ref):
        qh, _ = _xa_norm(q_ref[...], None)
        kh, _ = _xa_norm(k_ref[...], None)
        qn = qh * gq_ref[...]
        kn = kh * gk_ref[...]
        s = _bdot(qn, kn, NT) * scale
        s = s - jnp.max(s, axis=-1, keepdims=True)
        p = jnp.exp(s)
        p = p / jnp.sum(p, axis=-1, keepdims=True)
        o_ref[...] = _bdot(p, v_ref[...], NN)

    vec = pl.BlockSpec((1, XA_DIM), lambda h, i: (0, 0))
    return pl.pallas_call(
        body, grid=(XA_HEADS, S // tm),
        in_specs=[pl.BlockSpec((tm, XA_DIM), lambda h, i: (i, qb + h)),
                  pl.BlockSpec((n_mem, XA_DIM), lambda h, i: (0, h)),
                  pl.BlockSpec((n_mem, XA_DIM), lambda h, i: (0, XA_HEADS + h)), vec, vec],
        out_specs=pl.BlockSpec((tm, XA_DIM), lambda h, i: (i, h)),
        out_shape=jax.ShapeDtypeStruct((S, XA_WIDTH), F32),
        compiler_params=_cp(("parallel", "parallel")), name=name)(proj, kv, kv, gq.reshape(1, XA_DIM), gk.reshape(1, XA_DIM))


def _xa_bwd(dcat, proj, xq_off, kv, gq, gk, *, name, tm=512):
    S = proj.shape[0]
    tm = min(tm, S)
    nt = S // tm
    qb = xq_off // XA_DIM
    db = MIX_WIDTH // XA_DIM
    n_mem = kv.shape[0]
    scale = XA_DIM ** -0.5

    def body(d_ref, q_ref, k_ref, v_ref, gq_ref, gk_ref, dq_ref, dk_ref, dv_ref, dgq_ref, dgk_ref, dkn_acc):
        h = pl.program_id(0)
        i = pl.program_id(1)
        q = q_ref[...]
        k = k_ref[...]
        qh, rq = _xa_norm(q, None)
        kh, rk = _xa_norm(k, None)
        gqv = gq_ref[...]
        gkv = gk_ref[...]
        qn = qh * gqv
        kn = kh * gkv
        s = _bdot(qn, kn, NT) * scale
        s = s - jnp.max(s, axis=-1, keepdims=True)
        p = jnp.exp(s)
        p = p / jnp.sum(p, axis=-1, keepdims=True)
        d = d_ref[...]
        dp = _bdot(d, v_ref[...], NT)
        ds = p * (dp - jnp.sum(dp * p, axis=-1, keepdims=True)) * scale
        dqn = _bdot(ds, kn, NN)

        @pl.when(i == 0)
        def _():
            dkn_acc[...] = jnp.zeros_like(dkn_acc)
            dv_ref[...] = jnp.zeros_like(dv_ref)

        @pl.when(jnp.logical_and(i == 0, h == 0))
        def _():
            dgq_ref[...] = jnp.zeros_like(dgq_ref)
            dgk_ref[...] = jnp.zeros_like(dgk_ref)

        dkn_acc[...] += _bdot(ds, qn, TN)
        dv_ref[...] += _bdot(p, d, TN)
        dgq_ref[...] += jnp.sum(dqn * qh, axis=0, keepdims=True)
        dy = dqn * gqv
        dq_ref[...] = (rq * (dy - qh * jnp.mean(dy * qh, axis=-1, keepdims=True))).astype(BF16)

        @pl.when(i == nt - 1)
        def _():
            dkn = dkn_acc[...]
            dgk_ref[...] += jnp.sum(dkn * kh, axis=0, keepdims=True)
            dyk = dkn * gkv
            dk_ref[...] = rk * (dyk - kh * jnp.mean(dyk * kh, axis=-1, keepdims=True))

    vec = pl.BlockSpec((1, XA_DIM), lambda h, i: (0, 0))
    kblk = pl.BlockSpec((n_mem, XA_DIM), lambda h, i: (0, h))
    vblk = pl.BlockSpec((n_mem, XA_DIM), lambda h, i: (0, XA_HEADS + h))
    dq, dk, dv, dgq, dgk = pl.pallas_call(
        body, grid=(XA_HEADS, nt),
        in_specs=[pl.BlockSpec((tm, XA_DIM), lambda h, i: (i, db + h)),
                  pl.BlockSpec((tm, XA_DIM), lambda h, i: (i, qb + h)), kblk, vblk, vec, vec],
        out_specs=(pl.BlockSpec((tm, XA_DIM), lambda h, i: (i, h)), kblk, kblk, vec, vec),
        out_shape=(jax.ShapeDtypeStruct((S, XA_WIDTH), BF16), jax.ShapeDtypeStruct((n_mem, XA_WIDTH), F32),
                   jax.ShapeDtypeStruct((n_mem, XA_WIDTH), F32), jax.ShapeDtypeStruct((1, XA_DIM), F32),
                   jax.ShapeDtypeStruct((1, XA_DIM), F32)),
        scratch_shapes=[pltpu.VMEM((n_mem, XA_DIM), F32)],
        compiler_params=_cp(("arbitrary", "arbitrary")), name=name)(
            dcat, proj, kv, kv, gq.reshape(1, XA_DIM), gk.reshape(1, XA_DIM))
    return dq, jnp.concatenate([dk, dv], axis=1), dgq, dgk


SB_TQ = 256
SB_TK = 256
SB_HEADS = 24


SB_PAIR = 2
SB_PW = SB_PAIR * HEAD_DIM


def _hdot(a, b, dims, dot=None):
    dot = dot or _bdot
    n = a.shape[0] if a.ndim == 3 else b.shape[0]
    return jnp.stack([dot(a[i] if a.ndim == 3 else a, b[i] if b.ndim == 3 else b, dims) for i in range(n)])


def _sb_tile(qi, kj, t0, s0, masked):
    z = _hdot(qi, kj, NT)
    sp = _softplus(z)
    ls = z - sp
    if not masked:
        return -sp, ls, None
    mask = (s0 + _iota2(z.shape[1:], 1)) < (t0 + _iota2(z.shape[1:], 0))
    return jnp.where(mask, -sp, 0.0), ls, mask


def _dot2(x, tri):
    hi = x.astype(BF16)
    lo = (x - hi.astype(F32)).astype(BF16)
    plain = lambda u, v, dims: lax.dot_general(u, v, (dims, ((), ())), preferred_element_type=F32)
    return _hdot(hi, tri, NN, plain) + _hdot(lo, tri, NN, plain)


def _sb_heads(ref, rows=slice(None)):
    return jnp.stack([ref[rows, hh * HEAD_DIM:(hh + 1) * HEAD_DIM] for hh in range(SB_PAIR)])


def _sb_fwd(proj, gq, gk, *, name):
    S = proj.shape[0]
    tq, tk = min(SB_TQ, S), min(SB_TK, S)
    nq = S // tq
    scale = HEAD_DIM ** -0.5

    def body(q_ref, k_ref, v_ref, gq_ref, gk_ref, o_ref, tot_ref, qn_s, kn_s, v_s):
        q = _sb_heads(q_ref)
        k = _sb_heads(k_ref)
        qn_s[...] = (q * lax.rsqrt(jnp.mean(q * q, axis=-1, keepdims=True) + EPS) * (gq_ref[...] * scale)).astype(BF16)
        kn_s[...] = (k * lax.rsqrt(jnp.mean(k * k, axis=-1, keepdims=True) + EPS) * gk_ref[...]).astype(BF16)
        v_s[...] = _sb_heads(v_ref).astype(BF16)
        after = (_iota2((tk, tk), 0) > _iota2((tk, tk), 1)).astype(BF16)

        def qblock(i, _):
            rows = pl.ds(pl.multiple_of(i * tq, tq), tq)
            qi = qn_s[:, rows, :]
            jd = (i * tq) // tk

            def tile(j, acc, run, masked):
                cols = pl.ds(pl.multiple_of(j * tk, tk), tk)
                lr, ls, mask = _sb_tile(qi, kn_s[:, cols, :], i * tq, j * tk, masked)
                later = _dot2(lr, after) + run
                a = jnp.exp(ls + later)
                if masked:
                    a = jnp.where(mask, a, 0.0)
                acc = acc + _hdot(a, v_s[:, cols, :], NN)
                return acc, run + jnp.sum(lr, axis=-1, keepdims=True)

            acc, run = tile(jd, jnp.zeros((SB_PAIR, tq, HEAD_DIM), F32), jnp.zeros((SB_PAIR, tq, 1), F32), True)
            acc, run = lax.fori_loop(0, jd, lambda jj, c: tile(jd - 1 - jj, c[0], c[1], False), (acc, run))
            tot = run + jnp.zeros((SB_PAIR, tq, HEAD_DIM), F32)
            for hh in range(SB_PAIR):
                o_ref[rows, hh * HEAD_DIM:(hh + 1) * HEAD_DIM] = acc[hh]
                tot_ref[rows, hh * HEAD_DIM:(hh + 1) * HEAD_DIM] = tot[hh]
            return 0

        lax.fori_loop(0, nq, qblock, 0)

    npair = SB_HEADS // SB_PAIR
    vec = pl.BlockSpec((1, HEAD_DIM), lambda h: (0, 0))
    hb = lambda off: pl.BlockSpec((S, SB_PW), lambda h: (0, off + h), pipeline_mode=pl.Buffered(1))
    return pl.pallas_call(
        body, grid=(npair,), in_specs=[hb(0), hb(npair), hb(2 * npair), vec, vec],
        out_specs=(hb(0), hb(0)), out_shape=(jax.ShapeDtypeStruct((S, MIX_WIDTH), F32),) * 2,
        scratch_shapes=[pltpu.VMEM((SB_PAIR, S, HEAD_DIM), BF16)] * 3,
        compiler_params=_cp(("parallel",)), name=name)(proj, proj, proj, gq.reshape(1, HEAD_DIM), gk.reshape(1, HEAD_DIM))


def _sb_bwd(dmix, tot, proj, gq, gk, *, name):
    S = proj.shape[0]
    tq, tk = min(SB_TQ, S), min(SB_TK, S)
    nq = S // tq
    scale = HEAD_DIM ** -0.5

    def body(do_ref, o_ref, q_ref, k_ref, v_ref, gq_ref, gk_ref, dq_ref, dk_ref, dv_ref, dgq_ref, dgk_ref,
             qn_s, kn_s, v_s, dkn_s, dqn_s, dv_s):
        h = pl.program_id(0)
        q = _sb_heads(q_ref)
        k = _sb_heads(k_ref)
        rq = lax.rsqrt(jnp.mean(q * q, axis=-1, keepdims=True) + EPS)
        rk = lax.rsqrt(jnp.mean(k * k, axis=-1, keepdims=True) + EPS)
        gqv = gq_ref[...]
        gkv = gk_ref[...]
        qn_s[...] = (q * rq * (gqv * scale)).astype(BF16)
        kn_s[...] = (k * rk * gkv).astype(BF16)
        v_s[...] = _sb_heads(v_ref).astype(BF16)
        dkn_s[...] = jnp.zeros_like(dkn_s)
        dv_s[...] = jnp.zeros_like(dv_s)
        r_i = _iota2((tk, tk), 0)
        c_i = _iota2((tk, tk), 1)
        upto = (r_i <= c_i).astype(BF16)
        before = (r_i < c_i).astype(BF16)

        def qblock(i, _):
            rows = pl.ds(pl.multiple_of(i * tq, tq), tq)
            qi = qn_s[:, rows, :]
            doi = _sb_heads(do_ref, rows).astype(BF16)
            tot_i = jnp.max(_sb_heads(o_ref, rows), axis=-1, keepdims=True)
            jd = (i * tq) // tk

            def tile(j, dqn, run, run_b, masked):
                cols = pl.ds(pl.multiple_of(j * tk, tk), tk)
                kj = kn_s[:, cols, :]
                lr, ls, mask = _sb_tile(qi, kj, i * tq, j * tk, masked)
                later = tot_i - (_dot2(lr, upto) + run)
                a = jnp.exp(ls + later)
                if masked:
                    a = jnp.where(mask, a, 0.0)
                b = _hdot(doi, v_s[:, cols, :], NT) * a
                cum = _dot2(b, before) + run_b
                beta = jnp.exp(ls)
                dz = b * (1.0 - beta) - cum * beta
                if masked:
                    dz = jnp.where(mask, dz, 0.0)
                dzb = dz.astype(BF16)
                dv_s[:, cols, :] += _hdot(a, doi, TN)
                dkn_s[:, cols, :] += _hdot(dzb, qi, TN)
                dqn = dqn + _hdot(dzb, kj, NN)
                return dqn, run + jnp.sum(lr, axis=-1, keepdims=True), run_b + jnp.sum(b, axis=-1, keepdims=True)

            zero1 = jnp.zeros((SB_PAIR, tq, 1), F32)
            carry = lax.fori_loop(0, jd, lambda j, c: tile(j, c[0], c[1], c[2], False),
                                  (jnp.zeros((SB_PAIR, tq, HEAD_DIM), F32), zero1, zero1))
            dqn, _, _ = tile(jd, carry[0], carry[1], carry[2], True)
            dqn_s[:, rows, :] = dqn * scale
            return 0

        lax.fori_loop(0, nq, qblock, 0)

        @pl.when(h == 0)
        def _():
            dgq_ref[...] = jnp.zeros_like(dgq_ref)
            dgk_ref[...] = jnp.zeros_like(dgk_ref)

        heads_sum = lambda z: jnp.sum(jnp.sum(z, axis=1, keepdims=True), axis=0)
        dqn = dqn_s[...]
        qh = q * rq
        dgq_ref[...] += heads_sum(dqn * qh)
        dy = dqn * gqv
        dq = (rq * (dy - qh * jnp.mean(dy * qh, axis=-1, keepdims=True))).astype(BF16)
        dkn = dkn_s[...]
        kh = k * rk
        dgk_ref[...] += heads_sum(dkn * kh)
        dyk = dkn * gkv
        dk = (rk * (dyk - kh * jnp.mean(dyk * kh, axis=-1, keepdims=True))).astype(BF16)
        dv = dv_s[...].astype(BF16)
        for hh in range(SB_PAIR):
            lanes = slice(hh * HEAD_DIM, (hh + 1) * HEAD_DIM)
            dq_ref[:, lanes] = dq[hh]
            dk_ref[:, lanes] = dk[hh]
            dv_ref[:, lanes] = dv[hh]

    npair = SB_HEADS // SB_PAIR
    vec = pl.BlockSpec((1, HEAD_DIM), lambda h: (0, 0))
    hb = lambda off: pl.BlockSpec((S, SB_PW), lambda h: (0, off + h), pipeline_mode=pl.Buffered(1))
    dq, dk, dv, dgq, dgk = pl.pallas_call(
        body, grid=(npair,),
        in_specs=[hb(0), hb(0), hb(0), hb(npair), hb(2 * npair), vec, vec],
        out_specs=(hb(0), hb(0), hb(0), vec, vec),
        out_shape=(jax.ShapeDtypeStruct((S, MIX_WIDTH), BF16),) * 3 + (jax.ShapeDtypeStruct((1, HEAD_DIM), F32),) * 2,
        scratch_shapes=[pltpu.VMEM((SB_PAIR, S, HEAD_DIM), BF16)] * 3 + [pltpu.VMEM((SB_PAIR, S, HEAD_DIM), F32)] * 3,
        compiler_params=_cp(("arbitrary",)), name=name)(
            dmix, tot, proj, proj, proj, gq.reshape(1, HEAD_DIM), gk.reshape(1, HEAD_DIM))
    return [dq, dk, dv], dgq, dgk


def _shift_down(x, k):
    if k == 0:
        return x
    r = pltpu.roll(x, k, 0)
    return jnp.where(_iota2(x.shape, 0) >= k, r, 0.0)


def _shift_up(x, k):
    if k == 0:
        return x
    n = x.shape[0]
    r = pltpu.roll(x, n - k, 0)
    return jnp.where(_iota2(x.shape, 0) < n - k, r, 0.0)


def _conv(x, w):
    c = w[DN_CONV - 1] * x
    for k in range(1, DN_CONV):
        c = c + w[DN_CONV - 1 - k] * _shift_down(x, k)
    return c


def _dn_pre_fwd(proj, conv_w, col0, ncols, *, l2, scale, name):
    S = proj.shape[0]
    cb = col0 // HEAD_DIM

    def body(x_ref, w_ref, o_ref):
        c = _conv(x_ref[...], [w_ref[k:k + 1, :] for k in range(DN_CONV)])
        a = c * _sigmoid(c)
        if l2:
            a = a * (lax.rsqrt(jnp.sum(a * a, axis=-1, keepdims=True) + EPS) * scale)
        o_ref[...] = a

    return pl.pallas_call(
        body, grid=(ncols // HEAD_DIM,),
        in_specs=[pl.BlockSpec((S, HEAD_DIM), lambda j: (0, cb + j)), pl.BlockSpec((DN_CONV, HEAD_DIM), lambda j: (0, cb + j))],
        out_specs=pl.BlockSpec((S, HEAD_DIM), lambda j: (0, j)), out_shape=jax.ShapeDtypeStruct((S, ncols), F32),
        compiler_params=_cp(("parallel",)), name=name)(proj, conv_w)


def _dn_pre_bwd(dout, proj, conv_w, col0, ncols, *, l2, scale, name):
    S = proj.shape[0]
    cb = col0 // HEAD_DIM
    dw_in = HEAD_DIM

    def body(d_ref, x_ref, w_ref, dx_ref, dw_ref):
        x = x_ref[...]
        w = [w_ref[k:k + 1, :] for k in range(DN_CONV)]
        c = _conv(x, w)
        sg = _sigmoid(c)
        a = c * sg
        d = d_ref[...]
        if l2:
            r = lax.rsqrt(jnp.sum(a * a, axis=-1, keepdims=True) + EPS)
            y = a * r
            d = d * scale
            d = r * (d - y * jnp.sum(d * y, axis=-1, keepdims=True))
        dc = d * sg * (1.0 + c * (1.0 - sg))
        dx = w[DN_CONV - 1] * dc
        for k in range(1, DN_CONV):
            dx = dx + w[DN_CONV - 1 - k] * _shift_up(dc, k)
        dx_ref[...] = dx.astype(BF16)
        for k in range(DN_CONV):
            dw_ref[3 - k:4 - k, :] = jnp.sum(dc * _shift_down(x, k), axis=0, keepdims=True)

    return pl.pallas_call(
        body, grid=(ncols // HEAD_DIM,),
        in_specs=[pl.BlockSpec((S, dw_in), lambda j: (0, j)), pl.BlockSpec((S, HEAD_DIM), lambda j: (0, cb + j)),
                  pl.BlockSpec((DN_CONV, HEAD_DIM), lambda j: (0, cb + j))],
        out_specs=(pl.BlockSpec((S, HEAD_DIM), lambda j: (0, j)), pl.BlockSpec((DN_CONV, HEAD_DIM), lambda j: (0, j))),
        out_shape=(jax.ShapeDtypeStruct((S, ncols), BF16), jax.ShapeDtypeStruct((DN_CONV, ncols), F32)),
        compiler_params=_cp(("parallel",)), name=name)(dout, proj, conv_w)


def _dn_ab_fwd(proj, a_log, dt_bias, *, name, tm=512):
    S = proj.shape[0]
    tm = min(tm, S)
    ab = P0_AB // LANE

    def body(a_ref, b_ref, al_ref, dt_ref, g_ref, be_ref):
        g_ref[...] = -jnp.exp(al_ref[...]) * _softplus(a_ref[...] + dt_ref[...])
        be_ref[...] = _sigmoid(b_ref[...])

    vec = pl.BlockSpec((1, LANE), lambda i: (0, 0))
    out = pl.BlockSpec((tm, LANE), lambda i: (i, 0))
    return pl.pallas_call(
        body, grid=(S // tm,),
        in_specs=[pl.BlockSpec((tm, LANE), lambda i: (i, ab)), pl.BlockSpec((tm, LANE), lambda i: (i, ab + 1)), vec, vec],
        out_specs=(out, out), out_shape=(jax.ShapeDtypeStruct((S, LANE), F32),) * 2,
        compiler_params=_cp(("parallel",)), name=name)(proj, proj, a_log, dt_bias)


def _dn_ab_bwd(dg, dbeta, proj, a_log, dt_bias, *, name, tm=512):
    S = proj.shape[0]
    tm = min(tm, S)
    ab = P0_AB // LANE

    def body(dg_ref, db_ref, a_ref, b_ref, al_ref, dt_ref, dab_ref, dal_ref, ddt_ref):
        i = pl.program_id(0)
        ea = jnp.exp(al_ref[...])
        u = a_ref[...] + dt_ref[...]
        dgv = dg_ref[...]
        da = dgv * (-ea) * _sigmoid(u)
        be = _sigmoid(b_ref[...])
        dab_ref[:, 0:LANE] = da.astype(BF16)
        dab_ref[:, LANE:2 * LANE] = (db_ref[...] * be * (1.0 - be)).astype(BF16)
        dab_ref[:, 2 * LANE:] = jnp.zeros((tm, 2 * LANE), BF16)

        @pl.when(i == 0)
        def _():
            dal_ref[...] = jnp.zeros_like(dal_ref)
            ddt_ref[...] = jnp.zeros_like(ddt_ref)

        dal_ref[...] += jnp.sum(dgv * (-ea) * _softplus(u), axis=0, keepdims=True)
        ddt_ref[...] += jnp.sum(da, axis=0, keepdims=True)

    vec = pl.BlockSpec((1, LANE), lambda i: (0, 0))
    row = pl.BlockSpec((tm, LANE), lambda i: (i, 0))
    return pl.pallas_call(
        body, grid=(S // tm,),
        in_specs=[row, row, pl.BlockSpec((tm, LANE), lambda i: (i, ab)), pl.BlockSpec((tm, LANE), lambda i: (i, ab + 1)), vec, vec],
        out_specs=(pl.BlockSpec((tm, 4 * LANE), lambda i: (i, 0)), vec, vec),
        out_shape=(jax.ShapeDtypeStruct((S, 4 * LANE), BF16), jax.ShapeDtypeStruct((1, LANE), F32),
                   jax.ShapeDtypeStruct((1, LANE), F32)),
        compiler_params=_cp(("arbitrary",)), name=name)(dg, dbeta, proj, proj, a_log, dt_bias)


def _dot3(a, b):
    ah = a.astype(BF16)
    al = (a - ah.astype(F32)).astype(BF16)
    bh = b.astype(BF16)
    bl = (b - bh.astype(F32)).astype(BF16)
    d = lambda u, v: lax.dot_general(u, v, (NN, ((), ())), preferred_element_type=F32)
    return d(ah, bh) + (d(ah, bl) + d(al, bh))


DN_PAIR = 4
DN_QK = DN_PAIR // 2


def _dn_qk_heads(ref, rows):
    return jnp.stack([ref[rows, (hh // 2) * HEAD_DIM:(hh // 2 + 1) * HEAD_DIM] for hh in range(DN_PAIR)])


def _dn_big(shape, imap):
    return pl.BlockSpec(shape, imap, pipeline_mode=pl.Buffered(1))


_pdot = _hdot


def _tri_inverse(a):
    eye = (_iota2((CH, CH), 0) == _iota2((CH, CH), 1)).astype(F32)
    d3 = lambda u, v: jnp.stack([_dot3(u[i], v[i]) for i in range(DN_PAIR)])
    t = eye - a
    x = d3(a, a)
    n = 2
    while True:
        t = t + d3(t, x)
        n *= 2
        if n >= CH:
            break
        x = d3(x, x)
    return t


def _pick_col(m, n):
    return jnp.sum(jnp.where(_iota2(m.shape, 2) == n, m, 0.0), axis=2, keepdims=True)


def _dn_chunk_common(kk, qk, gc_c, gc_r, be_c):
    r_i = _iota2((CH, CH), 0)
    c_i = _iota2((CH, CH), 1)
    incl = r_i >= c_i
    strict = r_i > c_i
    dec = jnp.exp(jnp.where(incl, gc_c - gc_r, -1e30))
    e = jnp.exp(gc_c)
    gl = jnp.sum(jnp.where(_iota2((1, CH), 1) == CH - 1, gc_r, 0.0), axis=-1, keepdims=True)
    kds = jnp.exp(gl - gc_c)
    cd = jnp.exp(gl)
    a = jnp.where(strict, be_c * kk * dec, 0.0)
    p = qk * dec
    return dict(incl=incl, strict=strict, dec=dec, e=e, kds=kds, cd=cd, kk=kk, a=a, qk=qk, p=p)


def _dn_decay_tables(g_ref, b_ref, gcr, gcc, bcc):
    r_i = _iota2((CH, CH), 0)
    c_i = _iota2((CH, CH), 1)
    lc = (r_i >= c_i).astype(F32)
    eye = (r_i == c_i).astype(F32)
    for hh in range(DN_PAIR):
        g_rows_v = g_ref[hh]
        gcr[hh] = _fdot(g_rows_v, lc, NT)
        gcc[hh] = _fdot(lc, g_rows_v, NT)
        bcc[hh] = _fdot(eye, b_ref[hh], NT)
    return lc


def _dn_core_fwd(qn, kn, vc, g_rows, b_rows, out_g, *, name):
    S = qn.shape[0]
    nc = S // CH

    def body(q_ref, k_ref, v_ref, g_ref, b_ref, og_ref, o_ref, st_ref, t_ref, gcr, gcc, bcc):
        _dn_decay_tables(g_ref, b_ref, gcr, gcc, bcc)
        ogv = og_ref[...]

        def chunk(n, states):
            rows = pl.ds(pl.multiple_of(n * CH, CH), CH)
            q = _dn_qk_heads(q_ref, rows)
            k = _dn_qk_heads(k_ref, rows)
            kk = _pdot(k, k, NT)
            qk = _pdot(q, k, NT)
            v = jnp.stack([v_ref[rows, hh * HEAD_DIM:(hh + 1) * HEAD_DIM] for hh in range(DN_PAIR)])
            gc_c = _pick_col(gcc[...], n)
            be_c = _pick_col(bcc[...], n)
            gc_r = gcr[:, pl.ds(n, 1), :]
            c = _dn_chunk_common(kk, qk, gc_c, gc_r, be_c)
            t = _tri_inverse(c["a"])
            u0 = _pdot(t, be_c * v, NN)
            w = _pdot(t, (be_c * c["e"]) * k, NN)
            u = u0 - _pdot(w, states, NN)
            o = _pdot(c["e"] * q, states, NN) + _pdot(c["p"], u, NN)
            on = o * lax.rsqrt(jnp.mean(o * o, axis=-1, keepdims=True) + EPS) * ogv
            for hh in range(DN_PAIR):
                st_ref[hh, n] = states[hh]
                t_ref[hh, n] = t[hh]
                o_ref[rows, hh * HEAD_DIM:(hh + 1) * HEAD_DIM] = on[hh]
            return c["cd"] * states + _pdot(c["kds"] * k, u, TN)

        lax.fori_loop(0, nc, chunk, jnp.zeros((DN_PAIR, HEAD_DIM, HEAD_DIM), F32))

    qk_spec = _dn_big((S, DN_QK * HEAD_DIM), lambda h: (0, h))
    v_spec = _dn_big((S, DN_PAIR * HEAD_DIM), lambda h: (0, h))
    rows_spec = pl.BlockSpec((DN_PAIR, LANE, CH), lambda h: (h, 0, 0))
    return pl.pallas_call(
        body, grid=(DN_V_HEADS // DN_PAIR,),
        in_specs=[qk_spec, qk_spec, v_spec, rows_spec, rows_spec, pl.BlockSpec((1, HEAD_DIM), lambda h: (0, 0))],
        out_specs=(v_spec, _dn_big((DN_PAIR, nc, HEAD_DIM, HEAD_DIM), lambda h: (h, 0, 0, 0)),
                   _dn_big((DN_PAIR, nc, CH, CH), lambda h: (h, 0, 0, 0))),
        out_shape=(jax.ShapeDtypeStruct((S, MIX_WIDTH), F32), jax.ShapeDtypeStruct((DN_V_HEADS, nc, HEAD_DIM, HEAD_DIM), F32),
                   jax.ShapeDtypeStruct((DN_V_HEADS, nc, CH, CH), F32)),
        scratch_shapes=[pltpu.VMEM((DN_PAIR, LANE, CH), F32), pltpu.VMEM((DN_PAIR, CH, LANE), F32),
                        pltpu.VMEM((DN_PAIR, CH, LANE), F32)],
        compiler_params=_cp(("parallel",)), name=name)(qn, kn, vc, g_rows, b_rows, out_g.reshape(1, HEAD_DIM))


def _dn_chunk_bwd(q, k, v, kk, qk, state, t, gc_c, gc_r, be_c, don, ogv, ds_next):
    ones = jnp.ones((CH, LANE), F32)
    last_row = _iota2((CH, 1), 0) == CH - 1
    rowsum = lambda z: jnp.sum(z, axis=-1, keepdims=True)
    colsum = lambda z: jnp.sum(z, axis=-2, keepdims=True)
    c = _dn_chunk_common(kk, qk, gc_c, gc_r, be_c)
    e, kds, cd, dec, a, p = c["e"], c["kds"], c["cd"], c["dec"], c["a"], c["p"]
    vb = be_c * v
    kbe = (be_c * e) * k
    u0 = _pdot(t, vb, NN)
    w = _pdot(t, kbe, NN)
    u = u0 - _pdot(w, state, NN)
    qd = e * q
    kd = kds * k
    o = _pdot(qd, state, NN) + _pdot(p, u, NN)
    r = lax.rsqrt(jnp.mean(o * o, axis=-1, keepdims=True) + EPS)
    y = o * r
    dog = colsum(don * y)
    dy = don * ogv
    d_o = r * (dy - y * jnp.mean(dy * y, axis=-1, keepdims=True))
    du = _pdot(p, d_o, TN) + _pdot(kd, ds_next, NN)
    dqd = _pdot(d_o, state, NT)
    dstate = _pdot(qd, d_o, TN) + cd * ds_next - _pdot(w, du, TN)
    dcd = colsum(rowsum(ds_next * state))
    dkd = _pdot(u, ds_next, NT)
    dw = -_pdot(du, state, NT)
    dvb = _pdot(t, du, TN)
    dkbe = _pdot(t, dw, TN)
    da = -jnp.where(c["strict"], _pdot(dvb, u0, NT) + _pdot(dkbe, w, NT), 0.0)
    dp = jnp.where(c["incl"], _pdot(d_o, u, NT), 0.0)
    gmat = da * a + dp * p
    dad = da * dec
    x = be_c * dad
    dpd = dp * dec
    dk = _pdot(x, k, NN) + _pdot(x, k, TN) + _pdot(dpd, q, TN)
    dq = _pdot(dpd, k, NN) + e * dqd
    dbe = rowsum(dad * c["kk"])
    dgc = rowsum(gmat) + rowsum(dqd * q) * e
    rk = rowsum(dkd * k) * kds
    dk = dk + kds * dkd
    dgc = dgc - rk
    dgl = colsum(rk) + dcd * cd
    sk = rowsum(dkbe * k)
    dk = dk + (be_c * e) * dkbe
    dbe = dbe + sk * e + rowsum(dvb * v)
    dgc = dgc + sk * be_c * e
    dgc = dgc + jnp.where(last_row, dgl, 0.0)
    dgc = dgc - _pdot(gmat, ones, TN, dot=_fdot)
    return dq, dk, be_c * dvb, dgc, dbe, dog, dstate


def _dn_core_bwd(dmix, qn, kn, vc, g_rows, b_rows, out_g, states, tinv, *, name):
    S = qn.shape[0]
    nc = S // CH

    def body(do_ref, q_ref, k_ref, v_ref, g_ref, b_ref, og_ref, st_ref, t_ref,
             dq_ref, dk_ref, dv_ref, dg_ref, db_ref, dog_ref, gcr, gcc, bcc, dgc_acc):
        h = pl.program_id(0)
        lc = _dn_decay_tables(g_ref, b_ref, gcr, gcc, bcc)
        ogv = og_ref[...]
        dgc_acc[...] = jnp.zeros_like(dgc_acc)
        db_ref[...] = jnp.zeros_like(db_ref)
        lane_n = _iota2((CH, LANE), 1)

        @pl.when(h == 0)
        def _():
            dog_ref[...] = jnp.zeros_like(dog_ref)

        def chunk(m, carry):
            ds_nexts, dog = carry
            n = nc - 1 - m
            rows = pl.ds(pl.multiple_of(n * CH, CH), CH)
            q = _dn_qk_heads(q_ref, rows)
            k = _dn_qk_heads(k_ref, rows)
            kk = _pdot(k, k, NT)
            qk = _pdot(q, k, NT)
            heads = lambda ref: jnp.stack([ref[rows, hh * HEAD_DIM:(hh + 1) * HEAD_DIM] for hh in range(DN_PAIR)])
            state = jnp.stack([st_ref[hh, n] for hh in range(DN_PAIR)])
            t = jnp.stack([t_ref[hh, n] for hh in range(DN_PAIR)])
            dq, dk, dv, dgc, dbe, dog_h, dstate = _dn_chunk_bwd(
                q, k, heads(v_ref), kk, qk, state, t, _pick_col(gcc[...], n), gcr[:, pl.ds(n, 1), :],
                _pick_col(bcc[...], n), heads(do_ref), ogv, ds_nexts)
            for hh in range(DN_PAIR):
                dv_ref[rows, hh * HEAD_DIM:(hh + 1) * HEAD_DIM] = dv[hh]
            dgc_acc[...] = jnp.where(lane_n == n, dgc, dgc_acc[...])
            db_ref[...] = jnp.where(lane_n == n, dbe, db_ref[...])
            for i in range(DN_QK):
                dq_ref[rows, i * HEAD_DIM:(i + 1) * HEAD_DIM] = dq[2 * i] + dq[2 * i + 1]
                dk_ref[rows, i * HEAD_DIM:(i + 1) * HEAD_DIM] = dk[2 * i] + dk[2 * i + 1]
            return dstate, dog + jnp.sum(dog_h, axis=0)

        _, dog = lax.fori_loop(0, nc, chunk, (jnp.zeros((DN_PAIR, HEAD_DIM, HEAD_DIM), F32), jnp.zeros((1, HEAD_DIM), F32)))
        dog_ref[...] += dog
        for hh in range(DN_PAIR):
            dg_ref[hh] = _fdot(lc, dgc_acc[hh], TN)

    qk_spec = _dn_big((S, DN_QK * HEAD_DIM), lambda h: (0, h))
    v_spec = _dn_big((S, DN_PAIR * HEAD_DIM), lambda h: (0, h))
    rows_spec = pl.BlockSpec((DN_PAIR, LANE, CH), lambda h: (h, 0, 0))
    cols_spec = pl.BlockSpec((DN_PAIR, CH, LANE), lambda h: (h, 0, 0))
    vec = pl.BlockSpec((1, HEAD_DIM), lambda h: (0, 0))
    qk_out = jax.ShapeDtypeStruct((S, DN_QK_WIDTH), F32)
    return pl.pallas_call(
        body, grid=(DN_V_HEADS // DN_PAIR,),
        in_specs=[v_spec, qk_spec, qk_spec, v_spec, rows_spec, rows_spec, vec,
                  _dn_big((DN_PAIR, nc, HEAD_DIM, HEAD_DIM), lambda h: (h, 0, 0, 0)),
                  _dn_big((DN_PAIR, nc, CH, CH), lambda h: (h, 0, 0, 0))],
        out_specs=(qk_spec, qk_spec, v_spec, cols_spec, cols_spec, vec),
        out_shape=(qk_out, qk_out, jax.ShapeDtypeStruct((S, MIX_WIDTH), F32), jax.ShapeDtypeStruct((DN_V_HEADS, CH, LANE), F32),
                   jax.ShapeDtypeStruct((DN_V_HEADS, CH, LANE), F32), jax.ShapeDtypeStruct((1, HEAD_DIM), F32)),
        scratch_shapes=[pltpu.VMEM((DN_PAIR, LANE, CH), F32), pltpu.VMEM((DN_PAIR, CH, LANE), F32),
                        pltpu.VMEM((DN_PAIR, CH, LANE), F32), pltpu.VMEM((DN_PAIR, CH, LANE), F32)],
        compiler_params=_cp(("arbitrary",)), name=name)(
            dmix, qn, kn, vc, g_rows, b_rows, out_g.reshape(1, HEAD_DIM), states, tinv)


def _rows_form(x, nc):
    t = x[:, :DN_V_HEADS].T.reshape(DN_V_HEADS, nc, CH)
    return jnp.pad(t, ((0, 0), (0, LANE - nc), (0, 0)))


def _cols_to_nat(x, nc):
    t = jnp.transpose(x[:, :, :nc], (2, 1, 0)).reshape(nc * CH, DN_V_HEADS)
    return jnp.pad(t, ((0, 0), (0, LANE - DN_V_HEADS)))


_C_QKV = 2 * DN_QK_WIDTH + MIX_WIDTH


def _true_pieces(lo, hi):
    out = []
    while lo < hi:
        s = lo // P0_SHARD
        end = min(hi, (s + 1) * P0_SHARD)
        out.append((s, lo - s * P0_SHARD, end - s * P0_SHARD))
        lo = end
    return out


def _padded_pieces(lo, hi):
    a0, b0, x0 = _C_QKV, _C_QKV + DN_V_HEADS, _C_QKV + 2 * DN_V_HEADS
    out = []
    for t0, t1, shift in ((0, a0, 0), (a0, b0, P0_AB - a0), (b0, x0, P0_AB + LANE - b0), (x0, DN_PROJ, a0 - x0)):
        s, e = max(lo, t0), min(hi, t1)
        if s < e:
            out.append((s + shift, e + shift))
    return out


def _pad_lane(v):
    v = v.reshape(1, -1)
    return jnp.pad(v, ((0, 0), (0, LANE - v.shape[1])))


SLOT1 = SB_PROJ // N_CHIPS
MM_TN = 512


def _local_step(x, mem, target, norm_g, mem_norm_g, xa_q_g, xa_k_g, w_in0, conv_w, a_log, dt_bias, out_g, sb_q_g, sb_k_g,
                late_weights, early_grads, grads_swapped):
    S = x.shape[0]
    nc = S // CH
    al = _pad_lane(a_log)
    dtb = _pad_lane(dt_bias)
    q_scale = HEAD_DIM ** -0.5
    tiles1 = SLOT1 // MM_TN

    kv_rhs = lambda l: pl.BlockSpec((N_CHIPS, None, D_MODEL // N_CHIPS, MM_TN), lambda i, j: (0, l, 0, j))
    kv_rhs_t = lambda l: pl.BlockSpec((None, None, D_MODEL // N_CHIPS, 2 * XA_WIDTH), lambda i, j: (j, l, 0, 0))
    out_rhs = lambda l: pl.BlockSpec((N_CHIPS, None, INNER // N_CHIPS, MM_TN), lambda i, j: (0, l, 0, j))
    out_rhs_t = lambda l: pl.BlockSpec((None, None, MM_TN, D_MODEL), lambda i, j: (j // 2, l, j % 2, 0))
    in1_rhs = pl.BlockSpec((None, 2, D_MODEL // 2, MM_TN), lambda i, j: (j // tiles1, 0, 0, j % tiles1))
    in1_rhs_t = pl.BlockSpec((None, None, D_MODEL // 2, MM_TN), lambda i, j, k: (k // tiles1, j, 0, k % tiles1))
    slot_rows = lambda rows: dict(
        tm=rows, o_spec=pl.BlockSpec((None, rows, MM_TN), lambda i, j: (i, 0, j)),
        o_shape=jax.ShapeDtypeStruct((N_CHIPS, rows, 2 * XA_WIDTH), BF16))
    in1_out = dict(tm=D_MODEL // 2, o_spec=pl.BlockSpec((None, None, D_MODEL // 2, MM_TN),
                                                        lambda i, j: (j // tiles1, i, 0, j % tiles1)),
                   o_shape=jax.ShapeDtypeStruct((N_CHIPS, 2, D_MODEL // 2, SLOT1), BF16))

    h0 = _rmsnorm_fwd(x, norm_g[0], name="norm0")
    proj0 = _matmul(h0, w_in0, name="proj0")
    qn = _dn_pre_fwd(proj0, conv_w, 0, DN_QK_WIDTH, l2=True, scale=q_scale, name="dn_pre_q")
    kn = _dn_pre_fwd(proj0, conv_w, DN_QK_WIDTH, DN_QK_WIDTH, l2=True, scale=1.0, name="dn_pre_k")
    vc = _dn_pre_fwd(proj0, conv_w, 2 * DN_QK_WIDTH, MIX_WIDTH, l2=False, scale=1.0, name="dn_pre_v")
    g_nat, b_nat = _dn_ab_fwd(proj0, al, dtb, name="dn_ab")
    g_rows = _rows_form(g_nat, nc)
    b_rows = _rows_form(b_nat, nc)
    mix0, states, tinv = _dn_core_fwd(qn, kn, vc, g_rows, b_rows, out_g, name="dn_core")
    w_kv, w_out, w_in1 = late_weights(mix0)
    mem_n = _rmsnorm_fwd(mem, mem_norm_g, name="mem_norm")
    kv = [_matmul(mem_n, w_kv, n=2 * XA_WIDTH, tn=MM_TN, b_spec=kv_rhs(l), name=f"kv{l}") for l in range(2)]
    xa0 = _xa_fwd(proj0, P0_XQ, kv[0], xa_q_g[0], xa_k_g[0], name="xa0")
    y0 = _gate_fwd(mix0, xa0, proj0, P0_Z, name="gate0")
    x1 = _matmul(y0, w_out, n=D_MODEL, tn=MM_TN, b_spec=out_rhs(0), res=x, name="out0")

    h1 = _rmsnorm_fwd(x1, norm_g[1], name="norm1")
    proj1 = _matmul(h1, w_in1, n=SB_PROJ, tn=MM_TN, b_spec=in1_rhs, name="proj1")
    mix1, tot1 = _sb_fwd(proj1, sb_q_g, sb_k_g, name="sb")
    xa1 = _xa_fwd(proj1, P1_XQ, kv[1], xa_q_g[1], xa_k_g[1], name="xa1")
    y1 = _gate_fwd(mix1, xa1, proj1, P1_Z, name="gate1")
    x2 = _matmul(y1, w_out, n=D_MODEL, tn=MM_TN, b_spec=out_rhs(1), res=x1, name="out1")

    dx2, loss_vec = _loss_head(x2, target, name="loss")

    d_wout1 = _matmul(y1, dx2, ta=True, name="d_wout1", **slot_rows(INNER // N_CHIPS))
    dy1 = _matmul(dx2, w_out, tb=True, n=INNER, tn=MM_TN, b_spec=out_rhs_t(1), name="dy1")
    dcat1, dz1 = _gate_bwd(dy1, mix1, xa1, proj1, P1_Z, name="gate1_bwd")
    dqkv1, d_sbq, d_sbk = _sb_bwd(dcat1, tot1, proj1, sb_q_g, sb_k_g, name="sb_bwd")
    dxq1, dkv1, d_xaq1, d_xak1 = _xa_bwd(dcat1, proj1, P1_XQ, kv[1], xa_q_g[1], xa_k_g[1], name="xa1_bwd")
    dproj1 = dqkv1 + [dxq1, dz1]
    d_win1 = _matmul(h1, dproj1, ta=True, name="d_win1", **in1_out)
    d_wkv1 = _matmul(mem_n, dkv1, ta=True, name="d_wkv1", **slot_rows(D_MODEL // N_CHIPS))
    token = early_grads(1, d_win1, d_wout1, d_wkv1)
    dproj1 = dqkv1 + [dxq1 + token[0, 0].astype(BF16), dz1]
    dh1 = _matmul(dproj1, w_in1, tb=True, n=D_MODEL, tn=D_MODEL // 2, tk=MM_TN, b_spec=in1_rhs_t, name="dh1")
    token = grads_swapped(dh1)
    dx1, d_ng1 = _rmsnorm_bwd(dh1, x1, norm_g[1] + token[0, 0], dx2, name="norm1_bwd")

    d_wout0 = _matmul(y0, dx1, ta=True, name="d_wout0", **slot_rows(INNER // N_CHIPS))
    dy0 = _matmul(dx1, w_out, tb=True, n=INNER, tn=MM_TN, b_spec=out_rhs_t(0), name="dy0")
    dcat0, dz0 = _gate_bwd(dy0, mix0, xa0, proj0, P0_Z, name="gate0_bwd")
    dqv, dkv_h, dvc, dg_cols, db_cols, d_outg = _dn_core_bwd(
        dcat0, qn, kn, vc, g_rows, b_rows, out_g, states, tinv, name="dn_core_bwd")
    dpq, dwq = _dn_pre_bwd(dqv, proj0, conv_w, 0, DN_QK_WIDTH, l2=True, scale=q_scale, name="dn_pre_q_bwd")
    dpk, dwk = _dn_pre_bwd(dkv_h, proj0, conv_w, DN_QK_WIDTH, DN_QK_WIDTH, l2=True, scale=1.0, name="dn_pre_k_bwd")
    dpv, dwv = _dn_pre_bwd(dvc, proj0, conv_w, 2 * DN_QK_WIDTH, MIX_WIDTH, l2=False, scale=1.0, name="dn_pre_v_bwd")
    dab, d_alog, d_dt = _dn_ab_bwd(_cols_to_nat(dg_cols, nc), _cols_to_nat(db_cols, nc), proj0, al, dtb, name="dn_ab_bwd")
    dxq0, dkv0, d_xaq0, d_xak0 = _xa_bwd(dcat0, proj0, P0_XQ, kv[0], xa_q_g[0], xa_k_g[0], name="xa0_bwd")
    d_win0 = _matmul(h0, [dpq, dpk, dpv, dxq0, dz0, dab], ta=True, out_dtype=BF16, name="d_win0")
    d_wkv0 = _matmul(mem_n, dkv0, ta=True, name="d_wkv0", **slot_rows(D_MODEL // N_CHIPS))
    token = early_grads(0, d_win0, d_wout0, d_wkv0)
    zero = token[0, 0]
    dh0 = _matmul([dpq, dpk, dpv, dxq0, dz0, dab + zero.astype(BF16)], w_in0, tb=True, tk=MM_TN, name="dh0")
    dx0, d_ng0 = _rmsnorm_bwd(dh0, x, norm_g[0] + zero, dx1, name="norm0_bwd")

    dmem0 = _matmul(dkv0, w_kv, tb=True, n=D_MODEL, tn=D_MODEL // N_CHIPS, b_spec=kv_rhs_t(0), name="dmem0")
    dmem_n = _matmul(dkv1, w_kv, tb=True, n=D_MODEL, tn=D_MODEL // N_CHIPS, b_spec=kv_rhs_t(1), res=dmem0, name="dmem1")
    _, d_memg = _rmsnorm_bwd(dmem_n, mem, mem_norm_g, None, name="mem_norm_bwd")

    grads = dict(
        norm_g=jnp.concatenate([d_ng0, d_ng1], axis=0), mem_norm_g=d_memg.reshape(-1),
        xa_q_norm_g=jnp.concatenate([d_xaq0, d_xaq1], axis=0), xa_k_norm_g=jnp.concatenate([d_xak0, d_xak1], axis=0),
        dn_conv_w=jnp.concatenate([dwq, dwk, dwv], axis=1),
        dn_a_log=d_alog[:, :DN_V_HEADS], dn_dt_bias=d_dt[:, :DN_V_HEADS], dn_out_norm_g=d_outg,
        sb_q_norm_g=d_sbq, sb_k_norm_g=d_sbk)
    return loss_vec, dx0, grads


ANY = pl.BlockSpec(memory_space=pl.ANY)


def _place():
    x, y, c = lax.axis_index("x"), lax.axis_index("y"), lax.axis_index("c")
    chips = [(1 - x, y), (x, 1 - y), (1 - x, 1 - y)]
    return x, y, c, 2 * x + y, (x, y, 1 - c), chips


def _rcopy(src, dst, send, recv, i, dev):
    return pltpu.make_async_remote_copy(src_ref=src, dst_ref=dst, send_sem=send.at[i], recv_sem=recv.at[i],
                                        device_id=dev, device_id_type=MESH)


def _swap_halves(xs, *, name):
    nt = len(xs)

    def body(*refs):
        src, dst = refs[:nt], refs[nt:2 * nt]
        send, recv = refs[2 * nt:]
        x, y, c, j, sib, chips = _place()
        cps = []
        for t in range(nt):
            for s in range(N_CHIPS):
                cps.append(_rcopy(src[t].at[s, 1 - c], dst[t].at[s], send, recv, 4 * t + s, sib))
                cps[-1].start()
        for cp in cps:
            cp.wait_recv()
        for cp in cps:
            cp.wait_send()

    return pl.pallas_call(
        body, in_specs=[ANY] * nt, out_specs=[ANY] * nt,
        out_shape=[jax.ShapeDtypeStruct((N_CHIPS,) + a.shape[2:], a.dtype) for a in xs],
        scratch_shapes=[pltpu.SemaphoreType.DMA((4 * nt,)), pltpu.SemaphoreType.DMA((4 * nt,))], name=name)(*xs)


def _swap_with_sibling(fs, *, name):
    nt = len(fs)

    def body(*refs):
        src, dst = refs[:nt], refs[nt:2 * nt]
        send, recv = refs[2 * nt:]
        x, y, c, j, sib, chips = _place()
        cps = [_rcopy(src[t], dst[t], send, recv, t, sib) for t in range(nt)]
        for cp in cps:
            cp.start()
        for cp in cps:
            cp.wait_recv()
        for cp in cps:
            cp.wait_send()

    return pl.pallas_call(
        body, in_specs=[ANY] * nt, out_specs=[ANY] * nt,
        out_shape=[jax.ShapeDtypeStruct(a.shape, a.dtype) for a in fs],
        scratch_shapes=[pltpu.SemaphoreType.DMA((nt,)), pltpu.SemaphoreType.DMA((nt,))], name=name)(*fs)


HBM_SPEC = pl.BlockSpec(memory_space=pltpu.HBM)
SEM_SPEC = pl.BlockSpec(memory_space=pltpu.SEMAPHORE)
SIDE_EFFECT = pltpu.SideEffectType.DATAFLOW_SIDE_EFFECTING


def _gather_plan(src, land):
    x, y, c, j, sib, chips = _place()
    return [(src[t].at[c], land[t].at[j, c], (cx, cy, c), land[t].at[2 * cx + cy, c])
            for t in range(len(src)) for cx, cy in chips]


def _scatter_plan(src, land):
    x, y, c, j, sib, chips = _place()
    return [(src[t].at[2 * cx + cy], land[t].at[k], (cx, cy, c), land[t].at[k])
            for t in range(len(src)) for k, (cx, cy) in enumerate(chips)]


def _swap_plan(src, land):
    x, y, c, j, sib, chips = _place()
    return [(src[t].at[s, 1 - c], land[t].at[s], sib, land[t].at[s]) for t in range(len(src)) for s in range(N_CHIPS)]


def _exchange_start(srcs, lands, plan, *, name, per_tensor=3):
    ns, nb = len(srcs), len(srcs) + len(lands)
    n = per_tensor * ns

    def body(*refs):
        send, recv, token = refs[nb], refs[nb + 1], refs[-1]
        for i, (s, d, dev, _) in enumerate(plan(refs[:ns], refs[ns:nb])):
            _rcopy(s, d, send, recv, i, dev).start()
        token[...] = jnp.zeros_like(token)

    bufs = list(srcs) + list(lands)
    outs = pl.pallas_call(
        body, name=name,
        out_shape=(pltpu.SemaphoreType.DMA((n,)), pltpu.SemaphoreType.DMA((n,)), *[pltpu.HBM(a.shape, a.dtype) for a in bufs],
                   jax.ShapeDtypeStruct((8, LANE), F32)),
        in_specs=[HBM_SPEC] * nb, out_specs=(SEM_SPEC, SEM_SPEC, *[HBM_SPEC] * nb, pl.BlockSpec(memory_space=pltpu.VMEM)),
        input_output_aliases={i: 2 + i for i in range(nb)},
        compiler_params=pltpu.CompilerParams(has_side_effects=SIDE_EFFECT))(
            *[pltpu.with_memory_space_constraint(a, pltpu.HBM) for a in bufs])
    return outs[0], outs[1], list(outs[2:2 + ns]), list(outs[2 + ns:2 + nb]), outs[-1]


def _exchange_wait(srcs, lands, send, recv, after, plan, *, name):
    ns, nb = len(srcs), len(srcs) + len(lands)
    afters = list(after) if isinstance(after, (list, tuple)) else [after]

    def body(*refs):
        send_s, recv_s = refs[nb], refs[nb + 1]
        for i, (s, d, dev, inc) in enumerate(plan(refs[:ns], refs[ns:nb])):
            _rcopy(s, d, send_s, recv_s, i, dev).wait_send()
            _rcopy(inc, inc, send_s, recv_s, i, dev).wait_recv()

    bufs = list(srcs) + list(lands)
    outs = pl.pallas_call(
        body, name=name, out_shape=tuple(pltpu.HBM(a.shape, a.dtype) for a in bufs),
        in_specs=[HBM_SPEC] * nb + [SEM_SPEC, SEM_SPEC] + [ANY] * len(afters), out_specs=tuple([HBM_SPEC] * nb),
        input_output_aliases={i: i for i in range(nb)},
        compiler_params=pltpu.CompilerParams(has_side_effects=SIDE_EFFECT))(*bufs, send, recv, *afters)
    return list(outs[:ns]), list(outs[ns:])


def _forward_halves(lands, *, name):
    nt = len(lands)

    def body(*refs):
        src, dst = refs[:nt], refs[nt:2 * nt]
        send, recv = refs[2 * nt:]
        x, y, c, j, sib, chips = _place()
        cps = []
        for t in range(nt):
            for k, (cx, cy) in enumerate(chips):
                cps.append(_rcopy(src[t].at[2 * cx + cy, c], dst[t].at[2 * cx + cy, c], send, recv, 3 * t + k, sib))
                cps[-1].start()
        for t in range(nt):
            for k, (cx, cy) in enumerate(chips):
                other = dst[t].at[2 * cx + cy, 1 - c]
                _rcopy(other, other, send, recv, 3 * t + k, sib).wait_recv()
        for cp in cps:
            cp.wait_send()

    return pl.pallas_call(
        body, in_specs=[ANY] * nt, out_specs=[ANY] * nt, out_shape=[jax.ShapeDtypeStruct(a.shape, a.dtype) for a in lands],
        input_output_aliases={t: t for t in range(nt)},
        scratch_shapes=[pltpu.SemaphoreType.DMA((3 * nt,)), pltpu.SemaphoreType.DMA((3 * nt,))], name=name)(*lands)


def _all_reduce_small(parts, *, name):
    n = len(parts)
    offs, rows = [], 0
    for p in parts:
        offs.append(rows)
        rows += -(-p.shape[0] // 8) * 8

    def body(*refs):
        p_refs, o_refs = refs[:n], refs[n:2 * n]
        buf, send, recv = refs[2 * n:]
        x, y, c = lax.axis_index("x"), lax.axis_index("y"), lax.axis_index("c")
        me = 4 * x + 2 * y + c
        buf[me] = jnp.zeros((rows, LANE), F32)
        for p_ref, off in zip(p_refs, offs):
            buf[me, off:off + p_ref.shape[0], :] = p_ref[...]
        cps = []
        for r in range(1, 8):
            dev = (x ^ (r >> 2), y ^ ((r >> 1) & 1), c ^ (r & 1))
            cps.append(_rcopy(buf.at[me], buf.at[me], send, recv, r - 1, dev))
            cps[-1].start()
        for r in range(1, 8):
            frm = buf.at[me ^ r]
            _rcopy(frm, frm, send, recv, r - 1, (x, y, c)).wait_recv()
        for cp in cps:
            cp.wait_send()
        acc = buf[0]
        for d in range(1, 8):
            acc = acc + buf[d]
        for o_ref, off in zip(o_refs, offs):
            o_ref[...] = acc[off:off + o_ref.shape[0], :]

    vm = pl.BlockSpec(memory_space=pltpu.VMEM)
    return pl.pallas_call(
        body, in_specs=[vm] * n, out_specs=[vm] * n, out_shape=[jax.ShapeDtypeStruct(p.shape, F32) for p in parts],
        scratch_shapes=[pltpu.VMEM((8, rows, LANE), F32), pltpu.SemaphoreType.DMA((7,)), pltpu.SemaphoreType.DMA((7,))],
        name=name)(*parts)


def _add_halves(x, b, c_idx, *, name, tr=256):
    _, _, R, C = x.shape
    tr = min(tr, R)

    def body(c_ref, x_ref, b_ref, o_ref):
        o_ref[...] = (x_ref[...].astype(F32) + b_ref[...].astype(F32)).astype(o_ref.dtype)

    return pl.pallas_call(
        body,
        grid_spec=pltpu.PrefetchScalarGridSpec(
            num_scalar_prefetch=1, grid=(N_CHIPS, R // tr),
            in_specs=[pl.BlockSpec((None, None, tr, C), lambda s, i, c_ref: (s, c_ref[0], i, 0)),
                      pl.BlockSpec((None, tr, C), lambda s, i, c_ref: (s, i, 0))],
            out_specs=pl.BlockSpec((None, tr, C), lambda s, i, c_ref: (s, i, 0))),
        out_shape=jax.ShapeDtypeStruct(b.shape, b.dtype), compiler_params=_cp(("parallel", "parallel")), name=name)(c_idx, x, b)


def _sum_slot(p, rcv, j_idx, *, name, tr=256):
    _, R, C = p.shape
    tr = min(tr, R)

    def body(j_ref, p_ref, r_ref, o_ref):
        acc = p_ref[...].astype(F32)
        for k in range(3):
            acc = acc + r_ref[k].astype(F32)
        o_ref[...] = acc

    return pl.pallas_call(
        body,
        grid_spec=pltpu.PrefetchScalarGridSpec(
            num_scalar_prefetch=1, grid=(R // tr,),
            in_specs=[pl.BlockSpec((None, tr, C), lambda i, j_ref: (j_ref[0], i, 0)),
                      pl.BlockSpec((3, tr, C), lambda i, j_ref: (0, i, 0))],
            out_specs=pl.BlockSpec((tr, C), lambda i, j_ref: (i, 0))),
        out_shape=jax.ShapeDtypeStruct((R, C), F32), compiler_params=_cp(("parallel",)), name=name)(j_idx, p, rcv)


def _adamw_math(w, g, m, v):
    nm = ADAM_B1 * m + (1.0 - ADAM_B1) * g
    nv = ADAM_B2 * v + (1.0 - ADAM_B2) * (g * g)
    m_hat = nm / (1.0 - ADAM_B1 ** ADAM_STEP)
    v_hat = nv / (1.0 - ADAM_B2 ** ADAM_STEP)
    return -ADAM_LR * (m_hat / (jnp.sqrt(v_hat) + ADAM_EPS) + ADAM_WD * w), nm, nv


def _adamw_halves(w, g_mine, g_theirs, m, v, c_idx, *, name, layer=0, into=None, tr=128):
    _, _, R, C = w.shape
    tr = tr if R % tr == 0 else R

    def body(c_ref, w_ref, gm_ref, gt_ref, m_ref, v_ref, *rest):
        g_ref, d_ref, nm_ref, nv_ref = rest[-4:]
        gv = jnp.where(pl.program_id(0) == c_ref[0], gm_ref[...], gt_ref[...])
        d, nm, nv = _adamw_math(w_ref[...], gv, m_ref[...], v_ref[...])
        g_ref[...] = gv
        d_ref[...] = d
        nm_ref[...] = nm
        nv_ref[...] = nv

    full = pl.BlockSpec((None, None, tr, C), lambda hh, i, c_ref: (layer, hh, i, 0))
    half = pl.BlockSpec((tr, C), lambda hh, i, c_ref: (i, 0))
    sh = jax.ShapeDtypeStruct(w.shape, F32)
    extra = [] if into is None else list(into)
    return pl.pallas_call(
        body,
        grid_spec=pltpu.PrefetchScalarGridSpec(num_scalar_prefetch=1, grid=(2, R // tr),
                                               in_specs=[full, half, half, full, full] + [ANY] * len(extra),
                                               out_specs=(full,) * 4),
        out_shape=(sh,) * 4, input_output_aliases={6 + t: t for t in range(len(extra))},
        compiler_params=_cp(("parallel", "parallel")), name=name)(c_idx, w, g_mine, g_theirs, m, v, *extra)


def _adamw_parts(ws, gs, ms, vs, *, name):
    n = len(ws)

    def body(*refs):
        ins, outs = refs[:4 * n], refs[4 * n:]
        for t in range(n):
            d, nm, nv = _adamw_math(ins[t][...], ins[n + t][...], ins[2 * n + t][...], ins[3 * n + t][...])
            outs[t][...] = d
            outs[n + t][...] = nm
            outs[2 * n + t][...] = nv

    vm = pl.BlockSpec(memory_space=pltpu.VMEM)
    shapes = [jax.ShapeDtypeStruct(w.shape, F32) for w in ws] * 3
    outs = pl.pallas_call(body, in_specs=[vm] * (4 * n), out_specs=[vm] * (3 * n), out_shape=shapes, name=name)(
        *ws, *gs, *ms, *vs)
    return outs[:n], outs[n:2 * n], outs[2 * n:]


_SMALL = ["norm_g", "mem_norm_g", "xa_q_norm_g", "xa_k_norm_g", "dn_a_log", "dn_dt_bias", "dn_out_norm_g",
          "sb_q_norm_g", "sb_k_norm_g"]


def _rows128(a):
    flat = a.reshape(-1)
    pad = -flat.shape[0] % LANE
    if pad:
        flat = jnp.pad(flat, (0, pad))
    return flat.reshape(-1, LANE)


def _unrows(r, shape):
    return r.reshape(-1)[:math.prod(shape)].reshape(shape)


def kernel(x, mem, norm_g, mem_norm_g, mem_w_kv, xa_q_norm_g, xa_k_norm_g, w_out, dn_w_in, dn_conv_w, dn_a_log, dn_dt_bias, dn_out_norm_g, sb_w_in, sb_q_norm_g, sb_k_norm_g, loss_target, m_norm_g, m_mem_norm_g, m_mem_w_kv, m_xa_q_norm_g, m_xa_k_norm_g, m_w_out, m_dn_w_in, m_dn_conv_w, m_dn_a_log, m_dn_dt_bias, m_dn_out_norm_g, m_sb_w_in, m_sb_q_norm_g, m_sb_k_norm_g, v_norm_g, v_mem_norm_g, v_mem_w_kv, v_xa_q_norm_g, v_xa_k_norm_g, v_w_out, v_dn_w_in, v_dn_conv_w, v_dn_a_log, v_dn_dt_bias, v_dn_out_norm_g, v_sb_w_in, v_sb_q_norm_g, v_sb_k_norm_g):
    W = dict(norm_g=norm_g, mem_norm_g=mem_norm_g, mem_w_kv=mem_w_kv, xa_q_norm_g=xa_q_norm_g, xa_k_norm_g=xa_k_norm_g,
             w_out=w_out, dn_w_in=dn_w_in, dn_conv_w=dn_conv_w, dn_a_log=dn_a_log, dn_dt_bias=dn_dt_bias,
             dn_out_norm_g=dn_out_norm_g, sb_w_in=sb_w_in, sb_q_norm_g=sb_q_norm_g, sb_k_norm_g=sb_k_norm_g)
    M = dict(norm_g=m_norm_g, mem_norm_g=m_mem_norm_g, mem_w_kv=m_mem_w_kv, xa_q_norm_g=m_xa_q_norm_g,
             xa_k_norm_g=m_xa_k_norm_g, w_out=m_w_out, dn_w_in=m_dn_w_in, dn_conv_w=m_dn_conv_w, dn_a_log=m_dn_a_log,
             dn_dt_bias=m_dn_dt_bias, dn_out_norm_g=m_dn_out_norm_g, sb_w_in=m_sb_w_in, sb_q_norm_g=m_sb_q_norm_g,
             sb_k_norm_g=m_sb_k_norm_g)
    V = dict(norm_g=v_norm_g, mem_norm_g=v_mem_norm_g, mem_w_kv=v_mem_w_kv, xa_q_norm_g=v_xa_q_norm_g,
             xa_k_norm_g=v_xa_k_norm_g, w_out=v_w_out, dn_w_in=v_dn_w_in, dn_conv_w=v_dn_conv_w, dn_a_log=v_dn_a_log,
             dn_dt_bias=v_dn_dt_bias, dn_out_norm_g=v_dn_out_norm_g, sb_w_in=v_sb_w_in, sb_q_norm_g=v_sb_q_norm_g,
             sb_k_norm_g=v_sb_k_norm_g)
    names = ["norm_g", "mem_norm_g", "mem_w_kv", "xa_q_norm_g", "xa_k_norm_g", "w_out", "dn_w_in", "dn_conv_w",
             "dn_a_log", "dn_dt_bias", "dn_out_norm_g", "sb_w_in", "sb_q_norm_g", "sb_k_norm_g"]
    cx, cy, cc = lax.axis_index("x"), lax.axis_index("y"), lax.axis_index("c")
    slot = 2 * cx + cy
    half_r = D_MODEL // 2
    conv_cols = dn_conv_w.shape[2]

    w0s = jnp.pad(dn_w_in[0].astype(BF16), ((0, 0), (0, P0_SHARD_PAD - P0_SHARD))).reshape(2, half_r, P0_SHARD_PAD)
    convs = jnp.pad(dn_conv_w[0], ((0, 8 - DN_CONV), (0, 0))).reshape(8, 2, conv_cols // 2).transpose(1, 0, 2)
    c_idx = jnp.reshape(cc, (1,)).astype(jnp.int32)
    j_idx = jnp.reshape(slot, (1,)).astype(jnp.int32)
    own_a = [w0s, convs]
    lands_a = [lax.dynamic_update_slice(lax.empty((N_CHIPS,) + o.shape, o.dtype), o[None], (slot, 0, 0, 0)) for o in own_a]
    send_a, recv_a, own_a, lands_a, token_a = _exchange_start(own_a, lands_a, _gather_plan, name="gather_start")
    zero_a = token_a[0, 0]
    M["dn_w_in"] = m_dn_w_in + zero_a
    V["dn_w_in"] = v_dn_w_in + zero_a
    own_b = [(sb_w_in[0] + zero_a).astype(BF16).reshape(2, half_r, SB_PROJ // N_CHIPS), (w_out + zero_a).astype(BF16),
             (mem_w_kv + zero_a).astype(BF16)]
    view0 = (1, 2, half_r, P0_SHARD)
    _, lands_a = _exchange_wait(own_a, lands_a, send_a, recv_a,
                                [M["dn_w_in"].reshape(view0), V["dn_w_in"].reshape(view0)] + own_b,
                                _gather_plan, name="gather_wait")
    (g0, gconv), own_b = lax.optimization_barrier((_forward_halves(lands_a, name="gather_forward"), own_b))
    lands_b = [lax.dynamic_update_slice(lax.empty((N_CHIPS,) + o.shape, o.dtype), o[None], (slot, 0, 0, 0)) for o in own_b]
    send_b, recv_b, own_b, lands_b, token_b = _exchange_start(own_b, lands_b, _gather_plan, name="gather_late_start")

    def late_weights(after):
        _, lands = _exchange_wait(own_b, lands_b, send_b, recv_b, after, _gather_plan, name="gather_late_wait")
        g1, gout, gkv = _forward_halves(lands, name="gather_late_forward")
        return gkv, gout, g1

    rs = {}

    def scatter_start(tag, xs, from_sib=None):
        if from_sib is None:
            from_sib = _swap_halves(xs, name=f"rs{tag}_swap")
        ps = [_add_halves(a, b, c_idx, name=f"rs{tag}_add{t}") for t, (a, b) in enumerate(zip(xs, from_sib))]
        rcv = [lax.empty((3,) + p.shape[1:], p.dtype) for p in ps]
        send, recv, ps, rcv, token = _exchange_start(ps, rcv, _scatter_plan, name=f"rs{tag}_scatter_start")
        rs[tag] = (ps, rcv, send, recv)
        return token

    def scatter_finish(tag, after):
        ps, rcv, send, recv = rs[tag]
        ps, rcv = _exchange_wait(ps, rcv, send, recv, after, _scatter_plan, name=f"rs{tag}_scatter_wait")
        return [_sum_slot(p, r, j_idx, name=f"rs{tag}_sum{t}") for t, (p, r) in enumerate(zip(ps, rcv))]

    def early_grads(layer, d_win, d_wout, d_wkv):
        if layer == 0:
            d_win = jnp.stack([jnp.pad(jnp.concatenate(
                [d_win[:, lo:hi] for lo, hi in _padded_pieces(s * P0_SHARD, (s + 1) * P0_SHARD)], axis=1),
                ((0, 0), (0, P0_SHARD_PAD - P0_SHARD))) for s in range(N_CHIPS)]).reshape(N_CHIPS, 2, half_r, P0_SHARD_PAD)
        xs = [d_win, d_wout.reshape(N_CHIPS, 2, -1, D_MODEL), d_wkv.reshape(N_CHIPS, 2, -1, 2 * XA_WIDTH)]
        if layer == 0:
            return scatter_start(0, xs)
        lands = [lax.empty((N_CHIPS,) + a.shape[2:], a.dtype) for a in xs]
        send, recv, xs, lands, token = _exchange_start(xs, lands, _swap_plan, per_tensor=N_CHIPS, name="rs1_swap_start")
        rs["swap1"] = (xs, lands, send, recv)
        return token

    def grads_swapped(after):
        xs, lands, send, recv = rs["swap1"]
        xs, from_sib = _exchange_wait(xs, lands, send, recv, after, _swap_plan, name="rs1_swap_wait")
        return scatter_start(1, xs, from_sib)

    shards0 = g0.reshape(N_CHIPS, D_MODEL, P0_SHARD_PAD)
    z = lambda n: jnp.zeros((D_MODEL, n), BF16)
    w_in0 = jnp.concatenate(
        [shards0[s][:, lo:hi] for s, lo, hi in _true_pieces(0, _C_QKV) + _true_pieces(_C_QKV + 2 * DN_V_HEADS, DN_PROJ)]
        + [shards0[s][:, lo:hi] for s, lo, hi in _true_pieces(_C_QKV, _C_QKV + DN_V_HEADS)] + [z(LANE - DN_V_HEADS)]
        + [shards0[s][:, lo:hi] for s, lo, hi in _true_pieces(_C_QKV + DN_V_HEADS, _C_QKV + 2 * DN_V_HEADS)]
        + [z(P0 - P0_AB - LANE - DN_V_HEADS)], axis=1)
    conv_f = gconv.transpose(2, 0, 1, 3).reshape(8, N_CHIPS * conv_cols)[:DN_CONV]

    loss_vec, grad_x, g = _local_step(
        x[0], mem[0], loss_target[0], norm_g + token_b[0, 0], mem_norm_g, xa_q_norm_g, xa_k_norm_g, w_in0, conv_f,
        dn_a_log[0], dn_dt_bias[0], dn_out_norm_g[0], sb_q_norm_g[0], sb_k_norm_g[0], late_weights, early_grads,
        grads_swapped)

    mine1 = scatter_finish(1, grad_x)
    theirs1 = _swap_with_sibling(mine1, name="rs1_join")
    big1 = [("sb_w_in", None), ("w_out", 1), ("mem_w_kv", 1)]
    big0 = [("dn_w_in", None), ("w_out", 0), ("mem_w_kv", 0)]

    out_g, out_d, out_m, out_v = {}, {}, {}, {}
    partial = {}

    def adamw_big(big, mine, theirs):
        for (n, layer), mine_g, their_g in zip(big, mine, theirs):
            layers = 1 if layer is None else 2
            view = (layers, 2) + mine_g.shape
            partial[n] = _adamw_halves(W[n].reshape(view), mine_g, their_g, M[n].reshape(view), V[n].reshape(view), c_idx,
                                       layer=layer or 0, into=partial.get(n), name=f"adamw_{n}" + ("" if layer is None else str(layer)))
        return [partial[n][0] for n, _ in big]

    done1 = lax.optimization_barrier(tuple(adamw_big(big1, mine1, theirs1)))[-1]
    mine0 = scatter_finish(0, done1)
    mine0[0] = mine0[0][:, :P0_SHARD]

    parts, _ = lax.optimization_barrier(([_rows128(g[n]) for n in _SMALL] + [_rows128(g["dn_conv_w"]), loss_vec], mine0[0]))
    red = _all_reduce_small(parts, name="all_reduce_small")
    small_rows = dict(zip(_SMALL, red))
    conv_full = red[len(_SMALL)].reshape(DN_CONV, N_CHIPS * conv_cols)
    small_rows["dn_conv_w"] = _rows128(lax.dynamic_slice_in_dim(conv_full, slot * conv_cols, conv_cols, axis=1))
    loss = red[-1][0, 0]

    adamw_big(big0, mine0, _swap_with_sibling(mine0, name="rs0_join"))
    for n, outs in partial.items():
        out_g[n], out_d[n], out_m[n], out_v[n] = [o.reshape(W[n].shape) for o in outs]
    small_names = _SMALL + ["dn_conv_w"]
    ds, nms, nvs = _adamw_parts([_rows128(W[n]) for n in small_names], [small_rows[n] for n in small_names],
                                [_rows128(M[n]) for n in small_names], [_rows128(V[n]) for n in small_names], name="adamw_small")
    for n, d, nm, nv in zip(small_names, ds, nms, nvs):
        shp = W[n].shape
        out_g[n], out_d[n], out_m[n], out_v[n] = [_unrows(r, shp) for r in (small_rows[n], d, nm, nv)]

    return (loss, grad_x[None], *[out_g[n] for n in names], *[out_d[n] for n in names], *[out_m[n] for n in names],
            *[out_v[n] for n in names])
```

```python
import math

import jax
import jax.numpy as jnp
from jax import lax
from jax.experimental import pallas as pl
from jax.experimental.pallas import tpu as pltpu

F32 = jnp.float32
BF16 = jnp.bfloat16
HI = lax.Precision.HIGHEST
MESH = pl.DeviceIdType.MESH

D_MODEL = 2048
INNER = 4096
XA_WIDTH = 1024
XA_HEADS = 4
XA_DIM = 256
MIX_WIDTH = 3072
HEAD_DIM = 128
DN_V_HEADS = 24
DN_QK_WIDTH = 1536
DN_CONV = 4
DN_PROJ = 11312
SB_PROJ = 14336
EPS = 1e-6
N_CHIPS = 4

CH = 128
LANE = 128

P0_XQ = 6144
P0_Z = 7168
P0_AB = 11264
P0 = 11776
P0_SHARD = DN_PROJ // N_CHIPS
P1_XQ = 9216
P1_Z = 10240
P1 = SB_PROJ

ADAM_LR = 0.001
ADAM_B1 = 0.9
ADAM_B2 = 0.999
ADAM_EPS = 1e-08
ADAM_WD = 0.01
ADAM_STEP = 10

VMEM_LIMIT = 48 * 1024 * 1024


def _cp(sem=None, **kw):
    return pltpu.CompilerParams(dimension_semantics=sem, vmem_limit_bytes=VMEM_LIMIT, **kw)


def _bdot(a, b, dims):
    return lax.dot_general(a.astype(BF16), b.astype(BF16), (dims, ((), ())), preferred_element_type=F32)


def _fdot(a, b, dims):
    return lax.dot_general(a, b, (dims, ((), ())), precision=HI, preferred_element_type=F32)


NN = ((1,), (0,))
NT = ((1,), (1,))
TN = ((0,), (0,))


def _sigmoid(x):
    return 1.0 / (1.0 + jnp.exp(-x))


def _softplus(x):
    return jnp.maximum(x, 0.0) + jnp.log(1.0 + jnp.exp(-jnp.abs(x)))


def _iota2(shape, axis):
    return lax.broadcasted_iota(jnp.int32, shape, axis)


MM_FULL_K = 4096
MM_BLOCK_BYTES = 4 * 1024 * 1024


def _matmul(a, b, *, ta=False, tb=False, out_dtype=F32, res=None, name, n=None, tm=None, tn=None, tk=None,
            b_spec=None, o_spec=None, o_shape=None):
    a_segs = list(a) if isinstance(a, (list, tuple)) else [a]
    b_segs = list(b) if isinstance(b, (list, tuple)) else [b]
    a0, b0 = a_segs[0], b_segs[0]
    M = a0.shape[1] if ta else a0.shape[0]
    K = a0.shape[0] if ta else sum(s.shape[1] for s in a_segs)
    if n is None:
        n = b0.shape[0] if tb else sum(s.shape[1] for s in b_segs)
    N = n
    dims = ((0,) if ta else (1,), (1,) if tb else (0,))
    has_res = res is not None
    flat = lambda v: v.reshape(-1, v.shape[-1])
    o_shape = o_shape or jax.ShapeDtypeStruct((M, N), out_dtype)

    def seg_specs(segs, tile, block, pos):
        specs, ranges, off = [], [], 0
        for s in segs:
            cnt = s.shape[1] // tile
            assert s.shape[1] % tile == 0, (name, s.shape, tile)

            def imap(*g, off=off, cnt=cnt):
                t = jnp.clip(g[pos] - off, 0, cnt - 1)
                return (g[0], t) if pos == 2 else (0, t)

            specs.append(pl.BlockSpec(block, imap))
            ranges.append((off, off + cnt))
            off += cnt
        return specs, ranges

    if K <= MM_FULL_K:
        assert len(a_segs) == 1
        tm = tm or min(M, 1024, max(256, MM_BLOCK_BYTES // (K * a0.dtype.itemsize)))
        tn = tn or min(N, 512)
        assert M % tm == 0 and N % tn == 0, (name, M, N, K, tm, tn)
        nb = len(b_segs)
        if b_spec is not None:
            b_specs, b_ranges = [b_spec], [(0, N // tn)]
        elif nb > 1:
            assert not tb
            b_specs, b_ranges = seg_specs(b_segs, tn, (K, tn), 1)
        else:
            b_specs = [pl.BlockSpec((tn, K), lambda i, j: (j, 0)) if tb else pl.BlockSpec((K, tn), lambda i, j: (0, j))]
            b_ranges = [(0, N // tn)]

        def body_full(*refs):
            a_ref, b_refs = refs[0], refs[1:1 + nb]
            r_ref = refs[1 + nb] if has_res else None
            o_ref = refs[-1]
            j = pl.program_id(1)
            for b_ref, (lo, hi) in zip(b_refs, b_ranges):
                def emit(b_ref=b_ref):
                    r = _bdot(a_ref[...], flat(b_ref[...]), dims)
                    if has_res:
                        r = r + r_ref[...]
                    o_ref[...] = r.astype(o_ref.dtype).reshape(o_ref.shape)
                if nb == 1:
                    emit()
                else:
                    pl.when(jnp.logical_and(j >= lo, j < hi))(emit)

        a_spec = pl.BlockSpec((K, tm), lambda i, j: (0, i)) if ta else pl.BlockSpec((tm, K), lambda i, j: (i, 0))
        o_spec = o_spec or pl.BlockSpec((tm, tn), lambda i, j: (i, j))
        r_spec = [pl.BlockSpec((tm, tn), lambda i, j: (i, j))] if has_res else []
        return pl.pallas_call(
            body_full, grid=(M // tm, N // tn), in_specs=[a_spec] + b_specs + r_spec, out_specs=o_spec, out_shape=o_shape,
            compiler_params=_cp(("parallel", "arbitrary")), name=name)(*([a0] + b_segs + ([res] if has_res else [])))

    assert tb and not ta and len(b_segs) == 1
    tm, tn = tm or min(M, 1024), tn or min(N, 1024)
    tk = tk or (1024 if all(s.shape[1] % 1024 == 0 for s in a_segs) else 512)
    assert M % tm == 0 and N % tn == 0 and K % tk == 0, (name, M, N, K, tm, tn, tk)
    nk = K // tk
    na = len(a_segs)
    if na > 1:
        a_specs, a_ranges = seg_specs(a_segs, tk, (tm, tk), 2)
    else:
        a_specs, a_ranges = [pl.BlockSpec((tm, tk), lambda i, j, k: (i, k))], [(0, nk)]
    b_spec = b_spec or pl.BlockSpec((tn, tk), lambda i, j, k: (j, k))

    def body(*refs):
        a_refs, b_ref = refs[:na], refs[na]
        r_ref = refs[na + 1] if has_res else None
        o_ref, acc = refs[-2], refs[-1]
        k = pl.program_id(2)

        @pl.when(k == 0)
        def _():
            acc[...] = jnp.zeros_like(acc)

        for a_ref, (lo, hi) in zip(a_refs, a_ranges):
            def emit(a_ref=a_ref):
                acc[...] += _bdot(a_ref[...], flat(b_ref[...]), dims)
            if na == 1:
                emit()
            else:
                pl.when(jnp.logical_and(k >= lo, k < hi))(emit)

        @pl.when(k == nk - 1)
        def _():
            r = acc[...]
            if has_res:
                r = r + r_ref[...]
            o_ref[...] = r.astype(o_ref.dtype).reshape(o_ref.shape)

    o_spec = o_spec or pl.BlockSpec((tm, tn), lambda i, j, k: (i, j))
    r_spec = [pl.BlockSpec((tm, tn), lambda i, j, k: (i, j))] if has_res else []
    return pl.pallas_call(
        body, grid=(M // tm, N // tn, nk), in_specs=a_specs + [b_spec] + r_spec, out_specs=o_spec, out_shape=o_shape,
        scratch_shapes=[pltpu.VMEM((tm, tn), F32)],
        compiler_params=_cp(("parallel", "parallel", "arbitrary")), name=name)(*(a_segs + [b0] + ([res] if has_res else [])))


def _rmsnorm_fwd(x, g, *, name, tm=256):
    S, Dm = x.shape
    tm = min(tm, S)

    def body(x_ref, g_ref, o_ref):
        xv = x_ref[...]
        r = lax.rsqrt(jnp.mean(xv * xv, axis=-1, keepdims=True) + EPS)
        o_ref[...] = (xv * r * g_ref[...]).astype(BF16)

    return pl.pallas_call(
        body, grid=(S // tm,), in_specs=[pl.BlockSpec((tm, Dm), lambda i: (i, 0)), pl.BlockSpec((1, Dm), lambda i: (0, 0))],
        out_specs=pl.BlockSpec((tm, Dm), lambda i: (i, 0)), out_shape=jax.ShapeDtypeStruct((S, Dm), BF16),
        compiler_params=_cp(("parallel",)), name=name)(x, g.reshape(1, Dm))


def _rmsnorm_bwd(dh, x, g, dres, *, name, tm=256):
    S, Dm = x.shape
    tm = min(tm, S)
    want_dx = dres is not None

    def body(*refs):
        if want_dx:
            dh_ref, x_ref, g_ref, dr_ref, dx_ref, dg_ref = refs
        else:
            dh_ref, x_ref, g_ref, dg_ref = refs
        i = pl.program_id(0)
        xv = x_ref[...]
        dhv = dh_ref[...]
        r = lax.rsqrt(jnp.mean(xv * xv, axis=-1, keepdims=True) + EPS)
        y = xv * r
        part = jnp.sum(dhv * y, axis=0, keepdims=True)

        @pl.when(i == 0)
        def _():
            dg_ref[...] = jnp.zeros_like(dg_ref)

        dg_ref[...] += part
        if want_dx:
            dy = dhv * g_ref[...]
            dx_ref[...] = dr_ref[...] + r * (dy - y * jnp.mean(dy * y, axis=-1, keepdims=True))

    row = pl.BlockSpec((tm, Dm), lambda i: (i, 0))
    vec = pl.BlockSpec((1, Dm), lambda i: (0, 0))
    if want_dx:
        dx, dg = pl.pallas_call(
            body, grid=(S // tm,), in_specs=[row, row, vec, row], out_specs=(row, vec),
            out_shape=(jax.ShapeDtypeStruct((S, Dm), F32), jax.ShapeDtypeStruct((1, Dm), F32)),
            compiler_params=_cp(("arbitrary",)), name=name)(dh, x, g.reshape(1, Dm), dres)
        return dx, dg
    dg = pl.pallas_call(
        body, grid=(S // tm,), in_specs=[row, row, vec], out_specs=vec,
        out_shape=jax.ShapeDtypeStruct((1, Dm), F32), compiler_params=_cp(("arbitrary",)), name=name)(dh, x, g.reshape(1, Dm))
    return None, dg


GATE_TN = XA_WIDTH
GATE_MIX_TILES = MIX_WIDTH // GATE_TN


def _gate_cat_specs(tm):
    return [pl.BlockSpec((tm, GATE_TN), lambda i, j: (i, jnp.minimum(j, GATE_MIX_TILES - 1))),
            pl.BlockSpec((tm, GATE_TN), lambda i, j: (i, 0))]


def _gate_fwd(mix, xa, proj, z_off, *, name, tm=256):
    S = mix.shape[0]
    tm = min(tm, S)
    zb = z_off // GATE_TN

    def body(m_ref, x_ref, z_ref, y_ref):
        z = z_ref[...]
        c = jnp.where(pl.program_id(1) < GATE_MIX_TILES, m_ref[...], x_ref[...])
        y_ref[...] = (c * z * _sigmoid(z)).astype(BF16)

    blk = pl.BlockSpec((tm, GATE_TN), lambda i, j: (i, j))
    return pl.pallas_call(
        body, grid=(S // tm, INNER // GATE_TN),
        in_specs=_gate_cat_specs(tm) + [pl.BlockSpec((tm, GATE_TN), lambda i, j: (i, zb + j))],
        out_specs=blk, out_shape=jax.ShapeDtypeStruct((S, INNER), BF16),
        compiler_params=_cp(("parallel", "arbitrary")), name=name)(mix, xa, proj)


def _gate_bwd(dy, mix, xa, proj, z_off, *, name, tm=256):
    S = mix.shape[0]
    tm = min(tm, S)
    zb = z_off // GATE_TN

    def body(dy_ref, m_ref, x_ref, z_ref, dc_ref, dz_ref):
        z = z_ref[...]
        sg = _sigmoid(z)
        d = dy_ref[...]
        c = jnp.where(pl.program_id(1) < GATE_MIX_TILES, m_ref[...], x_ref[...])
        dc_ref[...] = d * z * sg
        dz_ref[...] = (d * c * sg * (1.0 + z * (1.0 - sg))).astype(BF16)

    blk = pl.BlockSpec((tm, GATE_TN), lambda i, j: (i, j))
    return pl.pallas_call(
        body, grid=(S // tm, INNER // GATE_TN),
        in_specs=[blk] + _gate_cat_specs(tm) + [pl.BlockSpec((tm, GATE_TN), lambda i, j: (i, zb + j))], out_specs=(blk, blk),
        out_shape=(jax.ShapeDtypeStruct((S, INNER), F32), jax.ShapeDtypeStruct((S, INNER), BF16)),
        compiler_params=_cp(("parallel", "arbitrary")), name=name)(dy, mix, xa, proj)


def _loss_head(x, target, *, name, tm=256):
    S, Dm = x.shape
    tm = min(tm, S)

    nt = S // tm

    def body(x_ref, t_ref, dx_ref, l_ref, acc):
        i = pl.program_id(0)
        e = x_ref[...] - t_ref[...]
        dx_ref[...] = e * (1.0 / Dm)

        @pl.when(i == 0)
        def _():
            acc[...] = jnp.zeros_like(acc)

        acc[...] += jnp.sum(e * e, axis=0, keepdims=True) * (0.5 / Dm)

        @pl.when(i == nt - 1)
        def _():
            l_ref[...] = jnp.sum(acc[...], axis=1, keepdims=True) + jnp.zeros((1, LANE), F32)

    row = pl.BlockSpec((tm, Dm), lambda i: (i, 0))
    return pl.pallas_call(
        body, grid=(nt,), in_specs=[row, row], out_specs=(row, pl.BlockSpec((1, LANE), lambda i: (0, 0))),
        out_shape=(jax.ShapeDtypeStruct((S, Dm), F32), jax.ShapeDtypeStruct((1, LANE), F32)),
        scratch_shapes=[pltpu.VMEM((1, Dm), F32)],
        compiler_params=_cp(("arbitrary",)), name=name)(x, target)


def _xa_norm(v, g):
    r = lax.rsqrt(jnp.mean(v * v, axis=-1, keepdims=True) + EPS)
    return v * r, r


def _xa_fwd(proj, xq_off, kv, gq, gk, *, name, tm=512):
    S = proj.shape[0]
    tm = min(tm, S)
    qb = xq_off // XA_DIM
    n_mem = kv.shape[0]
    scale = XA_DIM ** -0.5

    def body(q_ref, k_ref, v_ref, gq_ref, gk_ref, o_ref):
        qh, _ = _xa_norm(q_ref[...], None)
        kh, _ = _xa_norm(k_ref[...], None)
        qn = qh * gq_ref[...]
        kn = kh * gk_ref[...]
        s = _bdot(qn, kn, NT) * scale
        s = s - jnp.max(s, axis=-1, keepdims=True)
        p = jnp.exp(s)
        p = p / jnp.sum(p, axis=-1, keepdims=True)
        o_ref[...] = _bdot(p, v_ref[...], NN)

    vec = pl.BlockSpec((1, XA_DIM), lambda h, i: (0, 0))
    return pl.pallas_call(
        body, grid=(XA_HEADS, S // tm),
        in_specs=[pl.BlockSpec((tm, XA_DIM), lambda h, i: (i, qb + h)),
                  pl.BlockSpec((n_mem, XA_DIM), lambda h, i: (0, h)),
                  pl.BlockSpec((n_mem, XA_DIM), lambda h, i: (0, XA_HEADS + h)), vec, vec],
        out_specs=pl.BlockSpec((tm, XA_DIM), lambda h, i: (i, h)),
        out_shape=jax.ShapeDtypeStruct((S, XA_WIDTH), F32),
        compiler_params=_cp(("parallel", "parallel")), name=name)(proj, kv, kv, gq.reshape(1, XA_DIM), gk.reshape(1, XA_DIM))


def _xa_bwd(dcat, proj, xq_off, kv, gq, gk, *, name, tm=512):
    S = proj.shape[0]
    tm = min(tm, S)
    nt = S // tm
    qb = xq_off // XA_DIM
    db = MIX_WIDTH // XA_DIM
    n_mem = kv.shape[0]
    scale = XA_DIM ** -0.5

    def body(d_ref, q_ref, k_ref, v_ref, gq_ref, gk_ref, dq_ref, dk_ref, dv_ref, dgq_ref, dgk_ref, dkn_acc):
        h = pl.program_id(0)
        i = pl.program_id(1)
        q = q_ref[...]
        k = k_ref[...]
        qh, rq = _xa_norm(q, None)
        kh, rk = _xa_norm(k, None)
        gqv = gq_ref[...]
        gkv = gk_ref[...]
        qn = qh * gqv
        kn = kh * gkv
        s = _bdot(qn, kn, NT) * scale
        s = s - jnp.max(s, axis=-1, keepdims=True)
        p = jnp.exp(s)
        p = p / jnp.sum(p, axis=-1, keepdims=True)
        d = d_ref[...]
        dp = _bdot(d, v_ref[...], NT)
        ds = p * (dp - jnp.sum(dp * p, axis=-1, keepdims=True)) * scale
        dqn = _bdot(ds, kn, NN)

        @pl.when(i == 0)
        def _():
            dkn_acc[...] = jnp.zeros_like(dkn_acc)
            dv_ref[...] = jnp.zeros_like(dv_ref)

        @pl.when(jnp.logical_and(i == 0, h == 0))
        def _():
            dgq_ref[...] = jnp.zeros_like(dgq_ref)
            dgk_ref[...] = jnp.zeros_like(dgk_ref)

        dkn_acc[...] += _bdot(ds, qn, TN)
        dv_ref[...] += _bdot(p, d, TN)
        dgq_ref[...] += jnp.sum(dqn * qh, axis=0, keepdims=True)
        dy = dqn * gqv
        dq_ref[...] = (rq * (dy - qh * jnp.mean(dy * qh, axis=-1, keepdims=True))).astype(BF16)

        @pl.when(i == nt - 1)
        def _():
            dkn = dkn_acc[...]
            dgk_ref[...] += jnp.sum(dkn * kh, axis=0, keepdims=True)
            dyk = dkn * gkv
            dk_ref[...] = rk * (dyk - kh * jnp.mean(dyk * kh, axis=-1, keepdims=True))

    vec = pl.BlockSpec((1, XA_DIM), lambda h, i: (0, 0))
    kblk = pl.BlockSpec((n_mem, XA_DIM), lambda h, i: (0, h))
    vblk = pl.BlockSpec((n_mem, XA_DIM), lambda h, i: (0, XA_HEADS + h))
    dq, dk, dv, dgq, dgk = pl.pallas_call(
        body, grid=(XA_HEADS, nt),
        in_specs=[pl.BlockSpec((tm, XA_DIM), lambda h, i: (i, db + h)),
                  pl.BlockSpec((tm, XA_DIM), lambda h, i: (i, qb + h)), kblk, vblk, vec, vec],
        out_specs=(pl.BlockSpec((tm, XA_DIM), lambda h, i: (i, h)), kblk, kblk, vec, vec),
        out_shape=(jax.ShapeDtypeStruct((S, XA_WIDTH), BF16), jax.ShapeDtypeStruct((n_mem, XA_WIDTH), F32),
                   jax.ShapeDtypeStruct((n_mem, XA_WIDTH), F32), jax.ShapeDtypeStruct((1, XA_DIM), F32),
                   jax.ShapeDtypeStruct((1, XA_DIM), F32)),
        scratch_shapes=[pltpu.VMEM((n_mem, XA_DIM), F32)],
        compiler_params=_cp(("arbitrary", "arbitrary")), name=name)(
            dcat, proj, kv, kv, gq.reshape(1, XA_DIM), gk.reshape(1, XA_DIM))
    return dq, jnp.concatenate([dk, dv], axis=1), dgq, dgk


SB_TQ = 256
SB_TK = 256
SB_HEADS = 24


SB_PAIR = 2
SB_PW = SB_PAIR * HEAD_DIM


def _hdot(a, b, dims, dot=None):
    dot = dot or _bdot
    n = a.shape[0] if a.ndim == 3 else b.shape[0]
    return jnp.stack([dot(a[i] if a.ndim == 3 else a, b[i] if b.ndim == 3 else b, dims) for i in range(n)])


def _sb_tile(qi, kj, t0, s0, masked):
    z = _hdot(qi, kj, NT)
    sp = _softplus(z)
    ls = z - sp
    if not masked:
        return -sp, ls, None
    mask = (s0 + _iota2(z.shape[1:], 1)) < (t0 + _iota2(z.shape[1:], 0))
    return jnp.where(mask, -sp, 0.0), ls, mask


def _dot2(x, tri):
    hi = x.astype(BF16)
    lo = (x - hi.astype(F32)).astype(BF16)
    plain = lambda u, v, dims: lax.dot_general(u, v, (dims, ((), ())), preferred_element_type=F32)
    return _hdot(hi, tri, NN, plain) + _hdot(lo, tri, NN, plain)


def _sb_heads(ref, rows=slice(None)):
    return jnp.stack([ref[rows, hh * HEAD_DIM:(hh + 1) * HEAD_DIM] for hh in range(SB_PAIR)])


def _sb_fwd(proj, gq, gk, *, name):
    S = proj.shape[0]
    tq, tk = min(SB_TQ, S), min(SB_TK, S)
    nq = S // tq
    scale = HEAD_DIM ** -0.5

    def body(q_ref, k_ref, v_ref, gq_ref, gk_ref, o_ref, tot_ref, qn_s, kn_s, v_s):
        q = _sb_heads(q_ref)
        k = _sb_heads(k_ref)
        qn_s[...] = (q * lax.rsqrt(jnp.mean(q * q, axis=-1, keepdims=True) + EPS) * (gq_ref[...] * scale)).astype(BF16)
        kn_s[...] = (k * lax.rsqrt(jnp.mean(k * k, axis=-1, keepdims=True) + EPS) * gk_ref[...]).astype(BF16)
        v_s[...] = _sb_heads(v_ref).astype(BF16)
        after = (_iota2((tk, tk), 0) > _iota2((tk, tk), 1)).astype(BF16)

        def qblock(i, _):
            rows = pl.ds(pl.multiple_of(i * tq, tq), tq)
            qi = qn_s[:, rows, :]
            jd = (i * tq) // tk

            def tile(j, acc, run, masked):
                cols = pl.ds(pl.multiple_of(j * tk, tk), tk)
                lr, ls, mask = _sb_tile(qi, kn_s[:, cols, :], i * tq, j * tk, masked)
                later = _dot2(lr, after) + run
                a = jnp.exp(ls + later)
                if masked:
                    a = jnp.where(mask, a, 0.0)
                acc = acc + _hdot(a, v_s[:, cols, :], NN)
                return acc, run + jnp.sum(lr, axis=-1, keepdims=True)

            acc, run = tile(jd, jnp.zeros((SB_PAIR, tq, HEAD_DIM), F32), jnp.zeros((SB_PAIR, tq, 1), F32), True)
            acc, run = lax.fori_loop(0, jd, lambda jj, c: tile(jd - 1 - jj, c[0], c[1], False), (acc, run))
            tot = run + jnp.zeros((SB_PAIR, tq, HEAD_DIM), F32)
            for hh in range(SB_PAIR):
                o_ref[rows, hh * HEAD_DIM:(hh + 1) * HEAD_DIM] = acc[hh]
                tot_ref[rows, hh * HEAD_DIM:(hh + 1) * HEAD_DIM] = tot[hh]
            return 0

        lax.fori_loop(0, nq, qblock, 0)

    npair = SB_HEADS // SB_PAIR
    vec = pl.BlockSpec((1, HEAD_DIM), lambda h: (0, 0))
    hb = lambda off: pl.BlockSpec((S, SB_PW), lambda h: (0, off + h), pipeline_mode=pl.Buffered(1))
    return pl.pallas_call(
        body, grid=(npair,), in_specs=[hb(0), hb(npair), hb(2 * npair), vec, vec],
        out_specs=(hb(0), hb(0)), out_shape=(jax.ShapeDtypeStruct((S, MIX_WIDTH), F32),) * 2,
        scratch_shapes=[pltpu.VMEM((SB_PAIR, S, HEAD_DIM), BF16)] * 3,
        compiler_params=_cp(("parallel",)), name=name)(proj, proj, proj, gq.reshape(1, HEAD_DIM), gk.reshape(1, HEAD_DIM))


def _sb_bwd(dmix, tot, proj, gq, gk, *, name):
    S = proj.shape[0]
    tq, tk = min(SB_TQ, S), min(SB_TK, S)
    nq = S // tq
    scale = HEAD_DIM ** -0.5

    def body(do_ref, o_ref, q_ref, k_ref, v_ref, gq_ref, gk_ref, dq_ref, dk_ref, dv_ref, dgq_ref, dgk_ref,
             qn_s, kn_s, v_s, dkn_s, dqn_s, dv_s):
        h = pl.program_id(0)
        q = _sb_heads(q_ref)
        k = _sb_heads(k_ref)
        rq = lax.rsqrt(jnp.mean(q * q, axis=-1, keepdims=True) + EPS)
        rk = lax.rsqrt(jnp.mean(k * k, axis=-1, keepdims=True) + EPS)
        gqv = gq_ref[...]
        gkv = gk_ref[...]
        qn_s[...] = (q * rq * (gqv * scale)).astype(BF16)
        kn_s[...] = (k * rk * gkv).astype(BF16)
        v_s[...] = _sb_heads(v_ref).astype(BF16)
        dkn_s[...] = jnp.zeros_like(dkn_s)
        dv_s[...] = jnp.zeros_like(dv_s)
        r_i = _iota2((tk, tk), 0)
        c_i = _iota2((tk, tk), 1)
        upto = (r_i <= c_i).astype(BF16)
        before = (r_i < c_i).astype(BF16)

        def qblock(i, _):
            rows = pl.ds(pl.multiple_of(i * tq, tq), tq)
            qi = qn_s[:, rows, :]
            doi = _sb_heads(do_ref, rows).astype(BF16)
            tot_i = jnp.max(_sb_heads(o_ref, rows), axis=-1, keepdims=True)
            jd = (i * tq) // tk

            def tile(j, dqn, run, run_b, masked):
                cols = pl.ds(pl.multiple_of(j * tk, tk), tk)
                kj = kn_s[:, cols, :]
                lr, ls, mask = _sb_tile(qi, kj, i * tq, j * tk, masked)
                later = tot_i - (_dot2(lr, upto) + run)
                a = jnp.exp(ls + later)
                if masked:
                    a = jnp.where(mask, a, 0.0)
                b = _hdot(doi, v_s[:, cols, :], NT) * a
                cum = _dot2(b, before) + run_b
                beta = jnp.exp(ls)
                dz = b * (1.0 - beta) - cum * beta
                if masked:
                    dz = jnp.where(mask, dz, 0.0)
                dzb = dz.astype(BF16)
                dv_s[:, cols, :] += _hdot(a, doi, TN)
                dkn_s[:, cols, :] += _hdot(dzb, qi, TN)
                dqn = dqn + _hdot(dzb, kj, NN)
                return dqn, run + jnp.sum(lr, axis=-1, keepdims=True), run_b + jnp.sum(b, axis=-1, keepdims=True)

            zero1 = jnp.zeros((SB_PAIR, tq, 1), F32)
            carry = lax.fori_loop(0, jd, lambda j, c: tile(j, c[0], c[1], c[2], False),
                                  (jnp.zeros((SB_PAIR, tq, HEAD_DIM), F32), zero1, zero1))
            dqn, _, _ = tile(jd, carry[0], carry[1], carry[2], True)
            dqn_s[:, rows, :] = dqn * scale
            return 0

        lax.fori_loop(0, nq, qblock, 0)

        @pl.when(h == 0)
        def _():
            dgq_ref[...] = jnp.zeros_like(dgq_ref)
            dgk_ref[...] = jnp.zeros_like(dgk_ref)

        heads_sum = lambda z: jnp.sum(jnp.sum(z, axis=1, keepdims=True), axis=0)
        dqn = dqn_s[...]
        qh = q * rq
        dgq_ref[...] += heads_sum(dqn * qh)
        dy = dqn * gqv
        dq = (rq * (dy - qh * jnp.mean(dy * qh, axis=-1, keepdims=True))).astype(BF16)
        dkn = dkn_s[...]
        kh = k * rk
        dgk_ref[...] += heads_sum(dkn * kh)
        dyk = dkn * gkv
        dk = (rk * (dyk - kh * jnp.mean(dyk * kh, axis=-1, keepdims=True))).astype(BF16)
        dv = dv_s[...].astype(BF16)
        for hh in range(SB_PAIR):
            lanes = slice(hh * HEAD_DIM, (hh + 1) * HEAD_DIM)
            dq_ref[:, lanes] = dq[hh]
            dk_ref[:, lanes] = dk[hh]
            dv_ref[:, lanes] = dv[hh]

    npair = SB_HEADS // SB_PAIR
    vec = pl.BlockSpec((1, HEAD_DIM), lambda h: (0, 0))
    hb = lambda off: pl.BlockSpec((S, SB_PW), lambda h: (0, off + h), pipeline_mode=pl.Buffered(1))
    dq, dk, dv, dgq, dgk = pl.pallas_call(
        body, grid=(npair,),
        in_specs=[hb(0), hb(0), hb(0), hb(npair), hb(2 * npair), vec, vec],
        out_specs=(hb(0), hb(0), hb(0), vec, vec),
        out_shape=(jax.ShapeDtypeStruct((S, MIX_WIDTH), BF16),) * 3 + (jax.ShapeDtypeStruct((1, HEAD_DIM), F32),) * 2,
        scratch_shapes=[pltpu.VMEM((SB_PAIR, S, HEAD_DIM), BF16)] * 3 + [pltpu.VMEM((SB_PAIR, S, HEAD_DIM), F32)] * 3,
        compiler_params=_cp(("arbitrary",)), name=name)(
            dmix, tot, proj, proj, proj, gq.reshape(1, HEAD_DIM), gk.reshape(1, HEAD_DIM))
    return [dq, dk, dv], dgq, dgk


def _shift_down(x, k):
    if k == 0:
        return x
    r = pltpu.roll(x, k, 0)
    return jnp.where(_iota2(x.shape, 0) >= k, r, 0.0)


def _shift_up(x, k):
    if k == 0:
        return x
    n = x.shape[0]
    r = pltpu.roll(x, n - k, 0)
    return jnp.where(_iota2(x.shape, 0) < n - k, r, 0.0)


def _conv(x, w):
    c = w[DN_CONV - 1] * x
    for k in range(1, DN_CONV):
        c = c + w[DN_CONV - 1 - k] * _shift_down(x, k)
    return c


def _dn_pre_fwd(proj, conv_w, col0, ncols, *, l2, scale, name):
    S = proj.shape[0]
    cb = col0 // HEAD_DIM

    def body(x_ref, w_ref, o_ref):
        c = _conv(x_ref[...], [w_ref[k:k + 1, :] for k in range(DN_CONV)])
        a = c * _sigmoid(c)
        if l2:
            a = a * (lax.rsqrt(jnp.sum(a * a, axis=-1, keepdims=True) + EPS) * scale)
        o_ref[...] = a

    return pl.pallas_call(
        body, grid=(ncols // HEAD_DIM,),
        in_specs=[pl.BlockSpec((S, HEAD_DIM), lambda j: (0, cb + j)), pl.BlockSpec((DN_CONV, HEAD_DIM), lambda j: (0, cb + j))],
        out_specs=pl.BlockSpec((S, HEAD_DIM), lambda j: (0, j)), out_shape=jax.ShapeDtypeStruct((S, ncols), F32),
        compiler_params=_cp(("parallel",)), name=name)(proj, conv_w)


def _dn_pre_bwd(dout, proj, conv_w, col0, ncols, *, l2, scale, name):
    S = proj.shape[0]
    cb = col0 // HEAD_DIM
    dw_in = HEAD_DIM

    def body(d_ref, x_ref, w_ref, dx_ref, dw_ref):
        x = x_ref[...]
        w = [w_ref[k:k + 1, :] for k in range(DN_CONV)]
        c = _conv(x, w)
        sg = _sigmoid(c)
        a = c * sg
        d = d_ref[...]
        if l2:
            r = lax.rsqrt(jnp.sum(a * a, axis=-1, keepdims=True) + EPS)
            y = a * r
            d = d * scale
            d = r * (d - y * jnp.sum(d * y, axis=-1, keepdims=True))
        dc = d * sg * (1.0 + c * (1.0 - sg))
        dx = w[DN_CONV - 1] * dc
        for k in range(1, DN_CONV):
            dx = dx + w[DN_CONV - 1 - k] * _shift_up(dc, k)
        dx_ref[...] = dx.astype(BF16)
        for k in range(DN_CONV):
            dw_ref[3 - k:4 - k, :] = jnp.sum(dc * _shift_down(x, k), axis=0, keepdims=True)

    return pl.pallas_call(
        body, grid=(ncols // HEAD_DIM,),
        in_specs=[pl.BlockSpec((S, dw_in), lambda j: (0, j)), pl.BlockSpec((S, HEAD_DIM), lambda j: (0, cb + j)),
                  pl.BlockSpec((DN_CONV, HEAD_DIM), lambda j: (0, cb + j))],
        out_specs=(pl.BlockSpec((S, HEAD_DIM), lambda j: (0, j)), pl.BlockSpec((DN_CONV, HEAD_DIM), lambda j: (0, j))),
        out_shape=(jax.ShapeDtypeStruct((S, ncols), BF16), jax.ShapeDtypeStruct((DN_CONV, ncols), F32)),
        compiler_params=_cp(("parallel",)), name=name)(dout, proj, conv_w)


def _dn_ab_fwd(proj, a_log, dt_bias, *, name, tm=512):
    S = proj.shape[0]
    tm = min(tm, S)
    ab = P0_AB // LANE

    def body(a_ref, b_ref, al_ref, dt_ref, g_ref, be_ref):
        g_ref[...] = -jnp.exp(al_ref[...]) * _softplus(a_ref[...] + dt_ref[...])
        be_ref[...] = _sigmoid(b_ref[...])

    vec = pl.BlockSpec((1, LANE), lambda i: (0, 0))
    out = pl.BlockSpec((tm, LANE), lambda i: (i, 0))
    return pl.pallas_call(
        body, grid=(S // tm,),
        in_specs=[pl.BlockSpec((tm, LANE), lambda i: (i, ab)), pl.BlockSpec((tm, LANE), lambda i: (i, ab + 1)), vec, vec],
        out_specs=(out, out), out_shape=(jax.ShapeDtypeStruct((S, LANE), F32),) * 2,
        compiler_params=_cp(("parallel",)), name=name)(proj, proj, a_log, dt_bias)


def _dn_ab_bwd(dg, dbeta, proj, a_log, dt_bias, *, name, tm=512):
    S = proj.shape[0]
    tm = min(tm, S)
    ab = P0_AB // LANE

    def body(dg_ref, db_ref, a_ref, b_ref, al_ref, dt_ref, dab_ref, dal_ref, ddt_ref):
        i = pl.program_id(0)
        ea = jnp.exp(al_ref[...])
        u = a_ref[...] + dt_ref[...]
        dgv = dg_ref[...]
        da = dgv * (-ea) * _sigmoid(u)
        be = _sigmoid(b_ref[...])
        dab_ref[:, 0:LANE] = da.astype(BF16)
        dab_ref[:, LANE:2 * LANE] = (db_ref[...] * be * (1.0 - be)).astype(BF16)
        dab_ref[:, 2 * LANE:] = jnp.zeros((tm, 2 * LANE), BF16)

        @pl.when(i == 0)
        def _():
            dal_ref[...] = jnp.zeros_like(dal_ref)
            ddt_ref[...] = jnp.zeros_like(ddt_ref)

        dal_ref[...] += jnp.sum(dgv * (-ea) * _softplus(u), axis=0, keepdims=True)
        ddt_ref[...] += jnp.sum(da, axis=0, keepdims=True)

    vec = pl.BlockSpec((1, LANE), lambda i: (0, 0))
    row = pl.BlockSpec((tm, LANE), lambda i: (i, 0))
    return pl.pallas_call(
        body, grid=(S // tm,),
        in_specs=[row, row, pl.BlockSpec((tm, LANE), lambda i: (i, ab)), pl.BlockSpec((tm, LANE), lambda i: (i, ab + 1)), vec, vec],
        out_specs=(pl.BlockSpec((tm, 4 * LANE), lambda i: (i, 0)), vec, vec),
        out_shape=(jax.ShapeDtypeStruct((S, 4 * LANE), BF16), jax.ShapeDtypeStruct((1, LANE), F32),
                   jax.ShapeDtypeStruct((1, LANE), F32)),
        compiler_params=_cp(("arbitrary",)), name=name)(dg, dbeta, proj, proj, a_log, dt_bias)


def _dot3(a, b):
    ah = a.astype(BF16)
    al = (a - ah.astype(F32)).astype(BF16)
    bh = b.astype(BF16)
    bl = (b - bh.astype(F32)).astype(BF16)
    d = lambda u, v: lax.dot_general(u, v, (NN, ((), ())), preferred_element_type=F32)
    return d(ah, bh) + (d(ah, bl) + d(al, bh))


DN_PAIR = 4
DN_QK = DN_PAIR // 2


def _dn_qk_heads(ref, rows):
    return jnp.stack([ref[rows, (hh // 2) * HEAD_DIM:(hh // 2 + 1) * HEAD_DIM] for hh in range(DN_PAIR)])


def _dn_big(shape, imap):
    return pl.BlockSpec(shape, imap, pipeline_mode=pl.Buffered(1))


_pdot = _hdot


def _tri_inverse(a):
    eye = (_iota2((CH, CH), 0) == _iota2((CH, CH), 1)).astype(F32)
    d3 = lambda u, v: jnp.stack([_dot3(u[i], v[i]) for i in range(DN_PAIR)])
    t = eye - a
    x = d3(a, a)
    n = 2
    while True:
        t = t + d3(t, x)
        n *= 2
        if n >= CH:
            break
        x = d3(x, x)
    return t


def _pick_col(m, n):
    return jnp.sum(jnp.where(_iota2(m.shape, 2) == n, m, 0.0), axis=2, keepdims=True)


def _dn_chunk_common(kk, qk, gc_c, gc_r, be_c):
    r_i = _iota2((CH, CH), 0)
    c_i = _iota2((CH, CH), 1)
    incl = r_i >= c_i
    strict = r_i > c_i
    dec = jnp.exp(jnp.where(incl, gc_c - gc_r, -1e30))
    e = jnp.exp(gc_c)
    gl = jnp.sum(jnp.where(_iota2((1, CH), 1) == CH - 1, gc_r, 0.0), axis=-1, keepdims=True)
    kds = jnp.exp(gl - gc_c)
    cd = jnp.exp(gl)
    a = jnp.where(strict, be_c * kk * dec, 0.0)
    p = qk * dec
    return dict(incl=incl, strict=strict, dec=dec, e=e, kds=kds, cd=cd, kk=kk, a=a, qk=qk, p=p)


def _dn_decay_tables(g_ref, b_ref, gcr, gcc, bcc):
    r_i = _iota2((CH, CH), 0)
    c_i = _iota2((CH, CH), 1)
    lc = (r_i >= c_i).astype(F32)
    eye = (r_i == c_i).astype(F32)
    for hh in range(DN_PAIR):
        g_rows_v = g_ref[hh]
        gcr[hh] = _fdot(g_rows_v, lc, NT)
        gcc[hh] = _fdot(lc, g_rows_v, NT)
        bcc[hh] = _fdot(eye, b_ref[hh], NT)
    return lc


def _dn_core_fwd(qn, kn, vc, g_rows, b_rows, out_g, *, name):
    S = qn.shape[0]
    nc = S // CH

    def body(q_ref, k_ref, v_ref, g_ref, b_ref, og_ref, o_ref, st_ref, t_ref, gcr, gcc, bcc):
        _dn_decay_tables(g_ref, b_ref, gcr, gcc, bcc)
        ogv = og_ref[...]

        def chunk(n, states):
            rows = pl.ds(pl.multiple_of(n * CH, CH), CH)
            q = _dn_qk_heads(q_ref, rows)
            k = _dn_qk_heads(k_ref, rows)
            kk = _pdot(k, k, NT)
            qk = _pdot(q, k, NT)
            v = jnp.stack([v_ref[rows, hh * HEAD_DIM:(hh + 1) * HEAD_DIM] for hh in range(DN_PAIR)])
            gc_c = _pick_col(gcc[...], n)
            be_c = _pick_col(bcc[...], n)
            gc_r = gcr[:, pl.ds(n, 1), :]
            c = _dn_chunk_common(kk, qk, gc_c, gc_r, be_c)
            t = _tri_inverse(c["a"])
            u0 = _pdot(t, be_c * v, NN)
            w = _pdot(t, (be_c * c["e"]) * k, NN)
            u = u0 - _pdot(w, states, NN)
            o = _pdot(c["e"] * q, states, NN) + _pdot(c["p"], u, NN)
            on = o * lax.rsqrt(jnp.mean(o * o, axis=-1, keepdims=True) + EPS) * ogv
            for hh in range(DN_PAIR):
                st_ref[hh, n] = states[hh]
                t_ref[hh, n] = t[hh]
                o_ref[rows, hh * HEAD_DIM:(hh + 1) * HEAD_DIM] = on[hh]
            return c["cd"] * states + _pdot(c["kds"] * k, u, TN)

        lax.fori_loop(0, nc, chunk, jnp.zeros((DN_PAIR, HEAD_DIM, HEAD_DIM), F32))

    qk_spec = pl.BlockSpec((S, DN_QK * HEAD_DIM), lambda h: (0, h))
    v_in = pl.BlockSpec((S, DN_PAIR * HEAD_DIM), lambda h: (0, h))
    v_spec = _dn_big((S, DN_PAIR * HEAD_DIM), lambda h: (0, h))
    rows_spec = pl.BlockSpec((DN_PAIR, LANE, CH), lambda h: (h, 0, 0))
    return pl.pallas_call(
        body, grid=(DN_V_HEADS // DN_PAIR,),
        in_specs=[qk_spec, qk_spec, v_in, rows_spec, rows_spec, pl.BlockSpec((1, HEAD_DIM), lambda h: (0, 0))],
        out_specs=(v_spec, _dn_big((DN_PAIR, nc, HEAD_DIM, HEAD_DIM), lambda h: (h, 0, 0, 0)),
                   _dn_big((DN_PAIR, nc, CH, CH), lambda h: (h, 0, 0, 0))),
        out_shape=(jax.ShapeDtypeStruct((S, MIX_WIDTH), F32), jax.ShapeDtypeStruct((DN_V_HEADS, nc, HEAD_DIM, HEAD_DIM), F32),
                   jax.ShapeDtypeStruct((DN_V_HEADS, nc, CH, CH), F32)),
        scratch_shapes=[pltpu.VMEM((DN_PAIR, LANE, CH), F32), pltpu.VMEM((DN_PAIR, CH, LANE), F32),
                        pltpu.VMEM((DN_PAIR, CH, LANE), F32)],
        compiler_params=_cp(("parallel",)), name=name)(qn, kn, vc, g_rows, b_rows, out_g.reshape(1, HEAD_DIM))


def _dn_chunk_bwd(q, k, v, kk, qk, state, t, gc_c, gc_r, be_c, don, ogv, ds_next):
    ones = jnp.ones((CH, LANE), F32)
    last_row = _iota2((CH, 1), 0) == CH - 1
    rowsum = lambda z: jnp.sum(z, axis=-1, keepdims=True)
    colsum = lambda z: jnp.sum(z, axis=-2, keepdims=True)
    c = _dn_chunk_common(kk, qk, gc_c, gc_r, be_c)
    e, kds, cd, dec, a, p = c["e"], c["kds"], c["cd"], c["dec"], c["a"], c["p"]
    vb = be_c * v
    kbe = (be_c * e) * k
    u0 = _pdot(t, vb, NN)
    w = _pdot(t, kbe, NN)
    u = u0 - _pdot(w, state, NN)
    qd = e * q
    kd = kds * k
    o = _pdot(qd, state, NN) + _pdot(p, u, NN)
    r = lax.rsqrt(jnp.mean(o * o, axis=-1, keepdims=True) + EPS)
    y = o * r
    dog = colsum(don * y)
    dy = don * ogv
    d_o = r * (dy - y * jnp.mean(dy * y, axis=-1, keepdims=True))
    du = _pdot(p, d_o, TN) + _pdot(kd, ds_next, NN)
    dqd = _pdot(d_o, state, NT)
    dstate = _pdot(qd, d_o, TN) + cd * ds_next - _pdot(w, du, TN)
    dcd = colsum(rowsum(ds_next * state))
    dkd = _pdot(u, ds_next, NT)
    dw = -_pdot(du, state, NT)
    dvb = _pdot(t, du, TN)
    dkbe = _pdot(t, dw, TN)
    da = -jnp.where(c["strict"], _pdot(dvb, u0, NT) + _pdot(dkbe, w, NT), 0.0)
    dp = jnp.where(c["incl"], _pdot(d_o, u, NT), 0.0)
    gmat = da * a + dp * p
    dad = da * dec
    x = be_c * dad
    dpd = dp * dec
    dk = _pdot(x, k, NN) + _pdot(x, k, TN) + _pdot(dpd, q, TN)
    dq = _pdot(dpd, k, NN) + e * dqd
    dbe = rowsum(dad * c["kk"])
    dgc = rowsum(gmat) + rowsum(dqd * q) * e
    rk = rowsum(dkd * k) * kds
    dk = dk + kds * dkd
    dgc = dgc - rk
    dgl = colsum(rk) + dcd * cd
    sk = rowsum(dkbe * k)
    dk = dk + (be_c * e) * dkbe
    dbe = dbe + sk * e + rowsum(dvb * v)
    dgc = dgc + sk * be_c * e
    dgc = dgc + jnp.where(last_row, dgl, 0.0)
    dgc = dgc - _pdot(gmat, ones, TN, dot=_fdot)
    return dq, dk, be_c * dvb, dgc, dbe, dog, dstate


def _dn_core_bwd(dmix, qn, kn, vc, g_rows, b_rows, out_g, states, tinv, *, name):
    S = qn.shape[0]
    nc = S // CH

    def body(do_ref, q_ref, k_ref, v_ref, g_ref, b_ref, og_ref, st_ref, t_ref,
             dq_ref, dk_ref, dv_ref, dg_ref, db_ref, dog_ref, gcr, gcc, bcc, dgc_acc):
        h = pl.program_id(0)
        lc = _dn_decay_tables(g_ref, b_ref, gcr, gcc, bcc)
        ogv = og_ref[...]
        dgc_acc[...] = jnp.zeros_like(dgc_acc)
        db_ref[...] = jnp.zeros_like(db_ref)
        lane_n = _iota2((CH, LANE), 1)

        @pl.when(h == 0)
        def _():
            dog_ref[...] = jnp.zeros_like(dog_ref)

        def chunk(m, carry):
            ds_nexts, dog = carry
            n = nc - 1 - m
            rows = pl.ds(pl.multiple_of(n * CH, CH), CH)
            q = _dn_qk_heads(q_ref, rows)
            k = _dn_qk_heads(k_ref, rows)
            kk = _pdot(k, k, NT)
            qk = _pdot(q, k, NT)
            heads = lambda ref: jnp.stack([ref[rows, hh * HEAD_DIM:(hh + 1) * HEAD_DIM] for hh in range(DN_PAIR)])
            state = jnp.stack([st_ref[hh, n] for hh in range(DN_PAIR)])
            t = jnp.stack([t_ref[hh, n] for hh in range(DN_PAIR)])
            dq, dk, dv, dgc, dbe, dog_h, dstate = _dn_chunk_bwd(
                q, k, heads(v_ref), kk, qk, state, t, _pick_col(gcc[...], n), gcr[:, pl.ds(n, 1), :],
                _pick_col(bcc[...], n), heads(do_ref), ogv, ds_nexts)
            for hh in range(DN_PAIR):
                dv_ref[rows, hh * HEAD_DIM:(hh + 1) * HEAD_DIM] = dv[hh]
            dgc_acc[...] = jnp.where(lane_n == n, dgc, dgc_acc[...])
            db_ref[...] = jnp.where(lane_n == n, dbe, db_ref[...])
            for i in range(DN_QK):
                dq_ref[rows, i * HEAD_DIM:(i + 1) * HEAD_DIM] = dq[2 * i] + dq[2 * i + 1]
                dk_ref[rows, i * HEAD_DIM:(i + 1) * HEAD_DIM] = dk[2 * i] + dk[2 * i + 1]
            return dstate, dog + jnp.sum(dog_h, axis=0)

        _, dog = lax.fori_loop(0, nc, chunk, (jnp.zeros((DN_PAIR, HEAD_DIM, HEAD_DIM), F32), jnp.zeros((1, HEAD_DIM), F32)))
        dog_ref[...] += dog
        for hh in range(DN_PAIR):
            dg_ref[hh] = _fdot(lc, dgc_acc[hh], TN)

    qk_spec = _dn_big((S, DN_QK * HEAD_DIM), lambda h: (0, h))
    v_spec = _dn_big((S, DN_PAIR * HEAD_DIM), lambda h: (0, h))
    rows_spec = pl.BlockSpec((DN_PAIR, LANE, CH), lambda h: (h, 0, 0))
    cols_spec = pl.BlockSpec((DN_PAIR, CH, LANE), lambda h: (h, 0, 0))
    vec = pl.BlockSpec((1, HEAD_DIM), lambda h: (0, 0))
    qk_out = jax.ShapeDtypeStruct((S, DN_QK_WIDTH), F32)
    return pl.pallas_call(
        body, grid=(DN_V_HEADS // DN_PAIR,),
        in_specs=[v_spec, qk_spec, qk_spec, v_spec, rows_spec, rows_spec, vec,
                  _dn_big((DN_PAIR, nc, HEAD_DIM, HEAD_DIM), lambda h: (h, 0, 0, 0)),
                  _dn_big((DN_PAIR, nc, CH, CH), lambda h: (h, 0, 0, 0))],
        out_specs=(qk_spec, qk_spec, v_spec, cols_spec, cols_spec, vec),
        out_shape=(qk_out, qk_out, jax.ShapeDtypeStruct((S, MIX_WIDTH), F32), jax.ShapeDtypeStruct((DN_V_HEADS, CH, LANE), F32),
                   jax.ShapeDtypeStruct((DN_V_HEADS, CH, LANE), F32), jax.ShapeDtypeStruct((1, HEAD_DIM), F32)),
        scratch_shapes=[pltpu.VMEM((DN_PAIR, LANE, CH), F32), pltpu.VMEM((DN_PAIR, CH, LANE), F32),
                        pltpu.VMEM((DN_PAIR, CH, LANE), F32), pltpu.VMEM((DN_PAIR, CH, LANE), F32)],
        compiler_params=_cp(("arbitrary",)), name=name)(
            dmix, qn, kn, vc, g_rows, b_rows, out_g.reshape(1, HEAD_DIM), states, tinv)


def _rows_form(x, nc):
    t = x[:, :DN_V_HEADS].T.reshape(DN_V_HEADS, nc, CH)
    return jnp.pad(t, ((0, 0), (0, LANE - nc), (0, 0)))


def _cols_to_nat(x, nc):
    t = jnp.transpose(x[:, :, :nc], (2, 1, 0)).reshape(nc * CH, DN_V_HEADS)
    return jnp.pad(t, ((0, 0), (0, LANE - DN_V_HEADS)))


_C_QKV = 2 * DN_QK_WIDTH + MIX_WIDTH


def _padded_pieces(lo, hi):
    a0, b0, x0 = _C_QKV, _C_QKV + DN_V_HEADS, _C_QKV + 2 * DN_V_HEADS
    out = []
    for t0, t1, shift in ((0, a0, 0), (a0, b0, P0_AB - a0), (b0, x0, P0_AB + LANE - b0), (x0, DN_PROJ, a0 - x0)):
        s, e = max(lo, t0), min(hi, t1)
        if s < e:
            out.append((s + shift, e + shift))
    return out


def _shard_pieces(s):
    return _padded_pieces(s * P0_SHARD, (s + 1) * P0_SHARD)


def _shard_runs(s):
    return [(lo // LANE, -(-hi // LANE)) for lo, hi in _shard_pieces(s)]


WIN_COLS = LANE * max(sum(b - a for a, b in _shard_runs(s)) for s in range(N_CHIPS))


def _pack_own_shard(s, shard):
    zeros = lambda n: jnp.zeros((shard.shape[0], n), shard.dtype)
    out, t = [], 0
    for (lo, hi), (b0, b1) in zip(_shard_pieces(s), _shard_runs(s)):
        out += [zeros(lo - b0 * LANE), shard[:, t:t + hi - lo], zeros(b1 * LANE - hi)]
        t += hi - lo
    out.append(zeros(WIN_COLS - LANE * sum(b - a for a, b in _shard_runs(s))))
    return jnp.concatenate([o for o in out if o.shape[1]], axis=1)


def _unpack_own_shard(s, win):
    out, off = [], 0
    for (lo, hi), (b0, b1) in zip(_shard_pieces(s), _shard_runs(s)):
        start = off + lo - b0 * LANE
        out.append(win[:, start:start + hi - lo])
        off += (b1 - b0) * LANE
    return jnp.concatenate(out, axis=1)


def _windows_to_padded(wins):
    src = {}
    for s in range(N_CHIPS):
        off = 0
        for b0, b1 in _shard_runs(s):
            for b in range(b0, b1):
                src.setdefault(b, []).append((s, off + (b - b0) * LANE))
            off += (b1 - b0) * LANE
    out, b = [], 0
    while b < P0 // LANE:
        if b not in src:
            out.append(jnp.zeros((wins.shape[1], LANE), wins.dtype))
            b += 1
        elif len(src[b]) > 1:
            out.append(sum(wins[s][:, o:o + LANE] for s, o in src[b]))
            b += 1
        else:
            (s, o), n = src[b][0], 1
            while src.get(b + n) == [(s, o + n * LANE)]:
                n += 1
            out.append(wins[s][:, o:o + n * LANE])
            b += n
    return jnp.concatenate(out, axis=1)


def _padded_to_window(s, g):
    out = [g[:, b0 * LANE:b1 * LANE] for b0, b1 in _shard_runs(s)]
    rest = WIN_COLS - sum(o.shape[1] for o in out)
    return jnp.concatenate(out + ([jnp.zeros((g.shape[0], rest), g.dtype)] if rest else []), axis=1)


def _pad_lane(v):
    v = v.reshape(1, -1)
    return jnp.pad(v, ((0, 0), (0, LANE - v.shape[1])))


SLOT1 = SB_PROJ // N_CHIPS
MM_TN = 512


def _local_step(x, mem, target, norm_g, mem_norm_g, xa_q_g, xa_k_g, w_in0, conv_w, a_log, dt_bias, out_g, sb_q_g, sb_k_g,
                late_weights, early_grads, grads_swapped):
    S = x.shape[0]
    nc = S // CH
    al = _pad_lane(a_log)
    dtb = _pad_lane(dt_bias)
    q_scale = HEAD_DIM ** -0.5
    tiles1 = SLOT1 // MM_TN

    kv_rhs = lambda l: pl.BlockSpec((N_CHIPS, None, D_MODEL // N_CHIPS, MM_TN), lambda i, j: (0, l, 0, j))
    kv_rhs_t = lambda l: pl.BlockSpec((None, None, D_MODEL // N_CHIPS, 2 * XA_WIDTH), lambda i, j: (j, l, 0, 0))
    out_rhs = lambda l: pl.BlockSpec((N_CHIPS, None, INNER // N_CHIPS, MM_TN), lambda i, j: (0, l, 0, j))
    out_rhs_t = lambda l: pl.BlockSpec((None, None, MM_TN, D_MODEL), lambda i, j: (j // 2, l, j % 2, 0))
    in1_rhs = pl.BlockSpec((None, 2, D_MODEL // 2, MM_TN), lambda i, j: (j // tiles1, 0, 0, j % tiles1))
    in1_rhs_t = pl.BlockSpec((None, None, D_MODEL // 2, MM_TN), lambda i, j, k: (k // tiles1, j, 0, k % tiles1))
    slot_rows = lambda rows: dict(
        tm=rows, o_spec=pl.BlockSpec((None, rows, MM_TN), lambda i, j: (i, 0, j)),
        o_shape=jax.ShapeDtypeStruct((N_CHIPS, rows, 2 * XA_WIDTH), BF16))
    in1_out = dict(tm=D_MODEL // 2, o_spec=pl.BlockSpec((None, None, D_MODEL // 2, MM_TN),
                                                        lambda i, j: (j // tiles1, i, 0, j % tiles1)),
                   o_shape=jax.ShapeDtypeStruct((N_CHIPS, 2, D_MODEL // 2, SLOT1), BF16))

    h0 = _rmsnorm_fwd(x, norm_g[0], name="norm0")
    proj0 = _matmul(h0, w_in0, name="proj0")
    qn = _dn_pre_fwd(proj0, conv_w, 0, DN_QK_WIDTH, l2=True, scale=q_scale, name="dn_pre_q")
    kn = _dn_pre_fwd(proj0, conv_w, DN_QK_WIDTH, DN_QK_WIDTH, l2=True, scale=1.0, name="dn_pre_k")
    vc = _dn_pre_fwd(proj0, conv_w, 2 * DN_QK_WIDTH, MIX_WIDTH, l2=False, scale=1.0, name="dn_pre_v")
    g_nat, b_nat = _dn_ab_fwd(proj0, al, dtb, name="dn_ab")
    g_rows = _rows_form(g_nat, nc)
    b_rows = _rows_form(b_nat, nc)
    mix0, states, tinv = _dn_core_fwd(qn, kn, vc, g_rows, b_rows, out_g, name="dn_core")
    w_kv, w_out, w_in1 = late_weights(mix0)
    mem_n = _rmsnorm_fwd(mem, mem_norm_g, name="mem_norm")
    kv = [_matmul(mem_n, w_kv, n=2 * XA_WIDTH, tn=MM_TN, b_spec=kv_rhs(l), name=f"kv{l}") for l in range(2)]
    xa0 = _xa_fwd(proj0, P0_XQ, kv[0], xa_q_g[0], xa_k_g[0], name="xa0")
    y0 = _gate_fwd(mix0, xa0, proj0, P0_Z, name="gate0")
    x1 = _matmul(y0, w_out, n=D_MODEL, tn=MM_TN, b_spec=out_rhs(0), res=x, name="out0")

    h1 = _rmsnorm_fwd(x1, norm_g[1], name="norm1")
    proj1 = _matmul(h1, w_in1, n=SB_PROJ, tn=MM_TN, b_spec=in1_rhs, name="proj1")
    mix1, tot1 = _sb_fwd(proj1, sb_q_g, sb_k_g, name="sb")
    xa1 = _xa_fwd(proj1, P1_XQ, kv[1], xa_q_g[1], xa_k_g[1], name="xa1")
    y1 = _gate_fwd(mix1, xa1, proj1, P1_Z, name="gate1")
    x2 = _matmul(y1, w_out, n=D_MODEL, tn=MM_TN, b_spec=out_rhs(1), res=x1, name="out1")

    dx2, loss_vec = _loss_head(x2, target, name="loss")

    d_wout1 = _matmul(y1, dx2, ta=True, name="d_wout1", **slot_rows(INNER // N_CHIPS))
    dy1 = _matmul(dx2, w_out, tb=True, n=INNER, tn=MM_TN, b_spec=out_rhs_t(1), name="dy1")
    dcat1, dz1 = _gate_bwd(dy1, mix1, xa1, proj1, P1_Z, name="gate1_bwd")
    dqkv1, d_sbq, d_sbk = _sb_bwd(dcat1, tot1, proj1, sb_q_g, sb_k_g, name="sb_bwd")
    dxq1, dkv1, d_xaq1, d_xak1 = _xa_bwd(dcat1, proj1, P1_XQ, kv[1], xa_q_g[1], xa_k_g[1], name="xa1_bwd")
    dproj1 = dqkv1 + [dxq1, dz1]
    d_win1 = _matmul(h1, dproj1, ta=True, name="d_win1", **in1_out)
    d_wkv1 = _matmul(mem_n, dkv1, ta=True, name="d_wkv1", **slot_rows(D_MODEL // N_CHIPS))
    token = early_grads(1, d_win1, d_wout1, d_wkv1)
    dproj1 = dqkv1 + [dxq1 + token[0, 0].astype(BF16), dz1]
    dh1 = _matmul(dproj1, w_in1, tb=True, n=D_MODEL, tn=D_MODEL // 2, tk=MM_TN, b_spec=in1_rhs_t, name="dh1")
    token = grads_swapped(dh1)
    dx1, d_ng1 = _rmsnorm_bwd(dh1, x1, norm_g[1] + token[0, 0], dx2, name="norm1_bwd")

    d_wout0 = _matmul(y0, dx1, ta=True, name="d_wout0", **slot_rows(INNER // N_CHIPS))
    dy0 = _matmul(dx1, w_out, tb=True, n=INNER, tn=MM_TN, b_spec=out_rhs_t(0), name="dy0")
    dcat0, dz0 = _gate_bwd(dy0, mix0, xa0, proj0, P0_Z, name="gate0_bwd")
    dqv, dkv_h, dvc, dg_cols, db_cols, d_outg = _dn_core_bwd(
        dcat0, qn, kn, vc, g_rows, b_rows, out_g, states, tinv, name="dn_core_bwd")
    dpq, dwq = _dn_pre_bwd(dqv, proj0, conv_w, 0, DN_QK_WIDTH, l2=True, scale=q_scale, name="dn_pre_q_bwd")
    dpk, dwk = _dn_pre_bwd(dkv_h, proj0, conv_w, DN_QK_WIDTH, DN_QK_WIDTH, l2=True, scale=1.0, name="dn_pre_k_bwd")
    dpv, dwv = _dn_pre_bwd(dvc, proj0, conv_w, 2 * DN_QK_WIDTH, MIX_WIDTH, l2=False, scale=1.0, name="dn_pre_v_bwd")
    dab, d_alog, d_dt = _dn_ab_bwd(_cols_to_nat(dg_cols, nc), _cols_to_nat(db_cols, nc), proj0, al, dtb, name="dn_ab_bwd")
    dxq0, dkv0, d_xaq0, d_xak0 = _xa_bwd(dcat0, proj0, P0_XQ, kv[0], xa_q_g[0], xa_k_g[0], name="xa0_bwd")
    d_win0 = _matmul(h0, [dpq, dpk, dpv, dxq0, dz0, dab], ta=True, out_dtype=BF16, name="d_win0")
    d_wkv0 = _matmul(mem_n, dkv0, ta=True, name="d_wkv0", **slot_rows(D_MODEL // N_CHIPS))
    token = early_grads(0, d_win0, d_wout0, d_wkv0)
    zero = token[0, 0]
    dh0 = _matmul([dpq, dpk, dpv, dxq0, dz0, dab + zero.astype(BF16)], w_in0, tb=True, tk=MM_TN, name="dh0")
    dx0, d_ng0 = _rmsnorm_bwd(dh0, x, norm_g[0] + zero, dx1, name="norm0_bwd")

    dmem0 = _matmul(dkv0, w_kv, tb=True, n=D_MODEL, tn=D_MODEL // N_CHIPS, b_spec=kv_rhs_t(0), name="dmem0")
    dmem_n = _matmul(dkv1, w_kv, tb=True, n=D_MODEL, tn=D_MODEL // N_CHIPS, b_spec=kv_rhs_t(1), res=dmem0, name="dmem1")
    _, d_memg = _rmsnorm_bwd(dmem_n, mem, mem_norm_g, None, name="mem_norm_bwd")

    grads = dict(
        norm_g=jnp.concatenate([d_ng0, d_ng1], axis=0), mem_norm_g=d_memg.reshape(-1),
        xa_q_norm_g=jnp.concatenate([d_xaq0, d_xaq1], axis=0), xa_k_norm_g=jnp.concatenate([d_xak0, d_xak1], axis=0),
        dn_conv_w=jnp.concatenate([dwq, dwk, dwv], axis=1),
        dn_a_log=d_alog[:, :DN_V_HEADS], dn_dt_bias=d_dt[:, :DN_V_HEADS], dn_out_norm_g=d_outg,
        sb_q_norm_g=d_sbq, sb_k_norm_g=d_sbk)
    return loss_vec, dx0, grads


ANY = pl.BlockSpec(memory_space=pl.ANY)


def _place():
    x, y, c = lax.axis_index("x"), lax.axis_index("y"), lax.axis_index("c")
    chips = [(1 - x, y), (x, 1 - y), (1 - x, 1 - y)]
    return x, y, c, 2 * x + y, (x, y, 1 - c), chips


def _rcopy(src, dst, send, recv, i, dev):
    return pltpu.make_async_remote_copy(src_ref=src, dst_ref=dst, send_sem=send.at[i], recv_sem=recv.at[i],
                                        device_id=dev, device_id_type=MESH)


def _swap_halves(xs, *, name):
    nt = len(xs)

    def body(*refs):
        src, dst = refs[:nt], refs[nt:2 * nt]
        send, recv = refs[2 * nt:]
        x, y, c, j, sib, chips = _place()
        cps = []
        for t in range(nt):
            for s in range(N_CHIPS):
                cps.append(_rcopy(src[t].at[s, 1 - c], dst[t].at[s], send, recv, 4 * t + s, sib))
                cps[-1].start()
        for cp in cps:
            cp.wait_recv()
        for cp in cps:
            cp.wait_send()

    return pl.pallas_call(
        body, in_specs=[ANY] * nt, out_specs=[ANY] * nt,
        out_shape=[jax.ShapeDtypeStruct((N_CHIPS,) + a.shape[2:], a.dtype) for a in xs],
        scratch_shapes=[pltpu.SemaphoreType.DMA((4 * nt,)), pltpu.SemaphoreType.DMA((4 * nt,))], name=name)(*xs)


def _swap_with_sibling(fs, *, name):
    nt = len(fs)

    def body(*refs):
        src, dst = refs[:nt], refs[nt:2 * nt]
        send, recv = refs[2 * nt:]
        x, y, c, j, sib, chips = _place()
        cps = [_rcopy(src[t], dst[t], send, recv, t, sib) for t in range(nt)]
        for cp in cps:
            cp.start()
        for cp in cps:
            cp.wait_recv()
        for cp in cps:
            cp.wait_send()

    return pl.pallas_call(
        body, in_specs=[ANY] * nt, out_specs=[ANY] * nt,
        out_shape=[jax.ShapeDtypeStruct(a.shape, a.dtype) for a in fs],
        scratch_shapes=[pltpu.SemaphoreType.DMA((nt,)), pltpu.SemaphoreType.DMA((nt,))], name=name)(*fs)


HBM_SPEC = pl.BlockSpec(memory_space=pltpu.HBM)
SEM_SPEC = pl.BlockSpec(memory_space=pltpu.SEMAPHORE)
SIDE_EFFECT = pltpu.SideEffectType.DATAFLOW_SIDE_EFFECTING


def _gather_plan(src, land):
    x, y, c, j, sib, chips = _place()
    return [(src[t].at[c], land[t].at[j, c], (cx, cy, c), land[t].at[2 * cx + cy, c])
            for t in range(len(src)) for cx, cy in chips]


def _scatter_plan(src, land):
    x, y, c, j, sib, chips = _place()
    return [(src[t].at[2 * cx + cy], land[t].at[k], (cx, cy, c), land[t].at[k])
            for t in range(len(src)) for k, (cx, cy) in enumerate(chips)]


def _swap_plan(src, land):
    x, y, c, j, sib, chips = _place()
    return [(src[t].at[s, 1 - c], land[t].at[s], sib, land[t].at[s]) for t in range(len(src)) for s in range(N_CHIPS)]


def _exchange_start(srcs, lands, plan, *, name, per_tensor=3):
    ns, nb = len(srcs), len(srcs) + len(lands)
    n = per_tensor * ns

    def body(*refs):
        send, recv, token = refs[nb], refs[nb + 1], refs[-1]
        for i, (s, d, dev, _) in enumerate(plan(refs[:ns], refs[ns:nb])):
            _rcopy(s, d, send, recv, i, dev).start()
        token[...] = jnp.zeros_like(token)

    bufs = list(srcs) + list(lands)
    outs = pl.pallas_call(
        body, name=name,
        out_shape=(pltpu.SemaphoreType.DMA((n,)), pltpu.SemaphoreType.DMA((n,)), *[pltpu.HBM(a.shape, a.dtype) for a in bufs],
                   jax.ShapeDtypeStruct((8, LANE), F32)),
        in_specs=[HBM_SPEC] * nb, out_specs=(SEM_SPEC, SEM_SPEC, *[HBM_SPEC] * nb, pl.BlockSpec(memory_space=pltpu.VMEM)),
        input_output_aliases={i: 2 + i for i in range(nb)},
        compiler_params=pltpu.CompilerParams(has_side_effects=SIDE_EFFECT))(
            *[pltpu.with_memory_space_constraint(a, pltpu.HBM) for a in bufs])
    return outs[0], outs[1], list(outs[2:2 + ns]), list(outs[2 + ns:2 + nb]), outs[-1]


def _exchange_wait(srcs, lands, send, recv, after, plan, *, name):
    ns, nb = len(srcs), len(srcs) + len(lands)
    afters = list(after) if isinstance(after, (list, tuple)) else [after]

    def body(*refs):
        send_s, recv_s = refs[nb], refs[nb + 1]
        for i, (s, d, dev, inc) in enumerate(plan(refs[:ns], refs[ns:nb])):
            _rcopy(s, d, send_s, recv_s, i, dev).wait_send()
            _rcopy(inc, inc, send_s, recv_s, i, dev).wait_recv()

    bufs = list(srcs) + list(lands)
    outs = pl.pallas_call(
        body, name=name, out_shape=tuple(pltpu.HBM(a.shape, a.dtype) for a in bufs),
        in_specs=[HBM_SPEC] * nb + [SEM_SPEC, SEM_SPEC] + [ANY] * len(afters), out_specs=tuple([HBM_SPEC] * nb),
        input_output_aliases={i: i for i in range(nb)},
        compiler_params=pltpu.CompilerParams(has_side_effects=SIDE_EFFECT))(*bufs, send, recv, *afters)
    return list(outs[:ns]), list(outs[ns:])


def _forward_halves(lands, *, name):
    nt = len(lands)

    def body(*refs):
        src, dst = refs[:nt], refs[nt:2 * nt]
        send, recv = refs[2 * nt:]
        x, y, c, j, sib, chips = _place()
        cps = []
        for t in range(nt):
            for k, (cx, cy) in enumerate(chips):
                cps.append(_rcopy(src[t].at[2 * cx + cy, c], dst[t].at[2 * cx + cy, c], send, recv, 3 * t + k, sib))
                cps[-1].start()
        for t in range(nt):
            for k, (cx, cy) in enumerate(chips):
                other = dst[t].at[2 * cx + cy, 1 - c]
                _rcopy(other, other, send, recv, 3 * t + k, sib).wait_recv()
        for cp in cps:
            cp.wait_send()

    return pl.pallas_call(
        body, in_specs=[ANY] * nt, out_specs=[ANY] * nt, out_shape=[jax.ShapeDtypeStruct(a.shape, a.dtype) for a in lands],
        input_output_aliases={t: t for t in range(nt)},
        scratch_shapes=[pltpu.SemaphoreType.DMA((3 * nt,)), pltpu.SemaphoreType.DMA((3 * nt,))], name=name)(*lands)


def _all_reduce_small(parts, *, name):
    n = len(parts)
    offs, rows = [], 0
    for p in parts:
        offs.append(rows)
        rows += -(-p.shape[0] // 8) * 8

    def body(*refs):
        p_refs, o_refs = refs[:n], refs[n:2 * n]
        buf, send, recv = refs[2 * n:]
        x, y, c = lax.axis_index("x"), lax.axis_index("y"), lax.axis_index("c")
        me = 4 * x + 2 * y + c
        buf[me] = jnp.zeros((rows, LANE), F32)
        for p_ref, off in zip(p_refs, offs):
            buf[me, off:off + p_ref.shape[0], :] = p_ref[...]
        cps = []
        for r in range(1, 8):
            dev = (x ^ (r >> 2), y ^ ((r >> 1) & 1), c ^ (r & 1))
            cps.append(_rcopy(buf.at[me], buf.at[me], send, recv, r - 1, dev))
            cps[-1].start()
        for r in range(1, 8):
            frm = buf.at[me ^ r]
            _rcopy(frm, frm, send, recv, r - 1, (x, y, c)).wait_recv()
        for cp in cps:
            cp.wait_send()
        acc = buf[0]
        for d in range(1, 8):
            acc = acc + buf[d]
        for o_ref, off in zip(o_refs, offs):
            o_ref[...] = acc[off:off + o_ref.shape[0], :]

    vm = pl.BlockSpec(memory_space=pltpu.VMEM)
    return pl.pallas_call(
        body, in_specs=[vm] * n, out_specs=[vm] * n, out_shape=[jax.ShapeDtypeStruct(p.shape, F32) for p in parts],
        scratch_shapes=[pltpu.VMEM((8, rows, LANE), F32), pltpu.SemaphoreType.DMA((7,)), pltpu.SemaphoreType.DMA((7,))],
        name=name)(*parts)


def _add_halves(x, b, c_idx, *, name, tr=256):
    _, _, R, C = x.shape
    tr = min(tr, R)

    def body(c_ref, x_ref, b_ref, o_ref):
        o_ref[...] = (x_ref[...].astype(F32) + b_ref[...].astype(F32)).astype(o_ref.dtype)

    return pl.pallas_call(
        body,
        grid_spec=pltpu.PrefetchScalarGridSpec(
            num_scalar_prefetch=1, grid=(N_CHIPS, R // tr),
            in_specs=[pl.BlockSpec((None, None, tr, C), lambda s, i, c_ref: (s, c_ref[0], i, 0)),
                      pl.BlockSpec((None, tr, C), lambda s, i, c_ref: (s, i, 0))],
            out_specs=pl.BlockSpec((None, tr, C), lambda s, i, c_ref: (s, i, 0))),
        out_shape=jax.ShapeDtypeStruct(b.shape, b.dtype), compiler_params=_cp(("parallel", "parallel")), name=name)(c_idx, x, b)


def _sum_slot(p, rcv, j_idx, *, name, tr=256):
    _, R, C = p.shape
    tr = min(tr, R)

    def body(j_ref, p_ref, r_ref, o_ref):
        acc = p_ref[...].astype(F32)
        for k in range(3):
            acc = acc + r_ref[k].astype(F32)
        o_ref[...] = acc

    return pl.pallas_call(
        body,
        grid_spec=pltpu.PrefetchScalarGridSpec(
            num_scalar_prefetch=1, grid=(R // tr,),
            in_specs=[pl.BlockSpec((None, tr, C), lambda i, j_ref: (j_ref[0], i, 0)),
                      pl.BlockSpec((3, tr, C), lambda i, j_ref: (0, i, 0))],
            out_specs=pl.BlockSpec((tr, C), lambda i, j_ref: (i, 0))),
        out_shape=jax.ShapeDtypeStruct((R, C), F32), compiler_params=_cp(("parallel",)), name=name)(j_idx, p, rcv)


def _adamw_math(w, g, m, v):
    nm = ADAM_B1 * m + (1.0 - ADAM_B1) * g
    nv = ADAM_B2 * v + (1.0 - ADAM_B2) * (g * g)
    m_hat = nm / (1.0 - ADAM_B1 ** ADAM_STEP)
    v_hat = nv / (1.0 - ADAM_B2 ** ADAM_STEP)
    return -ADAM_LR * (m_hat / (jnp.sqrt(v_hat) + ADAM_EPS) + ADAM_WD * w), nm, nv


def _adamw_halves(w, g_mine, g_theirs, m, v, c_idx, *, name, layer=0, into=None, tr=128):
    _, _, R, C = w.shape
    tr = tr if R % tr == 0 else R

    def body(c_ref, w_ref, gm_ref, gt_ref, m_ref, v_ref, *rest):
        g_ref, d_ref, nm_ref, nv_ref = rest[-4:]
        gv = jnp.where(pl.program_id(0) == c_ref[0], gm_ref[...], gt_ref[...])
        d, nm, nv = _adamw_math(w_ref[...], gv, m_ref[...], v_ref[...])
        g_ref[...] = gv
        d_ref[...] = d
        nm_ref[...] = nm
        nv_ref[...] = nv

    full = pl.BlockSpec((None, None, tr, C), lambda hh, i, c_ref: (layer, hh, i, 0))
    half = pl.BlockSpec((tr, C), lambda hh, i, c_ref: (i, 0))
    sh = jax.ShapeDtypeStruct(w.shape, F32)
    extra = [] if into is None else list(into)
    return pl.pallas_call(
        body,
        grid_spec=pltpu.PrefetchScalarGridSpec(num_scalar_prefetch=1, grid=(2, R // tr),
                                               in_specs=[full, half, half, full, full] + [ANY] * len(extra),
                                               out_specs=(full,) * 4),
        out_shape=(sh,) * 4, input_output_aliases={6 + t: t for t in range(len(extra))},
        compiler_params=_cp(("parallel", "parallel")), name=name)(c_idx, w, g_mine, g_theirs, m, v, *extra)


def _adamw_parts(ws, gs, ms, vs, *, name):
    n = len(ws)

    def body(*refs):
        ins, outs = refs[:4 * n], refs[4 * n:]
        for t in range(n):
            d, nm, nv = _adamw_math(ins[t][...], ins[n + t][...], ins[2 * n + t][...], ins[3 * n + t][...])
            outs[t][...] = d
            outs[n + t][...] = nm
            outs[2 * n + t][...] = nv

    vm = pl.BlockSpec(memory_space=pltpu.VMEM)
    shapes = [jax.ShapeDtypeStruct(w.shape, F32) for w in ws] * 3
    outs = pl.pallas_call(body, in_specs=[vm] * (4 * n), out_specs=[vm] * (3 * n), out_shape=shapes, name=name)(
        *ws, *gs, *ms, *vs)
    return outs[:n], outs[n:2 * n], outs[2 * n:]


_SMALL = ["norm_g", "mem_norm_g", "xa_q_norm_g", "xa_k_norm_g", "dn_a_log", "dn_dt_bias", "dn_out_norm_g",
          "sb_q_norm_g", "sb_k_norm_g"]


def _rows128(a):
    flat = a.reshape(-1)
    pad = -flat.shape[0] % LANE
    if pad:
        flat = jnp.pad(flat, (0, pad))
    return flat.reshape(-1, LANE)


def _unrows(r, shape):
    return r.reshape(-1)[:math.prod(shape)].reshape(shape)


def kernel(x, mem, norm_g, mem_norm_g, mem_w_kv, xa_q_norm_g, xa_k_norm_g, w_out, dn_w_in, dn_conv_w, dn_a_log, dn_dt_bias, dn_out_norm_g, sb_w_in, sb_q_norm_g, sb_k_norm_g, loss_target, m_norm_g, m_mem_norm_g, m_mem_w_kv, m_xa_q_norm_g, m_xa_k_norm_g, m_w_out, m_dn_w_in, m_dn_conv_w, m_dn_a_log, m_dn_dt_bias, m_dn_out_norm_g, m_sb_w_in, m_sb_q_norm_g, m_sb_k_norm_g, v_norm_g, v_mem_norm_g, v_mem_w_kv, v_xa_q_norm_g, v_xa_k_norm_g, v_w_out, v_dn_w_in, v_dn_conv_w, v_dn_a_log, v_dn_dt_bias, v_dn_out_norm_g, v_sb_w_in, v_sb_q_norm_g, v_sb_k_norm_g):
    W = dict(norm_g=norm_g, mem_norm_g=mem_norm_g, mem_w_kv=mem_w_kv, xa_q_norm_g=xa_q_norm_g, xa_k_norm_g=xa_k_norm_g,
             w_out=w_out, dn_w_in=dn_w_in, dn_conv_w=dn_conv_w, dn_a_log=dn_a_log, dn_dt_bias=dn_dt_bias,
             dn_out_norm_g=dn_out_norm_g, sb_w_in=sb_w_in, sb_q_norm_g=sb_q_norm_g, sb_k_norm_g=sb_k_norm_g)
    M = dict(norm_g=m_norm_g, mem_norm_g=m_mem_norm_g, mem_w_kv=m_mem_w_kv, xa_q_norm_g=m_xa_q_norm_g,
             xa_k_norm_g=m_xa_k_norm_g, w_out=m_w_out, dn_w_in=m_dn_w_in, dn_conv_w=m_dn_conv_w, dn_a_log=m_dn_a_log,
             dn_dt_bias=m_dn_dt_bias, dn_out_norm_g=m_dn_out_norm_g, sb_w_in=m_sb_w_in, sb_q_norm_g=m_sb_q_norm_g,
             sb_k_norm_g=m_sb_k_norm_g)
    V = dict(norm_g=v_norm_g, mem_norm_g=v_mem_norm_g, mem_w_kv=v_mem_w_kv, xa_q_norm_g=v_xa_q_norm_g,
             xa_k_norm_g=v_xa_k_norm_g, w_out=v_w_out, dn_w_in=v_dn_w_in, dn_conv_w=v_dn_conv_w, dn_a_log=v_dn_a_log,
             dn_dt_bias=v_dn_dt_bias, dn_out_norm_g=v_dn_out_norm_g, sb_w_in=v_sb_w_in, sb_q_norm_g=v_sb_q_norm_g,
             sb_k_norm_g=v_sb_k_norm_g)
    names = ["norm_g", "mem_norm_g", "mem_w_kv", "xa_q_norm_g", "xa_k_norm_g", "w_out", "dn_w_in", "dn_conv_w",
             "dn_a_log", "dn_dt_bias", "dn_out_norm_g", "sb_w_in", "sb_q_norm_g", "sb_k_norm_g"]
    cx, cy, cc = lax.axis_index("x"), lax.axis_index("y"), lax.axis_index("c")
    slot = 2 * cx + cy
    half_r = D_MODEL // 2
    conv_cols = dn_conv_w.shape[2]

    by_slot = lambda fn, a: lax.switch(slot, [lambda v, s=s: fn(s, v) for s in range(N_CHIPS)], a)
    w0s = by_slot(_pack_own_shard, dn_w_in[0].astype(BF16)).reshape(2, half_r, WIN_COLS)
    convs = jnp.pad(dn_conv_w[0], ((0, 8 - DN_CONV), (0, 0))).reshape(8, 2, conv_cols // 2).transpose(1, 0, 2)
    c_idx = jnp.reshape(cc, (1,)).astype(jnp.int32)
    j_idx = jnp.reshape(slot, (1,)).astype(jnp.int32)
    own_a = [w0s, convs]
    lands_a = [lax.dynamic_update_slice(lax.empty((N_CHIPS,) + o.shape, o.dtype), o[None], (slot, 0, 0, 0)) for o in own_a]
    send_a, recv_a, own_a, lands_a, token_a = _exchange_start(own_a, lands_a, _gather_plan, name="gather_start")
    zero_a = token_a[0, 0]
    M["dn_w_in"] = m_dn_w_in + zero_a
    V["dn_w_in"] = v_dn_w_in + zero_a
    own_b = [(sb_w_in[0] + zero_a).astype(BF16).reshape(2, half_r, SB_PROJ // N_CHIPS), (w_out + zero_a).astype(BF16),
             (mem_w_kv + zero_a).astype(BF16)]
    view0 = (1, 2, half_r, P0_SHARD)
    _, lands_a = _exchange_wait(own_a, lands_a, send_a, recv_a,
                                [M["dn_w_in"].reshape(view0), V["dn_w_in"].reshape(view0)] + own_b,
                                _gather_plan, name="gather_wait")
    (g0, gconv), own_b = lax.optimization_barrier((_forward_halves(lands_a, name="gather_forward"), own_b))
    lands_b = [lax.dynamic_update_slice(lax.empty((N_CHIPS,) + o.shape, o.dtype), o[None], (slot, 0, 0, 0)) for o in own_b]
    send_b, recv_b, own_b, lands_b, token_b = _exchange_start(own_b, lands_b, _gather_plan, name="gather_late_start")

    def late_weights(after):
        _, lands = _exchange_wait(own_b, lands_b, send_b, recv_b, after, _gather_plan, name="gather_late_wait")
        g1, gout, gkv = _forward_halves(lands, name="gather_late_forward")
        return gkv, gout, g1

    rs = {}

    def scatter_start(tag, xs, from_sib=None):
        if from_sib is None:
            from_sib = _swap_halves(xs, name=f"rs{tag}_swap")
        ps = [_add_halves(a, b, c_idx, name=f"rs{tag}_add{t}") for t, (a, b) in enumerate(zip(xs, from_sib))]
        rcv = [lax.empty((3,) + p.shape[1:], p.dtype) for p in ps]
        send, recv, ps, rcv, token = _exchange_start(ps, rcv, _scatter_plan, name=f"rs{tag}_scatter_start")
        rs[tag] = (ps, rcv, send, recv)
        return token

    def scatter_finish(tag, after):
        ps, rcv, send, recv = rs[tag]
        ps, rcv = _exchange_wait(ps, rcv, send, recv, after, _scatter_plan, name=f"rs{tag}_scatter_wait")
        return [_sum_slot(p, r, j_idx, name=f"rs{tag}_sum{t}") for t, (p, r) in enumerate(zip(ps, rcv))]

    def early_grads(layer, d_win, d_wout, d_wkv):
        if layer == 0:
            d_win = jnp.stack([_padded_to_window(s, d_win) for s in range(N_CHIPS)]).reshape(N_CHIPS, 2, half_r, WIN_COLS)
        xs = [d_win, d_wout.reshape(N_CHIPS, 2, -1, D_MODEL), d_wkv.reshape(N_CHIPS, 2, -1, 2 * XA_WIDTH)]
        if layer == 0:
            return scatter_start(0, xs)
        lands = [lax.empty((N_CHIPS,) + a.shape[2:], a.dtype) for a in xs]
        send, recv, xs, lands, token = _exchange_start(xs, lands, _swap_plan, per_tensor=N_CHIPS, name="rs1_swap_start")
        rs["swap1"] = (xs, lands, send, recv)
        return token

    def grads_swapped(after):
        xs, lands, send, recv = rs["swap1"]
        xs, from_sib = _exchange_wait(xs, lands, send, recv, after, _swap_plan, name="rs1_swap_wait")
        return scatter_start(1, xs, from_sib)

    w_in0 = _windows_to_padded(g0.reshape(N_CHIPS, D_MODEL, WIN_COLS))
    conv_f = gconv.transpose(2, 0, 1, 3).reshape(8, N_CHIPS * conv_cols)[:DN_CONV]

    loss_vec, grad_x, g = _local_step(
        x[0], mem[0], loss_target[0], norm_g + token_b[0, 0], mem_norm_g, xa_q_norm_g, xa_k_norm_g, w_in0, conv_f,
        dn_a_log[0], dn_dt_bias[0], dn_out_norm_g[0], sb_q_norm_g[0], sb_k_norm_g[0], late_weights, early_grads,
        grads_swapped)

    mine1 = scatter_finish(1, grad_x)
    theirs1 = _swap_with_sibling(mine1, name="rs1_join")
    big1 = [("sb_w_in", None), ("w_out", 1), ("mem_w_kv", 1)]
    big0 = [("dn_w_in", None), ("w_out", 0), ("mem_w_kv", 0)]

    out_g, out_d, out_m, out_v = {}, {}, {}, {}
    partial = {}

    def adamw_big(big, mine, theirs):
        for (n, layer), mine_g, their_g in zip(big, mine, theirs):
            layers = 1 if layer is None else 2
            view = (layers, 2) + mine_g.shape
            partial[n] = _adamw_halves(W[n].reshape(view), mine_g, their_g, M[n].reshape(view), V[n].reshape(view), c_idx,
                                       layer=layer or 0, into=partial.get(n), name=f"adamw_{n}" + ("" if layer is None else str(layer)))
        return [partial[n][0] for n, _ in big]

    done1 = lax.optimization_barrier(tuple(adamw_big(big1, mine1, theirs1)))[-1]
    mine0 = scatter_finish(0, done1)
    mine0[0] = by_slot(_unpack_own_shard, mine0[0])

    parts, _ = lax.optimization_barrier(([_rows128(g[n]) for n in _SMALL] + [_rows128(g["dn_conv_w"]), loss_vec], mine0[0]))
    red = _all_reduce_small(parts, name="all_reduce_small")
    small_rows = dict(zip(_SMALL, red))
    conv_full = red[len(_SMALL)].reshape(DN_CONV, N_CHIPS * conv_cols)
    small_rows["dn_conv_w"] = _rows128(lax.dynamic_slice_in_dim(conv_full, slot * conv_cols, conv_cols, axis=1))
    loss = red[-1][0, 0]

    adamw_big(big0, mine0, _swap_with_sibling(mine0, name="rs0_join"))
    for n, outs in partial.items():
        out_g[n], out_d[n], out_m[n], out_v[n] = [o.reshape(W[n].shape) for o in outs]
    small_names = _SMALL + ["dn_conv_w"]
    ds, nms, nvs = _adamw_parts([_rows128(W[n]) for n in small_names], [small_rows[n] for n in small_names],
                                [_rows128(M[n]) for n in small_names], [_rows128(V[n]) for n in small_names], name="adamw_small")
    for n, d, nm, nv in zip(small_names, ds, nms, nvs):
        shp = W[n].shape
        out_g[n], out_d[n], out_m[n], out_v[n] = [_unrows(r, shp) for r in (small_rows[n], d, nm, nv)]

    return (loss, grad_x[None], *[out_g[n] for n in names], *[out_d[n] for n in names], *[out_m[n] for n in names],
            *[out_v[n] for n in names])
```

```python
import math

import jax
import jax.numpy as jnp
from jax import lax
from jax.experimental import pallas as pl
from jax.experimental.pallas import tpu as pltpu

F32 = jnp.float32
BF16 = jnp.bfloat16
HI = lax.Precision.HIGHEST
MESH = pl.DeviceIdType.MESH

D_MODEL = 2048
INNER = 4096
XA_WIDTH = 1024
XA_HEADS = 4
XA_DIM = 256
MIX_WIDTH = 3072
HEAD_DIM = 128
DN_V_HEADS = 24
DN_QK_WIDTH = 1536
DN_CONV = 4
DN_PROJ = 11312
SB_PROJ = 14336
EPS = 1e-6
N_CHIPS = 4

CH = 128
LANE = 128

P0_XQ = 6144
P0_Z = 7168
P0_AB = 11264
P0 = 11776
P0_SHARD = DN_PROJ // N_CHIPS
P1_XQ = 9216
P1_Z = 10240
P1 = SB_PROJ

ADAM_LR = 0.001
ADAM_B1 = 0.9
ADAM_B2 = 0.999
ADAM_EPS = 1e-08
ADAM_WD = 0.01
ADAM_STEP = 10

VMEM_LIMIT = 48 * 1024 * 1024


def _cp(sem=None, **kw):
    return pltpu.CompilerParams(dimension_semantics=sem, vmem_limit_bytes=VMEM_LIMIT, **kw)


def _bdot(a, b, dims):
    return lax.dot_general(a.astype(BF16), b.astype(BF16), (dims, ((), ())), preferred_element_type=F32)


def _fdot(a, b, dims):
    return lax.dot_general(a, b, (dims, ((), ())), precision=HI, preferred_element_type=F32)


NN = ((1,), (0,))
NT = ((1,), (1,))
TN = ((0,), (0,))


def _sigmoid(x):
    return 1.0 / (1.0 + jnp.exp(-x))


def _softplus(x):
    return jnp.maximum(x, 0.0) + jnp.log(1.0 + jnp.exp(-jnp.abs(x)))


def _iota2(shape, axis):
    return lax.broadcasted_iota(jnp.int32, shape, axis)


MM_FULL_K = 4096
MM_BLOCK_BYTES = 4 * 1024 * 1024


def _matmul(a, b, *, ta=False, tb=False, out_dtype=F32, res=None, name, n=None, tm=None, tn=None, tk=None,
            b_spec=None, o_spec=None, o_shape=None):
    a_segs = list(a) if isinstance(a, (list, tuple)) else [a]
    b_segs = list(b) if isinstance(b, (list, tuple)) else [b]
    a0, b0 = a_segs[0], b_segs[0]
    M = a0.shape[1] if ta else a0.shape[0]
    K = a0.shape[0] if ta else sum(s.shape[1] for s in a_segs)
    if n is None:
        n = b0.shape[0] if tb else sum(s.shape[1] for s in b_segs)
    N = n
    dims = ((0,) if ta else (1,), (1,) if tb else (0,))
    has_res = res is not None
    flat = lambda v: v.reshape(-1, v.shape[-1])
    o_shape = o_shape or jax.ShapeDtypeStruct((M, N), out_dtype)

    def seg_specs(segs, tile, block, pos):
        specs, ranges, off = [], [], 0
        for s in segs:
            cnt = s.shape[1] // tile
            assert s.shape[1] % tile == 0, (name, s.shape, tile)

            def imap(*g, off=off, cnt=cnt):
                t = jnp.clip(g[pos] - off, 0, cnt - 1)
                return (g[0], t) if pos == 2 else (0, t)

            specs.append(pl.BlockSpec(block, imap))
            ranges.append((off, off + cnt))
            off += cnt
        return specs, ranges

    if K <= MM_FULL_K:
        assert len(a_segs) == 1
        tm = tm or min(M, 1024, max(256, MM_BLOCK_BYTES // (K * a0.dtype.itemsize)))
        tn = tn or min(N, 512)
        assert M % tm == 0 and N % tn == 0, (name, M, N, K, tm, tn)
        nb = len(b_segs)
        if b_spec is not None:
            b_specs, b_ranges = [b_spec], [(0, N // tn)]
        elif nb > 1:
            assert not tb
            b_specs, b_ranges = seg_specs(b_segs, tn, (K, tn), 1)
        else:
            b_specs = [pl.BlockSpec((tn, K), lambda i, j: (j, 0)) if tb else pl.BlockSpec((K, tn), lambda i, j: (0, j))]
            b_ranges = [(0, N // tn)]

        def body_full(*refs):
            a_ref, b_refs = refs[0], refs[1:1 + nb]
            r_ref = refs[1 + nb] if has_res else None
            o_ref = refs[-1]
            j = pl.program_id(1)
            for b_ref, (lo, hi) in zip(b_refs, b_ranges):
                def emit(b_ref=b_ref):
                    r = _bdot(a_ref[...], flat(b_ref[...]), dims)
                    if has_res:
                        r = r + r_ref[...]
                    o_ref[...] = r.astype(o_ref.dtype).reshape(o_ref.shape)
                if nb == 1:
                    emit()
                else:
                    pl.when(jnp.logical_and(j >= lo, j < hi))(emit)

        a_spec = pl.BlockSpec((K, tm), lambda i, j: (0, i)) if ta else pl.BlockSpec((tm, K), lambda i, j: (i, 0))
        o_spec = o_spec or pl.BlockSpec((tm, tn), lambda i, j: (i, j))
        r_spec = [pl.BlockSpec((tm, tn), lambda i, j: (i, j))] if has_res else []
        return pl.pallas_call(
            body_full, grid=(M // tm, N // tn), in_specs=[a_spec] + b_specs + r_spec, out_specs=o_spec, out_shape=o_shape,
            compiler_params=_cp(("parallel", "arbitrary")), name=name)(*([a0] + b_segs + ([res] if has_res else [])))

    assert tb and not ta and len(b_segs) == 1
    tm, tn = tm or min(M, 1024), tn or min(N, 1024)
    tk = tk or (1024 if all(s.shape[1] % 1024 == 0 for s in a_segs) else 512)
    assert M % tm == 0 and N % tn == 0 and K % tk == 0, (name, M, N, K, tm, tn, tk)
    nk = K // tk
    na = len(a_segs)
    if na > 1:
        a_specs, a_ranges = seg_specs(a_segs, tk, (tm, tk), 2)
    else:
        a_specs, a_ranges = [pl.BlockSpec((tm, tk), lambda i, j, k: (i, k))], [(0, nk)]
    b_spec = b_spec or pl.BlockSpec((tn, tk), lambda i, j, k: (j, k))

    def body(*refs):
        a_refs, b_ref = refs[:na], refs[na]
        r_ref = refs[na + 1] if has_res else None
        o_ref, acc = refs[-2], refs[-1]
        k = pl.program_id(2)

        @pl.when(k == 0)
        def _():
            acc[...] = jnp.zeros_like(acc)

        for a_ref, (lo, hi) in zip(a_refs, a_ranges):
            def emit(a_ref=a_ref):
                acc[...] += _bdot(a_ref[...], flat(b_ref[...]), dims)
            if na == 1:
                emit()
            else:
                pl.when(jnp.logical_and(k >= lo, k < hi))(emit)

        @pl.when(k == nk - 1)
        def _():
            r = acc[...]
            if has_res:
                r = r + r_ref[...]
            o_ref[...] = r.astype(o_ref.dtype).reshape(o_ref.shape)

    o_spec = o_spec or pl.BlockSpec((tm, tn), lambda i, j, k: (i, j))
    r_spec = [pl.BlockSpec((tm, tn), lambda i, j, k: (i, j))] if has_res else []
    return pl.pallas_call(
        body, grid=(M // tm, N // tn, nk), in_specs=a_specs + [b_spec] + r_spec, out_specs=o_spec, out_shape=o_shape,
        scratch_shapes=[pltpu.VMEM((tm, tn), F32)],
        compiler_params=_cp(("parallel", "parallel", "arbitrary")), name=name)(*(a_segs + [b0] + ([res] if has_res else [])))


def _rmsnorm_fwd(x, g, *, name, tm=256):
    S, Dm = x.shape
    tm = min(tm, S)

    def body(x_ref, g_ref, o_ref):
        xv = x_ref[...]
        r = lax.rsqrt(jnp.mean(xv * xv, axis=-1, keepdims=True) + EPS)
        o_ref[...] = (xv * r * g_ref[...]).astype(BF16)

    return pl.pallas_call(
        body, grid=(S // tm,), in_specs=[pl.BlockSpec((tm, Dm), lambda i: (i, 0)), pl.BlockSpec((1, Dm), lambda i: (0, 0))],
        out_specs=pl.BlockSpec((tm, Dm), lambda i: (i, 0)), out_shape=jax.ShapeDtypeStruct((S, Dm), BF16),
        compiler_params=_cp(("parallel",)), name=name)(x, g.reshape(1, Dm))


def _rmsnorm_bwd(dh, x, g, dres, *, name, tm=256):
    S, Dm = x.shape
    tm = min(tm, S)
    want_dx = dres is not None

    def body(*refs):
        if want_dx:
            dh_ref, x_ref, g_ref, dr_ref, dx_ref, dg_ref = refs
        else:
            dh_ref, x_ref, g_ref, dg_ref = refs
        i = pl.program_id(0)
        xv = x_ref[...]
        dhv = dh_ref[...]
        r = lax.rsqrt(jnp.mean(xv * xv, axis=-1, keepdims=True) + EPS)
        y = xv * r
        part = jnp.sum(dhv * y, axis=0, keepdims=True)

        @pl.when(i == 0)
        def _():
            dg_ref[...] = jnp.zeros_like(dg_ref)

        dg_ref[...] += part
        if want_dx:
            dy = dhv * g_ref[...]
            dx_ref[...] = dr_ref[...] + r * (dy - y * jnp.mean(dy * y, axis=-1, keepdims=True))

    row = pl.BlockSpec((tm, Dm), lambda i: (i, 0))
    vec = pl.BlockSpec((1, Dm), lambda i: (0, 0))
    if want_dx:
        dx, dg = pl.pallas_call(
            body, grid=(S // tm,), in_specs=[row, row, vec, row], out_specs=(row, vec),
            out_shape=(jax.ShapeDtypeStruct((S, Dm), F32), jax.ShapeDtypeStruct((1, Dm), F32)),
            compiler_params=_cp(("arbitrary",)), name=name)(dh, x, g.reshape(1, Dm), dres)
        return dx, dg
    dg = pl.pallas_call(
        body, grid=(S // tm,), in_specs=[row, row, vec], out_specs=vec,
        out_shape=jax.ShapeDtypeStruct((1, Dm), F32), compiler_params=_cp(("arbitrary",)), name=name)(dh, x, g.reshape(1, Dm))
    return None, dg


GATE_TN = XA_WIDTH
GATE_MIX_TILES = MIX_WIDTH // GATE_TN


def _gate_cat_specs(tm):
    return [pl.BlockSpec((tm, GATE_TN), lambda i, j: (i, jnp.minimum(j, GATE_MIX_TILES - 1))),
            pl.BlockSpec((tm, GATE_TN), lambda i, j: (i, 0))]


def _gate_fwd(mix, xa, proj, z_off, *, name, tm=256):
    S = mix.shape[0]
    tm = min(tm, S)
    zb = z_off // GATE_TN

    def body(m_ref, x_ref, z_ref, y_ref):
        z = z_ref[...]
        c = jnp.where(pl.program_id(1) < GATE_MIX_TILES, m_ref[...], x_ref[...])
        y_ref[...] = (c * z * _sigmoid(z)).astype(BF16)

    blk = pl.BlockSpec((tm, GATE_TN), lambda i, j: (i, j))
    return pl.pallas_call(
        body, grid=(S // tm, INNER // GATE_TN),
        in_specs=_gate_cat_specs(tm) + [pl.BlockSpec((tm, GATE_TN), lambda i, j: (i, zb + j))],
        out_specs=blk, out_shape=jax.ShapeDtypeStruct((S, INNER), BF16),
        compiler_params=_cp(("parallel", "arbitrary")), name=name)(mix, xa, proj)


def _gate_bwd(dy, mix, xa, proj, z_off, *, name, tm=256):
    S = mix.shape[0]
    tm = min(tm, S)
    zb = z_off // GATE_TN

    def body(dy_ref, m_ref, x_ref, z_ref, dc_ref, dz_ref):
        z = z_ref[...]
        sg = _sigmoid(z)
        d = dy_ref[...]
        c = jnp.where(pl.program_id(1) < GATE_MIX_TILES, m_ref[...], x_ref[...])
        dc_ref[...] = d * z * sg
        dz_ref[...] = (d * c * sg * (1.0 + z * (1.0 - sg))).astype(BF16)

    blk = pl.BlockSpec((tm, GATE_TN), lambda i, j: (i, j))
    return pl.pallas_call(
        body, grid=(S // tm, INNER // GATE_TN),
        in_specs=[blk] + _gate_cat_specs(tm) + [pl.BlockSpec((tm, GATE_TN), lambda i, j: (i, zb + j))], out_specs=(blk, blk),
        out_shape=(jax.ShapeDtypeStruct((S, INNER), F32), jax.ShapeDtypeStruct((S, INNER), BF16)),
        compiler_params=_cp(("parallel", "arbitrary")), name=name)(dy, mix, xa, proj)


def _loss_head(x, target, *, name, tm=256):
    S, Dm = x.shape
    tm = min(tm, S)

    nt = S // tm

    def body(x_ref, t_ref, dx_ref, l_ref, acc):
        i = pl.program_id(0)
        e = x_ref[...] - t_ref[...]
        dx_ref[...] = e * (1.0 / Dm)

        @pl.when(i == 0)
        def _():
            acc[...] = jnp.zeros_like(acc)

        acc[...] += jnp.sum(e * e, axis=0, keepdims=True) * (0.5 / Dm)

        @pl.when(i == nt - 1)
        def _():
            l_ref[...] = jnp.sum(acc[...], axis=1, keepdims=True) + jnp.zeros((1, LANE), F32)

    row = pl.BlockSpec((tm, Dm), lambda i: (i, 0))
    return pl.pallas_call(
        body, grid=(nt,), in_specs=[row, row], out_specs=(row, pl.BlockSpec((1, LANE), lambda i: (0, 0))),
        out_shape=(jax.ShapeDtypeStruct((S, Dm), F32), jax.ShapeDtypeStruct((1, LANE), F32)),
        scratch_shapes=[pltpu.VMEM((1, Dm), F32)],
        compiler_params=_cp(("arbitrary",)), name=name)(x, target)


def _xa_norm(v, g):
    r = lax.rsqrt(jnp.mean(v * v, axis=-1, keepdims=True) + EPS)
    return v * r, r


def _xa_fwd(proj, xq_off, kv, gq, gk, *, name, tm=512):
    S = proj.shape[0]
    tm = min(tm, S)
    qb = xq_off // XA_DIM
    n_mem = kv.shape[0]
    scale = XA_DIM ** -0.5

    def body(q_ref, k_ref, v_ref, gq_ref, gk_ref, o_ref):
        qh, _ = _xa_norm(q_ref[...], None)
        kh, _ = _xa_norm(k_ref[...], None)
        qn = qh * gq_ref[...]
        kn = kh * gk_ref[...]
        s = _bdot(qn, kn, NT) * scale
        s = s - jnp.max(s, axis=-1, keepdims=True)
        p = jnp.exp(s)
        p = p / jnp.sum(p, axis=-1, keepdims=True)
        o_ref[...] = _bdot(p, v_ref[...], NN)

    vec = pl.BlockSpec((1, XA_DIM), lambda h, i: (0, 0))
    return pl.pallas_call(
        body, grid=(XA_HEADS, S // tm),
        in_specs=[pl.BlockSpec((tm, XA_DIM), lambda h, i: (i, qb + h)),
                  pl.BlockSpec((n_mem, XA_DIM), lambda h, i: (0, h)),
                  pl.BlockSpec((n_mem, XA_DIM), lambda h, i: (0, XA_HEADS + h)), vec, vec],
        out_specs=pl.BlockSpec((tm, XA_DIM), lambda h, i: (i, h)),
        out_shape=jax.ShapeDtypeStruct((S, XA_WIDTH), F32),
        compiler_params=_cp(("parallel", "parallel")), name=name)(proj, kv, kv, gq.reshape(1, XA_DIM), gk.reshape(1, XA_DIM))


def _xa_bwd(dcat, proj, xq_off, kv, gq, gk, *, name, tm=512):
    S = proj.shape[0]
    tm = min(tm, S)
    nt = S // tm
    qb = xq_off // XA_DIM
    db = MIX_WIDTH // XA_DIM
    n_mem = kv.shape[0]
    scale = XA_DIM ** -0.5

    def body(d_ref, q_ref, k_ref, v_ref, gq_ref, gk_ref, dq_ref, dk_ref, dv_ref, dgq_ref, dgk_ref, dkn_acc):
        h = pl.program_id(0)
        i = pl.program_id(1)
        q = q_ref[...]
        k = k_ref[...]
        qh, rq = _xa_norm(q, None)
        kh, rk = _xa_norm(k, None)
        gqv = gq_ref[...]
        gkv = gk_ref[...]
        qn = qh * gqv
        kn = kh * gkv
        s = _bdot(qn, kn, NT) * scale
        s = s - jnp.max(s, axis=-1, keepdims=True)
        p = jnp.exp(s)
        p = p / jnp.sum(p, axis=-1, keepdims=True)
        d = d_ref[...]
        dp = _bdot(d, v_ref[...], NT)
        ds = p * (dp - jnp.sum(dp * p, axis=-1, keepdims=True)) * scale
        dqn = _bdot(ds, kn, NN)

        @pl.when(i == 0)
        def _():
            dkn_acc[...] = jnp.zeros_like(dkn_acc)
            dv_ref[...] = jnp.zeros_like(dv_ref)

        @pl.when(jnp.logical_and(i == 0, h == 0))
        def _():
            dgq_ref[...] = jnp.zeros_like(dgq_ref)
            dgk_ref[...] = jnp.zeros_like(dgk_ref)

        dkn_acc[...] += _bdot(ds, qn, TN)
        dv_ref[...] += _bdot(p, d, TN)
        dgq_ref[...] += jnp.sum(dqn * qh, axis=0, keepdims=True)
        dy = dqn * gqv
        dq_ref[...] = (rq * (dy - qh * jnp.mean(dy * qh, axis=-1, keepdims=True))).astype(BF16)

        @pl.when(i == nt - 1)
        def _():
            dkn = dkn_acc[...]
            dgk_ref[...] += jnp.sum(dkn * kh, axis=0, keepdims=True)
            dyk = dkn * gkv
            dk_ref[...] = rk * (dyk - kh * jnp.mean(dyk * kh, axis=-1, keepdims=True))

    vec = pl.BlockSpec((1, XA_DIM), lambda h, i: (0, 0))
    kblk = pl.BlockSpec((n_mem, XA_DIM), lambda h, i: (0, h))
    vblk = pl.BlockSpec((n_mem, XA_DIM), lambda h, i: (0, XA_HEADS + h))
    dq, dk, dv, dgq, dgk = pl.pallas_call(
        body, grid=(XA_HEADS, nt),
        in_specs=[pl.BlockSpec((tm, XA_DIM), lambda h, i: (i, db + h)),
                  pl.BlockSpec((tm, XA_DIM), lambda h, i: (i, qb + h)), kblk, vblk, vec, vec],
        out_specs=(pl.BlockSpec((tm, XA_DIM), lambda h, i: (i, h)), kblk, kblk, vec, vec),
        out_shape=(jax.ShapeDtypeStruct((S, XA_WIDTH), BF16), jax.ShapeDtypeStruct((n_mem, XA_WIDTH), F32),
                   jax.ShapeDtypeStruct((n_mem, XA_WIDTH), F32), jax.ShapeDtypeStruct((1, XA_DIM), F32),
                   jax.ShapeDtypeStruct((1, XA_DIM), F32)),
        scratch_shapes=[pltpu.VMEM((n_mem, XA_DIM), F32)],
        compiler_params=_cp(("arbitrary", "arbitrary")), name=name)(
            dcat, proj, kv, kv, gq.reshape(1, XA_DIM), gk.reshape(1, XA_DIM))
    return dq, jnp.concatenate([dk, dv], axis=1), dgq, dgk


SB_TQ = 256
SB_TK = 256
SB_HEADS = 24


SB_PAIR = 2
SB_PW = SB_PAIR * HEAD_DIM


def _hdot(a, b, dims, dot=None):
    dot = dot or _bdot
    n = a.shape[0] if a.ndim == 3 else b.shape[0]
    return jnp.stack([dot(a[i] if a.ndim == 3 else a, b[i] if b.ndim == 3 else b, dims) for i in range(n)])


def _sb_tile(qi, kj, t0, s0, masked):
    z = _hdot(qi, kj, NT)
    sp = _softplus(z)
    ls = z - sp
    if not masked:
        return -sp, ls, None
    mask = (s0 + _iota2(z.shape[1:], 1)) < (t0 + _iota2(z.shape[1:], 0))
    return jnp.where(mask, -sp, 0.0), ls, mask


def _dot2(x, tri):
    hi = x.astype(BF16)
    lo = (x - hi.astype(F32)).astype(BF16)
    plain = lambda u, v, dims: lax.dot_general(u, v, (dims, ((), ())), preferred_element_type=F32)
    return _hdot(hi, tri, NN, plain) + _hdot(lo, tri, NN, plain)


def _sb_heads(ref, rows=slice(None)):
    return jnp.stack([ref[rows, hh * HEAD_DIM:(hh + 1) * HEAD_DIM] for hh in range(SB_PAIR)])


def _sb_fwd(proj, gq, gk, *, name):
    S = proj.shape[0]
    tq, tk = min(SB_TQ, S), min(SB_TK, S)
    nq = S // tq
    scale = HEAD_DIM ** -0.5

    def body(q_ref, k_ref, v_ref, gq_ref, gk_ref, o_ref, tot_ref, qn_s, kn_s, v_s):
        q = _sb_heads(q_ref)
        k = _sb_heads(k_ref)
        qn_s[...] = (q * lax.rsqrt(jnp.mean(q * q, axis=-1, keepdims=True) + EPS) * (gq_ref[...] * scale)).astype(BF16)
        kn_s[...] = (k * lax.rsqrt(jnp.mean(k * k, axis=-1, keepdims=True) + EPS) * gk_ref[...]).astype(BF16)
        v_s[...] = _sb_heads(v_ref).astype(BF16)
        after = (_iota2((tk, tk), 0) > _iota2((tk, tk), 1)).astype(BF16)

        def qblock(i, _):
            rows = pl.ds(pl.multiple_of(i * tq, tq), tq)
            qi = qn_s[:, rows, :]
            jd = (i * tq) // tk

            def tile(j, acc, run, masked):
                cols = pl.ds(pl.multiple_of(j * tk, tk), tk)
                lr, ls, mask = _sb_tile(qi, kn_s[:, cols, :], i * tq, j * tk, masked)
                later = _dot2(lr, after) + run
                a = jnp.exp(ls + later)
                if masked:
                    a = jnp.where(mask, a, 0.0)
                acc = acc + _hdot(a, v_s[:, cols, :], NN)
                return acc, run + jnp.sum(lr, axis=-1, keepdims=True)

            acc, run = tile(jd, jnp.zeros((SB_PAIR, tq, HEAD_DIM), F32), jnp.zeros((SB_PAIR, tq, 1), F32), True)
            acc, run = lax.fori_loop(0, jd, lambda jj, c: tile(jd - 1 - jj, c[0], c[1], False), (acc, run))
            tot = run + jnp.zeros((SB_PAIR, tq, HEAD_DIM), F32)
            for hh in range(SB_PAIR):
                o_ref[rows, hh * HEAD_DIM:(hh + 1) * HEAD_DIM] = acc[hh]
                tot_ref[rows, hh * HEAD_DIM:(hh + 1) * HEAD_DIM] = tot[hh]
            return 0

        lax.fori_loop(0, nq, qblock, 0)

    npair = SB_HEADS // SB_PAIR
    vec = pl.BlockSpec((1, HEAD_DIM), lambda h: (0, 0))
    hb = lambda off: pl.BlockSpec((S, SB_PW), lambda h: (0, off + h), pipeline_mode=pl.Buffered(1))
    return pl.pallas_call(
        body, grid=(npair,), in_specs=[hb(0), hb(npair), hb(2 * npair), vec, vec],
        out_specs=(hb(0), hb(0)), out_shape=(jax.ShapeDtypeStruct((S, MIX_WIDTH), F32),) * 2,
        scratch_shapes=[pltpu.VMEM((SB_PAIR, S, HEAD_DIM), BF16)] * 3,
        compiler_params=_cp(("parallel",)), name=name)(proj, proj, proj, gq.reshape(1, HEAD_DIM), gk.reshape(1, HEAD_DIM))


def _sb_bwd(dmix, tot, proj, gq, gk, *, name):
    S = proj.shape[0]
    tq, tk = min(SB_TQ, S), min(SB_TK, S)
    nq = S // tq
    scale = HEAD_DIM ** -0.5

    def body(do_ref, o_ref, q_ref, k_ref, v_ref, gq_ref, gk_ref, dq_ref, dk_ref, dv_ref, dgq_ref, dgk_ref,
             qn_s, kn_s, v_s, dkn_s, dqn_s, dv_s):
        h = pl.program_id(0)
        q = _sb_heads(q_ref)
        k = _sb_heads(k_ref)
        rq = lax.rsqrt(jnp.mean(q * q, axis=-1, keepdims=True) + EPS)
        rk = lax.rsqrt(jnp.mean(k * k, axis=-1, keepdims=True) + EPS)
        gqv = gq_ref[...]
        gkv = gk_ref[...]
        qn_s[...] = (q * rq * (gqv * scale)).astype(BF16)
        kn_s[...] = (k * rk * gkv).astype(BF16)
        v_s[...] = _sb_heads(v_ref).astype(BF16)
        dkn_s[...] = jnp.zeros_like(dkn_s)
        dv_s[...] = jnp.zeros_like(dv_s)
        r_i = _iota2((tk, tk), 0)
        c_i = _iota2((tk, tk), 1)
        upto = (r_i <= c_i).astype(BF16)
        before = (r_i < c_i).astype(BF16)

        def qblock(i, _):
            rows = pl.ds(pl.multiple_of(i * tq, tq), tq)
            qi = qn_s[:, rows, :]
            doi = _sb_heads(do_ref, rows).astype(BF16)
            tot_i = jnp.max(_sb_heads(o_ref, rows), axis=-1, keepdims=True)
            jd = (i * tq) // tk

            def tile(j, dqn, run, run_b, masked):
                cols = pl.ds(pl.multiple_of(j * tk, tk), tk)
                kj = kn_s[:, cols, :]
                lr, ls, mask = _sb_tile(qi, kj, i * tq, j * tk, masked)
                later = tot_i - (_dot2(lr, upto) + run)
                a = jnp.exp(ls + later)
                if masked:
                    a = jnp.where(mask, a, 0.0)
                b = _hdot(doi, v_s[:, cols, :], NT) * a
                cum = _dot2(b, before) + run_b
                beta = jnp.exp(ls)
                dz = b * (1.0 - beta) - cum * beta
                if masked:
                    dz = jnp.where(mask, dz, 0.0)
                dzb = dz.astype(BF16)
                dv_s[:, cols, :] += _hdot(a, doi, TN)
                dkn_s[:, cols, :] += _hdot(dzb, qi, TN)
                dqn = dqn + _hdot(dzb, kj, NN)
                return dqn, run + jnp.sum(lr, axis=-1, keepdims=True), run_b + jnp.sum(b, axis=-1, keepdims=True)

            zero1 = jnp.zeros((SB_PAIR, tq, 1), F32)
            carry = lax.fori_loop(0, jd, lambda j, c: tile(j, c[0], c[1], c[2], False),
                                  (jnp.zeros((SB_PAIR, tq, HEAD_DIM), F32), zero1, zero1))
            dqn, _, _ = tile(jd, carry[0], carry[1], carry[2], True)
            dqn_s[:, rows, :] = dqn * scale
            return 0

        lax.fori_loop(0, nq, qblock, 0)

        @pl.when(h == 0)
        def _():
            dgq_ref[...] = jnp.zeros_like(dgq_ref)
            dgk_ref[...] = jnp.zeros_like(dgk_ref)

        heads_sum = lambda z: jnp.sum(jnp.sum(z, axis=1, keepdims=True), axis=0)
        dqn = dqn_s[...]
        qh = q * rq
        dgq_ref[...] += heads_sum(dqn * qh)
        dy = dqn * gqv
        dq = (rq * (dy - qh * jnp.mean(dy * qh, axis=-1, keepdims=True))).astype(BF16)
        dkn = dkn_s[...]
        kh = k * rk
        dgk_ref[...] += heads_sum(dkn * kh)
        dyk = dkn * gkv
        dk = (rk * (dyk - kh * jnp.mean(dyk * kh, axis=-1, keepdims=True))).astype(BF16)
        dv = dv_s[...].astype(BF16)
        for hh in range(SB_PAIR):
            lanes = slice(hh * HEAD_DIM, (hh + 1) * HEAD_DIM)
            dq_ref[:, lanes] = dq[hh]
            dk_ref[:, lanes] = dk[hh]
            dv_ref[:, lanes] = dv[hh]

    npair = SB_HEADS // SB_PAIR
    vec = pl.BlockSpec((1, HEAD_DIM), lambda h: (0, 0))
    hb = lambda off: pl.BlockSpec((S, SB_PW), lambda h: (0, off + h), pipeline_mode=pl.Buffered(1))
    dq, dk, dv, dgq, dgk = pl.pallas_call(
        body, grid=(npair,),
        in_specs=[hb(0), hb(0), hb(0), hb(npair), hb(2 * npair), vec, vec],
        out_specs=(hb(0), hb(0), hb(0), vec, vec),
        out_shape=(jax.ShapeDtypeStruct((S, MIX_WIDTH), BF16),) * 3 + (jax.ShapeDtypeStruct((1, HEAD_DIM), F32),) * 2,
        scratch_shapes=[pltpu.VMEM((SB_PAIR, S, HEAD_DIM), BF16)] * 3 + [pltpu.VMEM((SB_PAIR, S, HEAD_DIM), F32)] * 3,
        compiler_params=_cp(("arbitrary",)), name=name)(
            dmix, tot, proj, proj, proj, gq.reshape(1, HEAD_DIM), gk.reshape(1, HEAD_DIM))
    return [dq, dk, dv], dgq, dgk


def _shift_down(x, k):
    if k == 0:
        return x
    r = pltpu.roll(x, k, 0)
    return jnp.where(_iota2(x.shape, 0) >= k, r, 0.0)


def _shift_up(x, k):
    if k == 0:
        return x
    n = x.shape[0]
    r = pltpu.roll(x, n - k, 0)
    return jnp.where(_iota2(x.shape, 0) < n - k, r, 0.0)


def _conv(x, w):
    c = w[DN_CONV - 1] * x
    for k in range(1, DN_CONV):
        c = c + w[DN_CONV - 1 - k] * _shift_down(x, k)
    return c


def _dn_pre_fwd(proj, conv_w, col0, ncols, *, l2, scale, name):
    S = proj.shape[0]
    cb = col0 // HEAD_DIM

    def body(x_ref, w_ref, o_ref):
        c = _conv(x_ref[...], [w_ref[k:k + 1, :] for k in range(DN_CONV)])
        a = c * _sigmoid(c)
        if l2:
            a = a * (lax.rsqrt(jnp.sum(a * a, axis=-1, keepdims=True) + EPS) * scale)
        o_ref[...] = a

    return pl.pallas_call(
        body, grid=(ncols // HEAD_DIM,),
        in_specs=[pl.BlockSpec((S, HEAD_DIM), lambda j: (0, cb + j)), pl.BlockSpec((DN_CONV, HEAD_DIM), lambda j: (0, cb + j))],
        out_specs=pl.BlockSpec((S, HEAD_DIM), lambda j: (0, j)), out_shape=jax.ShapeDtypeStruct((S, ncols), F32),
        compiler_params=_cp(("parallel",)), name=name)(proj, conv_w)


def _dn_pre_bwd(dout, proj, conv_w, col0, ncols, *, l2, scale, name):
    S = proj.shape[0]
    cb = col0 // HEAD_DIM
    dw_in = HEAD_DIM

    def body(d_ref, x_ref, w_ref, dx_ref, dw_ref):
        x = x_ref[...]
        w = [w_ref[k:k + 1, :] for k in range(DN_CONV)]
        c = _conv(x, w)
        sg = _sigmoid(c)
        a = c * sg
        d = d_ref[...]
        if l2:
            r = lax.rsqrt(jnp.sum(a * a, axis=-1, keepdims=True) + EPS)
            y = a * r
            d = d * scale
            d = r * (d - y * jnp.sum(d * y, axis=-1, keepdims=True))
        dc = d * sg * (1.0 + c * (1.0 - sg))
        dx = w[DN_CONV - 1] * dc
        for k in range(1, DN_CONV):
            dx = dx + w[DN_CONV - 1 - k] * _shift_up(dc, k)
        dx_ref[...] = dx.astype(BF16)
        for k in range(DN_CONV):
            dw_ref[3 - k:4 - k, :] = jnp.sum(dc * _shift_down(x, k), axis=0, keepdims=True)

    return pl.pallas_call(
        body, grid=(ncols // HEAD_DIM,),
        in_specs=[pl.BlockSpec((S, dw_in), lambda j: (0, j)), pl.BlockSpec((S, HEAD_DIM), lambda j: (0, cb + j)),
                  pl.BlockSpec((DN_CONV, HEAD_DIM), lambda j: (0, cb + j))],
        out_specs=(pl.BlockSpec((S, HEAD_DIM), lambda j: (0, j)), pl.BlockSpec((DN_CONV, HEAD_DIM), lambda j: (0, j))),
        out_shape=(jax.ShapeDtypeStruct((S, ncols), BF16), jax.ShapeDtypeStruct((DN_CONV, ncols), F32)),
        compiler_params=_cp(("parallel",)), name=name)(dout, proj, conv_w)


def _dn_ab_fwd(proj, a_log, dt_bias, *, name, tm=512):
    S = proj.shape[0]
    tm = min(tm, S)
    ab = P0_AB // LANE

    def body(a_ref, b_ref, al_ref, dt_ref, g_ref, be_ref):
        g_ref[...] = -jnp.exp(al_ref[...]) * _softplus(a_ref[...] + dt_ref[...])
        be_ref[...] = _sigmoid(b_ref[...])

    vec = pl.BlockSpec((1, LANE), lambda i: (0, 0))
    out = pl.BlockSpec((tm, LANE), lambda i: (i, 0))
    return pl.pallas_call(
        body, grid=(S // tm,),
        in_specs=[pl.BlockSpec((tm, LANE), lambda i: (i, ab)), pl.BlockSpec((tm, LANE), lambda i: (i, ab + 1)), vec, vec],
        out_specs=(out, out), out_shape=(jax.ShapeDtypeStruct((S, LANE), F32),) * 2,
        compiler_params=_cp(("parallel",)), name=name)(proj, proj, a_log, dt_bias)


def _dn_ab_bwd(dg, dbeta, proj, a_log, dt_bias, *, name, tm=512):
    S = proj.shape[0]
    tm = min(tm, S)
    ab = P0_AB // LANE

    def body(dg_ref, db_ref, a_ref, b_ref, al_ref, dt_ref, dab_ref, dal_ref, ddt_ref):
        i = pl.program_id(0)
        ea = jnp.exp(al_ref[...])
        u = a_ref[...] + dt_ref[...]
        dgv = dg_ref[...]
        da = dgv * (-ea) * _sigmoid(u)
        be = _sigmoid(b_ref[...])
        dab_ref[:, 0:LANE] = da.astype(BF16)
        dab_ref[:, LANE:2 * LANE] = (db_ref[...] * be * (1.0 - be)).astype(BF16)
        dab_ref[:, 2 * LANE:] = jnp.zeros((tm, 2 * LANE), BF16)

        @pl.when(i == 0)
        def _():
            dal_ref[...] = jnp.zeros_like(dal_ref)
            ddt_ref[...] = jnp.zeros_like(ddt_ref)

        dal_ref[...] += jnp.sum(dgv * (-ea) * _softplus(u), axis=0, keepdims=True)
        ddt_ref[...] += jnp.sum(da, axis=0, keepdims=True)

    vec = pl.BlockSpec((1, LANE), lambda i: (0, 0))
    row = pl.BlockSpec((tm, LANE), lambda i: (i, 0))
    return pl.pallas_call(
        body, grid=(S // tm,),
        in_specs=[row, row, pl.BlockSpec((tm, LANE), lambda i: (i, ab)), pl.BlockSpec((tm, LANE), lambda i: (i, ab + 1)), vec, vec],
        out_specs=(pl.BlockSpec((tm, 4 * LANE), lambda i: (i, 0)), vec, vec),
        out_shape=(jax.ShapeDtypeStruct((S, 4 * LANE), BF16), jax.ShapeDtypeStruct((1, LANE), F32),
                   jax.ShapeDtypeStruct((1, LANE), F32)),
        compiler_params=_cp(("arbitrary",)), name=name)(dg, dbeta, proj, proj, a_log, dt_bias)


def _dot3(a, b):
    ah = a.astype(BF16)
    al = (a - ah.astype(F32)).astype(BF16)
    bh = b.astype(BF16)
    bl = (b - bh.astype(F32)).astype(BF16)
    d = lambda u, v: lax.dot_general(u, v, (NN, ((), ())), preferred_element_type=F32)
    return d(ah, bh) + (d(ah, bl) + d(al, bh))


DN_PAIR = 4
DN_QK = DN_PAIR // 2


def _dn_qk_heads(ref, rows):
    return jnp.stack([ref[rows, (hh // 2) * HEAD_DIM:(hh // 2 + 1) * HEAD_DIM] for hh in range(DN_PAIR)])


def _dn_big(shape, imap):
    return pl.BlockSpec(shape, imap, pipeline_mode=pl.Buffered(1))


_pdot = _hdot


def _tri_inverse(a):
    eye = (_iota2((CH, CH), 0) == _iota2((CH, CH), 1)).astype(F32)
    d3 = lambda u, v: jnp.stack([_dot3(u[i], v[i]) for i in range(DN_PAIR)])
    t = eye - a
    x = d3(a, a)
    n = 2
    while True:
        t = t + d3(t, x)
        n *= 2
        if n >= CH:
            break
        x = d3(x, x)
    return t


def _pick_col(m, n):
    return jnp.sum(jnp.where(_iota2(m.shape, 2) == n, m, 0.0), axis=2, keepdims=True)


def _dn_chunk_common(kk, qk, gc_c, gc_r, be_c):
    r_i = _iota2((CH, CH), 0)
    c_i = _iota2((CH, CH), 1)
    incl = r_i >= c_i
    strict = r_i > c_i
    dec = jnp.exp(jnp.where(incl, gc_c - gc_r, -1e30))
    e = jnp.exp(gc_c)
    gl = jnp.sum(jnp.where(_iota2((1, CH), 1) == CH - 1, gc_r, 0.0), axis=-1, keepdims=True)
    kds = jnp.exp(gl - gc_c)
    cd = jnp.exp(gl)
    a = jnp.where(strict, be_c * kk * dec, 0.0)
    p = qk * dec
    return dict(incl=incl, strict=strict, dec=dec, e=e, kds=kds, cd=cd, kk=kk, a=a, qk=qk, p=p)


def _dn_decay_tables(g_ref, b_ref, gcr, gcc, bcc):
    r_i = _iota2((CH, CH), 0)
    c_i = _iota2((CH, CH), 1)
    lc = (r_i >= c_i).astype(F32)
    eye = (r_i == c_i).astype(F32)
    for hh in range(DN_PAIR):
        g_rows_v = g_ref[hh]
        gcr[hh] = _fdot(g_rows_v, lc, NT)
        gcc[hh] = _fdot(lc, g_rows_v, NT)
        bcc[hh] = _fdot(eye, b_ref[hh], NT)
    return lc


def _dn_core_fwd(qn, kn, vc, g_rows, b_rows, out_g, *, name):
    S = qn.shape[0]
    nc = S // CH

    def body(q_ref, k_ref, v_ref, g_ref, b_ref, og_ref, o_ref, st_ref, t_ref, gcr, gcc, bcc):
        _dn_decay_tables(g_ref, b_ref, gcr, gcc, bcc)
        ogv = og_ref[...]

        def chunk(n, states):
            rows = pl.ds(pl.multiple_of(n * CH, CH), CH)
            q = _dn_qk_heads(q_ref, rows)
            k = _dn_qk_heads(k_ref, rows)
            kk = _pdot(k, k, NT)
            qk = _pdot(q, k, NT)
            v = jnp.stack([v_ref[rows, hh * HEAD_DIM:(hh + 1) * HEAD_DIM] for hh in range(DN_PAIR)])
            gc_c = _pick_col(gcc[...], n)
            be_c = _pick_col(bcc[...], n)
            gc_r = gcr[:, pl.ds(n, 1), :]
            c = _dn_chunk_common(kk, qk, gc_c, gc_r, be_c)
            t = _tri_inverse(c["a"])
            u0 = _pdot(t, be_c * v, NN)
            w = _pdot(t, (be_c * c["e"]) * k, NN)
            u = u0 - _pdot(w, states, NN)
            o = _pdot(c["e"] * q, states, NN) + _pdot(c["p"], u, NN)
            on = o * lax.rsqrt(jnp.mean(o * o, axis=-1, keepdims=True) + EPS) * ogv
            for hh in range(DN_PAIR):
                st_ref[hh, n] = states[hh]
                t_ref[hh, n] = t[hh]
                o_ref[rows, hh * HEAD_DIM:(hh + 1) * HEAD_DIM] = on[hh]
            return c["cd"] * states + _pdot(c["kds"] * k, u, TN)

        lax.fori_loop(0, nc, chunk, jnp.zeros((DN_PAIR, HEAD_DIM, HEAD_DIM), F32))

    qk_spec = pl.BlockSpec((S, DN_QK * HEAD_DIM), lambda h: (0, h))
    v_in = pl.BlockSpec((S, DN_PAIR * HEAD_DIM), lambda h: (0, h))
    v_spec = _dn_big((S, DN_PAIR * HEAD_DIM), lambda h: (0, h))
    rows_spec = pl.BlockSpec((DN_PAIR, LANE, CH), lambda h: (h, 0, 0))
    return pl.pallas_call(
        body, grid=(DN_V_HEADS // DN_PAIR,),
        in_specs=[qk_spec, qk_spec, v_in, rows_spec, rows_spec, pl.BlockSpec((1, HEAD_DIM), lambda h: (0, 0))],
        out_specs=(v_spec, _dn_big((DN_PAIR, nc, HEAD_DIM, HEAD_DIM), lambda h: (h, 0, 0, 0)),
                   _dn_big((DN_PAIR, nc, CH, CH), lambda h: (h, 0, 0, 0))),
        out_shape=(jax.ShapeDtypeStruct((S, MIX_WIDTH), F32), jax.ShapeDtypeStruct((DN_V_HEADS, nc, HEAD_DIM, HEAD_DIM), F32),
                   jax.ShapeDtypeStruct((DN_V_HEADS, nc, CH, CH), F32)),
        scratch_shapes=[pltpu.VMEM((DN_PAIR, LANE, CH), F32), pltpu.VMEM((DN_PAIR, CH, LANE), F32),
                        pltpu.VMEM((DN_PAIR, CH, LANE), F32)],
        compiler_params=_cp(("parallel",)), name=name)(qn, kn, vc, g_rows, b_rows, out_g.reshape(1, HEAD_DIM))


def _dn_chunk_bwd(q, k, v, kk, qk, state, t, gc_c, gc_r, be_c, don, ogv, ds_next):
    ones = jnp.ones((CH, LANE), F32)
    last_row = _iota2((CH, 1), 0) == CH - 1
    rowsum = lambda z: jnp.sum(z, axis=-1, keepdims=True)
    colsum = lambda z: jnp.sum(z, axis=-2, keepdims=True)
    c = _dn_chunk_common(kk, qk, gc_c, gc_r, be_c)
    e, kds, cd, dec, a, p = c["e"], c["kds"], c["cd"], c["dec"], c["a"], c["p"]
    vb = be_c * v
    kbe = (be_c * e) * k
    u0 = _pdot(t, vb, NN)
    w = _pdot(t, kbe, NN)
    u = u0 - _pdot(w, state, NN)
    qd = e * q
    kd = kds * k
    o = _pdot(qd, state, NN) + _pdot(p, u, NN)
    r = lax.rsqrt(jnp.mean(o * o, axis=-1, keepdims=True) + EPS)
    y = o * r
    dog = colsum(don * y)
    dy = don * ogv
    d_o = r * (dy - y * jnp.mean(dy * y, axis=-1, keepdims=True))
    du = _pdot(p, d_o, TN) + _pdot(kd, ds_next, NN)
    dqd = _pdot(d_o, state, NT)
    dstate = _pdot(qd, d_o, TN) + cd * ds_next - _pdot(w, du, TN)
    dcd = colsum(rowsum(ds_next * state))
    dkd = _pdot(u, ds_next, NT)
    dw = -_pdot(du, state, NT)
    dvb = _pdot(t, du, TN)
    dkbe = _pdot(t, dw, TN)
    da = -jnp.where(c["strict"], _pdot(dvb, u0, NT) + _pdot(dkbe, w, NT), 0.0)
    dp = jnp.where(c["incl"], _pdot(d_o, u, NT), 0.0)
    gmat = da * a + dp * p
    dad = da * dec
    x = be_c * dad
    dpd = dp * dec
    dk = _pdot(x, k, NN) + _pdot(x, k, TN) + _pdot(dpd, q, TN)
    dq = _pdot(dpd, k, NN) + e * dqd
    dbe = rowsum(dad * c["kk"])
    dgc = rowsum(gmat) + rowsum(dqd * q) * e
    rk = rowsum(dkd * k) * kds
    dk = dk + kds * dkd
    dgc = dgc - rk
    dgl = colsum(rk) + dcd * cd
    sk = rowsum(dkbe * k)
    dk = dk + (be_c * e) * dkbe
    dbe = dbe + sk * e + rowsum(dvb * v)
    dgc = dgc + sk * be_c * e
    dgc = dgc + jnp.where(last_row, dgl, 0.0)
    dgc = dgc - _pdot(gmat, ones, TN, dot=_fdot)
    return dq, dk, be_c * dvb, dgc, dbe, dog, dstate


def _dn_core_bwd(dmix, qn, kn, vc, g_rows, b_rows, out_g, states, tinv, *, name):
    S = qn.shape[0]
    nc = S // CH

    def body(do_ref, q_ref, k_ref, v_ref, g_ref, b_ref, og_ref, st_ref, t_ref,
             dq_ref, dk_ref, dv_ref, dg_ref, db_ref, dog_ref, gcr, gcc, bcc, dgc_acc):
        h = pl.program_id(0)
        lc = _dn_decay_tables(g_ref, b_ref, gcr, gcc, bcc)
        ogv = og_ref[...]
        dgc_acc[...] = jnp.zeros_like(dgc_acc)
        db_ref[...] = jnp.zeros_like(db_ref)
        lane_n = _iota2((CH, LANE), 1)

        @pl.when(h == 0)
        def _():
            dog_ref[...] = jnp.zeros_like(dog_ref)

        def chunk(m, carry):
            ds_nexts, dog = carry
            n = nc - 1 - m
            rows = pl.ds(pl.multiple_of(n * CH, CH), CH)
            q = _dn_qk_heads(q_ref, rows)
            k = _dn_qk_heads(k_ref, rows)
            kk = _pdot(k, k, NT)
            qk = _pdot(q, k, NT)
            heads = lambda ref: jnp.stack([ref[rows, hh * HEAD_DIM:(hh + 1) * HEAD_DIM] for hh in range(DN_PAIR)])
            state = jnp.stack([st_ref[hh, n] for hh in range(DN_PAIR)])
            t = jnp.stack([t_ref[hh, n] for hh in range(DN_PAIR)])
            dq, dk, dv, dgc, dbe, dog_h, dstate = _dn_chunk_bwd(
                q, k, heads(v_ref), kk, qk, state, t, _pick_col(gcc[...], n), gcr[:, pl.ds(n, 1), :],
                _pick_col(bcc[...], n), heads(do_ref), ogv, ds_nexts)
            for hh in range(DN_PAIR):
                dv_ref[rows, hh * HEAD_DIM:(hh + 1) * HEAD_DIM] = dv[hh]
            dgc_acc[...] = jnp.where(lane_n == n, dgc, dgc_acc[...])
            db_ref[...] = jnp.where(lane_n == n, dbe, db_ref[...])
            for i in range(DN_QK):
                dq_ref[rows, i * HEAD_DIM:(i + 1) * HEAD_DIM] = dq[2 * i] + dq[2 * i + 1]
                dk_ref[rows, i * HEAD_DIM:(i + 1) * HEAD_DIM] = dk[2 * i] + dk[2 * i + 1]
            return dstate, dog + jnp.sum(dog_h, axis=0)

        _, dog = lax.fori_loop(0, nc, chunk, (jnp.zeros((DN_PAIR, HEAD_DIM, HEAD_DIM), F32), jnp.zeros((1, HEAD_DIM), F32)))
        dog_ref[...] += dog
        for hh in range(DN_PAIR):
            dg_ref[hh] = _fdot(lc, dgc_acc[hh], TN)

    qk_spec = _dn_big((S, DN_QK * HEAD_DIM), lambda h: (0, h))
    v_spec = _dn_big((S, DN_PAIR * HEAD_DIM), lambda h: (0, h))
    rows_spec = pl.BlockSpec((DN_PAIR, LANE, CH), lambda h: (h, 0, 0))
    cols_spec = pl.BlockSpec((DN_PAIR, CH, LANE), lambda h: (h, 0, 0))
    vec = pl.BlockSpec((1, HEAD_DIM), lambda h: (0, 0))
    qk_out = jax.ShapeDtypeStruct((S, DN_QK_WIDTH), F32)
    return pl.pallas_call(
        body, grid=(DN_V_HEADS // DN_PAIR,),
        in_specs=[v_spec, qk_spec, qk_spec, v_spec, rows_spec, rows_spec, vec,
                  _dn_big((DN_PAIR, nc, HEAD_DIM, HEAD_DIM), lambda h: (h, 0, 0, 0)),
                  _dn_big((DN_PAIR, nc, CH, CH), lambda h: (h, 0, 0, 0))],
        out_specs=(qk_spec, qk_spec, v_spec, cols_spec, cols_spec, vec),
        out_shape=(qk_out, qk_out, jax.ShapeDtypeStruct((S, MIX_WIDTH), F32), jax.ShapeDtypeStruct((DN_V_HEADS, CH, LANE), F32),
                   jax.ShapeDtypeStruct((DN_V_HEADS, CH, LANE), F32), jax.ShapeDtypeStruct((1, HEAD_DIM), F32)),
        scratch_shapes=[pltpu.VMEM((DN_PAIR, LANE, CH), F32), pltpu.VMEM((DN_PAIR, CH, LANE), F32),
                        pltpu.VMEM((DN_PAIR, CH, LANE), F32), pltpu.VMEM((DN_PAIR, CH, LANE), F32)],
        compiler_params=_cp(("arbitrary",)), name=name)(
            dmix, qn, kn, vc, g_rows, b_rows, out_g.reshape(1, HEAD_DIM), states, tinv)


def _rows_form(x, nc):
    t = x[:, :DN_V_HEADS].T.reshape(DN_V_HEADS, nc, CH)
    return jnp.pad(t, ((0, 0), (0, LANE - nc), (0, 0)))


def _cols_to_nat(x, nc):
    t = jnp.transpose(x[:, :, :nc], (2, 1, 0)).reshape(nc * CH, DN_V_HEADS)
    return jnp.pad(t, ((0, 0), (0, LANE - DN_V_HEADS)))


_C_QKV = 2 * DN_QK_WIDTH + MIX_WIDTH


def _padded_pieces(lo, hi):
    a0, b0, x0 = _C_QKV, _C_QKV + DN_V_HEADS, _C_QKV + 2 * DN_V_HEADS
    out = []
    for t0, t1, shift in ((0, a0, 0), (a0, b0, P0_AB - a0), (b0, x0, P0_AB + LANE - b0), (x0, DN_PROJ, a0 - x0)):
        s, e = max(lo, t0), min(hi, t1)
        if s < e:
            out.append((s + shift, e + shift))
    return out


def _shard_pieces(s):
    return _padded_pieces(s * P0_SHARD, (s + 1) * P0_SHARD)


def _shard_runs(s):
    return [(lo // LANE, -(-hi // LANE)) for lo, hi in _shard_pieces(s)]


WIN_COLS = LANE * max(sum(b - a for a, b in _shard_runs(s)) for s in range(N_CHIPS))


def _pack_own_shard(s, shard):
    zeros = lambda n: jnp.zeros((shard.shape[0], n), shard.dtype)
    out, t = [], 0
    for (lo, hi), (b0, b1) in zip(_shard_pieces(s), _shard_runs(s)):
        out += [zeros(lo - b0 * LANE), shard[:, t:t + hi - lo], zeros(b1 * LANE - hi)]
        t += hi - lo
    out.append(zeros(WIN_COLS - LANE * sum(b - a for a, b in _shard_runs(s))))
    return jnp.concatenate([o for o in out if o.shape[1]], axis=1)


def _unpack_own_shard(s, win):
    out, off = [], 0
    for (lo, hi), (b0, b1) in zip(_shard_pieces(s), _shard_runs(s)):
        start = off + lo - b0 * LANE
        out.append(win[:, start:start + hi - lo])
        off += (b1 - b0) * LANE
    return jnp.concatenate(out, axis=1)


def _windows_to_padded(wins):
    src = {}
    for s in range(N_CHIPS):
        off = 0
        for b0, b1 in _shard_runs(s):
            for b in range(b0, b1):
                src.setdefault(b, []).append((s, off + (b - b0) * LANE))
            off += (b1 - b0) * LANE
    out, b = [], 0
    while b < P0 // LANE:
        if b not in src:
            out.append(jnp.zeros((wins.shape[1], LANE), wins.dtype))
            b += 1
        elif len(src[b]) > 1:
            out.append(sum(wins[s][:, o:o + LANE] for s, o in src[b]))
            b += 1
        else:
            (s, o), n = src[b][0], 1
            while src.get(b + n) == [(s, o + n * LANE)]:
                n += 1
            out.append(wins[s][:, o:o + n * LANE])
            b += n
    return jnp.concatenate(out, axis=1)


def _padded_to_window(s, g):
    out = [g[:, b0 * LANE:b1 * LANE] for b0, b1 in _shard_runs(s)]
    rest = WIN_COLS - sum(o.shape[1] for o in out)
    return jnp.concatenate(out + ([jnp.zeros((g.shape[0], rest), g.dtype)] if rest else []), axis=1)


def _pad_lane(v):
    v = v.reshape(1, -1)
    return jnp.pad(v, ((0, 0), (0, LANE - v.shape[1])))


SLOT1 = SB_PROJ // N_CHIPS
MM_TN = 512


def _local_step(x, mem, target, norm_g, mem_norm_g, xa_q_g, xa_k_g, w_in0, conv_w, a_log, dt_bias, out_g, sb_q_g, sb_k_g,
                late_weights, early_grads, grads_swapped):
    S = x.shape[0]
    nc = S // CH
    al = _pad_lane(a_log)
    dtb = _pad_lane(dt_bias)
    q_scale = HEAD_DIM ** -0.5
    tiles1 = SLOT1 // MM_TN

    kv_rhs = lambda l: pl.BlockSpec((N_CHIPS, None, D_MODEL // N_CHIPS, MM_TN), lambda i, j: (0, l, 0, j))
    kv_rhs_t = lambda l: pl.BlockSpec((None, None, D_MODEL // N_CHIPS, 2 * XA_WIDTH), lambda i, j: (j, l, 0, 0))
    out_rhs = lambda l: pl.BlockSpec((N_CHIPS, None, INNER // N_CHIPS, MM_TN), lambda i, j: (0, l, 0, j))
    out_rhs_t = lambda l: pl.BlockSpec((None, None, MM_TN, D_MODEL), lambda i, j: (j // 2, l, j % 2, 0))
    in1_rhs = pl.BlockSpec((None, 2, D_MODEL // 2, MM_TN), lambda i, j: (j // tiles1, 0, 0, j % tiles1))
    in1_rhs_t = pl.BlockSpec((None, None, D_MODEL // 2, MM_TN), lambda i, j, k: (k // tiles1, j, 0, k % tiles1))
    slot_rows = lambda rows: dict(
        tm=rows, o_spec=pl.BlockSpec((None, rows, MM_TN), lambda i, j: (i, 0, j)),
        o_shape=jax.ShapeDtypeStruct((N_CHIPS, rows, 2 * XA_WIDTH), BF16))
    in1_out = dict(tm=D_MODEL // 2, o_spec=pl.BlockSpec((None, None, D_MODEL // 2, MM_TN),
                                                        lambda i, j: (j // tiles1, i, 0, j % tiles1)),
                   o_shape=jax.ShapeDtypeStruct((N_CHIPS, 2, D_MODEL // 2, SLOT1), BF16))

    h0 = _rmsnorm_fwd(x, norm_g[0], name="norm0")
    proj0 = _matmul(h0, w_in0, name="proj0")
    qn = _dn_pre_fwd(proj0, conv_w, 0, DN_QK_WIDTH, l2=True, scale=q_scale, name="dn_pre_q")
    kn = _dn_pre_fwd(proj0, conv_w, DN_QK_WIDTH, DN_QK_WIDTH, l2=True, scale=1.0, name="dn_pre_k")
    vc = _dn_pre_fwd(proj0, conv_w, 2 * DN_QK_WIDTH, MIX_WIDTH, l2=False, scale=1.0, name="dn_pre_v")
    g_nat, b_nat = _dn_ab_fwd(proj0, al, dtb, name="dn_ab")
    g_rows = _rows_form(g_nat, nc)
    b_rows = _rows_form(b_nat, nc)
    mix0, states, tinv = _dn_core_fwd(qn, kn, vc, g_rows, b_rows, out_g, name="dn_core")
    w_kv, w_out, w_in1 = late_weights(mix0)
    mem_n = _rmsnorm_fwd(mem, mem_norm_g, name="mem_norm")
    kv = [_matmul(mem_n, w_kv, n=2 * XA_WIDTH, tn=MM_TN, b_spec=kv_rhs(l), name=f"kv{l}") for l in range(2)]
    xa0 = _xa_fwd(proj0, P0_XQ, kv[0], xa_q_g[0], xa_k_g[0], name="xa0")
    y0 = _gate_fwd(mix0, xa0, proj0, P0_Z, name="gate0")
    x1 = _matmul(y0, w_out, n=D_MODEL, tn=MM_TN, b_spec=out_rhs(0), res=x, name="out0")

    h1 = _rmsnorm_fwd(x1, norm_g[1], name="norm1")
    proj1 = _matmul(h1, w_in1, n=SB_PROJ, tn=MM_TN, b_spec=in1_rhs, name="proj1")
    mix1, tot1 = _sb_fwd(proj1, sb_q_g, sb_k_g, name="sb")
    xa1 = _xa_fwd(proj1, P1_XQ, kv[1], xa_q_g[1], xa_k_g[1], name="xa1")
    y1 = _gate_fwd(mix1, xa1, proj1, P1_Z, name="gate1")
    x2 = _matmul(y1, w_out, n=D_MODEL, tn=MM_TN, b_spec=out_rhs(1), res=x1, name="out1")

    dx2, loss_vec = _loss_head(x2, target, name="loss")

    d_wout1 = _matmul(y1, dx2, ta=True, name="d_wout1", **slot_rows(INNER // N_CHIPS))
    dy1 = _matmul(dx2, w_out, tb=True, n=INNER, tn=MM_TN, b_spec=out_rhs_t(1), name="dy1")
    dcat1, dz1 = _gate_bwd(dy1, mix1, xa1, proj1, P1_Z, name="gate1_bwd")
    dqkv1, d_sbq, d_sbk = _sb_bwd(dcat1, tot1, proj1, sb_q_g, sb_k_g, name="sb_bwd")
    dxq1, dkv1, d_xaq1, d_xak1 = _xa_bwd(dcat1, proj1, P1_XQ, kv[1], xa_q_g[1], xa_k_g[1], name="xa1_bwd")
    dproj1 = dqkv1 + [dxq1, dz1]
    d_win1 = _matmul(h1, dproj1, ta=True, name="d_win1", **in1_out)
    d_wkv1 = _matmul(mem_n, dkv1, ta=True, name="d_wkv1", **slot_rows(D_MODEL // N_CHIPS))
    token = early_grads(1, d_win1, d_wout1, d_wkv1)
    dproj1 = dqkv1 + [dxq1 + token[0, 0].astype(BF16), dz1]
    dh1 = _matmul(dproj1, w_in1, tb=True, n=D_MODEL, tn=D_MODEL // 2, tk=MM_TN, b_spec=in1_rhs_t, name="dh1")
    token = grads_swapped(dh1)
    dx1, d_ng1 = _rmsnorm_bwd(dh1, x1, norm_g[1] + token[0, 0], dx2, name="norm1_bwd")

    d_wout0 = _matmul(y0, dx1, ta=True, name="d_wout0", **slot_rows(INNER // N_CHIPS))
    dy0 = _matmul(dx1, w_out, tb=True, n=INNER, tn=MM_TN, b_spec=out_rhs_t(0), name="dy0")
    dcat0, dz0 = _gate_bwd(dy0, mix0, xa0, proj0, P0_Z, name="gate0_bwd")
    dqv, dkv_h, dvc, dg_cols, db_cols, d_outg = _dn_core_bwd(
        dcat0, qn, kn, vc, g_rows, b_rows, out_g, states, tinv, name="dn_core_bwd")
    dpq, dwq = _dn_pre_bwd(dqv, proj0, conv_w, 0, DN_QK_WIDTH, l2=True, scale=q_scale, name="dn_pre_q_bwd")
    dpk, dwk = _dn_pre_bwd(dkv_h, proj0, conv_w, DN_QK_WIDTH, DN_QK_WIDTH, l2=True, scale=1.0, name="dn_pre_k_bwd")
    dpv, dwv = _dn_pre_bwd(dvc, proj0, conv_w, 2 * DN_QK_WIDTH, MIX_WIDTH, l2=False, scale=1.0, name="dn_pre_v_bwd")
    dab, d_alog, d_dt = _dn_ab_bwd(_cols_to_nat(dg_cols, nc), _cols_to_nat(db_cols, nc), proj0, al, dtb, name="dn_ab_bwd")
    dxq0, dkv0, d_xaq0, d_xak0 = _xa_bwd(dcat0, proj0, P0_XQ, kv[0], xa_q_g[0], xa_k_g[0], name="xa0_bwd")
    d_win0 = _matmul(h0, [dpq, dpk, dpv, dxq0, dz0, dab], ta=True, out_dtype=BF16, name="d_win0")
    d_wkv0 = _matmul(mem_n, dkv0, ta=True, name="d_wkv0", **slot_rows(D_MODEL // N_CHIPS))
    token = early_grads(0, d_win0, d_wout0, d_wkv0)
    zero = token[0, 0]
    dh0 = _matmul([dpq, dpk, dpv, dxq0, dz0, dab + zero.astype(BF16)], w_in0, tb=True, tk=MM_TN, name="dh0")
    dx0, d_ng0 = _rmsnorm_bwd(dh0, x, norm_g[0] + zero, dx1, name="norm0_bwd")

    dmem0 = _matmul(dkv0, w_kv, tb=True, n=D_MODEL, tn=D_MODEL // N_CHIPS, b_spec=kv_rhs_t(0), name="dmem0")
    dmem_n = _matmul(dkv1, w_kv, tb=True, n=D_MODEL, tn=D_MODEL // N_CHIPS, b_spec=kv_rhs_t(1), res=dmem0, name="dmem1")
    _, d_memg = _rmsnorm_bwd(dmem_n, mem, mem_norm_g, None, name="mem_norm_bwd")

    grads = dict(
        norm_g=jnp.concatenate([d_ng0, d_ng1], axis=0), mem_norm_g=d_memg.reshape(-1),
        xa_q_norm_g=jnp.concatenate([d_xaq0, d_xaq1], axis=0), xa_k_norm_g=jnp.concatenate([d_xak0, d_xak1], axis=0),
        dn_conv_w=jnp.concatenate([dwq, dwk, dwv], axis=1),
        dn_a_log=d_alog[:, :DN_V_HEADS], dn_dt_bias=d_dt[:, :DN_V_HEADS], dn_out_norm_g=d_outg,
        sb_q_norm_g=d_sbq, sb_k_norm_g=d_sbk)
    return loss_vec, dx0, grads


ANY = pl.BlockSpec(memory_space=pl.ANY)


def _place():
    x, y, c = lax.axis_index("x"), lax.axis_index("y"), lax.axis_index("c")
    chips = [(1 - x, y), (x, 1 - y), (1 - x, 1 - y)]
    return x, y, c, 2 * x + y, (x, y, 1 - c), chips


def _rcopy(src, dst, send, recv, i, dev):
    return pltpu.make_async_remote_copy(src_ref=src, dst_ref=dst, send_sem=send.at[i], recv_sem=recv.at[i],
                                        device_id=dev, device_id_type=MESH)


def _swap_halves(xs, *, name):
    nt = len(xs)

    def body(*refs):
        src, dst = refs[:nt], refs[nt:2 * nt]
        send, recv = refs[2 * nt:]
        x, y, c, j, sib, chips = _place()
        cps = []
        for t in range(nt):
            for s in range(N_CHIPS):
                cps.append(_rcopy(src[t].at[s, 1 - c], dst[t].at[s], send, recv, 4 * t + s, sib))
                cps[-1].start()
        for cp in cps:
            cp.wait_recv()
        for cp in cps:
            cp.wait_send()

    return pl.pallas_call(
        body, in_specs=[ANY] * nt, out_specs=[ANY] * nt,
        out_shape=[jax.ShapeDtypeStruct((N_CHIPS,) + a.shape[2:], a.dtype) for a in xs],
        scratch_shapes=[pltpu.SemaphoreType.DMA((4 * nt,)), pltpu.SemaphoreType.DMA((4 * nt,))], name=name)(*xs)


def _swap_with_sibling(fs, *, name):
    nt = len(fs)

    def body(*refs):
        src, dst = refs[:nt], refs[nt:2 * nt]
        send, recv = refs[2 * nt:]
        x, y, c, j, sib, chips = _place()
        cps = [_rcopy(src[t], dst[t], send, recv, t, sib) for t in range(nt)]
        for cp in cps:
            cp.start()
        for cp in cps:
            cp.wait_recv()
        for cp in cps:
            cp.wait_send()

    return pl.pallas_call(
        body, in_specs=[ANY] * nt, out_specs=[ANY] * nt,
        out_shape=[jax.ShapeDtypeStruct(a.shape, a.dtype) for a in fs],
        scratch_shapes=[pltpu.SemaphoreType.DMA((nt,)), pltpu.SemaphoreType.DMA((nt,))], name=name)(*fs)


HBM_SPEC = pl.BlockSpec(memory_space=pltpu.HBM)
SEM_SPEC = pl.BlockSpec(memory_space=pltpu.SEMAPHORE)
SIDE_EFFECT = pltpu.SideEffectType.DATAFLOW_SIDE_EFFECTING


def _gather_plan(src, land):
    x, y, c, j, sib, chips = _place()
    return [(src[t].at[c], land[t].at[j, c], (cx, cy, c), land[t].at[2 * cx + cy, c])
            for t in range(len(src)) for cx, cy in chips]


def _scatter_plan(src, land):
    x, y, c, j, sib, chips = _place()
    return [(src[t].at[2 * cx + cy], land[t].at[k], (cx, cy, c), land[t].at[k])
            for t in range(len(src)) for k, (cx, cy) in enumerate(chips)]


def _swap_plan(src, land):
    x, y, c, j, sib, chips = _place()
    return [(src[t].at[s, 1 - c], land[t].at[s], sib, land[t].at[s]) for t in range(len(src)) for s in range(N_CHIPS)]


def _exchange_start(srcs, lands, plan, *, name, per_tensor=3):
    ns, nb = len(srcs), len(srcs) + len(lands)
    n = per_tensor * ns

    def body(*refs):
        send, recv, token = refs[nb], refs[nb + 1], refs[-1]
        for i, (s, d, dev, _) in enumerate(plan(refs[:ns], refs[ns:nb])):
            _rcopy(s, d, send, recv, i, dev).start()
        token[...] = jnp.zeros_like(token)

    bufs = list(srcs) + list(lands)
    outs = pl.pallas_call(
        body, name=name,
        out_shape=(pltpu.SemaphoreType.DMA((n,)), pltpu.SemaphoreType.DMA((n,)), *[pltpu.HBM(a.shape, a.dtype) for a in bufs],
                   jax.ShapeDtypeStruct((8, LANE), F32)),
        in_specs=[HBM_SPEC] * nb, out_specs=(SEM_SPEC, SEM_SPEC, *[HBM_SPEC] * nb, pl.BlockSpec(memory_space=pltpu.VMEM)),
        input_output_aliases={i: 2 + i for i in range(nb)},
        compiler_params=pltpu.CompilerParams(has_side_effects=SIDE_EFFECT))(
            *[pltpu.with_memory_space_constraint(a, pltpu.HBM) for a in bufs])
    return outs[0], outs[1], list(outs[2:2 + ns]), list(outs[2 + ns:2 + nb]), outs[-1]


def _exchange_wait(srcs, lands, send, recv, after, plan, *, name):
    ns, nb = len(srcs), len(srcs) + len(lands)
    afters = list(after) if isinstance(after, (list, tuple)) else [after]

    def body(*refs):
        send_s, recv_s = refs[nb], refs[nb + 1]
        for i, (s, d, dev, inc) in enumerate(plan(refs[:ns], refs[ns:nb])):
            _rcopy(s, d, send_s, recv_s, i, dev).wait_send()
            _rcopy(inc, inc, send_s, recv_s, i, dev).wait_recv()

    bufs = list(srcs) + list(lands)
    outs = pl.pallas_call(
        body, name=name, out_shape=tuple(pltpu.HBM(a.shape, a.dtype) for a in bufs),
        in_specs=[HBM_SPEC] * nb + [SEM_SPEC, SEM_SPEC] + [ANY] * len(afters), out_specs=tuple([HBM_SPEC] * nb),
        input_output_aliases={i: i for i in range(nb)},
        compiler_params=pltpu.CompilerParams(has_side_effects=SIDE_EFFECT))(*bufs, send, recv, *afters)
    return list(outs[:ns]), list(outs[ns:])


def _forward_halves(lands, *, name):
    nt = len(lands)

    def body(*refs):
        src, dst = refs[:nt], refs[nt:2 * nt]
        send, recv = refs[2 * nt:]
        x, y, c, j, sib, chips = _place()
        cps = []
        for t in range(nt):
            for k, (cx, cy) in enumerate(chips):
                cps.append(_rcopy(src[t].at[2 * cx + cy, c], dst[t].at[2 * cx + cy, c], send, recv, 3 * t + k, sib))
                cps[-1].start()
        for t in range(nt):
            for k, (cx, cy) in enumerate(chips):
                other = dst[t].at[2 * cx + cy, 1 - c]
                _rcopy(other, other, send, recv, 3 * t + k, sib).wait_recv()
        for cp in cps:
            cp.wait_send()

    return pl.pallas_call(
        body, in_specs=[ANY] * nt, out_specs=[ANY] * nt, out_shape=[jax.ShapeDtypeStruct(a.shape, a.dtype) for a in lands],
        input_output_aliases={t: t for t in range(nt)},
        scratch_shapes=[pltpu.SemaphoreType.DMA((3 * nt,)), pltpu.SemaphoreType.DMA((3 * nt,))], name=name)(*lands)


def _all_reduce_small(parts, *, name):
    n = len(parts)
    offs, rows = [], 0
    for p in parts:
        offs.append(rows)
        rows += -(-p.shape[0] // 8) * 8

    def body(*refs):
        p_refs, o_refs = refs[:n], refs[n:2 * n]
        buf, send, recv = refs[2 * n:]
        x, y, c = lax.axis_index("x"), lax.axis_index("y"), lax.axis_index("c")
        me = 4 * x + 2 * y + c
        buf[me] = jnp.zeros((rows, LANE), F32)
        for p_ref, off in zip(p_refs, offs):
            buf[me, off:off + p_ref.shape[0], :] = p_ref[...]
        cps = []
        for r in range(1, 8):
            dev = (x ^ (r >> 2), y ^ ((r >> 1) & 1), c ^ (r & 1))
            cps.append(_rcopy(buf.at[me], buf.at[me], send, recv, r - 1, dev))
            cps[-1].start()
        for r in range(1, 8):
            frm = buf.at[me ^ r]
            _rcopy(frm, frm, send, recv, r - 1, (x, y, c)).wait_recv()
        for cp in cps:
            cp.wait_send()
        acc = buf[0]
        for d in range(1, 8):
            acc = acc + buf[d]
        for o_ref, off in zip(o_refs, offs):
            o_ref[...] = acc[off:off + o_ref.shape[0], :]

    vm = pl.BlockSpec(memory_space=pltpu.VMEM)
    return pl.pallas_call(
        body, in_specs=[vm] * n, out_specs=[vm] * n, out_shape=[jax.ShapeDtypeStruct(p.shape, F32) for p in parts],
        scratch_shapes=[pltpu.VMEM((8, rows, LANE), F32), pltpu.SemaphoreType.DMA((7,)), pltpu.SemaphoreType.DMA((7,))],
        name=name)(*parts)


def _add_halves(x, b, c_idx, *, name, tr=256):
    _, _, R, C = x.shape
    tr = min(tr, R)

    def body(c_ref, x_ref, b_ref, o_ref):
        o_ref[...] = (x_ref[...].astype(F32) + b_ref[...].astype(F32)).astype(o_ref.dtype)

    return pl.pallas_call(
        body,
        grid_spec=pltpu.PrefetchScalarGridSpec(
            num_scalar_prefetch=1, grid=(N_CHIPS, R // tr),
            in_specs=[pl.BlockSpec((None, None, tr, C), lambda s, i, c_ref: (s, c_ref[0], i, 0)),
                      pl.BlockSpec((None, tr, C), lambda s, i, c_ref: (s, i, 0))],
            out_specs=pl.BlockSpec((None, tr, C), lambda s, i, c_ref: (s, i, 0))),
        out_shape=jax.ShapeDtypeStruct(b.shape, b.dtype), compiler_params=_cp(("parallel", "parallel")), name=name)(c_idx, x, b)


def _sum_slot(p, rcv, j_idx, *, name, tr=256):
    _, R, C = p.shape
    tr = min(tr, R)

    def body(j_ref, p_ref, r_ref, o_ref):
        acc = p_ref[...].astype(F32)
        for k in range(3):
            acc = acc + r_ref[k].astype(F32)
        o_ref[...] = acc

    return pl.pallas_call(
        body,
        grid_spec=pltpu.PrefetchScalarGridSpec(
            num_scalar_prefetch=1, grid=(R // tr,),
            in_specs=[pl.BlockSpec((None, tr, C), lambda i, j_ref: (j_ref[0], i, 0)),
                      pl.BlockSpec((3, tr, C), lambda i, j_ref: (0, i, 0))],
            out_specs=pl.BlockSpec((tr, C), lambda i, j_ref: (i, 0))),
        out_shape=jax.ShapeDtypeStruct((R, C), F32), compiler_params=_cp(("parallel",)), name=name)(j_idx, p, rcv)


def _adamw_math(w, g, m, v):
    nm = ADAM_B1 * m + (1.0 - ADAM_B1) * g
    nv = ADAM_B2 * v + (1.0 - ADAM_B2) * (g * g)
    m_hat = nm / (1.0 - ADAM_B1 ** ADAM_STEP)
    v_hat = nv / (1.0 - ADAM_B2 ** ADAM_STEP)
    return -ADAM_LR * (m_hat / (jnp.sqrt(v_hat) + ADAM_EPS) + ADAM_WD * w), nm, nv


def _adamw_halves(w, g_mine, g_theirs, m, v, c_idx, *, name, layer=0, into=None, tr=128):
    _, _, R, C = w.shape
    tr = tr if R % tr == 0 else R

    def body(c_ref, w_ref, gm_ref, gt_ref, m_ref, v_ref, *rest):
        g_ref, d_ref, nm_ref, nv_ref = rest[-4:]
        gv = jnp.where(pl.program_id(0) == c_ref[0], gm_ref[...], gt_ref[...])
        d, nm, nv = _adamw_math(w_ref[...], gv, m_ref[...], v_ref[...])
        g_ref[...] = gv
        d_ref[...] = d
        nm_ref[...] = nm
        nv_ref[...] = nv

    full = pl.BlockSpec((None, None, tr, C), lambda hh, i, c_ref: (layer, hh, i, 0))
    half = pl.BlockSpec((tr, C), lambda hh, i, c_ref: (i, 0))
    sh = jax.ShapeDtypeStruct(w.shape, F32)
    extra = [] if into is None else list(into)
    return pl.pallas_call(
        body,
        grid_spec=pltpu.PrefetchScalarGridSpec(num_scalar_prefetch=1, grid=(2, R // tr),
                                               in_specs=[full, half, half, full, full] + [ANY] * len(extra),
                                               out_specs=(full,) * 4),
        out_shape=(sh,) * 4, input_output_aliases={6 + t: t for t in range(len(extra))},
        compiler_params=_cp(("parallel", "parallel")), name=name)(c_idx, w, g_mine, g_theirs, m, v, *extra)


def _adamw_parts(ws, gs, ms, vs, *, name):
    n = len(ws)

    def body(*refs):
        ins, outs = refs[:4 * n], refs[4 * n:]
        for t in range(n):
            d, nm, nv = _adamw_math(ins[t][...], ins[n + t][...], ins[2 * n + t][...], ins[3 * n + t][...])
            outs[t][...] = d
            outs[n + t][...] = nm
            outs[2 * n + t][...] = nv

    vm = pl.BlockSpec(memory_space=pltpu.VMEM)
    shapes = [jax.ShapeDtypeStruct(w.shape, F32) for w in ws] * 3
    outs = pl.pallas_call(body, in_specs=[vm] * (4 * n), out_specs=[vm] * (3 * n), out_shape=shapes, name=name)(
        *ws, *gs, *ms, *vs)
    return outs[:n], outs[n:2 * n], outs[2 * n:]


_SMALL = ["norm_g", "mem_norm_g", "xa_q_norm_g", "xa_k_norm_g", "dn_a_log", "dn_dt_bias", "dn_out_norm_g",
          "sb_q_norm_g", "sb_k_norm_g"]


def _rows128(a):
    flat = a.reshape(-1)
    pad = -flat.shape[0] % LANE
    if pad:
        flat = jnp.pad(flat, (0, pad))
    return flat.reshape(-1, LANE)


def _unrows(r, shape):
    return r.reshape(-1)[:math.prod(shape)].reshape(shape)


def kernel(x, mem, norm_g, mem_norm_g, mem_w_kv, xa_q_norm_g, xa_k_norm_g, w_out, dn_w_in, dn_conv_w, dn_a_log, dn_dt_bias, dn_out_norm_g, sb_w_in, sb_q_norm_g, sb_k_norm_g, loss_target, m_norm_g, m_mem_norm_g, m_mem_w_kv, m_xa_q_norm_g, m_xa_k_norm_g, m_w_out, m_dn_w_in, m_dn_conv_w, m_dn_a_log, m_dn_dt_bias, m_dn_out_norm_g, m_sb_w_in, m_sb_q_norm_g, m_sb_k_norm_g, v_norm_g, v_mem_norm_g, v_mem_w_kv, v_xa_q_norm_g, v_xa_k_norm_g, v_w_out, v_dn_w_in, v_dn_conv_w, v_dn_a_log, v_dn_dt_bias, v_dn_out_norm_g, v_sb_w_in, v_sb_q_norm_g, v_sb_k_norm_g):
    W = dict(norm_g=norm_g, mem_norm_g=mem_norm_g, mem_w_kv=mem_w_kv, xa_q_norm_g=xa_q_norm_g, xa_k_norm_g=xa_k_norm_g,
             w_out=w_out, dn_w_in=dn_w_in, dn_conv_w=dn_conv_w, dn_a_log=dn_a_log, dn_dt_bias=dn_dt_bias,
             dn_out_norm_g=dn_out_norm_g, sb_w_in=sb_w_in, sb_q_norm_g=sb_q_norm_g, sb_k_norm_g=sb_k_norm_g)
    M = dict(norm_g=m_norm_g, mem_norm_g=m_mem_norm_g, mem_w_kv=m_mem_w_kv, xa_q_norm_g=m_xa_q_norm_g,
             xa_k_norm_g=m_xa_k_norm_g, w_out=m_w_out, dn_w_in=m_dn_w_in, dn_conv_w=m_dn_conv_w, dn_a_log=m_dn_a_log,
             dn_dt_bias=m_dn_dt_bias, dn_out_norm_g=m_dn_out_norm_g, sb_w_in=m_sb_w_in, sb_q_norm_g=m_sb_q_norm_g,
             sb_k_norm_g=m_sb_k_norm_g)
    V = dict(norm_g=v_norm_g, mem_norm_g=v_mem_norm_g, mem_w_kv=v_mem_w_kv, xa_q_norm_g=v_xa_q_norm_g,
             xa_k_norm_g=v_xa_k_norm_g, w_out=v_w_out, dn_w_in=v_dn_w_in, dn_conv_w=v_dn_conv_w, dn_a_log=v_dn_a_log,
             dn_dt_bias=v_dn_dt_bias, dn_out_norm_g=v_dn_out_norm_g, sb_w_in=v_sb_w_in, sb_q_norm_g=v_sb_q_norm_g,
             sb_k_norm_g=v_sb_k_norm_g)
    names = ["norm_g", "mem_norm_g", "mem_w_kv", "xa_q_norm_g", "xa_k_norm_g", "w_out", "dn_w_in", "dn_conv_w",
             "dn_a_log", "dn_dt_bias", "dn_out_norm_g", "sb_w_in", "sb_q_norm_g", "sb_k_norm_g"]
    cx, cy, cc = lax.axis_index("x"), lax.axis_index("y"), lax.axis_index("c")
    slot = 2 * cx + cy
    half_r = D_MODEL // 2
    conv_cols = dn_conv_w.shape[2]

    by_slot = lambda fn, a: lax.switch(slot, [lambda v, s=s: fn(s, v) for s in range(N_CHIPS)], a)
    w0s = by_slot(_pack_own_shard, dn_w_in[0].astype(BF16)).reshape(2, half_r, WIN_COLS)
    convs = jnp.pad(dn_conv_w[0], ((0, 8 - DN_CONV), (0, 0))).reshape(8, 2, conv_cols // 2).transpose(1, 0, 2)
    c_idx = jnp.reshape(cc, (1,)).astype(jnp.int32)
    j_idx = jnp.reshape(slot, (1,)).astype(jnp.int32)
    own_a = [w0s, convs]
    lands_a = [lax.dynamic_update_slice(lax.empty((N_CHIPS,) + o.shape, o.dtype), o[None], (slot, 0, 0, 0)) for o in own_a]
    send_a, recv_a, own_a, lands_a, token_a = _exchange_start(own_a, lands_a, _gather_plan, name="gather_start")
    zero_a = token_a[0, 0]
    M["dn_w_in"] = m_dn_w_in + zero_a
    V["dn_w_in"] = v_dn_w_in + zero_a
    own_b = [(sb_w_in[0] + zero_a).astype(BF16).reshape(2, half_r, SB_PROJ // N_CHIPS), (w_out + zero_a).astype(BF16),
             (mem_w_kv + zero_a).astype(BF16)]
    view0 = (1, 2, half_r, P0_SHARD)
    _, lands_a = _exchange_wait(own_a, lands_a, send_a, recv_a,
                                [M["dn_w_in"].reshape(view0), V["dn_w_in"].reshape(view0)] + own_b,
                                _gather_plan, name="gather_wait")
    (g0, gconv), own_b = lax.optimization_barrier((_forward_halves(lands_a, name="gather_forward"), own_b))
    def land(o):
        return lax.dynamic_update_slice(lax.empty((N_CHIPS,) + o.shape, o.dtype), o[None], (slot, 0, 0, 0))

    late = {}
    late[1] = _exchange_start(own_b[1:], [land(o) for o in own_b[1:]], _gather_plan, name="gather_late1_start")
    w1s, _ = lax.optimization_barrier((own_b[0], late[1][4]))
    late[2] = _exchange_start([w1s], [land(w1s)], _gather_plan, name="gather_late2_start")
    token_b = late[2][4]

    def late_weights(after):
        got = []
        for i in (1, 2):
            send, recv, srcs, lands, _ = late[i]
            _, lands = _exchange_wait(srcs, lands, send, recv, after, _gather_plan, name=f"gather_late{i}_wait")
            got += _forward_halves(lands, name=f"gather_late{i}_forward")
            after = got[-1]
        gout, gkv, g1 = got
        return gkv, gout, g1

    rs = {}

    def scatter_start(tag, xs, from_sib=None):
        if from_sib is None:
            from_sib = _swap_halves(xs, name=f"rs{tag}_swap")
        ps = [_add_halves(a, b, c_idx, name=f"rs{tag}_add{t}") for t, (a, b) in enumerate(zip(xs, from_sib))]
        rcv = [lax.empty((3,) + p.shape[1:], p.dtype) for p in ps]
        send, recv, ps, rcv, token = _exchange_start(ps, rcv, _scatter_plan, name=f"rs{tag}_scatter_start")
        rs[tag] = (ps, rcv, send, recv)
        return token

    def scatter_finish(tag, after):
        ps, rcv, send, recv = rs[tag]
        ps, rcv = _exchange_wait(ps, rcv, send, recv, after, _scatter_plan, name=f"rs{tag}_scatter_wait")
        return [_sum_slot(p, r, j_idx, name=f"rs{tag}_sum{t}") for t, (p, r) in enumerate(zip(ps, rcv))]

    def early_grads(layer, d_win, d_wout, d_wkv):
        if layer == 0:
            d_win = jnp.stack([_padded_to_window(s, d_win) for s in range(N_CHIPS)]).reshape(N_CHIPS, 2, half_r, WIN_COLS)
        xs = [d_win, d_wout.reshape(N_CHIPS, 2, -1, D_MODEL), d_wkv.reshape(N_CHIPS, 2, -1, 2 * XA_WIDTH)]
        if layer == 0:
            return scatter_start(0, xs)
        lands = [lax.empty((N_CHIPS,) + a.shape[2:], a.dtype) for a in xs]
        send, recv, xs, lands, token = _exchange_start(xs, lands, _swap_plan, per_tensor=N_CHIPS, name="rs1_swap_start")
        rs["swap1"] = (xs, lands, send, recv)
        return token

    def grads_swapped(after):
        xs, lands, send, recv = rs["swap1"]
        xs, from_sib = _exchange_wait(xs, lands, send, recv, after, _swap_plan, name="rs1_swap_wait")
        return scatter_start(1, xs, from_sib)

    w_in0 = _windows_to_padded(g0.reshape(N_CHIPS, D_MODEL, WIN_COLS))
    conv_f = gconv.transpose(2, 0, 1, 3).reshape(8, N_CHIPS * conv_cols)[:DN_CONV]

    loss_vec, grad_x, g = _local_step(
        x[0], mem[0], loss_target[0], norm_g + token_b[0, 0], mem_norm_g, xa_q_norm_g, xa_k_norm_g, w_in0, conv_f,
        dn_a_log[0], dn_dt_bias[0], dn_out_norm_g[0], sb_q_norm_g[0], sb_k_norm_g[0], late_weights, early_grads,
        grads_swapped)

    mine1 = scatter_finish(1, grad_x)
    theirs1 = _swap_with_sibling(mine1, name="rs1_join")
    big1 = [("sb_w_in", None), ("w_out", 1), ("mem_w_kv", 1)]
    big0 = [("dn_w_in", None), ("w_out", 0), ("mem_w_kv", 0)]

    out_g, out_d, out_m, out_v = {}, {}, {}, {}
    partial = {}

    def adamw_big(big, mine, theirs):
        for (n, layer), mine_g, their_g in zip(big, mine, theirs):
            layers = 1 if layer is None else 2
            view = (layers, 2) + mine_g.shape
            partial[n] = _adamw_halves(W[n].reshape(view), mine_g, their_g, M[n].reshape(view), V[n].reshape(view), c_idx,
                                       layer=layer or 0, into=partial.get(n), name=f"adamw_{n}" + ("" if layer is None else str(layer)))
        return [partial[n][0] for n, _ in big]

    done1 = lax.optimization_barrier(tuple(adamw_big(big1, mine1, theirs1)))[-1]
    mine0 = scatter_finish(0, done1)
    mine0[0] = by_slot(_unpack_own_shard, mine0[0])

    parts, _ = lax.optimization_barrier(([_rows128(g[n]) for n in _SMALL] + [_rows128(g["dn_conv_w"]), loss_vec], mine0[0]))
    red = _all_reduce_small(parts, name="all_reduce_small")
    small_rows = dict(zip(_SMALL, red))
    conv_full = red[len(_SMALL)].reshape(DN_CONV, N_CHIPS * conv_cols)
    small_rows["dn_conv_w"] = _rows128(lax.dynamic_slice_in_dim(conv_full, slot * conv_cols, conv_cols, axis=1))
    loss = red[-1][0, 0]

    adamw_big(big0, mine0, _swap_with_sibling(mine0, name="rs0_join"))
    for n, outs in partial.items():
        out_g[n], out_d[n], out_m[n], out_v[n] = [o.reshape(W[n].shape) for o in outs]
    small_names = _SMALL + ["dn_conv_w"]
    ds, nms, nvs = _adamw_parts([_rows128(W[n]) for n in small_names], [small_rows[n] for n in small_names],
                                [_rows128(M[n]) for n in small_names], [_rows128(V[n]) for n in small_names], name="adamw_small")
    for n, d, nm, nv in zip(small_names, ds, nms, nvs):
        shp = W[n].shape
        out_g[n], out_d[n], out_m[n], out_v[n] = [_unrows(r, shp) for r in (small_rows[n], d, nm, nv)]

    return (loss, grad_x[None], *[out_g[n] for n in names], *[out_d[n] for n in names], *[out_m[n] for n in names],
            *[out_v[n] for n in names])
```

```python
import math

import jax
import jax.numpy as jnp
from jax import lax
from jax.experimental import pallas as pl
from jax.experimental.pallas import tpu as pltpu

F32 = jnp.float32
BF16 = jnp.bfloat16
HI = lax.Precision.HIGHEST
MESH = pl.DeviceIdType.MESH

D_MODEL = 2048
INNER = 4096
XA_WIDTH = 1024
XA_HEADS = 4
XA_DIM = 256
MIX_WIDTH = 3072
HEAD_DIM = 128
DN_V_HEADS = 24
DN_QK_WIDTH = 1536
DN_CONV = 4
DN_PROJ = 11312
SB_PROJ = 14336
EPS = 1e-6
N_CHIPS = 4

CH = 128
LANE = 128

P0_XQ = 6144
P0_Z = 7168
P0_AB = 11264
P0 = 11776
P0_SHARD = DN_PROJ // N_CHIPS
P1_XQ = 9216
P1_Z = 10240
P1 = SB_PROJ

ADAM_LR = 0.001
ADAM_B1 = 0.9
ADAM_B2 = 0.999
ADAM_EPS = 1e-08
ADAM_WD = 0.01
ADAM_STEP = 10

VMEM_LIMIT = 48 * 1024 * 1024


def _cp(sem=None, **kw):
    return pltpu.CompilerParams(dimension_semantics=sem, vmem_limit_bytes=VMEM_LIMIT, **kw)


def _bdot(a, b, dims):
    return lax.dot_general(a.astype(BF16), b.astype(BF16), (dims, ((), ())), preferred_element_type=F32)


def _fdot(a, b, dims):
    return lax.dot_general(a, b, (dims, ((), ())), precision=HI, preferred_element_type=F32)


NN = ((1,), (0,))
NT = ((1,), (1,))
TN = ((0,), (0,))


def _sigmoid(x):
    return 1.0 / (1.0 + jnp.exp(-x))


def _softplus(x):
    return jnp.maximum(x, 0.0) + jnp.log(1.0 + jnp.exp(-jnp.abs(x)))


def _iota2(shape, axis):
    return lax.broadcasted_iota(jnp.int32, shape, axis)


MM_FULL_K = 4096
MM_BLOCK_BYTES = 4 * 1024 * 1024


def _matmul(a, b, *, ta=False, tb=False, out_dtype=F32, res=None, name, n=None, tm=None, tn=None, tk=None,
            b_spec=None, o_spec=None, o_shape=None):
    a_segs = list(a) if isinstance(a, (list, tuple)) else [a]
    b_segs = list(b) if isinstance(b, (list, tuple)) else [b]
    a0, b0 = a_segs[0], b_segs[0]
    M = a0.shape[1] if ta else a0.shape[0]
    K = a0.shape[0] if ta else sum(s.shape[1] for s in a_segs)
    if n is None:
        n = b0.shape[0] if tb else sum(s.shape[1] for s in b_segs)
    N = n
    dims = ((0,) if ta else (1,), (1,) if tb else (0,))
    has_res = res is not None
    flat = lambda v: v.reshape(-1, v.shape[-1])
    o_shape = o_shape or jax.ShapeDtypeStruct((M, N), out_dtype)

    def seg_specs(segs, tile, block, pos):
        specs, ranges, off = [], [], 0
        for s in segs:
            cnt = s.shape[1] // tile
            assert s.shape[1] % tile == 0, (name, s.shape, tile)

            def imap(*g, off=off, cnt=cnt):
                t = jnp.clip(g[pos] - off, 0, cnt - 1)
                return (g[0], t) if pos == 2 else (0, t)

            specs.append(pl.BlockSpec(block, imap))
            ranges.append((off, off + cnt))
            off += cnt
        return specs, ranges

    if K <= MM_FULL_K:
        assert len(a_segs) == 1
        tm = tm or min(M, 1024, max(256, MM_BLOCK_BYTES // (K * a0.dtype.itemsize)))
        tn = tn or min(N, 512)
        assert M % tm == 0 and N % tn == 0, (name, M, N, K, tm, tn)
        nb = len(b_segs)
        if b_spec is not None:
            b_specs, b_ranges = [b_spec], [(0, N // tn)]
        elif nb > 1:
            assert not tb
            b_specs, b_ranges = seg_specs(b_segs, tn, (K, tn), 1)
        else:
            b_specs = [pl.BlockSpec((tn, K), lambda i, j: (j, 0)) if tb else pl.BlockSpec((K, tn), lambda i, j: (0, j))]
            b_ranges = [(0, N // tn)]

        def body_full(*refs):
            a_ref, b_refs = refs[0], refs[1:1 + nb]
            r_ref = refs[1 + nb] if has_res else None
            o_ref = refs[-1]
            j = pl.program_id(1)
            for b_ref, (lo, hi) in zip(b_refs, b_ranges):
                def emit(b_ref=b_ref):
                    r = _bdot(a_ref[...], flat(b_ref[...]), dims)
                    if has_res:
                        r = r + r_ref[...]
                    o_ref[...] = r.astype(o_ref.dtype).reshape(o_ref.shape)
                if nb == 1:
                    emit()
                else:
                    pl.when(jnp.logical_and(j >= lo, j < hi))(emit)

        a_spec = pl.BlockSpec((K, tm), lambda i, j: (0, i)) if ta else pl.BlockSpec((tm, K), lambda i, j: (i, 0))
        o_spec = o_spec or pl.BlockSpec((tm, tn), lambda i, j: (i, j))
        r_spec = [pl.BlockSpec((tm, tn), lambda i, j: (i, j))] if has_res else []
        return pl.pallas_call(
            body_full, grid=(M // tm, N // tn), in_specs=[a_spec] + b_specs + r_spec, out_specs=o_spec, out_shape=o_shape,
            compiler_params=_cp(("parallel", "arbitrary")), name=name)(*([a0] + b_segs + ([res] if has_res else [])))

    assert tb and not ta and len(b_segs) == 1
    tm, tn = tm or min(M, 1024), tn or min(N, 1024)
    tk = tk or (1024 if all(s.shape[1] % 1024 == 0 for s in a_segs) else 512)
    assert M % tm == 0 and N % tn == 0 and K % tk == 0, (name, M, N, K, tm, tn, tk)
    nk = K // tk
    na = len(a_segs)
    if na > 1:
        a_specs, a_ranges = seg_specs(a_segs, tk, (tm, tk), 2)
    else:
        a_specs, a_ranges = [pl.BlockSpec((tm, tk), lambda i, j, k: (i, k))], [(0, nk)]
    b_spec = b_spec or pl.BlockSpec((tn, tk), lambda i, j, k: (j, k))

    def body(*refs):
        a_refs, b_ref = refs[:na], refs[na]
        r_ref = refs[na + 1] if has_res else None
        o_ref, acc = refs[-2], refs[-1]
        k = pl.program_id(2)

        @pl.when(k == 0)
        def _():
            acc[...] = jnp.zeros_like(acc)

        for a_ref, (lo, hi) in zip(a_refs, a_ranges):
            def emit(a_ref=a_ref):
                acc[...] += _bdot(a_ref[...], flat(b_ref[...]), dims)
            if na == 1:
                emit()
            else:
                pl.when(jnp.logical_and(k >= lo, k < hi))(emit)

        @pl.when(k == nk - 1)
        def _():
            r = acc[...]
            if has_res:
                r = r + r_ref[...]
            o_ref[...] = r.astype(o_ref.dtype).reshape(o_ref.shape)

    o_spec = o_spec or pl.BlockSpec((tm, tn), lambda i, j, k: (i, j))
    r_spec = [pl.BlockSpec((tm, tn), lambda i, j, k: (i, j))] if has_res else []
    return pl.pallas_call(
        body, grid=(M // tm, N // tn, nk), in_specs=a_specs + [b_spec] + r_spec, out_specs=o_spec, out_shape=o_shape,
        scratch_shapes=[pltpu.VMEM((tm, tn), F32)],
        compiler_params=_cp(("parallel", "parallel", "arbitrary")), name=name)(*(a_segs + [b0] + ([res] if has_res else [])))


def _rmsnorm_fwd(x, g, *, name, tm=256):
    S, Dm = x.shape
    tm = min(tm, S)

    def body(x_ref, g_ref, o_ref):
        xv = x_ref[...]
        r = lax.rsqrt(jnp.mean(xv * xv, axis=-1, keepdims=True) + EPS)
        o_ref[...] = (xv * r * g_ref[...]).astype(BF16)

    return pl.pallas_call(
        body, grid=(S // tm,), in_specs=[pl.BlockSpec((tm, Dm), lambda i: (i, 0)), pl.BlockSpec((1, Dm), lambda i: (0, 0))],
        out_specs=pl.BlockSpec((tm, Dm), lambda i: (i, 0)), out_shape=jax.ShapeDtypeStruct((S, Dm), BF16),
        compiler_params=_cp(("parallel",)), name=name)(x, g.reshape(1, Dm))


def _rmsnorm_bwd(dh, x, g, dres, *, name, tm=256):
    S, Dm = x.shape
    tm = min(tm, S)
    want_dx = dres is not None

    def body(*refs):
        if want_dx:
            dh_ref, x_ref, g_ref, dr_ref, dx_ref, dg_ref = refs
        else:
            dh_ref, x_ref, g_ref, dg_ref = refs
        i = pl.program_id(0)
        xv = x_ref[...]
        dhv = dh_ref[...]
        r = lax.rsqrt(jnp.mean(xv * xv, axis=-1, keepdims=True) + EPS)
        y = xv * r
        part = jnp.sum(dhv * y, axis=0, keepdims=True)

        @pl.when(i == 0)
        def _():
            dg_ref[...] = jnp.zeros_like(dg_ref)

        dg_ref[...] += part
        if want_dx:
            dy = dhv * g_ref[...]
            dx_ref[...] = dr_ref[...] + r * (dy - y * jnp.mean(dy * y, axis=-1, keepdims=True))

    row = pl.BlockSpec((tm, Dm), lambda i: (i, 0))
    vec = pl.BlockSpec((1, Dm), lambda i: (0, 0))
    if want_dx:
        dx, dg = pl.pallas_call(
            body, grid=(S // tm,), in_specs=[row, row, vec, row], out_specs=(row, vec),
            out_shape=(jax.ShapeDtypeStruct((S, Dm), F32), jax.ShapeDtypeStruct((1, Dm), F32)),
            compiler_params=_cp(("arbitrary",)), name=name)(dh, x, g.reshape(1, Dm), dres)
        return dx, dg
    dg = pl.pallas_call(
        body, grid=(S // tm,), in_specs=[row, row, vec], out_specs=vec,
        out_shape=jax.ShapeDtypeStruct((1, Dm), F32), compiler_params=_cp(("arbitrary",)), name=name)(dh, x, g.reshape(1, Dm))
    return None, dg


GATE_TN = XA_WIDTH
GATE_MIX_TILES = MIX_WIDTH // GATE_TN


def _gate_cat_specs(tm):
    return [pl.BlockSpec((tm, GATE_TN), lambda i, j: (i, jnp.minimum(j, GATE_MIX_TILES - 1))),
            pl.BlockSpec((tm, GATE_TN), lambda i, j: (i, 0))]


def _gate_fwd(mix, xa, proj, z_off, *, name, tm=256):
    S = mix.shape[0]
    tm = min(tm, S)
    zb = z_off // GATE_TN

    def body(m_ref, x_ref, z_ref, y_ref):
        z = z_ref[...]
        c = jnp.where(pl.program_id(1) < GATE_MIX_TILES, m_ref[...], x_ref[...])
        y_ref[...] = (c * z * _sigmoid(z)).astype(BF16)

    blk = pl.BlockSpec((tm, GATE_TN), lambda i, j: (i, j))
    return pl.pallas_call(
        body, grid=(S // tm, INNER // GATE_TN),
        in_specs=_gate_cat_specs(tm) + [pl.BlockSpec((tm, GATE_TN), lambda i, j: (i, zb + j))],
        out_specs=blk, out_shape=jax.ShapeDtypeStruct((S, INNER), BF16),
        compiler_params=_cp(("parallel", "arbitrary")), name=name)(mix, xa, proj)


def _gate_bwd(dy, mix, xa, proj, z_off, *, name, tm=256):
    S = mix.shape[0]
    tm = min(tm, S)
    zb = z_off // GATE_TN

    def body(dy_ref, m_ref, x_ref, z_ref, dc_ref, dz_ref):
        z = z_ref[...]
        sg = _sigmoid(z)
        d = dy_ref[...]
        c = jnp.where(pl.program_id(1) < GATE_MIX_TILES, m_ref[...], x_ref[...])
        dc_ref[...] = d * z * sg
        dz_ref[...] = (d * c * sg * (1.0 + z * (1.0 - sg))).astype(BF16)

    blk = pl.BlockSpec((tm, GATE_TN), lambda i, j: (i, j))
    return pl.pallas_call(
        body, grid=(S // tm, INNER // GATE_TN),
        in_specs=[blk] + _gate_cat_specs(tm) + [pl.BlockSpec((tm, GATE_TN), lambda i, j: (i, zb + j))], out_specs=(blk, blk),
        out_shape=(jax.ShapeDtypeStruct((S, INNER), F32), jax.ShapeDtypeStruct((S, INNER), BF16)),
        compiler_params=_cp(("parallel", "arbitrary")), name=name)(dy, mix, xa, proj)


def _loss_head(x, target, *, name, tm=256):
    S, Dm = x.shape
    tm = min(tm, S)

    nt = S // tm

    def body(x_ref, t_ref, dx_ref, l_ref, acc):
        i = pl.program_id(0)
        e = x_ref[...] - t_ref[...]
        dx_ref[...] = e * (1.0 / Dm)

        @pl.when(i == 0)
        def _():
            acc[...] = jnp.zeros_like(acc)

        acc[...] += jnp.sum(e * e, axis=0, keepdims=True) * (0.5 / Dm)

        @pl.when(i == nt - 1)
        def _():
            l_ref[...] = jnp.sum(acc[...], axis=1, keepdims=True) + jnp.zeros((1, LANE), F32)

    row = pl.BlockSpec((tm, Dm), lambda i: (i, 0))
    return pl.pallas_call(
        body, grid=(nt,), in_specs=[row, row], out_specs=(row, pl.BlockSpec((1, LANE), lambda i: (0, 0))),
        out_shape=(jax.ShapeDtypeStruct((S, Dm), F32), jax.ShapeDtypeStruct((1, LANE), F32)),
        scratch_shapes=[pltpu.VMEM((1, Dm), F32)],
        compiler_params=_cp(("arbitrary",)), name=name)(x, target)


def _xa_norm(v, g):
    r = lax.rsqrt(jnp.mean(v * v, axis=-1, keepdims=True) + EPS)
    return v * r, r


def _xa_fwd(proj, xq_off, kv, gq, gk, *, name, tm=512):
    S = proj.shape[0]
    tm = min(tm, S)
    qb = xq_off // XA_DIM
    n_mem = kv.shape[0]
    scale = XA_DIM ** -0.5

    def body(q_ref, k_ref, v_ref, gq_ref, gk_ref, o_ref):
        qh, _ = _xa_norm(q_ref[...], None)
        kh, _ = _xa_norm(k_ref[...], None)
        qn = qh * gq_ref[...]
        kn = kh * gk_ref[...]
        s = _bdot(qn, kn, NT) * scale
        s = s - jnp.max(s, axis=-1, keepdims=True)
        p = jnp.exp(s)
        p = p / jnp.sum(p, axis=-1, keepdims=True)
        o_ref[...] = _bdot(p, v_ref[...], NN)

    vec = pl.BlockSpec((1, XA_DIM), lambda h, i: (0, 0))
    return pl.pallas_call(
        body, grid=(XA_HEADS, S // tm),
        in_specs=[pl.BlockSpec((tm, XA_DIM), lambda h, i: (i, qb + h)),
                  pl.BlockSpec((n_mem, XA_DIM), lambda h, i: (0, h)),
                  pl.BlockSpec((n_mem, XA_DIM), lambda h, i: (0, XA_HEADS + h)), vec, vec],
        out_specs=pl.BlockSpec((tm, XA_DIM), lambda h, i: (i, h)),
        out_shape=jax.ShapeDtypeStruct((S, XA_WIDTH), F32),
        compiler_params=_cp(("parallel", "parallel")), name=name)(proj, kv, kv, gq.reshape(1, XA_DIM), gk.reshape(1, XA_DIM))


def _xa_bwd(dcat, proj, xq_off, kv, gq, gk, *, name, tm=512):
    S = proj.shape[0]
    tm = min(tm, S)
    nt = S // tm
    qb = xq_off // XA_DIM
    db = MIX_WIDTH // XA_DIM
    n_mem = kv.shape[0]
    scale = XA_DIM ** -0.5

    def body(d_ref, q_ref, k_ref, v_ref, gq_ref, gk_ref, dq_ref, dk_ref, dv_ref, dgq_ref, dgk_ref, dkn_acc):
        h = pl.program_id(0)
        i = pl.program_id(1)
        q = q_ref[...]
        k = k_ref[...]
        qh, rq = _xa_norm(q, None)
        kh, rk = _xa_norm(k, None)
        gqv = gq_ref[...]
        gkv = gk_ref[...]
        qn = qh * gqv
        kn = kh * gkv
        s = _bdot(qn, kn, NT) * scale
        s = s - jnp.max(s, axis=-1, keepdims=True)
        p = jnp.exp(s)
        p = p / jnp.sum(p, axis=-1, keepdims=True)
        d = d_ref[...]
        dp = _bdot(d, v_ref[...], NT)
        ds = p * (dp - jnp.sum(dp * p, axis=-1, keepdims=True)) * scale
        dqn = _bdot(ds, kn, NN)

        @pl.when(i == 0)
        def _():
            dkn_acc[...] = jnp.zeros_like(dkn_acc)
            dv_ref[...] = jnp.zeros_like(dv_ref)

        @pl.when(jnp.logical_and(i == 0, h == 0))
        def _():
            dgq_ref[...] = jnp.zeros_like(dgq_ref)
            dgk_ref[...] = jnp.zeros_like(dgk_ref)

        dkn_acc[...] += _bdot(ds, qn, TN)
        dv_ref[...] += _bdot(p, d, TN)
        dgq_ref[...] += jnp.sum(dqn * qh, axis=0, keepdims=True)
        dy = dqn * gqv
        dq_ref[...] = (rq * (dy - qh * jnp.mean(dy * qh, axis=-1, keepdims=True))).astype(BF16)

        @pl.when(i == nt - 1)
        def _():
            dkn = dkn_acc[...]
            dgk_ref[...] += jnp.sum(dkn * kh, axis=0, keepdims=True)
            dyk = dkn * gkv
            dk_ref[...] = rk * (dyk - kh * jnp.mean(dyk * kh, axis=-1, keepdims=True))

    vec = pl.BlockSpec((1, XA_DIM), lambda h, i: (0, 0))
    kblk = pl.BlockSpec((n_mem, XA_DIM), lambda h, i: (0, h))
    vblk = pl.BlockSpec((n_mem, XA_DIM), lambda h, i: (0, XA_HEADS + h))
    dq, dk, dv, dgq, dgk = pl.pallas_call(
        body, grid=(XA_HEADS, nt),
        in_specs=[pl.BlockSpec((tm, XA_DIM), lambda h, i: (i, db + h)),
                  pl.BlockSpec((tm, XA_DIM), lambda h, i: (i, qb + h)), kblk, vblk, vec, vec],
        out_specs=(pl.BlockSpec((tm, XA_DIM), lambda h, i: (i, h)), kblk, kblk, vec, vec),
        out_shape=(jax.ShapeDtypeStruct((S, XA_WIDTH), BF16), jax.ShapeDtypeStruct((n_mem, XA_WIDTH), F32),
                   jax.ShapeDtypeStruct((n_mem, XA_WIDTH), F32), jax.ShapeDtypeStruct((1, XA_DIM), F32),
                   jax.ShapeDtypeStruct((1, XA_DIM), F32)),
        scratch_shapes=[pltpu.VMEM((n_mem, XA_DIM), F32)],
        compiler_params=_cp(("arbitrary", "arbitrary")), name=name)(
            dcat, proj, kv, kv, gq.reshape(1, XA_DIM), gk.reshape(1, XA_DIM))
    return dq, jnp.concatenate([dk, dv], axis=1), dgq, dgk


SB_TQ = 256
SB_TK = 256
SB_HEADS = 24


SB_PAIR = 2
SB_PW = SB_PAIR * HEAD_DIM


def _hdot(a, b, dims, dot=None):
    dot = dot or _bdot
    n = a.shape[0] if a.ndim == 3 else b.shape[0]
    return jnp.stack([dot(a[i] if a.ndim == 3 else a, b[i] if b.ndim == 3 else b, dims) for i in range(n)])


def _sb_tile(qi, kj, t0, s0, masked):
    z = _hdot(qi, kj, NT)
    sp = _softplus(z)
    ls = z - sp
    if not masked:
        return -sp, ls, None
    mask = (s0 + _iota2(z.shape[1:], 1)) < (t0 + _iota2(z.shape[1:], 0))
    return jnp.where(mask, -sp, 0.0), ls, mask


def _dot2(x, tri):
    hi = x.astype(BF16)
    lo = (x - hi.astype(F32)).astype(BF16)
    plain = lambda u, v, dims: lax.dot_general(u, v, (dims, ((), ())), preferred_element_type=F32)
    return _hdot(hi, tri, NN, plain) + _hdot(lo, tri, NN, plain)


def _sb_heads(ref, rows=slice(None)):
    return jnp.stack([ref[rows, hh * HEAD_DIM:(hh + 1) * HEAD_DIM] for hh in range(SB_PAIR)])


def _sb_fwd(proj, gq, gk, *, name):
    S = proj.shape[0]
    tq, tk = min(SB_TQ, S), min(SB_TK, S)
    nq = S // tq
    scale = HEAD_DIM ** -0.5

    def body(q_ref, k_ref, v_ref, gq_ref, gk_ref, o_ref, tot_ref, qn_s, kn_s, v_s):
        q = _sb_heads(q_ref)
        k = _sb_heads(k_ref)
        qn_s[...] = (q * lax.rsqrt(jnp.mean(q * q, axis=-1, keepdims=True) + EPS) * (gq_ref[...] * scale)).astype(BF16)
        kn_s[...] = (k * lax.rsqrt(jnp.mean(k * k, axis=-1, keepdims=True) + EPS) * gk_ref[...]).astype(BF16)
        v_s[...] = _sb_heads(v_ref).astype(BF16)
        after = (_iota2((tk, tk), 0) > _iota2((tk, tk), 1)).astype(BF16)

        def qblock(i, _):
            rows = pl.ds(pl.multiple_of(i * tq, tq), tq)
            qi = qn_s[:, rows, :]
            jd = (i * tq) // tk

            def tile(j, acc, run, masked):
                cols = pl.ds(pl.multiple_of(j * tk, tk), tk)
                lr, ls, mask = _sb_tile(qi, kn_s[:, cols, :], i * tq, j * tk, masked)
                later = _dot2(lr, after) + run
                a = jnp.exp(ls + later)
                if masked:
                    a = jnp.where(mask, a, 0.0)
                acc = acc + _hdot(a, v_s[:, cols, :], NN)
                return acc, run + jnp.sum(lr, axis=-1, keepdims=True)

            acc, run = tile(jd, jnp.zeros((SB_PAIR, tq, HEAD_DIM), F32), jnp.zeros((SB_PAIR, tq, 1), F32), True)
            acc, run = lax.fori_loop(0, jd, lambda jj, c: tile(jd - 1 - jj, c[0], c[1], False), (acc, run))
            tot = run + jnp.zeros((SB_PAIR, tq, HEAD_DIM), F32)
            for hh in range(SB_PAIR):
                o_ref[rows, hh * HEAD_DIM:(hh + 1) * HEAD_DIM] = acc[hh]
                tot_ref[rows, hh * HEAD_DIM:(hh + 1) * HEAD_DIM] = tot[hh]
            return 0

        lax.fori_loop(0, nq, qblock, 0)

    npair = SB_HEADS // SB_PAIR
    vec = pl.BlockSpec((1, HEAD_DIM), lambda h: (0, 0))
    hb = lambda off: pl.BlockSpec((S, SB_PW), lambda h: (0, off + h), pipeline_mode=pl.Buffered(1))
    return pl.pallas_call(
        body, grid=(npair,), in_specs=[hb(0), hb(npair), hb(2 * npair), vec, vec],
        out_specs=(hb(0), hb(0)), out_shape=(jax.ShapeDtypeStruct((S, MIX_WIDTH), F32),) * 2,
        scratch_shapes=[pltpu.VMEM((SB_PAIR, S, HEAD_DIM), BF16)] * 3,
        compiler_params=_cp(("parallel",)), name=name)(proj, proj, proj, gq.reshape(1, HEAD_DIM), gk.reshape(1, HEAD_DIM))


def _sb_bwd(dmix, tot, proj, gq, gk, *, name):
    S = proj.shape[0]
    tq, tk = min(SB_TQ, S), min(SB_TK, S)
    nq = S // tq
    scale = HEAD_DIM ** -0.5

    def body(do_ref, o_ref, q_ref, k_ref, v_ref, gq_ref, gk_ref, dq_ref, dk_ref, dv_ref, dgq_ref, dgk_ref,
             qn_s, kn_s, v_s, dkn_s, dqn_s, dv_s):
        h = pl.program_id(0)
        q = _sb_heads(q_ref)
        k = _sb_heads(k_ref)
        rq = lax.rsqrt(jnp.mean(q * q, axis=-1, keepdims=True) + EPS)
        rk = lax.rsqrt(jnp.mean(k * k, axis=-1, keepdims=True) + EPS)
        gqv = gq_ref[...]
        gkv = gk_ref[...]
        qn_s[...] = (q * rq * (gqv * scale)).astype(BF16)
        kn_s[...] = (k * rk * gkv).astype(BF16)
        v_s[...] = _sb_heads(v_ref).astype(BF16)
        dkn_s[...] = jnp.zeros_like(dkn_s)
        dv_s[...] = jnp.zeros_like(dv_s)
        r_i = _iota2((tk, tk), 0)
        c_i = _iota2((tk, tk), 1)
        upto = (r_i <= c_i).astype(BF16)
        before = (r_i < c_i).astype(BF16)

        def qblock(i, _):
            rows = pl.ds(pl.multiple_of(i * tq, tq), tq)
            qi = qn_s[:, rows, :]
            doi = _sb_heads(do_ref, rows).astype(BF16)
            tot_i = jnp.max(_sb_heads(o_ref, rows), axis=-1, keepdims=True)
            jd = (i * tq) // tk

            def tile(j, dqn, run, run_b, masked):
                cols = pl.ds(pl.multiple_of(j * tk, tk), tk)
                kj = kn_s[:, cols, :]
                lr, ls, mask = _sb_tile(qi, kj, i * tq, j * tk, masked)
                later = tot_i - (_dot2(lr, upto) + run)
                a = jnp.exp(ls + later)
                if masked:
                    a = jnp.where(mask, a, 0.0)
                b = _hdot(doi, v_s[:, cols, :], NT) * a
                cum = _hdot(b, before, NN) + run_b
                beta = jnp.exp(ls)
                dz = b * (1.0 - beta) - cum * beta
                if masked:
                    dz = jnp.where(mask, dz, 0.0)
                dzb = dz.astype(BF16)
                dv_s[:, cols, :] += _hdot(a, doi, TN)
                dkn_s[:, cols, :] += _hdot(dzb, qi, TN)
                dqn = dqn + _hdot(dzb, kj, NN)
                return dqn, run + jnp.sum(lr, axis=-1, keepdims=True), run_b + jnp.sum(b, axis=-1, keepdims=True)

            zero1 = jnp.zeros((SB_PAIR, tq, 1), F32)
            carry = lax.fori_loop(0, jd, lambda j, c: tile(j, c[0], c[1], c[2], False),
                                  (jnp.zeros((SB_PAIR, tq, HEAD_DIM), F32), zero1, zero1))
            dqn, _, _ = tile(jd, carry[0], carry[1], carry[2], True)
            dqn_s[:, rows, :] = dqn * scale
            return 0

        lax.fori_loop(0, nq, qblock, 0)

        @pl.when(h == 0)
        def _():
            dgq_ref[...] = jnp.zeros_like(dgq_ref)
            dgk_ref[...] = jnp.zeros_like(dgk_ref)

        heads_sum = lambda z: jnp.sum(jnp.sum(z, axis=1, keepdims=True), axis=0)
        dqn = dqn_s[...]
        qh = q * rq
        dgq_ref[...] += heads_sum(dqn * qh)
        dy = dqn * gqv
        dq = (rq * (dy - qh * jnp.mean(dy * qh, axis=-1, keepdims=True))).astype(BF16)
        dkn = dkn_s[...]
        kh = k * rk
        dgk_ref[...] += heads_sum(dkn * kh)
        dyk = dkn * gkv
        dk = (rk * (dyk - kh * jnp.mean(dyk * kh, axis=-1, keepdims=True))).astype(BF16)
        dv = dv_s[...].astype(BF16)
        for hh in range(SB_PAIR):
            lanes = slice(hh * HEAD_DIM, (hh + 1) * HEAD_DIM)
            dq_ref[:, lanes] = dq[hh]
            dk_ref[:, lanes] = dk[hh]
            dv_ref[:, lanes] = dv[hh]

    npair = SB_HEADS // SB_PAIR
    vec = pl.BlockSpec((1, HEAD_DIM), lambda h: (0, 0))
    hb = lambda off: pl.BlockSpec((S, SB_PW), lambda h: (0, off + h), pipeline_mode=pl.Buffered(1))
    dq, dk, dv, dgq, dgk = pl.pallas_call(
        body, grid=(npair,),
        in_specs=[hb(0), hb(0), hb(0), hb(npair), hb(2 * npair), vec, vec],
        out_specs=(hb(0), hb(0), hb(0), vec, vec),
        out_shape=(jax.ShapeDtypeStruct((S, MIX_WIDTH), BF16),) * 3 + (jax.ShapeDtypeStruct((1, HEAD_DIM), F32),) * 2,
        scratch_shapes=[pltpu.VMEM((SB_PAIR, S, HEAD_DIM), BF16)] * 3 + [pltpu.VMEM((SB_PAIR, S, HEAD_DIM), F32)] * 3,
        compiler_params=_cp(("arbitrary",)), name=name)(
            dmix, tot, proj, proj, proj, gq.reshape(1, HEAD_DIM), gk.reshape(1, HEAD_DIM))
    return [dq, dk, dv], dgq, dgk


def _shift_down(x, k):
    if k == 0:
        return x
    r = pltpu.roll(x, k, 0)
    return jnp.where(_iota2(x.shape, 0) >= k, r, 0.0)


def _shift_up(x, k):
    if k == 0:
        return x
    n = x.shape[0]
    r = pltpu.roll(x, n - k, 0)
    return jnp.where(_iota2(x.shape, 0) < n - k, r, 0.0)


def _conv(x, w):
    c = w[DN_CONV - 1] * x
    for k in range(1, DN_CONV):
        c = c + w[DN_CONV - 1 - k] * _shift_down(x, k)
    return c


def _dn_pre_fwd(proj, conv_w, col0, ncols, *, l2, scale, name):
    S = proj.shape[0]
    cb = col0 // HEAD_DIM

    def body(x_ref, w_ref, o_ref):
        c = _conv(x_ref[...], [w_ref[k:k + 1, :] for k in range(DN_CONV)])
        a = c * _sigmoid(c)
        if l2:
            a = a * (lax.rsqrt(jnp.sum(a * a, axis=-1, keepdims=True) + EPS) * scale)
        o_ref[...] = a

    return pl.pallas_call(
        body, grid=(ncols // HEAD_DIM,),
        in_specs=[pl.BlockSpec((S, HEAD_DIM), lambda j: (0, cb + j)), pl.BlockSpec((DN_CONV, HEAD_DIM), lambda j: (0, cb + j))],
        out_specs=pl.BlockSpec((S, HEAD_DIM), lambda j: (0, j)), out_shape=jax.ShapeDtypeStruct((S, ncols), F32),
        compiler_params=_cp(("parallel",)), name=name)(proj, conv_w)


def _dn_pre_bwd(dout, proj, conv_w, col0, ncols, *, l2, scale, name):
    S = proj.shape[0]
    cb = col0 // HEAD_DIM
    dw_in = HEAD_DIM

    def body(d_ref, x_ref, w_ref, dx_ref, dw_ref):
        x = x_ref[...]
        w = [w_ref[k:k + 1, :] for k in range(DN_CONV)]
        c = _conv(x, w)
        sg = _sigmoid(c)
        a = c * sg
        d = d_ref[...]
        if l2:
            r = lax.rsqrt(jnp.sum(a * a, axis=-1, keepdims=True) + EPS)
            y = a * r
            d = d * scale
            d = r * (d - y * jnp.sum(d * y, axis=-1, keepdims=True))
        dc = d * sg * (1.0 + c * (1.0 - sg))
        dx = w[DN_CONV - 1] * dc
        for k in range(1, DN_CONV):
            dx = dx + w[DN_CONV - 1 - k] * _shift_up(dc, k)
        dx_ref[...] = dx.astype(BF16)
        for k in range(DN_CONV):
            dw_ref[3 - k:4 - k, :] = jnp.sum(dc * _shift_down(x, k), axis=0, keepdims=True)

    return pl.pallas_call(
        body, grid=(ncols // HEAD_DIM,),
        in_specs=[pl.BlockSpec((S, dw_in), lambda j: (0, j)), pl.BlockSpec((S, HEAD_DIM), lambda j: (0, cb + j)),
                  pl.BlockSpec((DN_CONV, HEAD_DIM), lambda j: (0, cb + j))],
        out_specs=(pl.BlockSpec((S, HEAD_DIM), lambda j: (0, j)), pl.BlockSpec((DN_CONV, HEAD_DIM), lambda j: (0, j))),
        out_shape=(jax.ShapeDtypeStruct((S, ncols), BF16), jax.ShapeDtypeStruct((DN_CONV, ncols), F32)),
        compiler_params=_cp(("parallel",)), name=name)(dout, proj, conv_w)


def _dn_ab_fwd(proj, a_log, dt_bias, *, name, tm=512):
    S = proj.shape[0]
    tm = min(tm, S)
    ab = P0_AB // LANE

    def body(a_ref, b_ref, al_ref, dt_ref, g_ref, be_ref):
        g_ref[...] = -jnp.exp(al_ref[...]) * _softplus(a_ref[...] + dt_ref[...])
        be_ref[...] = _sigmoid(b_ref[...])

    vec = pl.BlockSpec((1, LANE), lambda i: (0, 0))
    out = pl.BlockSpec((tm, LANE), lambda i: (i, 0))
    return pl.pallas_call(
        body, grid=(S // tm,),
        in_specs=[pl.BlockSpec((tm, LANE), lambda i: (i, ab)), pl.BlockSpec((tm, LANE), lambda i: (i, ab + 1)), vec, vec],
        out_specs=(out, out), out_shape=(jax.ShapeDtypeStruct((S, LANE), F32),) * 2,
        compiler_params=_cp(("parallel",)), name=name)(proj, proj, a_log, dt_bias)


def _dn_ab_bwd(dg, dbeta, proj, a_log, dt_bias, *, name, tm=512):
    S = proj.shape[0]
    tm = min(tm, S)
    ab = P0_AB // LANE

    def body(dg_ref, db_ref, a_ref, b_ref, al_ref, dt_ref, dab_ref, dal_ref, ddt_ref):
        i = pl.program_id(0)
        ea = jnp.exp(al_ref[...])
        u = a_ref[...] + dt_ref[...]
        dgv = dg_ref[...]
        da = dgv * (-ea) * _sigmoid(u)
        be = _sigmoid(b_ref[...])
        dab_ref[:, 0:LANE] = da.astype(BF16)
        dab_ref[:, LANE:2 * LANE] = (db_ref[...] * be * (1.0 - be)).astype(BF16)
        dab_ref[:, 2 * LANE:] = jnp.zeros((tm, 2 * LANE), BF16)

        @pl.when(i == 0)
        def _():
            dal_ref[...] = jnp.zeros_like(dal_ref)
            ddt_ref[...] = jnp.zeros_like(ddt_ref)

        dal_ref[...] += jnp.sum(dgv * (-ea) * _softplus(u), axis=0, keepdims=True)
        ddt_ref[...] += jnp.sum(da, axis=0, keepdims=True)

    vec = pl.BlockSpec((1, LANE), lambda i: (0, 0))
    row = pl.BlockSpec((tm, LANE), lambda i: (i, 0))
    return pl.pallas_call(
        body, grid=(S // tm,),
        in_specs=[row, row, pl.BlockSpec((tm, LANE), lambda i: (i, ab)), pl.BlockSpec((tm, LANE), lambda i: (i, ab + 1)), vec, vec],
        out_specs=(pl.BlockSpec((tm, 4 * LANE), lambda i: (i, 0)), vec, vec),
        out_shape=(jax.ShapeDtypeStruct((S, 4 * LANE), BF16), jax.ShapeDtypeStruct((1, LANE), F32),
                   jax.ShapeDtypeStruct((1, LANE), F32)),
        compiler_params=_cp(("arbitrary",)), name=name)(dg, dbeta, proj, proj, a_log, dt_bias)


def _dot3(a, b):
    ah = a.astype(BF16)
    al = (a - ah.astype(F32)).astype(BF16)
    bh = b.astype(BF16)
    bl = (b - bh.astype(F32)).astype(BF16)
    d = lambda u, v: lax.dot_general(u, v, (NN, ((), ())), preferred_element_type=F32)
    return d(ah, bh) + (d(ah, bl) + d(al, bh))


DN_PAIR = 4
DN_QK = DN_PAIR // 2


def _dn_qk_heads(ref, rows):
    return jnp.stack([ref[rows, (hh // 2) * HEAD_DIM:(hh // 2 + 1) * HEAD_DIM] for hh in range(DN_PAIR)])


def _dn_big(shape, imap):
    return pl.BlockSpec(shape, imap, pipeline_mode=pl.Buffered(1))


_pdot = _hdot


def _tri_inverse(a):
    eye = (_iota2((CH, CH), 0) == _iota2((CH, CH), 1)).astype(F32)
    d3 = lambda u, v: jnp.stack([_dot3(u[i], v[i]) for i in range(DN_PAIR)])
    t = eye - a
    x = d3(a, a)
    n = 2
    while True:
        t = t + d3(t, x)
        n *= 2
        if n >= CH:
            break
        x = d3(x, x)
    return t


def _pick_col(m, n):
    return jnp.sum(jnp.where(_iota2(m.shape, 2) == n, m, 0.0), axis=2, keepdims=True)


def _dn_chunk_common(kk, qk, gc_c, gc_r, be_c):
    r_i = _iota2((CH, CH), 0)
    c_i = _iota2((CH, CH), 1)
    incl = r_i >= c_i
    strict = r_i > c_i
    dec = jnp.exp(jnp.where(incl, gc_c - gc_r, -1e30))
    e = jnp.exp(gc_c)
    gl = jnp.sum(jnp.where(_iota2((1, CH), 1) == CH - 1, gc_r, 0.0), axis=-1, keepdims=True)
    kds = jnp.exp(gl - gc_c)
    cd = jnp.exp(gl)
    a = jnp.where(strict, be_c * kk * dec, 0.0)
    p = qk * dec
    return dict(incl=incl, strict=strict, dec=dec, e=e, kds=kds, cd=cd, kk=kk, a=a, qk=qk, p=p)


def _dn_decay_tables(g_ref, b_ref, gcr, gcc, bcc):
    r_i = _iota2((CH, CH), 0)
    c_i = _iota2((CH, CH), 1)
    lc = (r_i >= c_i).astype(F32)
    eye = (r_i == c_i).astype(F32)
    for hh in range(DN_PAIR):
        g_rows_v = g_ref[hh]
        gcr[hh] = _fdot(g_rows_v, lc, NT)
        gcc[hh] = _fdot(lc, g_rows_v, NT)
        bcc[hh] = _fdot(eye, b_ref[hh], NT)
    return lc


def _dn_core_fwd(qn, kn, vc, g_rows, b_rows, out_g, *, name):
    S = qn.shape[0]
    nc = S // CH

    def body(q_ref, k_ref, v_ref, g_ref, b_ref, og_ref, o_ref, st_ref, t_ref, gcr, gcc, bcc):
        _dn_decay_tables(g_ref, b_ref, gcr, gcc, bcc)
        ogv = og_ref[...]

        def chunk(n, states):
            rows = pl.ds(pl.multiple_of(n * CH, CH), CH)
            q = _dn_qk_heads(q_ref, rows)
            k = _dn_qk_heads(k_ref, rows)
            kk = _pdot(k, k, NT)
            qk = _pdot(q, k, NT)
            v = jnp.stack([v_ref[rows, hh * HEAD_DIM:(hh + 1) * HEAD_DIM] for hh in range(DN_PAIR)])
            gc_c = _pick_col(gcc[...], n)
            be_c = _pick_col(bcc[...], n)
            gc_r = gcr[:, pl.ds(n, 1), :]
            c = _dn_chunk_common(kk, qk, gc_c, gc_r, be_c)
            t = _tri_inverse(c["a"])
            u0 = _pdot(t, be_c * v, NN)
            w = _pdot(t, (be_c * c["e"]) * k, NN)
            u = u0 - _pdot(w, states, NN)
            o = _pdot(c["e"] * q, states, NN) + _pdot(c["p"], u, NN)
            on = o * lax.rsqrt(jnp.mean(o * o, axis=-1, keepdims=True) + EPS) * ogv
            for hh in range(DN_PAIR):
                st_ref[hh, n] = states[hh]
                t_ref[hh, n] = t[hh]
                o_ref[rows, hh * HEAD_DIM:(hh + 1) * HEAD_DIM] = on[hh]
            return c["cd"] * states + _pdot(c["kds"] * k, u, TN)

        lax.fori_loop(0, nc, chunk, jnp.zeros((DN_PAIR, HEAD_DIM, HEAD_DIM), F32))

    qk_spec = pl.BlockSpec((S, DN_QK * HEAD_DIM), lambda h: (0, h))
    v_in = pl.BlockSpec((S, DN_PAIR * HEAD_DIM), lambda h: (0, h))
    v_spec = _dn_big((S, DN_PAIR * HEAD_DIM), lambda h: (0, h))
    rows_spec = pl.BlockSpec((DN_PAIR, LANE, CH), lambda h: (h, 0, 0))
    return pl.pallas_call(
        body, grid=(DN_V_HEADS // DN_PAIR,),
        in_specs=[qk_spec, qk_spec, v_in, rows_spec, rows_spec, pl.BlockSpec((1, HEAD_DIM), lambda h: (0, 0))],
        out_specs=(v_spec, _dn_big((DN_PAIR, nc, HEAD_DIM, HEAD_DIM), lambda h: (h, 0, 0, 0)),
                   _dn_big((DN_PAIR, nc, CH, CH), lambda h: (h, 0, 0, 0))),
        out_shape=(jax.ShapeDtypeStruct((S, MIX_WIDTH), F32), jax.ShapeDtypeStruct((DN_V_HEADS, nc, HEAD_DIM, HEAD_DIM), F32),
                   jax.ShapeDtypeStruct((DN_V_HEADS, nc, CH, CH), F32)),
        scratch_shapes=[pltpu.VMEM((DN_PAIR, LANE, CH), F32), pltpu.VMEM((DN_PAIR, CH, LANE), F32),
                        pltpu.VMEM((DN_PAIR, CH, LANE), F32)],
        compiler_params=_cp(("parallel",)), name=name)(qn, kn, vc, g_rows, b_rows, out_g.reshape(1, HEAD_DIM))


def _dn_chunk_bwd(q, k, v, kk, qk, state, t, gc_c, gc_r, be_c, don, ogv, ds_next):
    ones = jnp.ones((CH, LANE), F32)
    last_row = _iota2((CH, 1), 0) == CH - 1
    rowsum = lambda z: jnp.sum(z, axis=-1, keepdims=True)
    colsum = lambda z: jnp.sum(z, axis=-2, keepdims=True)
    c = _dn_chunk_common(kk, qk, gc_c, gc_r, be_c)
    e, kds, cd, dec, a, p = c["e"], c["kds"], c["cd"], c["dec"], c["a"], c["p"]
    vb = be_c * v
    kbe = (be_c * e) * k
    u0 = _pdot(t, vb, NN)
    w = _pdot(t, kbe, NN)
    u = u0 - _pdot(w, state, NN)
    qd = e * q
    kd = kds * k
    o = _pdot(qd, state, NN) + _pdot(p, u, NN)
    r = lax.rsqrt(jnp.mean(o * o, axis=-1, keepdims=True) + EPS)
    y = o * r
    dog = colsum(don * y)
    dy = don * ogv
    d_o = r * (dy - y * jnp.mean(dy * y, axis=-1, keepdims=True))
    du = _pdot(p, d_o, TN) + _pdot(kd, ds_next, NN)
    dqd = _pdot(d_o, state, NT)
    dstate = _pdot(qd, d_o, TN) + cd * ds_next - _pdot(w, du, TN)
    dcd = colsum(rowsum(ds_next * state))
    dkd = _pdot(u, ds_next, NT)
    dw = -_pdot(du, state, NT)
    dvb = _pdot(t, du, TN)
    dkbe = _pdot(t, dw, TN)
    da = -jnp.where(c["strict"], _pdot(dvb, u0, NT) + _pdot(dkbe, w, NT), 0.0)
    dp = jnp.where(c["incl"], _pdot(d_o, u, NT), 0.0)
    gmat = da * a + dp * p
    dad = da * dec
    x = be_c * dad
    dpd = dp * dec
    dk = _pdot(x, k, NN) + _pdot(x, k, TN) + _pdot(dpd, q, TN)
    dq = _pdot(dpd, k, NN) + e * dqd
    dbe = rowsum(dad * c["kk"])
    dgc = rowsum(gmat) + rowsum(dqd * q) * e
    rk = rowsum(dkd * k) * kds
    dk = dk + kds * dkd
    dgc = dgc - rk
    dgl = colsum(rk) + dcd * cd
    sk = rowsum(dkbe * k)
    dk = dk + (be_c * e) * dkbe
    dbe = dbe + sk * e + rowsum(dvb * v)
    dgc = dgc + sk * be_c * e
    dgc = dgc + jnp.where(last_row, dgl, 0.0)
    dgc = dgc - _pdot(gmat, ones, TN, dot=_fdot)
    return dq, dk, be_c * dvb, dgc, dbe, dog, dstate


def _dn_core_bwd(dmix, qn, kn, vc, g_rows, b_rows, out_g, states, tinv, *, name):
    S = qn.shape[0]
    nc = S // CH

    def body(do_ref, q_ref, k_ref, v_ref, g_ref, b_ref, og_ref, st_ref, t_ref,
             dq_ref, dk_ref, dv_ref, dg_ref, db_ref, dog_ref, gcr, gcc, bcc, dgc_acc):
        h = pl.program_id(0)
        lc = _dn_decay_tables(g_ref, b_ref, gcr, gcc, bcc)
        ogv = og_ref[...]
        dgc_acc[...] = jnp.zeros_like(dgc_acc)
        db_ref[...] = jnp.zeros_like(db_ref)
        lane_n = _iota2((CH, LANE), 1)

        @pl.when(h == 0)
        def _():
            dog_ref[...] = jnp.zeros_like(dog_ref)

        def chunk(m, carry):
            ds_nexts, dog = carry
            n = nc - 1 - m
            rows = pl.ds(pl.multiple_of(n * CH, CH), CH)
            q = _dn_qk_heads(q_ref, rows)
            k = _dn_qk_heads(k_ref, rows)
            kk = _pdot(k, k, NT)
            qk = _pdot(q, k, NT)
            heads = lambda ref: jnp.stack([ref[rows, hh * HEAD_DIM:(hh + 1) * HEAD_DIM] for hh in range(DN_PAIR)])
            state = jnp.stack([st_ref[hh, n] for hh in range(DN_PAIR)])
            t = jnp.stack([t_ref[hh, n] for hh in range(DN_PAIR)])
            dq, dk, dv, dgc, dbe, dog_h, dstate = _dn_chunk_bwd(
                q, k, heads(v_ref), kk, qk, state, t, _pick_col(gcc[...], n), gcr[:, pl.ds(n, 1), :],
                _pick_col(bcc[...], n), heads(do_ref), ogv, ds_nexts)
            for hh in range(DN_PAIR):
                dv_ref[rows, hh * HEAD_DIM:(hh + 1) * HEAD_DIM] = dv[hh]
            dgc_acc[...] = jnp.where(lane_n == n, dgc, dgc_acc[...])
            db_ref[...] = jnp.where(lane_n == n, dbe, db_ref[...])
            for i in range(DN_QK):
                dq_ref[rows, i * HEAD_DIM:(i + 1) * HEAD_DIM] = dq[2 * i] + dq[2 * i + 1]
                dk_ref[rows, i * HEAD_DIM:(i + 1) * HEAD_DIM] = dk[2 * i] + dk[2 * i + 1]
            return dstate, dog + jnp.sum(dog_h, axis=0)

        _, dog = lax.fori_loop(0, nc, chunk, (jnp.zeros((DN_PAIR, HEAD_DIM, HEAD_DIM), F32), jnp.zeros((1, HEAD_DIM), F32)))
        dog_ref[...] += dog
        for hh in range(DN_PAIR):
            dg_ref[hh] = _fdot(lc, dgc_acc[hh], TN)

    qk_spec = _dn_big((S, DN_QK * HEAD_DIM), lambda h: (0, h))
    v_spec = _dn_big((S, DN_PAIR * HEAD_DIM), lambda h: (0, h))
    rows_spec = pl.BlockSpec((DN_PAIR, LANE, CH), lambda h: (h, 0, 0))
    cols_spec = pl.BlockSpec((DN_PAIR, CH, LANE), lambda h: (h, 0, 0))
    vec = pl.BlockSpec((1, HEAD_DIM), lambda h: (0, 0))
    qk_out = jax.ShapeDtypeStruct((S, DN_QK_WIDTH), F32)
    return pl.pallas_call(
        body, grid=(DN_V_HEADS // DN_PAIR,),
        in_specs=[v_spec, qk_spec, qk_spec, v_spec, rows_spec, rows_spec, vec,
                  _dn_big((DN_PAIR, nc, HEAD_DIM, HEAD_DIM), lambda h: (h, 0, 0, 0)),
                  _dn_big((DN_PAIR, nc, CH, CH), lambda h: (h, 0, 0, 0))],
        out_specs=(qk_spec, qk_spec, v_spec, cols_spec, cols_spec, vec),
        out_shape=(qk_out, qk_out, jax.ShapeDtypeStruct((S, MIX_WIDTH), F32), jax.ShapeDtypeStruct((DN_V_HEADS, CH, LANE), F32),
                   jax.ShapeDtypeStruct((DN_V_HEADS, CH, LANE), F32), jax.ShapeDtypeStruct((1, HEAD_DIM), F32)),
        scratch_shapes=[pltpu.VMEM((DN_PAIR, LANE, CH), F32), pltpu.VMEM((DN_PAIR, CH, LANE), F32),
                        pltpu.VMEM((DN_PAIR, CH, LANE), F32), pltpu.VMEM((DN_PAIR, CH, LANE), F32)],
        compiler_params=_cp(("arbitrary",)), name=name)(
            dmix, qn, kn, vc, g_rows, b_rows, out_g.reshape(1, HEAD_DIM), states, tinv)


def _rows_form(x, nc):
    t = x[:, :DN_V_HEADS].T.reshape(DN_V_HEADS, nc, CH)
    return jnp.pad(t, ((0, 0), (0, LANE - nc), (0, 0)))


def _cols_to_nat(x, nc):
    t = jnp.transpose(x[:, :, :nc], (2, 1, 0)).reshape(nc * CH, DN_V_HEADS)
    return jnp.pad(t, ((0, 0), (0, LANE - DN_V_HEADS)))


_C_QKV = 2 * DN_QK_WIDTH + MIX_WIDTH


def _padded_pieces(lo, hi):
    a0, b0, x0 = _C_QKV, _C_QKV + DN_V_HEADS, _C_QKV + 2 * DN_V_HEADS
    out = []
    for t0, t1, shift in ((0, a0, 0), (a0, b0, P0_AB - a0), (b0, x0, P0_AB + LANE - b0), (x0, DN_PROJ, a0 - x0)):
        s, e = max(lo, t0), min(hi, t1)
        if s < e:
            out.append((s + shift, e + shift))
    return out


def _shard_pieces(s):
    return _padded_pieces(s * P0_SHARD, (s + 1) * P0_SHARD)


def _shard_runs(s):
    return [(lo // LANE, -(-hi // LANE)) for lo, hi in _shard_pieces(s)]


WIN_COLS = LANE * max(sum(b - a for a, b in _shard_runs(s)) for s in range(N_CHIPS))


def _pack_own_shard(s, shard):
    zeros = lambda n: jnp.zeros((shard.shape[0], n), shard.dtype)
    out, t = [], 0
    for (lo, hi), (b0, b1) in zip(_shard_pieces(s), _shard_runs(s)):
        out += [zeros(lo - b0 * LANE), shard[:, t:t + hi - lo], zeros(b1 * LANE - hi)]
        t += hi - lo
    out.append(zeros(WIN_COLS - LANE * sum(b - a for a, b in _shard_runs(s))))
    return jnp.concatenate([o for o in out if o.shape[1]], axis=1)


def _unpack_own_shard(s, win):
    out, off = [], 0
    for (lo, hi), (b0, b1) in zip(_shard_pieces(s), _shard_runs(s)):
        start = off + lo - b0 * LANE
        out.append(win[:, start:start + hi - lo])
        off += (b1 - b0) * LANE
    return jnp.concatenate(out, axis=1)


def _windows_to_padded(wins):
    src = {}
    for s in range(N_CHIPS):
        off = 0
        for b0, b1 in _shard_runs(s):
            for b in range(b0, b1):
                src.setdefault(b, []).append((s, off + (b - b0) * LANE))
            off += (b1 - b0) * LANE
    out, b = [], 0
    while b < P0 // LANE:
        if b not in src:
            out.append(jnp.zeros((wins.shape[1], LANE), wins.dtype))
            b += 1
        elif len(src[b]) > 1:
            out.append(sum(wins[s][:, o:o + LANE] for s, o in src[b]))
            b += 1
        else:
            (s, o), n = src[b][0], 1
            while src.get(b + n) == [(s, o + n * LANE)]:
                n += 1
            out.append(wins[s][:, o:o + n * LANE])
            b += n
    return jnp.concatenate(out, axis=1)


def _padded_to_window(s, g):
    out = [g[:, b0 * LANE:b1 * LANE] for b0, b1 in _shard_runs(s)]
    rest = WIN_COLS - sum(o.shape[1] for o in out)
    return jnp.concatenate(out + ([jnp.zeros((g.shape[0], rest), g.dtype)] if rest else []), axis=1)


def _pad_lane(v):
    v = v.reshape(1, -1)
    return jnp.pad(v, ((0, 0), (0, LANE - v.shape[1])))


SLOT1 = SB_PROJ // N_CHIPS
MM_TN = 512


def _local_step(x, mem, target, norm_g, mem_norm_g, xa_q_g, xa_k_g, w_in0, conv_w, a_log, dt_bias, out_g, sb_q_g, sb_k_g,
                late_weights, early_grads, grads_swapped):
    S = x.shape[0]
    nc = S // CH
    al = _pad_lane(a_log)
    dtb = _pad_lane(dt_bias)
    q_scale = HEAD_DIM ** -0.5
    tiles1 = SLOT1 // MM_TN

    kv_rhs = lambda l: pl.BlockSpec((N_CHIPS, None, D_MODEL // N_CHIPS, MM_TN), lambda i, j: (0, l, 0, j))
    kv_rhs_t = lambda l: pl.BlockSpec((None, None, D_MODEL // N_CHIPS, 2 * XA_WIDTH), lambda i, j: (j, l, 0, 0))
    out_rhs = lambda l: pl.BlockSpec((N_CHIPS, None, INNER // N_CHIPS, MM_TN), lambda i, j: (0, l, 0, j))
    out_rhs_t = lambda l: pl.BlockSpec((None, None, MM_TN, D_MODEL), lambda i, j: (j // 2, l, j % 2, 0))
    in1_rhs = pl.BlockSpec((None, 2, D_MODEL // 2, MM_TN), lambda i, j: (j // tiles1, 0, 0, j % tiles1))
    in1_rhs_t = pl.BlockSpec((None, None, D_MODEL // 2, MM_TN), lambda i, j, k: (k // tiles1, j, 0, k % tiles1))
    slot_rows = lambda rows: dict(
        tm=rows, o_spec=pl.BlockSpec((None, rows, MM_TN), lambda i, j: (i, 0, j)),
        o_shape=jax.ShapeDtypeStruct((N_CHIPS, rows, 2 * XA_WIDTH), BF16))
    in1_out = dict(tm=D_MODEL // 2, o_spec=pl.BlockSpec((None, None, D_MODEL // 2, MM_TN),
                                                        lambda i, j: (j // tiles1, i, 0, j % tiles1)),
                   o_shape=jax.ShapeDtypeStruct((N_CHIPS, 2, D_MODEL // 2, SLOT1), BF16))

    h0 = _rmsnorm_fwd(x, norm_g[0], name="norm0")
    proj0 = _matmul(h0, w_in0, name="proj0")
    qn = _dn_pre_fwd(proj0, conv_w, 0, DN_QK_WIDTH, l2=True, scale=q_scale, name="dn_pre_q")
    kn = _dn_pre_fwd(proj0, conv_w, DN_QK_WIDTH, DN_QK_WIDTH, l2=True, scale=1.0, name="dn_pre_k")
    vc = _dn_pre_fwd(proj0, conv_w, 2 * DN_QK_WIDTH, MIX_WIDTH, l2=False, scale=1.0, name="dn_pre_v")
    g_nat, b_nat = _dn_ab_fwd(proj0, al, dtb, name="dn_ab")
    g_rows = _rows_form(g_nat, nc)
    b_rows = _rows_form(b_nat, nc)
    mix0, states, tinv = _dn_core_fwd(qn, kn, vc, g_rows, b_rows, out_g, name="dn_core")
    w_kv, w_out, w_in1 = late_weights(mix0)
    mem_n = _rmsnorm_fwd(mem, mem_norm_g, name="mem_norm")
    kv = [_matmul(mem_n, w_kv, n=2 * XA_WIDTH, tn=MM_TN, b_spec=kv_rhs(l), name=f"kv{l}") for l in range(2)]
    xa0 = _xa_fwd(proj0, P0_XQ, kv[0], xa_q_g[0], xa_k_g[0], name="xa0")
    y0 = _gate_fwd(mix0, xa0, proj0, P0_Z, name="gate0")
    x1 = _matmul(y0, w_out, n=D_MODEL, tn=MM_TN, b_spec=out_rhs(0), res=x, name="out0")

    h1 = _rmsnorm_fwd(x1, norm_g[1], name="norm1")
    proj1 = _matmul(h1, w_in1, n=SB_PROJ, tn=MM_TN, b_spec=in1_rhs, name="proj1")
    mix1, tot1 = _sb_fwd(proj1, sb_q_g, sb_k_g, name="sb")
    xa1 = _xa_fwd(proj1, P1_XQ, kv[1], xa_q_g[1], xa_k_g[1], name="xa1")
    y1 = _gate_fwd(mix1, xa1, proj1, P1_Z, name="gate1")
    x2 = _matmul(y1, w_out, n=D_MODEL, tn=MM_TN, b_spec=out_rhs(1), res=x1, name="out1")

    dx2, loss_vec = _loss_head(x2, target, name="loss")

    d_wout1 = _matmul(y1, dx2, ta=True, name="d_wout1", **slot_rows(INNER // N_CHIPS))
    dy1 = _matmul(dx2, w_out, tb=True, n=INNER, tn=MM_TN, b_spec=out_rhs_t(1), name="dy1")
    dcat1, dz1 = _gate_bwd(dy1, mix1, xa1, proj1, P1_Z, name="gate1_bwd")
    dqkv1, d_sbq, d_sbk = _sb_bwd(dcat1, tot1, proj1, sb_q_g, sb_k_g, name="sb_bwd")
    dxq1, dkv1, d_xaq1, d_xak1 = _xa_bwd(dcat1, proj1, P1_XQ, kv[1], xa_q_g[1], xa_k_g[1], name="xa1_bwd")
    dproj1 = dqkv1 + [dxq1, dz1]
    d_win1 = _matmul(h1, dproj1, ta=True, name="d_win1", **in1_out)
    d_wkv1 = _matmul(mem_n, dkv1, ta=True, name="d_wkv1", **slot_rows(D_MODEL // N_CHIPS))
    token = early_grads(1, d_win1, d_wout1, d_wkv1)
    dproj1 = dqkv1 + [dxq1 + token[0, 0].astype(BF16), dz1]
    dh1 = _matmul(dproj1, w_in1, tb=True, n=D_MODEL, tn=D_MODEL // 2, tk=MM_TN, b_spec=in1_rhs_t, name="dh1")
    token = grads_swapped(dh1)
    dx1, d_ng1 = _rmsnorm_bwd(dh1, x1, norm_g[1] + token[0, 0], dx2, name="norm1_bwd")

    d_wout0 = _matmul(y0, dx1, ta=True, name="d_wout0", **slot_rows(INNER // N_CHIPS))
    dy0 = _matmul(dx1, w_out, tb=True, n=INNER, tn=MM_TN, b_spec=out_rhs_t(0), name="dy0")
    dcat0, dz0 = _gate_bwd(dy0, mix0, xa0, proj0, P0_Z, name="gate0_bwd")
    dqv, dkv_h, dvc, dg_cols, db_cols, d_outg = _dn_core_bwd(
        dcat0, qn, kn, vc, g_rows, b_rows, out_g, states, tinv, name="dn_core_bwd")
    dpq, dwq = _dn_pre_bwd(dqv, proj0, conv_w, 0, DN_QK_WIDTH, l2=True, scale=q_scale, name="dn_pre_q_bwd")
    dpk, dwk = _dn_pre_bwd(dkv_h, proj0, conv_w, DN_QK_WIDTH, DN_QK_WIDTH, l2=True, scale=1.0, name="dn_pre_k_bwd")
    dpv, dwv = _dn_pre_bwd(dvc, proj0, conv_w, 2 * DN_QK_WIDTH, MIX_WIDTH, l2=False, scale=1.0, name="dn_pre_v_bwd")
    dab, d_alog, d_dt = _dn_ab_bwd(_cols_to_nat(dg_cols, nc), _cols_to_nat(db_cols, nc), proj0, al, dtb, name="dn_ab_bwd")
    dxq0, dkv0, d_xaq0, d_xak0 = _xa_bwd(dcat0, proj0, P0_XQ, kv[0], xa_q_g[0], xa_k_g[0], name="xa0_bwd")
    d_win0 = _matmul(h0, [dpq, dpk, dpv, dxq0, dz0, dab], ta=True, out_dtype=BF16, name="d_win0")
    d_wkv0 = _matmul(mem_n, dkv0, ta=True, name="d_wkv0", **slot_rows(D_MODEL // N_CHIPS))
    token = early_grads(0, d_win0, d_wout0, d_wkv0)
    zero = token[0, 0]
    dh0 = _matmul([dpq, dpk, dpv, dxq0, dz0, dab + zero.astype(BF16)], w_in0, tb=True, tk=MM_TN, name="dh0")
    dx0, d_ng0 = _rmsnorm_bwd(dh0, x, norm_g[0] + zero, dx1, name="norm0_bwd")

    dmem0 = _matmul(dkv0, w_kv, tb=True, n=D_MODEL, tn=D_MODEL // N_CHIPS, b_spec=kv_rhs_t(0), name="dmem0")
    dmem_n = _matmul(dkv1, w_kv, tb=True, n=D_MODEL, tn=D_MODEL // N_CHIPS, b_spec=kv_rhs_t(1), res=dmem0, name="dmem1")
    _, d_memg = _rmsnorm_bwd(dmem_n, mem, mem_norm_g, None, name="mem_norm_bwd")

    grads = dict(
        norm_g=jnp.concatenate([d_ng0, d_ng1], axis=0), mem_norm_g=d_memg.reshape(-1),
        xa_q_norm_g=jnp.concatenate([d_xaq0, d_xaq1], axis=0), xa_k_norm_g=jnp.concatenate([d_xak0, d_xak1], axis=0),
        dn_conv_w=jnp.concatenate([dwq, dwk, dwv], axis=1),
        dn_a_log=d_alog[:, :DN_V_HEADS], dn_dt_bias=d_dt[:, :DN_V_HEADS], dn_out_norm_g=d_outg,
        sb_q_norm_g=d_sbq, sb_k_norm_g=d_sbk)
    return loss_vec, dx0, grads


ANY = pl.BlockSpec(memory_space=pl.ANY)


def _place():
    x, y, c = lax.axis_index("x"), lax.axis_index("y"), lax.axis_index("c")
    chips = [(1 - x, y), (x, 1 - y), (1 - x, 1 - y)]
    return x, y, c, 2 * x + y, (x, y, 1 - c), chips


def _rcopy(src, dst, send, recv, i, dev):
    return pltpu.make_async_remote_copy(src_ref=src, dst_ref=dst, send_sem=send.at[i], recv_sem=recv.at[i],
                                        device_id=dev, device_id_type=MESH)


def _swap_halves(xs, *, name):
    nt = len(xs)

    def body(*refs):
        src, dst = refs[:nt], refs[nt:2 * nt]
        send, recv = refs[2 * nt:]
        x, y, c, j, sib, chips = _place()
        cps = []
        for t in range(nt):
            for s in range(N_CHIPS):
                cps.append(_rcopy(src[t].at[s, 1 - c], dst[t].at[s], send, recv, 4 * t + s, sib))
                cps[-1].start()
        for cp in cps:
            cp.wait_recv()
        for cp in cps:
            cp.wait_send()

    return pl.pallas_call(
        body, in_specs=[ANY] * nt, out_specs=[ANY] * nt,
        out_shape=[jax.ShapeDtypeStruct((N_CHIPS,) + a.shape[2:], a.dtype) for a in xs],
        scratch_shapes=[pltpu.SemaphoreType.DMA((4 * nt,)), pltpu.SemaphoreType.DMA((4 * nt,))], name=name)(*xs)


def _swap_with_sibling(fs, *, name):
    nt = len(fs)

    def body(*refs):
        src, dst = refs[:nt], refs[nt:2 * nt]
        send, recv = refs[2 * nt:]
        x, y, c, j, sib, chips = _place()
        cps = [_rcopy(src[t], dst[t], send, recv, t, sib) for t in range(nt)]
        for cp in cps:
            cp.start()
        for cp in cps:
            cp.wait_recv()
        for cp in cps:
            cp.wait_send()

    return pl.pallas_call(
        body, in_specs=[ANY] * nt, out_specs=[ANY] * nt,
        out_shape=[jax.ShapeDtypeStruct(a.shape, a.dtype) for a in fs],
        scratch_shapes=[pltpu.SemaphoreType.DMA((nt,)), pltpu.SemaphoreType.DMA((nt,))], name=name)(*fs)


HBM_SPEC = pl.BlockSpec(memory_space=pltpu.HBM)
SEM_SPEC = pl.BlockSpec(memory_space=pltpu.SEMAPHORE)
SIDE_EFFECT = pltpu.SideEffectType.DATAFLOW_SIDE_EFFECTING


def _gather_plan(src, land):
    x, y, c, j, sib, chips = _place()
    return [(src[t].at[c], land[t].at[j, c], (cx, cy, c), land[t].at[2 * cx + cy, c])
            for t in range(len(src)) for cx, cy in chips]


def _scatter_plan(src, land):
    x, y, c, j, sib, chips = _place()
    return [(src[t].at[2 * cx + cy], land[t].at[k], (cx, cy, c), land[t].at[k])
            for t in range(len(src)) for k, (cx, cy) in enumerate(chips)]


def _swap_plan(src, land):
    x, y, c, j, sib, chips = _place()
    return [(src[t].at[s, 1 - c], land[t].at[s], sib, land[t].at[s]) for t in range(len(src)) for s in range(N_CHIPS)]


def _exchange_start(srcs, lands, plan, *, name, per_tensor=3):
    ns, nb = len(srcs), len(srcs) + len(lands)
    n = per_tensor * ns

    def body(*refs):
        send, recv, token = refs[nb], refs[nb + 1], refs[-1]
        for i, (s, d, dev, _) in enumerate(plan(refs[:ns], refs[ns:nb])):
            _rcopy(s, d, send, recv, i, dev).start()
        token[...] = jnp.zeros_like(token)

    bufs = list(srcs) + list(lands)
    outs = pl.pallas_call(
        body, name=name,
        out_shape=(pltpu.SemaphoreType.DMA((n,)), pltpu.SemaphoreType.DMA((n,)), *[pltpu.HBM(a.shape, a.dtype) for a in bufs],
                   jax.ShapeDtypeStruct((8, LANE), F32)),
        in_specs=[HBM_SPEC] * nb, out_specs=(SEM_SPEC, SEM_SPEC, *[HBM_SPEC] * nb, pl.BlockSpec(memory_space=pltpu.VMEM)),
        input_output_aliases={i: 2 + i for i in range(nb)},
        compiler_params=pltpu.CompilerParams(has_side_effects=SIDE_EFFECT))(
            *[pltpu.with_memory_space_constraint(a, pltpu.HBM) for a in bufs])
    return outs[0], outs[1], list(outs[2:2 + ns]), list(outs[2 + ns:2 + nb]), outs[-1]


def _exchange_wait(srcs, lands, send, recv, after, plan, *, name):
    ns, nb = len(srcs), len(srcs) + len(lands)
    afters = list(after) if isinstance(after, (list, tuple)) else [after]

    def body(*refs):
        send_s, recv_s = refs[nb], refs[nb + 1]
        for i, (s, d, dev, inc) in enumerate(plan(refs[:ns], refs[ns:nb])):
            _rcopy(s, d, send_s, recv_s, i, dev).wait_send()
            _rcopy(inc, inc, send_s, recv_s, i, dev).wait_recv()

    bufs = list(srcs) + list(lands)
    outs = pl.pallas_call(
        body, name=name, out_shape=tuple(pltpu.HBM(a.shape, a.dtype) for a in bufs),
        in_specs=[HBM_SPEC] * nb + [SEM_SPEC, SEM_SPEC] + [ANY] * len(afters), out_specs=tuple([HBM_SPEC] * nb),
        input_output_aliases={i: i for i in range(nb)},
        compiler_params=pltpu.CompilerParams(has_side_effects=SIDE_EFFECT))(*bufs, send, recv, *afters)
    return list(outs[:ns]), list(outs[ns:])


def _forward_halves(lands, *, name):
    nt = len(lands)

    def body(*refs):
        src, dst = refs[:nt], refs[nt:2 * nt]
        send, recv = refs[2 * nt:]
        x, y, c, j, sib, chips = _place()
        cps = []
        for t in range(nt):
            for k, (cx, cy) in enumerate(chips):
                cps.append(_rcopy(src[t].at[2 * cx + cy, c], dst[t].at[2 * cx + cy, c], send, recv, 3 * t + k, sib))
                cps[-1].start()
        for t in range(nt):
            for k, (cx, cy) in enumerate(chips):
                other = dst[t].at[2 * cx + cy, 1 - c]
                _rcopy(other, other, send, recv, 3 * t + k, sib).wait_recv()
        for cp in cps:
            cp.wait_send()

    return pl.pallas_call(
        body, in_specs=[ANY] * nt, out_specs=[ANY] * nt, out_shape=[jax.ShapeDtypeStruct(a.shape, a.dtype) for a in lands],
        input_output_aliases={t: t for t in range(nt)},
        scratch_shapes=[pltpu.SemaphoreType.DMA((3 * nt,)), pltpu.SemaphoreType.DMA((3 * nt,))], name=name)(*lands)


def _all_reduce_small(parts, *, name):
    n = len(parts)
    offs, rows = [], 0
    for p in parts:
        offs.append(rows)
        rows += -(-p.shape[0] // 8) * 8

    def body(*refs):
        p_refs, o_refs = refs[:n], refs[n:2 * n]
        buf, send, recv = refs[2 * n:]
        x, y, c = lax.axis_index("x"), lax.axis_index("y"), lax.axis_index("c")
        me = 4 * x + 2 * y + c
        buf[me] = jnp.zeros((rows, LANE), F32)
        for p_ref, off in zip(p_refs, offs):
            buf[me, off:off + p_ref.shape[0], :] = p_ref[...]
        cps = []
        for r in range(1, 8):
            dev = (x ^ (r >> 2), y ^ ((r >> 1) & 1), c ^ (r & 1))
            cps.append(_rcopy(buf.at[me], buf.at[me], send, recv, r - 1, dev))
            cps[-1].start()
        for r in range(1, 8):
            frm = buf.at[me ^ r]
            _rcopy(frm, frm, send, recv, r - 1, (x, y, c)).wait_recv()
        for cp in cps:
            cp.wait_send()
        acc = buf[0]
        for d in range(1, 8):
            acc = acc + buf[d]
        for o_ref, off in zip(o_refs, offs):
            o_ref[...] = acc[off:off + o_ref.shape[0], :]

    vm = pl.BlockSpec(memory_space=pltpu.VMEM)
    return pl.pallas_call(
        body, in_specs=[vm] * n, out_specs=[vm] * n, out_shape=[jax.ShapeDtypeStruct(p.shape, F32) for p in parts],
        scratch_shapes=[pltpu.VMEM((8, rows, LANE), F32), pltpu.SemaphoreType.DMA((7,)), pltpu.SemaphoreType.DMA((7,))],
        name=name)(*parts)


def _add_halves(x, b, c_idx, *, name, tr=256):
    _, _, R, C = x.shape
    tr = min(tr, R)

    def body(c_ref, x_ref, b_ref, o_ref):
        o_ref[...] = (x_ref[...].astype(F32) + b_ref[...].astype(F32)).astype(o_ref.dtype)

    return pl.pallas_call(
        body,
        grid_spec=pltpu.PrefetchScalarGridSpec(
            num_scalar_prefetch=1, grid=(N_CHIPS, R // tr),
            in_specs=[pl.BlockSpec((None, None, tr, C), lambda s, i, c_ref: (s, c_ref[0], i, 0)),
                      pl.BlockSpec((None, tr, C), lambda s, i, c_ref: (s, i, 0))],
            out_specs=pl.BlockSpec((None, tr, C), lambda s, i, c_ref: (s, i, 0))),
        out_shape=jax.ShapeDtypeStruct(b.shape, b.dtype), compiler_params=_cp(("parallel", "parallel")), name=name)(c_idx, x, b)


def _sum_slot(p, rcv, j_idx, *, name, tr=256):
    _, R, C = p.shape
    tr = min(tr, R)

    def body(j_ref, p_ref, r_ref, o_ref):
        acc = p_ref[...].astype(F32)
        for k in range(3):
            acc = acc + r_ref[k].astype(F32)
        o_ref[...] = acc

    return pl.pallas_call(
        body,
        grid_spec=pltpu.PrefetchScalarGridSpec(
            num_scalar_prefetch=1, grid=(R // tr,),
            in_specs=[pl.BlockSpec((None, tr, C), lambda i, j_ref: (j_ref[0], i, 0)),
                      pl.BlockSpec((3, tr, C), lambda i, j_ref: (0, i, 0))],
            out_specs=pl.BlockSpec((tr, C), lambda i, j_ref: (i, 0))),
        out_shape=jax.ShapeDtypeStruct((R, C), F32), compiler_params=_cp(("parallel",)), name=name)(j_idx, p, rcv)


def _adamw_math(w, g, m, v):
    nm = ADAM_B1 * m + (1.0 - ADAM_B1) * g
    nv = ADAM_B2 * v + (1.0 - ADAM_B2) * (g * g)
    m_hat = nm / (1.0 - ADAM_B1 ** ADAM_STEP)
    v_hat = nv / (1.0 - ADAM_B2 ** ADAM_STEP)
    return -ADAM_LR * (m_hat / (jnp.sqrt(v_hat) + ADAM_EPS) + ADAM_WD * w), nm, nv


def _adamw_halves(w, g_mine, g_theirs, m, v, c_idx, *, name, layer=0, into=None, tr=128):
    _, _, R, C = w.shape
    tr = tr if R % tr == 0 else R

    def body(c_ref, w_ref, gm_ref, gt_ref, m_ref, v_ref, *rest):
        g_ref, d_ref, nm_ref, nv_ref = rest[-4:]
        gv = jnp.where(pl.program_id(0) == c_ref[0], gm_ref[...], gt_ref[...])
        d, nm, nv = _adamw_math(w_ref[...], gv, m_ref[...], v_ref[...])
        g_ref[...] = gv
        d_ref[...] = d
        nm_ref[...] = nm
        nv_ref[...] = nv

    full = pl.BlockSpec((None, None, tr, C), lambda hh, i, c_ref: (layer, hh, i, 0))
    half = pl.BlockSpec((tr, C), lambda hh, i, c_ref: (i, 0))
    sh = jax.ShapeDtypeStruct(w.shape, F32)
    extra = [] if into is None else list(into)
    return pl.pallas_call(
        body,
        grid_spec=pltpu.PrefetchScalarGridSpec(num_scalar_prefetch=1, grid=(2, R // tr),
                                               in_specs=[full, half, half, full, full] + [ANY] * len(extra),
                                               out_specs=(full,) * 4),
        out_shape=(sh,) * 4, input_output_aliases={6 + t: t for t in range(len(extra))},
        compiler_params=_cp(("parallel", "parallel")), name=name)(c_idx, w, g_mine, g_theirs, m, v, *extra)


def _adamw_parts(ws, gs, ms, vs, *, name):
    n = len(ws)

    def body(*refs):
        ins, outs = refs[:4 * n], refs[4 * n:]
        for t in range(n):
            d, nm, nv = _adamw_math(ins[t][...], ins[n + t][...], ins[2 * n + t][...], ins[3 * n + t][...])
            outs[t][...] = d
            outs[n + t][...] = nm
            outs[2 * n + t][...] = nv

    vm = pl.BlockSpec(memory_space=pltpu.VMEM)
    shapes = [jax.ShapeDtypeStruct(w.shape, F32) for w in ws] * 3
    outs = pl.pallas_call(body, in_specs=[vm] * (4 * n), out_specs=[vm] * (3 * n), out_shape=shapes, name=name)(
        *ws, *gs, *ms, *vs)
    return outs[:n], outs[n:2 * n], outs[2 * n:]


_SMALL = ["norm_g", "mem_norm_g", "xa_q_norm_g", "xa_k_norm_g", "dn_a_log", "dn_dt_bias", "dn_out_norm_g",
          "sb_q_norm_g", "sb_k_norm_g"]


def _rows128(a):
    flat = a.reshape(-1)
    pad = -flat.shape[0] % LANE
    if pad:
        flat = jnp.pad(flat, (0, pad))
    return flat.reshape(-1, LANE)


def _unrows(r, shape):
    return r.reshape(-1)[:math.prod(shape)].reshape(shape)


def kernel(x, mem, norm_g, mem_norm_g, mem_w_kv, xa_q_norm_g, xa_k_norm_g, w_out, dn_w_in, dn_conv_w, dn_a_log, dn_dt_bias, dn_out_norm_g, sb_w_in, sb_q_norm_g, sb_k_norm_g, loss_target, m_norm_g, m_mem_norm_g, m_mem_w_kv, m_xa_q_norm_g, m_xa_k_norm_g, m_w_out, m_dn_w_in, m_dn_conv_w, m_dn_a_log, m_dn_dt_bias, m_dn_out_norm_g, m_sb_w_in, m_sb_q_norm_g, m_sb_k_norm_g, v_norm_g, v_mem_norm_g, v_mem_w_kv, v_xa_q_norm_g, v_xa_k_norm_g, v_w_out, v_dn_w_in, v_dn_conv_w, v_dn_a_log, v_dn_dt_bias, v_dn_out_norm_g, v_sb_w_in, v_sb_q_norm_g, v_sb_k_norm_g):
    W = dict(norm_g=norm_g, mem_norm_g=mem_norm_g, mem_w_kv=mem_w_kv, xa_q_norm_g=xa_q_norm_g, xa_k_norm_g=xa_k_norm_g,
             w_out=w_out, dn_w_in=dn_w_in, dn_conv_w=dn_conv_w, dn_a_log=dn_a_log, dn_dt_bias=dn_dt_bias,
             dn_out_norm_g=dn_out_norm_g, sb_w_in=sb_w_in, sb_q_norm_g=sb_q_norm_g, sb_k_norm_g=sb_k_norm_g)
    M = dict(norm_g=m_norm_g, mem_norm_g=m_mem_norm_g, mem_w_kv=m_mem_w_kv, xa_q_norm_g=m_xa_q_norm_g,
             xa_k_norm_g=m_xa_k_norm_g, w_out=m_w_out, dn_w_in=m_dn_w_in, dn_conv_w=m_dn_conv_w, dn_a_log=m_dn_a_log,
             dn_dt_bias=m_dn_dt_bias, dn_out_norm_g=m_dn_out_norm_g, sb_w_in=m_sb_w_in, sb_q_norm_g=m_sb_q_norm_g,
             sb_k_norm_g=m_sb_k_norm_g)
    V = dict(norm_g=v_norm_g, mem_norm_g=v_mem_norm_g, mem_w_kv=v_mem_w_kv, xa_q_norm_g=v_xa_q_norm_g,
             xa_k_norm_g=v_xa_k_norm_g, w_out=v_w_out, dn_w_in=v_dn_w_in, dn_conv_w=v_dn_conv_w, dn_a_log=v_dn_a_log,
             dn_dt_bias=v_dn_dt_bias, dn_out_norm_g=v_dn_out_norm_g, sb_w_in=v_sb_w_in, sb_q_norm_g=v_sb_q_norm_g,
             sb_k_norm_g=v_sb_k_norm_g)
    names = ["norm_g", "mem_norm_g", "mem_w_kv", "xa_q_norm_g", "xa_k_norm_g", "w_out", "dn_w_in", "dn_conv_w",
             "dn_a_log", "dn_dt_bias", "dn_out_norm_g", "sb_w_in", "sb_q_norm_g", "sb_k_norm_g"]
    cx, cy, cc = lax.axis_index("x"), lax.axis_index("y"), lax.axis_index("c")
    slot = 2 * cx + cy
    half_r = D_MODEL // 2
    conv_cols = dn_conv_w.shape[2]

    by_slot = lambda fn, a: lax.switch(slot, [lambda v, s=s: fn(s, v) for s in range(N_CHIPS)], a)
    w0s = by_slot(_pack_own_shard, dn_w_in[0].astype(BF16)).reshape(2, half_r, WIN_COLS)
    convs = jnp.pad(dn_conv_w[0], ((0, 8 - DN_CONV), (0, 0))).reshape(8, 2, conv_cols // 2).transpose(1, 0, 2)
    c_idx = jnp.reshape(cc, (1,)).astype(jnp.int32)
    j_idx = jnp.reshape(slot, (1,)).astype(jnp.int32)
    own_a = [w0s, convs]
    lands_a = [lax.dynamic_update_slice(lax.empty((N_CHIPS,) + o.shape, o.dtype), o[None], (slot, 0, 0, 0)) for o in own_a]
    send_a, recv_a, own_a, lands_a, token_a = _exchange_start(own_a, lands_a, _gather_plan, name="gather_start")
    zero_a = token_a[0, 0]
    M["dn_w_in"] = m_dn_w_in + zero_a
    V["dn_w_in"] = v_dn_w_in + zero_a
    own_b = [(sb_w_in[0] + zero_a).astype(BF16).reshape(2, half_r, SB_PROJ // N_CHIPS), (w_out + zero_a).astype(BF16),
             (mem_w_kv + zero_a).astype(BF16)]
    view0 = (1, 2, half_r, P0_SHARD)
    _, lands_a = _exchange_wait(own_a, lands_a, send_a, recv_a,
                                [M["dn_w_in"].reshape(view0), V["dn_w_in"].reshape(view0)] + own_b,
                                _gather_plan, name="gather_wait")
    (g0, gconv), own_b = lax.optimization_barrier((_forward_halves(lands_a, name="gather_forward"), own_b))
    def land(o):
        return lax.dynamic_update_slice(lax.empty((N_CHIPS,) + o.shape, o.dtype), o[None], (slot, 0, 0, 0))

    late = {}
    late[1] = _exchange_start(own_b[1:], [land(o) for o in own_b[1:]], _gather_plan, name="gather_late1_start")
    w1s, _ = lax.optimization_barrier((own_b[0], late[1][4]))
    late[2] = _exchange_start([w1s], [land(w1s)], _gather_plan, name="gather_late2_start")
    token_b = late[2][4]

    def late_weights(after):
        got = []
        for i in (1, 2):
            send, recv, srcs, lands, _ = late[i]
            _, lands = _exchange_wait(srcs, lands, send, recv, after, _gather_plan, name=f"gather_late{i}_wait")
            got += _forward_halves(lands, name=f"gather_late{i}_forward")
            after = got[-1]
        gout, gkv, g1 = got
        return gkv, gout, g1

    rs = {}

    def scatter_start(tag, xs, from_sib=None):
        if from_sib is None:
            from_sib = _swap_halves(xs, name=f"rs{tag}_swap")
        ps = [_add_halves(a, b, c_idx, name=f"rs{tag}_add{t}") for t, (a, b) in enumerate(zip(xs, from_sib))]
        rcv = [lax.empty((3,) + p.shape[1:], p.dtype) for p in ps]
        send, recv, ps, rcv, token = _exchange_start(ps, rcv, _scatter_plan, name=f"rs{tag}_scatter_start")
        rs[tag] = (ps, rcv, send, recv)
        return token

    def scatter_finish(tag, after):
        ps, rcv, send, recv = rs[tag]
        ps, rcv = _exchange_wait(ps, rcv, send, recv, after, _scatter_plan, name=f"rs{tag}_scatter_wait")
        return [_sum_slot(p, r, j_idx, name=f"rs{tag}_sum{t}") for t, (p, r) in enumerate(zip(ps, rcv))]

    def early_grads(layer, d_win, d_wout, d_wkv):
        if layer == 0:
            d_win = jnp.stack([_padded_to_window(s, d_win) for s in range(N_CHIPS)]).reshape(N_CHIPS, 2, half_r, WIN_COLS)
        xs = [d_win, d_wout.reshape(N_CHIPS, 2, -1, D_MODEL), d_wkv.reshape(N_CHIPS, 2, -1, 2 * XA_WIDTH)]
        if layer == 0:
            return scatter_start(0, xs)
        lands = [lax.empty((N_CHIPS,) + a.shape[2:], a.dtype) for a in xs]
        send, recv, xs, lands, token = _exchange_start(xs, lands, _swap_plan, per_tensor=N_CHIPS, name="rs1_swap_start")
        rs["swap1"] = (xs, lands, send, recv)
        return token

    def grads_swapped(after):
        xs, lands, send, recv = rs["swap1"]
        xs, from_sib = _exchange_wait(xs, lands, send, recv, after, _swap_plan, name="rs1_swap_wait")
        return scatter_start(1, xs, from_sib)

    w_in0 = _windows_to_padded(g0.reshape(N_CHIPS, D_MODEL, WIN_COLS))
    conv_f = gconv.transpose(2, 0, 1, 3).reshape(8, N_CHIPS * conv_cols)[:DN_CONV]

    loss_vec, grad_x, g = _local_step(
        x[0], mem[0], loss_target[0], norm_g + token_b[0, 0], mem_norm_g, xa_q_norm_g, xa_k_norm_g, w_in0, conv_f,
        dn_a_log[0], dn_dt_bias[0], dn_out_norm_g[0], sb_q_norm_g[0], sb_k_norm_g[0], late_weights, early_grads,
        grads_swapped)

    mine1 = scatter_finish(1, grad_x)
    theirs1 = _swap_with_sibling(mine1, name="rs1_join")
    big1 = [("sb_w_in", None), ("w_out", 1), ("mem_w_kv", 1)]
    big0 = [("dn_w_in", None), ("w_out", 0), ("mem_w_kv", 0)]

    out_g, out_d, out_m, out_v = {}, {}, {}, {}
    partial = {}

    def adamw_big(big, mine, theirs):
        for (n, layer), mine_g, their_g in zip(big, mine, theirs):
            layers = 1 if layer is None else 2
            view = (layers, 2) + mine_g.shape
            partial[n] = _adamw_halves(W[n].reshape(view), mine_g, their_g, M[n].reshape(view), V[n].reshape(view), c_idx,
                                       layer=layer or 0, into=partial.get(n), name=f"adamw_{n}" + ("" if layer is None else str(layer)))
        return [partial[n][0] for n, _ in big]

    done1 = lax.optimization_barrier(tuple(adamw_big(big1, mine1, theirs1)))[-1]
    mine0 = scatter_finish(0, done1)
    mine0[0] = by_slot(_unpack_own_shard, mine0[0])

    parts, _ = lax.optimization_barrier(([_rows128(g[n]) for n in _SMALL] + [_rows128(g["dn_conv_w"]), loss_vec], mine0[0]))
    red = _all_reduce_small(parts, name="all_reduce_small")
    small_rows = dict(zip(_SMALL, red))
    conv_full = red[len(_SMALL)].reshape(DN_CONV, N_CHIPS * conv_cols)
    small_rows["dn_conv_w"] = _rows128(lax.dynamic_slice_in_dim(conv_full, slot * conv_cols, conv_cols, axis=1))
    loss = red[-1][0, 0]

    adamw_big(big0, mine0, _swap_with_sibling(mine0, name="rs0_join"))
    for n, outs in partial.items():
        out_g[n], out_d[n], out_m[n], out_v[n] = [o.reshape(W[n].shape) for o in outs]
    small_names = _SMALL + ["dn_conv_w"]
    ds, nms, nvs = _adamw_parts([_rows128(W[n]) for n in small_names], [small_rows[n] for n in small_names],
                                [_rows128(M[n]) for n in small_names], [_rows128(V[n]) for n in small_names], name="adamw_small")
    for n, d, nm, nv in zip(small_names, ds, nms, nvs):
        shp = W[n].shape
        out_g[n], out_d[n], out_m[n], out_v[n] = [_unrows(r, shp) for r in (small_rows[n], d, nm, nv)]

    return (loss, grad_x[None], *[out_g[n] for n in names], *[out_d[n] for n in names], *[out_m[n] for n in names],
            *[out_v[n] for n in names])
```

```python
import math

import jax
import jax.numpy as jnp
from jax import lax
from jax.experimental import pallas as pl
from jax.experimental.pallas import tpu as pltpu

F32 = jnp.float32
BF16 = jnp.bfloat16
HI = lax.Precision.HIGHEST
MESH = pl.DeviceIdType.MESH

D_MODEL = 2048
INNER = 4096
XA_WIDTH = 1024
XA_HEADS = 4
XA_DIM = 256
MIX_WIDTH = 3072
HEAD_DIM = 128
DN_V_HEADS = 24
DN_QK_WIDTH = 1536
DN_CONV = 4
DN_PROJ = 11312
SB_PROJ = 14336
EPS = 1e-6
N_CHIPS = 4

CH = 128
LANE = 128

P0_XQ = 6144
P0_Z = 7168
P0_AB = 11264
P0 = 11776
P0_SHARD = DN_PROJ // N_CHIPS
P1_XQ = 9216
P1_Z = 10240
P1 = SB_PROJ

ADAM_LR = 0.001
ADAM_B1 = 0.9
ADAM_B2 = 0.999
ADAM_EPS = 1e-08
ADAM_WD = 0.01
ADAM_STEP = 10

VMEM_LIMIT = 48 * 1024 * 1024


def _cp(sem=None, **kw):
    return pltpu.CompilerParams(dimension_semantics=sem, vmem_limit_bytes=VMEM_LIMIT, **kw)


def _bdot(a, b, dims):
    return lax.dot_general(a.astype(BF16), b.astype(BF16), (dims, ((), ())), preferred_element_type=F32)


def _fdot(a, b, dims):
    return lax.dot_general(a, b, (dims, ((), ())), precision=HI, preferred_element_type=F32)


NN = ((1,), (0,))
NT = ((1,), (1,))
TN = ((0,), (0,))


def _sigmoid(x):
    return 1.0 / (1.0 + jnp.exp(-x))


def _softplus(x):
    return jnp.maximum(x, 0.0) + jnp.log(1.0 + jnp.exp(-jnp.abs(x)))


def _iota2(shape, axis):
    return lax.broadcasted_iota(jnp.int32, shape, axis)


MM_FULL_K = 4096
MM_BLOCK_BYTES = 4 * 1024 * 1024


def _matmul(a, b, *, ta=False, tb=False, out_dtype=F32, res=None, name, n=None, tm=None, tn=None, tk=None,
            b_spec=None, o_spec=None, o_shape=None):
    a_segs = list(a) if isinstance(a, (list, tuple)) else [a]
    b_segs = list(b) if isinstance(b, (list, tuple)) else [b]
    a0, b0 = a_segs[0], b_segs[0]
    M = a0.shape[1] if ta else a0.shape[0]
    K = a0.shape[0] if ta else sum(s.shape[1] for s in a_segs)
    if n is None:
        n = b0.shape[0] if tb else sum(s.shape[1] for s in b_segs)
    N = n
    dims = ((0,) if ta else (1,), (1,) if tb else (0,))
    has_res = res is not None
    flat = lambda v: v.reshape(-1, v.shape[-1])
    o_shape = o_shape or jax.ShapeDtypeStruct((M, N), out_dtype)

    def seg_specs(segs, tile, block, pos):
        specs, ranges, off = [], [], 0
        for s in segs:
            cnt = s.shape[1] // tile
            assert s.shape[1] % tile == 0, (name, s.shape, tile)

            def imap(*g, off=off, cnt=cnt):
                t = jnp.clip(g[pos] - off, 0, cnt - 1)
                return (g[0], t) if pos == 2 else (0, t)

            specs.append(pl.BlockSpec(block, imap))
            ranges.append((off, off + cnt))
            off += cnt
        return specs, ranges

    if K <= MM_FULL_K:
        assert len(a_segs) == 1
        tm = tm or min(M, 1024, max(256, MM_BLOCK_BYTES // (K * a0.dtype.itemsize)))
        tn = tn or min(N, 512)
        assert M % tm == 0 and N % tn == 0, (name, M, N, K, tm, tn)
        nb = len(b_segs)
        if b_spec is not None:
            b_specs, b_ranges = [b_spec], [(0, N // tn)]
        elif nb > 1:
            assert not tb
            b_specs, b_ranges = seg_specs(b_segs, tn, (K, tn), 1)
        else:
            b_specs = [pl.BlockSpec((tn, K), lambda i, j: (j, 0)) if tb else pl.BlockSpec((K, tn), lambda i, j: (0, j))]
            b_ranges = [(0, N // tn)]

        def body_full(*refs):
            a_ref, b_refs = refs[0], refs[1:1 + nb]
            r_ref = refs[1 + nb] if has_res else None
            o_ref = refs[-1]
            j = pl.program_id(1)
            for b_ref, (lo, hi) in zip(b_refs, b_ranges):
                def emit(b_ref=b_ref):
                    r = _bdot(a_ref[...], flat(b_ref[...]), dims)
                    if has_res:
                        r = r + r_ref[...]
                    o_ref[...] = r.astype(o_ref.dtype).reshape(o_ref.shape)
                if nb == 1:
                    emit()
                else:
                    pl.when(jnp.logical_and(j >= lo, j < hi))(emit)

        a_spec = pl.BlockSpec((K, tm), lambda i, j: (0, i)) if ta else pl.BlockSpec((tm, K), lambda i, j: (i, 0))
        o_spec = o_spec or pl.BlockSpec((tm, tn), lambda i, j: (i, j))
        r_spec = [pl.BlockSpec((tm, tn), lambda i, j: (i, j))] if has_res else []
        return pl.pallas_call(
            body_full, grid=(M // tm, N // tn), in_specs=[a_spec] + b_specs + r_spec, out_specs=o_spec, out_shape=o_shape,
            compiler_params=_cp(("parallel", "arbitrary")), name=name)(*([a0] + b_segs + ([res] if has_res else [])))

    assert tb and not ta and len(b_segs) == 1
    tm, tn = tm or min(M, 1024), tn or min(N, 1024)
    tk = tk or (1024 if all(s.shape[1] % 1024 == 0 for s in a_segs) else 512)
    assert M % tm == 0 and N % tn == 0 and K % tk == 0, (name, M, N, K, tm, tn, tk)
    nk = K // tk
    na = len(a_segs)
    if na > 1:
        a_specs, a_ranges = seg_specs(a_segs, tk, (tm, tk), 2)
    else:
        a_specs, a_ranges = [pl.BlockSpec((tm, tk), lambda i, j, k: (i, k))], [(0, nk)]
    b_spec = b_spec or pl.BlockSpec((tn, tk), lambda i, j, k: (j, k))

    def body(*refs):
        a_refs, b_ref = refs[:na], refs[na]
        r_ref = refs[na + 1] if has_res else None
        o_ref, acc = refs[-2], refs[-1]
        k = pl.program_id(2)

        @pl.when(k == 0)
        def _():
            acc[...] = jnp.zeros_like(acc)

        for a_ref, (lo, hi) in zip(a_refs, a_ranges):
            def emit(a_ref=a_ref):
                acc[...] += _bdot(a_ref[...], flat(b_ref[...]), dims)
            if na == 1:
                emit()
            else:
                pl.when(jnp.logical_and(k >= lo, k < hi))(emit)

        @pl.when(k == nk - 1)
        def _():
            r = acc[...]
            if has_res:
                r = r + r_ref[...]
            o_ref[...] = r.astype(o_ref.dtype).reshape(o_ref.shape)

    o_spec = o_spec or pl.BlockSpec((tm, tn), lambda i, j, k: (i, j))
    r_spec = [pl.BlockSpec((tm, tn), lambda i, j, k: (i, j))] if has_res else []
    return pl.pallas_call(
        body, grid=(M // tm, N // tn, nk), in_specs=a_specs + [b_spec] + r_spec, out_specs=o_spec, out_shape=o_shape,
        scratch_shapes=[pltpu.VMEM((tm, tn), F32)],
        compiler_params=_cp(("parallel", "parallel", "arbitrary")), name=name)(*(a_segs + [b0] + ([res] if has_res else [])))


def _rmsnorm_fwd(x, g, *, name, tm=256):
    S, Dm = x.shape
    tm = min(tm, S)

    def body(x_ref, g_ref, o_ref):
        xv = x_ref[...]
        r = lax.rsqrt(jnp.mean(xv * xv, axis=-1, keepdims=True) + EPS)
        o_ref[...] = (xv * r * g_ref[...]).astype(BF16)

    return pl.pallas_call(
        body, grid=(S // tm,), in_specs=[pl.BlockSpec((tm, Dm), lambda i: (i, 0)), pl.BlockSpec((1, Dm), lambda i: (0, 0))],
        out_specs=pl.BlockSpec((tm, Dm), lambda i: (i, 0)), out_shape=jax.ShapeDtypeStruct((S, Dm), BF16),
        compiler_params=_cp(("parallel",)), name=name)(x, g.reshape(1, Dm))


def _rmsnorm_bwd(dh, x, g, dres, *, name, tm=256):
    S, Dm = x.shape
    tm = min(tm, S)
    want_dx = dres is not None

    def body(*refs):
        if want_dx:
            dh_ref, x_ref, g_ref, dr_ref, dx_ref, dg_ref = refs
        else:
            dh_ref, x_ref, g_ref, dg_ref = refs
        i = pl.program_id(0)
        xv = x_ref[...]
        dhv = dh_ref[...]
        r = lax.rsqrt(jnp.mean(xv * xv, axis=-1, keepdims=True) + EPS)
        y = xv * r
        part = jnp.sum(dhv * y, axis=0, keepdims=True)

        @pl.when(i == 0)
        def _():
            dg_ref[...] = jnp.zeros_like(dg_ref)

        dg_ref[...] += part
        if want_dx:
            dy = dhv * g_ref[...]
            dx_ref[...] = dr_ref[...] + r * (dy - y * jnp.mean(dy * y, axis=-1, keepdims=True))

    row = pl.BlockSpec((tm, Dm), lambda i: (i, 0))
    vec = pl.BlockSpec((1, Dm), lambda i: (0, 0))
    if want_dx:
        dx, dg = pl.pallas_call(
            body, grid=(S // tm,), in_specs=[row, row, vec, row], out_specs=(row, vec),
            out_shape=(jax.ShapeDtypeStruct((S, Dm), F32), jax.ShapeDtypeStruct((1, Dm), F32)),
            compiler_params=_cp(("arbitrary",)), name=name)(dh, x, g.reshape(1, Dm), dres)
        return dx, dg
    dg = pl.pallas_call(
        body, grid=(S // tm,), in_specs=[row, row, vec], out_specs=vec,
        out_shape=jax.ShapeDtypeStruct((1, Dm), F32), compiler_params=_cp(("arbitrary",)), name=name)(dh, x, g.reshape(1, Dm))
    return None, dg


GATE_TN = XA_WIDTH
GATE_MIX_TILES = MIX_WIDTH // GATE_TN


def _gate_cat_specs(tm):
    return [pl.BlockSpec((tm, GATE_TN), lambda i, j: (i, jnp.minimum(j, GATE_MIX_TILES - 1))),
            pl.BlockSpec((tm, GATE_TN), lambda i, j: (i, 0))]


def _gate_fwd(mix, xa, proj, z_off, *, name, tm=256):
    S = mix.shape[0]
    tm = min(tm, S)
    zb = z_off // GATE_TN

    def body(m_ref, x_ref, z_ref, y_ref):
        z = z_ref[...]
        c = jnp.where(pl.program_id(1) < GATE_MIX_TILES, m_ref[...], x_ref[...])
        y_ref[...] = (c * z * _sigmoid(z)).astype(BF16)

    blk = pl.BlockSpec((tm, GATE_TN), lambda i, j: (i, j))
    return pl.pallas_call(
        body, grid=(S // tm, INNER // GATE_TN),
        in_specs=_gate_cat_specs(tm) + [pl.BlockSpec((tm, GATE_TN), lambda i, j: (i, zb + j))],
        out_specs=blk, out_shape=jax.ShapeDtypeStruct((S, INNER), BF16),
        compiler_params=_cp(("parallel", "arbitrary")), name=name)(mix, xa, proj)


def _gate_bwd(dy, mix, xa, proj, z_off, *, name, tm=256):
    S = mix.shape[0]
    tm = min(tm, S)
    zb = z_off // GATE_TN

    def body(dy_ref, m_ref, x_ref, z_ref, dc_ref, dz_ref):
        z = z_ref[...]
        sg = _sigmoid(z)
        d = dy_ref[...]
        c = jnp.where(pl.program_id(1) < GATE_MIX_TILES, m_ref[...], x_ref[...])
        dc_ref[...] = d * z * sg
        dz_ref[...] = (d * c * sg * (1.0 + z * (1.0 - sg))).astype(BF16)

    blk = pl.BlockSpec((tm, GATE_TN), lambda i, j: (i, j))
    return pl.pallas_call(
        body, grid=(S // tm, INNER // GATE_TN),
        in_specs=[blk] + _gate_cat_specs(tm) + [pl.BlockSpec((tm, GATE_TN), lambda i, j: (i, zb + j))], out_specs=(blk, blk),
        out_shape=(jax.ShapeDtypeStruct((S, INNER), F32), jax.ShapeDtypeStruct((S, INNER), BF16)),
        compiler_params=_cp(("parallel", "arbitrary")), name=name)(dy, mix, xa, proj)


def _loss_head(x, target, *, name, tm=256):
    S, Dm = x.shape
    tm = min(tm, S)

    nt = S // tm

    def body(x_ref, t_ref, dx_ref, l_ref, acc):
        i = pl.program_id(0)
        e = x_ref[...] - t_ref[...]
        dx_ref[...] = e * (1.0 / Dm)

        @pl.when(i == 0)
        def _():
            acc[...] = jnp.zeros_like(acc)

        acc[...] += jnp.sum(e * e, axis=0, keepdims=True) * (0.5 / Dm)

        @pl.when(i == nt - 1)
        def _():
            l_ref[...] = jnp.sum(acc[...], axis=1, keepdims=True) + jnp.zeros((1, LANE), F32)

    row = pl.BlockSpec((tm, Dm), lambda i: (i, 0))
    return pl.pallas_call(
        body, grid=(nt,), in_specs=[row, row], out_specs=(row, pl.BlockSpec((1, LANE), lambda i: (0, 0))),
        out_shape=(jax.ShapeDtypeStruct((S, Dm), F32), jax.ShapeDtypeStruct((1, LANE), F32)),
        scratch_shapes=[pltpu.VMEM((1, Dm), F32)],
        compiler_params=_cp(("arbitrary",)), name=name)(x, target)


def _xa_norm(v, g):
    r = lax.rsqrt(jnp.mean(v * v, axis=-1, keepdims=True) + EPS)
    return v * r, r


def _xa_fwd(proj, xq_off, kv, gq, gk, *, name, tm=512):
    S = proj.shape[0]
    tm = min(tm, S)
    qb = xq_off // XA_DIM
    n_mem = kv.shape[0]
    scale = XA_DIM ** -0.5

    def body(q_ref, k_ref, v_ref, gq_ref, gk_ref, o_ref):
        qh, _ = _xa_norm(q_ref[...], None)
        kh, _ = _xa_norm(k_ref[...], None)
        qn = qh * gq_ref[...]
        kn = kh * gk_ref[...]
        s = _bdot(qn, kn, NT) * scale
        s = s - jnp.max(s, axis=-1, keepdims=True)
        p = jnp.exp(s)
        p = p / jnp.sum(p, axis=-1, keepdims=True)
        o_ref[...] = _bdot(p, v_ref[...], NN)

    vec = pl.BlockSpec((1, XA_DIM), lambda h, i: (0, 0))
    return pl.pallas_call(
        body, grid=(XA_HEADS, S // tm),
        in_specs=[pl.BlockSpec((tm, XA_DIM), lambda h, i: (i, qb + h)),
                  pl.BlockSpec((n_mem, XA_DIM), lambda h, i: (0, h)),
                  pl.BlockSpec((n_mem, XA_DIM), lambda h, i: (0, XA_HEADS + h)), vec, vec],
        out_specs=pl.BlockSpec((tm, XA_DIM), lambda h, i: (i, h)),
        out_shape=jax.ShapeDtypeStruct((S, XA_WIDTH), F32),
        compiler_params=_cp(("parallel", "parallel")), name=name)(proj, kv, kv, gq.reshape(1, XA_DIM), gk.reshape(1, XA_DIM))


def _xa_bwd(dcat, proj, xq_off, kv, gq, gk, *, name, tm=512):
    S = proj.shape[0]
    tm = min(tm, S)
    nt = S // tm
    qb = xq_off // XA_DIM
    db = MIX_WIDTH // XA_DIM
    n_mem = kv.shape[0]
    scale = XA_DIM ** -0.5

    def body(d_ref, q_ref, k_ref, v_ref, gq_ref, gk_ref, dq_ref, dk_ref, dv_ref, dgq_ref, dgk_ref, dkn_acc):
        h = pl.program_id(0)
        i = pl.program_id(1)
        q = q_ref[...]
        k = k_ref[...]
        qh, rq = _xa_norm(q, None)
        kh, rk = _xa_norm(k, None)
        gqv = gq_ref[...]
        gkv = gk_ref[...]
        qn = qh * gqv
        kn = kh * gkv
        s = _bdot(qn, kn, NT) * scale
        s = s - jnp.max(s, axis=-1, keepdims=True)
        p = jnp.exp(s)
        p = p / jnp.sum(p, axis=-1, keepdims=True)
        d = d_ref[...]
        dp = _bdot(d, v_ref[...], NT)
        ds = p * (dp - jnp.sum(dp * p, axis=-1, keepdims=True)) * scale
        dqn = _bdot(ds, kn, NN)

        @pl.when(i == 0)
        def _():
            dkn_acc[...] = jnp.zeros_like(dkn_acc)
            dv_ref[...] = jnp.zeros_like(dv_ref)

        @pl.when(jnp.logical_and(i == 0, h == 0))
        def _():
            dgq_ref[...] = jnp.zeros_like(dgq_ref)
            dgk_ref[...] = jnp.zeros_like(dgk_ref)

        dkn_acc[...] += _bdot(ds, qn, TN)
        dv_ref[...] += _bdot(p, d, TN)
        dgq_ref[...] += jnp.sum(dqn * qh, axis=0, keepdims=True)
        dy = dqn * gqv
        dq_ref[...] = (rq * (dy - qh * jnp.mean(dy * qh, axis=-1, keepdims=True))).astype(BF16)

        @pl.when(i == nt - 1)
        def _():
            dkn = dkn_acc[...]
            dgk_ref[...] += jnp.sum(dkn * kh, axis=0, keepdims=True)
            dyk = dkn * gkv
            dk_ref[...] = rk * (dyk - kh * jnp.mean(dyk * kh, axis=-1, keepdims=True))

    vec = pl.BlockSpec((1, XA_DIM), lambda h, i: (0, 0))
    kblk = pl.BlockSpec((n_mem, XA_DIM), lambda h, i: (0, h))
    vblk = pl.BlockSpec((n_mem, XA_DIM), lambda h, i: (0, XA_HEADS + h))
    dq, dk, dv, dgq, dgk = pl.pallas_call(
        body, grid=(XA_HEADS, nt),
        in_specs=[pl.BlockSpec((tm, XA_DIM), lambda h, i: (i, db + h)),
                  pl.BlockSpec((tm, XA_DIM), lambda h, i: (i, qb + h)), kblk, vblk, vec, vec],
        out_specs=(pl.BlockSpec((tm, XA_DIM), lambda h, i: (i, h)), kblk, kblk, vec, vec),
        out_shape=(jax.ShapeDtypeStruct((S, XA_WIDTH), BF16), jax.ShapeDtypeStruct((n_mem, XA_WIDTH), F32),
                   jax.ShapeDtypeStruct((n_mem, XA_WIDTH), F32), jax.ShapeDtypeStruct((1, XA_DIM), F32),
                   jax.ShapeDtypeStruct((1, XA_DIM), F32)),
        scratch_shapes=[pltpu.VMEM((n_mem, XA_DIM), F32)],
        compiler_params=_cp(("arbitrary", "arbitrary")), name=name)(
            dcat, proj, kv, kv, gq.reshape(1, XA_DIM), gk.reshape(1, XA_DIM))
    return dq, jnp.concatenate([dk, dv], axis=1), dgq, dgk


SB_TQ = 256
SB_TK = 256
SB_HEADS = 24


SB_PAIR = 2
SB_PW = SB_PAIR * HEAD_DIM


def _hdot(a, b, dims, dot=None):
    dot = dot or _bdot
    n = a.shape[0] if a.ndim == 3 else b.shape[0]
    return jnp.stack([dot(a[i] if a.ndim == 3 else a, b[i] if b.ndim == 3 else b, dims) for i in range(n)])


def _sb_tile(qi, kj, t0, s0, masked):
    z = _hdot(qi, kj, NT)
    sp = _softplus(z)
    ls = z - sp
    if not masked:
        return -sp, ls, None
    mask = (s0 + _iota2(z.shape[1:], 1)) < (t0 + _iota2(z.shape[1:], 0))
    return jnp.where(mask, -sp, 0.0), ls, mask


def _dot2(x, tri):
    hi = x.astype(BF16)
    lo = (x - hi.astype(F32)).astype(BF16)
    plain = lambda u, v, dims: lax.dot_general(u, v, (dims, ((), ())), preferred_element_type=F32)
    return _hdot(hi, tri, NN, plain) + _hdot(lo, tri, NN, plain)


def _sb_heads(ref, rows=slice(None)):
    return jnp.stack([ref[rows, hh * HEAD_DIM:(hh + 1) * HEAD_DIM] for hh in range(SB_PAIR)])


def _sb_fwd(proj, gq, gk, *, name):
    S = proj.shape[0]
    tq, tk = min(SB_TQ, S), min(SB_TK, S)
    nq = S // tq
    scale = HEAD_DIM ** -0.5

    def body(q_ref, k_ref, v_ref, gq_ref, gk_ref, o_ref, tot_ref, qn_s, kn_s, v_s):
        q = _sb_heads(q_ref)
        k = _sb_heads(k_ref)
        qn_s[...] = (q * lax.rsqrt(jnp.mean(q * q, axis=-1, keepdims=True) + EPS) * (gq_ref[...] * scale)).astype(BF16)
        kn_s[...] = (k * lax.rsqrt(jnp.mean(k * k, axis=-1, keepdims=True) + EPS) * gk_ref[...]).astype(BF16)
        v_s[...] = _sb_heads(v_ref).astype(BF16)
        after = (_iota2((tk, tk), 0) > _iota2((tk, tk), 1)).astype(BF16)

        def qblock(i, _):
            rows = pl.ds(pl.multiple_of(i * tq, tq), tq)
            qi = qn_s[:, rows, :]
            jd = (i * tq) // tk

            def tile(j, acc, run, masked):
                cols = pl.ds(pl.multiple_of(j * tk, tk), tk)
                lr, ls, mask = _sb_tile(qi, kn_s[:, cols, :], i * tq, j * tk, masked)
                later = _dot2(lr, after) + run
                a = jnp.exp(ls + later)
                if masked:
                    a = jnp.where(mask, a, 0.0)
                acc = acc + _hdot(a, v_s[:, cols, :], NN)
                return acc, run + jnp.sum(lr, axis=-1, keepdims=True)

            acc, run = tile(jd, jnp.zeros((SB_PAIR, tq, HEAD_DIM), F32), jnp.zeros((SB_PAIR, tq, 1), F32), True)
            acc, run = lax.fori_loop(0, jd, lambda jj, c: tile(jd - 1 - jj, c[0], c[1], False), (acc, run))
            tot = run + jnp.zeros((SB_PAIR, tq, HEAD_DIM), F32)
            for hh in range(SB_PAIR):
                o_ref[rows, hh * HEAD_DIM:(hh + 1) * HEAD_DIM] = acc[hh]
                tot_ref[rows, hh * HEAD_DIM:(hh + 1) * HEAD_DIM] = tot[hh]
            return 0

        lax.fori_loop(0, nq, qblock, 0)

    npair = SB_HEADS // SB_PAIR
    vec = pl.BlockSpec((1, HEAD_DIM), lambda h: (0, 0))
    hb = lambda off: pl.BlockSpec((S, SB_PW), lambda h: (0, off + h), pipeline_mode=pl.Buffered(1))
    return pl.pallas_call(
        body, grid=(npair,), in_specs=[hb(0), hb(npair), hb(2 * npair), vec, vec],
        out_specs=(hb(0), hb(0)), out_shape=(jax.ShapeDtypeStruct((S, MIX_WIDTH), F32),) * 2,
        scratch_shapes=[pltpu.VMEM((SB_PAIR, S, HEAD_DIM), BF16)] * 3,
        compiler_params=_cp(("parallel",)), name=name)(proj, proj, proj, gq.reshape(1, HEAD_DIM), gk.reshape(1, HEAD_DIM))


def _sb_bwd(dmix, tot, proj, gq, gk, *, name):
    S = proj.shape[0]
    tq, tk = min(SB_TQ, S), min(SB_TK, S)
    nq = S // tq
    scale = HEAD_DIM ** -0.5

    def body(do_ref, o_ref, q_ref, k_ref, v_ref, gq_ref, gk_ref, dq_ref, dk_ref, dv_ref, dgq_ref, dgk_ref,
             qn_s, kn_s, v_s, dkn_s, dqn_s, dv_s):
        h = pl.program_id(0)
        q = _sb_heads(q_ref)
        k = _sb_heads(k_ref)
        rq = lax.rsqrt(jnp.mean(q * q, axis=-1, keepdims=True) + EPS)
        rk = lax.rsqrt(jnp.mean(k * k, axis=-1, keepdims=True) + EPS)
        gqv = gq_ref[...]
        gkv = gk_ref[...]
        qn_s[...] = (q * rq * (gqv * scale)).astype(BF16)
        kn_s[...] = (k * rk * gkv).astype(BF16)
        v_s[...] = _sb_heads(v_ref).astype(BF16)
        dkn_s[...] = jnp.zeros_like(dkn_s)
        dv_s[...] = jnp.zeros_like(dv_s)
        r_i = _iota2((tk, tk), 0)
        c_i = _iota2((tk, tk), 1)
        upto = (r_i <= c_i).astype(BF16)
        before = (r_i < c_i).astype(BF16)

        def qblock(i, _):
            rows = pl.ds(pl.multiple_of(i * tq, tq), tq)
            qi = qn_s[:, rows, :]
            doi = _sb_heads(do_ref, rows).astype(BF16)
            tot_i = jnp.max(_sb_heads(o_ref, rows), axis=-1, keepdims=True)
            jd = (i * tq) // tk

            def tile(j, dqn, run, run_b, masked):
                cols = pl.ds(pl.multiple_of(j * tk, tk), tk)
                kj = kn_s[:, cols, :]
                lr, ls, mask = _sb_tile(qi, kj, i * tq, j * tk, masked)
                later = tot_i - (_dot2(lr, upto) + run)
                a = jnp.exp(ls + later)
                if masked:
                    a = jnp.where(mask, a, 0.0)
                b = _hdot(doi, v_s[:, cols, :], NT) * a
                cum = _hdot(b, before, NN) + run_b
                beta = jnp.exp(ls)
                dz = b * (1.0 - beta) - cum * beta
                if masked:
                    dz = jnp.where(mask, dz, 0.0)
                dzb = dz.astype(BF16)
                dv_s[:, cols, :] += _hdot(a, doi, TN)
                dkn_s[:, cols, :] += _hdot(dzb, qi, TN)
                dqn = dqn + _hdot(dzb, kj, NN)
                return dqn, run + jnp.sum(lr, axis=-1, keepdims=True), run_b + jnp.sum(b, axis=-1, keepdims=True)

            zero1 = jnp.zeros((SB_PAIR, tq, 1), F32)
            carry = lax.fori_loop(0, jd, lambda j, c: tile(j, c[0], c[1], c[2], False),
                                  (jnp.zeros((SB_PAIR, tq, HEAD_DIM), F32), zero1, zero1))
            dqn, _, _ = tile(jd, carry[0], carry[1], carry[2], True)
            dqn_s[:, rows, :] = dqn * scale
            return 0

        lax.fori_loop(0, nq, qblock, 0)

        @pl.when(h == 0)
        def _():
            dgq_ref[...] = jnp.zeros_like(dgq_ref)
            dgk_ref[...] = jnp.zeros_like(dgk_ref)

        heads_sum = lambda z: jnp.sum(jnp.sum(z, axis=1, keepdims=True), axis=0)
        dqn = dqn_s[...]
        qh = q * rq
        dgq_ref[...] += heads_sum(dqn * qh)
        dy = dqn * gqv
        dq = (rq * (dy - qh * jnp.mean(dy * qh, axis=-1, keepdims=True))).astype(BF16)
        dkn = dkn_s[...]
        kh = k * rk
        dgk_ref[...] += heads_sum(dkn * kh)
        dyk = dkn * gkv
        dk = (rk * (dyk - kh * jnp.mean(dyk * kh, axis=-1, keepdims=True))).astype(BF16)
        dv = dv_s[...].astype(BF16)
        for hh in range(SB_PAIR):
            lanes = slice(hh * HEAD_DIM, (hh + 1) * HEAD_DIM)
            dq_ref[:, lanes] = dq[hh]
            dk_ref[:, lanes] = dk[hh]
            dv_ref[:, lanes] = dv[hh]

    npair = SB_HEADS // SB_PAIR
    vec = pl.BlockSpec((1, HEAD_DIM), lambda h: (0, 0))
    hb = lambda off: pl.BlockSpec((S, SB_PW), lambda h: (0, off + h), pipeline_mode=pl.Buffered(1))
    dq, dk, dv, dgq, dgk = pl.pallas_call(
        body, grid=(npair,),
        in_specs=[hb(0), hb(0), hb(0), hb(npair), hb(2 * npair), vec, vec],
        out_specs=(hb(0), hb(0), hb(0), vec, vec),
        out_shape=(jax.ShapeDtypeStruct((S, MIX_WIDTH), BF16),) * 3 + (jax.ShapeDtypeStruct((1, HEAD_DIM), F32),) * 2,
        scratch_shapes=[pltpu.VMEM((SB_PAIR, S, HEAD_DIM), BF16)] * 3 + [pltpu.VMEM((SB_PAIR, S, HEAD_DIM), F32)] * 3,
        compiler_params=_cp(("arbitrary",)), name=name)(
            dmix, tot, proj, proj, proj, gq.reshape(1, HEAD_DIM), gk.reshape(1, HEAD_DIM))
    return [dq, dk, dv], dgq, dgk


def _shift_down(x, k):
    if k == 0:
        return x
    r = pltpu.roll(x, k, 0)
    return jnp.where(_iota2(x.shape, 0) >= k, r, 0.0)


def _shift_up(x, k):
    if k == 0:
        return x
    n = x.shape[0]
    r = pltpu.roll(x, n - k, 0)
    return jnp.where(_iota2(x.shape, 0) < n - k, r, 0.0)


def _conv(x, w):
    c = w[DN_CONV - 1] * x
    for k in range(1, DN_CONV):
        c = c + w[DN_CONV - 1 - k] * _shift_down(x, k)
    return c


def _dn_pre_fwd(proj, conv_w, col0, ncols, *, l2, scale, name):
    S = proj.shape[0]
    cb = col0 // HEAD_DIM

    def body(x_ref, w_ref, o_ref):
        c = _conv(x_ref[...], [w_ref[k:k + 1, :] for k in range(DN_CONV)])
        a = c * _sigmoid(c)
        if l2:
            a = a * (lax.rsqrt(jnp.sum(a * a, axis=-1, keepdims=True) + EPS) * scale)
        o_ref[...] = a

    return pl.pallas_call(
        body, grid=(ncols // HEAD_DIM,),
        in_specs=[pl.BlockSpec((S, HEAD_DIM), lambda j: (0, cb + j)), pl.BlockSpec((DN_CONV, HEAD_DIM), lambda j: (0, cb + j))],
        out_specs=pl.BlockSpec((S, HEAD_DIM), lambda j: (0, j)), out_shape=jax.ShapeDtypeStruct((S, ncols), F32),
        compiler_params=_cp(("parallel",)), name=name)(proj, conv_w)


def _dn_pre_bwd(dout, proj, conv_w, col0, ncols, *, l2, scale, name):
    S = proj.shape[0]
    cb = col0 // HEAD_DIM
    dw_in = HEAD_DIM

    def body(d_ref, x_ref, w_ref, dx_ref, dw_ref):
        x = x_ref[...]
        w = [w_ref[k:k + 1, :] for k in range(DN_CONV)]
        c = _conv(x, w)
        sg = _sigmoid(c)
        a = c * sg
        d = d_ref[...]
        if l2:
            r = lax.rsqrt(jnp.sum(a * a, axis=-1, keepdims=True) + EPS)
            y = a * r
            d = d * scale
            d = r * (d - y * jnp.sum(d * y, axis=-1, keepdims=True))
        dc = d * sg * (1.0 + c * (1.0 - sg))
        dx = w[DN_CONV - 1] * dc
        for k in range(1, DN_CONV):
            dx = dx + w[DN_CONV - 1 - k] * _shift_up(dc, k)
        dx_ref[...] = dx.astype(BF16)
        for k in range(DN_CONV):
            dw_ref[3 - k:4 - k, :] = jnp.sum(dc * _shift_down(x, k), axis=0, keepdims=True)

    return pl.pallas_call(
        body, grid=(ncols // HEAD_DIM,),
        in_specs=[pl.BlockSpec((S, dw_in), lambda j: (0, j)), pl.BlockSpec((S, HEAD_DIM), lambda j: (0, cb + j)),
                  pl.BlockSpec((DN_CONV, HEAD_DIM), lambda j: (0, cb + j))],
        out_specs=(pl.BlockSpec((S, HEAD_DIM), lambda j: (0, j)), pl.BlockSpec((DN_CONV, HEAD_DIM), lambda j: (0, j))),
        out_shape=(jax.ShapeDtypeStruct((S, ncols), BF16), jax.ShapeDtypeStruct((DN_CONV, ncols), F32)),
        compiler_params=_cp(("parallel",)), name=name)(dout, proj, conv_w)


def _dn_ab_fwd(proj, a_log, dt_bias, *, name, tm=512):
    S = proj.shape[0]
    tm = min(tm, S)
    ab = P0_AB // LANE

    def body(a_ref, b_ref, al_ref, dt_ref, g_ref, be_ref):
        g_ref[...] = -jnp.exp(al_ref[...]) * _softplus(a_ref[...] + dt_ref[...])
        be_ref[...] = _sigmoid(b_ref[...])

    vec = pl.BlockSpec((1, LANE), lambda i: (0, 0))
    out = pl.BlockSpec((tm, LANE), lambda i: (i, 0))
    return pl.pallas_call(
        body, grid=(S // tm,),
        in_specs=[pl.BlockSpec((tm, LANE), lambda i: (i, ab)), pl.BlockSpec((tm, LANE), lambda i: (i, ab + 1)), vec, vec],
        out_specs=(out, out), out_shape=(jax.ShapeDtypeStruct((S, LANE), F32),) * 2,
        compiler_params=_cp(("parallel",)), name=name)(proj, proj, a_log, dt_bias)


def _dn_ab_bwd(dg, dbeta, proj, a_log, dt_bias, *, name, tm=512):
    S = proj.shape[0]
    tm = min(tm, S)
    ab = P0_AB // LANE

    def body(dg_ref, db_ref, a_ref, b_ref, al_ref, dt_ref, dab_ref, dal_ref, ddt_ref):
        i = pl.program_id(0)
        ea = jnp.exp(al_ref[...])
        u = a_ref[...] + dt_ref[...]
        dgv = dg_ref[...]
        da = dgv * (-ea) * _sigmoid(u)
        be = _sigmoid(b_ref[...])
        dab_ref[:, 0:LANE] = da.astype(BF16)
        dab_ref[:, LANE:2 * LANE] = (db_ref[...] * be * (1.0 - be)).astype(BF16)
        dab_ref[:, 2 * LANE:] = jnp.zeros((tm, 2 * LANE), BF16)

        @pl.when(i == 0)
        def _():
            dal_ref[...] = jnp.zeros_like(dal_ref)
            ddt_ref[...] = jnp.zeros_like(ddt_ref)

        dal_ref[...] += jnp.sum(dgv * (-ea) * _softplus(u), axis=0, keepdims=True)
        ddt_ref[...] += jnp.sum(da, axis=0, keepdims=True)

    vec = pl.BlockSpec((1, LANE), lambda i: (0, 0))
    row = pl.BlockSpec((tm, LANE), lambda i: (i, 0))
    return pl.pallas_call(
        body, grid=(S // tm,),
        in_specs=[row, row, pl.BlockSpec((tm, LANE), lambda i: (i, ab)), pl.BlockSpec((tm, LANE), lambda i: (i, ab + 1)), vec, vec],
        out_specs=(pl.BlockSpec((tm, 4 * LANE), lambda i: (i, 0)), vec, vec),
        out_shape=(jax.ShapeDtypeStruct((S, 4 * LANE), BF16), jax.ShapeDtypeStruct((1, LANE), F32),
                   jax.ShapeDtypeStruct((1, LANE), F32)),
        compiler_params=_cp(("arbitrary",)), name=name)(dg, dbeta, proj, proj, a_log, dt_bias)


def _dot3(a, b):
    ah = a.astype(BF16)
    al = (a - ah.astype(F32)).astype(BF16)
    bh = b.astype(BF16)
    bl = (b - bh.astype(F32)).astype(BF16)
    d = lambda u, v: lax.dot_general(u, v, (NN, ((), ())), preferred_element_type=F32)
    return d(ah, bh) + (d(ah, bl) + d(al, bh))


DN_PAIR = 4
DN_QK = DN_PAIR // 2


def _dn_qk_heads(ref, rows):
    return jnp.stack([ref[rows, (hh // 2) * HEAD_DIM:(hh // 2 + 1) * HEAD_DIM] for hh in range(DN_PAIR)])


def _dn_big(shape, imap):
    return pl.BlockSpec(shape, imap, pipeline_mode=pl.Buffered(1))


_pdot = _hdot


def _tri_inverse(a):
    eye = (_iota2((CH, CH), 0) == _iota2((CH, CH), 1)).astype(F32)
    d3 = lambda u, v: jnp.stack([_dot3(u[i], v[i]) for i in range(DN_PAIR)])
    t = eye - a
    x = d3(a, a)
    n = 2
    while True:
        t = t + d3(t, x)
        n *= 2
        if n >= CH:
            break
        x = d3(x, x)
    return t


def _pick_col(m, n):
    return jnp.sum(jnp.where(_iota2(m.shape, 2) == n, m, 0.0), axis=2, keepdims=True)


def _dn_chunk_common(kk, qk, gc_c, gc_r, be_c):
    r_i = _iota2((CH, CH), 0)
    c_i = _iota2((CH, CH), 1)
    incl = r_i >= c_i
    strict = r_i > c_i
    dec = jnp.exp(jnp.where(incl, gc_c - gc_r, -1e30))
    e = jnp.exp(gc_c)
    gl = jnp.sum(jnp.where(_iota2((1, CH), 1) == CH - 1, gc_r, 0.0), axis=-1, keepdims=True)
    kds = jnp.exp(gl - gc_c)
    cd = jnp.exp(gl)
    a = jnp.where(strict, be_c * kk * dec, 0.0)
    p = qk * dec
    return dict(incl=incl, strict=strict, dec=dec, e=e, kds=kds, cd=cd, kk=kk, a=a, qk=qk, p=p)


def _dn_decay_tables(g_ref, b_ref, gcr, gcc, bcc):
    r_i = _iota2((CH, CH), 0)
    c_i = _iota2((CH, CH), 1)
    lc = (r_i >= c_i).astype(F32)
    eye = (r_i == c_i).astype(F32)
    for hh in range(DN_PAIR):
        g_rows_v = g_ref[hh]
        gcr[hh] = _fdot(g_rows_v, lc, NT)
        gcc[hh] = _fdot(lc, g_rows_v, NT)
        bcc[hh] = _fdot(eye, b_ref[hh], NT)
    return lc


def _dn_core_fwd(qn, kn, vc, g_rows, b_rows, out_g, *, name):
    S = qn.shape[0]
    nc = S // CH

    def body(q_ref, k_ref, v_ref, g_ref, b_ref, og_ref, o_ref, st_ref, t_ref, gcr, gcc, bcc):
        _dn_decay_tables(g_ref, b_ref, gcr, gcc, bcc)
        ogv = og_ref[...]

        def chunk(n, states):
            rows = pl.ds(pl.multiple_of(n * CH, CH), CH)
            q = _dn_qk_heads(q_ref, rows)
            k = _dn_qk_heads(k_ref, rows)
            kk = _pdot(k, k, NT)
            qk = _pdot(q, k, NT)
            v = jnp.stack([v_ref[rows, hh * HEAD_DIM:(hh + 1) * HEAD_DIM] for hh in range(DN_PAIR)])
            gc_c = _pick_col(gcc[...], n)
            be_c = _pick_col(bcc[...], n)
            gc_r = gcr[:, pl.ds(n, 1), :]
            c = _dn_chunk_common(kk, qk, gc_c, gc_r, be_c)
            t = _tri_inverse(c["a"])
            u0 = _pdot(t, be_c * v, NN)
            w = _pdot(t, (be_c * c["e"]) * k, NN)
            u = u0 - _pdot(w, states, NN)
            o = _pdot(c["e"] * q, states, NN) + _pdot(c["p"], u, NN)
            on = o * lax.rsqrt(jnp.mean(o * o, axis=-1, keepdims=True) + EPS) * ogv
            for hh in range(DN_PAIR):
                st_ref[hh, n] = states[hh]
                t_ref[hh, n] = t[hh]
                o_ref[rows, hh * HEAD_DIM:(hh + 1) * HEAD_DIM] = on[hh]
            return c["cd"] * states + _pdot(c["kds"] * k, u, TN)

        lax.fori_loop(0, nc, chunk, jnp.zeros((DN_PAIR, HEAD_DIM, HEAD_DIM), F32))

    qk_spec = pl.BlockSpec((S, DN_QK * HEAD_DIM), lambda h: (0, h))
    v_in = pl.BlockSpec((S, DN_PAIR * HEAD_DIM), lambda h: (0, h))
    v_spec = _dn_big((S, DN_PAIR * HEAD_DIM), lambda h: (0, h))
    rows_spec = pl.BlockSpec((DN_PAIR, LANE, CH), lambda h: (h, 0, 0))
    return pl.pallas_call(
        body, grid=(DN_V_HEADS // DN_PAIR,),
        in_specs=[qk_spec, qk_spec, v_in, rows_spec, rows_spec, pl.BlockSpec((1, HEAD_DIM), lambda h: (0, 0))],
        out_specs=(v_spec, _dn_big((DN_PAIR, nc, HEAD_DIM, HEAD_DIM), lambda h: (h, 0, 0, 0)),
                   _dn_big((DN_PAIR, nc, CH, CH), lambda h: (h, 0, 0, 0))),
        out_shape=(jax.ShapeDtypeStruct((S, MIX_WIDTH), F32), jax.ShapeDtypeStruct((DN_V_HEADS, nc, HEAD_DIM, HEAD_DIM), F32),
                   jax.ShapeDtypeStruct((DN_V_HEADS, nc, CH, CH), F32)),
        scratch_shapes=[pltpu.VMEM((DN_PAIR, LANE, CH), F32), pltpu.VMEM((DN_PAIR, CH, LANE), F32),
                        pltpu.VMEM((DN_PAIR, CH, LANE), F32)],
        compiler_params=_cp(("parallel",)), name=name)(qn, kn, vc, g_rows, b_rows, out_g.reshape(1, HEAD_DIM))


def _dn_chunk_bwd(q, k, v, kk, qk, state, t, gc_c, gc_r, be_c, don, ogv, ds_next):
    ones = jnp.ones((CH, LANE), F32)
    last_row = _iota2((CH, 1), 0) == CH - 1
    rowsum = lambda z: jnp.sum(z, axis=-1, keepdims=True)
    colsum = lambda z: jnp.sum(z, axis=-2, keepdims=True)
    c = _dn_chunk_common(kk, qk, gc_c, gc_r, be_c)
    e, kds, cd, dec, a, p = c["e"], c["kds"], c["cd"], c["dec"], c["a"], c["p"]
    vb = be_c * v
    kbe = (be_c * e) * k
    u0 = _pdot(t, vb, NN)
    w = _pdot(t, kbe, NN)
    u = u0 - _pdot(w, state, NN)
    qd = e * q
    kd = kds * k
    o = _pdot(qd, state, NN) + _pdot(p, u, NN)
    r = lax.rsqrt(jnp.mean(o * o, axis=-1, keepdims=True) + EPS)
    y = o * r
    dog = colsum(don * y)
    dy = don * ogv
    d_o = r * (dy - y * jnp.mean(dy * y, axis=-1, keepdims=True))
    du = _pdot(p, d_o, TN) + _pdot(kd, ds_next, NN)
    dqd = _pdot(d_o, state, NT)
    dstate = _pdot(qd, d_o, TN) + cd * ds_next - _pdot(w, du, TN)
    dcd = colsum(rowsum(ds_next * state))
    dkd = _pdot(u, ds_next, NT)
    dw = -_pdot(du, state, NT)
    dvb = _pdot(t, du, TN)
    dkbe = _pdot(t, dw, TN)
    da = -jnp.where(c["strict"], _pdot(dvb, u0, NT) + _pdot(dkbe, w, NT), 0.0)
    dp = jnp.where(c["incl"], _pdot(d_o, u, NT), 0.0)
    gmat = da * a + dp * p
    dad = da * dec
    x = be_c * dad
    dpd = dp * dec
    dk = _pdot(x, k, NN) + _pdot(x, k, TN) + _pdot(dpd, q, TN)
    dq = _pdot(dpd, k, NN) + e * dqd
    dbe = rowsum(dad * c["kk"])
    dgc = rowsum(gmat) + rowsum(dqd * q) * e
    rk = rowsum(dkd * k) * kds
    dk = dk + kds * dkd
    dgc = dgc - rk
    dgl = colsum(rk) + dcd * cd
    sk = rowsum(dkbe * k)
    dk = dk + (be_c * e) * dkbe
    dbe = dbe + sk * e + rowsum(dvb * v)
    dgc = dgc + sk * be_c * e
    dgc = dgc + jnp.where(last_row, dgl, 0.0)
    dgc = dgc - _pdot(gmat, ones, TN, dot=_fdot)
    return dq, dk, be_c * dvb, dgc, dbe, dog, dstate


def _dn_core_bwd(dmix, qn, kn, vc, g_rows, b_rows, out_g, states, tinv, *, name):
    S = qn.shape[0]
    nc = S // CH

    def body(do_ref, q_ref, k_ref, v_ref, g_ref, b_ref, og_ref, st_ref, t_ref,
             dq_ref, dk_ref, dv_ref, dg_ref, db_ref, dog_ref, gcr, gcc, bcc, dgc_acc):
        h = pl.program_id(0)
        lc = _dn_decay_tables(g_ref, b_ref, gcr, gcc, bcc)
        ogv = og_ref[...]
        dgc_acc[...] = jnp.zeros_like(dgc_acc)
        db_ref[...] = jnp.zeros_like(db_ref)
        lane_n = _iota2((CH, LANE), 1)

        @pl.when(h == 0)
        def _():
            dog_ref[...] = jnp.zeros_like(dog_ref)

        def chunk(m, carry):
            ds_nexts, dog = carry
            n = nc - 1 - m
            rows = pl.ds(pl.multiple_of(n * CH, CH), CH)
            q = _dn_qk_heads(q_ref, rows)
            k = _dn_qk_heads(k_ref, rows)
            kk = _pdot(k, k, NT)
            qk = _pdot(q, k, NT)
            heads = lambda ref: jnp.stack([ref[rows, hh * HEAD_DIM:(hh + 1) * HEAD_DIM] for hh in range(DN_PAIR)])
            state = jnp.stack([st_ref[hh, n] for hh in range(DN_PAIR)])
            t = jnp.stack([t_ref[hh, n] for hh in range(DN_PAIR)])
            dq, dk, dv, dgc, dbe, dog_h, dstate = _dn_chunk_bwd(
                q, k, heads(v_ref), kk, qk, state, t, _pick_col(gcc[...], n), gcr[:, pl.ds(n, 1), :],
                _pick_col(bcc[...], n), heads(do_ref), ogv, ds_nexts)
            for hh in range(DN_PAIR):
                dv_ref[rows, hh * HEAD_DIM:(hh + 1) * HEAD_DIM] = dv[hh]
            dgc_acc[...] = jnp.where(lane_n == n, dgc, dgc_acc[...])
            db_ref[...] = jnp.where(lane_n == n, dbe, db_ref[...])
            for i in range(DN_QK):
                dq_ref[rows, i * HEAD_DIM:(i + 1) * HEAD_DIM] = dq[2 * i] + dq[2 * i + 1]
                dk_ref[rows, i * HEAD_DIM:(i + 1) * HEAD_DIM] = dk[2 * i] + dk[2 * i + 1]
            return dstate, dog + jnp.sum(dog_h, axis=0)

        _, dog = lax.fori_loop(0, nc, chunk, (jnp.zeros((DN_PAIR, HEAD_DIM, HEAD_DIM), F32), jnp.zeros((1, HEAD_DIM), F32)))
        dog_ref[...] += dog
        for hh in range(DN_PAIR):
            dg_ref[hh] = _fdot(lc, dgc_acc[hh], TN)

    qk_spec = _dn_big((S, DN_QK * HEAD_DIM), lambda h: (0, h))
    v_spec = _dn_big((S, DN_PAIR * HEAD_DIM), lambda h: (0, h))
    rows_spec = pl.BlockSpec((DN_PAIR, LANE, CH), lambda h: (h, 0, 0))
    cols_spec = pl.BlockSpec((DN_PAIR, CH, LANE), lambda h: (h, 0, 0))
    vec = pl.BlockSpec((1, HEAD_DIM), lambda h: (0, 0))
    qk_out = jax.ShapeDtypeStruct((S, DN_QK_WIDTH), F32)
    return pl.pallas_call(
        body, grid=(DN_V_HEADS // DN_PAIR,),
        in_specs=[v_spec, qk_spec, qk_spec, v_spec, rows_spec, rows_spec, vec,
                  _dn_big((DN_PAIR, nc, HEAD_DIM, HEAD_DIM), lambda h: (h, 0, 0, 0)),
                  _dn_big((DN_PAIR, nc, CH, CH), lambda h: (h, 0, 0, 0))],
        out_specs=(qk_spec, qk_spec, v_spec, cols_spec, cols_spec, vec),
        out_shape=(qk_out, qk_out, jax.ShapeDtypeStruct((S, MIX_WIDTH), F32), jax.ShapeDtypeStruct((DN_V_HEADS, CH, LANE), F32),
                   jax.ShapeDtypeStruct((DN_V_HEADS, CH, LANE), F32), jax.ShapeDtypeStruct((1, HEAD_DIM), F32)),
        scratch_shapes=[pltpu.VMEM((DN_PAIR, LANE, CH), F32), pltpu.VMEM((DN_PAIR, CH, LANE), F32),
                        pltpu.VMEM((DN_PAIR, CH, LANE), F32), pltpu.VMEM((DN_PAIR, CH, LANE), F32)],
        compiler_params=_cp(("arbitrary",)), name=name)(
            dmix, qn, kn, vc, g_rows, b_rows, out_g.reshape(1, HEAD_DIM), states, tinv)


def _rows_form(x, nc):
    t = x[:, :DN_V_HEADS].T.reshape(DN_V_HEADS, nc, CH)
    return jnp.pad(t, ((0, 0), (0, LANE - nc), (0, 0)))


def _cols_to_nat(x, nc):
    t = jnp.transpose(x[:, :, :nc], (2, 1, 0)).reshape(nc * CH, DN_V_HEADS)
    return jnp.pad(t, ((0, 0), (0, LANE - DN_V_HEADS)))


_C_QKV = 2 * DN_QK_WIDTH + MIX_WIDTH


def _padded_pieces(lo, hi):
    a0, b0, x0 = _C_QKV, _C_QKV + DN_V_HEADS, _C_QKV + 2 * DN_V_HEADS
    out = []
    for t0, t1, shift in ((0, a0, 0), (a0, b0, P0_AB - a0), (b0, x0, P0_AB + LANE - b0), (x0, DN_PROJ, a0 - x0)):
        s, e = max(lo, t0), min(hi, t1)
        if s < e:
            out.append((s + shift, e + shift))
    return out


def _shard_pieces(s):
    return _padded_pieces(s * P0_SHARD, (s + 1) * P0_SHARD)


def _shard_runs(s):
    return [(lo // LANE, -(-hi // LANE)) for lo, hi in _shard_pieces(s)]


WIN_COLS = LANE * max(sum(b - a for a, b in _shard_runs(s)) for s in range(N_CHIPS))


def _pack_own_shard(s, shard):
    zeros = lambda n: jnp.zeros((shard.shape[0], n), shard.dtype)
    out, t = [], 0
    for (lo, hi), (b0, b1) in zip(_shard_pieces(s), _shard_runs(s)):
        out += [zeros(lo - b0 * LANE), shard[:, t:t + hi - lo], zeros(b1 * LANE - hi)]
        t += hi - lo
    out.append(zeros(WIN_COLS - LANE * sum(b - a for a, b in _shard_runs(s))))
    return jnp.concatenate([o for o in out if o.shape[1]], axis=1)


def _unpack_own_shard(s, win):
    out, off = [], 0
    for (lo, hi), (b0, b1) in zip(_shard_pieces(s), _shard_runs(s)):
        start = off + lo - b0 * LANE
        out.append(win[:, start:start + hi - lo])
        off += (b1 - b0) * LANE
    return jnp.concatenate(out, axis=1)


def _windows_to_padded(wins):
    src = {}
    for s in range(N_CHIPS):
        off = 0
        for b0, b1 in _shard_runs(s):
            for b in range(b0, b1):
                src.setdefault(b, []).append((s, off + (b - b0) * LANE))
            off += (b1 - b0) * LANE
    out, b = [], 0
    while b < P0 // LANE:
        if b not in src:
            out.append(jnp.zeros((wins.shape[1], LANE), wins.dtype))
            b += 1
        elif len(src[b]) > 1:
            out.append(sum(wins[s][:, o:o + LANE] for s, o in src[b]))
            b += 1
        else:
            (s, o), n = src[b][0], 1
            while src.get(b + n) == [(s, o + n * LANE)]:
                n += 1
            out.append(wins[s][:, o:o + n * LANE])
            b += n
    return jnp.concatenate(out, axis=1)


def _padded_to_window(s, g):
    out = [g[:, b0 * LANE:b1 * LANE] for b0, b1 in _shard_runs(s)]
    rest = WIN_COLS - sum(o.shape[1] for o in out)
    return jnp.concatenate(out + ([jnp.zeros((g.shape[0], rest), g.dtype)] if rest else []), axis=1)


def _pad_lane(v):
    v = v.reshape(1, -1)
    return jnp.pad(v, ((0, 0), (0, LANE - v.shape[1])))


SLOT1 = SB_PROJ // N_CHIPS
MM_TN = 512


def _local_step(x, mem, target, norm_g, mem_norm_g, xa_q_g, xa_k_g, w_in0, conv_w, a_log, dt_bias, out_g, sb_q_g, sb_k_g,
                late_weights, early_grads, grads_swapped):
    S = x.shape[0]
    nc = S // CH
    al = _pad_lane(a_log)
    dtb = _pad_lane(dt_bias)
    q_scale = HEAD_DIM ** -0.5
    tiles1 = SLOT1 // MM_TN

    kv_rhs = lambda l: pl.BlockSpec((N_CHIPS, None, D_MODEL // N_CHIPS, MM_TN), lambda i, j: (0, l, 0, j))
    kv_rhs_t = lambda l: pl.BlockSpec((None, None, D_MODEL // N_CHIPS, 2 * XA_WIDTH), lambda i, j: (j, l, 0, 0))
    out_rhs = lambda l: pl.BlockSpec((N_CHIPS, None, INNER // N_CHIPS, MM_TN), lambda i, j: (0, l, 0, j))
    out_rhs_t = lambda l: pl.BlockSpec((None, None, MM_TN, D_MODEL), lambda i, j: (j // 2, l, j % 2, 0))
    in1_rhs = pl.BlockSpec((None, 2, D_MODEL // 2, MM_TN), lambda i, j: (j // tiles1, 0, 0, j % tiles1))
    in1_rhs_t = pl.BlockSpec((None, None, D_MODEL // 2, MM_TN), lambda i, j, k: (k // tiles1, j, 0, k % tiles1))
    slot_rows = lambda rows: dict(
        tm=rows, o_spec=pl.BlockSpec((None, rows, MM_TN), lambda i, j: (i, 0, j)),
        o_shape=jax.ShapeDtypeStruct((N_CHIPS, rows, 2 * XA_WIDTH), BF16))
    in1_out = dict(tm=D_MODEL // 2, o_spec=pl.BlockSpec((None, None, D_MODEL // 2, MM_TN),
                                                        lambda i, j: (j // tiles1, i, 0, j % tiles1)),
                   o_shape=jax.ShapeDtypeStruct((N_CHIPS, 2, D_MODEL // 2, SLOT1), BF16))

    h0 = _rmsnorm_fwd(x, norm_g[0], name="norm0")
    proj0 = _matmul(h0, w_in0, name="proj0")
    qn = _dn_pre_fwd(proj0, conv_w, 0, DN_QK_WIDTH, l2=True, scale=q_scale, name="dn_pre_q")
    kn = _dn_pre_fwd(proj0, conv_w, DN_QK_WIDTH, DN_QK_WIDTH, l2=True, scale=1.0, name="dn_pre_k")
    vc = _dn_pre_fwd(proj0, conv_w, 2 * DN_QK_WIDTH, MIX_WIDTH, l2=False, scale=1.0, name="dn_pre_v")
    g_nat, b_nat = _dn_ab_fwd(proj0, al, dtb, name="dn_ab")
    g_rows = _rows_form(g_nat, nc)
    b_rows = _rows_form(b_nat, nc)
    mix0, states, tinv = _dn_core_fwd(qn, kn, vc, g_rows, b_rows, out_g, name="dn_core")
    w_kv, w_out, w_in1 = late_weights(mix0)
    mem_n = _rmsnorm_fwd(mem, mem_norm_g, name="mem_norm")
    kv = [_matmul(mem_n, w_kv, n=2 * XA_WIDTH, tn=MM_TN, b_spec=kv_rhs(l), name=f"kv{l}") for l in range(2)]
    xa0 = _xa_fwd(proj0, P0_XQ, kv[0], xa_q_g[0], xa_k_g[0], name="xa0")
    y0 = _gate_fwd(mix0, xa0, proj0, P0_Z, name="gate0")
    x1 = _matmul(y0, w_out, n=D_MODEL, tn=MM_TN, b_spec=out_rhs(0), res=x, name="out0")

    h1 = _rmsnorm_fwd(x1, norm_g[1], name="norm1")
    proj1 = _matmul(h1, w_in1, n=SB_PROJ, tn=MM_TN, b_spec=in1_rhs, name="proj1")
    mix1, tot1 = _sb_fwd(proj1, sb_q_g, sb_k_g, name="sb")
    xa1 = _xa_fwd(proj1, P1_XQ, kv[1], xa_q_g[1], xa_k_g[1], name="xa1")
    y1 = _gate_fwd(mix1, xa1, proj1, P1_Z, name="gate1")
    x2 = _matmul(y1, w_out, n=D_MODEL, tn=MM_TN, b_spec=out_rhs(1), res=x1, name="out1")

    dx2, loss_vec = _loss_head(x2, target, name="loss")

    d_wout1 = _matmul(y1, dx2, ta=True, name="d_wout1", **slot_rows(INNER // N_CHIPS))
    dy1 = _matmul(dx2, w_out, tb=True, n=INNER, tn=MM_TN, b_spec=out_rhs_t(1), name="dy1")
    dcat1, dz1 = _gate_bwd(dy1, mix1, xa1, proj1, P1_Z, name="gate1_bwd")
    dqkv1, d_sbq, d_sbk = _sb_bwd(dcat1, tot1, proj1, sb_q_g, sb_k_g, name="sb_bwd")
    dxq1, dkv1, d_xaq1, d_xak1 = _xa_bwd(dcat1, proj1, P1_XQ, kv[1], xa_q_g[1], xa_k_g[1], name="xa1_bwd")
    dproj1 = dqkv1 + [dxq1, dz1]
    d_win1 = _matmul(h1, dproj1, ta=True, name="d_win1", **in1_out)
    d_wkv1 = _matmul(mem_n, dkv1, ta=True, name="d_wkv1", **slot_rows(D_MODEL // N_CHIPS))
    token = early_grads(1, d_win1, d_wout1, d_wkv1)
    dproj1 = dqkv1 + [dxq1 + token[0, 0].astype(BF16), dz1]
    dh1 = _matmul(dproj1, w_in1, tb=True, n=D_MODEL, tn=D_MODEL // 2, tk=MM_TN, b_spec=in1_rhs_t, name="dh1")
    token = grads_swapped(dh1)
    dx1, d_ng1 = _rmsnorm_bwd(dh1, x1, norm_g[1] + token[0, 0], dx2, name="norm1_bwd")

    d_wout0 = _matmul(y0, dx1, ta=True, name="d_wout0", **slot_rows(INNER // N_CHIPS))
    dy0 = _matmul(dx1, w_out, tb=True, n=INNER, tn=MM_TN, b_spec=out_rhs_t(0), name="dy0")
    dcat0, dz0 = _gate_bwd(dy0, mix0, xa0, proj0, P0_Z, name="gate0_bwd")
    dqv, dkv_h, dvc, dg_cols, db_cols, d_outg = _dn_core_bwd(
        dcat0, qn, kn, vc, g_rows, b_rows, out_g, states, tinv, name="dn_core_bwd")
    dpq, dwq = _dn_pre_bwd(dqv, proj0, conv_w, 0, DN_QK_WIDTH, l2=True, scale=q_scale, name="dn_pre_q_bwd")
    dpk, dwk = _dn_pre_bwd(dkv_h, proj0, conv_w, DN_QK_WIDTH, DN_QK_WIDTH, l2=True, scale=1.0, name="dn_pre_k_bwd")
    dpv, dwv = _dn_pre_bwd(dvc, proj0, conv_w, 2 * DN_QK_WIDTH, MIX_WIDTH, l2=False, scale=1.0, name="dn_pre_v_bwd")
    dab, d_alog, d_dt = _dn_ab_bwd(_cols_to_nat(dg_cols, nc), _cols_to_nat(db_cols, nc), proj0, al, dtb, name="dn_ab_bwd")
    dxq0, dkv0, d_xaq0, d_xak0 = _xa_bwd(dcat0, proj0, P0_XQ, kv[0], xa_q_g[0], xa_k_g[0], name="xa0_bwd")
    d_win0 = _matmul(h0, [dpq, dpk, dpv, dxq0, dz0, dab], ta=True, out_dtype=BF16, name="d_win0")
    d_wkv0 = _matmul(mem_n, dkv0, ta=True, name="d_wkv0", **slot_rows(D_MODEL // N_CHIPS))
    token = early_grads(0, d_win0, d_wout0, d_wkv0)
    zero = token[0, 0]
    dh0 = _matmul([dpq, dpk, dpv, dxq0, dz0, dab + zero.astype(BF16)], w_in0, tb=True, tk=MM_TN, name="dh0")
    dx0, d_ng0 = _rmsnorm_bwd(dh0, x, norm_g[0] + zero, dx1, name="norm0_bwd")

    dmem0 = _matmul(dkv0, w_kv, tb=True, n=D_MODEL, tn=D_MODEL // N_CHIPS, b_spec=kv_rhs_t(0), name="dmem0")
    dmem_n = _matmul(dkv1, w_kv, tb=True, n=D_MODEL, tn=D_MODEL // N_CHIPS, b_spec=kv_rhs_t(1), res=dmem0, name="dmem1")
    _, d_memg = _rmsnorm_bwd(dmem_n, mem, mem_norm_g, None, name="mem_norm_bwd")

    grads = dict(
        norm_g=jnp.concatenate([d_ng0, d_ng1], axis=0), mem_norm_g=d_memg.reshape(-1),
        xa_q_norm_g=jnp.concatenate([d_xaq0, d_xaq1], axis=0), xa_k_norm_g=jnp.concatenate([d_xak0, d_xak1], axis=0),
        dn_conv_w=jnp.concatenate([dwq, dwk, dwv], axis=1),
        dn_a_log=d_alog[:, :DN_V_HEADS], dn_dt_bias=d_dt[:, :DN_V_HEADS], dn_out_norm_g=d_outg,
        sb_q_norm_g=d_sbq, sb_k_norm_g=d_sbk)
    return loss_vec, dx0, grads


ANY = pl.BlockSpec(memory_space=pl.ANY)


def _place():
    x, y, c = lax.axis_index("x"), lax.axis_index("y"), lax.axis_index("c")
    chips = [(1 - x, y), (x, 1 - y), (1 - x, 1 - y)]
    return x, y, c, 2 * x + y, (x, y, 1 - c), chips


def _rcopy(src, dst, send, recv, i, dev):
    return pltpu.make_async_remote_copy(src_ref=src, dst_ref=dst, send_sem=send.at[i], recv_sem=recv.at[i],
                                        device_id=dev, device_id_type=MESH)


def _swap_halves(xs, *, name):
    nt = len(xs)

    def body(*refs):
        src, dst = refs[:nt], refs[nt:2 * nt]
        send, recv = refs[2 * nt:]
        x, y, c, j, sib, chips = _place()
        cps = []
        for t in range(nt):
            for s in range(N_CHIPS):
                cps.append(_rcopy(src[t].at[s, 1 - c], dst[t].at[s], send, recv, 4 * t + s, sib))
                cps[-1].start()
        for cp in cps:
            cp.wait_recv()
        for cp in cps:
            cp.wait_send()

    return pl.pallas_call(
        body, in_specs=[ANY] * nt, out_specs=[ANY] * nt,
        out_shape=[jax.ShapeDtypeStruct((N_CHIPS,) + a.shape[2:], a.dtype) for a in xs],
        scratch_shapes=[pltpu.SemaphoreType.DMA((4 * nt,)), pltpu.SemaphoreType.DMA((4 * nt,))], name=name)(*xs)


def _swap_with_sibling(fs, *, name):
    nt = len(fs)

    def body(*refs):
        src, dst = refs[:nt], refs[nt:2 * nt]
        send, recv = refs[2 * nt:]
        x, y, c, j, sib, chips = _place()
        cps = [_rcopy(src[t], dst[t], send, recv, t, sib) for t in range(nt)]
        for cp in cps:
            cp.start()
        for cp in cps:
            cp.wait_recv()
        for cp in cps:
            cp.wait_send()

    return pl.pallas_call(
        body, in_specs=[ANY] * nt, out_specs=[ANY] * nt,
        out_shape=[jax.ShapeDtypeStruct(a.shape, a.dtype) for a in fs],
        scratch_shapes=[pltpu.SemaphoreType.DMA((nt,)), pltpu.SemaphoreType.DMA((nt,))], name=name)(*fs)


HBM_SPEC = pl.BlockSpec(memory_space=pltpu.HBM)
SEM_SPEC = pl.BlockSpec(memory_space=pltpu.SEMAPHORE)
SIDE_EFFECT = pltpu.SideEffectType.DATAFLOW_SIDE_EFFECTING


def _gather_plan(src, land):
    x, y, c, j, sib, chips = _place()
    return [(src[t].at[c], land[t].at[j, c], (cx, cy, c), land[t].at[2 * cx + cy, c])
            for t in range(len(src)) for cx, cy in chips]


def _scatter_plan(src, land):
    x, y, c, j, sib, chips = _place()
    return [(src[t].at[2 * cx + cy], land[t].at[k], (cx, cy, c), land[t].at[k])
            for t in range(len(src)) for k, (cx, cy) in enumerate(chips)]


def _swap_plan(src, land):
    x, y, c, j, sib, chips = _place()
    return [(src[t].at[s, 1 - c], land[t].at[s], sib, land[t].at[s]) for t in range(len(src)) for s in range(N_CHIPS)]


def _exchange_start(srcs, lands, plan, *, name, per_tensor=3):
    ns, nb = len(srcs), len(srcs) + len(lands)
    n = per_tensor * ns

    def body(*refs):
        send, recv, token = refs[nb], refs[nb + 1], refs[-1]
        for i, (s, d, dev, _) in enumerate(plan(refs[:ns], refs[ns:nb])):
            _rcopy(s, d, send, recv, i, dev).start()
        token[...] = jnp.zeros_like(token)

    bufs = list(srcs) + list(lands)
    outs = pl.pallas_call(
        body, name=name,
        out_shape=(pltpu.SemaphoreType.DMA((n,)), pltpu.SemaphoreType.DMA((n,)), *[pltpu.HBM(a.shape, a.dtype) for a in bufs],
                   jax.ShapeDtypeStruct((8, LANE), F32)),
        in_specs=[HBM_SPEC] * nb, out_specs=(SEM_SPEC, SEM_SPEC, *[HBM_SPEC] * nb, pl.BlockSpec(memory_space=pltpu.VMEM)),
        input_output_aliases={i: 2 + i for i in range(nb)},
        compiler_params=pltpu.CompilerParams(has_side_effects=SIDE_EFFECT))(
            *[pltpu.with_memory_space_constraint(a, pltpu.HBM) for a in bufs])
    return outs[0], outs[1], list(outs[2:2 + ns]), list(outs[2 + ns:2 + nb]), outs[-1]


def _exchange_wait(srcs, lands, send, recv, after, plan, *, name):
    ns, nb = len(srcs), len(srcs) + len(lands)
    afters = list(after) if isinstance(after, (list, tuple)) else [after]

    def body(*refs):
        send_s, recv_s = refs[nb], refs[nb + 1]
        for i, (s, d, dev, inc) in enumerate(plan(refs[:ns], refs[ns:nb])):
            _rcopy(s, d, send_s, recv_s, i, dev).wait_send()
            _rcopy(inc, inc, send_s, recv_s, i, dev).wait_recv()

    bufs = list(srcs) + list(lands)
    outs = pl.pallas_call(
        body, name=name, out_shape=tuple(pltpu.HBM(a.shape, a.dtype) for a in bufs),
        in_specs=[HBM_SPEC] * nb + [SEM_SPEC, SEM_SPEC] + [ANY] * len(afters), out_specs=tuple([HBM_SPEC] * nb),
        input_output_aliases={i: i for i in range(nb)},
        compiler_params=pltpu.CompilerParams(has_side_effects=SIDE_EFFECT))(*bufs, send, recv, *afters)
    return list(outs[:ns]), list(outs[ns:])


def _forward_halves(lands, *, name):
    nt = len(lands)

    def body(*refs):
        src, dst = refs[:nt], refs[nt:2 * nt]
        send, recv = refs[2 * nt:]
        x, y, c, j, sib, chips = _place()
        cps = []
        for t in range(nt):
            for k, (cx, cy) in enumerate(chips):
                cps.append(_rcopy(src[t].at[2 * cx + cy, c], dst[t].at[2 * cx + cy, c], send, recv, 3 * t + k, sib))
                cps[-1].start()
        for t in range(nt):
            for k, (cx, cy) in enumerate(chips):
                other = dst[t].at[2 * cx + cy, 1 - c]
                _rcopy(other, other, send, recv, 3 * t + k, sib).wait_recv()
        for cp in cps:
            cp.wait_send()

    return pl.pallas_call(
        body, in_specs=[ANY] * nt, out_specs=[ANY] * nt, out_shape=[jax.ShapeDtypeStruct(a.shape, a.dtype) for a in lands],
        input_output_aliases={t: t for t in range(nt)},
        scratch_shapes=[pltpu.SemaphoreType.DMA((3 * nt,)), pltpu.SemaphoreType.DMA((3 * nt,))], name=name)(*lands)


def _all_reduce_small(parts, *, name):
    n = len(parts)
    offs, rows = [], 0
    for p in parts:
        offs.append(rows)
        rows += -(-p.shape[0] // 8) * 8

    def body(*refs):
        p_refs, o_refs = refs[:n], refs[n:2 * n]
        buf, send, recv = refs[2 * n:]
        x, y, c = lax.axis_index("x"), lax.axis_index("y"), lax.axis_index("c")
        me = 4 * x + 2 * y + c
        buf[me] = jnp.zeros((rows, LANE), F32)
        for p_ref, off in zip(p_refs, offs):
            buf[me, off:off + p_ref.shape[0], :] = p_ref[...]
        cps = []
        for r in range(1, 8):
            dev = (x ^ (r >> 2), y ^ ((r >> 1) & 1), c ^ (r & 1))
            cps.append(_rcopy(buf.at[me], buf.at[me], send, recv, r - 1, dev))
            cps[-1].start()
        for r in range(1, 8):
            frm = buf.at[me ^ r]
            _rcopy(frm, frm, send, recv, r - 1, (x, y, c)).wait_recv()
        for cp in cps:
            cp.wait_send()
        acc = buf[0]
        for d in range(1, 8):
            acc = acc + buf[d]
        for o_ref, off in zip(o_refs, offs):
            o_ref[...] = acc[off:off + o_ref.shape[0], :]

    vm = pl.BlockSpec(memory_space=pltpu.VMEM)
    return pl.pallas_call(
        body, in_specs=[vm] * n, out_specs=[vm] * n, out_shape=[jax.ShapeDtypeStruct(p.shape, F32) for p in parts],
        scratch_shapes=[pltpu.VMEM((8, rows, LANE), F32), pltpu.SemaphoreType.DMA((7,)), pltpu.SemaphoreType.DMA((7,))],
        name=name)(*parts)


def _add_halves(x, b, c_idx, *, name, tr=256):
    _, _, R, C = x.shape
    tr = min(tr, R)

    def body(c_ref, x_ref, b_ref, o_ref):
        o_ref[...] = (x_ref[...].astype(F32) + b_ref[...].astype(F32)).astype(o_ref.dtype)

    return pl.pallas_call(
        body,
        grid_spec=pltpu.PrefetchScalarGridSpec(
            num_scalar_prefetch=1, grid=(N_CHIPS, R // tr),
            in_specs=[pl.BlockSpec((None, None, tr, C), lambda s, i, c_ref: (s, c_ref[0], i, 0)),
                      pl.BlockSpec((None, tr, C), lambda s, i, c_ref: (s, i, 0))],
            out_specs=pl.BlockSpec((None, tr, C), lambda s, i, c_ref: (s, i, 0))),
        out_shape=jax.ShapeDtypeStruct(b.shape, b.dtype), compiler_params=_cp(("parallel", "parallel")), name=name)(c_idx, x, b)


def _sum_slot(p, rcv, j_idx, *, name, tr=256):
    _, R, C = p.shape
    tr = min(tr, R)

    def body(j_ref, p_ref, r_ref, o_ref):
        acc = p_ref[...].astype(F32)
        for k in range(3):
            acc = acc + r_ref[k].astype(F32)
        o_ref[...] = acc

    return pl.pallas_call(
        body,
        grid_spec=pltpu.PrefetchScalarGridSpec(
            num_scalar_prefetch=1, grid=(R // tr,),
            in_specs=[pl.BlockSpec((None, tr, C), lambda i, j_ref: (j_ref[0], i, 0)),
                      pl.BlockSpec((3, tr, C), lambda i, j_ref: (0, i, 0))],
            out_specs=pl.BlockSpec((tr, C), lambda i, j_ref: (i, 0))),
        out_shape=jax.ShapeDtypeStruct((R, C), F32), compiler_params=_cp(("parallel",)), name=name)(j_idx, p, rcv)


def _adamw_math(w, g, m, v):
    nm = ADAM_B1 * m + (1.0 - ADAM_B1) * g
    nv = ADAM_B2 * v + (1.0 - ADAM_B2) * (g * g)
    m_hat = nm / (1.0 - ADAM_B1 ** ADAM_STEP)
    v_hat = nv / (1.0 - ADAM_B2 ** ADAM_STEP)
    return -ADAM_LR * (m_hat / (jnp.sqrt(v_hat) + ADAM_EPS) + ADAM_WD * w), nm, nv


def _adamw_halves(w, g_mine, g_theirs, m, v, c_idx, *, name, layer=0, into=None, tr=128):
    _, _, R, C = w.shape
    tr = tr if R % tr == 0 else R

    def body(c_ref, w_ref, gm_ref, gt_ref, m_ref, v_ref, *rest):
        g_ref, d_ref, nm_ref, nv_ref = rest[-4:]
        gv = jnp.where(pl.program_id(0) == c_ref[0], gm_ref[...], gt_ref[...])
        d, nm, nv = _adamw_math(w_ref[...], gv, m_ref[...], v_ref[...])
        g_ref[...] = gv
        d_ref[...] = d
        nm_ref[...] = nm
        nv_ref[...] = nv

    full = pl.BlockSpec((None, None, tr, C), lambda hh, i, c_ref: (layer, hh, i, 0))
    half = pl.BlockSpec((tr, C), lambda hh, i, c_ref: (i, 0))
    sh = jax.ShapeDtypeStruct(w.shape, F32)
    extra = [] if into is None else list(into)
    return pl.pallas_call(
        body,
        grid_spec=pltpu.PrefetchScalarGridSpec(num_scalar_prefetch=1, grid=(2, R // tr),
                                               in_specs=[full, half, half, full, full] + [ANY] * len(extra),
                                               out_specs=(full,) * 4),
        out_shape=(sh,) * 4, input_output_aliases={6 + t: t for t in range(len(extra))},
        compiler_params=_cp(("parallel", "parallel")), name=name)(c_idx, w, g_mine, g_theirs, m, v, *extra)


def _adamw_parts(ws, gs, ms, vs, *, name):
    n = len(ws)

    def body(*refs):
        ins, outs = refs[:4 * n], refs[4 * n:]
        for t in range(n):
            d, nm, nv = _adamw_math(ins[t][...], ins[n + t][...], ins[2 * n + t][...], ins[3 * n + t][...])
            outs[t][...] = d
            outs[n + t][...] = nm
            outs[2 * n + t][...] = nv

    vm = pl.BlockSpec(memory_space=pltpu.VMEM)
    shapes = [jax.ShapeDtypeStruct(w.shape, F32) for w in ws] * 3
    outs = pl.pallas_call(body, in_specs=[vm] * (4 * n), out_specs=[vm] * (3 * n), out_shape=shapes, name=name)(
        *ws, *gs, *ms, *vs)
    return outs[:n], outs[n:2 * n], outs[2 * n:]


_SMALL = ["norm_g", "mem_norm_g", "xa_q_norm_g", "xa_k_norm_g", "dn_a_log", "dn_dt_bias", "dn_out_norm_g",
          "sb_q_norm_g", "sb_k_norm_g"]


def _rows128(a):
    flat = a.reshape(-1)
    pad = -flat.shape[0] % LANE
    if pad:
        flat = jnp.pad(flat, (0, pad))
    return flat.reshape(-1, LANE)


def _unrows(r, shape):
    return r.reshape(-1)[:math.prod(shape)].reshape(shape)


def kernel(x, mem, norm_g, mem_norm_g, mem_w_kv, xa_q_norm_g, xa_k_norm_g, w_out, dn_w_in, dn_conv_w, dn_a_log, dn_dt_bias, dn_out_norm_g, sb_w_in, sb_q_norm_g, sb_k_norm_g, loss_target, m_norm_g, m_mem_norm_g, m_mem_w_kv, m_xa_q_norm_g, m_xa_k_norm_g, m_w_out, m_dn_w_in, m_dn_conv_w, m_dn_a_log, m_dn_dt_bias, m_dn_out_norm_g, m_sb_w_in, m_sb_q_norm_g, m_sb_k_norm_g, v_norm_g, v_mem_norm_g, v_mem_w_kv, v_xa_q_norm_g, v_xa_k_norm_g, v_w_out, v_dn_w_in, v_dn_conv_w, v_dn_a_log, v_dn_dt_bias, v_dn_out_norm_g, v_sb_w_in, v_sb_q_norm_g, v_sb_k_norm_g):
    W = dict(norm_g=norm_g, mem_norm_g=mem_norm_g, mem_w_kv=mem_w_kv, xa_q_norm_g=xa_q_norm_g, xa_k_norm_g=xa_k_norm_g,
             w_out=w_out, dn_w_in=dn_w_in, dn_conv_w=dn_conv_w, dn_a_log=dn_a_log, dn_dt_bias=dn_dt_bias,
             dn_out_norm_g=dn_out_norm_g, sb_w_in=sb_w_in, sb_q_norm_g=sb_q_norm_g, sb_k_norm_g=sb_k_norm_g)
    M = dict(norm_g=m_norm_g, mem_norm_g=m_mem_norm_g, mem_w_kv=m_mem_w_kv, xa_q_norm_g=m_xa_q_norm_g,
             xa_k_norm_g=m_xa_k_norm_g, w_out=m_w_out, dn_w_in=m_dn_w_in, dn_conv_w=m_dn_conv_w, dn_a_log=m_dn_a_log,
             dn_dt_bias=m_dn_dt_bias, dn_out_norm_g=m_dn_out_norm_g, sb_w_in=m_sb_w_in, sb_q_norm_g=m_sb_q_norm_g,
             sb_k_norm_g=m_sb_k_norm_g)
    V = dict(norm_g=v_norm_g, mem_norm_g=v_mem_norm_g, mem_w_kv=v_mem_w_kv, xa_q_norm_g=v_xa_q_norm_g,
             xa_k_norm_g=v_xa_k_norm_g, w_out=v_w_out, dn_w_in=v_dn_w_in, dn_conv_w=v_dn_conv_w, dn_a_log=v_dn_a_log,
             dn_dt_bias=v_dn_dt_bias, dn_out_norm_g=v_dn_out_norm_g, sb_w_in=v_sb_w_in, sb_q_norm_g=v_sb_q_norm_g,
             sb_k_norm_g=v_sb_k_norm_g)
    names = ["norm_g", "mem_norm_g", "mem_w_kv", "xa_q_norm_g", "xa_k_norm_g", "w_out", "dn_w_in", "dn_conv_w",
             "dn_a_log", "dn_dt_bias", "dn_out_norm_g", "sb_w_in", "sb_q_norm_g", "sb_k_norm_g"]
    cx, cy, cc = lax.axis_index("x"), lax.axis_index("y"), lax.axis_index("c")
    slot = 2 * cx + cy
    half_r = D_MODEL // 2
    conv_cols = dn_conv_w.shape[2]

    by_slot = lambda fn, a: lax.switch(slot, [lambda v, s=s: fn(s, v) for s in range(N_CHIPS)], a)
    w0s = by_slot(_pack_own_shard, dn_w_in[0].astype(BF16)).reshape(2, half_r, WIN_COLS)
    convs = jnp.pad(dn_conv_w[0], ((0, 8 - DN_CONV), (0, 0))).reshape(8, 2, conv_cols // 2).transpose(1, 0, 2)
    c_idx = jnp.reshape(cc, (1,)).astype(jnp.int32)
    j_idx = jnp.reshape(slot, (1,)).astype(jnp.int32)
    own_a = [w0s, convs]
    lands_a = [lax.dynamic_update_slice(lax.empty((N_CHIPS,) + o.shape, o.dtype), o[None], (slot, 0, 0, 0)) for o in own_a]
    send_a, recv_a, own_a, lands_a, token_a = _exchange_start(own_a, lands_a, _gather_plan, name="gather_start")
    zero_a = token_a[0, 0]
    M["dn_w_in"] = m_dn_w_in + zero_a
    V["dn_w_in"] = v_dn_w_in + zero_a
    own_b = [(sb_w_in[0] + zero_a).astype(BF16).reshape(2, half_r, SB_PROJ // N_CHIPS), (w_out + zero_a).astype(BF16),
             (mem_w_kv + zero_a).astype(BF16)]
    lands_b = [lax.dynamic_update_slice(lax.empty((N_CHIPS,) + o.shape, o.dtype), o[None], (slot, 0, 0, 0)) for o in own_b]
    view0 = (1, 2, half_r, P0_SHARD)
    _, lands_a = _exchange_wait(own_a, lands_a, send_a, recv_a,
                                [M["dn_w_in"].reshape(view0), V["dn_w_in"].reshape(view0)] + own_b + lands_b,
                                _gather_plan, name="gather_wait")
    (g0, gconv), own_b, lands_b = lax.optimization_barrier((_forward_halves(lands_a, name="gather_forward"), own_b, lands_b))
    late = {}
    late[1] = _exchange_start(own_b[1:], lands_b[1:], _gather_plan, name="gather_late1_start")
    (w1s, land1), _ = lax.optimization_barrier(((own_b[0], lands_b[0]), late[1][4]))
    late[2] = _exchange_start([w1s], [land1], _gather_plan, name="gather_late2_start")
    token_b = late[2][4]

    def late_weights(after):
        got = []
        for i in (1, 2):
            send, recv, srcs, lands, _ = late[i]
            _, lands = _exchange_wait(srcs, lands, send, recv, after, _gather_plan, name=f"gather_late{i}_wait")
            got += _forward_halves(lands, name=f"gather_late{i}_forward")
            after = got[-1]
        gout, gkv, g1 = got
        return gkv, gout, g1

    rs = {}

    def scatter_start(tag, xs, from_sib=None):
        if from_sib is None:
            from_sib = _swap_halves(xs, name=f"rs{tag}_swap")
        ps = [_add_halves(a, b, c_idx, name=f"rs{tag}_add{t}") for t, (a, b) in enumerate(zip(xs, from_sib))]
        rcv = [lax.empty((3,) + p.shape[1:], p.dtype) for p in ps]
        send, recv, ps, rcv, token = _exchange_start(ps, rcv, _scatter_plan, name=f"rs{tag}_scatter_start")
        rs[tag] = (ps, rcv, send, recv)
        return token

    def scatter_finish(tag, after):
        ps, rcv, send, recv = rs[tag]
        ps, rcv = _exchange_wait(ps, rcv, send, recv, after, _scatter_plan, name=f"rs{tag}_scatter_wait")
        return [_sum_slot(p, r, j_idx, name=f"rs{tag}_sum{t}") for t, (p, r) in enumerate(zip(ps, rcv))]

    def early_grads(layer, d_win, d_wout, d_wkv):
        if layer == 0:
            d_win = jnp.stack([_padded_to_window(s, d_win) for s in range(N_CHIPS)]).reshape(N_CHIPS, 2, half_r, WIN_COLS)
        xs = [d_win, d_wout.reshape(N_CHIPS, 2, -1, D_MODEL), d_wkv.reshape(N_CHIPS, 2, -1, 2 * XA_WIDTH)]
        if layer == 0:
            return scatter_start(0, xs)
        lands = [lax.empty((N_CHIPS,) + a.shape[2:], a.dtype) for a in xs]
        send, recv, xs, lands, token = _exchange_start(xs, lands, _swap_plan, per_tensor=N_CHIPS, name="rs1_swap_start")
        rs["swap1"] = (xs, lands, send, recv)
        return token

    def grads_swapped(after):
        xs, lands, send, recv = rs["swap1"]
        xs, from_sib = _exchange_wait(xs, lands, send, recv, after, _swap_plan, name="rs1_swap_wait")
        return scatter_start(1, xs, from_sib)

    w_in0 = _windows_to_padded(g0.reshape(N_CHIPS, D_MODEL, WIN_COLS))
    conv_f = gconv.transpose(2, 0, 1, 3).reshape(8, N_CHIPS * conv_cols)[:DN_CONV]

    loss_vec, grad_x, g = _local_step(
        x[0], mem[0], loss_target[0], norm_g + token_b[0, 0], mem_norm_g, xa_q_norm_g, xa_k_norm_g, w_in0, conv_f,
        dn_a_log[0], dn_dt_bias[0], dn_out_norm_g[0], sb_q_norm_g[0], sb_k_norm_g[0], late_weights, early_grads,
        grads_swapped)

    mine1 = scatter_finish(1, grad_x)
    theirs1 = _swap_with_sibling(mine1, name="rs1_join")
    big1 = [("sb_w_in", None), ("w_out", 1), ("mem_w_kv", 1)]
    big0 = [("dn_w_in", None), ("w_out", 0), ("mem_w_kv", 0)]

    out_g, out_d, out_m, out_v = {}, {}, {}, {}
    partial = {}

    def adamw_big(big, mine, theirs):
        for (n, layer), mine_g, their_g in zip(big, mine, theirs):
            layers = 1 if layer is None else 2
            view = (layers, 2) + mine_g.shape
            partial[n] = _adamw_halves(W[n].reshape(view), mine_g, their_g, M[n].reshape(view), V[n].reshape(view), c_idx,
                                       layer=layer or 0, into=partial.get(n), name=f"adamw_{n}" + ("" if layer is None else str(layer)))
        return [partial[n][0] for n, _ in big]

    done1 = lax.optimization_barrier(tuple(adamw_big(big1, mine1, theirs1)))[-1]
    mine0 = scatter_finish(0, done1)
    mine0[0] = by_slot(_unpack_own_shard, mine0[0])

    parts, _ = lax.optimization_barrier(([_rows128(g[n]) for n in _SMALL] + [_rows128(g["dn_conv_w"]), loss_vec], mine0[0]))
    red = _all_reduce_small(parts, name="all_reduce_small")
    small_rows = dict(zip(_SMALL, red))
    conv_full = red[len(_SMALL)].reshape(DN_CONV, N_CHIPS * conv_cols)
    small_rows["dn_conv_w"] = _rows128(lax.dynamic_slice_in_dim(conv_full, slot * conv_cols, conv_cols, axis=1))
    loss = red[-1][0, 0]

    adamw_big(big0, mine0, _swap_with_sibling(mine0, name="rs0_join"))
    for n, outs in partial.items():
        out_g[n], out_d[n], out_m[n], out_v[n] = [o.reshape(W[n].shape) for o in outs]
    small_names = _SMALL + ["dn_conv_w"]
    ds, nms, nvs = _adamw_parts([_rows128(W[n]) for n in small_names], [small_rows[n] for n in small_names],
                                [_rows128(M[n]) for n in small_names], [_rows128(V[n]) for n in small_names], name="adamw_small")
    for n, d, nm, nv in zip(small_names, ds, nms, nvs):
        shp = W[n].shape
        out_g[n], out_d[n], out_m[n], out_v[n] = [_unrows(r, shp) for r in (small_rows[n], d, nm, nv)]

    return (loss, grad_x[None], *[out_g[n] for n in names], *[out_d[n] for n in names], *[out_m[n] for n in names],
            *[out_v[n] for n in names])
```

```python
import math

import jax
import jax.numpy as jnp
from jax import lax
from jax.experimental import pallas as pl
from jax.experimental.pallas import tpu as pltpu

F32 = jnp.float32
BF16 = jnp.bfloat16
HI = lax.Precision.HIGHEST
MESH = pl.DeviceIdType.MESH

D_MODEL = 2048
INNER = 4096
XA_WIDTH = 1024
XA_HEADS = 4
XA_DIM = 256
MIX_WIDTH = 3072
HEAD_DIM = 128
DN_V_HEADS = 24
DN_QK_WIDTH = 1536
DN_CONV = 4
DN_PROJ = 11312
SB_PROJ = 14336
EPS = 1e-6
N_CHIPS = 4

CH = 128
LANE = 128

P0_XQ = 6144
P0_Z = 7168
P0_AB = 11264
P0 = 11776
P0_SHARD = DN_PROJ // N_CHIPS
P1_XQ = 9216
P1_Z = 10240
P1 = SB_PROJ

ADAM_LR = 0.001
ADAM_B1 = 0.9
ADAM_B2 = 0.999
ADAM_EPS = 1e-08
ADAM_WD = 0.01
ADAM_STEP = 10

VMEM_LIMIT = 48 * 1024 * 1024


def _cp(sem=None, **kw):
    return pltpu.CompilerParams(dimension_semantics=sem, vmem_limit_bytes=VMEM_LIMIT, **kw)


def _bdot(a, b, dims):
    return lax.dot_general(a.astype(BF16), b.astype(BF16), (dims, ((), ())), preferred_element_type=F32)


def _fdot(a, b, dims):
    return lax.dot_general(a, b, (dims, ((), ())), precision=HI, preferred_element_type=F32)


NN = ((1,), (0,))
NT = ((1,), (1,))
TN = ((0,), (0,))


def _sigmoid(x):
    return 1.0 / (1.0 + jnp.exp(-x))


def _softplus(x):
    return jnp.maximum(x, 0.0) + jnp.log(1.0 + jnp.exp(-jnp.abs(x)))


def _iota2(shape, axis):
    return lax.broadcasted_iota(jnp.int32, shape, axis)


MM_FULL_K = 4096
MM_BLOCK_BYTES = 4 * 1024 * 1024


def _matmul(a, b, *, ta=False, tb=False, out_dtype=F32, res=None, name, n=None, tm=None, tn=None, tk=None,
            b_spec=None, o_spec=None, o_shape=None):
    a_segs = list(a) if isinstance(a, (list, tuple)) else [a]
    b_segs = list(b) if isinstance(b, (list, tuple)) else [b]
    a0, b0 = a_segs[0], b_segs[0]
    M = a0.shape[1] if ta else a0.shape[0]
    K = a0.shape[0] if ta else sum(s.shape[1] for s in a_segs)
    if n is None:
        n = b0.shape[0] if tb else sum(s.shape[1] for s in b_segs)
    N = n
    dims = ((0,) if ta else (1,), (1,) if tb else (0,))
    has_res = res is not None
    flat = lambda v: v.reshape(-1, v.shape[-1])
    o_shape = o_shape or jax.ShapeDtypeStruct((M, N), out_dtype)

    def seg_specs(segs, tile, block, pos):
        specs, ranges, off = [], [], 0
        for s in segs:
            cnt = s.shape[1] // tile
            assert s.shape[1] % tile == 0, (name, s.shape, tile)

            def imap(*g, off=off, cnt=cnt):
                t = jnp.clip(g[pos] - off, 0, cnt - 1)
                return (g[0], t) if pos == 2 else (0, t)

            specs.append(pl.BlockSpec(block, imap))
            ranges.append((off, off + cnt))
            off += cnt
        return specs, ranges

    if K <= MM_FULL_K:
        assert len(a_segs) == 1
        tm = tm or min(M, 1024, max(256, MM_BLOCK_BYTES // (K * a0.dtype.itemsize)))
        tn = tn or min(N, 512)
        assert M % tm == 0 and N % tn == 0, (name, M, N, K, tm, tn)
        nb = len(b_segs)
        if b_spec is not None:
            b_specs, b_ranges = [b_spec], [(0, N // tn)]
        elif nb > 1:
            assert not tb
            b_specs, b_ranges = seg_specs(b_segs, tn, (K, tn), 1)
        else:
            b_specs = [pl.BlockSpec((tn, K), lambda i, j: (j, 0)) if tb else pl.BlockSpec((K, tn), lambda i, j: (0, j))]
            b_ranges = [(0, N // tn)]

        def body_full(*refs):
            a_ref, b_refs = refs[0], refs[1:1 + nb]
            r_ref = refs[1 + nb] if has_res else None
            o_ref = refs[-1]
            j = pl.program_id(1)
            for b_ref, (lo, hi) in zip(b_refs, b_ranges):
                def emit(b_ref=b_ref):
                    r = _bdot(a_ref[...], flat(b_ref[...]), dims)
                    if has_res:
                        r = r + r_ref[...]
                    o_ref[...] = r.astype(o_ref.dtype).reshape(o_ref.shape)
                if nb == 1:
                    emit()
                else:
                    pl.when(jnp.logical_and(j >= lo, j < hi))(emit)

        a_spec = pl.BlockSpec((K, tm), lambda i, j: (0, i)) if ta else pl.BlockSpec((tm, K), lambda i, j: (i, 0))
        o_spec = o_spec or pl.BlockSpec((tm, tn), lambda i, j: (i, j))
        r_spec = [pl.BlockSpec((tm, tn), lambda i, j: (i, j))] if has_res else []
        return pl.pallas_call(
            body_full, grid=(M // tm, N // tn), in_specs=[a_spec] + b_specs + r_spec, out_specs=o_spec, out_shape=o_shape,
            compiler_params=_cp(("parallel", "arbitrary")), name=name)(*([a0] + b_segs + ([res] if has_res else [])))

    assert tb and not ta and len(b_segs) == 1
    tm, tn = tm or min(M, 1024), tn or min(N, 1024)
    tk = tk or (1024 if all(s.shape[1] % 1024 == 0 for s in a_segs) else 512)
    assert M % tm == 0 and N % tn == 0 and K % tk == 0, (name, M, N, K, tm, tn, tk)
    nk = K // tk
    na = len(a_segs)
    if na > 1:
        a_specs, a_ranges = seg_specs(a_segs, tk, (tm, tk), 2)
    else:
        a_specs, a_ranges = [pl.BlockSpec((tm, tk), lambda i, j, k: (i, k))], [(0, nk)]
    b_spec = b_spec or pl.BlockSpec((tn, tk), lambda i, j, k: (j, k))

    def body(*refs):
        a_refs, b_ref = refs[:na], refs[na]
        r_ref = refs[na + 1] if has_res else None
        o_ref, acc = refs[-2], refs[-1]
        k = pl.program_id(2)

        @pl.when(k == 0)
        def _():
            acc[...] = jnp.zeros_like(acc)

        for a_ref, (lo, hi) in zip(a_refs, a_ranges):
            def emit(a_ref=a_ref):
                acc[...] += _bdot(a_ref[...], flat(b_ref[...]), dims)
            if na == 1:
                emit()
            else:
                pl.when(jnp.logical_and(k >= lo, k < hi))(emit)

        @pl.when(k == nk - 1)
        def _():
            r = acc[...]
            if has_res:
                r = r + r_ref[...]
            o_ref[...] = r.astype(o_ref.dtype).reshape(o_ref.shape)

    o_spec = o_spec or pl.BlockSpec((tm, tn), lambda i, j, k: (i, j))
    r_spec = [pl.BlockSpec((tm, tn), lambda i, j, k: (i, j))] if has_res else []
    return pl.pallas_call(
        body, grid=(M // tm, N // tn, nk), in_specs=a_specs + [b_spec] + r_spec, out_specs=o_spec, out_shape=o_shape,
        scratch_shapes=[pltpu.VMEM((tm, tn), F32)],
        compiler_params=_cp(("parallel", "parallel", "arbitrary")), name=name)(*(a_segs + [b0] + ([res] if has_res else [])))


def _rmsnorm_fwd(x, g, *, name, tm=256):
    S, Dm = x.shape
    tm = min(tm, S)

    def body(x_ref, g_ref, o_ref):
        xv = x_ref[...]
        r = lax.rsqrt(jnp.mean(xv * xv, axis=-1, keepdims=True) + EPS)
        o_ref[...] = (xv * r * g_ref[...]).astype(BF16)

    return pl.pallas_call(
        body, grid=(S // tm,), in_specs=[pl.BlockSpec((tm, Dm), lambda i: (i, 0)), pl.BlockSpec((1, Dm), lambda i: (0, 0))],
        out_specs=pl.BlockSpec((tm, Dm), lambda i: (i, 0)), out_shape=jax.ShapeDtypeStruct((S, Dm), BF16),
        compiler_params=_cp(("parallel",)), name=name)(x, g.reshape(1, Dm))


def _rmsnorm_bwd(dh, x, g, dres, *, name, tm=256):
    S, Dm = x.shape
    tm = min(tm, S)
    want_dx = dres is not None

    def body(*refs):
        if want_dx:
            dh_ref, x_ref, g_ref, dr_ref, dx_ref, dxb_ref, dg_ref = refs
        else:
            dh_ref, x_ref, g_ref, dg_ref = refs
        i = pl.program_id(0)
        xv = x_ref[...]
        dhv = dh_ref[...]
        r = lax.rsqrt(jnp.mean(xv * xv, axis=-1, keepdims=True) + EPS)
        y = xv * r
        part = jnp.sum(dhv * y, axis=0, keepdims=True)

        @pl.when(i == 0)
        def _():
            dg_ref[...] = jnp.zeros_like(dg_ref)

        dg_ref[...] += part
        if want_dx:
            dy = dhv * g_ref[...]
            dx = dr_ref[...] + r * (dy - y * jnp.mean(dy * y, axis=-1, keepdims=True))
            dx_ref[...] = dx
            dxb_ref[...] = dx.astype(BF16)

    row = pl.BlockSpec((tm, Dm), lambda i: (i, 0))
    vec = pl.BlockSpec((1, Dm), lambda i: (0, 0))
    if want_dx:
        dx, dxb, dg = pl.pallas_call(
            body, grid=(S // tm,), in_specs=[row, row, vec, row], out_specs=(row, row, vec),
            out_shape=(jax.ShapeDtypeStruct((S, Dm), F32), jax.ShapeDtypeStruct((S, Dm), BF16),
                       jax.ShapeDtypeStruct((1, Dm), F32)),
            compiler_params=_cp(("arbitrary",)), name=name)(dh, x, g.reshape(1, Dm), dres)
        return (dx, dxb), dg
    dg = pl.pallas_call(
        body, grid=(S // tm,), in_specs=[row, row, vec], out_specs=vec,
        out_shape=jax.ShapeDtypeStruct((1, Dm), F32), compiler_params=_cp(("arbitrary",)), name=name)(dh, x, g.reshape(1, Dm))
    return None, dg


GATE_TN = XA_WIDTH
GATE_MIX_TILES = MIX_WIDTH // GATE_TN


def _gate_cat_specs(tm):
    return [pl.BlockSpec((tm, GATE_TN), lambda i, j: (i, jnp.minimum(j, GATE_MIX_TILES - 1))),
            pl.BlockSpec((tm, GATE_TN), lambda i, j: (i, 0))]


def _gate_fwd(mix, xa, proj, z_off, *, name, tm=256):
    S = mix.shape[0]
    tm = min(tm, S)
    zb = z_off // GATE_TN

    def body(m_ref, x_ref, z_ref, y_ref):
        z = z_ref[...]
        c = jnp.where(pl.program_id(1) < GATE_MIX_TILES, m_ref[...], x_ref[...])
        y_ref[...] = (c * z * _sigmoid(z)).astype(BF16)

    blk = pl.BlockSpec((tm, GATE_TN), lambda i, j: (i, j))
    return pl.pallas_call(
        body, grid=(S // tm, INNER // GATE_TN),
        in_specs=_gate_cat_specs(tm) + [pl.BlockSpec((tm, GATE_TN), lambda i, j: (i, zb + j))],
        out_specs=blk, out_shape=jax.ShapeDtypeStruct((S, INNER), BF16),
        compiler_params=_cp(("parallel", "arbitrary")), name=name)(mix, xa, proj)


def _gate_bwd(dy, mix, xa, proj, z_off, *, name, tm=256):
    S = mix.shape[0]
    tm = min(tm, S)
    zb = z_off // GATE_TN

    def body(dy_ref, m_ref, x_ref, z_ref, dc_ref, dz_ref):
        z = z_ref[...]
        sg = _sigmoid(z)
        d = dy_ref[...]
        c = jnp.where(pl.program_id(1) < GATE_MIX_TILES, m_ref[...], x_ref[...])
        dc_ref[...] = d * z * sg
        dz_ref[...] = (d * c * sg * (1.0 + z * (1.0 - sg))).astype(BF16)

    blk = pl.BlockSpec((tm, GATE_TN), lambda i, j: (i, j))
    return pl.pallas_call(
        body, grid=(S // tm, INNER // GATE_TN),
        in_specs=[blk] + _gate_cat_specs(tm) + [pl.BlockSpec((tm, GATE_TN), lambda i, j: (i, zb + j))], out_specs=(blk, blk),
        out_shape=(jax.ShapeDtypeStruct((S, INNER), F32), jax.ShapeDtypeStruct((S, INNER), BF16)),
        compiler_params=_cp(("parallel", "arbitrary")), name=name)(dy, mix, xa, proj)


def _loss_head(x, target, *, name, tm=256):
    S, Dm = x.shape
    tm = min(tm, S)

    nt = S // tm

    def body(x_ref, t_ref, dx_ref, dxb_ref, l_ref, acc):
        i = pl.program_id(0)
        e = x_ref[...] - t_ref[...]
        dx = e * (1.0 / Dm)
        dx_ref[...] = dx
        dxb_ref[...] = dx.astype(BF16)

        @pl.when(i == 0)
        def _():
            acc[...] = jnp.zeros_like(acc)

        acc[...] += jnp.sum(e * e, axis=0, keepdims=True) * (0.5 / Dm)

        @pl.when(i == nt - 1)
        def _():
            l_ref[...] = jnp.sum(acc[...], axis=1, keepdims=True) + jnp.zeros((1, LANE), F32)

    row = pl.BlockSpec((tm, Dm), lambda i: (i, 0))
    return pl.pallas_call(
        body, grid=(nt,), in_specs=[row, row], out_specs=(row, row, pl.BlockSpec((1, LANE), lambda i: (0, 0))),
        out_shape=(jax.ShapeDtypeStruct((S, Dm), F32), jax.ShapeDtypeStruct((S, Dm), BF16), jax.ShapeDtypeStruct((1, LANE), F32)),
        scratch_shapes=[pltpu.VMEM((1, Dm), F32)],
        compiler_params=_cp(("arbitrary",)), name=name)(x, target)


def _xa_norm(v, g):
    r = lax.rsqrt(jnp.mean(v * v, axis=-1, keepdims=True) + EPS)
    return v * r, r


def _xa_fwd(proj, xq_off, kv, gq, gk, *, name, tm=512):
    S = proj.shape[0]
    tm = min(tm, S)
    qb = xq_off // XA_DIM
    n_mem = kv.shape[0]
    scale = XA_DIM ** -0.5

    def body(q_ref, k_ref, v_ref, gq_ref, gk_ref, o_ref):
        qh, _ = _xa_norm(q_ref[...], None)
        kh, _ = _xa_norm(k_ref[...], None)
        qn = qh * gq_ref[...]
        kn = kh * gk_ref[...]
        s = _bdot(qn, kn, NT) * scale
        s = s - jnp.max(s, axis=-1, keepdims=True)
        p = jnp.exp(s)
        p = p / jnp.sum(p, axis=-1, keepdims=True)
        o_ref[...] = _bdot(p, v_ref[...], NN)

    vec = pl.BlockSpec((1, XA_DIM), lambda h, i: (0, 0))
    return pl.pallas_call(
        body, grid=(XA_HEADS, S // tm),
        in_specs=[pl.BlockSpec((tm, XA_DIM), lambda h, i: (i, qb + h)),
                  pl.BlockSpec((n_mem, XA_DIM), lambda h, i: (0, h)),
                  pl.BlockSpec((n_mem, XA_DIM), lambda h, i: (0, XA_HEADS + h)), vec, vec],
        out_specs=pl.BlockSpec((tm, XA_DIM), lambda h, i: (i, h)),
        out_shape=jax.ShapeDtypeStruct((S, XA_WIDTH), F32),
        compiler_params=_cp(("parallel", "parallel")), name=name)(proj, kv, kv, gq.reshape(1, XA_DIM), gk.reshape(1, XA_DIM))


def _xa_bwd(dcat, proj, xq_off, kv, gq, gk, *, name, tm=512):
    S = proj.shape[0]
    tm = min(tm, S)
    nt = S // tm
    qb = xq_off // XA_DIM
    db = MIX_WIDTH // XA_DIM
    n_mem = kv.shape[0]
    scale = XA_DIM ** -0.5

    def body(d_ref, q_ref, k_ref, v_ref, gq_ref, gk_ref, dq_ref, dk_ref, dv_ref, dgq_ref, dgk_ref, dkn_acc):
        h = pl.program_id(0)
        i = pl.program_id(1)
        q = q_ref[...]
        k = k_ref[...]
        qh, rq = _xa_norm(q, None)
        kh, rk = _xa_norm(k, None)
        gqv = gq_ref[...]
        gkv = gk_ref[...]
        qn = qh * gqv
        kn = kh * gkv
        s = _bdot(qn, kn, NT) * scale
        s = s - jnp.max(s, axis=-1, keepdims=True)
        p = jnp.exp(s)
        p = p / jnp.sum(p, axis=-1, keepdims=True)
        d = d_ref[...]
        dp = _bdot(d, v_ref[...], NT)
        ds = p * (dp - jnp.sum(dp * p, axis=-1, keepdims=True)) * scale
        dqn = _bdot(ds, kn, NN)

        @pl.when(i == 0)
        def _():
            dkn_acc[...] = jnp.zeros_like(dkn_acc)
            dv_ref[...] = jnp.zeros_like(dv_ref)

        @pl.when(jnp.logical_and(i == 0, h == 0))
        def _():
            dgq_ref[...] = jnp.zeros_like(dgq_ref)
            dgk_ref[...] = jnp.zeros_like(dgk_ref)

        dkn_acc[...] += _bdot(ds, qn, TN)
        dv_ref[...] += _bdot(p, d, TN)
        dgq_ref[...] += jnp.sum(dqn * qh, axis=0, keepdims=True)
        dy = dqn * gqv
        dq_ref[...] = (rq * (dy - qh * jnp.mean(dy * qh, axis=-1, keepdims=True))).astype(BF16)

        @pl.when(i == nt - 1)
        def _():
            dkn = dkn_acc[...]
            dgk_ref[...] += jnp.sum(dkn * kh, axis=0, keepdims=True)
            dyk = dkn * gkv
            dk_ref[...] = rk * (dyk - kh * jnp.mean(dyk * kh, axis=-1, keepdims=True))

    vec = pl.BlockSpec((1, XA_DIM), lambda h, i: (0, 0))
    kblk = pl.BlockSpec((n_mem, XA_DIM), lambda h, i: (0, h))
    vblk = pl.BlockSpec((n_mem, XA_DIM), lambda h, i: (0, XA_HEADS + h))
    dq, dk, dv, dgq, dgk = pl.pallas_call(
        body, grid=(XA_HEADS, nt),
        in_specs=[pl.BlockSpec((tm, XA_DIM), lambda h, i: (i, db + h)),
                  pl.BlockSpec((tm, XA_DIM), lambda h, i: (i, qb + h)), kblk, vblk, vec, vec],
        out_specs=(pl.BlockSpec((tm, XA_DIM), lambda h, i: (i, h)), kblk, kblk, vec, vec),
        out_shape=(jax.ShapeDtypeStruct((S, XA_WIDTH), BF16), jax.ShapeDtypeStruct((n_mem, XA_WIDTH), F32),
                   jax.ShapeDtypeStruct((n_mem, XA_WIDTH), F32), jax.ShapeDtypeStruct((1, XA_DIM), F32),
                   jax.ShapeDtypeStruct((1, XA_DIM), F32)),
        scratch_shapes=[pltpu.VMEM((n_mem, XA_DIM), F32)],
        compiler_params=_cp(("arbitrary", "arbitrary")), name=name)(
            dcat, proj, kv, kv, gq.reshape(1, XA_DIM), gk.reshape(1, XA_DIM))
    return dq, jnp.concatenate([dk, dv], axis=1), dgq, dgk


SB_TQ = 256
SB_TK = 256
SB_HEADS = 24


SB_PAIR = 2
SB_PW = SB_PAIR * HEAD_DIM


def _hdot(a, b, dims, dot=None):
    dot = dot or _bdot
    n = a.shape[0] if a.ndim == 3 else b.shape[0]
    return jnp.stack([dot(a[i] if a.ndim == 3 else a, b[i] if b.ndim == 3 else b, dims) for i in range(n)])


def _sb_tile(qi, kj, t0, s0, masked):
    z = _hdot(qi, kj, NT)
    sp = _softplus(z)
    ls = z - sp
    if not masked:
        return -sp, ls, None
    mask = (s0 + _iota2(z.shape[1:], 1)) < (t0 + _iota2(z.shape[1:], 0))
    return jnp.where(mask, -sp, 0.0), ls, mask


def _dot2(x, tri):
    hi = x.astype(BF16)
    lo = (x - hi.astype(F32)).astype(BF16)
    plain = lambda u, v, dims: lax.dot_general(u, v, (dims, ((), ())), preferred_element_type=F32)
    return _hdot(hi, tri, NN, plain) + _hdot(lo, tri, NN, plain)


def _sb_heads(ref, rows=slice(None)):
    return jnp.stack([ref[rows, hh * HEAD_DIM:(hh + 1) * HEAD_DIM] for hh in range(SB_PAIR)])


def _sb_fwd(proj, gq, gk, *, name):
    S = proj.shape[0]
    tq, tk = min(SB_TQ, S), min(SB_TK, S)
    nq = S // tq
    scale = HEAD_DIM ** -0.5

    def body(q_ref, k_ref, v_ref, gq_ref, gk_ref, o_ref, tot_ref, qn_s, kn_s, v_s):
        q = _sb_heads(q_ref)
        k = _sb_heads(k_ref)
        qn_s[...] = (q * lax.rsqrt(jnp.mean(q * q, axis=-1, keepdims=True) + EPS) * (gq_ref[...] * scale)).astype(BF16)
        kn_s[...] = (k * lax.rsqrt(jnp.mean(k * k, axis=-1, keepdims=True) + EPS) * gk_ref[...]).astype(BF16)
        v_s[...] = _sb_heads(v_ref).astype(BF16)
        after = (_iota2((tk, tk), 0) > _iota2((tk, tk), 1)).astype(BF16)

        def qblock(i, _):
            rows = pl.ds(pl.multiple_of(i * tq, tq), tq)
            qi = qn_s[:, rows, :]
            jd = (i * tq) // tk

            def tile(j, acc, run, masked):
                cols = pl.ds(pl.multiple_of(j * tk, tk), tk)
                lr, ls, mask = _sb_tile(qi, kn_s[:, cols, :], i * tq, j * tk, masked)
                later = _dot2(lr, after) + run
                a = jnp.exp(ls + later)
                if masked:
                    a = jnp.where(mask, a, 0.0)
                acc = acc + _hdot(a, v_s[:, cols, :], NN)
                return acc, run + jnp.sum(lr, axis=-1, keepdims=True)

            acc, run = tile(jd, jnp.zeros((SB_PAIR, tq, HEAD_DIM), F32), jnp.zeros((SB_PAIR, tq, 1), F32), True)
            acc, run = lax.fori_loop(0, jd, lambda jj, c: tile(jd - 1 - jj, c[0], c[1], False), (acc, run))
            tot = run + jnp.zeros((SB_PAIR, tq, HEAD_DIM), F32)
            for hh in range(SB_PAIR):
                o_ref[rows, hh * HEAD_DIM:(hh + 1) * HEAD_DIM] = acc[hh]
                tot_ref[rows, hh * HEAD_DIM:(hh + 1) * HEAD_DIM] = tot[hh]
            return 0

        lax.fori_loop(0, nq, qblock, 0)

    npair = SB_HEADS // SB_PAIR
    vec = pl.BlockSpec((1, HEAD_DIM), lambda h: (0, 0))
    hb = lambda off: pl.BlockSpec((S, SB_PW), lambda h: (0, off + h), pipeline_mode=pl.Buffered(1))
    return pl.pallas_call(
        body, grid=(npair,), in_specs=[hb(0), hb(npair), hb(2 * npair), vec, vec],
        out_specs=(hb(0), hb(0)), out_shape=(jax.ShapeDtypeStruct((S, MIX_WIDTH), F32),) * 2,
        scratch_shapes=[pltpu.VMEM((SB_PAIR, S, HEAD_DIM), BF16)] * 3,
        compiler_params=_cp(("parallel",)), name=name)(proj, proj, proj, gq.reshape(1, HEAD_DIM), gk.reshape(1, HEAD_DIM))


def _sb_bwd(dmix, tot, proj, gq, gk, *, name):
    S = proj.shape[0]
    tq, tk = min(SB_TQ, S), min(SB_TK, S)
    nq = S // tq
    scale = HEAD_DIM ** -0.5

    def body(do_ref, o_ref, q_ref, k_ref, v_ref, gq_ref, gk_ref, dq_ref, dk_ref, dv_ref, dgq_ref, dgk_ref,
             qn_s, kn_s, v_s, dkn_s, dqn_s, dv_s):
        h = pl.program_id(0)
        q = _sb_heads(q_ref)
        k = _sb_heads(k_ref)
        rq = lax.rsqrt(jnp.mean(q * q, axis=-1, keepdims=True) + EPS)
        rk = lax.rsqrt(jnp.mean(k * k, axis=-1, keepdims=True) + EPS)
        gqv = gq_ref[...]
        gkv = gk_ref[...]
        qn_s[...] = (q * rq * (gqv * scale)).astype(BF16)
        kn_s[...] = (k * rk * gkv).astype(BF16)
        v_s[...] = _sb_heads(v_ref).astype(BF16)
        dkn_s[...] = jnp.zeros_like(dkn_s)
        dv_s[...] = jnp.zeros_like(dv_s)
        r_i = _iota2((tk, tk), 0)
        c_i = _iota2((tk, tk), 1)
        upto = (r_i <= c_i).astype(BF16)
        before = (r_i < c_i).astype(BF16)

        def qblock(i, _):
            rows = pl.ds(pl.multiple_of(i * tq, tq), tq)
            qi = qn_s[:, rows, :]
            doi = _sb_heads(do_ref, rows).astype(BF16)
            tot_i = jnp.max(_sb_heads(o_ref, rows), axis=-1, keepdims=True)
            jd = (i * tq) // tk

            def tile(j, dqn, run, run_b, masked):
                cols = pl.ds(pl.multiple_of(j * tk, tk), tk)
                kj = kn_s[:, cols, :]
                lr, ls, mask = _sb_tile(qi, kj, i * tq, j * tk, masked)
                later = tot_i - (_dot2(lr, upto) + run)
                a = jnp.exp(ls + later)
                if masked:
                    a = jnp.where(mask, a, 0.0)
                b = _hdot(doi, v_s[:, cols, :], NT) * a
                cum = _hdot(b, before, NN) + run_b
                beta = jnp.exp(ls)
                dz = b * (1.0 - beta) - cum * beta
                if masked:
                    dz = jnp.where(mask, dz, 0.0)
                dzb = dz.astype(BF16)
                dv_s[:, cols, :] += _hdot(a, doi, TN)
                dkn_s[:, cols, :] += _hdot(dzb, qi, TN)
                dqn = dqn + _hdot(dzb, kj, NN)
                return dqn, run + jnp.sum(lr, axis=-1, keepdims=True), run_b + jnp.sum(b, axis=-1, keepdims=True)

            zero1 = jnp.zeros((SB_PAIR, tq, 1), F32)
            carry = lax.fori_loop(0, jd, lambda j, c: tile(j, c[0], c[1], c[2], False),
                                  (jnp.zeros((SB_PAIR, tq, HEAD_DIM), F32), zero1, zero1))
            dqn, _, _ = tile(jd, carry[0], carry[1], carry[2], True)
            dqn_s[:, rows, :] = dqn * scale
            return 0

        lax.fori_loop(0, nq, qblock, 0)

        @pl.when(h == 0)
        def _():
            dgq_ref[...] = jnp.zeros_like(dgq_ref)
            dgk_ref[...] = jnp.zeros_like(dgk_ref)

        heads_sum = lambda z: jnp.sum(jnp.sum(z, axis=1, keepdims=True), axis=0)
        dqn = dqn_s[...]
        qh = q * rq
        dgq_ref[...] += heads_sum(dqn * qh)
        dy = dqn * gqv
        dq = (rq * (dy - qh * jnp.mean(dy * qh, axis=-1, keepdims=True))).astype(BF16)
        dkn = dkn_s[...]
        kh = k * rk
        dgk_ref[...] += heads_sum(dkn * kh)
        dyk = dkn * gkv
        dk = (rk * (dyk - kh * jnp.mean(dyk * kh, axis=-1, keepdims=True))).astype(BF16)
        dv = dv_s[...].astype(BF16)
        for hh in range(SB_PAIR):
            lanes = slice(hh * HEAD_DIM, (hh + 1) * HEAD_DIM)
            dq_ref[:, lanes] = dq[hh]
            dk_ref[:, lanes] = dk[hh]
            dv_ref[:, lanes] = dv[hh]

    npair = SB_HEADS // SB_PAIR
    vec = pl.BlockSpec((1, HEAD_DIM), lambda h: (0, 0))
    hb = lambda off: pl.BlockSpec((S, SB_PW), lambda h: (0, off + h), pipeline_mode=pl.Buffered(1))
    dq, dk, dv, dgq, dgk = pl.pallas_call(
        body, grid=(npair,),
        in_specs=[hb(0), hb(0), hb(0), hb(npair), hb(2 * npair), vec, vec],
        out_specs=(hb(0), hb(0), hb(0), vec, vec),
        out_shape=(jax.ShapeDtypeStruct((S, MIX_WIDTH), BF16),) * 3 + (jax.ShapeDtypeStruct((1, HEAD_DIM), F32),) * 2,
        scratch_shapes=[pltpu.VMEM((SB_PAIR, S, HEAD_DIM), BF16)] * 3 + [pltpu.VMEM((SB_PAIR, S, HEAD_DIM), F32)] * 3,
        compiler_params=_cp(("arbitrary",)), name=name)(
            dmix, tot, proj, proj, proj, gq.reshape(1, HEAD_DIM), gk.reshape(1, HEAD_DIM))
    return [dq, dk, dv], dgq, dgk


def _shift_down(x, k):
    if k == 0:
        return x
    r = pltpu.roll(x, k, 0)
    return jnp.where(_iota2(x.shape, 0) >= k, r, 0.0)


def _shift_up(x, k):
    if k == 0:
        return x
    n = x.shape[0]
    r = pltpu.roll(x, n - k, 0)
    return jnp.where(_iota2(x.shape, 0) < n - k, r, 0.0)


def _conv(x, w):
    c = w[DN_CONV - 1] * x
    for k in range(1, DN_CONV):
        c = c + w[DN_CONV - 1 - k] * _shift_down(x, k)
    return c


def _dn_pre_fwd(proj, conv_w, col0, ncols, *, l2, scale, name):
    S = proj.shape[0]
    cb = col0 // HEAD_DIM

    def body(x_ref, w_ref, o_ref):
        c = _conv(x_ref[...], [w_ref[k:k + 1, :] for k in range(DN_CONV)])
        a = c * _sigmoid(c)
        if l2:
            a = a * (lax.rsqrt(jnp.sum(a * a, axis=-1, keepdims=True) + EPS) * scale)
        o_ref[...] = a

    return pl.pallas_call(
        body, grid=(ncols // HEAD_DIM,),
        in_specs=[pl.BlockSpec((S, HEAD_DIM), lambda j: (0, cb + j)), pl.BlockSpec((DN_CONV, HEAD_DIM), lambda j: (0, cb + j))],
        out_specs=pl.BlockSpec((S, HEAD_DIM), lambda j: (0, j)), out_shape=jax.ShapeDtypeStruct((S, ncols), F32),
        compiler_params=_cp(("parallel",)), name=name)(proj, conv_w)


def _dn_pre_bwd(dout, proj, conv_w, col0, ncols, *, l2, scale, name):
    S = proj.shape[0]
    cb = col0 // HEAD_DIM
    dw_in = HEAD_DIM

    def body(d_ref, x_ref, w_ref, dx_ref, dw_ref):
        x = x_ref[...]
        w = [w_ref[k:k + 1, :] for k in range(DN_CONV)]
        c = _conv(x, w)
        sg = _sigmoid(c)
        a = c * sg
        d = d_ref[...]
        if l2:
            r = lax.rsqrt(jnp.sum(a * a, axis=-1, keepdims=True) + EPS)
            y = a * r
            d = d * scale
            d = r * (d - y * jnp.sum(d * y, axis=-1, keepdims=True))
        dc = d * sg * (1.0 + c * (1.0 - sg))
        dx = w[DN_CONV - 1] * dc
        for k in range(1, DN_CONV):
            dx = dx + w[DN_CONV - 1 - k] * _shift_up(dc, k)
        dx_ref[...] = dx.astype(BF16)
        for k in range(DN_CONV):
            dw_ref[3 - k:4 - k, :] = jnp.sum(dc * _shift_down(x, k), axis=0, keepdims=True)

    return pl.pallas_call(
        body, grid=(ncols // HEAD_DIM,),
        in_specs=[pl.BlockSpec((S, dw_in), lambda j: (0, j)), pl.BlockSpec((S, HEAD_DIM), lambda j: (0, cb + j)),
                  pl.BlockSpec((DN_CONV, HEAD_DIM), lambda j: (0, cb + j))],
        out_specs=(pl.BlockSpec((S, HEAD_DIM), lambda j: (0, j)), pl.BlockSpec((DN_CONV, HEAD_DIM), lambda j: (0, j))),
        out_shape=(jax.ShapeDtypeStruct((S, ncols), BF16), jax.ShapeDtypeStruct((DN_CONV, ncols), F32)),
        compiler_params=_cp(("parallel",)), name=name)(dout, proj, conv_w)


def _dn_ab_fwd(proj, a_log, dt_bias, *, name, tm=512):
    S = proj.shape[0]
    tm = min(tm, S)
    ab = P0_AB // LANE

    def body(a_ref, b_ref, al_ref, dt_ref, g_ref, be_ref):
        g_ref[...] = -jnp.exp(al_ref[...]) * _softplus(a_ref[...] + dt_ref[...])
        be_ref[...] = _sigmoid(b_ref[...])

    vec = pl.BlockSpec((1, LANE), lambda i: (0, 0))
    out = pl.BlockSpec((tm, LANE), lambda i: (i, 0))
    return pl.pallas_call(
        body, grid=(S // tm,),
        in_specs=[pl.BlockSpec((tm, LANE), lambda i: (i, ab)), pl.BlockSpec((tm, LANE), lambda i: (i, ab + 1)), vec, vec],
        out_specs=(out, out), out_shape=(jax.ShapeDtypeStruct((S, LANE), F32),) * 2,
        compiler_params=_cp(("parallel",)), name=name)(proj, proj, a_log, dt_bias)


def _dn_ab_bwd(dg, dbeta, proj, a_log, dt_bias, *, name, tm=512):
    S = proj.shape[0]
    tm = min(tm, S)
    ab = P0_AB // LANE

    def body(dg_ref, db_ref, a_ref, b_ref, al_ref, dt_ref, dab_ref, dal_ref, ddt_ref):
        i = pl.program_id(0)
        ea = jnp.exp(al_ref[...])
        u = a_ref[...] + dt_ref[...]
        dgv = dg_ref[...]
        da = dgv * (-ea) * _sigmoid(u)
        be = _sigmoid(b_ref[...])
        dab_ref[:, 0:LANE] = da.astype(BF16)
        dab_ref[:, LANE:2 * LANE] = (db_ref[...] * be * (1.0 - be)).astype(BF16)
        dab_ref[:, 2 * LANE:] = jnp.zeros((tm, 2 * LANE), BF16)

        @pl.when(i == 0)
        def _():
            dal_ref[...] = jnp.zeros_like(dal_ref)
            ddt_ref[...] = jnp.zeros_like(ddt_ref)

        dal_ref[...] += jnp.sum(dgv * (-ea) * _softplus(u), axis=0, keepdims=True)
        ddt_ref[...] += jnp.sum(da, axis=0, keepdims=True)

    vec = pl.BlockSpec((1, LANE), lambda i: (0, 0))
    row = pl.BlockSpec((tm, LANE), lambda i: (i, 0))
    return pl.pallas_call(
        body, grid=(S // tm,),
        in_specs=[row, row, pl.BlockSpec((tm, LANE), lambda i: (i, ab)), pl.BlockSpec((tm, LANE), lambda i: (i, ab + 1)), vec, vec],
        out_specs=(pl.BlockSpec((tm, 4 * LANE), lambda i: (i, 0)), vec, vec),
        out_shape=(jax.ShapeDtypeStruct((S, 4 * LANE), BF16), jax.ShapeDtypeStruct((1, LANE), F32),
                   jax.ShapeDtypeStruct((1, LANE), F32)),
        compiler_params=_cp(("arbitrary",)), name=name)(dg, dbeta, proj, proj, a_log, dt_bias)


def _dot3(a, b):
    ah = a.astype(BF16)
    al = (a - ah.astype(F32)).astype(BF16)
    bh = b.astype(BF16)
    bl = (b - bh.astype(F32)).astype(BF16)
    d = lambda u, v: lax.dot_general(u, v, (NN, ((), ())), preferred_element_type=F32)
    return d(ah, bh) + (d(ah, bl) + d(al, bh))


DN_PAIR = 4
DN_QK = DN_PAIR // 2


def _dn_qk_heads(ref, rows):
    return jnp.stack([ref[rows, (hh // 2) * HEAD_DIM:(hh // 2 + 1) * HEAD_DIM] for hh in range(DN_PAIR)])


def _dn_big(shape, imap):
    return pl.BlockSpec(shape, imap, pipeline_mode=pl.Buffered(1))


_pdot = _hdot


def _tri_inverse(a):
    eye = (_iota2((CH, CH), 0) == _iota2((CH, CH), 1)).astype(F32)
    d3 = lambda u, v: jnp.stack([_dot3(u[i], v[i]) for i in range(DN_PAIR)])
    t = eye - a
    x = d3(a, a)
    n = 2
    while True:
        t = t + d3(t, x)
        n *= 2
        if n >= CH:
            break
        x = d3(x, x)
    return t


def _pick_col(m, n):
    return jnp.sum(jnp.where(_iota2(m.shape, 2) == n, m, 0.0), axis=2, keepdims=True)


def _dn_chunk_common(kk, qk, gc_c, gc_r, be_c):
    r_i = _iota2((CH, CH), 0)
    c_i = _iota2((CH, CH), 1)
    incl = r_i >= c_i
    strict = r_i > c_i
    dec = jnp.exp(jnp.where(incl, gc_c - gc_r, -1e30))
    e = jnp.exp(gc_c)
    gl = jnp.sum(jnp.where(_iota2((1, CH), 1) == CH - 1, gc_r, 0.0), axis=-1, keepdims=True)
    kds = jnp.exp(gl - gc_c)
    cd = jnp.exp(gl)
    a = jnp.where(strict, be_c * kk * dec, 0.0)
    p = qk * dec
    return dict(incl=incl, strict=strict, dec=dec, e=e, kds=kds, cd=cd, kk=kk, a=a, qk=qk, p=p)


def _dn_decay_tables(g_ref, b_ref, gcr, gcc, bcc):
    r_i = _iota2((CH, CH), 0)
    c_i = _iota2((CH, CH), 1)
    lc = (r_i >= c_i).astype(F32)
    eye = (r_i == c_i).astype(F32)
    for hh in range(DN_PAIR):
        g_rows_v = g_ref[hh]
        gcr[hh] = _fdot(g_rows_v, lc, NT)
        gcc[hh] = _fdot(lc, g_rows_v, NT)
        bcc[hh] = _fdot(eye, b_ref[hh], NT)
    return lc


def _dn_core_fwd(qn, kn, vc, g_rows, b_rows, out_g, *, name):
    S = qn.shape[0]
    nc = S // CH

    def body(q_ref, k_ref, v_ref, g_ref, b_ref, og_ref, o_ref, st_ref, t_ref, gcr, gcc, bcc):
        _dn_decay_tables(g_ref, b_ref, gcr, gcc, bcc)
        ogv = og_ref[...]

        def chunk(n, states):
            rows = pl.ds(pl.multiple_of(n * CH, CH), CH)
            q = _dn_qk_heads(q_ref, rows)
            k = _dn_qk_heads(k_ref, rows)
            kk = _pdot(k, k, NT)
            qk = _pdot(q, k, NT)
            v = jnp.stack([v_ref[rows, hh * HEAD_DIM:(hh + 1) * HEAD_DIM] for hh in range(DN_PAIR)])
            gc_c = _pick_col(gcc[...], n)
            be_c = _pick_col(bcc[...], n)
            gc_r = gcr[:, pl.ds(n, 1), :]
            c = _dn_chunk_common(kk, qk, gc_c, gc_r, be_c)
            t = _tri_inverse(c["a"])
            u0 = _pdot(t, be_c * v, NN)
            w = _pdot(t, (be_c * c["e"]) * k, NN)
            u = u0 - _pdot(w, states, NN)
            o = _pdot(c["e"] * q, states, NN) + _pdot(c["p"], u, NN)
            on = o * lax.rsqrt(jnp.mean(o * o, axis=-1, keepdims=True) + EPS) * ogv
            for hh in range(DN_PAIR):
                st_ref[hh, n] = states[hh]
                t_ref[hh, n] = t[hh]
                o_ref[rows, hh * HEAD_DIM:(hh + 1) * HEAD_DIM] = on[hh]
            return c["cd"] * states + _pdot(c["kds"] * k, u, TN)

        lax.fori_loop(0, nc, chunk, jnp.zeros((DN_PAIR, HEAD_DIM, HEAD_DIM), F32))

    qk_spec = pl.BlockSpec((S, DN_QK * HEAD_DIM), lambda h: (0, h))
    v_in = pl.BlockSpec((S, DN_PAIR * HEAD_DIM), lambda h: (0, h))
    v_spec = _dn_big((S, DN_PAIR * HEAD_DIM), lambda h: (0, h))
    rows_spec = pl.BlockSpec((DN_PAIR, LANE, CH), lambda h: (h, 0, 0))
    return pl.pallas_call(
        body, grid=(DN_V_HEADS // DN_PAIR,),
        in_specs=[qk_spec, qk_spec, v_in, rows_spec, rows_spec, pl.BlockSpec((1, HEAD_DIM), lambda h: (0, 0))],
        out_specs=(v_spec, _dn_big((DN_PAIR, nc, HEAD_DIM, HEAD_DIM), lambda h: (h, 0, 0, 0)),
                   _dn_big((DN_PAIR, nc, CH, CH), lambda h: (h, 0, 0, 0))),
        out_shape=(jax.ShapeDtypeStruct((S, MIX_WIDTH), F32), jax.ShapeDtypeStruct((DN_V_HEADS, nc, HEAD_DIM, HEAD_DIM), F32),
                   jax.ShapeDtypeStruct((DN_V_HEADS, nc, CH, CH), F32)),
        scratch_shapes=[pltpu.VMEM((DN_PAIR, LANE, CH), F32), pltpu.VMEM((DN_PAIR, CH, LANE), F32),
                        pltpu.VMEM((DN_PAIR, CH, LANE), F32)],
        compiler_params=_cp(("parallel",)), name=name)(qn, kn, vc, g_rows, b_rows, out_g.reshape(1, HEAD_DIM))


def _dn_chunk_bwd(q, k, v, kk, qk, state, t, gc_c, gc_r, be_c, don, ogv, ds_next):
    ones = jnp.ones((CH, LANE), F32)
    last_row = _iota2((CH, 1), 0) == CH - 1
    rowsum = lambda z: jnp.sum(z, axis=-1, keepdims=True)
    colsum = lambda z: jnp.sum(z, axis=-2, keepdims=True)
    c = _dn_chunk_common(kk, qk, gc_c, gc_r, be_c)
    e, kds, cd, dec, a, p = c["e"], c["kds"], c["cd"], c["dec"], c["a"], c["p"]
    vb = be_c * v
    kbe = (be_c * e) * k
    u0 = _pdot(t, vb, NN)
    w = _pdot(t, kbe, NN)
    u = u0 - _pdot(w, state, NN)
    qd = e * q
    kd = kds * k
    o = _pdot(qd, state, NN) + _pdot(p, u, NN)
    r = lax.rsqrt(jnp.mean(o * o, axis=-1, keepdims=True) + EPS)
    y = o * r
    dog = colsum(don * y)
    dy = don * ogv
    d_o = r * (dy - y * jnp.mean(dy * y, axis=-1, keepdims=True))
    du = _pdot(p, d_o, TN) + _pdot(kd, ds_next, NN)
    dqd = _pdot(d_o, state, NT)
    dstate = _pdot(qd, d_o, TN) + cd * ds_next - _pdot(w, du, TN)
    dcd = colsum(rowsum(ds_next * state))
    dkd = _pdot(u, ds_next, NT)
    dw = -_pdot(du, state, NT)
    dvb = _pdot(t, du, TN)
    dkbe = _pdot(t, dw, TN)
    da = -jnp.where(c["strict"], _pdot(dvb, u0, NT) + _pdot(dkbe, w, NT), 0.0)
    dp = jnp.where(c["incl"], _pdot(d_o, u, NT), 0.0)
    gmat = da * a + dp * p
    dad = da * dec
    x = be_c * dad
    dpd = dp * dec
    dk = _pdot(x, k, NN) + _pdot(x, k, TN) + _pdot(dpd, q, TN)
    dq = _pdot(dpd, k, NN) + e * dqd
    dbe = rowsum(dad * c["kk"])
    dgc = rowsum(gmat) + rowsum(dqd * q) * e
    rk = rowsum(dkd * k) * kds
    dk = dk + kds * dkd
    dgc = dgc - rk
    dgl = colsum(rk) + dcd * cd
    sk = rowsum(dkbe * k)
    dk = dk + (be_c * e) * dkbe
    dbe = dbe + sk * e + rowsum(dvb * v)
    dgc = dgc + sk * be_c * e
    dgc = dgc + jnp.where(last_row, dgl, 0.0)
    dgc = dgc - _pdot(gmat, ones, TN, dot=_fdot)
    return dq, dk, be_c * dvb, dgc, dbe, dog, dstate


def _dn_core_bwd(dmix, qn, kn, vc, g_rows, b_rows, out_g, states, tinv, *, name):
    S = qn.shape[0]
    nc = S // CH

    def body(do_ref, q_ref, k_ref, v_ref, g_ref, b_ref, og_ref, st_ref, t_ref,
             dq_ref, dk_ref, dv_ref, dg_ref, db_ref, dog_ref, gcr, gcc, bcc, dgc_acc):
        h = pl.program_id(0)
        lc = _dn_decay_tables(g_ref, b_ref, gcr, gcc, bcc)
        ogv = og_ref[...]
        dgc_acc[...] = jnp.zeros_like(dgc_acc)
        db_ref[...] = jnp.zeros_like(db_ref)
        lane_n = _iota2((CH, LANE), 1)

        @pl.when(h == 0)
        def _():
            dog_ref[...] = jnp.zeros_like(dog_ref)

        def chunk(m, carry):
            ds_nexts, dog = carry
            n = nc - 1 - m
            rows = pl.ds(pl.multiple_of(n * CH, CH), CH)
            q = _dn_qk_heads(q_ref, rows)
            k = _dn_qk_heads(k_ref, rows)
            kk = _pdot(k, k, NT)
            qk = _pdot(q, k, NT)
            heads = lambda ref: jnp.stack([ref[rows, hh * HEAD_DIM:(hh + 1) * HEAD_DIM] for hh in range(DN_PAIR)])
            state = jnp.stack([st_ref[hh, n] for hh in range(DN_PAIR)])
            t = jnp.stack([t_ref[hh, n] for hh in range(DN_PAIR)])
            dq, dk, dv, dgc, dbe, dog_h, dstate = _dn_chunk_bwd(
                q, k, heads(v_ref), kk, qk, state, t, _pick_col(gcc[...], n), gcr[:, pl.ds(n, 1), :],
                _pick_col(bcc[...], n), heads(do_ref), ogv, ds_nexts)
            for hh in range(DN_PAIR):
                dv_ref[rows, hh * HEAD_DIM:(hh + 1) * HEAD_DIM] = dv[hh]
            dgc_acc[...] = jnp.where(lane_n == n, dgc, dgc_acc[...])
            db_ref[...] = jnp.where(lane_n == n, dbe, db_ref[...])
            for i in range(DN_QK):
                dq_ref[rows, i * HEAD_DIM:(i + 1) * HEAD_DIM] = dq[2 * i] + dq[2 * i + 1]
                dk_ref[rows, i * HEAD_DIM:(i + 1) * HEAD_DIM] = dk[2 * i] + dk[2 * i + 1]
            return dstate, dog + jnp.sum(dog_h, axis=0)

        _, dog = lax.fori_loop(0, nc, chunk, (jnp.zeros((DN_PAIR, HEAD_DIM, HEAD_DIM), F32), jnp.zeros((1, HEAD_DIM), F32)))
        dog_ref[...] += dog
        for hh in range(DN_PAIR):
            dg_ref[hh] = _fdot(lc, dgc_acc[hh], TN)

    qk_spec = _dn_big((S, DN_QK * HEAD_DIM), lambda h: (0, h))
    v_spec = _dn_big((S, DN_PAIR * HEAD_DIM), lambda h: (0, h))
    rows_spec = pl.BlockSpec((DN_PAIR, LANE, CH), lambda h: (h, 0, 0))
    cols_spec = pl.BlockSpec((DN_PAIR, CH, LANE), lambda h: (h, 0, 0))
    vec = pl.BlockSpec((1, HEAD_DIM), lambda h: (0, 0))
    qk_out = jax.ShapeDtypeStruct((S, DN_QK_WIDTH), F32)
    return pl.pallas_call(
        body, grid=(DN_V_HEADS // DN_PAIR,),
        in_specs=[v_spec, qk_spec, qk_spec, v_spec, rows_spec, rows_spec, vec,
                  _dn_big((DN_PAIR, nc, HEAD_DIM, HEAD_DIM), lambda h: (h, 0, 0, 0)),
                  _dn_big((DN_PAIR, nc, CH, CH), lambda h: (h, 0, 0, 0))],
        out_specs=(qk_spec, qk_spec, v_spec, cols_spec, cols_spec, vec),
        out_shape=(qk_out, qk_out, jax.ShapeDtypeStruct((S, MIX_WIDTH), F32), jax.ShapeDtypeStruct((DN_V_HEADS, CH, LANE), F32),
                   jax.ShapeDtypeStruct((DN_V_HEADS, CH, LANE), F32), jax.ShapeDtypeStruct((1, HEAD_DIM), F32)),
        scratch_shapes=[pltpu.VMEM((DN_PAIR, LANE, CH), F32), pltpu.VMEM((DN_PAIR, CH, LANE), F32),
                        pltpu.VMEM((DN_PAIR, CH, LANE), F32), pltpu.VMEM((DN_PAIR, CH, LANE), F32)],
        compiler_params=_cp(("arbitrary",)), name=name)(
            dmix, qn, kn, vc, g_rows, b_rows, out_g.reshape(1, HEAD_DIM), states, tinv)


def _rows_form(x, nc):
    t = x[:, :DN_V_HEADS].T.reshape(DN_V_HEADS, nc, CH)
    return jnp.pad(t, ((0, 0), (0, LANE - nc), (0, 0)))


def _cols_to_nat(x, nc):
    t = jnp.transpose(x[:, :, :nc], (2, 1, 0)).reshape(nc * CH, DN_V_HEADS)
    return jnp.pad(t, ((0, 0), (0, LANE - DN_V_HEADS)))


_C_QKV = 2 * DN_QK_WIDTH + MIX_WIDTH


def _padded_pieces(lo, hi):
    a0, b0, x0 = _C_QKV, _C_QKV + DN_V_HEADS, _C_QKV + 2 * DN_V_HEADS
    out = []
    for t0, t1, shift in ((0, a0, 0), (a0, b0, P0_AB - a0), (b0, x0, P0_AB + LANE - b0), (x0, DN_PROJ, a0 - x0)):
        s, e = max(lo, t0), min(hi, t1)
        if s < e:
            out.append((s + shift, e + shift))
    return out


def _shard_pieces(s):
    return _padded_pieces(s * P0_SHARD, (s + 1) * P0_SHARD)


def _shard_runs(s):
    return [(lo // LANE, -(-hi // LANE)) for lo, hi in _shard_pieces(s)]


WIN_COLS = LANE * max(sum(b - a for a, b in _shard_runs(s)) for s in range(N_CHIPS))


def _pack_own_shard(s, shard):
    zeros = lambda n: jnp.zeros((shard.shape[0], n), shard.dtype)
    out, t = [], 0
    for (lo, hi), (b0, b1) in zip(_shard_pieces(s), _shard_runs(s)):
        out += [zeros(lo - b0 * LANE), shard[:, t:t + hi - lo], zeros(b1 * LANE - hi)]
        t += hi - lo
    out.append(zeros(WIN_COLS - LANE * sum(b - a for a, b in _shard_runs(s))))
    return jnp.concatenate([o for o in out if o.shape[1]], axis=1)


def _unpack_own_shard(s, win):
    out, off = [], 0
    for (lo, hi), (b0, b1) in zip(_shard_pieces(s), _shard_runs(s)):
        start = off + lo - b0 * LANE
        out.append(win[:, start:start + hi - lo])
        off += (b1 - b0) * LANE
    return jnp.concatenate(out, axis=1)


def _windows_to_padded(wins):
    src = {}
    for s in range(N_CHIPS):
        off = 0
        for b0, b1 in _shard_runs(s):
            for b in range(b0, b1):
                src.setdefault(b, []).append((s, off + (b - b0) * LANE))
            off += (b1 - b0) * LANE
    out, b = [], 0
    while b < P0 // LANE:
        if b not in src:
            out.append(jnp.zeros((wins.shape[1], LANE), wins.dtype))
            b += 1
        elif len(src[b]) > 1:
            out.append(sum(wins[s][:, o:o + LANE] for s, o in src[b]))
            b += 1
        else:
            (s, o), n = src[b][0], 1
            while src.get(b + n) == [(s, o + n * LANE)]:
                n += 1
            out.append(wins[s][:, o:o + n * LANE])
            b += n
    return jnp.concatenate(out, axis=1)


def _padded_to_window(s, g):
    out = [g[:, b0 * LANE:b1 * LANE] for b0, b1 in _shard_runs(s)]
    rest = WIN_COLS - sum(o.shape[1] for o in out)
    return jnp.concatenate(out + ([jnp.zeros((g.shape[0], rest), g.dtype)] if rest else []), axis=1)


def _pad_lane(v):
    v = v.reshape(1, -1)
    return jnp.pad(v, ((0, 0), (0, LANE - v.shape[1])))


SLOT1 = SB_PROJ // N_CHIPS
MM_TN = 512


def _local_step(x, mem, target, norm_g, mem_norm_g, xa_q_g, xa_k_g, w_in0, conv_w, a_log, dt_bias, out_g, sb_q_g, sb_k_g,
                late_weights, early_grads, grads_swapped):
    S = x.shape[0]
    nc = S // CH
    al = _pad_lane(a_log)
    dtb = _pad_lane(dt_bias)
    q_scale = HEAD_DIM ** -0.5
    tiles1 = SLOT1 // MM_TN

    kv_rhs = lambda l: pl.BlockSpec((N_CHIPS, None, D_MODEL // N_CHIPS, MM_TN), lambda i, j: (0, l, 0, j))
    kv_rhs_t = lambda l: pl.BlockSpec((None, None, D_MODEL // N_CHIPS, 2 * XA_WIDTH), lambda i, j: (j, l, 0, 0))
    out_rhs = lambda l: pl.BlockSpec((N_CHIPS, None, INNER // N_CHIPS, MM_TN), lambda i, j: (0, l, 0, j))
    out_rhs_t = lambda l: pl.BlockSpec((None, None, MM_TN, D_MODEL), lambda i, j: (j // 2, l, j % 2, 0))
    in1_rhs = pl.BlockSpec((None, 2, D_MODEL // 2, MM_TN), lambda i, j: (j // tiles1, 0, 0, j % tiles1))
    in1_rhs_t = pl.BlockSpec((None, None, D_MODEL // 2, MM_TN), lambda i, j, k: (k // tiles1, j, 0, k % tiles1))
    slot_rows = lambda rows: dict(
        tm=rows, o_spec=pl.BlockSpec((None, rows, MM_TN), lambda i, j: (i, 0, j)),
        o_shape=jax.ShapeDtypeStruct((N_CHIPS, rows, 2 * XA_WIDTH), BF16))
    in1_out = dict(tm=D_MODEL // 2, o_spec=pl.BlockSpec((None, None, D_MODEL // 2, MM_TN),
                                                        lambda i, j: (j // tiles1, i, 0, j % tiles1)),
                   o_shape=jax.ShapeDtypeStruct((N_CHIPS, 2, D_MODEL // 2, SLOT1), BF16))

    h0 = _rmsnorm_fwd(x, norm_g[0], name="norm0")
    proj0 = _matmul(h0, w_in0, name="proj0")
    qn = _dn_pre_fwd(proj0, conv_w, 0, DN_QK_WIDTH, l2=True, scale=q_scale, name="dn_pre_q")
    kn = _dn_pre_fwd(proj0, conv_w, DN_QK_WIDTH, DN_QK_WIDTH, l2=True, scale=1.0, name="dn_pre_k")
    vc = _dn_pre_fwd(proj0, conv_w, 2 * DN_QK_WIDTH, MIX_WIDTH, l2=False, scale=1.0, name="dn_pre_v")
    g_nat, b_nat = _dn_ab_fwd(proj0, al, dtb, name="dn_ab")
    g_rows = _rows_form(g_nat, nc)
    b_rows = _rows_form(b_nat, nc)
    mix0, states, tinv = _dn_core_fwd(qn, kn, vc, g_rows, b_rows, out_g, name="dn_core")
    w_kv, w_out, w_in1 = late_weights(mix0)
    mem_n = _rmsnorm_fwd(mem, mem_norm_g, name="mem_norm")
    kv = [_matmul(mem_n, w_kv, n=2 * XA_WIDTH, tn=MM_TN, b_spec=kv_rhs(l), name=f"kv{l}") for l in range(2)]
    xa0 = _xa_fwd(proj0, P0_XQ, kv[0], xa_q_g[0], xa_k_g[0], name="xa0")
    y0 = _gate_fwd(mix0, xa0, proj0, P0_Z, name="gate0")
    x1 = _matmul(y0, w_out, n=D_MODEL, tn=MM_TN, b_spec=out_rhs(0), res=x, name="out0")

    h1 = _rmsnorm_fwd(x1, norm_g[1], name="norm1")
    proj1 = _matmul(h1, w_in1, n=SB_PROJ, tn=MM_TN, b_spec=in1_rhs, name="proj1")
    mix1, tot1 = _sb_fwd(proj1, sb_q_g, sb_k_g, name="sb")
    xa1 = _xa_fwd(proj1, P1_XQ, kv[1], xa_q_g[1], xa_k_g[1], name="xa1")
    y1 = _gate_fwd(mix1, xa1, proj1, P1_Z, name="gate1")
    x2 = _matmul(y1, w_out, n=D_MODEL, tn=MM_TN, b_spec=out_rhs(1), res=x1, name="out1")

    dx2, dx2b, loss_vec = _loss_head(x2, target, name="loss")

    d_wout1 = _matmul(y1, dx2b, ta=True, name="d_wout1", **slot_rows(INNER // N_CHIPS))
    dy1 = _matmul(dx2b, w_out, tb=True, n=INNER, tn=MM_TN, b_spec=out_rhs_t(1), name="dy1")
    dcat1, dz1 = _gate_bwd(dy1, mix1, xa1, proj1, P1_Z, name="gate1_bwd")
    dqkv1, d_sbq, d_sbk = _sb_bwd(dcat1, tot1, proj1, sb_q_g, sb_k_g, name="sb_bwd")
    dxq1, dkv1, d_xaq1, d_xak1 = _xa_bwd(dcat1, proj1, P1_XQ, kv[1], xa_q_g[1], xa_k_g[1], name="xa1_bwd")
    dproj1 = dqkv1 + [dxq1, dz1]
    d_win1 = _matmul(h1, dproj1, ta=True, name="d_win1", **in1_out)
    d_wkv1 = _matmul(mem_n, dkv1, ta=True, name="d_wkv1", **slot_rows(D_MODEL // N_CHIPS))
    token = early_grads(1, d_win1, d_wout1, d_wkv1)
    dproj1 = dqkv1 + [dxq1 + token[0, 0].astype(BF16), dz1]
    dh1 = _matmul(dproj1, w_in1, tb=True, n=D_MODEL, tn=D_MODEL // 2, tk=MM_TN, b_spec=in1_rhs_t, name="dh1")
    token = grads_swapped(dh1)
    (dx1, dx1b), d_ng1 = _rmsnorm_bwd(dh1, x1, norm_g[1] + token[0, 0], dx2, name="norm1_bwd")

    d_wout0 = _matmul(y0, dx1b, ta=True, name="d_wout0", **slot_rows(INNER // N_CHIPS))
    dy0 = _matmul(dx1b, w_out, tb=True, n=INNER, tn=MM_TN, b_spec=out_rhs_t(0), name="dy0")
    dcat0, dz0 = _gate_bwd(dy0, mix0, xa0, proj0, P0_Z, name="gate0_bwd")
    dqv, dkv_h, dvc, dg_cols, db_cols, d_outg = _dn_core_bwd(
        dcat0, qn, kn, vc, g_rows, b_rows, out_g, states, tinv, name="dn_core_bwd")
    dpq, dwq = _dn_pre_bwd(dqv, proj0, conv_w, 0, DN_QK_WIDTH, l2=True, scale=q_scale, name="dn_pre_q_bwd")
    dpk, dwk = _dn_pre_bwd(dkv_h, proj0, conv_w, DN_QK_WIDTH, DN_QK_WIDTH, l2=True, scale=1.0, name="dn_pre_k_bwd")
    dpv, dwv = _dn_pre_bwd(dvc, proj0, conv_w, 2 * DN_QK_WIDTH, MIX_WIDTH, l2=False, scale=1.0, name="dn_pre_v_bwd")
    dab, d_alog, d_dt = _dn_ab_bwd(_cols_to_nat(dg_cols, nc), _cols_to_nat(db_cols, nc), proj0, al, dtb, name="dn_ab_bwd")
    dxq0, dkv0, d_xaq0, d_xak0 = _xa_bwd(dcat0, proj0, P0_XQ, kv[0], xa_q_g[0], xa_k_g[0], name="xa0_bwd")
    d_win0 = _matmul(h0, [dpq, dpk, dpv, dxq0, dz0, dab], ta=True, out_dtype=BF16, name="d_win0")
    d_wkv0 = _matmul(mem_n, dkv0, ta=True, name="d_wkv0", **slot_rows(D_MODEL // N_CHIPS))
    token = early_grads(0, d_win0, d_wout0, d_wkv0)
    zero = token[0, 0]
    dh0 = _matmul([dpq, dpk, dpv, dxq0, dz0, dab + zero.astype(BF16)], w_in0, tb=True, tk=MM_TN, name="dh0")
    (dx0, _), d_ng0 = _rmsnorm_bwd(dh0, x, norm_g[0] + zero, dx1, name="norm0_bwd")

    dmem0 = _matmul(dkv0, w_kv, tb=True, n=D_MODEL, tn=D_MODEL // N_CHIPS, b_spec=kv_rhs_t(0), name="dmem0")
    dmem_n = _matmul(dkv1, w_kv, tb=True, n=D_MODEL, tn=D_MODEL // N_CHIPS, b_spec=kv_rhs_t(1), res=dmem0, name="dmem1")
    _, d_memg = _rmsnorm_bwd(dmem_n, mem, mem_norm_g, None, name="mem_norm_bwd")

    grads = dict(
        norm_g=jnp.concatenate([d_ng0, d_ng1], axis=0), mem_norm_g=d_memg.reshape(-1),
        xa_q_norm_g=jnp.concatenate([d_xaq0, d_xaq1], axis=0), xa_k_norm_g=jnp.concatenate([d_xak0, d_xak1], axis=0),
        dn_conv_w=jnp.concatenate([dwq, dwk, dwv], axis=1),
        dn_a_log=d_alog[:, :DN_V_HEADS], dn_dt_bias=d_dt[:, :DN_V_HEADS], dn_out_norm_g=d_outg,
        sb_q_norm_g=d_sbq, sb_k_norm_g=d_sbk)
    return loss_vec, dx0, grads


ANY = pl.BlockSpec(memory_space=pl.ANY)


def _place():
    x, y, c = lax.axis_index("x"), lax.axis_index("y"), lax.axis_index("c")
    chips = [(1 - x, y), (x, 1 - y), (1 - x, 1 - y)]
    return x, y, c, 2 * x + y, (x, y, 1 - c), chips


def _rcopy(src, dst, send, recv, i, dev):
    return pltpu.make_async_remote_copy(src_ref=src, dst_ref=dst, send_sem=send.at[i], recv_sem=recv.at[i],
                                        device_id=dev, device_id_type=MESH)


def _swap_halves(xs, *, name):
    nt = len(xs)

    def body(*refs):
        src, dst = refs[:nt], refs[nt:2 * nt]
        send, recv = refs[2 * nt:]
        x, y, c, j, sib, chips = _place()
        cps = []
        for t in range(nt):
            for s in range(N_CHIPS):
                cps.append(_rcopy(src[t].at[s, 1 - c], dst[t].at[s], send, recv, 4 * t + s, sib))
                cps[-1].start()
        for cp in cps:
            cp.wait_recv()
        for cp in cps:
            cp.wait_send()

    return pl.pallas_call(
        body, in_specs=[ANY] * nt, out_specs=[ANY] * nt,
        out_shape=[jax.ShapeDtypeStruct((N_CHIPS,) + a.shape[2:], a.dtype) for a in xs],
        scratch_shapes=[pltpu.SemaphoreType.DMA((4 * nt,)), pltpu.SemaphoreType.DMA((4 * nt,))], name=name)(*xs)


def _swap_with_sibling(fs, *, name):
    nt = len(fs)

    def body(*refs):
        src, dst = refs[:nt], refs[nt:2 * nt]
        send, recv = refs[2 * nt:]
        x, y, c, j, sib, chips = _place()
        cps = [_rcopy(src[t], dst[t], send, recv, t, sib) for t in range(nt)]
        for cp in cps:
            cp.start()
        for cp in cps:
            cp.wait_recv()
        for cp in cps:
            cp.wait_send()

    return pl.pallas_call(
        body, in_specs=[ANY] * nt, out_specs=[ANY] * nt,
        out_shape=[jax.ShapeDtypeStruct(a.shape, a.dtype) for a in fs],
        scratch_shapes=[pltpu.SemaphoreType.DMA((nt,)), pltpu.SemaphoreType.DMA((nt,))], name=name)(*fs)


HBM_SPEC = pl.BlockSpec(memory_space=pltpu.HBM)
SEM_SPEC = pl.BlockSpec(memory_space=pltpu.SEMAPHORE)
SIDE_EFFECT = pltpu.SideEffectType.DATAFLOW_SIDE_EFFECTING


def _gather_plan(src, land):
    x, y, c, j, sib, chips = _place()
    return [(src[t].at[c], land[t].at[j, c], (cx, cy, c), land[t].at[2 * cx + cy, c])
            for t in range(len(src)) for cx, cy in chips]


def _scatter_plan(src, land):
    x, y, c, j, sib, chips = _place()
    return [(src[t].at[2 * cx + cy], land[t].at[k], (cx, cy, c), land[t].at[k])
            for t in range(len(src)) for k, (cx, cy) in enumerate(chips)]


def _swap_plan(src, land):
    x, y, c, j, sib, chips = _place()
    return [(src[t].at[s, 1 - c], land[t].at[s], sib, land[t].at[s]) for t in range(len(src)) for s in range(N_CHIPS)]


def _exchange_start(srcs, lands, plan, *, name, per_tensor=3):
    ns, nb = len(srcs), len(srcs) + len(lands)
    n = per_tensor * ns

    def body(*refs):
        send, recv, token = refs[nb], refs[nb + 1], refs[-1]
        for i, (s, d, dev, _) in enumerate(plan(refs[:ns], refs[ns:nb])):
            _rcopy(s, d, send, recv, i, dev).start()
        token[...] = jnp.zeros_like(token)

    bufs = list(srcs) + list(lands)
    outs = pl.pallas_call(
        body, name=name,
        out_shape=(pltpu.SemaphoreType.DMA((n,)), pltpu.SemaphoreType.DMA((n,)), *[pltpu.HBM(a.shape, a.dtype) for a in bufs],
                   jax.ShapeDtypeStruct((8, LANE), F32)),
        in_specs=[HBM_SPEC] * nb, out_specs=(SEM_SPEC, SEM_SPEC, *[HBM_SPEC] * nb, pl.BlockSpec(memory_space=pltpu.VMEM)),
        input_output_aliases={i: 2 + i for i in range(nb)},
        compiler_params=pltpu.CompilerParams(has_side_effects=SIDE_EFFECT))(
            *[pltpu.with_memory_space_constraint(a, pltpu.HBM) for a in bufs])
    return outs[0], outs[1], list(outs[2:2 + ns]), list(outs[2 + ns:2 + nb]), outs[-1]


def _exchange_wait(srcs, lands, send, recv, after, plan, *, name):
    ns, nb = len(srcs), len(srcs) + len(lands)
    afters = list(after) if isinstance(after, (list, tuple)) else [after]

    def body(*refs):
        send_s, recv_s = refs[nb], refs[nb + 1]
        for i, (s, d, dev, inc) in enumerate(plan(refs[:ns], refs[ns:nb])):
            _rcopy(s, d, send_s, recv_s, i, dev).wait_send()
            _rcopy(inc, inc, send_s, recv_s, i, dev).wait_recv()

    bufs = list(srcs) + list(lands)
    outs = pl.pallas_call(
        body, name=name, out_shape=tuple(pltpu.HBM(a.shape, a.dtype) for a in bufs),
        in_specs=[HBM_SPEC] * nb + [SEM_SPEC, SEM_SPEC] + [ANY] * len(afters), out_specs=tuple([HBM_SPEC] * nb),
        input_output_aliases={i: i for i in range(nb)},
        compiler_params=pltpu.CompilerParams(has_side_effects=SIDE_EFFECT))(*bufs, send, recv, *afters)
    return list(outs[:ns]), list(outs[ns:])


def _forward_halves(lands, *, name):
    nt = len(lands)

    def body(*refs):
        src, dst = refs[:nt], refs[nt:2 * nt]
        send, recv = refs[2 * nt:]
        x, y, c, j, sib, chips = _place()
        cps = []
        for t in range(nt):
            for k, (cx, cy) in enumerate(chips):
                cps.append(_rcopy(src[t].at[2 * cx + cy, c], dst[t].at[2 * cx + cy, c], send, recv, 3 * t + k, sib))
                cps[-1].start()
        for t in range(nt):
            for k, (cx, cy) in enumerate(chips):
                other = dst[t].at[2 * cx + cy, 1 - c]
                _rcopy(other, other, send, recv, 3 * t + k, sib).wait_recv()
        for cp in cps:
            cp.wait_send()

    return pl.pallas_call(
        body, in_specs=[ANY] * nt, out_specs=[ANY] * nt, out_shape=[jax.ShapeDtypeStruct(a.shape, a.dtype) for a in lands],
        input_output_aliases={t: t for t in range(nt)},
        scratch_shapes=[pltpu.SemaphoreType.DMA((3 * nt,)), pltpu.SemaphoreType.DMA((3 * nt,))], name=name)(*lands)


def _all_reduce_small(parts, *, name):
    n = len(parts)
    offs, rows = [], 0
    for p in parts:
        offs.append(rows)
        rows += -(-p.shape[0] // 8) * 8

    def body(*refs):
        p_refs, o_refs = refs[:n], refs[n:2 * n]
        buf, send, recv = refs[2 * n:]
        x, y, c = lax.axis_index("x"), lax.axis_index("y"), lax.axis_index("c")
        me = 4 * x + 2 * y + c
        buf[me] = jnp.zeros((rows, LANE), F32)
        for p_ref, off in zip(p_refs, offs):
            buf[me, off:off + p_ref.shape[0], :] = p_ref[...]
        cps = []
        for r in range(1, 8):
            dev = (x ^ (r >> 2), y ^ ((r >> 1) & 1), c ^ (r & 1))
            cps.append(_rcopy(buf.at[me], buf.at[me], send, recv, r - 1, dev))
            cps[-1].start()
        for r in range(1, 8):
            frm = buf.at[me ^ r]
            _rcopy(frm, frm, send, recv, r - 1, (x, y, c)).wait_recv()
        for cp in cps:
            cp.wait_send()
        acc = buf[0]
        for d in range(1, 8):
            acc = acc + buf[d]
        for o_ref, off in zip(o_refs, offs):
            o_ref[...] = acc[off:off + o_ref.shape[0], :]

    vm = pl.BlockSpec(memory_space=pltpu.VMEM)
    return pl.pallas_call(
        body, in_specs=[vm] * n, out_specs=[vm] * n, out_shape=[jax.ShapeDtypeStruct(p.shape, F32) for p in parts],
        scratch_shapes=[pltpu.VMEM((8, rows, LANE), F32), pltpu.SemaphoreType.DMA((7,)), pltpu.SemaphoreType.DMA((7,))],
        name=name)(*parts)


def _add_halves(x, b, c_idx, *, name, tr=256):
    _, _, R, C = x.shape
    tr = min(tr, R)

    def body(c_ref, x_ref, b_ref, o_ref):
        o_ref[...] = (x_ref[...].astype(F32) + b_ref[...].astype(F32)).astype(o_ref.dtype)

    return pl.pallas_call(
        body,
        grid_spec=pltpu.PrefetchScalarGridSpec(
            num_scalar_prefetch=1, grid=(N_CHIPS, R // tr),
            in_specs=[pl.BlockSpec((None, None, tr, C), lambda s, i, c_ref: (s, c_ref[0], i, 0)),
                      pl.BlockSpec((None, tr, C), lambda s, i, c_ref: (s, i, 0))],
            out_specs=pl.BlockSpec((None, tr, C), lambda s, i, c_ref: (s, i, 0))),
        out_shape=jax.ShapeDtypeStruct(b.shape, b.dtype), compiler_params=_cp(("parallel", "parallel")), name=name)(c_idx, x, b)


def _sum_slot(p, rcv, j_idx, *, name, tr=256):
    _, R, C = p.shape
    tr = min(tr, R)

    def body(j_ref, p_ref, r_ref, o_ref):
        acc = p_ref[...].astype(F32)
        for k in range(3):
            acc = acc + r_ref[k].astype(F32)
        o_ref[...] = acc

    return pl.pallas_call(
        body,
        grid_spec=pltpu.PrefetchScalarGridSpec(
            num_scalar_prefetch=1, grid=(R // tr,),
            in_specs=[pl.BlockSpec((None, tr, C), lambda i, j_ref: (j_ref[0], i, 0)),
                      pl.BlockSpec((3, tr, C), lambda i, j_ref: (0, i, 0))],
            out_specs=pl.BlockSpec((tr, C), lambda i, j_ref: (i, 0))),
        out_shape=jax.ShapeDtypeStruct((R, C), F32), compiler_params=_cp(("parallel",)), name=name)(j_idx, p, rcv)


def _adamw_math(w, g, m, v):
    nm = ADAM_B1 * m + (1.0 - ADAM_B1) * g
    nv = ADAM_B2 * v + (1.0 - ADAM_B2) * (g * g)
    m_hat = nm / (1.0 - ADAM_B1 ** ADAM_STEP)
    v_hat = nv / (1.0 - ADAM_B2 ** ADAM_STEP)
    return -ADAM_LR * (m_hat / (jnp.sqrt(v_hat) + ADAM_EPS) + ADAM_WD * w), nm, nv


def _adamw_halves(w, g_mine, g_theirs, m, v, c_idx, *, name, layer=0, into=None, tr=128):
    _, _, R, C = w.shape
    tr = tr if R % tr == 0 else R

    def body(c_ref, w_ref, gm_ref, gt_ref, m_ref, v_ref, *rest):
        g_ref, d_ref, nm_ref, nv_ref = rest[-4:]
        gv = jnp.where(pl.program_id(0) == c_ref[0], gm_ref[...], gt_ref[...])
        d, nm, nv = _adamw_math(w_ref[...], gv, m_ref[...], v_ref[...])
        g_ref[...] = gv
        d_ref[...] = d
        nm_ref[...] = nm
        nv_ref[...] = nv

    full = pl.BlockSpec((None, None, tr, C), lambda hh, i, c_ref: (layer, hh, i, 0))
    half = pl.BlockSpec((tr, C), lambda hh, i, c_ref: (i, 0))
    sh = jax.ShapeDtypeStruct(w.shape, F32)
    extra = [] if into is None else list(into)
    return pl.pallas_call(
        body,
        grid_spec=pltpu.PrefetchScalarGridSpec(num_scalar_prefetch=1, grid=(2, R // tr),
                                               in_specs=[full, half, half, full, full] + [ANY] * len(extra),
                                               out_specs=(full,) * 4),
        out_shape=(sh,) * 4, input_output_aliases={6 + t: t for t in range(len(extra))},
        compiler_params=_cp(("parallel", "parallel")), name=name)(c_idx, w, g_mine, g_theirs, m, v, *extra)


def _adamw_parts(ws, gs, ms, vs, *, name):
    n = len(ws)

    def body(*refs):
        ins, outs = refs[:4 * n], refs[4 * n:]
        for t in range(n):
            d, nm, nv = _adamw_math(ins[t][...], ins[n + t][...], ins[2 * n + t][...], ins[3 * n + t][...])
            outs[t][...] = d
            outs[n + t][...] = nm
            outs[2 * n + t][...] = nv

    vm = pl.BlockSpec(memory_space=pltpu.VMEM)
    shapes = [jax.ShapeDtypeStruct(w.shape, F32) for w in ws] * 3
    outs = pl.pallas_call(body, in_specs=[vm] * (4 * n), out_specs=[vm] * (3 * n), out_shape=shapes, name=name)(
        *ws, *gs, *ms, *vs)
    return outs[:n], outs[n:2 * n], outs[2 * n:]


_SMALL = ["norm_g", "mem_norm_g", "xa_q_norm_g", "xa_k_norm_g", "dn_a_log", "dn_dt_bias", "dn_out_norm_g",
          "sb_q_norm_g", "sb_k_norm_g"]


def _rows128(a):
    flat = a.reshape(-1)
    pad = -flat.shape[0] % LANE
    if pad:
        flat = jnp.pad(flat, (0, pad))
    return flat.reshape(-1, LANE)


def _unrows(r, shape):
    return r.reshape(-1)[:math.prod(shape)].reshape(shape)


def kernel(x, mem, norm_g, mem_norm_g, mem_w_kv, xa_q_norm_g, xa_k_norm_g, w_out, dn_w_in, dn_conv_w, dn_a_log, dn_dt_bias, dn_out_norm_g, sb_w_in, sb_q_norm_g, sb_k_norm_g, loss_target, m_norm_g, m_mem_norm_g, m_mem_w_kv, m_xa_q_norm_g, m_xa_k_norm_g, m_w_out, m_dn_w_in, m_dn_conv_w, m_dn_a_log, m_dn_dt_bias, m_dn_out_norm_g, m_sb_w_in, m_sb_q_norm_g, m_sb_k_norm_g, v_norm_g, v_mem_norm_g, v_mem_w_kv, v_xa_q_norm_g, v_xa_k_norm_g, v_w_out, v_dn_w_in, v_dn_conv_w, v_dn_a_log, v_dn_dt_bias, v_dn_out_norm_g, v_sb_w_in, v_sb_q_norm_g, v_sb_k_norm_g):
    W = dict(norm_g=norm_g, mem_norm_g=mem_norm_g, mem_w_kv=mem_w_kv, xa_q_norm_g=xa_q_norm_g, xa_k_norm_g=xa_k_norm_g,
             w_out=w_out, dn_w_in=dn_w_in, dn_conv_w=dn_conv_w, dn_a_log=dn_a_log, dn_dt_bias=dn_dt_bias,
             dn_out_norm_g=dn_out_norm_g, sb_w_in=sb_w_in, sb_q_norm_g=sb_q_norm_g, sb_k_norm_g=sb_k_norm_g)
    M = dict(norm_g=m_norm_g, mem_norm_g=m_mem_norm_g, mem_w_kv=m_mem_w_kv, xa_q_norm_g=m_xa_q_norm_g,
             xa_k_norm_g=m_xa_k_norm_g, w_out=m_w_out, dn_w_in=m_dn_w_in, dn_conv_w=m_dn_conv_w, dn_a_log=m_dn_a_log,
             dn_dt_bias=m_dn_dt_bias, dn_out_norm_g=m_dn_out_norm_g, sb_w_in=m_sb_w_in, sb_q_norm_g=m_sb_q_norm_g,
             sb_k_norm_g=m_sb_k_norm_g)
    V = dict(norm_g=v_norm_g, mem_norm_g=v_mem_norm_g, mem_w_kv=v_mem_w_kv, xa_q_norm_g=v_xa_q_norm_g,
             xa_k_norm_g=v_xa_k_norm_g, w_out=v_w_out, dn_w_in=v_dn_w_in, dn_conv_w=v_dn_conv_w, dn_a_log=v_dn_a_log,
             dn_dt_bias=v_dn_dt_bias, dn_out_norm_g=v_dn_out_norm_g, sb_w_in=v_sb_w_in, sb_q_norm_g=v_sb_q_norm_g,
             sb_k_norm_g=v_sb_k_norm_g)
    names = ["norm_g", "mem_norm_g", "mem_w_kv", "xa_q_norm_g", "xa_k_norm_g", "w_out", "dn_w_in", "dn_conv_w",
             "dn_a_log", "dn_dt_bias", "dn_out_norm_g", "sb_w_in", "sb_q_norm_g", "sb_k_norm_g"]
    cx, cy, cc = lax.axis_index("x"), lax.axis_index("y"), lax.axis_index("c")
    slot = 2 * cx + cy
    half_r = D_MODEL // 2
    conv_cols = dn_conv_w.shape[2]

    by_slot = lambda fn, a: lax.switch(slot, [lambda v, s=s: fn(s, v) for s in range(N_CHIPS)], a)
    w0s = by_slot(_pack_own_shard, dn_w_in[0].astype(BF16)).reshape(2, half_r, WIN_COLS)
    convs = jnp.pad(dn_conv_w[0], ((0, 8 - DN_CONV), (0, 0))).reshape(8, 2, conv_cols // 2).transpose(1, 0, 2)
    c_idx = jnp.reshape(cc, (1,)).astype(jnp.int32)
    j_idx = jnp.reshape(slot, (1,)).astype(jnp.int32)
    own_a = [w0s, convs]
    lands_a = [lax.dynamic_update_slice(lax.empty((N_CHIPS,) + o.shape, o.dtype), o[None], (slot, 0, 0, 0)) for o in own_a]
    send_a, recv_a, own_a, lands_a, token_a = _exchange_start(own_a, lands_a, _gather_plan, name="gather_start")
    zero_a = token_a[0, 0]
    M["dn_w_in"] = m_dn_w_in + zero_a
    V["dn_w_in"] = v_dn_w_in + zero_a
    own_b = [(sb_w_in[0] + zero_a).astype(BF16).reshape(2, half_r, SB_PROJ // N_CHIPS), (w_out + zero_a).astype(BF16),
             (mem_w_kv + zero_a).astype(BF16)]
    lands_b = [lax.dynamic_update_slice(lax.empty((N_CHIPS,) + o.shape, o.dtype), o[None], (slot, 0, 0, 0)) for o in own_b]
    view0 = (1, 2, half_r, P0_SHARD)
    _, lands_a = _exchange_wait(own_a, lands_a, send_a, recv_a,
                                [M["dn_w_in"].reshape(view0), V["dn_w_in"].reshape(view0)] + own_b + lands_b,
                                _gather_plan, name="gather_wait")
    (g0, gconv), own_b, lands_b = lax.optimization_barrier((_forward_halves(lands_a, name="gather_forward"), own_b, lands_b))
    late = {}
    late[1] = _exchange_start(own_b[1:], lands_b[1:], _gather_plan, name="gather_late1_start")
    (w1s, land1), _ = lax.optimization_barrier(((own_b[0], lands_b[0]), late[1][4]))
    late[2] = _exchange_start([w1s], [land1], _gather_plan, name="gather_late2_start")
    token_b = late[2][4]

    def late_weights(after):
        got = []
        for i in (1, 2):
            send, recv, srcs, lands, _ = late[i]
            _, lands = _exchange_wait(srcs, lands, send, recv, after, _gather_plan, name=f"gather_late{i}_wait")
            got += _forward_halves(lands, name=f"gather_late{i}_forward")
            after = got[-1]
        gout, gkv, g1 = got
        return gkv, gout, g1

    rs = {}

    def scatter_start(tag, xs, from_sib=None):
        if from_sib is None:
            from_sib = _swap_halves(xs, name=f"rs{tag}_swap")
        ps = [_add_halves(a, b, c_idx, name=f"rs{tag}_add{t}") for t, (a, b) in enumerate(zip(xs, from_sib))]
        rcv = [lax.empty((3,) + p.shape[1:], p.dtype) for p in ps]
        send, recv, ps, rcv, token = _exchange_start(ps, rcv, _scatter_plan, name=f"rs{tag}_scatter_start")
        rs[tag] = (ps, rcv, send, recv)
        return token

    def scatter_finish(tag, after):
        ps, rcv, send, recv = rs[tag]
        ps, rcv = _exchange_wait(ps, rcv, send, recv, after, _scatter_plan, name=f"rs{tag}_scatter_wait")
        return [_sum_slot(p, r, j_idx, name=f"rs{tag}_sum{t}") for t, (p, r) in enumerate(zip(ps, rcv))]

    def early_grads(layer, d_win, d_wout, d_wkv):
        if layer == 0:
            d_win = jnp.stack([_padded_to_window(s, d_win) for s in range(N_CHIPS)]).reshape(N_CHIPS, 2, half_r, WIN_COLS)
        xs = [d_win, d_wout.reshape(N_CHIPS, 2, -1, D_MODEL), d_wkv.reshape(N_CHIPS, 2, -1, 2 * XA_WIDTH)]
        if layer == 0:
            return scatter_start(0, xs)
        lands = [lax.empty((N_CHIPS,) + a.shape[2:], a.dtype) for a in xs]
        send, recv, xs, lands, token = _exchange_start(xs, lands, _swap_plan, per_tensor=N_CHIPS, name="rs1_swap_start")
        rs["swap1"] = (xs, lands, send, recv)
        return token

    def grads_swapped(after):
        xs, lands, send, recv = rs["swap1"]
        xs, from_sib = _exchange_wait(xs, lands, send, recv, after, _swap_plan, name="rs1_swap_wait")
        return scatter_start(1, xs, from_sib)

    w_in0 = _windows_to_padded(g0.reshape(N_CHIPS, D_MODEL, WIN_COLS))
    conv_f = gconv.transpose(2, 0, 1, 3).reshape(8, N_CHIPS * conv_cols)[:DN_CONV]

    loss_vec, grad_x, g = _local_step(
        x[0], mem[0], loss_target[0], norm_g + token_b[0, 0], mem_norm_g, xa_q_norm_g, xa_k_norm_g, w_in0, conv_f,
        dn_a_log[0], dn_dt_bias[0], dn_out_norm_g[0], sb_q_norm_g[0], sb_k_norm_g[0], late_weights, early_grads,
        grads_swapped)

    mine1 = scatter_finish(1, grad_x)
    theirs1 = _swap_with_sibling(mine1, name="rs1_join")
    big1 = [("sb_w_in", None), ("w_out", 1), ("mem_w_kv", 1)]
    big0 = [("dn_w_in", None), ("w_out", 0), ("mem_w_kv", 0)]

    out_g, out_d, out_m, out_v = {}, {}, {}, {}
    partial = {}

    def adamw_big(big, mine, theirs):
        for (n, layer), mine_g, their_g in zip(big, mine, theirs):
            layers = 1 if layer is None else 2
            view = (layers, 2) + mine_g.shape
            partial[n] = _adamw_halves(W[n].reshape(view), mine_g, their_g, M[n].reshape(view), V[n].reshape(view), c_idx,
                                       layer=layer or 0, into=partial.get(n), name=f"adamw_{n}" + ("" if layer is None else str(layer)))
        return [partial[n][0] for n, _ in big]

    done1 = lax.optimization_barrier(tuple(adamw_big(big1, mine1, theirs1)))[-1]
    mine0 = scatter_finish(0, done1)
    mine0[0] = by_slot(_unpack_own_shard, mine0[0])

    parts, _ = lax.optimization_barrier(([_rows128(g[n]) for n in _SMALL] + [_rows128(g["dn_conv_w"]), loss_vec], mine0[0]))
    red = _all_reduce_small(parts, name="all_reduce_small")
    small_rows = dict(zip(_SMALL, red))
    conv_full = red[len(_SMALL)].reshape(DN_CONV, N_CHIPS * conv_cols)
    small_rows["dn_conv_w"] = _rows128(lax.dynamic_slice_in_dim(conv_full, slot * conv_cols, conv_cols, axis=1))
    loss = red[-1][0, 0]

    adamw_big(big0, mine0, _swap_with_sibling(mine0, name="rs0_join"))
    for n, outs in partial.items():
        out_g[n], out_d[n], out_m[n], out_v[n] = [o.reshape(W[n].shape) for o in outs]
    small_names = _SMALL + ["dn_conv_w"]
    ds, nms, nvs = _adamw_parts([_rows128(W[n]) for n in small_names], [small_rows[n] for n in small_names],
                                [_rows128(M[n]) for n in small_names], [_rows128(V[n]) for n in small_names], name="adamw_small")
    for n, d, nm, nv in zip(small_names, ds, nms, nvs):
        shp = W[n].shape
        out_g[n], out_d[n], out_m[n], out_v[n] = [_unrows(r, shp) for r in (small_rows[n], d, nm, nv)]

    return (loss, grad_x[None], *[out_g[n] for n in names], *[out_d[n] for n in names], *[out_m[n] for n in names],
            *[out_v[n] for n in names])
```
